```python
import jax, jax.numpy as jnp
from jax import lax
import numpy as np

D_MODEL = 1024
BATCH = 8
SEQ = 16384
DEPTH = 2

HEAD_DIM = 64
A_CH = D_MODEL // 2
A_CONV = 31
N_Q_HEADS = (D_MODEL // 2) // HEAD_DIM
N_KV_HEADS = 2
GROUP = N_Q_HEADS // N_KV_HEADS
WINDOW = 128
BLOCK = 128
ROPE_THETA = 500000.0
ROPE_DIM = HEAD_DIM // 4
Q_DIM = N_Q_HEADS * HEAD_DIM
KV_DIM = N_KV_HEADS * HEAD_DIM
EVEN_IN = 2 * A_CH + Q_DIM + 2 * KV_DIM
MIX_DIM = A_CH + Q_DIM
SC_DIM = D_MODEL
SC_CONV = 3
D_FF = 2816
FFN_CONV = 3
N_EVEN = (DEPTH + 1) // 2
N_ODD = DEPTH // 2
RMS_EPS = 1e-6
LN_EPS = 1e-5

kernel_name = "hybrid_conformer_swa_shortconv_trunk"


def rms_norm(x, g):
    xf = x.astype(jnp.float32)
    y = xf * lax.rsqrt(jnp.mean(xf * xf, axis=-1, keepdims=True) + RMS_EPS)
    return (y * g.astype(jnp.float32)).astype(x.dtype)


def layer_norm(x, g, b):
    xf = x.astype(jnp.float32)
    mu = jnp.mean(xf, axis=-1, keepdims=True)
    xc = xf - mu
    y = xc * lax.rsqrt(jnp.mean(xc * xc, axis=-1, keepdims=True) + LN_EPS)
    return (y * g.astype(jnp.float32) + b.astype(jnp.float32)).astype(x.dtype)


def causal_dwconv(x, w):
    k, c = w.shape
    return lax.conv_general_dilated(
        x, w[:, None, :].astype(x.dtype), window_strides=(1,), padding=[(k - 1, 0)],
        dimension_numbers=('NWC', 'WIO', 'NWC'), feature_group_count=c)


def partial_rope(x, positions):
    half = ROPE_DIM // 2
    inv_freq = ROPE_THETA ** (-(jnp.arange(half, dtype=jnp.float32) * 2.0 / ROPE_DIM))
    ang = positions.astype(jnp.float32)[..., None] * inv_freq
    cos = jnp.cos(ang)[:, :, None, :]
    sin = jnp.sin(ang)[:, :, None, :]
    xf = x.astype(jnp.float32)
    x1, x2, rest = xf[..., :half], xf[..., half:ROPE_DIM], xf[..., ROPE_DIM:]
    out = jnp.concatenate([x1 * cos - x2 * sin, x2 * cos + x1 * sin, rest], axis=-1)
    return out.astype(x.dtype)


def sliding_window_attention(q, k, v, sinks):
    bsz, s_len = q.shape[0], q.shape[1]
    nb = s_len // BLOCK
    qb = q.reshape(bsz, nb, BLOCK, N_KV_HEADS, GROUP, HEAD_DIM)
    pad = ((0, 0), (BLOCK, 0), (0, 0), (0, 0))
    kb = jnp.pad(k, pad).reshape(bsz, nb + 1, BLOCK, N_KV_HEADS, HEAD_DIM)
    vb = jnp.pad(v, pad).reshape(bsz, nb + 1, BLOCK, N_KV_HEADS, HEAD_DIM)
    kw = jnp.concatenate([kb[:, :-1], kb[:, 1:]], axis=2)
    vw = jnp.concatenate([vb[:, :-1], vb[:, 1:]], axis=2)
    s = jnp.einsum('bnqhgd,bnkhd->bnhgqk', qb, kw,
                   preferred_element_type=jnp.float32) * (HEAD_DIM ** -0.5)
    q_idx = jnp.arange(BLOCK)[:, None]
    k_idx = jnp.arange(2 * BLOCK)[None, :]
    diff = q_idx + BLOCK - k_idx
    band = (diff >= 0) & (diff < WINDOW)
    key_valid = (jnp.arange(nb)[:, None] * BLOCK - BLOCK + k_idx) >= 0
    mask = band[None, :, :] & key_valid[:, None, :]
    s = jnp.where(mask[None, :, None, None], s, -jnp.inf)
    sink = sinks.astype(jnp.float32).reshape(N_KV_HEADS, GROUP)[None, None, :, :, None, None]
    m = jnp.maximum(jnp.max(s, axis=-1, keepdims=True), sink)
    p = jnp.exp(s - m)
    denom = jnp.sum(p, axis=-1, keepdims=True) + jnp.exp(sink - m)
    o = jnp.einsum('bnhgqk,bnkhd->bnqhgd', (p / denom).astype(v.dtype), vw)
    return o.reshape(bsz, s_len, N_Q_HEADS * HEAD_DIM)


def conv_attn_mixer(h, positions, w_in, a_conv_w, a_conv_b, a_ln_g, a_ln_b, sinks, w_out):
    bsz, s_len = h.shape[0], h.shape[1]
    z = h @ w_in
    a_lin, a_gate, q, k, v = jnp.split(
        z, [A_CH, 2 * A_CH, 2 * A_CH + Q_DIM, 2 * A_CH + Q_DIM + KV_DIM], axis=-1)
    a = a_lin * jax.nn.sigmoid(a_gate)
    a = causal_dwconv(a, a_conv_w) + a_conv_b
    a = jax.nn.silu(layer_norm(a, a_ln_g, a_ln_b))
    q = partial_rope(q.reshape(bsz, s_len, N_Q_HEADS, HEAD_DIM), positions)
    k = partial_rope(k.reshape(bsz, s_len, N_KV_HEADS, HEAD_DIM), positions)
    v = v.reshape(bsz, s_len, N_KV_HEADS, HEAD_DIM)
    o = sliding_window_attention(q, k, v, sinks)
    return jnp.concatenate([a, o], axis=-1) @ w_out


def short_conv_mixer(h, w_in, conv_w, w_out):
    z = h @ w_in
    b_gate, c_gate, u = jnp.split(z, 3, axis=-1)
    y = b_gate * causal_dwconv(c_gate * u, conv_w)
    return y @ w_out


def conv_glu_ffn(h, w_up, conv_w, w_down):
    u = causal_dwconv(h @ w_up, conv_w)
    g, val = jnp.split(u, 2, axis=-1)
    return (jax.nn.silu(g) * val) @ w_down


def _fwd_setup_inputs(seed: int = 0) -> dict:
    key = jax.random.key(seed)
    ks = jax.random.split(key, 20)

    def nrm(k, shape, scale):
        return jax.random.normal(k, shape, jnp.float32) * scale

    def gain(k, shape):
        return 1.0 + 0.05 * jax.random.normal(k, shape, jnp.float32)

    x = nrm(ks[0], (BATCH, SEQ, D_MODEL), 1.0)
    offsets = jax.random.randint(ks[1], (BATCH, 1), 0, 4096, dtype=jnp.int32)
    positions = offsets + jnp.arange(SEQ, dtype=jnp.int32)[None, :]
    return {
        'x': x,
        'positions': positions,
        'mix_norm_pre': gain(ks[2], (DEPTH, D_MODEL)),
        'mix_norm_post': gain(ks[3], (DEPTH, D_MODEL)),
        'ffn_norm_pre': gain(ks[4], (DEPTH, D_MODEL)),
        'ffn_norm_post': gain(ks[5], (DEPTH, D_MODEL)),
        'ev_w_in': nrm(ks[6], (N_EVEN, D_MODEL, EVEN_IN), D_MODEL ** -0.5),
        'ev_a_conv_w': nrm(ks[7], (N_EVEN, A_CONV, A_CH), A_CONV ** -0.5),
        'ev_a_conv_b': nrm(ks[8], (N_EVEN, A_CH), 0.02),
        'ev_a_ln_g': gain(ks[9], (N_EVEN, A_CH)),
        'ev_a_ln_b': nrm(ks[10], (N_EVEN, A_CH), 0.02),
        'ev_sinks': nrm(ks[11], (N_EVEN, N_Q_HEADS), 1.0),
        'ev_w_out': nrm(ks[12], (N_EVEN, MIX_DIM, D_MODEL), MIX_DIM ** -0.5),
        'od_w_in': nrm(ks[13], (N_ODD, D_MODEL, 3 * SC_DIM), D_MODEL ** -0.5),
        'od_conv_w': nrm(ks[14], (N_ODD, SC_CONV, SC_DIM), SC_CONV ** -0.5),
        'od_w_out': nrm(ks[15], (N_ODD, SC_DIM, D_MODEL), SC_DIM ** -0.5),
        'ffn_w_up': nrm(ks[16], (DEPTH, D_MODEL, 2 * D_FF), D_MODEL ** -0.5),
        'ffn_conv_w': nrm(ks[17], (DEPTH, FFN_CONV, 2 * D_FF), FFN_CONV ** -0.5),
        'ffn_w_down': nrm(ks[18], (DEPTH, D_FF, D_MODEL), D_FF ** -0.5),
    }


def _fwd_reference(x, positions, mix_norm_pre, mix_norm_post, ffn_norm_pre, ffn_norm_post,
              ev_w_in, ev_a_conv_w, ev_a_conv_b, ev_a_ln_g, ev_a_ln_b, ev_sinks, ev_w_out,
              od_w_in, od_conv_w, od_w_out, ffn_w_up, ffn_conv_w, ffn_w_down):
    for i in range(DEPTH):
        j = i // 2
        h = rms_norm(x, mix_norm_pre[i])
        if i % 2 == 0:
            m = conv_attn_mixer(h, positions, ev_w_in[j], ev_a_conv_w[j], ev_a_conv_b[j],
                                ev_a_ln_g[j], ev_a_ln_b[j], ev_sinks[j], ev_w_out[j])
        else:
            m = short_conv_mixer(h, od_w_in[j], od_conv_w[j], od_w_out[j])
        x = x + rms_norm(m, mix_norm_post[i])
        h = rms_norm(x, ffn_norm_pre[i])
        f = conv_glu_ffn(h, ffn_w_up[i], ffn_conv_w[i], ffn_w_down[i])
        x = x + rms_norm(f, ffn_norm_post[i])
    return x


import jax as _jax
import jax.numpy as _jnp

TWIN_FORMAT = 'train_step'
FWD_PARAMS = ['x', 'positions', 'mix_norm_pre', 'mix_norm_post', 'ffn_norm_pre', 'ffn_norm_post', 'ev_w_in', 'ev_a_conv_w', 'ev_a_conv_b', 'ev_a_ln_g', 'ev_a_ln_b', 'ev_sinks', 'ev_w_out', 'od_w_in', 'od_conv_w', 'od_w_out', 'ffn_w_up', 'ffn_conv_w', 'ffn_w_down']
TWIN_WEIGHTS = ['mix_norm_pre', 'mix_norm_post', 'ffn_norm_pre', 'ffn_norm_post', 'ev_w_in', 'ev_a_conv_w', 'ev_a_conv_b', 'ev_a_ln_g', 'ev_a_ln_b', 'ev_sinks', 'ev_w_out', 'od_w_in', 'od_conv_w', 'od_w_out', 'ffn_w_up', 'ffn_conv_w', 'ffn_w_down']
TWIN_DIFF_INPUT = 'x'
TWIN_INPUTS = ['x', 'positions', 'mix_norm_pre', 'mix_norm_post', 'ffn_norm_pre', 'ffn_norm_post', 'ev_w_in', 'ev_a_conv_w', 'ev_a_conv_b', 'ev_a_ln_g', 'ev_a_ln_b', 'ev_sinks', 'ev_w_out', 'od_w_in', 'od_conv_w', 'od_w_out', 'ffn_w_up', 'ffn_conv_w', 'ffn_w_down', 'loss_target', 'm_mix_norm_pre', 'm_mix_norm_post', 'm_ffn_norm_pre', 'm_ffn_norm_post', 'm_ev_w_in', 'm_ev_a_conv_w', 'm_ev_a_conv_b', 'm_ev_a_ln_g', 'm_ev_a_ln_b', 'm_ev_sinks', 'm_ev_w_out', 'm_od_w_in', 'm_od_conv_w', 'm_od_w_out', 'm_ffn_w_up', 'm_ffn_conv_w', 'm_ffn_w_down', 'v_mix_norm_pre', 'v_mix_norm_post', 'v_ffn_norm_pre', 'v_ffn_norm_post', 'v_ev_w_in', 'v_ev_a_conv_w', 'v_ev_a_conv_b', 'v_ev_a_ln_g', 'v_ev_a_ln_b', 'v_ev_sinks', 'v_ev_w_out', 'v_od_w_in', 'v_od_conv_w', 'v_od_w_out', 'v_ffn_w_up', 'v_ffn_conv_w', 'v_ffn_w_down']
TWIN_OUTPUTS = ['loss', 'grad_x', 'grad_mix_norm_pre', 'grad_mix_norm_post', 'grad_ffn_norm_pre', 'grad_ffn_norm_post', 'grad_ev_w_in', 'grad_ev_a_conv_w', 'grad_ev_a_conv_b', 'grad_ev_a_ln_g', 'grad_ev_a_ln_b', 'grad_ev_sinks', 'grad_ev_w_out', 'grad_od_w_in', 'grad_od_conv_w', 'grad_od_w_out', 'grad_ffn_w_up', 'grad_ffn_conv_w', 'grad_ffn_w_down', 'delta_mix_norm_pre', 'delta_mix_norm_post', 'delta_ffn_norm_pre', 'delta_ffn_norm_post', 'delta_ev_w_in', 'delta_ev_a_conv_w', 'delta_ev_a_conv_b', 'delta_ev_a_ln_g', 'delta_ev_a_ln_b', 'delta_ev_sinks', 'delta_ev_w_out', 'delta_od_w_in', 'delta_od_conv_w', 'delta_od_w_out', 'delta_ffn_w_up', 'delta_ffn_conv_w', 'delta_ffn_w_down', 'new_m_mix_norm_pre', 'new_m_mix_norm_post', 'new_m_ffn_norm_pre', 'new_m_ffn_norm_post', 'new_m_ev_w_in', 'new_m_ev_a_conv_w', 'new_m_ev_a_conv_b', 'new_m_ev_a_ln_g', 'new_m_ev_a_ln_b', 'new_m_ev_sinks', 'new_m_ev_w_out', 'new_m_od_w_in', 'new_m_od_conv_w', 'new_m_od_w_out', 'new_m_ffn_w_up', 'new_m_ffn_conv_w', 'new_m_ffn_w_down', 'new_v_mix_norm_pre', 'new_v_mix_norm_post', 'new_v_ffn_norm_pre', 'new_v_ffn_norm_post', 'new_v_ev_w_in', 'new_v_ev_a_conv_w', 'new_v_ev_a_conv_b', 'new_v_ev_a_ln_g', 'new_v_ev_a_ln_b', 'new_v_ev_sinks', 'new_v_ev_w_out', 'new_v_od_w_in', 'new_v_od_conv_w', 'new_v_od_w_out', 'new_v_ffn_w_up', 'new_v_ffn_conv_w', 'new_v_ffn_w_down']
TWIN_LEAF_KINDS = {'loss': 'loss', 'grad_x': 'grad_x', 'grad_mix_norm_pre': 'grad_w', 'grad_mix_norm_post': 'grad_w', 'grad_ffn_norm_pre': 'grad_w', 'grad_ffn_norm_post': 'grad_w', 'grad_ev_w_in': 'grad_w', 'grad_ev_a_conv_w': 'grad_w', 'grad_ev_a_conv_b': 'grad_w', 'grad_ev_a_ln_g': 'grad_w', 'grad_ev_a_ln_b': 'grad_w', 'grad_ev_sinks': 'grad_w', 'grad_ev_w_out': 'grad_w', 'grad_od_w_in': 'grad_w', 'grad_od_conv_w': 'grad_w', 'grad_od_w_out': 'grad_w', 'grad_ffn_w_up': 'grad_w', 'grad_ffn_conv_w': 'grad_w', 'grad_ffn_w_down': 'grad_w', 'delta_mix_norm_pre': 'delta_w', 'delta_mix_norm_post': 'delta_w', 'delta_ffn_norm_pre': 'delta_w', 'delta_ffn_norm_post': 'delta_w', 'delta_ev_w_in': 'delta_w', 'delta_ev_a_conv_w': 'delta_w', 'delta_ev_a_conv_b': 'delta_w', 'delta_ev_a_ln_g': 'delta_w', 'delta_ev_a_ln_b': 'delta_w', 'delta_ev_sinks': 'delta_w', 'delta_ev_w_out': 'delta_w', 'delta_od_w_in': 'delta_w', 'delta_od_conv_w': 'delta_w', 'delta_od_w_out': 'delta_w', 'delta_ffn_w_up': 'delta_w', 'delta_ffn_conv_w': 'delta_w', 'delta_ffn_w_down': 'delta_w', 'new_m_mix_norm_pre': 'new_m', 'new_m_mix_norm_post': 'new_m', 'new_m_ffn_norm_pre': 'new_m', 'new_m_ffn_norm_post': 'new_m', 'new_m_ev_w_in': 'new_m', 'new_m_ev_a_conv_w': 'new_m', 'new_m_ev_a_conv_b': 'new_m', 'new_m_ev_a_ln_g': 'new_m', 'new_m_ev_a_ln_b': 'new_m', 'new_m_ev_sinks': 'new_m', 'new_m_ev_w_out': 'new_m', 'new_m_od_w_in': 'new_m', 'new_m_od_conv_w': 'new_m', 'new_m_od_w_out': 'new_m', 'new_m_ffn_w_up': 'new_m', 'new_m_ffn_conv_w': 'new_m', 'new_m_ffn_w_down': 'new_m', 'new_v_mix_norm_pre': 'new_v', 'new_v_mix_norm_post': 'new_v', 'new_v_ffn_norm_pre': 'new_v', 'new_v_ffn_norm_post': 'new_v', 'new_v_ev_w_in': 'new_v', 'new_v_ev_a_conv_w': 'new_v', 'new_v_ev_a_conv_b': 'new_v', 'new_v_ev_a_ln_g': 'new_v', 'new_v_ev_a_ln_b': 'new_v', 'new_v_ev_sinks': 'new_v', 'new_v_ev_w_out': 'new_v', 'new_v_od_w_in': 'new_v', 'new_v_od_conv_w': 'new_v', 'new_v_od_w_out': 'new_v', 'new_v_ffn_w_up': 'new_v', 'new_v_ffn_conv_w': 'new_v', 'new_v_ffn_w_down': 'new_v'}


def _forward(args):
    return _fwd_reference(*[args[k] for k in FWD_PARAMS])


def _output_shape():
    def fwd():
        inp = _fwd_setup_inputs(0)
        return _fwd_reference(*[inp[k] for k in FWD_PARAMS])
    out = _jax.eval_shape(fwd)
    return out.shape, out.dtype

N_MICROBATCH = 1
ADAM_LR = 0.001
ADAM_B1 = 0.9
ADAM_B2 = 0.999
ADAM_EPS = 1e-08
ADAM_WD = 0.01
ADAM_STEP = 10
PER_EXAMPLE_BATCH_AXIS = {'x': 0, 'positions': 0, 'loss_target': 0}
SHARED_INPUTS = []
_WEIGHT_DTYPES = {'mix_norm_pre': _jnp.float32, 'mix_norm_post': _jnp.float32, 'ffn_norm_pre': _jnp.float32, 'ffn_norm_post': _jnp.float32, 'ev_w_in': _jnp.float32, 'ev_a_conv_w': _jnp.float32, 'ev_a_conv_b': _jnp.float32, 'ev_a_ln_g': _jnp.float32, 'ev_a_ln_b': _jnp.float32, 'ev_sinks': _jnp.float32, 'ev_w_out': _jnp.float32, 'od_w_in': _jnp.float32, 'od_conv_w': _jnp.float32, 'od_w_out': _jnp.float32, 'ffn_w_up': _jnp.float32, 'ffn_conv_w': _jnp.float32, 'ffn_w_down': _jnp.float32}
MOMENT_SCALE = {'mix_norm_pre': 2.244704e+00, 'mix_norm_post': 1.287547e+02, 'ffn_norm_pre': 3.012128e+00, 'ffn_norm_post': 1.282609e+02, 'ev_w_in': 1.754897e+00, 'ev_a_conv_w': 3.419773e+00, 'ev_a_conv_b': 4.468705e+01, 'ev_a_ln_g': 1.532856e+01, 'ev_a_ln_b': 2.618496e+01, 'ev_sinks': 8.116390e-01, 'ev_w_out': 5.295253e+00, 'od_w_in': 1.051150e+00, 'od_conv_w': 1.067308e+00, 'od_w_out': 1.149985e+00, 'ffn_w_up': 1.309906e+00, 'ffn_conv_w': 1.709490e+00, 'ffn_w_down': 2.891627e+00}


def _to_microbatches(a, axis):
    t = _jnp.moveaxis(a, axis, 0)
    t = t.reshape((N_MICROBATCH, t.shape[0] // N_MICROBATCH) + t.shape[1:])
    return _jnp.moveaxis(t, 1, axis + 1)


def setup_inputs(seed: int = 0) -> dict:
    inp = _fwd_setup_inputs(seed)
    key = _jax.random.fold_in(_jax.random.key(seed), 7919)
    shape, _ = _output_shape()
    out = dict(inp)
    out["loss_target"] = _jax.random.normal(_jax.random.fold_in(key, 0), shape, _jnp.float32)
    for i, name in enumerate(TWIN_WEIGHTS):
        w = inp[name].astype(_jnp.float32)
        if MOMENT_SCALE is None:
            s = _jnp.sqrt(_jnp.mean(_jnp.square(w)) + 1e-30)
        else:
            s = MOMENT_SCALE[name]
        km, kv = _jax.random.split(_jax.random.fold_in(key, i + 1))
        out[name] = w
        out["m_" + name] = s * _jax.random.normal(km, w.shape, _jnp.float32)
        out["v_" + name] = (s * s) * _jax.random.uniform(kv, w.shape, _jnp.float32, 0.5, 1.5)
    if N_MICROBATCH > 1:
        for name, axis in PER_EXAMPLE_BATCH_AXIS.items():
            out[name] = _to_microbatches(out[name], axis)
    return {'x': out['x'], 'positions': out['positions'], 'mix_norm_pre': out['mix_norm_pre'], 'mix_norm_post': out['mix_norm_post'], 'ffn_norm_pre': out['ffn_norm_pre'], 'ffn_norm_post': out['ffn_norm_post'], 'ev_w_in': out['ev_w_in'], 'ev_a_conv_w': out['ev_a_conv_w'], 'ev_a_conv_b': out['ev_a_conv_b'], 'ev_a_ln_g': out['ev_a_ln_g'], 'ev_a_ln_b': out['ev_a_ln_b'], 'ev_sinks': out['ev_sinks'], 'ev_w_out': out['ev_w_out'], 'od_w_in': out['od_w_in'], 'od_conv_w': out['od_conv_w'], 'od_w_out': out['od_w_out'], 'ffn_w_up': out['ffn_w_up'], 'ffn_conv_w': out['ffn_conv_w'], 'ffn_w_down': out['ffn_w_down'], 'loss_target': out['loss_target'], 'm_mix_norm_pre': out['m_mix_norm_pre'], 'm_mix_norm_post': out['m_mix_norm_post'], 'm_ffn_norm_pre': out['m_ffn_norm_pre'], 'm_ffn_norm_post': out['m_ffn_norm_post'], 'm_ev_w_in': out['m_ev_w_in'], 'm_ev_a_conv_w': out['m_ev_a_conv_w'], 'm_ev_a_conv_b': out['m_ev_a_conv_b'], 'm_ev_a_ln_g': out['m_ev_a_ln_g'], 'm_ev_a_ln_b': out['m_ev_a_ln_b'], 'm_ev_sinks': out['m_ev_sinks'], 'm_ev_w_out': out['m_ev_w_out'], 'm_od_w_in': out['m_od_w_in'], 'm_od_conv_w': out['m_od_conv_w'], 'm_od_w_out': out['m_od_w_out'], 'm_ffn_w_up': out['m_ffn_w_up'], 'm_ffn_conv_w': out['m_ffn_conv_w'], 'm_ffn_w_down': out['m_ffn_w_down'], 'v_mix_norm_pre': out['v_mix_norm_pre'], 'v_mix_norm_post': out['v_mix_norm_post'], 'v_ffn_norm_pre': out['v_ffn_norm_pre'], 'v_ffn_norm_post': out['v_ffn_norm_post'], 'v_ev_w_in': out['v_ev_w_in'], 'v_ev_a_conv_w': out['v_ev_a_conv_w'], 'v_ev_a_conv_b': out['v_ev_a_conv_b'], 'v_ev_a_ln_g': out['v_ev_a_ln_g'], 'v_ev_a_ln_b': out['v_ev_a_ln_b'], 'v_ev_sinks': out['v_ev_sinks'], 'v_ev_w_out': out['v_ev_w_out'], 'v_od_w_in': out['v_od_w_in'], 'v_od_conv_w': out['v_od_conv_w'], 'v_od_w_out': out['v_od_w_out'], 'v_ffn_w_up': out['v_ffn_w_up'], 'v_ffn_conv_w': out['v_ffn_conv_w'], 'v_ffn_w_down': out['v_ffn_w_down']}


def _loss(weights, diff, rest, loss_target):
    with _jax.named_scope("forward"):
        args = {**rest, TWIN_DIFF_INPUT: diff, **{k: w.astype(_WEIGHT_DTYPES[k]) for k, w in weights.items()}}
        y = _forward(args)
    with _jax.named_scope("loss_head"):
        err = _jnp.square(y.astype(_jnp.float32) - loss_target)
        return 0.5 * _jnp.sum(_jnp.mean(err, axis=-1)) if err.ndim else 0.5 * err


def _adamw(w, g, m, v):
    m = ADAM_B1 * m + (1.0 - ADAM_B1) * g
    v = ADAM_B2 * v + (1.0 - ADAM_B2) * _jnp.square(g)
    m_hat = m / (1.0 - ADAM_B1 ** ADAM_STEP)
    v_hat = v / (1.0 - ADAM_B2 ** ADAM_STEP)
    delta = -ADAM_LR * (m_hat / (_jnp.sqrt(v_hat) + ADAM_EPS) + ADAM_WD * w)
    return delta, m, v


def reference(x, positions, mix_norm_pre, mix_norm_post, ffn_norm_pre, ffn_norm_post, ev_w_in, ev_a_conv_w, ev_a_conv_b, ev_a_ln_g, ev_a_ln_b, ev_sinks, ev_w_out, od_w_in, od_conv_w, od_w_out, ffn_w_up, ffn_conv_w, ffn_w_down, loss_target, m_mix_norm_pre, m_mix_norm_post, m_ffn_norm_pre, m_ffn_norm_post, m_ev_w_in, m_ev_a_conv_w, m_ev_a_conv_b, m_ev_a_ln_g, m_ev_a_ln_b, m_ev_sinks, m_ev_w_out, m_od_w_in, m_od_conv_w, m_od_w_out, m_ffn_w_up, m_ffn_conv_w, m_ffn_w_down, v_mix_norm_pre, v_mix_norm_post, v_ffn_norm_pre, v_ffn_norm_post, v_ev_w_in, v_ev_a_conv_w, v_ev_a_conv_b, v_ev_a_ln_g, v_ev_a_ln_b, v_ev_sinks, v_ev_w_out, v_od_w_in, v_od_conv_w, v_od_w_out, v_ffn_w_up, v_ffn_conv_w, v_ffn_w_down):
    given = dict(x=x, positions=positions, mix_norm_pre=mix_norm_pre, mix_norm_post=mix_norm_post, ffn_norm_pre=ffn_norm_pre, ffn_norm_post=ffn_norm_post, ev_w_in=ev_w_in, ev_a_conv_w=ev_a_conv_w, ev_a_conv_b=ev_a_conv_b, ev_a_ln_g=ev_a_ln_g, ev_a_ln_b=ev_a_ln_b, ev_sinks=ev_sinks, ev_w_out=ev_w_out, od_w_in=od_w_in, od_conv_w=od_conv_w, od_w_out=od_w_out, ffn_w_up=ffn_w_up, ffn_conv_w=ffn_conv_w, ffn_w_down=ffn_w_down, loss_target=loss_target, m_mix_norm_pre=m_mix_norm_pre, m_mix_norm_post=m_mix_norm_post, m_ffn_norm_pre=m_ffn_norm_pre, m_ffn_norm_post=m_ffn_norm_post, m_ev_w_in=m_ev_w_in, m_ev_a_conv_w=m_ev_a_conv_w, m_ev_a_conv_b=m_ev_a_conv_b, m_ev_a_ln_g=m_ev_a_ln_g, m_ev_a_ln_b=m_ev_a_ln_b, m_ev_sinks=m_ev_sinks, m_ev_w_out=m_ev_w_out, m_od_w_in=m_od_w_in, m_od_conv_w=m_od_conv_w, m_od_w_out=m_od_w_out, m_ffn_w_up=m_ffn_w_up, m_ffn_conv_w=m_ffn_conv_w, m_ffn_w_down=m_ffn_w_down, v_mix_norm_pre=v_mix_norm_pre, v_mix_norm_post=v_mix_norm_post, v_ffn_norm_pre=v_ffn_norm_pre, v_ffn_norm_post=v_ffn_norm_post, v_ev_w_in=v_ev_w_in, v_ev_a_conv_w=v_ev_a_conv_w, v_ev_a_conv_b=v_ev_a_conv_b, v_ev_a_ln_g=v_ev_a_ln_g, v_ev_a_ln_b=v_ev_a_ln_b, v_ev_sinks=v_ev_sinks, v_ev_w_out=v_ev_w_out, v_od_w_in=v_od_w_in, v_od_conv_w=v_od_conv_w, v_od_w_out=v_od_w_out, v_ffn_w_up=v_ffn_w_up, v_ffn_conv_w=v_ffn_conv_w, v_ffn_w_down=v_ffn_w_down)
    weights = {n: given[n] for n in TWIN_WEIGHTS}
    shared = {n: given[n] for n in SHARED_INPUTS}
    per_example = {n: given[n] for n in ['x', 'positions']}
    grad_fn = _jax.value_and_grad(_loss, argnums=(0, 1))

    def one_microbatch(ex, loss_target):
        ex = dict(ex)
        diff = ex.pop(TWIN_DIFF_INPUT)
        return grad_fn(weights, diff, {**shared, **ex}, loss_target)

    if N_MICROBATCH == 1:
        loss, (grad_w, grad_x) = one_microbatch(per_example, given["loss_target"])
    else:
        def body(carry, xs):
            loss_sum, grad_sum = carry
            l_k, (gw_k, gx_k) = one_microbatch(xs[0], xs[1])
            with _jax.named_scope("update"):
                return (loss_sum + l_k, _jax.tree.map(_jnp.add, grad_sum, gw_k)), gx_k

        init = (_jnp.zeros((), _jnp.float32), _jax.tree.map(_jnp.zeros_like, weights))
        (loss, grad_w), grad_x = _jax.lax.scan(body, init, (per_example, given["loss_target"]))
    with _jax.named_scope("update"):
        delta_w, new_m, new_v = {}, {}, {}
        for n in TWIN_WEIGHTS:
            delta_w[n], new_m[n], new_v[n] = _adamw(weights[n], grad_w[n], given["m_" + n], given["v_" + n])
    return (loss, grad_x, *[grad_w[n] for n in TWIN_WEIGHTS], *[delta_w[n] for n in TWIN_WEIGHTS],
            *[new_m[n] for n in TWIN_WEIGHTS], *[new_v[n] for n in TWIN_WEIGHTS])
```

```python
import functools

import jax
import jax.numpy as jnp
from jax import lax
from jax.experimental import pallas as pl
from jax.experimental.pallas import tpu as pltpu

F32 = jnp.float32
BF16 = jnp.bfloat16
MESH = pl.DeviceIdType.MESH

D_MODEL = 1024
HEAD_DIM = 64
A_CH = 512
A_CONV = 31
N_Q_HEADS = 8
WINDOW = 128
ROPE_THETA = 500000.0
ROPE_DIM = 16
D_FF = 2816
RMS_EPS = 1e-6
LN_EPS = 1e-5
ADAM_LR = 0.001
ADAM_B1 = 0.9
ADAM_B2 = 0.999
ADAM_EPS = 1e-08
ADAM_WD = 0.01
ADAM_STEP = 10

LANES = 128
HALO16 = 16
HALO32 = 32
VMEM_LIMIT = 56 * 1024 * 1024
N_CHIPS = 4


def _cparams(sem):
    return pltpu.CompilerParams(dimension_semantics=sem, vmem_limit_bytes=VMEM_LIMIT)


def _tile(n, pref):
    if n <= pref:
        return n
    t = (pref // LANES) * LANES
    while t >= LANES:
        if n % t == 0:
            return t
        t -= LANES
    return n


def _mm(a, b, out_dtype, name):
    m, k = a.shape
    _, n = b.shape
    tm = 512 if k > 2048 else 1024
    tm = min(tm, m)
    tn = _tile(n, 512)

    def body(a_ref, b_ref, o_ref):
        o_ref[...] = jnp.dot(a_ref[...], b_ref[...], preferred_element_type=F32).astype(o_ref.dtype)

    return pl.pallas_call(
        body, name=name, grid=(m // tm, n // tn),
        in_specs=[pl.BlockSpec((tm, k), lambda i, j: (i, 0)), pl.BlockSpec((k, tn), lambda i, j: (0, j))],
        out_specs=pl.BlockSpec((tm, tn), lambda i, j: (i, j)),
        out_shape=jax.ShapeDtypeStruct((m, n), out_dtype),
        compiler_params=_cparams(("parallel", "parallel")),
    )(a, b)


def _mm_tn(a, b, name):
    s, k = a.shape
    _, n = b.shape
    tk = k if k <= 1024 else k // 2
    tn = _tile(n, 1408)
    ts = min(512, s)

    def body(a_ref, b_ref, o_ref):
        @pl.when(pl.program_id(2) == 0)
        def _():
            o_ref[...] = jnp.zeros_like(o_ref)

        o_ref[...] += lax.dot_general(a_ref[...], b_ref[...], (((0,), (0,)), ((), ())),
                                      preferred_element_type=F32)

    return pl.pallas_call(
        body, name=name, grid=(k // tk, n // tn, s // ts),
        in_specs=[pl.BlockSpec((ts, tk), lambda i, j, l: (l, i)), pl.BlockSpec((ts, tn), lambda i, j, l: (l, j))],
        out_specs=pl.BlockSpec((tk, tn), lambda i, j, l: (i, j)),
        out_shape=jax.ShapeDtypeStruct((k, n), F32),
        compiler_params=_cparams(("parallel", "parallel", "arbitrary")),
    )(a, b)


def _rms_fwd(x, g, name):
    s, d = x.shape
    tr = min(512, s)

    def body(x_ref, g_ref, h_ref):
        xv = x_ref[...]
        r = lax.rsqrt(jnp.mean(xv * xv, axis=-1, keepdims=True) + RMS_EPS)
        h_ref[...] = (xv * r * g_ref[...]).astype(BF16)

    return pl.pallas_call(
        body, name=name, grid=(s // tr,),
        in_specs=[pl.BlockSpec((tr, d), lambda i: (i, 0)), pl.BlockSpec((1, d), lambda i: (0, 0))],
        out_specs=pl.BlockSpec((tr, d), lambda i: (i, 0)),
        out_shape=jax.ShapeDtypeStruct((s, d), BF16),
        compiler_params=_cparams(("parallel",)),
    )(x, g)


def _post_fwd(m, g, xres, name):
    s, d = m.shape
    tr = min(512, s)

    def body(m_ref, g_ref, x_ref, o_ref):
        mv = m_ref[...]
        r = lax.rsqrt(jnp.mean(mv * mv, axis=-1, keepdims=True) + RMS_EPS)
        o_ref[...] = x_ref[...] + mv * r * g_ref[...]

    return pl.pallas_call(
        body, name=name, grid=(s // tr,),
        in_specs=[pl.BlockSpec((tr, d), lambda i: (i, 0)), pl.BlockSpec((1, d), lambda i: (0, 0)),
                  pl.BlockSpec((tr, d), lambda i: (i, 0))],
        out_specs=pl.BlockSpec((tr, d), lambda i: (i, 0)),
        out_shape=jax.ShapeDtypeStruct((s, d), F32),
        compiler_params=_cparams(("parallel",)),
    )(m, g, xres)


def _norm_bwd(dy, xin, g, res, out_dtype, name):
    s, d = xin.shape
    tr = min(512, s)
    has_res = res is not None

    def body(*refs):
        if has_res:
            dy_ref, x_ref, g_ref, res_ref, o_ref, dg_ref = refs
        else:
            dy_ref, x_ref, g_ref, o_ref, dg_ref = refs

        @pl.when(pl.program_id(0) == 0)
        def _():
            dg_ref[...] = jnp.zeros_like(dg_ref)

        xv = x_ref[...]
        dyv = dy_ref[...].astype(F32)
        r = lax.rsqrt(jnp.mean(xv * xv, axis=-1, keepdims=True) + RMS_EPS)
        nrm = xv * r
        dn = dyv * g_ref[...]
        dx = r * (dn - nrm * jnp.mean(dn * nrm, axis=-1, keepdims=True))
        if has_res:
            dx = dx + res_ref[...]
        o_ref[...] = dx.astype(o_ref.dtype)
        dg_ref[...] += jnp.sum(dyv * nrm, axis=0, keepdims=True)

    row = pl.BlockSpec((tr, d), lambda i: (i, 0))
    vec = pl.BlockSpec((1, d), lambda i: (0, 0))
    in_specs = [row, row, vec] + ([row] if has_res else [])
    args = (dy, xin, g) + ((res,) if has_res else ())
    return pl.pallas_call(
        body, name=name, grid=(s // tr,),
        in_specs=in_specs, out_specs=[row, vec],
        out_shape=[jax.ShapeDtypeStruct((s, d), out_dtype), jax.ShapeDtypeStruct((1, d), F32)],
        compiler_params=_cparams(("arbitrary",)),
    )(*args)


def _loss_bwd(y, target, name):
    s, d = y.shape
    tr = min(512, s)

    def body(y_ref, t_ref, acc_ref, dy_ref):
        @pl.when(pl.program_id(0) == 0)
        def _():
            acc_ref[...] = jnp.zeros_like(acc_ref)

        e = y_ref[...] - t_ref[...]
        dy_ref[...] = e * (1.0 / d)
        acc_ref[...] += jnp.sum(e * e, axis=0, keepdims=True)

    row = pl.BlockSpec((tr, d), lambda i: (i, 0))
    vec = pl.BlockSpec((1, d), lambda i: (0, 0))
    return pl.pallas_call(
        body, name=name, grid=(s // tr,),
        in_specs=[row, row], out_specs=[vec, row],
        out_shape=[jax.ShapeDtypeStruct((1, d), F32), jax.ShapeDtypeStruct((s, d), F32)],
        compiler_params=_cparams(("arbitrary",)),
    )(y, target)


def _cur(tr, w, col=0):
    return pl.BlockSpec((tr, w), lambda i: (i, col))


def _prev(tr, h, w, col=0):
    return pl.BlockSpec((h, w), lambda i: (jnp.maximum(i * (tr // h) - 1, 0), col))


def _next(tr, h, w, nrows, col=0):
    last = nrows // h - 1
    return pl.BlockSpec((h, w), lambda i: (jnp.minimum((i + 1) * (tr // h), last), col))


def _full(shape):
    return pl.BlockSpec(shape, lambda i: tuple(0 for _ in shape))


def _silu_parts(g):
    sig = jax.nn.sigmoid(g)
    return sig, g * sig


FFN_CW = 256


def _conv3_taps(buf, w, off, rows):
    return (w[0:1] * buf[pl.ds(off, rows), :] + w[1:2] * buf[pl.ds(off + 1, rows), :]
            + w[2:3] * buf[pl.ds(off + 2, rows), :])


def _ffn_act_fwd(up, conv_w, name):
    s, f2 = up.shape
    f = f2 // 2
    tr = min(256, s)
    h = HALO16
    cw = FFN_CW

    def body(up_ref, prev_ref, w_ref, o_ref, gbuf, vbuf):
        first = pl.program_id(0) == 0
        for j in range(f // cw):
            cg = slice(j * cw, (j + 1) * cw)
            cv = slice(f + j * cw, f + (j + 1) * cw)
            gbuf[0:h, :] = jnp.where(first, 0.0, prev_ref[:, cg].astype(F32))
            vbuf[0:h, :] = jnp.where(first, 0.0, prev_ref[:, cv].astype(F32))
            gbuf[h:h + tr, :] = up_ref[:, cg].astype(F32)
            vbuf[h:h + tr, :] = up_ref[:, cv].astype(F32)
            g = _conv3_taps(gbuf, w_ref[:, cg], h - 2, tr)
            v = _conv3_taps(vbuf, w_ref[:, cv], h - 2, tr)
            o_ref[:, cg] = (g * jax.nn.sigmoid(g) * v).astype(BF16)

    return pl.pallas_call(
        body, name=name, grid=(s // tr,),
        in_specs=[_cur(tr, f2), _prev(tr, h, f2), _full((3, f2))],
        out_specs=_cur(tr, f),
        out_shape=jax.ShapeDtypeStruct((s, f), BF16),
        scratch_shapes=[pltpu.VMEM((h + tr, cw), F32), pltpu.VMEM((h + tr, cw), F32)],
        compiler_params=_cparams(("parallel",)),
    )(up, up, conv_w)


def _ffn_act_bwd(dact, up, conv_w, name):
    s, f2 = up.shape
    f = f2 // 2
    tr = min(256, s)
    h = HALO16
    cw = FFN_CW
    ext = tr + h

    def body(da_ref, dan_ref, up_ref, upp_ref, upn_ref, w_ref, o_ref, dw_ref, gbuf, vbuf, dgbuf, dvbuf):
        i = pl.program_id(0)
        first = i == 0
        last = i == pl.num_programs(0) - 1

        @pl.when(first)
        def _():
            dw_ref[...] = jnp.zeros_like(dw_ref)

        for j in range(f // cw):
            cg = slice(j * cw, (j + 1) * cw)
            cv = slice(f + j * cw, f + (j + 1) * cw)
            for buf, cs in ((gbuf, cg), (vbuf, cv)):
                buf[0:h, :] = jnp.where(first, 0.0, upp_ref[:, cs].astype(F32))
                buf[h:h + tr, :] = up_ref[:, cs].astype(F32)
                buf[h + tr:h + tr + h, :] = upn_ref[:, cs].astype(F32)
            wg = w_ref[:, cg]
            wv = w_ref[:, cv]
            g = _conv3_taps(gbuf, wg, h - 2, ext)
            v = _conv3_taps(vbuf, wv, h - 2, ext)
            da = jnp.concatenate([da_ref[:, cg].astype(F32),
                                  jnp.where(last, 0.0, dan_ref[:, cg].astype(F32))], axis=0)
            sig, sil = _silu_parts(g)
            dgbuf[...] = da * v * (sig * (1.0 + g * (1.0 - sig)))
            dvbuf[...] = da * sil
            for dbuf, xbuf, wj, cs in ((dgbuf, gbuf, wg, cg), (dvbuf, vbuf, wv, cv)):
                dup = wj[2:3] * dbuf[pl.ds(0, tr), :] + wj[1:2] * dbuf[pl.ds(1, tr), :] + wj[0:1] * dbuf[pl.ds(2, tr), :]
                o_ref[:, cs] = dup.astype(BF16)
                du = dbuf[pl.ds(0, tr), :]
                for t in range(3):
                    dw_ref[t:t + 1, cs] += jnp.sum(du * xbuf[pl.ds(h - 2 + t, tr), :], axis=0, keepdims=True)

    return pl.pallas_call(
        body, name=name, grid=(s // tr,),
        in_specs=[_cur(tr, f), _next(tr, h, f, s), _cur(tr, f2), _prev(tr, h, f2), _next(tr, h, f2, s), _full((3, f2))],
        out_specs=[_cur(tr, f2), _full((3, f2))],
        out_shape=[jax.ShapeDtypeStruct((s, f2), BF16), jax.ShapeDtypeStruct((3, f2), F32)],
        scratch_shapes=[pltpu.VMEM((h + ext, cw), F32), pltpu.VMEM((h + ext, cw), F32),
                        pltpu.VMEM((ext, cw), F32), pltpu.VMEM((ext, cw), F32)],
        compiler_params=_cparams(("arbitrary",)),
    )(dact, dact, up, up, up, conv_w)


def _od_gate_fwd(z, conv_w, name):
    s, d3 = z.shape
    d = d3 // 3
    tr = min(256, s)
    h = HALO16
    cw = FFN_CW

    def body(z_ref, prev_ref, w_ref, o_ref, buf):
        first = pl.program_id(0) == 0
        for j in range(d // cw):
            cb = slice(j * cw, (j + 1) * cw)
            cc = slice(d + j * cw, d + (j + 1) * cw)
            cu = slice(2 * d + j * cw, 2 * d + (j + 1) * cw)
            buf[0:h, :] = jnp.where(first, 0.0, prev_ref[:, cc].astype(F32) * prev_ref[:, cu].astype(F32))
            buf[h:h + tr, :] = z_ref[:, cc].astype(F32) * z_ref[:, cu].astype(F32)
            k = _conv3_taps(buf, w_ref[:, cb], h - 2, tr)
            o_ref[:, cb] = (z_ref[:, cb].astype(F32) * k).astype(BF16)

    return pl.pallas_call(
        body, name=name, grid=(s // tr,),
        in_specs=[_cur(tr, d3), _prev(tr, h, d3), _full((3, d))],
        out_specs=_cur(tr, d),
        out_shape=jax.ShapeDtypeStruct((s, d), BF16),
        scratch_shapes=[pltpu.VMEM((h + tr, cw), F32)],
        compiler_params=_cparams(("parallel",)),
    )(z, z, conv_w)


def _od_gate_bwd(dy, z, conv_w, name):
    s, d3 = z.shape
    d = d3 // 3
    tr = min(256, s)
    h = HALO16
    cw = FFN_CW
    ext = tr + h

    def body(dy_ref, dyn_ref, z_ref, zp_ref, zn_ref, w_ref, o_ref, dw_ref, buf, dbuf):
        i = pl.program_id(0)
        first = i == 0
        last = i == pl.num_programs(0) - 1

        @pl.when(first)
        def _():
            dw_ref[...] = jnp.zeros_like(dw_ref)

        for j in range(d // cw):
            cb = slice(j * cw, (j + 1) * cw)
            cc = slice(d + j * cw, d + (j + 1) * cw)
            cu = slice(2 * d + j * cw, 2 * d + (j + 1) * cw)
            w = w_ref[:, cb]
            cval = z_ref[:, cc].astype(F32)
            uval = z_ref[:, cu].astype(F32)
            buf[0:h, :] = jnp.where(first, 0.0, zp_ref[:, cc].astype(F32) * zp_ref[:, cu].astype(F32))
            buf[h:h + tr, :] = cval * uval
            k = _conv3_taps(buf, w, h - 2, tr)
            dyv = dy_ref[:, cb]
            o_ref[:, cb] = (dyv * k).astype(BF16)
            dbuf[0:tr, :] = dyv * z_ref[:, cb].astype(F32)
            dbuf[tr:ext, :] = jnp.where(last, 0.0, dyn_ref[:, cb] * zn_ref[:, cb].astype(F32))
            dcu = w[2:3] * dbuf[pl.ds(0, tr), :] + w[1:2] * dbuf[pl.ds(1, tr), :] + w[0:1] * dbuf[pl.ds(2, tr), :]
            o_ref[:, cc] = (dcu * uval).astype(BF16)
            o_ref[:, cu] = (dcu * cval).astype(BF16)
            dk = dbuf[pl.ds(0, tr), :]
            for t in range(3):
                dw_ref[t:t + 1, cb] += jnp.sum(dk * buf[pl.ds(h - 2 + t, tr), :], axis=0, keepdims=True)

    return pl.pallas_call(
        body, name=name, grid=(s // tr,),
        in_specs=[_cur(tr, d), _next(tr, h, d, s), _cur(tr, d3), _prev(tr, h, d3), _next(tr, h, d3, s), _full((3, d))],
        out_specs=[_cur(tr, d3), _full((3, d))],
        out_shape=[jax.ShapeDtypeStruct((s, d3), BF16), jax.ShapeDtypeStruct((3, d), F32)],
        scratch_shapes=[pltpu.VMEM((h + tr, cw), F32), pltpu.VMEM((ext, cw), F32)],
        compiler_params=_cparams(("arbitrary",)),
    )(dy, dy, z, z, z, conv_w)


Q0 = 2 * A_CH
K0 = Q0 + N_Q_HEADS * HEAD_DIM
V0 = K0 + 2 * HEAD_DIM
EVEN_IN = V0 + 2 * HEAD_DIM


def _rope_tables(positions):
    half = ROPE_DIM // 2
    inv_freq = ROPE_THETA ** (-(jnp.arange(half, dtype=F32) * 2.0 / ROPE_DIM))
    ang = positions.astype(F32)[:, None] * inv_freq
    cos, sin = jnp.cos(ang), jnp.sin(ang)
    s = positions.shape[0]
    ones = jnp.ones((s, HEAD_DIM - ROPE_DIM), F32)
    zeros = jnp.zeros((s, HEAD_DIM - ROPE_DIM), F32)
    zh = jnp.zeros((s, half), F32)
    c = jnp.concatenate([cos, cos, ones], axis=1)
    sa = jnp.concatenate([zh, sin, zeros], axis=1)
    sb = jnp.concatenate([-sin, zh, zeros], axis=1)
    return jnp.concatenate([c, c, sa, sa, sb, sb], axis=1)


def _rope_fwd(x, tab):
    c, sa, sb = tab[:, 0:LANES], tab[:, LANES:2 * LANES], tab[:, 2 * LANES:3 * LANES]
    return x * c + pltpu.roll(x, 8, 1) * sa + pltpu.roll(x, LANES - 8, 1) * sb


def _rope_bwd(dy, tab):
    c, sa, sb = tab[:, 0:LANES], tab[:, LANES:2 * LANES], tab[:, 2 * LANES:3 * LANES]
    return dy * c + pltpu.roll(dy * sa, LANES - 8, 1) + pltpu.roll(dy * sb, 8, 1)


def _ln_fwd(c, g, b):
    mu = jnp.mean(c, axis=-1, keepdims=True)
    xc = c - mu
    r = lax.rsqrt(jnp.mean(xc * xc, axis=-1, keepdims=True) + LN_EPS)
    nrm = xc * r
    return nrm, r, nrm * g + b


def _ev_mid_fwd(z, tab, conv_w, conv_b, ln_g, ln_b, name):
    s = z.shape[0]
    tr = min(256, s)
    h = HALO32
    cw = LANES

    def body(z_ref, zp_ref, tab_ref, w_ref, b_ref, g_ref, lb_ref, c_ref, a_ref, qkv_ref, gbuf, cbuf):
        first = pl.program_id(0) == 0
        glu_p = zp_ref[:, 0:A_CH].astype(F32) * jax.nn.sigmoid(zp_ref[:, A_CH:2 * A_CH].astype(F32))
        gbuf[0:h, :] = jnp.where(first, 0.0, glu_p)
        gbuf[h:h + tr, :] = z_ref[:, 0:A_CH].astype(F32) * jax.nn.sigmoid(z_ref[:, A_CH:2 * A_CH].astype(F32))
        for j in range(A_CH // cw):
            cs = slice(j * cw, (j + 1) * cw)
            acc = jnp.broadcast_to(b_ref[:, cs], (tr, cw))
            for t in range(A_CONV):
                acc = acc + w_ref[t:t + 1, cs] * gbuf[pl.ds(h - (A_CONV - 1) + t, tr), cs]
            cbuf[:, cs] = acc
        c = cbuf[...]
        c_ref[...] = c.astype(BF16)
        _, _, l = _ln_fwd(c, g_ref[...], lb_ref[...])
        a_ref[...] = (l * jax.nn.sigmoid(l)).astype(BF16)
        tab_v = tab_ref[...]
        for p in range(4):
            xq = z_ref[:, Q0 + p * LANES:Q0 + (p + 1) * LANES].astype(F32)
            qkv_ref[:, p * LANES:(p + 1) * LANES] = _rope_fwd(xq, tab_v).astype(BF16)
        lane = lax.broadcasted_iota(jnp.int32, (tr, LANES), 1)
        lo = lane < HEAD_DIM
        kr = _rope_fwd(z_ref[:, K0:K0 + LANES].astype(F32), tab_v)
        vr = z_ref[:, V0:V0 + LANES].astype(F32)
        for base, val in ((4 * LANES, kr), (6 * LANES, vr)):
            sw = pltpu.roll(val, HEAD_DIM, 1)
            qkv_ref[:, base:base + LANES] = jnp.where(lo, val, sw).astype(BF16)
            qkv_ref[:, base + LANES:base + 2 * LANES] = jnp.where(lo, sw, val).astype(BF16)

    return pl.pallas_call(
        body, name=name, grid=(s // tr,),
        in_specs=[_cur(tr, EVEN_IN), _prev(tr, h, 2 * A_CH), _cur(tr, 3 * LANES), _full((A_CONV, A_CH)),
                  _full((1, A_CH)), _full((1, A_CH)), _full((1, A_CH))],
        out_specs=[_cur(tr, A_CH), _cur(tr, A_CH), _cur(tr, 2 * A_CH)],
        out_shape=[jax.ShapeDtypeStruct((s, A_CH), BF16), jax.ShapeDtypeStruct((s, A_CH), BF16),
                   jax.ShapeDtypeStruct((s, 2 * A_CH), BF16)],
        scratch_shapes=[pltpu.VMEM((h + tr, A_CH), F32), pltpu.VMEM((tr, A_CH), F32)],
        compiler_params=_cparams(("parallel",)),
    )(z, z, tab, conv_w, conv_b, ln_g, ln_b)


def _ev_mid_bwd(dcat, c, z, dq, dkv, tab, conv_w, ln_g, ln_b, name):
    s = z.shape[0]
    tr = min(256, s)
    h = HALO32
    cw = LANES
    ext = tr + h

    def body(da_ref, dan_ref, c_ref, cn_ref, z_ref, zp_ref, dq_ref, dkv_ref, tab_ref, w_ref, g_ref, lb_ref,
             dz_ref, dw_ref, dvec_ref, gbuf, dcbuf):
        i = pl.program_id(0)
        first = i == 0
        last = i == pl.num_programs(0) - 1

        @pl.when(first)
        def _():
            dw_ref[...] = jnp.zeros_like(dw_ref)
            dvec_ref[...] = jnp.zeros_like(dvec_ref)

        gv = g_ref[...]

        def ln_silu_bwd(cv, dav):
            nrm, r, l = _ln_fwd(cv, gv, lb_ref[...])
            sig = jax.nn.sigmoid(l)
            dl = dav * (sig * (1.0 + l * (1.0 - sig)))
            dn = dl * gv
            dc = r * (dn - jnp.mean(dn, axis=-1, keepdims=True) - nrm * jnp.mean(dn * nrm, axis=-1, keepdims=True))
            return dc, dl, nrm

        dc, dl, nrm = ln_silu_bwd(c_ref[...].astype(F32), da_ref[...])
        dcn, _, _ = ln_silu_bwd(cn_ref[...].astype(F32), dan_ref[...])
        dcbuf[0:tr, :] = dc
        dcbuf[tr:ext, :] = jnp.where(last, 0.0, dcn)
        dvec_ref[0:1, :] += jnp.sum(dc, axis=0, keepdims=True)
        dvec_ref[1:2, :] += jnp.sum(dl * nrm, axis=0, keepdims=True)
        dvec_ref[2:3, :] += jnp.sum(dl, axis=0, keepdims=True)

        a_lin = z_ref[:, 0:A_CH].astype(F32)
        sig_g = jax.nn.sigmoid(z_ref[:, A_CH:2 * A_CH].astype(F32))
        glu_p = zp_ref[:, 0:A_CH].astype(F32) * jax.nn.sigmoid(zp_ref[:, A_CH:2 * A_CH].astype(F32))
        gbuf[0:h, :] = jnp.where(first, 0.0, glu_p)
        gbuf[h:h + tr, :] = a_lin * sig_g
        for j in range(A_CH // cw):
            cs = slice(j * cw, (j + 1) * cw)
            dcj = dcbuf[pl.ds(0, tr), cs]
            acc = jnp.zeros((tr, cw), F32)
            for t in range(A_CONV):
                acc = acc + w_ref[t:t + 1, cs] * dcbuf[pl.ds(A_CONV - 1 - t, tr), cs]
                dw_ref[t:t + 1, cs] += jnp.sum(dcj * gbuf[pl.ds(h - (A_CONV - 1) + t, tr), cs], axis=0, keepdims=True)
            dz_ref[:, cs] = (acc * sig_g[:, cs]).astype(BF16)
            dz_ref[:, A_CH + j * cw:A_CH + (j + 1) * cw] = (
                acc * a_lin[:, cs] * sig_g[:, cs] * (1.0 - sig_g[:, cs])).astype(BF16)

        tab_v = tab_ref[...]
        for p in range(4):
            cs = slice(p * LANES, (p + 1) * LANES)
            dz_ref[:, Q0 + p * LANES:Q0 + (p + 1) * LANES] = _rope_bwd(dq_ref[:, cs], tab_v).astype(BF16)
        lane = lax.broadcasted_iota(jnp.int32, (tr, LANES), 1)
        lo = lane < HEAD_DIM

        def fold(base):
            p0 = dkv_ref[:, base:base + LANES]
            p1 = dkv_ref[:, base + LANES:base + 2 * LANES]
            s0 = p0 + pltpu.roll(p0, HEAD_DIM, 1)
            s1 = p1 + pltpu.roll(p1, HEAD_DIM, 1)
            return jnp.where(lo, s0, s1)

        dz_ref[:, K0:K0 + LANES] = _rope_bwd(fold(0), tab_v).astype(BF16)
        dz_ref[:, V0:V0 + LANES] = fold(2 * LANES).astype(BF16)

    return pl.pallas_call(
        body, name=name, grid=(s // tr,),
        in_specs=[_cur(tr, A_CH), _next(tr, h, A_CH, s), _cur(tr, A_CH), _next(tr, h, A_CH, s),
                  _cur(tr, EVEN_IN), _prev(tr, h, 2 * A_CH), _cur(tr, A_CH), _cur(tr, A_CH), _cur(tr, 3 * LANES),
                  _full((A_CONV, A_CH)), _full((1, A_CH)), _full((1, A_CH))],
        out_specs=[_cur(tr, EVEN_IN), _full((A_CONV, A_CH)), _full((8, A_CH))],
        out_shape=[jax.ShapeDtypeStruct((s, EVEN_IN), BF16), jax.ShapeDtypeStruct((A_CONV, A_CH), F32),
                   jax.ShapeDtypeStruct((8, A_CH), F32)],
        scratch_shapes=[pltpu.VMEM((h + tr, A_CH), F32), pltpu.VMEM((ext, A_CH), F32)],
        compiler_params=_cparams(("arbitrary",)),
    )(dcat, dcat, c, c, z, z, dq, dkv, tab, conv_w, ln_g, ln_b)


NT = (((1,), (1,)), ((), ()))
TN = (((0,), (0,)), ((), ()))
QB = WINDOW
SCALE = HEAD_DIM ** -0.5


def _att_probs(q2m, kwin, sink, mask):
    sc = lax.dot_general(q2m, kwin, NT, preferred_element_type=F32) * SCALE
    sc = jnp.where(mask, sc, -jnp.inf)
    mx = jnp.maximum(jnp.max(sc, axis=-1, keepdims=True), sink)
    p = jnp.exp(sc - mx)
    ps = jnp.exp(sink - mx)
    inv = 1.0 / (jnp.sum(p, axis=-1, keepdims=True) + ps)
    return p * inv, ps * inv


def _att_mask(i):
    r = lax.broadcasted_iota(jnp.int32, (QB, 2 * QB), 0)
    kc = lax.broadcasted_iota(jnp.int32, (QB, 2 * QB), 1)
    diff = r + QB - kc
    return (diff >= 0) & (diff < WINDOW) & ((kc >= QB) | (i > 0))


def _half_masks(dtype):
    lane = lax.broadcasted_iota(jnp.int32, (1, LANES), 1)
    return (lane < HEAD_DIM).astype(dtype), (lane >= HEAD_DIM).astype(dtype)


def _att_fwd(qkv, a, sinks, name):
    s = qkv.shape[0]
    nb = s // QB

    def body(sink_ref, qkv_ref, kvp_ref, a_ref, o_ref):
        i = pl.program_id(0)
        mask = _att_mask(i)
        mlo, mhi = _half_masks(BF16)
        o_ref[:, 0:A_CH] = a_ref[...]
        for p in range(4):
            g = p // 2
            q2 = qkv_ref[:, p * LANES:(p + 1) * LANES]
            kwin = jnp.concatenate([kvp_ref[:, g * LANES:(g + 1) * LANES],
                                    qkv_ref[:, A_CH + g * LANES:A_CH + (g + 1) * LANES]], axis=0)
            vwin = jnp.concatenate([kvp_ref[:, (2 + g) * LANES:(3 + g) * LANES],
                                    qkv_ref[:, A_CH + (2 + g) * LANES:A_CH + (3 + g) * LANES]], axis=0)
            pe, _ = _att_probs(q2 * mlo, kwin, sink_ref[2 * p], mask)
            po, _ = _att_probs(q2 * mhi, kwin, sink_ref[2 * p + 1], mask)
            o = (jnp.dot(pe.astype(BF16), vwin * mlo, preferred_element_type=F32)
                 + jnp.dot(po.astype(BF16), vwin * mhi, preferred_element_type=F32))
            o_ref[:, A_CH + p * LANES:A_CH + (p + 1) * LANES] = o.astype(BF16)

    grid_spec = pltpu.PrefetchScalarGridSpec(
        num_scalar_prefetch=1, grid=(nb,),
        in_specs=[pl.BlockSpec((QB, 2 * A_CH), lambda i, sk: (i, 0)),
                  pl.BlockSpec((QB, A_CH), lambda i, sk: (jnp.maximum(i - 1, 0), 1)),
                  pl.BlockSpec((QB, A_CH), lambda i, sk: (i, 0))],
        out_specs=pl.BlockSpec((QB, 2 * A_CH), lambda i, sk: (i, 0)),
    )
    return pl.pallas_call(
        body, name=name, grid_spec=grid_spec,
        out_shape=jax.ShapeDtypeStruct((s, 2 * A_CH), BF16),
        compiler_params=_cparams(("parallel",)),
    )(sinks, qkv, qkv, a)


def _att_bwd(qkv, dcat, sinks, name):
    s = qkv.shape[0]
    nb = s // QB

    def body(sink_ref, qkv_ref, kvp_ref, do_ref, dq_ref, dkv_ref, ds_ref, carry):
        i = pl.program_id(0)

        @pl.when(i == 0)
        def _():
            ds_ref[...] = jnp.zeros_like(ds_ref)
            carry[...] = jnp.zeros_like(carry)

        @pl.when(i < nb)
        def _():
            mask = _att_mask(i)
            mlo, mhi = _half_masks(BF16)
            dwin = [jnp.zeros((2 * QB, LANES), F32) for _ in range(4)]
            for p in range(4):
                g = p // 2
                q2 = qkv_ref[:, p * LANES:(p + 1) * LANES]
                kwin = jnp.concatenate([kvp_ref[:, g * LANES:(g + 1) * LANES],
                                        qkv_ref[:, A_CH + g * LANES:A_CH + (g + 1) * LANES]], axis=0)
                vwin = jnp.concatenate([kvp_ref[:, (2 + g) * LANES:(3 + g) * LANES],
                                        qkv_ref[:, A_CH + (2 + g) * LANES:A_CH + (3 + g) * LANES]], axis=0)
                do2 = do_ref[:, p * LANES:(p + 1) * LANES].astype(BF16)
                dq2 = jnp.zeros((QB, LANES), F32)
                for e, hm in enumerate((mlo, mhi)):
                    qm = q2 * hm
                    dom = do2 * hm
                    prob, psink = _att_probs(qm, kwin, sink_ref[2 * p + e], mask)
                    dp = lax.dot_general(dom, vwin, NT, preferred_element_type=F32)
                    delta = jnp.sum(prob * dp, axis=-1, keepdims=True)
                    dsc = (prob * (dp - delta) * SCALE).astype(BF16)
                    ds_ref[2 * p + e:2 * p + e + 1, :] += jnp.broadcast_to(
                        jnp.sum(-psink * delta, axis=0, keepdims=True), (1, LANES))
                    dq2 = dq2 + jnp.dot(dsc, kwin * hm, preferred_element_type=F32)
                    dwin[g] = dwin[g] + lax.dot_general(dsc, qm, TN, preferred_element_type=F32)
                    dwin[2 + g] = dwin[2 + g] + lax.dot_general(prob.astype(BF16), dom, TN,
                                                                preferred_element_type=F32)
                dq_ref[:, p * LANES:(p + 1) * LANES] = dq2
            for n in range(4):
                cs = slice(n * LANES, (n + 1) * LANES)
                dkv_ref[:, cs] = carry[:, cs] + dwin[n][0:QB, :]
                carry[:, cs] = dwin[n][QB:2 * QB, :]

        @pl.when(i == nb)
        def _():
            dkv_ref[...] = carry[...]

    grid_spec = pltpu.PrefetchScalarGridSpec(
        num_scalar_prefetch=1, grid=(nb + 1,),
        in_specs=[pl.BlockSpec((QB, 2 * A_CH), lambda i, sk: (jnp.minimum(i, nb - 1), 0)),
                  pl.BlockSpec((QB, A_CH), lambda i, sk: (jnp.maximum(jnp.minimum(i, nb - 1) - 1, 0), 1)),
                  pl.BlockSpec((QB, A_CH), lambda i, sk: (jnp.minimum(i, nb - 1), 1))],
        out_specs=[pl.BlockSpec((QB, A_CH), lambda i, sk: (jnp.minimum(i, nb - 1), 0)),
                   pl.BlockSpec((QB, A_CH), lambda i, sk: (jnp.maximum(i - 1, 0), 0)),
                   pl.BlockSpec((8, LANES), lambda i, sk: (0, 0))],
        scratch_shapes=[pltpu.VMEM((QB, A_CH), F32)],
    )
    return pl.pallas_call(
        body, name=name, grid_spec=grid_spec,
        out_shape=[jax.ShapeDtypeStruct((s, A_CH), F32), jax.ShapeDtypeStruct((s, A_CH), F32),
                   jax.ShapeDtypeStruct((8, LANES), F32)],
        compiler_params=_cparams(("arbitrary",)),
    )(sinks, qkv, qkv, dcat)


def _local_step(x, positions, target, w):
    row = lambda a, i: a[i:i + 1]
    tab = _rope_tables(positions)
    g = {}

    def ffn_fwd(xin, i):
        h = _rms_fwd(xin, row(w["ffn_norm_pre"], i), f"ffn{i}_pre")
        up = _mm(h, w["ffn_w_up"][i], BF16, f"ffn{i}_up")
        act = _ffn_act_fwd(up, w["ffn_conv_w"][i], f"ffn{i}_act")
        f = _mm(act, w["ffn_w_down"][i], F32, f"ffn{i}_down")
        xout = _post_fwd(f, row(w["ffn_norm_post"], i), xin, f"ffn{i}_post")
        return xout, (xin, h, up, act, f)

    def ffn_bwd(dxout, saved, i):
        xin, h, up, act, f = saved
        df, dg_post = _norm_bwd(dxout, f, row(w["ffn_norm_post"], i), None, BF16, f"ffn{i}_post_bwd")
        d_down = _mm_tn(act, df, f"ffn{i}_down_dw")
        dact = _mm(df, w["ffn_w_down_t"][i], BF16, f"ffn{i}_down_dx")
        dup, d_cw = _ffn_act_bwd(dact, up, w["ffn_conv_w"][i], f"ffn{i}_act_bwd")
        d_up = _mm_tn(h, dup, f"ffn{i}_up_dw")
        dh = _mm(dup, w["ffn_w_up_t"][i], F32, f"ffn{i}_up_dx")
        dxin, dg_pre = _norm_bwd(dh, xin, row(w["ffn_norm_pre"], i), dxout, F32, f"ffn{i}_pre_bwd")
        return dxin, dict(ffn_norm_post=dg_post, ffn_norm_pre=dg_pre, ffn_w_up=d_up, ffn_conv_w=d_cw, ffn_w_down=d_down)

    h0 = _rms_fwd(x, row(w["mix_norm_pre"], 0), "ev_pre")
    z0 = _mm(h0, w["ev_w_in"], BF16, "ev_in")
    c0, a0, qkv = _ev_mid_fwd(z0, tab, w["ev_a_conv_w"], w["ev_a_conv_b"], w["ev_a_ln_g"], w["ev_a_ln_b"], "ev_mid")
    cat = _att_fwd(qkv, a0, w["ev_sinks"], "ev_att")
    m0 = _mm(cat, w["ev_w_out"], F32, "ev_out")
    x1 = _post_fwd(m0, row(w["mix_norm_post"], 0), x, "ev_post")
    x2, ffn0 = ffn_fwd(x1, 0)
    h2 = _rms_fwd(x2, row(w["mix_norm_pre"], 1), "od_pre")
    z1 = _mm(h2, w["od_w_in"], BF16, "od_in")
    y1 = _od_gate_fwd(z1, w["od_conv_w"], "od_mid")
    m1 = _mm(y1, w["od_w_out"], F32, "od_out")
    x3 = _post_fwd(m1, row(w["mix_norm_post"], 1), x2, "od_post")
    x4, ffn1 = ffn_fwd(x3, 1)

    sq, dx4 = _loss_bwd(x4, target, "loss")

    dx3, gf1 = ffn_bwd(dx4, ffn1, 1)
    dm1, dg_mo1 = _norm_bwd(dx3, m1, row(w["mix_norm_post"], 1), None, BF16, "od_post_bwd")
    g["od_w_out"] = _mm_tn(y1, dm1, "od_out_dw")
    dy1 = _mm(dm1, w["od_w_out_t"], F32, "od_out_dx")
    dz1, g["od_conv_w"] = _od_gate_bwd(dy1, z1, w["od_conv_w"], "od_mid_bwd")
    g["od_w_in"] = _mm_tn(h2, dz1, "od_in_dw")
    dh2 = _mm(dz1, w["od_w_in_t"], F32, "od_in_dx")
    dx2, dg_mp1 = _norm_bwd(dh2, x2, row(w["mix_norm_pre"], 1), dx3, F32, "od_pre_bwd")

    dx1, gf0 = ffn_bwd(dx2, ffn0, 0)
    dm0, dg_mo0 = _norm_bwd(dx1, m0, row(w["mix_norm_post"], 0), None, BF16, "ev_post_bwd")
    g["ev_w_out"] = _mm_tn(cat, dm0, "ev_out_dw")
    dcat = _mm(dm0, w["ev_w_out_t"], F32, "ev_out_dx")
    dq, dkv, dsk = _att_bwd(qkv, dcat, w["ev_sinks"], "ev_att_bwd")
    dz0, g["ev_a_conv_w"], dvec = _ev_mid_bwd(dcat, c0, z0, dq, dkv, tab, w["ev_a_conv_w"], w["ev_a_ln_g"],
                                              w["ev_a_ln_b"], "ev_mid_bwd")
    g["ev_w_in"] = _mm_tn(h0, dz0, "ev_in_dw")
    dh0 = _mm(dz0, w["ev_w_in_t"], F32, "ev_in_dx")
    dx0, dg_mp0 = _norm_bwd(dh0, x, row(w["mix_norm_pre"], 0), dx1, F32, "ev_pre_bwd")

    g["ev_a_conv_b"] = dvec[0:1]
    g["ev_a_ln_g"] = dvec[1:2]
    g["ev_a_ln_b"] = dvec[2:3]
    g["ev_sinks"] = dsk[:, 0]
    g["mix_norm_pre"] = jnp.concatenate([dg_mp0, dg_mp1], axis=0)
    g["mix_norm_post"] = jnp.concatenate([dg_mo0, dg_mo1], axis=0)
    for k in ("ffn_norm_pre", "ffn_norm_post", "ffn_w_up", "ffn_conv_w", "ffn_w_down"):
        g[k] = jnp.stack([gf0[k], gf1[k]], axis=0) if gf0[k].shape[0] != 1 else jnp.concatenate([gf0[k], gf1[k]], axis=0)
    return sq, dx0, g


ANY = pl.BlockSpec(memory_space=pl.ANY)
PACK_COLS = 1024


def _me():
    return lax.axis_index("x"), lax.axis_index("y"), lax.axis_index("c")


def _other_chips(x, y):
    return [(1 - x, y), (x, 1 - y), (1 - x, 1 - y)]


def _remote(src, dst, send, recv, dev):
    return pltpu.make_async_remote_copy(src_ref=src, dst_ref=dst, send_sem=send, recv_sem=recv,
                                        device_id=dev, device_id_type=MESH)


def _gather_chips(wp, name):
    r, cols = wp.shape
    rh = r // 2

    def body(w_ref, o_ref, send, recv, lsem):
        x, y, c = _me()
        p = 2 * x + y
        sib = (x, y, 1 - c)
        chips = _other_chips(x, y)
        half = pl.ds(c * rh, rh)
        other = pl.ds((1 - c) * rh, rh)
        mine = pltpu.make_async_copy(w_ref, o_ref.at[p], lsem)
        mine.start()
        sent = [_remote(w_ref.at[half], o_ref.at[p, half], send.at[k], recv.at[k], (cx, cy, c))
                for k, (cx, cy) in enumerate(chips)]
        for cp in sent:
            cp.start()
        for k, (cx, cy) in enumerate(chips):
            q = 2 * cx + cy
            _remote(w_ref.at[half], o_ref.at[q, half], send.at[k], recv.at[k], (cx, cy, c)).wait_recv()
            fwd = _remote(o_ref.at[q, half], o_ref.at[q, half], send.at[3 + k], recv.at[3 + k], sib)
            fwd.start()
            sent.append(fwd)
        for k, (cx, cy) in enumerate(chips):
            q = 2 * cx + cy
            _remote(o_ref.at[q, other], o_ref.at[q, other], send.at[3 + k], recv.at[3 + k], sib).wait_recv()
        for cp in sent:
            cp.wait_send()
        mine.wait()

    return pl.pallas_call(
        body, name=name, in_specs=[ANY], out_specs=ANY,
        out_shape=jax.ShapeDtypeStruct((N_CHIPS, r, cols), wp.dtype),
        scratch_shapes=[pltpu.SemaphoreType.DMA((6,)), pltpu.SemaphoreType.DMA((6,)), pltpu.SemaphoreType.DMA],
    )(wp)


def _exchange8(v, reduce, name):
    r, cols = v.shape
    rel = [(a, b, d) for a in (0, 1) for b in (0, 1) for d in (0, 1) if (a, b, d) != (0, 0, 0)]

    def body(v_ref, o_ref, *rest):
        if reduce:
            gbuf, send, recv = rest
        else:
            gbuf = o_ref
            send, recv = rest
        x, y, c = _me()
        me = 4 * x + 2 * y + c
        gbuf[me] = v_ref[...]
        sent = []
        for k, (a, b, d) in enumerate(rel):
            cp = _remote(v_ref, gbuf.at[me], send.at[k], recv.at[k], ((x + a) % 2, (y + b) % 2, (c + d) % 2))
            cp.start()
            sent.append(cp)
        for k, (a, b, d) in enumerate(rel):
            src = 4 * ((x + a) % 2) + 2 * ((y + b) % 2) + (c + d) % 2
            _remote(v_ref, gbuf.at[src], send.at[k], recv.at[k], (x, y, c)).wait_recv()
        for cp in sent:
            cp.wait_send()
        if reduce:
            acc = gbuf[0]
            for n in range(1, 8):
                acc = acc + gbuf[n]
            o_ref[...] = acc

    vmem = pl.BlockSpec(memory_space=pltpu.VMEM)
    sems = [pltpu.SemaphoreType.DMA((7,)), pltpu.SemaphoreType.DMA((7,))]
    if reduce:
        out_shape = jax.ShapeDtypeStruct((r, cols), F32)
        scratch = [pltpu.VMEM((8, r, cols), F32)] + sems
    else:
        out_shape = jax.ShapeDtypeStruct((8, r, cols), F32)
        scratch = sems
    return pl.pallas_call(body, name=name, in_specs=[vmem], out_specs=vmem, out_shape=out_shape,
                          scratch_shapes=scratch)(v)


def _rs_swap(g, name):
    _, _, rh, cols = g.shape

    def body(g_ref, o_ref, send, recv):
        x, y, c = _me()
        cps = [_remote(g_ref.at[q, 1 - c], o_ref.at[q], send.at[q], recv.at[q], (x, y, 1 - c)) for q in range(N_CHIPS)]
        for cp in cps:
            cp.start()
        for cp in cps:
            cp.wait()

    return pl.pallas_call(
        body, name=name, in_specs=[ANY], out_specs=ANY,
        out_shape=jax.ShapeDtypeStruct((N_CHIPS, rh, cols), F32),
        scratch_shapes=[pltpu.SemaphoreType.DMA((N_CHIPS,)), pltpu.SemaphoreType.DMA((N_CHIPS,))],
    )(g)


def _row_tile(rows, pref):
    if rows <= pref:
        return rows
    t = (pref // 8) * 8
    while t >= 8:
        if rows % t == 0:
            return t
        t -= 8
    return rows


def _rs_add(g, sib, c, name):
    _, _, rh, cols = g.shape
    tr = _row_tile(rh, 512)

    def body(c_ref, g_ref, s_ref, o_ref):
        o_ref[...] = g_ref[...] + s_ref[...]

    grid_spec = pltpu.PrefetchScalarGridSpec(
        num_scalar_prefetch=1, grid=(N_CHIPS, rh // tr),
        in_specs=[pl.BlockSpec((None, None, tr, cols), lambda q, i, cr: (q, cr[0], i, 0)),
                  pl.BlockSpec((None, tr, cols), lambda q, i, cr: (q, i, 0))],
        out_specs=pl.BlockSpec((None, tr, cols), lambda q, i, cr: (q, i, 0)),
    )
    return pl.pallas_call(
        body, name=name, grid_spec=grid_spec,
        out_shape=jax.ShapeDtypeStruct((N_CHIPS, rh, cols), F32),
        compiler_params=_cparams(("parallel", "parallel")),
    )(c, g, sib)


def _rs_ici(a, name):
    _, rh, cols = a.shape

    def body(a_ref, o_ref, send, recv, lsem):
        x, y, c = _me()
        p = 2 * x + y
        mine = pltpu.make_async_copy(a_ref.at[p], o_ref.at[p], lsem)
        mine.start()
        cps = []
        for k, (cx, cy) in enumerate(_other_chips(x, y)):
            cp = _remote(a_ref.at[2 * cx + cy], o_ref.at[p], send.at[k], recv.at[k], (cx, cy, c))
            cp.start()
            cps.append(cp)
        for k, (cx, cy) in enumerate(_other_chips(x, y)):
            q = 2 * cx + cy
            _remote(a_ref.at[q], o_ref.at[q], send.at[k], recv.at[k], (cx, cy, c)).wait_recv()
        for cp in cps:
            cp.wait_send()
        mine.wait()

    return pl.pallas_call(
        body, name=name, in_specs=[ANY], out_specs=ANY,
        out_shape=jax.ShapeDtypeStruct((N_CHIPS, rh, cols), F32),
        scratch_shapes=[pltpu.SemaphoreType.DMA((3,)), pltpu.SemaphoreType.DMA((3,)), pltpu.SemaphoreType.DMA],
    )(a)


def _rs_sum(rb, name):
    _, rh, cols = rb.shape
    tr = _row_tile(rh, 512)

    def body(r0, r1, r2, r3, o_ref):
        o_ref[...] = ((r0[...] + r1[...]) + r2[...]) + r3[...]

    def spec(q):
        return pl.BlockSpec((None, tr, cols), lambda i: (q, i, 0))

    return pl.pallas_call(
        body, name=name, grid=(rh // tr,),
        in_specs=[spec(0), spec(1), spec(2), spec(3)],
        out_specs=pl.BlockSpec((tr, cols), lambda i: (i, 0)),
        out_shape=jax.ShapeDtypeStruct((rh, cols), F32),
        compiler_params=_cparams(("parallel",)),
    )(rb, rb, rb, rb)


def _rs_share(hsum, name):
    rh, cols = hsum.shape

    def body(h_ref, o_ref, send, recv, lsem):
        x, y, c = _me()
        mine = pltpu.make_async_copy(h_ref, o_ref.at[c], lsem)
        mine.start()
        cp = _remote(h_ref, o_ref.at[c], send, recv, (x, y, 1 - c))
        cp.start()
        _remote(h_ref, o_ref.at[1 - c], send, recv, (x, y, 1 - c)).wait_recv()
        cp.wait_send()
        mine.wait()

    return pl.pallas_call(
        body, name=name, in_specs=[ANY], out_specs=ANY,
        out_shape=jax.ShapeDtypeStruct((2, rh, cols), F32),
        scratch_shapes=[pltpu.SemaphoreType.DMA, pltpu.SemaphoreType.DMA, pltpu.SemaphoreType.DMA],
    )(hsum)


def _adamw(w, g, m, v, name):
    rows, cols = w.shape
    tr = _row_tile(rows, 512)

    def body(w_ref, g_ref, m_ref, v_ref, d_ref, nm_ref, nv_ref):
        gv = g_ref[...]
        nm = ADAM_B1 * m_ref[...] + (1.0 - ADAM_B1) * gv
        nv = ADAM_B2 * v_ref[...] + (1.0 - ADAM_B2) * (gv * gv)
        m_hat = nm / (1.0 - ADAM_B1 ** ADAM_STEP)
        v_hat = nv / (1.0 - ADAM_B2 ** ADAM_STEP)
        d_ref[...] = -ADAM_LR * (m_hat / (jnp.sqrt(v_hat) + ADAM_EPS) + ADAM_WD * w_ref[...])
        nm_ref[...] = nm
        nv_ref[...] = nv

    spec = pl.BlockSpec((tr, cols), lambda i: (i, 0))
    shp = jax.ShapeDtypeStruct((rows, cols), F32)
    return pl.pallas_call(
        body, name=name, grid=(rows // tr,), in_specs=[spec] * 4, out_specs=[spec] * 3, out_shape=[shp] * 3,
        compiler_params=_cparams(("parallel",)),
    )(w, g, m, v)


WEIGHTS = ("mix_norm_pre", "mix_norm_post", "ffn_norm_pre", "ffn_norm_post", "ev_w_in", "ev_a_conv_w", "ev_a_conv_b",
           "ev_a_ln_g", "ev_a_ln_b", "ev_sinks", "ev_w_out", "od_w_in", "od_conv_w", "od_w_out", "ffn_w_up",
           "ffn_conv_w", "ffn_w_down")
MATS = (("ev_w_in", 2), ("ev_w_out", 1), ("od_w_in", 2), ("od_w_out", 1), ("ffn_w_up", 2), ("ffn_w_down", 1))
SMALL_SHARDED = ("ev_a_conv_w", "od_conv_w", "ffn_conv_w")
REPLICATED = ("mix_norm_pre", "mix_norm_post", "ffn_norm_pre", "ffn_norm_post", "ev_a_conv_b", "ev_a_ln_g",
              "ev_a_ln_b", "ev_sinks")


def _pack(parts, rows_multiple):
    flat = jnp.concatenate([p.reshape(-1) for p in parts])
    unit = rows_multiple * PACK_COLS
    pad = (-flat.shape[0]) % unit
    if pad:
        flat = jnp.concatenate([flat, jnp.zeros((pad,), flat.dtype)])
    return flat.reshape(-1, PACK_COLS)


def _unpack(buf, shapes):
    flat = buf.reshape(-1)
    out, off = [], 0
    for shp in shapes:
        n = 1
        for d in shp:
            n *= d
        out.append(flat[off:off + n].reshape(shp))
        off += n
    return out


def _shards_of(full, axis):
    n = full.shape[axis] // N_CHIPS
    return [lax.slice_in_dim(full, q * n, (q + 1) * n, axis=axis) for q in range(N_CHIPS)]


def kernel(x, positions, mix_norm_pre, mix_norm_post, ffn_norm_pre, ffn_norm_post, ev_w_in, ev_a_conv_w, ev_a_conv_b, ev_a_ln_g, ev_a_ln_b, ev_sinks, ev_w_out, od_w_in, od_conv_w, od_w_out, ffn_w_up, ffn_conv_w, ffn_w_down, loss_target, m_mix_norm_pre, m_mix_norm_post, m_ffn_norm_pre, m_ffn_norm_post, m_ev_w_in, m_ev_a_conv_w, m_ev_a_conv_b, m_ev_a_ln_g, m_ev_a_ln_b, m_ev_sinks, m_ev_w_out, m_od_w_in, m_od_conv_w, m_od_w_out, m_ffn_w_up, m_ffn_conv_w, m_ffn_w_down, v_mix_norm_pre, v_mix_norm_post, v_ffn_norm_pre, v_ffn_norm_post, v_ev_w_in, v_ev_a_conv_w, v_ev_a_conv_b, v_ev_a_ln_g, v_ev_a_ln_b, v_ev_sinks, v_ev_w_out, v_od_w_in, v_od_conv_w, v_od_w_out, v_ffn_w_up, v_ffn_conv_w, v_ffn_w_down):
    wts = dict(zip(WEIGHTS, (mix_norm_pre, mix_norm_post, ffn_norm_pre, ffn_norm_post, ev_w_in, ev_a_conv_w, ev_a_conv_b,
                             ev_a_ln_g, ev_a_ln_b, ev_sinks, ev_w_out, od_w_in, od_conv_w, od_w_out, ffn_w_up, ffn_conv_w,
                             ffn_w_down)))
    mom = dict(zip(WEIGHTS, (m_mix_norm_pre, m_mix_norm_post, m_ffn_norm_pre, m_ffn_norm_post, m_ev_w_in, m_ev_a_conv_w,
                             m_ev_a_conv_b, m_ev_a_ln_g, m_ev_a_ln_b, m_ev_sinks, m_ev_w_out, m_od_w_in, m_od_conv_w,
                             m_od_w_out, m_ffn_w_up, m_ffn_conv_w, m_ffn_w_down)))
    var = dict(zip(WEIGHTS, (v_mix_norm_pre, v_mix_norm_post, v_ffn_norm_pre, v_ffn_norm_post, v_ev_w_in, v_ev_a_conv_w,
                             v_ev_a_conv_b, v_ev_a_ln_g, v_ev_a_ln_b, v_ev_sinks, v_ev_w_out, v_od_w_in, v_od_conv_w,
                             v_od_w_out, v_ffn_w_up, v_ffn_conv_w, v_ffn_w_down)))
    xi, yi, ci = _me()
    chip = 2 * xi + yi

    mat_shapes = [wts[k].shape for k, _ in MATS]
    gathered = _gather_chips(_pack([wts[k].astype(BF16) for k, _ in MATS], 32), "gather_mats")
    small_shapes = [wts[k].shape for k in SMALL_SHARDED]
    small_all = _exchange8(_pack([wts[k] for k in SMALL_SHARDED], 8), False, "gather_small")
    w = {k: wts[k] for k in REPLICATED}
    per_chip = [_unpack(gathered[q], mat_shapes) for q in range(N_CHIPS)]
    for n, (k, axis) in enumerate(MATS):
        w[k] = jnp.concatenate([per_chip[q][n] for q in range(N_CHIPS)], axis=axis)
    per_chip = [_unpack(small_all[2 * q], small_shapes) for q in range(N_CHIPS)]
    for n, k in enumerate(SMALL_SHARDED):
        w[k] = jnp.concatenate([per_chip[q][n] for q in range(N_CHIPS)], axis=-1)
    for k in ("ev_w_in", "ev_w_out", "od_w_in", "od_w_out"):
        w[k] = w[k][0]
        w[k + "_t"] = w[k].T
    for k in ("ffn_w_up", "ffn_w_down"):
        w[k + "_t"] = jnp.swapaxes(w[k], 1, 2)
    for k in ("ev_a_conv_w", "od_conv_w"):
        w[k] = w[k][0]
    w["ev_sinks"] = w["ev_sinks"][0]

    sq, dx, g = _local_step(x[0], positions[0], loss_target[0], w)
    loss = lax.psum(0.5 * jnp.sum(sq) / D_MODEL, ("x", "y", "c"))

    shards = {k: _shards_of(g[k].reshape(wts[k].shape[:-2] + g[k].shape[-2:]), axis) for k, axis in MATS}
    gp = jnp.stack([_pack([shards[k][q] for k, _ in MATS], 32) for q in range(N_CHIPS)], axis=0)
    rows = gp.shape[1]
    gp = gp.reshape(N_CHIPS, 2, rows // 2, PACK_COLS)
    sib = _rs_swap(gp, "rs_swap")
    pair = _rs_add(gp, sib, jnp.reshape(ci, (1,)).astype(jnp.int32), "rs_add")
    landed = _rs_ici(pair, "rs_ici")
    half = _rs_sum(landed, "rs_sum")
    red = _rs_share(half, "rs_share").reshape(rows, PACK_COLS)
    grads = dict(zip([k for k, _ in MATS], _unpack(red, mat_shapes)))

    small_keys = REPLICATED + SMALL_SHARDED
    full_shapes = [wts[k].shape for k in REPLICATED] + [wts[k].shape[:-1] + (wts[k].shape[-1] * N_CHIPS,) for k in SMALL_SHARDED]
    sm = _exchange8(_pack([g[k] for k in small_keys], 8), True, "reduce_small")
    for k, full in zip(small_keys, _unpack(sm, full_shapes)):
        if k in SMALL_SHARDED:
            n = wts[k].shape[-1]
            full = lax.dynamic_slice_in_dim(full, chip * n, n, axis=full.ndim - 1)
        grads[k] = full

    deltas, new_m, new_v = {}, {}, {}
    for k in WEIGHTS:
        shp = wts[k].shape
        two_d = (-1, shp[-1])
        d, nm, nv = _adamw(wts[k].reshape(two_d), grads[k].reshape(two_d), mom[k].reshape(two_d), var[k].reshape(two_d),
                           "adamw_" + k)
        deltas[k], new_m[k], new_v[k] = d.reshape(shp), nm.reshape(shp), nv.reshape(shp)

    return (loss, dx[None], *[grads[k] for k in WEIGHTS], *[deltas[k] for k in WEIGHTS],
            *[new_m[k] for k in WEIGHTS], *[new_v[k] for k in WEIGHTS])
```

```python
import functools

import jax
import jax.numpy as jnp
from jax import lax
from jax.experimental import pallas as pl
from jax.experimental.pallas import tpu as pltpu

F32 = jnp.float32
BF16 = jnp.bfloat16
MESH = pl.DeviceIdType.MESH

D_MODEL = 1024
HEAD_DIM = 64
A_CH = 512
A_CONV = 31
N_Q_HEADS = 8
WINDOW = 128
ROPE_THETA = 500000.0
ROPE_DIM = 16
D_FF = 2816
RMS_EPS = 1e-6
LN_EPS = 1e-5
ADAM_LR = 0.001
ADAM_B1 = 0.9
ADAM_B2 = 0.999
ADAM_EPS = 1e-08
ADAM_WD = 0.01
ADAM_STEP = 10

LANES = 128
HALO16 = 16
HALO32 = 32
VMEM_LIMIT = 56 * 1024 * 1024
FFN_BWD_VMEM = 60 * 1024 * 1024
N_CHIPS = 4


def _cparams(sem):
    return pltpu.CompilerParams(dimension_semantics=sem, vmem_limit_bytes=VMEM_LIMIT)


def _tile(n, pref):
    if n <= pref:
        return n
    t = (pref // LANES) * LANES
    while t >= LANES:
        if n % t == 0:
            return t
        t -= LANES
    return n


def _mm(a, b, out_dtype, name):
    m, k = a.shape
    _, n = b.shape
    tm = 512 if k > 2048 else 1024
    tm = min(tm, m)
    tn = _tile(n, 512)

    def body(a_ref, b_ref, o_ref):
        o_ref[...] = jnp.dot(a_ref[...], b_ref[...], preferred_element_type=F32).astype(o_ref.dtype)

    return pl.pallas_call(
        body, name=name, grid=(m // tm, n // tn),
        in_specs=[pl.BlockSpec((tm, k), lambda i, j: (i, 0)), pl.BlockSpec((k, tn), lambda i, j: (0, j))],
        out_specs=pl.BlockSpec((tm, tn), lambda i, j: (i, j)),
        out_shape=jax.ShapeDtypeStruct((m, n), out_dtype),
        compiler_params=_cparams(("parallel", "parallel")),
    )(a, b)


def _mm_tn(a, b, name):
    s, k = a.shape
    _, n = b.shape
    tk = k if k <= 1024 else k // 2
    tn = _tile(n, 1408)
    ts = min(512, s)

    def body(a_ref, b_ref, o_ref):
        @pl.when(pl.program_id(2) == 0)
        def _():
            o_ref[...] = jnp.zeros_like(o_ref)

        o_ref[...] += lax.dot_general(a_ref[...], b_ref[...], (((0,), (0,)), ((), ())),
                                      preferred_element_type=F32)

    return pl.pallas_call(
        body, name=name, grid=(k // tk, n // tn, s // ts),
        in_specs=[pl.BlockSpec((ts, tk), lambda i, j, l: (l, i)), pl.BlockSpec((ts, tn), lambda i, j, l: (l, j))],
        out_specs=pl.BlockSpec((tk, tn), lambda i, j, l: (i, j)),
        out_shape=jax.ShapeDtypeStruct((k, n), F32),
        compiler_params=_cparams(("parallel", "parallel", "arbitrary")),
    )(a, b)


def _rms_fwd(x, g, name):
    s, d = x.shape
    tr = min(512, s)

    def body(x_ref, g_ref, h_ref):
        xv = x_ref[...]
        r = lax.rsqrt(jnp.mean(xv * xv, axis=-1, keepdims=True) + RMS_EPS)
        h_ref[...] = (xv * r * g_ref[...]).astype(BF16)

    return pl.pallas_call(
        body, name=name, grid=(s // tr,),
        in_specs=[pl.BlockSpec((tr, d), lambda i: (i, 0)), pl.BlockSpec((1, d), lambda i: (0, 0))],
        out_specs=pl.BlockSpec((tr, d), lambda i: (i, 0)),
        out_shape=jax.ShapeDtypeStruct((s, d), BF16),
        compiler_params=_cparams(("parallel",)),
    )(x, g)


def _post_fwd(m, g, xres, name):
    s, d = m.shape
    tr = min(512, s)

    def body(m_ref, g_ref, x_ref, o_ref):
        mv = m_ref[...]
        r = lax.rsqrt(jnp.mean(mv * mv, axis=-1, keepdims=True) + RMS_EPS)
        o_ref[...] = x_ref[...] + mv * r * g_ref[...]

    return pl.pallas_call(
        body, name=name, grid=(s // tr,),
        in_specs=[pl.BlockSpec((tr, d), lambda i: (i, 0)), pl.BlockSpec((1, d), lambda i: (0, 0)),
                  pl.BlockSpec((tr, d), lambda i: (i, 0))],
        out_specs=pl.BlockSpec((tr, d), lambda i: (i, 0)),
        out_shape=jax.ShapeDtypeStruct((s, d), F32),
        compiler_params=_cparams(("parallel",)),
    )(m, g, xres)


def _norm_bwd(dy, xin, g, res, out_dtype, name):
    s, d = xin.shape
    tr = min(512, s)
    has_res = res is not None

    def body(*refs):
        if has_res:
            dy_ref, x_ref, g_ref, res_ref, o_ref, dg_ref = refs
        else:
            dy_ref, x_ref, g_ref, o_ref, dg_ref = refs

        @pl.when(pl.program_id(0) == 0)
        def _():
            dg_ref[...] = jnp.zeros_like(dg_ref)

        xv = x_ref[...]
        dyv = dy_ref[...].astype(F32)
        r = lax.rsqrt(jnp.mean(xv * xv, axis=-1, keepdims=True) + RMS_EPS)
        nrm = xv * r
        dn = dyv * g_ref[...]
        dx = r * (dn - nrm * jnp.mean(dn * nrm, axis=-1, keepdims=True))
        if has_res:
            dx = dx + res_ref[...]
        o_ref[...] = dx.astype(o_ref.dtype)
        dg_ref[...] += jnp.sum(dyv * nrm, axis=0, keepdims=True)

    row = pl.BlockSpec((tr, d), lambda i: (i, 0))
    vec = pl.BlockSpec((1, d), lambda i: (0, 0))
    in_specs = [row, row, vec] + ([row] if has_res else [])
    args = (dy, xin, g) + ((res,) if has_res else ())
    return pl.pallas_call(
        body, name=name, grid=(s // tr,),
        in_specs=in_specs, out_specs=[row, vec],
        out_shape=[jax.ShapeDtypeStruct((s, d), out_dtype), jax.ShapeDtypeStruct((1, d), F32)],
        compiler_params=_cparams(("arbitrary",)),
    )(*args)


def _loss_bwd(y, target, name):
    s, d = y.shape
    tr = min(512, s)

    def body(y_ref, t_ref, acc_ref, dy_ref):
        @pl.when(pl.program_id(0) == 0)
        def _():
            acc_ref[...] = jnp.zeros_like(acc_ref)

        e = y_ref[...] - t_ref[...]
        dy_ref[...] = e * (1.0 / d)
        acc_ref[...] += jnp.sum(e * e, axis=0, keepdims=True)

    row = pl.BlockSpec((tr, d), lambda i: (i, 0))
    vec = pl.BlockSpec((1, d), lambda i: (0, 0))
    return pl.pallas_call(
        body, name=name, grid=(s // tr,),
        in_specs=[row, row], out_specs=[vec, row],
        out_shape=[jax.ShapeDtypeStruct((1, d), F32), jax.ShapeDtypeStruct((s, d), F32)],
        compiler_params=_cparams(("arbitrary",)),
    )(y, target)


def _cur(tr, w, col=0):
    return pl.BlockSpec((tr, w), lambda i: (i, col))


def _prev(tr, h, w, col=0):
    return pl.BlockSpec((h, w), lambda i: (jnp.maximum(i * (tr // h) - 1, 0), col))


def _next(tr, h, w, nrows, col=0):
    last = nrows // h - 1
    return pl.BlockSpec((h, w), lambda i: (jnp.minimum((i + 1) * (tr // h), last), col))


def _full(shape):
    return pl.BlockSpec(shape, lambda i: tuple(0 for _ in shape))


def _silu_parts(g):
    sig = jax.nn.sigmoid(g)
    return sig, g * sig


FFN_CW = 256


def _conv3_taps(buf, w, off, rows):
    return (w[0:1] * buf[pl.ds(off, rows), :] + w[1:2] * buf[pl.ds(off + 1, rows), :]
            + w[2:3] * buf[pl.ds(off + 2, rows), :])


WHOLE_VMEM = pl.BlockSpec(memory_space=pltpu.VMEM)


def _ffn_fwd(x, g_pre, wu, conv_w, wd, g_post, name):
    s, d = x.shape
    f2 = wu.shape[1]
    f = f2 // 2
    tr = min(256, s)
    h = HALO16
    cw = FFN_CW

    def body(x_ref, gpre_ref, wu_ref, cw_ref, wd_ref, gpost_ref, xo_ref, f_ref, h_ref, up_ref, u_ref,
             carry, gbuf, vbuf, facc):
        @pl.when(pl.program_id(0) == 0)
        def _():
            carry[...] = jnp.zeros_like(carry)

        xv = x_ref[...]
        r = lax.rsqrt(jnp.mean(xv * xv, axis=-1, keepdims=True) + RMS_EPS)
        hv = (xv * r * gpre_ref[...]).astype(BF16)
        h_ref[...] = hv
        for j in range(f // cw):
            cg = slice(j * cw, (j + 1) * cw)
            cv = slice(f + j * cw, f + (j + 1) * cw)
            for buf, cs in ((gbuf, cg), (vbuf, cv)):
                upc = jnp.dot(hv, wu_ref[:, cs], preferred_element_type=F32)
                up_ref[:, cs] = upc.astype(BF16)
                buf[0:h, :] = carry[:, cs]
                buf[h:h + tr, :] = upc
                carry[:, cs] = upc[tr - h:tr, :]
            g = _conv3_taps(gbuf, cw_ref[:, cg], h - 2, tr)
            v = _conv3_taps(vbuf, cw_ref[:, cv], h - 2, tr)
            u_ref[:, cg] = g.astype(BF16)
            u_ref[:, cv] = v.astype(BF16)
            act = (g * jax.nn.sigmoid(g) * v).astype(BF16)
            part = jnp.dot(act, wd_ref[j * cw:(j + 1) * cw, :], preferred_element_type=F32)
            if j == 0:
                facc[...] = part
            else:
                facc[...] += part
        fv = facc[...]
        f_ref[...] = fv
        r2 = lax.rsqrt(jnp.mean(fv * fv, axis=-1, keepdims=True) + RMS_EPS)
        xo_ref[...] = xv + fv * r2 * gpost_ref[...]

    row = _cur(tr, d)
    wide = _cur(tr, f2)
    return pl.pallas_call(
        body, name=name, grid=(s // tr,),
        in_specs=[row, _full((1, d)), WHOLE_VMEM, _full((3, f2)), WHOLE_VMEM, _full((1, d))],
        out_specs=[row, row, row, wide, wide],
        out_shape=[jax.ShapeDtypeStruct((s, d), F32), jax.ShapeDtypeStruct((s, d), F32),
                   jax.ShapeDtypeStruct((s, d), BF16), jax.ShapeDtypeStruct((s, f2), BF16),
                   jax.ShapeDtypeStruct((s, f2), BF16)],
        scratch_shapes=[pltpu.VMEM((h, f2), F32), pltpu.VMEM((h + tr, cw), F32), pltpu.VMEM((h + tr, cw), F32),
                        pltpu.VMEM((tr, d), F32)],
        compiler_params=_cparams(("arbitrary",)),
    )(x, g_pre, wu, conv_w, wd, g_post)


def _ffn_bwd(dxo, fout, x, up, u, g_pre, g_post, wd_t, wu_t, conv_w, name):
    s, d = x.shape
    f2 = up.shape[1]
    f = f2 // 2
    tr = min(256, s)
    nt = s // tr
    h = HALO16
    cw = FFN_CW

    def body(dy_ref, f_ref, x_ref, up_ref, u_ref, gpre_ref, gpost_ref, wdt_ref, wut_ref, cw_ref,
             dx_ref, dup_ref, act_ref, df_ref, dcw_ref, dgpost_ref, dgpre_ref, carry, dgbuf, dvbuf, dhacc):
        @pl.when(pl.program_id(0) == 0)
        def _():
            carry[...] = jnp.zeros_like(carry)
            dcw_ref[...] = jnp.zeros_like(dcw_ref)
            dgpost_ref[...] = jnp.zeros_like(dgpost_ref)
            dgpre_ref[...] = jnp.zeros_like(dgpre_ref)

        dy = dy_ref[...]
        fv = f_ref[...]
        r = lax.rsqrt(jnp.mean(fv * fv, axis=-1, keepdims=True) + RMS_EPS)
        nrm = fv * r
        dn = dy * gpost_ref[...]
        dfv = (r * (dn - nrm * jnp.mean(dn * nrm, axis=-1, keepdims=True))).astype(BF16)
        dgpost_ref[...] += jnp.sum(dy * nrm, axis=0, keepdims=True)
        df_ref[...] = dfv
        for j in range(f // cw):
            ch = slice(j * cw, (j + 1) * cw)
            cg = ch
            cv = slice(f + j * cw, f + (j + 1) * cw)
            dact = jnp.dot(dfv, wdt_ref[:, ch], preferred_element_type=F32)
            g = u_ref[:, cg].astype(F32)
            v = u_ref[:, cv].astype(F32)
            sig, sil = _silu_parts(g)
            act_ref[:, ch] = (sil * v).astype(BF16)
            du_g = dact * v * (sig * (1.0 + g * (1.0 - sig)))
            du_v = dact * sil
            for k, (dbuf, du, cs) in enumerate(((dgbuf, du_g, cg), (dvbuf, du_v, cv))):
                dbuf[0:tr, :] = du
                dbuf[tr:tr + h, :] = carry[:, cs]
                carry[:, cs] = du[0:h, :]
                w = cw_ref[:, cs]
                xin = up_ref[:, cs].astype(F32)
                acc = None
                for sh in range(3):
                    dsh = dbuf[pl.ds(sh, tr), :]
                    term = w[2 - sh:3 - sh] * dsh
                    acc = term if acc is None else acc + term
                    dcw_ref[2 - sh:3 - sh, cs] += jnp.sum(xin * dsh, axis=0, keepdims=True)
                dupb = acc.astype(BF16)
                dup_ref[:, cs] = dupb
                part = jnp.dot(dupb, wut_ref[cs, :], preferred_element_type=F32)
                if j == 0 and k == 0:
                    dhacc[...] = part
                else:
                    dhacc[...] += part
        dh = dhacc[...]
        xv = x_ref[...]
        r1 = lax.rsqrt(jnp.mean(xv * xv, axis=-1, keepdims=True) + RMS_EPS)
        n1 = xv * r1
        dn1 = dh * gpre_ref[...]
        dx_ref[...] = dy + r1 * (dn1 - n1 * jnp.mean(dn1 * n1, axis=-1, keepdims=True))
        dgpre_ref[...] += jnp.sum(dh * n1, axis=0, keepdims=True)

    def rev(w):
        return pl.BlockSpec((tr, w), lambda i: (nt - 1 - i, 0))

    vec = _full((1, d))
    return pl.pallas_call(
        body, name=name, grid=(nt,),
        in_specs=[rev(d), rev(d), rev(d), rev(f2), rev(f2), vec, vec, WHOLE_VMEM, WHOLE_VMEM, _full((3, f2))],
        out_specs=[rev(d), rev(f2), rev(f), rev(d), _full((3, f2)), vec, vec],
        out_shape=[jax.ShapeDtypeStruct((s, d), F32), jax.ShapeDtypeStruct((s, f2), BF16),
                   jax.ShapeDtypeStruct((s, f), BF16), jax.ShapeDtypeStruct((s, d), BF16),
                   jax.ShapeDtypeStruct((3, f2), F32), jax.ShapeDtypeStruct((1, d), F32),
                   jax.ShapeDtypeStruct((1, d), F32)],
        scratch_shapes=[pltpu.VMEM((h, f2), F32), pltpu.VMEM((tr + h, cw), F32), pltpu.VMEM((tr + h, cw), F32),
                        pltpu.VMEM((tr, d), F32)],
        compiler_params=pltpu.CompilerParams(dimension_semantics=("arbitrary",), vmem_limit_bytes=FFN_BWD_VMEM),
    )(dxo, fout, x, up, u, g_pre, g_post, wd_t, wu_t, conv_w)


def _od_gate_fwd(z, conv_w, name):
    s, d3 = z.shape
    d = d3 // 3
    tr = min(256, s)
    h = HALO16
    cw = FFN_CW

    def body(z_ref, prev_ref, w_ref, o_ref, buf):
        first = pl.program_id(0) == 0
        for j in range(d // cw):
            cb = slice(j * cw, (j + 1) * cw)
            cc = slice(d + j * cw, d + (j + 1) * cw)
            cu = slice(2 * d + j * cw, 2 * d + (j + 1) * cw)
            buf[0:h, :] = jnp.where(first, 0.0, prev_ref[:, cc].astype(F32) * prev_ref[:, cu].astype(F32))
            buf[h:h + tr, :] = z_ref[:, cc].astype(F32) * z_ref[:, cu].astype(F32)
            k = _conv3_taps(buf, w_ref[:, cb], h - 2, tr)
            o_ref[:, cb] = (z_ref[:, cb].astype(F32) * k).astype(BF16)

    return pl.pallas_call(
        body, name=name, grid=(s // tr,),
        in_specs=[_cur(tr, d3), _prev(tr, h, d3), _full((3, d))],
        out_specs=_cur(tr, d),
        out_shape=jax.ShapeDtypeStruct((s, d), BF16),
        scratch_shapes=[pltpu.VMEM((h + tr, cw), F32)],
        compiler_params=_cparams(("parallel",)),
    )(z, z, conv_w)


def _od_gate_bwd(dy, z, conv_w, name):
    s, d3 = z.shape
    d = d3 // 3
    tr = min(256, s)
    h = HALO16
    cw = FFN_CW
    ext = tr + h

    def body(dy_ref, dyn_ref, z_ref, zp_ref, zn_ref, w_ref, o_ref, dw_ref, buf, dbuf):
        i = pl.program_id(0)
        first = i == 0
        last = i == pl.num_programs(0) - 1

        @pl.when(first)
        def _():
            dw_ref[...] = jnp.zeros_like(dw_ref)

        for j in range(d // cw):
            cb = slice(j * cw, (j + 1) * cw)
            cc = slice(d + j * cw, d + (j + 1) * cw)
            cu = slice(2 * d + j * cw, 2 * d + (j + 1) * cw)
            w = w_ref[:, cb]
            cval = z_ref[:, cc].astype(F32)
            uval = z_ref[:, cu].astype(F32)
            buf[0:h, :] = jnp.where(first, 0.0, zp_ref[:, cc].astype(F32) * zp_ref[:, cu].astype(F32))
            buf[h:h + tr, :] = cval * uval
            k = _conv3_taps(buf, w, h - 2, tr)
            dyv = dy_ref[:, cb]
            o_ref[:, cb] = (dyv * k).astype(BF16)
            dbuf[0:tr, :] = dyv * z_ref[:, cb].astype(F32)
            dbuf[tr:ext, :] = jnp.where(last, 0.0, dyn_ref[:, cb] * zn_ref[:, cb].astype(F32))
            dcu = w[2:3] * dbuf[pl.ds(0, tr), :] + w[1:2] * dbuf[pl.ds(1, tr), :] + w[0:1] * dbuf[pl.ds(2, tr), :]
            o_ref[:, cc] = (dcu * uval).astype(BF16)
            o_ref[:, cu] = (dcu * cval).astype(BF16)
            dk = dbuf[pl.ds(0, tr), :]
            for t in range(3):
                dw_ref[t:t + 1, cb] += jnp.sum(dk * buf[pl.ds(h - 2 + t, tr), :], axis=0, keepdims=True)

    return pl.pallas_call(
        body, name=name, grid=(s // tr,),
        in_specs=[_cur(tr, d), _next(tr, h, d, s), _cur(tr, d3), _prev(tr, h, d3), _next(tr, h, d3, s), _full((3, d))],
        out_specs=[_cur(tr, d3), _full((3, d))],
        out_shape=[jax.ShapeDtypeStruct((s, d3), BF16), jax.ShapeDtypeStruct((3, d), F32)],
        scratch_shapes=[pltpu.VMEM((h + tr, cw), F32), pltpu.VMEM((ext, cw), F32)],
        compiler_params=_cparams(("arbitrary",)),
    )(dy, dy, z, z, z, conv_w)


Q0 = 2 * A_CH
K0 = Q0 + N_Q_HEADS * HEAD_DIM
V0 = K0 + 2 * HEAD_DIM
EVEN_IN = V0 + 2 * HEAD_DIM


def _rope_tables(positions):
    half = ROPE_DIM // 2
    inv_freq = ROPE_THETA ** (-(jnp.arange(half, dtype=F32) * 2.0 / ROPE_DIM))
    dim = jnp.arange(3 * LANES) % HEAD_DIM
    part = jnp.arange(3 * LANES) // LANES
    freq = jnp.where(dim < ROPE_DIM, inv_freq[dim % half], 0.0)
    ang = positions.astype(F32)[:, None] * freq[None, :]
    sin = jnp.sin(ang)
    upper = (dim >= half) & (dim < ROPE_DIM)
    lower = dim < half
    return jnp.where(part == 0, jnp.cos(ang), jnp.where(part == 1, jnp.where(upper, sin, 0.0),
                                                         jnp.where(lower, -sin, 0.0)))


def _rope_fwd(x, tab):
    c, sa, sb = tab[:, 0:LANES], tab[:, LANES:2 * LANES], tab[:, 2 * LANES:3 * LANES]
    return x * c + pltpu.roll(x, 8, 1) * sa + pltpu.roll(x, LANES - 8, 1) * sb


def _rope_bwd(dy, tab):
    c, sa, sb = tab[:, 0:LANES], tab[:, LANES:2 * LANES], tab[:, 2 * LANES:3 * LANES]
    return dy * c + pltpu.roll(dy * sa, LANES - 8, 1) + pltpu.roll(dy * sb, 8, 1)


def _ln_fwd(c, g, b):
    mu = jnp.mean(c, axis=-1, keepdims=True)
    xc = c - mu
    r = lax.rsqrt(jnp.mean(xc * xc, axis=-1, keepdims=True) + LN_EPS)
    nrm = xc * r
    return nrm, r, nrm * g + b


def _ev_mid_fwd(z, tab, conv_w, conv_b, ln_g, ln_b, name):
    s = z.shape[0]
    tr = min(256, s)
    h = HALO32
    cw = LANES

    def body(z_ref, zp_ref, tab_ref, w_ref, b_ref, g_ref, lb_ref, c_ref, a_ref, qkv_ref, gbuf, cbuf):
        first = pl.program_id(0) == 0
        glu_p = zp_ref[:, 0:A_CH].astype(F32) * jax.nn.sigmoid(zp_ref[:, A_CH:2 * A_CH].astype(F32))
        gbuf[0:h, :] = jnp.where(first, 0.0, glu_p)
        gbuf[h:h + tr, :] = z_ref[:, 0:A_CH].astype(F32) * jax.nn.sigmoid(z_ref[:, A_CH:2 * A_CH].astype(F32))
        for j in range(A_CH // cw):
            cs = slice(j * cw, (j + 1) * cw)
            acc = jnp.broadcast_to(b_ref[:, cs], (tr, cw))
            for t in range(A_CONV):
                acc = acc + w_ref[t:t + 1, cs] * gbuf[pl.ds(h - (A_CONV - 1) + t, tr), cs]
            cbuf[:, cs] = acc
        c = cbuf[...]
        c_ref[...] = c.astype(BF16)
        _, _, l = _ln_fwd(c, g_ref[...], lb_ref[...])
        a_ref[...] = (l * jax.nn.sigmoid(l)).astype(BF16)
        tab_v = tab_ref[...]
        for p in range(4):
            xq = z_ref[:, Q0 + p * LANES:Q0 + (p + 1) * LANES].astype(F32)
            qkv_ref[:, p * LANES:(p + 1) * LANES] = _rope_fwd(xq, tab_v).astype(BF16)
        lane = lax.broadcasted_iota(jnp.int32, (tr, LANES), 1)
        lo = lane < HEAD_DIM
        kr = _rope_fwd(z_ref[:, K0:K0 + LANES].astype(F32), tab_v)
        vr = z_ref[:, V0:V0 + LANES].astype(F32)
        for base, val in ((4 * LANES, kr), (6 * LANES, vr)):
            sw = pltpu.roll(val, HEAD_DIM, 1)
            qkv_ref[:, base:base + LANES] = jnp.where(lo, val, sw).astype(BF16)
            qkv_ref[:, base + LANES:base + 2 * LANES] = jnp.where(lo, sw, val).astype(BF16)

    return pl.pallas_call(
        body, name=name, grid=(s // tr,),
        in_specs=[_cur(tr, EVEN_IN), _prev(tr, h, 2 * A_CH), _cur(tr, 3 * LANES), _full((A_CONV, A_CH)),
                  _full((1, A_CH)), _full((1, A_CH)), _full((1, A_CH))],
        out_specs=[_cur(tr, A_CH), _cur(tr, A_CH), _cur(tr, 2 * A_CH)],
        out_shape=[jax.ShapeDtypeStruct((s, A_CH), BF16), jax.ShapeDtypeStruct((s, A_CH), BF16),
                   jax.ShapeDtypeStruct((s, 2 * A_CH), BF16)],
        scratch_shapes=[pltpu.VMEM((h + tr, A_CH), F32), pltpu.VMEM((tr, A_CH), F32)],
        compiler_params=_cparams(("parallel",)),
    )(z, z, tab, conv_w, conv_b, ln_g, ln_b)


def _ev_mid_bwd(dcat, c, z, dq, dkv, tab, conv_w, ln_g, ln_b, name):
    s = z.shape[0]
    tr = min(256, s)
    h = HALO32
    cw = LANES
    ext = tr + h

    def body(da_ref, dan_ref, c_ref, cn_ref, z_ref, zp_ref, dq_ref, dkv_ref, tab_ref, w_ref, g_ref, lb_ref,
             dz_ref, dw_ref, dvec_ref, gbuf, dcbuf):
        i = pl.program_id(0)
        first = i == 0
        last = i == pl.num_programs(0) - 1

        @pl.when(first)
        def _():
            dw_ref[...] = jnp.zeros_like(dw_ref)
            dvec_ref[...] = jnp.zeros_like(dvec_ref)

        gv = g_ref[...]

        def ln_silu_bwd(cv, dav):
            nrm, r, l = _ln_fwd(cv, gv, lb_ref[...])
            sig = jax.nn.sigmoid(l)
            dl = dav * (sig * (1.0 + l * (1.0 - sig)))
            dn = dl * gv
            dc = r * (dn - jnp.mean(dn, axis=-1, keepdims=True) - nrm * jnp.mean(dn * nrm, axis=-1, keepdims=True))
            return dc, dl, nrm

        dc, dl, nrm = ln_silu_bwd(c_ref[...].astype(F32), da_ref[...])
        dcn, _, _ = ln_silu_bwd(cn_ref[...].astype(F32), dan_ref[...])
        dcbuf[0:tr, :] = dc
        dcbuf[tr:ext, :] = jnp.where(last, 0.0, dcn)
        dvec_ref[0:1, :] += jnp.sum(dc, axis=0, keepdims=True)
        dvec_ref[1:2, :] += jnp.sum(dl * nrm, axis=0, keepdims=True)
        dvec_ref[2:3, :] += jnp.sum(dl, axis=0, keepdims=True)

        a_lin = z_ref[:, 0:A_CH].astype(F32)
        sig_g = jax.nn.sigmoid(z_ref[:, A_CH:2 * A_CH].astype(F32))
        glu_p = zp_ref[:, 0:A_CH].astype(F32) * jax.nn.sigmoid(zp_ref[:, A_CH:2 * A_CH].astype(F32))
        gbuf[0:h, :] = jnp.where(first, 0.0, glu_p)
        gbuf[h:h + tr, :] = a_lin * sig_g
        for j in range(A_CH // cw):
            cs = slice(j * cw, (j + 1) * cw)
            dcj = dcbuf[pl.ds(0, tr), cs]
            acc = jnp.zeros((tr, cw), F32)
            for t in range(A_CONV):
                acc = acc + w_ref[t:t + 1, cs] * dcbuf[pl.ds(A_CONV - 1 - t, tr), cs]
                dw_ref[t:t + 1, cs] += jnp.sum(dcj * gbuf[pl.ds(h - (A_CONV - 1) + t, tr), cs], axis=0, keepdims=True)
            dz_ref[:, cs] = (acc * sig_g[:, cs]).astype(BF16)
            dz_ref[:, A_CH + j * cw:A_CH + (j + 1) * cw] = (
                acc * a_lin[:, cs] * sig_g[:, cs] * (1.0 - sig_g[:, cs])).astype(BF16)

        tab_v = tab_ref[...]
        for p in range(4):
            cs = slice(p * LANES, (p + 1) * LANES)
            dz_ref[:, Q0 + p * LANES:Q0 + (p + 1) * LANES] = _rope_bwd(dq_ref[:, cs], tab_v).astype(BF16)
        lane = lax.broadcasted_iota(jnp.int32, (tr, LANES), 1)
        lo = lane < HEAD_DIM

        def fold(base):
            p0 = dkv_ref[:, base:base + LANES]
            p1 = dkv_ref[:, base + LANES:base + 2 * LANES]
            s0 = p0 + pltpu.roll(p0, HEAD_DIM, 1)
            s1 = p1 + pltpu.roll(p1, HEAD_DIM, 1)
            return jnp.where(lo, s0, s1)

        dz_ref[:, K0:K0 + LANES] = _rope_bwd(fold(0), tab_v).astype(BF16)
        dz_ref[:, V0:V0 + LANES] = fold(2 * LANES).astype(BF16)

    return pl.pallas_call(
        body, name=name, grid=(s // tr,),
        in_specs=[_cur(tr, A_CH), _next(tr, h, A_CH, s), _cur(tr, A_CH), _next(tr, h, A_CH, s),
                  _cur(tr, EVEN_IN), _prev(tr, h, 2 * A_CH), _cur(tr, A_CH), _cur(tr, A_CH), _cur(tr, 3 * LANES),
                  _full((A_CONV, A_CH)), _full((1, A_CH)), _full((1, A_CH))],
        out_specs=[_cur(tr, EVEN_IN), _full((A_CONV, A_CH)), _full((8, A_CH))],
        out_shape=[jax.ShapeDtypeStruct((s, EVEN_IN), BF16), jax.ShapeDtypeStruct((A_CONV, A_CH), F32),
                   jax.ShapeDtypeStruct((8, A_CH), F32)],
        scratch_shapes=[pltpu.VMEM((h + tr, A_CH), F32), pltpu.VMEM((ext, A_CH), F32)],
        compiler_params=_cparams(("arbitrary",)),
    )(dcat, dcat, c, c, z, z, dq, dkv, tab, conv_w, ln_g, ln_b)


NT = (((1,), (1,)), ((), ()))
TN = (((0,), (0,)), ((), ()))
QB = WINDOW
SCALE = HEAD_DIM ** -0.5


def _att_probs(q2m, kwin, sink, mask):
    sc = lax.dot_general(q2m, kwin, NT, preferred_element_type=F32) * SCALE
    sc = jnp.where(mask, sc, -jnp.inf)
    mx = jnp.maximum(jnp.max(sc, axis=-1, keepdims=True), sink)
    p = jnp.exp(sc - mx)
    ps = jnp.exp(sink - mx)
    inv = 1.0 / (jnp.sum(p, axis=-1, keepdims=True) + ps)
    return p * inv, ps * inv


def _att_mask(i):
    r = lax.broadcasted_iota(jnp.int32, (QB, 2 * QB), 0)
    kc = lax.broadcasted_iota(jnp.int32, (QB, 2 * QB), 1)
    diff = r + QB - kc
    return (diff >= 0) & (diff < WINDOW) & ((kc >= QB) | (i > 0))


def _half_masks(dtype):
    lane = lax.broadcasted_iota(jnp.int32, (1, LANES), 1)
    return (lane < HEAD_DIM).astype(dtype), (lane >= HEAD_DIM).astype(dtype)


def _att_fwd(qkv, a, sinks, name):
    s = qkv.shape[0]
    nb = s // QB

    def body(sink_ref, qkv_ref, kvp_ref, a_ref, o_ref):
        i = pl.program_id(0)
        mask = _att_mask(i)
        mlo, mhi = _half_masks(BF16)
        o_ref[:, 0:A_CH] = a_ref[...]
        for p in range(4):
            g = p // 2
            q2 = qkv_ref[:, p * LANES:(p + 1) * LANES]
            kwin = jnp.concatenate([kvp_ref[:, g * LANES:(g + 1) * LANES],
                                    qkv_ref[:, A_CH + g * LANES:A_CH + (g + 1) * LANES]], axis=0)
            vwin = jnp.concatenate([kvp_ref[:, (2 + g) * LANES:(3 + g) * LANES],
                                    qkv_ref[:, A_CH + (2 + g) * LANES:A_CH + (3 + g) * LANES]], axis=0)
            pe, _ = _att_probs(q2 * mlo, kwin, sink_ref[2 * p], mask)
            po, _ = _att_probs(q2 * mhi, kwin, sink_ref[2 * p + 1], mask)
            o = (jnp.dot(pe.astype(BF16), vwin * mlo, preferred_element_type=F32)
                 + jnp.dot(po.astype(BF16), vwin * mhi, preferred_element_type=F32))
            o_ref[:, A_CH + p * LANES:A_CH + (p + 1) * LANES] = o.astype(BF16)

    grid_spec = pltpu.PrefetchScalarGridSpec(
        num_scalar_prefetch=1, grid=(nb,),
        in_specs=[pl.BlockSpec((QB, 2 * A_CH), lambda i, sk: (i, 0)),
                  pl.BlockSpec((QB, A_CH), lambda i, sk: (jnp.maximum(i - 1, 0), 1)),
                  pl.BlockSpec((QB, A_CH), lambda i, sk: (i, 0))],
        out_specs=pl.BlockSpec((QB, 2 * A_CH), lambda i, sk: (i, 0)),
    )
    return pl.pallas_call(
        body, name=name, grid_spec=grid_spec,
        out_shape=jax.ShapeDtypeStruct((s, 2 * A_CH), BF16),
        compiler_params=_cparams(("parallel",)),
    )(sinks, qkv, qkv, a)


def _att_bwd(qkv, dcat, sinks, name):
    s = qkv.shape[0]
    nb = s // QB

    def body(sink_ref, qkv_ref, kvp_ref, do_ref, dq_ref, dkv_ref, ds_ref, carry):
        i = pl.program_id(0)

        @pl.when(i == 0)
        def _():
            ds_ref[...] = jnp.zeros_like(ds_ref)
            carry[...] = jnp.zeros_like(carry)

        @pl.when(i < nb)
        def _():
            mask = _att_mask(i)
            mlo, mhi = _half_masks(BF16)
            dwin = [jnp.zeros((2 * QB, LANES), F32) for _ in range(4)]
            for p in range(4):
                g = p // 2
                q2 = qkv_ref[:, p * LANES:(p + 1) * LANES]
                kwin = jnp.concatenate([kvp_ref[:, g * LANES:(g + 1) * LANES],
                                        qkv_ref[:, A_CH + g * LANES:A_CH + (g + 1) * LANES]], axis=0)
                vwin = jnp.concatenate([kvp_ref[:, (2 + g) * LANES:(3 + g) * LANES],
                                        qkv_ref[:, A_CH + (2 + g) * LANES:A_CH + (3 + g) * LANES]], axis=0)
                do2 = do_ref[:, p * LANES:(p + 1) * LANES].astype(BF16)
                dq2 = jnp.zeros((QB, LANES), F32)
                for e, hm in enumerate((mlo, mhi)):
                    qm = q2 * hm
                    dom = do2 * hm
                    prob, psink = _att_probs(qm, kwin, sink_ref[2 * p + e], mask)
                    dp = lax.dot_general(dom, vwin, NT, preferred_element_type=F32)
                    delta = jnp.sum(prob * dp, axis=-1, keepdims=True)
                    dsc = (prob * (dp - delta) * SCALE).astype(BF16)
                    ds_ref[2 * p + e:2 * p + e + 1, :] += jnp.broadcast_to(
                        jnp.sum(-psink * delta, axis=0, keepdims=True), (1, LANES))
                    dq2 = dq2 + jnp.dot(dsc, kwin * hm, preferred_element_type=F32)
                    dwin[g] = dwin[g] + lax.dot_general(dsc, qm, TN, preferred_element_type=F32)
                    dwin[2 + g] = dwin[2 + g] + lax.dot_general(prob.astype(BF16), dom, TN,
                                                                preferred_element_type=F32)
                dq_ref[:, p * LANES:(p + 1) * LANES] = dq2
            for n in range(4):
                cs = slice(n * LANES, (n + 1) * LANES)
                dkv_ref[:, cs] = carry[:, cs] + dwin[n][0:QB, :]
                carry[:, cs] = dwin[n][QB:2 * QB, :]

        @pl.when(i == nb)
        def _():
            dkv_ref[...] = carry[...]

    grid_spec = pltpu.PrefetchScalarGridSpec(
        num_scalar_prefetch=1, grid=(nb + 1,),
        in_specs=[pl.BlockSpec((QB, 2 * A_CH), lambda i, sk: (jnp.minimum(i, nb - 1), 0)),
                  pl.BlockSpec((QB, A_CH), lambda i, sk: (jnp.maximum(jnp.minimum(i, nb - 1) - 1, 0), 1)),
                  pl.BlockSpec((QB, A_CH), lambda i, sk: (jnp.minimum(i, nb - 1), 1))],
        out_specs=[pl.BlockSpec((QB, A_CH), lambda i, sk: (jnp.minimum(i, nb - 1), 0)),
                   pl.BlockSpec((QB, A_CH), lambda i, sk: (jnp.maximum(i - 1, 0), 0)),
                   pl.BlockSpec((8, LANES), lambda i, sk: (0, 0))],
        scratch_shapes=[pltpu.VMEM((QB, A_CH), F32)],
    )
    return pl.pallas_call(
        body, name=name, grid_spec=grid_spec,
        out_shape=[jax.ShapeDtypeStruct((s, A_CH), F32), jax.ShapeDtypeStruct((s, A_CH), F32),
                   jax.ShapeDtypeStruct((8, LANES), F32)],
        compiler_params=_cparams(("arbitrary",)),
    )(sinks, qkv, qkv, dcat)


def _local_step(x, positions, target, w):
    row = lambda a, i: a[i:i + 1]
    tab = _rope_tables(positions)
    g = {}

    def ffn_fwd(xin, i):
        xout, f, h, up, u = _ffn_fwd(xin, row(w["ffn_norm_pre"], i), w["ffn_w_up"][i], w["ffn_conv_w"][i],
                                     w["ffn_w_down"][i], row(w["ffn_norm_post"], i), f"ffn{i}_fwd")
        return xout, (xin, f, h, up, u)

    def ffn_bwd(dxout, saved, i):
        xin, f, h, up, u = saved
        dxin, dup, act, df, d_cw, dg_post, dg_pre = _ffn_bwd(
            dxout, f, xin, up, u, row(w["ffn_norm_pre"], i), row(w["ffn_norm_post"], i), w["ffn_w_down_t"][i],
            w["ffn_w_up_t"][i], w["ffn_conv_w"][i], f"ffn{i}_bwd")
        d_down = _mm_tn(act, df, f"ffn{i}_down_dw")
        d_up = _mm_tn(h, dup, f"ffn{i}_up_dw")
        return dxin, dict(ffn_norm_post=dg_post, ffn_norm_pre=dg_pre, ffn_w_up=d_up, ffn_conv_w=d_cw, ffn_w_down=d_down)

    h0 = _rms_fwd(x, row(w["mix_norm_pre"], 0), "ev_pre")
    z0 = _mm(h0, w["ev_w_in"], BF16, "ev_in")
    c0, a0, qkv = _ev_mid_fwd(z0, tab, w["ev_a_conv_w"], w["ev_a_conv_b"], w["ev_a_ln_g"], w["ev_a_ln_b"], "ev_mid")
    cat = _att_fwd(qkv, a0, w["ev_sinks"], "ev_att")
    m0 = _mm(cat, w["ev_w_out"], F32, "ev_out")
    x1 = _post_fwd(m0, row(w["mix_norm_post"], 0), x, "ev_post")
    x2, ffn0 = ffn_fwd(x1, 0)
    h2 = _rms_fwd(x2, row(w["mix_norm_pre"], 1), "od_pre")
    z1 = _mm(h2, w["od_w_in"], BF16, "od_in")
    y1 = _od_gate_fwd(z1, w["od_conv_w"], "od_mid")
    m1 = _mm(y1, w["od_w_out"], F32, "od_out")
    x3 = _post_fwd(m1, row(w["mix_norm_post"], 1), x2, "od_post")
    x4, ffn1 = ffn_fwd(x3, 1)

    sq, dx4 = _loss_bwd(x4, target, "loss")

    dx3, gf1 = ffn_bwd(dx4, ffn1, 1)
    dm1, dg_mo1 = _norm_bwd(dx3, m1, row(w["mix_norm_post"], 1), None, BF16, "od_post_bwd")
    g["od_w_out"] = _mm_tn(y1, dm1, "od_out_dw")
    dy1 = _mm(dm1, w["od_w_out_t"], F32, "od_out_dx")
    dz1, g["od_conv_w"] = _od_gate_bwd(dy1, z1, w["od_conv_w"], "od_mid_bwd")
    g["od_w_in"] = _mm_tn(h2, dz1, "od_in_dw")
    dh2 = _mm(dz1, w["od_w_in_t"], F32, "od_in_dx")
    dx2, dg_mp1 = _norm_bwd(dh2, x2, row(w["mix_norm_pre"], 1), dx3, F32, "od_pre_bwd")

    dx1, gf0 = ffn_bwd(dx2, ffn0, 0)
    dm0, dg_mo0 = _norm_bwd(dx1, m0, row(w["mix_norm_post"], 0), None, BF16, "ev_post_bwd")
    g["ev_w_out"] = _mm_tn(cat, dm0, "ev_out_dw")
    dcat = _mm(dm0, w["ev_w_out_t"], F32, "ev_out_dx")
    dq, dkv, dsk = _att_bwd(qkv, dcat, w["ev_sinks"], "ev_att_bwd")
    dz0, g["ev_a_conv_w"], dvec = _ev_mid_bwd(dcat, c0, z0, dq, dkv, tab, w["ev_a_conv_w"], w["ev_a_ln_g"],
                                              w["ev_a_ln_b"], "ev_mid_bwd")
    g["ev_w_in"] = _mm_tn(h0, dz0, "ev_in_dw")
    dh0 = _mm(dz0, w["ev_w_in_t"], F32, "ev_in_dx")
    dx0, dg_mp0 = _norm_bwd(dh0, x, row(w["mix_norm_pre"], 0), dx1, F32, "ev_pre_bwd")

    g["ev_a_conv_b"] = dvec[0:1]
    g["ev_a_ln_g"] = dvec[1:2]
    g["ev_a_ln_b"] = dvec[2:3]
    g["ev_sinks"] = dsk[:, 0]
    g["mix_norm_pre"] = jnp.concatenate([dg_mp0, dg_mp1], axis=0)
    g["mix_norm_post"] = jnp.concatenate([dg_mo0, dg_mo1], axis=0)
    for k in ("ffn_norm_pre", "ffn_norm_post", "ffn_w_up", "ffn_conv_w", "ffn_w_down"):
        g[k] = jnp.stack([gf0[k], gf1[k]], axis=0) if gf0[k].shape[0] != 1 else jnp.concatenate([gf0[k], gf1[k]], axis=0)
    return sq, dx0, g


ANY = pl.BlockSpec(memory_space=pl.ANY)
PACK_COLS = 1024


def _me():
    return lax.axis_index("x"), lax.axis_index("y"), lax.axis_index("c")


def _other_chips(x, y):
    return [(1 - x, y), (x, 1 - y), (1 - x, 1 - y)]


def _remote(src, dst, send, recv, dev):
    return pltpu.make_async_remote_copy(src_ref=src, dst_ref=dst, send_sem=send, recv_sem=recv,
                                        device_id=dev, device_id_type=MESH)


def _gather_chips(wp, name):
    r, cols = wp.shape
    rh = r // 2

    def body(w_ref, o_ref, send, recv, lsem):
        x, y, c = _me()
        p = 2 * x + y
        sib = (x, y, 1 - c)
        chips = _other_chips(x, y)
        half = pl.ds(c * rh, rh)
        other = pl.ds((1 - c) * rh, rh)
        mine = pltpu.make_async_copy(w_ref, o_ref.at[p], lsem)
        mine.start()
        sent = [_remote(w_ref.at[half], o_ref.at[p, half], send.at[k], recv.at[k], (cx, cy, c))
                for k, (cx, cy) in enumerate(chips)]
        for cp in sent:
            cp.start()
        for k, (cx, cy) in enumerate(chips):
            q = 2 * cx + cy
            _remote(w_ref.at[half], o_ref.at[q, half], send.at[k], recv.at[k], (cx, cy, c)).wait_recv()
            fwd = _remote(o_ref.at[q, half], o_ref.at[q, half], send.at[3 + k], recv.at[3 + k], sib)
            fwd.start()
            sent.append(fwd)
        for k, (cx, cy) in enumerate(chips):
            q = 2 * cx + cy
            _remote(o_ref.at[q, other], o_ref.at[q, other], send.at[3 + k], recv.at[3 + k], sib).wait_recv()
        for cp in sent:
            cp.wait_send()
        mine.wait()

    return pl.pallas_call(
        body, name=name, in_specs=[ANY], out_specs=ANY,
        out_shape=jax.ShapeDtypeStruct((N_CHIPS, r, cols), wp.dtype),
        scratch_shapes=[pltpu.SemaphoreType.DMA((6,)), pltpu.SemaphoreType.DMA((6,)), pltpu.SemaphoreType.DMA],
    )(wp)


def _exchange8(v, reduce, name):
    r, cols = v.shape
    rel = [(a, b, d) for a in (0, 1) for b in (0, 1) for d in (0, 1) if (a, b, d) != (0, 0, 0)]

    def body(v_ref, o_ref, *rest):
        if reduce:
            gbuf, send, recv = rest
        else:
            gbuf = o_ref
            send, recv = rest
        x, y, c = _me()
        me = 4 * x + 2 * y + c
        gbuf[me] = v_ref[...]
        sent = []
        for k, (a, b, d) in enumerate(rel):
            cp = _remote(v_ref, gbuf.at[me], send.at[k], recv.at[k], ((x + a) % 2, (y + b) % 2, (c + d) % 2))
            cp.start()
            sent.append(cp)
        for k, (a, b, d) in enumerate(rel):
            src = 4 * ((x + a) % 2) + 2 * ((y + b) % 2) + (c + d) % 2
            _remote(v_ref, gbuf.at[src], send.at[k], recv.at[k], (x, y, c)).wait_recv()
        for cp in sent:
            cp.wait_send()
        if reduce:
            acc = gbuf[0]
            for n in range(1, 8):
                acc = acc + gbuf[n]
            o_ref[...] = acc

    vmem = pl.BlockSpec(memory_space=pltpu.VMEM)
    sems = [pltpu.SemaphoreType.DMA((7,)), pltpu.SemaphoreType.DMA((7,))]
    if reduce:
        out_shape = jax.ShapeDtypeStruct((r, cols), F32)
        scratch = [pltpu.VMEM((8, r, cols), F32)] + sems
    else:
        out_shape = jax.ShapeDtypeStruct((8, r, cols), F32)
        scratch = sems
    return pl.pallas_call(body, name=name, in_specs=[vmem], out_specs=vmem, out_shape=out_shape,
                          scratch_shapes=scratch)(v)


def _rs_swap(g, name):
    _, _, rh, cols = g.shape

    def body(g_ref, o_ref, send, recv):
        x, y, c = _me()
        cps = [_remote(g_ref.at[q, 1 - c], o_ref.at[q], send.at[q], recv.at[q], (x, y, 1 - c)) for q in range(N_CHIPS)]
        for cp in cps:
            cp.start()
        for cp in cps:
            cp.wait()

    return pl.pallas_call(
        body, name=name, in_specs=[ANY], out_specs=ANY,
        out_shape=jax.ShapeDtypeStruct((N_CHIPS, rh, cols), F32),
        scratch_shapes=[pltpu.SemaphoreType.DMA((N_CHIPS,)), pltpu.SemaphoreType.DMA((N_CHIPS,))],
    )(g)


def _row_tile(rows, pref):
    if rows <= pref:
        return rows
    t = (pref // 8) * 8
    while t >= 8:
        if rows % t == 0:
            return t
        t -= 8
    return rows


def _rs_add(g, sib, c, name):
    _, _, rh, cols = g.shape
    tr = _row_tile(rh, 512)

    def body(c_ref, g_ref, s_ref, o_ref):
        o_ref[...] = (g_ref[...] + s_ref[...]).astype(BF16)

    grid_spec = pltpu.PrefetchScalarGridSpec(
        num_scalar_prefetch=1, grid=(N_CHIPS, rh // tr),
        in_specs=[pl.BlockSpec((None, None, tr, cols), lambda q, i, cr: (q, cr[0], i, 0)),
                  pl.BlockSpec((None, tr, cols), lambda q, i, cr: (q, i, 0))],
        out_specs=pl.BlockSpec((None, tr, cols), lambda q, i, cr: (q, i, 0)),
    )
    return pl.pallas_call(
        body, name=name, grid_spec=grid_spec,
        out_shape=jax.ShapeDtypeStruct((N_CHIPS, rh, cols), BF16),
        compiler_params=_cparams(("parallel", "parallel")),
    )(c, g, sib)


def _rs_ici(a, name):
    _, rh, cols = a.shape

    def body(a_ref, o_ref, send, recv, lsem):
        x, y, c = _me()
        p = 2 * x + y
        mine = pltpu.make_async_copy(a_ref.at[p], o_ref.at[p], lsem)
        mine.start()
        cps = []
        for k, (cx, cy) in enumerate(_other_chips(x, y)):
            cp = _remote(a_ref.at[2 * cx + cy], o_ref.at[p], send.at[k], recv.at[k], (cx, cy, c))
            cp.start()
            cps.append(cp)
        for k, (cx, cy) in enumerate(_other_chips(x, y)):
            q = 2 * cx + cy
            _remote(a_ref.at[q], o_ref.at[q], send.at[k], recv.at[k], (cx, cy, c)).wait_recv()
        for cp in cps:
            cp.wait_send()
        mine.wait()

    return pl.pallas_call(
        body, name=name, in_specs=[ANY], out_specs=ANY,
        out_shape=jax.ShapeDtypeStruct((N_CHIPS, rh, cols), a.dtype),
        scratch_shapes=[pltpu.SemaphoreType.DMA((3,)), pltpu.SemaphoreType.DMA((3,)), pltpu.SemaphoreType.DMA],
    )(a)


def _rs_sum(rb, name):
    _, rh, cols = rb.shape
    tr = _row_tile(rh, 512)

    def body(r0, r1, r2, r3, o_ref):
        o_ref[...] = ((r0[...].astype(F32) + r1[...].astype(F32)) + r2[...].astype(F32)) + r3[...].astype(F32)

    def spec(q):
        return pl.BlockSpec((None, tr, cols), lambda i: (q, i, 0))

    return pl.pallas_call(
        body, name=name, grid=(rh // tr,),
        in_specs=[spec(0), spec(1), spec(2), spec(3)],
        out_specs=pl.BlockSpec((tr, cols), lambda i: (i, 0)),
        out_shape=jax.ShapeDtypeStruct((rh, cols), F32),
        compiler_params=_cparams(("parallel",)),
    )(rb, rb, rb, rb)


def _rs_share(hsum, name):
    rh, cols = hsum.shape

    def body(h_ref, o_ref, send, recv, lsem):
        x, y, c = _me()
        mine = pltpu.make_async_copy(h_ref, o_ref.at[c], lsem)
        mine.start()
        cp = _remote(h_ref, o_ref.at[c], send, recv, (x, y, 1 - c))
        cp.start()
        _remote(h_ref, o_ref.at[1 - c], send, recv, (x, y, 1 - c)).wait_recv()
        cp.wait_send()
        mine.wait()

    return pl.pallas_call(
        body, name=name, in_specs=[ANY], out_specs=ANY,
        out_shape=jax.ShapeDtypeStruct((2, rh, cols), F32),
        scratch_shapes=[pltpu.SemaphoreType.DMA, pltpu.SemaphoreType.DMA, pltpu.SemaphoreType.DMA],
    )(hsum)


def _adamw(w, g, m, v, name):
    rows, cols = w.shape
    tr = _row_tile(rows, 512)

    def body(w_ref, g_ref, m_ref, v_ref, d_ref, nm_ref, nv_ref):
        gv = g_ref[...]
        nm = ADAM_B1 * m_ref[...] + (1.0 - ADAM_B1) * gv
        nv = ADAM_B2 * v_ref[...] + (1.0 - ADAM_B2) * (gv * gv)
        m_hat = nm / (1.0 - ADAM_B1 ** ADAM_STEP)
        v_hat = nv / (1.0 - ADAM_B2 ** ADAM_STEP)
        d_ref[...] = -ADAM_LR * (m_hat / (jnp.sqrt(v_hat) + ADAM_EPS) + ADAM_WD * w_ref[...])
        nm_ref[...] = nm
        nv_ref[...] = nv

    spec = pl.BlockSpec((tr, cols), lambda i: (i, 0))
    shp = jax.ShapeDtypeStruct((rows, cols), F32)
    return pl.pallas_call(
        body, name=name, grid=(rows // tr,), in_specs=[spec] * 4, out_specs=[spec] * 3, out_shape=[shp] * 3,
        compiler_params=_cparams(("parallel",)),
    )(w, g, m, v)


WEIGHTS = ("mix_norm_pre", "mix_norm_post", "ffn_norm_pre", "ffn_norm_post", "ev_w_in", "ev_a_conv_w", "ev_a_conv_b",
           "ev_a_ln_g", "ev_a_ln_b", "ev_sinks", "ev_w_out", "od_w_in", "od_conv_w", "od_w_out", "ffn_w_up",
           "ffn_conv_w", "ffn_w_down")
MATS = (("ev_w_in", 2), ("ev_w_out", 1), ("od_w_in", 2), ("od_w_out", 1), ("ffn_w_up", 2), ("ffn_w_down", 1))
SMALL_SHARDED = ("ev_a_conv_w", "od_conv_w", "ffn_conv_w")
REPLICATED = ("mix_norm_pre", "mix_norm_post", "ffn_norm_pre", "ffn_norm_post", "ev_a_conv_b", "ev_a_ln_g",
              "ev_a_ln_b", "ev_sinks")


def _pack(parts, rows_multiple):
    flat = jnp.concatenate([p.reshape(-1) for p in parts])
    unit = rows_multiple * PACK_COLS
    pad = (-flat.shape[0]) % unit
    if pad:
        flat = jnp.concatenate([flat, jnp.zeros((pad,), flat.dtype)])
    return flat.reshape(-1, PACK_COLS)


def _unpack(buf, shapes):
    flat = buf.reshape(-1)
    out, off = [], 0
    for shp in shapes:
        n = 1
        for d in shp:
            n *= d
        out.append(flat[off:off + n].reshape(shp))
        off += n
    return out


def _shards_of(full, axis):
    n = full.shape[axis] // N_CHIPS
    return [lax.slice_in_dim(full, q * n, (q + 1) * n, axis=axis) for q in range(N_CHIPS)]


def kernel(x, positions, mix_norm_pre, mix_norm_post, ffn_norm_pre, ffn_norm_post, ev_w_in, ev_a_conv_w, ev_a_conv_b, ev_a_ln_g, ev_a_ln_b, ev_sinks, ev_w_out, od_w_in, od_conv_w, od_w_out, ffn_w_up, ffn_conv_w, ffn_w_down, loss_target, m_mix_norm_pre, m_mix_norm_post, m_ffn_norm_pre, m_ffn_norm_post, m_ev_w_in, m_ev_a_conv_w, m_ev_a_conv_b, m_ev_a_ln_g, m_ev_a_ln_b, m_ev_sinks, m_ev_w_out, m_od_w_in, m_od_conv_w, m_od_w_out, m_ffn_w_up, m_ffn_conv_w, m_ffn_w_down, v_mix_norm_pre, v_mix_norm_post, v_ffn_norm_pre, v_ffn_norm_post, v_ev_w_in, v_ev_a_conv_w, v_ev_a_conv_b, v_ev_a_ln_g, v_ev_a_ln_b, v_ev_sinks, v_ev_w_out, v_od_w_in, v_od_conv_w, v_od_w_out, v_ffn_w_up, v_ffn_conv_w, v_ffn_w_down):
    wts = dict(zip(WEIGHTS, (mix_norm_pre, mix_norm_post, ffn_norm_pre, ffn_norm_post, ev_w_in, ev_a_conv_w, ev_a_conv_b,
                             ev_a_ln_g, ev_a_ln_b, ev_sinks, ev_w_out, od_w_in, od_conv_w, od_w_out, ffn_w_up, ffn_conv_w,
                             ffn_w_down)))
    mom = dict(zip(WEIGHTS, (m_mix_norm_pre, m_mix_norm_post, m_ffn_norm_pre, m_ffn_norm_post, m_ev_w_in, m_ev_a_conv_w,
                             m_ev_a_conv_b, m_ev_a_ln_g, m_ev_a_ln_b, m_ev_sinks, m_ev_w_out, m_od_w_in, m_od_conv_w,
                             m_od_w_out, m_ffn_w_up, m_ffn_conv_w, m_ffn_w_down)))
    var = dict(zip(WEIGHTS, (v_mix_norm_pre, v_mix_norm_post, v_ffn_norm_pre, v_ffn_norm_post, v_ev_w_in, v_ev_a_conv_w,
                             v_ev_a_conv_b, v_ev_a_ln_g, v_ev_a_ln_b, v_ev_sinks, v_ev_w_out, v_od_w_in, v_od_conv_w,
                             v_od_w_out, v_ffn_w_up, v_ffn_conv_w, v_ffn_w_down)))
    xi, yi, ci = _me()
    chip = 2 * xi + yi

    mat_shapes = [wts[k].shape for k, _ in MATS]
    gathered = _gather_chips(_pack([wts[k].astype(BF16) for k, _ in MATS], 32), "gather_mats")
    small_shapes = [wts[k].shape for k in SMALL_SHARDED]
    small_all = _exchange8(_pack([wts[k] for k in SMALL_SHARDED], 8), False, "gather_small")
    w = {k: wts[k] for k in REPLICATED}
    per_chip = [_unpack(gathered[q], mat_shapes) for q in range(N_CHIPS)]
    for n, (k, axis) in enumerate(MATS):
        w[k] = jnp.concatenate([per_chip[q][n] for q in range(N_CHIPS)], axis=axis)
    per_chip = [_unpack(small_all[2 * q], small_shapes) for q in range(N_CHIPS)]
    for n, k in enumerate(SMALL_SHARDED):
        w[k] = jnp.concatenate([per_chip[q][n] for q in range(N_CHIPS)], axis=-1)
    for k in ("ev_w_in", "ev_w_out", "od_w_in", "od_w_out"):
        w[k] = w[k][0]
        w[k + "_t"] = w[k].T
    for k in ("ffn_w_up", "ffn_w_down"):
        w[k + "_t"] = jnp.swapaxes(w[k], 1, 2)
    for k in ("ev_a_conv_w", "od_conv_w"):
        w[k] = w[k][0]
    w["ev_sinks"] = w["ev_sinks"][0]

    sq, dx, g = _local_step(x[0], positions[0], loss_target[0], w)
    loss = lax.psum(0.5 * jnp.sum(sq) / D_MODEL, ("x", "y", "c"))

    shards = {k: _shards_of(g[k].reshape(wts[k].shape[:-2] + g[k].shape[-2:]), axis) for k, axis in MATS}
    gp = jnp.stack([_pack([shards[k][q] for k, _ in MATS], 32) for q in range(N_CHIPS)], axis=0)
    rows = gp.shape[1]
    gp = gp.reshape(N_CHIPS, 2, rows // 2, PACK_COLS)
    sib = _rs_swap(gp, "rs_swap")
    pair = _rs_add(gp, sib, jnp.reshape(ci, (1,)).astype(jnp.int32), "rs_add")
    landed = _rs_ici(pair, "rs_ici")
    half = _rs_sum(landed, "rs_sum")
    red = _rs_share(half, "rs_share").reshape(rows, PACK_COLS)
    grads = dict(zip([k for k, _ in MATS], _unpack(red, mat_shapes)))

    small_keys = REPLICATED + SMALL_SHARDED
    full_shapes = [wts[k].shape for k in REPLICATED] + [wts[k].shape[:-1] + (wts[k].shape[-1] * N_CHIPS,) for k in SMALL_SHARDED]
    sm = _exchange8(_pack([g[k] for k in small_keys], 8), True, "reduce_small")
    for k, full in zip(small_keys, _unpack(sm, full_shapes)):
        if k in SMALL_SHARDED:
            n = wts[k].shape[-1]
            full = lax.dynamic_slice_in_dim(full, chip * n, n, axis=full.ndim - 1)
        grads[k] = full

    deltas, new_m, new_v = {}, {}, {}
    for k in WEIGHTS:
        shp = wts[k].shape
        two_d = (-1, shp[-1])
        d, nm, nv = _adamw(wts[k].reshape(two_d), grads[k].reshape(two_d), mom[k].reshape(two_d), var[k].reshape(two_d),
                           "adamw_" + k)
        deltas[k], new_m[k], new_v[k] = d.reshape(shp), nm.reshape(shp), nv.reshape(shp)

    return (loss, dx[None], *[grads[k] for k in WEIGHTS], *[deltas[k] for k in WEIGHTS],
            *[new_m[k] for k in WEIGHTS], *[new_v[k] for k in WEIGHTS])
```

```python
import functools

import jax
import jax.numpy as jnp
import numpy as np
from jax import lax
from jax.experimental import pallas as pl
from jax.experimental.pallas import tpu as pltpu

F32 = jnp.float32
BF16 = jnp.bfloat16
MESH = pl.DeviceIdType.MESH

D_MODEL = 1024
HEAD_DIM = 64
A_CH = 512
A_CONV = 31
N_Q_HEADS = 8
WINDOW = 128
ROPE_THETA = 500000.0
ROPE_DIM = 16
D_FF = 2816
RMS_EPS = 1e-6
LN_EPS = 1e-5
ADAM_LR = 0.001
ADAM_B1 = 0.9
ADAM_B2 = 0.999
ADAM_EPS = 1e-08
ADAM_WD = 0.01
ADAM_STEP = 10

LANES = 128
HALO16 = 16
HALO32 = 32
VMEM_LIMIT = 56 * 1024 * 1024
FFN_BWD_VMEM = 60 * 1024 * 1024
N_CHIPS = 4


def _cparams(sem):
    return pltpu.CompilerParams(dimension_semantics=sem, vmem_limit_bytes=VMEM_LIMIT)


def _tile(n, pref):
    if n <= pref:
        return n
    t = (pref // LANES) * LANES
    while t >= LANES:
        if n % t == 0:
            return t
        t -= LANES
    return n


def _mm(a, b, out_dtype, name):
    m, k = a.shape
    _, n = b.shape
    tm = 512 if k > 2048 else 1024
    tm = min(tm, m)
    tn = _tile(n, 512)

    def body(a_ref, b_ref, o_ref):
        o_ref[...] = jnp.dot(a_ref[...], b_ref[...], preferred_element_type=F32).astype(o_ref.dtype)

    return pl.pallas_call(
        body, name=name, grid=(m // tm, n // tn),
        in_specs=[pl.BlockSpec((tm, k), lambda i, j: (i, 0)), pl.BlockSpec((k, tn), lambda i, j: (0, j))],
        out_specs=pl.BlockSpec((tm, tn), lambda i, j: (i, j)),
        out_shape=jax.ShapeDtypeStruct((m, n), out_dtype),
        compiler_params=_cparams(("parallel", "parallel")),
    )(a, b)


def _mm_tn(a, b, name):
    s, k = a.shape
    _, n = b.shape
    tk = _tile(k, 1408)
    tn = _tile(n, 1408)
    ts = min(512, s)

    def body(a_ref, b_ref, o_ref):
        @pl.when(pl.program_id(2) == 0)
        def _():
            o_ref[...] = jnp.zeros_like(o_ref)

        o_ref[...] += lax.dot_general(a_ref[...], b_ref[...], (((0,), (0,)), ((), ())),
                                      preferred_element_type=F32)

    return pl.pallas_call(
        body, name=name, grid=(k // tk, n // tn, s // ts),
        in_specs=[pl.BlockSpec((ts, tk), lambda i, j, l: (l, i)), pl.BlockSpec((ts, tn), lambda i, j, l: (l, j))],
        out_specs=pl.BlockSpec((tk, tn), lambda i, j, l: (i, j)),
        out_shape=jax.ShapeDtypeStruct((k, n), F32),
        compiler_params=_cparams(("parallel", "parallel", "arbitrary")),
    )(a, b)


def _rms_fwd(x, g, name):
    s, d = x.shape
    tr = min(512, s)

    def body(x_ref, g_ref, h_ref):
        xv = x_ref[...]
        r = lax.rsqrt(jnp.mean(xv * xv, axis=-1, keepdims=True) + RMS_EPS)
        h_ref[...] = (xv * r * g_ref[...]).astype(BF16)

    return pl.pallas_call(
        body, name=name, grid=(s // tr,),
        in_specs=[pl.BlockSpec((tr, d), lambda i: (i, 0)), pl.BlockSpec((1, d), lambda i: (0, 0))],
        out_specs=pl.BlockSpec((tr, d), lambda i: (i, 0)),
        out_shape=jax.ShapeDtypeStruct((s, d), BF16),
        compiler_params=_cparams(("parallel",)),
    )(x, g)


def _post_fwd(m, g, xres, name):
    s, d = m.shape
    tr = min(512, s)

    def body(m_ref, g_ref, x_ref, o_ref):
        mv = m_ref[...]
        r = lax.rsqrt(jnp.mean(mv * mv, axis=-1, keepdims=True) + RMS_EPS)
        o_ref[...] = x_ref[...] + mv * r * g_ref[...]

    return pl.pallas_call(
        body, name=name, grid=(s // tr,),
        in_specs=[pl.BlockSpec((tr, d), lambda i: (i, 0)), pl.BlockSpec((1, d), lambda i: (0, 0)),
                  pl.BlockSpec((tr, d), lambda i: (i, 0))],
        out_specs=pl.BlockSpec((tr, d), lambda i: (i, 0)),
        out_shape=jax.ShapeDtypeStruct((s, d), F32),
        compiler_params=_cparams(("parallel",)),
    )(m, g, xres)


def _norm_bwd(dy, xin, g, res, out_dtype, name):
    s, d = xin.shape
    tr = min(512, s)
    has_res = res is not None

    def body(*refs):
        if has_res:
            dy_ref, x_ref, g_ref, res_ref, o_ref, dg_ref = refs
        else:
            dy_ref, x_ref, g_ref, o_ref, dg_ref = refs

        @pl.when(pl.program_id(0) == 0)
        def _():
            dg_ref[...] = jnp.zeros_like(dg_ref)

        xv = x_ref[...]
        dyv = dy_ref[...].astype(F32)
        r = lax.rsqrt(jnp.mean(xv * xv, axis=-1, keepdims=True) + RMS_EPS)
        nrm = xv * r
        dn = dyv * g_ref[...]
        dx = r * (dn - nrm * jnp.mean(dn * nrm, axis=-1, keepdims=True))
        if has_res:
            dx = dx + res_ref[...]
        o_ref[...] = dx.astype(o_ref.dtype)
        dg_ref[...] += jnp.sum(dyv * nrm, axis=0, keepdims=True)

    row = pl.BlockSpec((tr, d), lambda i: (i, 0))
    vec = pl.BlockSpec((1, d), lambda i: (0, 0))
    in_specs = [row, row, vec] + ([row] if has_res else [])
    args = (dy, xin, g) + ((res,) if has_res else ())
    return pl.pallas_call(
        body, name=name, grid=(s // tr,),
        in_specs=in_specs, out_specs=[row, vec],
        out_shape=[jax.ShapeDtypeStruct((s, d), out_dtype), jax.ShapeDtypeStruct((1, d), F32)],
        compiler_params=_cparams(("arbitrary",)),
    )(*args)


def _loss_bwd(y, target, name):
    s, d = y.shape
    tr = min(512, s)

    def body(y_ref, t_ref, acc_ref, dy_ref):
        @pl.when(pl.program_id(0) == 0)
        def _():
            acc_ref[...] = jnp.zeros_like(acc_ref)

        e = y_ref[...] - t_ref[...]
        dy_ref[...] = e * (1.0 / d)
        acc_ref[...] += jnp.sum(e * e, axis=0, keepdims=True)

    row = pl.BlockSpec((tr, d), lambda i: (i, 0))
    vec = pl.BlockSpec((1, d), lambda i: (0, 0))
    return pl.pallas_call(
        body, name=name, grid=(s // tr,),
        in_specs=[row, row], out_specs=[vec, row],
        out_shape=[jax.ShapeDtypeStruct((1, d), F32), jax.ShapeDtypeStruct((s, d), F32)],
        compiler_params=_cparams(("arbitrary",)),
    )(y, target)


def _cur(tr, w, col=0):
    return pl.BlockSpec((tr, w), lambda i: (i, col))


def _prev(tr, h, w, col=0):
    return pl.BlockSpec((h, w), lambda i: (jnp.maximum(i * (tr // h) - 1, 0), col))


def _next(tr, h, w, nrows, col=0):
    last = nrows // h - 1
    return pl.BlockSpec((h, w), lambda i: (jnp.minimum((i + 1) * (tr // h), last), col))


def _full(shape):
    return pl.BlockSpec(shape, lambda i: tuple(0 for _ in shape))


def _silu_parts(g):
    sig = jax.nn.sigmoid(g)
    return sig, g * sig


FFN_CW = 256


def _conv3_taps(buf, w, off, rows):
    return (w[0:1] * buf[pl.ds(off, rows), :] + w[1:2] * buf[pl.ds(off + 1, rows), :]
            + w[2:3] * buf[pl.ds(off + 2, rows), :])


WHOLE_VMEM = pl.BlockSpec(memory_space=pltpu.VMEM)


def _ffn_fwd(x, g_pre, wu, conv_w, wd, g_post, name):
    s, d = x.shape
    f2 = wu.shape[1]
    f = f2 // 2
    tr = min(256, s)
    h = HALO16
    cw = FFN_CW

    def body(x_ref, gpre_ref, wu_ref, cw_ref, wd_ref, gpost_ref, xo_ref, f_ref, h_ref, up_ref, u_ref,
             carry, gbuf, vbuf, facc):
        @pl.when(pl.program_id(0) == 0)
        def _():
            carry[...] = jnp.zeros_like(carry)

        xv = x_ref[...]
        r = lax.rsqrt(jnp.mean(xv * xv, axis=-1, keepdims=True) + RMS_EPS)
        hv = (xv * r * gpre_ref[...]).astype(BF16)
        h_ref[...] = hv
        for j in range(f // cw):
            cg = slice(j * cw, (j + 1) * cw)
            cv = slice(f + j * cw, f + (j + 1) * cw)
            for buf, cs in ((gbuf, cg), (vbuf, cv)):
                upc = jnp.dot(hv, wu_ref[:, cs], preferred_element_type=F32)
                up_ref[:, cs] = upc.astype(BF16)
                buf[0:h, :] = carry[:, cs]
                buf[h:h + tr, :] = upc
                carry[:, cs] = upc[tr - h:tr, :]
            g = _conv3_taps(gbuf, cw_ref[:, cg], h - 2, tr)
            v = _conv3_taps(vbuf, cw_ref[:, cv], h - 2, tr)
            u_ref[:, cg] = g.astype(BF16)
            u_ref[:, cv] = v.astype(BF16)
            act = (g * jax.nn.sigmoid(g) * v).astype(BF16)
            part = jnp.dot(act, wd_ref[j * cw:(j + 1) * cw, :], preferred_element_type=F32)
            if j == 0:
                facc[...] = part
            else:
                facc[...] += part
        fv = facc[...]
        f_ref[...] = fv
        r2 = lax.rsqrt(jnp.mean(fv * fv, axis=-1, keepdims=True) + RMS_EPS)
        xo_ref[...] = xv + fv * r2 * gpost_ref[...]

    row = _cur(tr, d)
    wide = _cur(tr, f2)
    return pl.pallas_call(
        body, name=name, grid=(s // tr,),
        in_specs=[row, _full((1, d)), WHOLE_VMEM, _full((3, f2)), WHOLE_VMEM, _full((1, d))],
        out_specs=[row, row, row, wide, wide],
        out_shape=[jax.ShapeDtypeStruct((s, d), F32), jax.ShapeDtypeStruct((s, d), F32),
                   jax.ShapeDtypeStruct((s, d), BF16), jax.ShapeDtypeStruct((s, f2), BF16),
                   jax.ShapeDtypeStruct((s, f2), BF16)],
        scratch_shapes=[pltpu.VMEM((h, f2), F32), pltpu.VMEM((h + tr, cw), F32), pltpu.VMEM((h + tr, cw), F32),
                        pltpu.VMEM((tr, d), F32)],
        compiler_params=_cparams(("arbitrary",)),
    )(x, g_pre, wu, conv_w, wd, g_post)


def _ffn_bwd(dxo, fout, x, up, u, g_pre, g_post, wd_t, wu_t, conv_w, name):
    s, d = x.shape
    f2 = up.shape[1]
    f = f2 // 2
    tr = min(256, s)
    nt = s // tr
    h = HALO16
    cw = FFN_CW

    def body(dy_ref, f_ref, x_ref, up_ref, u_ref, gpre_ref, gpost_ref, wdt_ref, wut_ref, cw_ref,
             dx_ref, dup_ref, act_ref, df_ref, dcw_ref, dgpost_ref, dgpre_ref, carry, dgbuf, dvbuf, dhacc):
        @pl.when(pl.program_id(0) == 0)
        def _():
            carry[...] = jnp.zeros_like(carry)
            dcw_ref[...] = jnp.zeros_like(dcw_ref)
            dgpost_ref[...] = jnp.zeros_like(dgpost_ref)
            dgpre_ref[...] = jnp.zeros_like(dgpre_ref)

        dy = dy_ref[...]
        fv = f_ref[...]
        r = lax.rsqrt(jnp.mean(fv * fv, axis=-1, keepdims=True) + RMS_EPS)
        nrm = fv * r
        dn = dy * gpost_ref[...]
        dfv = (r * (dn - nrm * jnp.mean(dn * nrm, axis=-1, keepdims=True))).astype(BF16)
        dgpost_ref[...] += jnp.sum(dy * nrm, axis=0, keepdims=True)
        df_ref[...] = dfv
        for j in range(f // cw):
            ch = slice(j * cw, (j + 1) * cw)
            cg = ch
            cv = slice(f + j * cw, f + (j + 1) * cw)
            dact = jnp.dot(dfv, wdt_ref[:, ch], preferred_element_type=F32)
            g = u_ref[:, cg].astype(F32)
            v = u_ref[:, cv].astype(F32)
            sig, sil = _silu_parts(g)
            act_ref[:, ch] = (sil * v).astype(BF16)
            du_g = dact * v * (sig * (1.0 + g * (1.0 - sig)))
            du_v = dact * sil
            for k, (dbuf, du, cs) in enumerate(((dgbuf, du_g, cg), (dvbuf, du_v, cv))):
                dbuf[0:tr, :] = du
                dbuf[tr:tr + h, :] = carry[:, cs]
                carry[:, cs] = du[0:h, :]
                w = cw_ref[:, cs]
                xin = up_ref[:, cs].astype(F32)
                acc = None
                for sh in range(3):
                    dsh = dbuf[pl.ds(sh, tr), :]
                    term = w[2 - sh:3 - sh] * dsh
                    acc = term if acc is None else acc + term
                    dcw_ref[2 - sh:3 - sh, cs] += jnp.sum(xin * dsh, axis=0, keepdims=True)
                dupb = acc.astype(BF16)
                dup_ref[:, cs] = dupb
                part = jnp.dot(dupb, wut_ref[cs, :], preferred_element_type=F32)
                if j == 0 and k == 0:
                    dhacc[...] = part
                else:
                    dhacc[...] += part
        dh = dhacc[...]
        xv = x_ref[...]
        r1 = lax.rsqrt(jnp.mean(xv * xv, axis=-1, keepdims=True) + RMS_EPS)
        n1 = xv * r1
        dn1 = dh * gpre_ref[...]
        dx_ref[...] = dy + r1 * (dn1 - n1 * jnp.mean(dn1 * n1, axis=-1, keepdims=True))
        dgpre_ref[...] += jnp.sum(dh * n1, axis=0, keepdims=True)

    def rev(w):
        return pl.BlockSpec((tr, w), lambda i: (nt - 1 - i, 0))

    vec = _full((1, d))
    return pl.pallas_call(
        body, name=name, grid=(nt,),
        in_specs=[rev(d), rev(d), rev(d), rev(f2), rev(f2), vec, vec, WHOLE_VMEM, WHOLE_VMEM, _full((3, f2))],
        out_specs=[rev(d), rev(f2), rev(f), rev(d), _full((3, f2)), vec, vec],
        out_shape=[jax.ShapeDtypeStruct((s, d), F32), jax.ShapeDtypeStruct((s, f2), BF16),
                   jax.ShapeDtypeStruct((s, f), BF16), jax.ShapeDtypeStruct((s, d), BF16),
                   jax.ShapeDtypeStruct((3, f2), F32), jax.ShapeDtypeStruct((1, d), F32),
                   jax.ShapeDtypeStruct((1, d), F32)],
        scratch_shapes=[pltpu.VMEM((h, f2), F32), pltpu.VMEM((tr + h, cw), F32), pltpu.VMEM((tr + h, cw), F32),
                        pltpu.VMEM((tr, d), F32)],
        compiler_params=pltpu.CompilerParams(dimension_semantics=("arbitrary",), vmem_limit_bytes=FFN_BWD_VMEM),
    )(dxo, fout, x, up, u, g_pre, g_post, wd_t, wu_t, conv_w)


def _od_gate_fwd(z, conv_w, name):
    s, d3 = z.shape
    d = d3 // 3
    tr = min(256, s)
    h = HALO16
    cw = FFN_CW

    def body(z_ref, prev_ref, w_ref, o_ref, buf):
        first = pl.program_id(0) == 0
        for j in range(d // cw):
            cb = slice(j * cw, (j + 1) * cw)
            cc = slice(d + j * cw, d + (j + 1) * cw)
            cu = slice(2 * d + j * cw, 2 * d + (j + 1) * cw)
            buf[0:h, :] = jnp.where(first, 0.0, prev_ref[:, cc].astype(F32) * prev_ref[:, cu].astype(F32))
            buf[h:h + tr, :] = z_ref[:, cc].astype(F32) * z_ref[:, cu].astype(F32)
            k = _conv3_taps(buf, w_ref[:, cb], h - 2, tr)
            o_ref[:, cb] = (z_ref[:, cb].astype(F32) * k).astype(BF16)

    return pl.pallas_call(
        body, name=name, grid=(s // tr,),
        in_specs=[_cur(tr, d3), _prev(tr, h, d3), _full((3, d))],
        out_specs=_cur(tr, d),
        out_shape=jax.ShapeDtypeStruct((s, d), BF16),
        scratch_shapes=[pltpu.VMEM((h + tr, cw), F32)],
        compiler_params=_cparams(("parallel",)),
    )(z, z, conv_w)


def _od_gate_bwd(dy, z, conv_w, name):
    s, d3 = z.shape
    d = d3 // 3
    tr = min(256, s)
    h = HALO16
    cw = FFN_CW
    ext = tr + h

    def body(dy_ref, dyn_ref, z_ref, zp_ref, zn_ref, w_ref, o_ref, dw_ref, buf, dbuf):
        i = pl.program_id(0)
        first = i == 0
        last = i == pl.num_programs(0) - 1

        @pl.when(first)
        def _():
            dw_ref[...] = jnp.zeros_like(dw_ref)

        for j in range(d // cw):
            cb = slice(j * cw, (j + 1) * cw)
            cc = slice(d + j * cw, d + (j + 1) * cw)
            cu = slice(2 * d + j * cw, 2 * d + (j + 1) * cw)
            w = w_ref[:, cb]
            cval = z_ref[:, cc].astype(F32)
            uval = z_ref[:, cu].astype(F32)
            buf[0:h, :] = jnp.where(first, 0.0, zp_ref[:, cc].astype(F32) * zp_ref[:, cu].astype(F32))
            buf[h:h + tr, :] = cval * uval
            k = _conv3_taps(buf, w, h - 2, tr)
            dyv = dy_ref[:, cb]
            o_ref[:, cb] = (dyv * k).astype(BF16)
            dbuf[0:tr, :] = dyv * z_ref[:, cb].astype(F32)
            dbuf[tr:ext, :] = jnp.where(last, 0.0, dyn_ref[:, cb] * zn_ref[:, cb].astype(F32))
            dcu = w[2:3] * dbuf[pl.ds(0, tr), :] + w[1:2] * dbuf[pl.ds(1, tr), :] + w[0:1] * dbuf[pl.ds(2, tr), :]
            o_ref[:, cc] = (dcu * uval).astype(BF16)
            o_ref[:, cu] = (dcu * cval).astype(BF16)
            dk = dbuf[pl.ds(0, tr), :]
            for t in range(3):
                dw_ref[t:t + 1, cb] += jnp.sum(dk * buf[pl.ds(h - 2 + t, tr), :], axis=0, keepdims=True)

    return pl.pallas_call(
        body, name=name, grid=(s // tr,),
        in_specs=[_cur(tr, d), _next(tr, h, d, s), _cur(tr, d3), _prev(tr, h, d3), _next(tr, h, d3, s), _full((3, d))],
        out_specs=[_cur(tr, d3), _full((3, d))],
        out_shape=[jax.ShapeDtypeStruct((s, d3), BF16), jax.ShapeDtypeStruct((3, d), F32)],
        scratch_shapes=[pltpu.VMEM((h + tr, cw), F32), pltpu.VMEM((ext, cw), F32)],
        compiler_params=_cparams(("arbitrary",)),
    )(dy, dy, z, z, z, conv_w)


Q0 = 2 * A_CH
K0 = Q0 + N_Q_HEADS * HEAD_DIM
V0 = K0 + 2 * HEAD_DIM
EVEN_IN = V0 + 2 * HEAD_DIM


def _rope_tables(positions):
    half = ROPE_DIM // 2
    inv_freq = ROPE_THETA ** (-(jnp.arange(half, dtype=F32) * 2.0 / ROPE_DIM))
    ang = positions.astype(F32)[:, None] * inv_freq
    cs = jnp.concatenate([jnp.cos(ang), jnp.sin(ang)], axis=1)
    spread = np.zeros((2 * half, 3 * LANES), np.float32)
    const = np.zeros((1, 3 * LANES), np.float32)
    for lane in range(3 * LANES):
        dim, part = lane % HEAD_DIM, lane // LANES
        if part == 0:
            if dim < ROPE_DIM:
                spread[dim % half, lane] = 1.0
            else:
                const[0, lane] = 1.0
        elif part == 1 and half <= dim < ROPE_DIM:
            spread[half + dim - half, lane] = 1.0
        elif part == 2 and dim < half:
            spread[half + dim, lane] = -1.0
    return jnp.dot(cs, jnp.asarray(spread), precision=lax.Precision.HIGHEST) + jnp.asarray(const)


def _rope_fwd(x, tab):
    c, sa, sb = tab[:, 0:LANES], tab[:, LANES:2 * LANES], tab[:, 2 * LANES:3 * LANES]
    return x * c + pltpu.roll(x, 8, 1) * sa + pltpu.roll(x, LANES - 8, 1) * sb


def _rope_bwd(dy, tab):
    c, sa, sb = tab[:, 0:LANES], tab[:, LANES:2 * LANES], tab[:, 2 * LANES:3 * LANES]
    return dy * c + pltpu.roll(dy * sa, LANES - 8, 1) + pltpu.roll(dy * sb, 8, 1)


def _ln_fwd(c, g, b):
    mu = jnp.mean(c, axis=-1, keepdims=True)
    xc = c - mu
    r = lax.rsqrt(jnp.mean(xc * xc, axis=-1, keepdims=True) + LN_EPS)
    nrm = xc * r
    return nrm, r, nrm * g + b


def _ev_mid_fwd(z, tab, conv_w, conv_b, ln_g, ln_b, name):
    s = z.shape[0]
    tr = min(256, s)
    h = HALO32
    cw = LANES

    def body(z_ref, zp_ref, tab_ref, w_ref, b_ref, g_ref, lb_ref, c_ref, a_ref, qkv_ref, gbuf, cbuf):
        first = pl.program_id(0) == 0
        glu_p = zp_ref[:, 0:A_CH].astype(F32) * jax.nn.sigmoid(zp_ref[:, A_CH:2 * A_CH].astype(F32))
        gbuf[0:h, :] = jnp.where(first, 0.0, glu_p)
        gbuf[h:h + tr, :] = z_ref[:, 0:A_CH].astype(F32) * jax.nn.sigmoid(z_ref[:, A_CH:2 * A_CH].astype(F32))
        for j in range(A_CH // cw):
            cs = slice(j * cw, (j + 1) * cw)
            acc = jnp.broadcast_to(b_ref[:, cs], (tr, cw))
            for t in range(A_CONV):
                acc = acc + w_ref[t:t + 1, cs] * gbuf[pl.ds(h - (A_CONV - 1) + t, tr), cs]
            cbuf[:, cs] = acc
        c = cbuf[...]
        c_ref[...] = c.astype(BF16)
        _, _, l = _ln_fwd(c, g_ref[...], lb_ref[...])
        a_ref[...] = (l * jax.nn.sigmoid(l)).astype(BF16)
        tab_v = tab_ref[...]
        for p in range(4):
            xq = z_ref[:, Q0 + p * LANES:Q0 + (p + 1) * LANES].astype(F32)
            qkv_ref[:, p * LANES:(p + 1) * LANES] = _rope_fwd(xq, tab_v).astype(BF16)
        lane = lax.broadcasted_iota(jnp.int32, (tr, LANES), 1)
        lo = lane < HEAD_DIM
        kr = _rope_fwd(z_ref[:, K0:K0 + LANES].astype(F32), tab_v)
        vr = z_ref[:, V0:V0 + LANES].astype(F32)
        for base, val in ((4 * LANES, kr), (6 * LANES, vr)):
            sw = pltpu.roll(val, HEAD_DIM, 1)
            qkv_ref[:, base:base + LANES] = jnp.where(lo, val, sw).astype(BF16)
            qkv_ref[:, base + LANES:base + 2 * LANES] = jnp.where(lo, sw, val).astype(BF16)

    return pl.pallas_call(
        body, name=name, grid=(s // tr,),
        in_specs=[_cur(tr, EVEN_IN), _prev(tr, h, 2 * A_CH), _cur(tr, 3 * LANES), _full((A_CONV, A_CH)),
                  _full((1, A_CH)), _full((1, A_CH)), _full((1, A_CH))],
        out_specs=[_cur(tr, A_CH), _cur(tr, A_CH), _cur(tr, 2 * A_CH)],
        out_shape=[jax.ShapeDtypeStruct((s, A_CH), BF16), jax.ShapeDtypeStruct((s, A_CH), BF16),
                   jax.ShapeDtypeStruct((s, 2 * A_CH), BF16)],
        scratch_shapes=[pltpu.VMEM((h + tr, A_CH), F32), pltpu.VMEM((tr, A_CH), F32)],
        compiler_params=_cparams(("parallel",)),
    )(z, z, tab, conv_w, conv_b, ln_g, ln_b)


def _ev_mid_bwd(dcat, c, z, dq, dkv, tab, conv_w, ln_g, ln_b, name):
    s = z.shape[0]
    tr = min(256, s)
    h = HALO32
    cw = LANES
    ext = tr + h

    def body(da_ref, dan_ref, c_ref, cn_ref, z_ref, zp_ref, dq_ref, dkv_ref, tab_ref, w_ref, g_ref, lb_ref,
             dz_ref, dw_ref, dvec_ref, gbuf, dcbuf):
        i = pl.program_id(0)
        first = i == 0
        last = i == pl.num_programs(0) - 1

        @pl.when(first)
        def _():
            dw_ref[...] = jnp.zeros_like(dw_ref)
            dvec_ref[...] = jnp.zeros_like(dvec_ref)

        gv = g_ref[...]

        def ln_silu_bwd(cv, dav):
            nrm, r, l = _ln_fwd(cv, gv, lb_ref[...])
            sig = jax.nn.sigmoid(l)
            dl = dav * (sig * (1.0 + l * (1.0 - sig)))
            dn = dl * gv
            dc = r * (dn - jnp.mean(dn, axis=-1, keepdims=True) - nrm * jnp.mean(dn * nrm, axis=-1, keepdims=True))
            return dc, dl, nrm

        dc, dl, nrm = ln_silu_bwd(c_ref[...].astype(F32), da_ref[...])
        dcn, _, _ = ln_silu_bwd(cn_ref[...].astype(F32), dan_ref[...])
        dcbuf[0:tr, :] = dc
        dcbuf[tr:ext, :] = jnp.where(last, 0.0, dcn)
        dvec_ref[0:1, :] += jnp.sum(dc, axis=0, keepdims=True)
        dvec_ref[1:2, :] += jnp.sum(dl * nrm, axis=0, keepdims=True)
        dvec_ref[2:3, :] += jnp.sum(dl, axis=0, keepdims=True)

        a_lin = z_ref[:, 0:A_CH].astype(F32)
        sig_g = jax.nn.sigmoid(z_ref[:, A_CH:2 * A_CH].astype(F32))
        glu_p = zp_ref[:, 0:A_CH].astype(F32) * jax.nn.sigmoid(zp_ref[:, A_CH:2 * A_CH].astype(F32))
        gbuf[0:h, :] = jnp.where(first, 0.0, glu_p)
        gbuf[h:h + tr, :] = a_lin * sig_g
        for j in range(A_CH // cw):
            cs = slice(j * cw, (j + 1) * cw)
            dcj = dcbuf[pl.ds(0, tr), cs]
            acc = jnp.zeros((tr, cw), F32)
            for t in range(A_CONV):
                acc = acc + w_ref[t:t + 1, cs] * dcbuf[pl.ds(A_CONV - 1 - t, tr), cs]
                dw_ref[t:t + 1, cs] += jnp.sum(dcj * gbuf[pl.ds(h - (A_CONV - 1) + t, tr), cs], axis=0, keepdims=True)
            dz_ref[:, cs] = (acc * sig_g[:, cs]).astype(BF16)
            dz_ref[:, A_CH + j * cw:A_CH + (j + 1) * cw] = (
                acc * a_lin[:, cs] * sig_g[:, cs] * (1.0 - sig_g[:, cs])).astype(BF16)

        tab_v = tab_ref[...]
        for p in range(4):
            cs = slice(p * LANES, (p + 1) * LANES)
            dz_ref[:, Q0 + p * LANES:Q0 + (p + 1) * LANES] = _rope_bwd(dq_ref[:, cs], tab_v).astype(BF16)
        lane = lax.broadcasted_iota(jnp.int32, (tr, LANES), 1)
        lo = lane < HEAD_DIM

        def fold(base):
            p0 = dkv_ref[:, base:base + LANES]
            p1 = dkv_ref[:, base + LANES:base + 2 * LANES]
            s0 = p0 + pltpu.roll(p0, HEAD_DIM, 1)
            s1 = p1 + pltpu.roll(p1, HEAD_DIM, 1)
            return jnp.where(lo, s0, s1)

        dz_ref[:, K0:K0 + LANES] = _rope_bwd(fold(0), tab_v).astype(BF16)
        dz_ref[:, V0:V0 + LANES] = fold(2 * LANES).astype(BF16)

    return pl.pallas_call(
        body, name=name, grid=(s // tr,),
        in_specs=[_cur(tr, A_CH), _next(tr, h, A_CH, s), _cur(tr, A_CH), _next(tr, h, A_CH, s),
                  _cur(tr, EVEN_IN), _prev(tr, h, 2 * A_CH), _cur(tr, A_CH), _cur(tr, A_CH), _cur(tr, 3 * LANES),
                  _full((A_CONV, A_CH)), _full((1, A_CH)), _full((1, A_CH))],
        out_specs=[_cur(tr, EVEN_IN), _full((A_CONV, A_CH)), _full((8, A_CH))],
        out_shape=[jax.ShapeDtypeStruct((s, EVEN_IN), BF16), jax.ShapeDtypeStruct((A_CONV, A_CH), F32),
                   jax.ShapeDtypeStruct((8, A_CH), F32)],
        scratch_shapes=[pltpu.VMEM((h + tr, A_CH), F32), pltpu.VMEM((ext, A_CH), F32)],
        compiler_params=_cparams(("arbitrary",)),
    )(dcat, dcat, c, c, z, z, dq, dkv, tab, conv_w, ln_g, ln_b)


NT = (((1,), (1,)), ((), ()))
TN = (((0,), (0,)), ((), ()))
QB = WINDOW
SCALE = HEAD_DIM ** -0.5


def _att_probs(q2m, kwin, sink, mask):
    sc = lax.dot_general(q2m, kwin, NT, preferred_element_type=F32) * SCALE
    sc = jnp.where(mask, sc, -jnp.inf)
    mx = jnp.maximum(jnp.max(sc, axis=-1, keepdims=True), sink)
    p = jnp.exp(sc - mx)
    ps = jnp.exp(sink - mx)
    inv = 1.0 / (jnp.sum(p, axis=-1, keepdims=True) + ps)
    return p * inv, ps * inv


def _att_mask(i):
    r = lax.broadcasted_iota(jnp.int32, (QB, 2 * QB), 0)
    kc = lax.broadcasted_iota(jnp.int32, (QB, 2 * QB), 1)
    diff = r + QB - kc
    return (diff >= 0) & (diff < WINDOW) & ((kc >= QB) | (i > 0))


def _half_masks(dtype):
    lane = lax.broadcasted_iota(jnp.int32, (1, LANES), 1)
    return (lane < HEAD_DIM).astype(dtype), (lane >= HEAD_DIM).astype(dtype)


def _att_fwd(qkv, a, sinks, name):
    s = qkv.shape[0]
    nb = s // QB

    def body(sink_ref, qkv_ref, kvp_ref, a_ref, o_ref):
        i = pl.program_id(0)
        mask = _att_mask(i)
        mlo, mhi = _half_masks(BF16)
        o_ref[:, 0:A_CH] = a_ref[...]
        for p in range(4):
            g = p // 2
            q2 = qkv_ref[:, p * LANES:(p + 1) * LANES]
            kwin = jnp.concatenate([kvp_ref[:, g * LANES:(g + 1) * LANES],
                                    qkv_ref[:, A_CH + g * LANES:A_CH + (g + 1) * LANES]], axis=0)
            vwin = jnp.concatenate([kvp_ref[:, (2 + g) * LANES:(3 + g) * LANES],
                                    qkv_ref[:, A_CH + (2 + g) * LANES:A_CH + (3 + g) * LANES]], axis=0)
            pe, _ = _att_probs(q2 * mlo, kwin, sink_ref[2 * p], mask)
            po, _ = _att_probs(q2 * mhi, kwin, sink_ref[2 * p + 1], mask)
            o = (jnp.dot(pe.astype(BF16), vwin * mlo, preferred_element_type=F32)
                 + jnp.dot(po.astype(BF16), vwin * mhi, preferred_element_type=F32))
            o_ref[:, A_CH + p * LANES:A_CH + (p + 1) * LANES] = o.astype(BF16)

    grid_spec = pltpu.PrefetchScalarGridSpec(
        num_scalar_prefetch=1, grid=(nb,),
        in_specs=[pl.BlockSpec((QB, 2 * A_CH), lambda i, sk: (i, 0)),
                  pl.BlockSpec((QB, A_CH), lambda i, sk: (jnp.maximum(i - 1, 0), 1)),
                  pl.BlockSpec((QB, A_CH), lambda i, sk: (i, 0))],
        out_specs=pl.BlockSpec((QB, 2 * A_CH), lambda i, sk: (i, 0)),
    )
    return pl.pallas_call(
        body, name=name, grid_spec=grid_spec,
        out_shape=jax.ShapeDtypeStruct((s, 2 * A_CH), BF16),
        compiler_params=_cparams(("parallel",)),
    )(sinks, qkv, qkv, a)


def _att_bwd(qkv, dcat, sinks, name):
    s = qkv.shape[0]
    nb = s // QB

    def body(sink_ref, qkv_ref, kvp_ref, do_ref, dq_ref, dkv_ref, ds_ref, carry):
        i = pl.program_id(0)

        @pl.when(i == 0)
        def _():
            ds_ref[...] = jnp.zeros_like(ds_ref)
            carry[...] = jnp.zeros_like(carry)

        @pl.when(i < nb)
        def _():
            mask = _att_mask(i)
            mlo, mhi = _half_masks(BF16)
            dwin = [jnp.zeros((2 * QB, LANES), F32) for _ in range(4)]
            for p in range(4):
                g = p // 2
                q2 = qkv_ref[:, p * LANES:(p + 1) * LANES]
                kwin = jnp.concatenate([kvp_ref[:, g * LANES:(g + 1) * LANES],
                                        qkv_ref[:, A_CH + g * LANES:A_CH + (g + 1) * LANES]], axis=0)
                vwin = jnp.concatenate([kvp_ref[:, (2 + g) * LANES:(3 + g) * LANES],
                                        qkv_ref[:, A_CH + (2 + g) * LANES:A_CH + (3 + g) * LANES]], axis=0)
                do2 = do_ref[:, p * LANES:(p + 1) * LANES].astype(BF16)
                dq2 = jnp.zeros((QB, LANES), F32)
                for e, hm in enumerate((mlo, mhi)):
                    qm = q2 * hm
                    dom = do2 * hm
                    prob, psink = _att_probs(qm, kwin, sink_ref[2 * p + e], mask)
                    dp = lax.dot_general(dom, vwin, NT, preferred_element_type=F32)
                    delta = jnp.sum(prob * dp, axis=-1, keepdims=True)
                    dsc = (prob * (dp - delta) * SCALE).astype(BF16)
                    ds_ref[2 * p + e:2 * p + e + 1, :] += jnp.broadcast_to(
                        jnp.sum(-psink * delta, axis=0, keepdims=True), (1, LANES))
                    dq2 = dq2 + jnp.dot(dsc, kwin * hm, preferred_element_type=F32)
                    dwin[g] = dwin[g] + lax.dot_general(dsc, qm, TN, preferred_element_type=F32)
                    dwin[2 + g] = dwin[2 + g] + lax.dot_general(prob.astype(BF16), dom, TN,
                                                                preferred_element_type=F32)
                dq_ref[:, p * LANES:(p + 1) * LANES] = dq2
            for n in range(4):
                cs = slice(n * LANES, (n + 1) * LANES)
                dkv_ref[:, cs] = carry[:, cs] + dwin[n][0:QB, :]
                carry[:, cs] = dwin[n][QB:2 * QB, :]

        @pl.when(i == nb)
        def _():
            dkv_ref[...] = carry[...]

    grid_spec = pltpu.PrefetchScalarGridSpec(
        num_scalar_prefetch=1, grid=(nb + 1,),
        in_specs=[pl.BlockSpec((QB, 2 * A_CH), lambda i, sk: (jnp.minimum(i, nb - 1), 0)),
                  pl.BlockSpec((QB, A_CH), lambda i, sk: (jnp.maximum(jnp.minimum(i, nb - 1) - 1, 0), 1)),
                  pl.BlockSpec((QB, A_CH), lambda i, sk: (jnp.minimum(i, nb - 1), 1))],
        out_specs=[pl.BlockSpec((QB, A_CH), lambda i, sk: (jnp.minimum(i, nb - 1), 0)),
                   pl.BlockSpec((QB, A_CH), lambda i, sk: (jnp.maximum(i - 1, 0), 0)),
                   pl.BlockSpec((8, LANES), lambda i, sk: (0, 0))],
        scratch_shapes=[pltpu.VMEM((QB, A_CH), F32)],
    )
    return pl.pallas_call(
        body, name=name, grid_spec=grid_spec,
        out_shape=[jax.ShapeDtypeStruct((s, A_CH), F32), jax.ShapeDtypeStruct((s, A_CH), F32),
                   jax.ShapeDtypeStruct((8, LANES), F32)],
        compiler_params=_cparams(("arbitrary",)),
    )(sinks, qkv, qkv, dcat)


def _local_step(x, positions, target, w):
    row = lambda a, i: a[i:i + 1]
    tab = _rope_tables(positions)
    g = {}

    def ffn_fwd(xin, i):
        xout, f, h, up, u = _ffn_fwd(xin, row(w["ffn_norm_pre"], i), w["ffn_w_up"][i], w["ffn_conv_w"][i],
                                     w["ffn_w_down"][i], row(w["ffn_norm_post"], i), f"ffn{i}_fwd")
        return xout, (xin, f, h, up, u)

    def ffn_bwd(dxout, saved, i):
        xin, f, h, up, u = saved
        dxin, dup, act, df, d_cw, dg_post, dg_pre = _ffn_bwd(
            dxout, f, xin, up, u, row(w["ffn_norm_pre"], i), row(w["ffn_norm_post"], i), w["ffn_w_down_t"][i],
            w["ffn_w_up_t"][i], w["ffn_conv_w"][i], f"ffn{i}_bwd")
        d_down = _mm_tn(act, df, f"ffn{i}_down_dw")
        d_up = _mm_tn(dup, h, f"ffn{i}_up_dw")
        return dxin, dict(ffn_norm_post=dg_post, ffn_norm_pre=dg_pre, ffn_w_up=d_up, ffn_conv_w=d_cw, ffn_w_down=d_down)

    h0 = _rms_fwd(x, row(w["mix_norm_pre"], 0), "ev_pre")
    z0 = _mm(h0, w["ev_w_in"], BF16, "ev_in")
    c0, a0, qkv = _ev_mid_fwd(z0, tab, w["ev_a_conv_w"], w["ev_a_conv_b"], w["ev_a_ln_g"], w["ev_a_ln_b"], "ev_mid")
    cat = _att_fwd(qkv, a0, w["ev_sinks"], "ev_att")
    m0 = _mm(cat, w["ev_w_out"], F32, "ev_out")
    x1 = _post_fwd(m0, row(w["mix_norm_post"], 0), x, "ev_post")
    x2, ffn0 = ffn_fwd(x1, 0)
    h2 = _rms_fwd(x2, row(w["mix_norm_pre"], 1), "od_pre")
    z1 = _mm(h2, w["od_w_in"], BF16, "od_in")
    y1 = _od_gate_fwd(z1, w["od_conv_w"], "od_mid")
    m1 = _mm(y1, w["od_w_out"], F32, "od_out")
    x3 = _post_fwd(m1, row(w["mix_norm_post"], 1), x2, "od_post")
    x4, ffn1 = ffn_fwd(x3, 1)

    sq, dx4 = _loss_bwd(x4, target, "loss")

    dx3, gf1 = ffn_bwd(dx4, ffn1, 1)
    dm1, dg_mo1 = _norm_bwd(dx3, m1, row(w["mix_norm_post"], 1), None, BF16, "od_post_bwd")
    g["od_w_out"] = _mm_tn(y1, dm1, "od_out_dw")
    dy1 = _mm(dm1, w["od_w_out_t"], F32, "od_out_dx")
    dz1, g["od_conv_w"] = _od_gate_bwd(dy1, z1, w["od_conv_w"], "od_mid_bwd")
    g["od_w_in"] = _mm_tn(dz1, h2, "od_in_dw")
    dh2 = _mm(dz1, w["od_w_in_t"], F32, "od_in_dx")
    dx2, dg_mp1 = _norm_bwd(dh2, x2, row(w["mix_norm_pre"], 1), dx3, F32, "od_pre_bwd")

    dx1, gf0 = ffn_bwd(dx2, ffn0, 0)
    dm0, dg_mo0 = _norm_bwd(dx1, m0, row(w["mix_norm_post"], 0), None, BF16, "ev_post_bwd")
    g["ev_w_out"] = _mm_tn(cat, dm0, "ev_out_dw")
    dcat = _mm(dm0, w["ev_w_out_t"], F32, "ev_out_dx")
    dq, dkv, dsk = _att_bwd(qkv, dcat, w["ev_sinks"], "ev_att_bwd")
    dz0, g["ev_a_conv_w"], dvec = _ev_mid_bwd(dcat, c0, z0, dq, dkv, tab, w["ev_a_conv_w"], w["ev_a_ln_g"],
                                              w["ev_a_ln_b"], "ev_mid_bwd")
    g["ev_w_in"] = _mm_tn(dz0, h0, "ev_in_dw")
    dh0 = _mm(dz0, w["ev_w_in_t"], F32, "ev_in_dx")
    dx0, dg_mp0 = _norm_bwd(dh0, x, row(w["mix_norm_pre"], 0), dx1, F32, "ev_pre_bwd")

    g["ev_a_conv_b"] = dvec[0:1]
    g["ev_a_ln_g"] = dvec[1:2]
    g["ev_a_ln_b"] = dvec[2:3]
    g["ev_sinks"] = dsk[:, 0]
    g["mix_norm_pre"] = jnp.concatenate([dg_mp0, dg_mp1], axis=0)
    g["mix_norm_post"] = jnp.concatenate([dg_mo0, dg_mo1], axis=0)
    for k in ("ffn_norm_pre", "ffn_norm_post", "ffn_w_up", "ffn_conv_w", "ffn_w_down"):
        g[k] = jnp.stack([gf0[k], gf1[k]], axis=0) if gf0[k].shape[0] != 1 else jnp.concatenate([gf0[k], gf1[k]], axis=0)
    return sq, dx0, g


ANY = pl.BlockSpec(memory_space=pl.ANY)
PACK_COLS = 1024


def _me():
    return lax.axis_index("x"), lax.axis_index("y"), lax.axis_index("c")


def _other_chips(x, y):
    return [(1 - x, y), (x, 1 - y), (1 - x, 1 - y)]


def _remote(src, dst, send, recv, dev):
    return pltpu.make_async_remote_copy(src_ref=src, dst_ref=dst, send_sem=send, recv_sem=recv,
                                        device_id=dev, device_id_type=MESH)


def _gather_chips(wp, name):
    r, cols = wp.shape
    rh = r // 2

    def body(w_ref, o_ref, send, recv):
        x, y, c = _me()
        p = 2 * x + y
        sib = (x, y, 1 - c)
        chips = _other_chips(x, y)
        half = pl.ds(c * rh, rh)
        other = pl.ds((1 - c) * rh, rh)
        sent = [_remote(w_ref.at[half], o_ref.at[p, half], send.at[k], recv.at[k], (cx, cy, c))
                for k, (cx, cy) in enumerate(chips)]
        for cp in sent:
            cp.start()
        for k, (cx, cy) in enumerate(chips):
            q = 2 * cx + cy
            _remote(w_ref.at[half], o_ref.at[q, half], send.at[k], recv.at[k], (cx, cy, c)).wait_recv()
            fwd = _remote(o_ref.at[q, half], o_ref.at[q, half], send.at[3 + k], recv.at[3 + k], sib)
            fwd.start()
            sent.append(fwd)
        for k, (cx, cy) in enumerate(chips):
            q = 2 * cx + cy
            _remote(o_ref.at[q, other], o_ref.at[q, other], send.at[3 + k], recv.at[3 + k], sib).wait_recv()
        for cp in sent:
            cp.wait_send()

    return pl.pallas_call(
        body, name=name, in_specs=[ANY], out_specs=ANY,
        out_shape=jax.ShapeDtypeStruct((N_CHIPS, r, cols), wp.dtype),
        scratch_shapes=[pltpu.SemaphoreType.DMA((6,)), pltpu.SemaphoreType.DMA((6,))],
    )(wp)


def _exchange8(v, reduce, name):
    r, cols = v.shape
    rel = [(a, b, d) for a in (0, 1) for b in (0, 1) for d in (0, 1) if (a, b, d) != (0, 0, 0)]

    def body(v_ref, o_ref, *rest):
        if reduce:
            gbuf, send, recv = rest
        else:
            gbuf = o_ref
            send, recv = rest
        x, y, c = _me()
        me = 4 * x + 2 * y + c
        gbuf[me] = v_ref[...]
        sent = []
        for k, (a, b, d) in enumerate(rel):
            cp = _remote(v_ref, gbuf.at[me], send.at[k], recv.at[k], ((x + a) % 2, (y + b) % 2, (c + d) % 2))
            cp.start()
            sent.append(cp)
        for k, (a, b, d) in enumerate(rel):
            src = 4 * ((x + a) % 2) + 2 * ((y + b) % 2) + (c + d) % 2
            _remote(v_ref, gbuf.at[src], send.at[k], recv.at[k], (x, y, c)).wait_recv()
        for cp in sent:
            cp.wait_send()
        if reduce:
            acc = gbuf[0]
            for n in range(1, 8):
                acc = acc + gbuf[n]
            o_ref[...] = acc

    vmem = pl.BlockSpec(memory_space=pltpu.VMEM)
    sems = [pltpu.SemaphoreType.DMA((7,)), pltpu.SemaphoreType.DMA((7,))]
    if reduce:
        out_shape = jax.ShapeDtypeStruct((r, cols), F32)
        scratch = [pltpu.VMEM((8, r, cols), F32)] + sems
    else:
        out_shape = jax.ShapeDtypeStruct((8, r, cols), F32)
        scratch = sems
    return pl.pallas_call(body, name=name, in_specs=[vmem], out_specs=vmem, out_shape=out_shape,
                          scratch_shapes=scratch)(v)


def _rs_swap(g, name):
    _, _, rh, cols = g.shape

    def body(g_ref, o_ref, send, recv):
        x, y, c = _me()
        cps = [_remote(g_ref.at[q, 1 - c], o_ref.at[q], send.at[q], recv.at[q], (x, y, 1 - c)) for q in range(N_CHIPS)]
        for cp in cps:
            cp.start()
        for cp in cps:
            cp.wait()

    return pl.pallas_call(
        body, name=name, in_specs=[ANY], out_specs=ANY,
        out_shape=jax.ShapeDtypeStruct((N_CHIPS, rh, cols), F32),
        scratch_shapes=[pltpu.SemaphoreType.DMA((N_CHIPS,)), pltpu.SemaphoreType.DMA((N_CHIPS,))],
    )(g)


def _row_tile(rows, pref):
    if rows <= pref:
        return rows
    t = (pref // 8) * 8
    while t >= 8:
        if rows % t == 0:
            return t
        t -= 8
    return rows


def _rs_add(g, sib, c, name):
    _, _, rh, cols = g.shape
    tr = _row_tile(rh, 512)

    def body(c_ref, g_ref, s_ref, o_ref):
        o_ref[...] = (g_ref[...] + s_ref[...]).astype(BF16)

    grid_spec = pltpu.PrefetchScalarGridSpec(
        num_scalar_prefetch=1, grid=(N_CHIPS, rh // tr),
        in_specs=[pl.BlockSpec((None, None, tr, cols), lambda q, i, cr: (q, cr[0], i, 0)),
                  pl.BlockSpec((None, tr, cols), lambda q, i, cr: (q, i, 0))],
        out_specs=pl.BlockSpec((None, tr, cols), lambda q, i, cr: (q, i, 0)),
    )
    return pl.pallas_call(
        body, name=name, grid_spec=grid_spec,
        out_shape=jax.ShapeDtypeStruct((N_CHIPS, rh, cols), BF16),
        compiler_params=_cparams(("parallel", "parallel")),
    )(c, g, sib)


def _rs_ici(a, name):
    _, rh, cols = a.shape

    def body(a_ref, o_ref, send, recv):
        x, y, c = _me()
        p = 2 * x + y
        cps = []
        for k, (cx, cy) in enumerate(_other_chips(x, y)):
            cp = _remote(a_ref.at[2 * cx + cy], o_ref.at[p], send.at[k], recv.at[k], (cx, cy, c))
            cp.start()
            cps.append(cp)
        for k, (cx, cy) in enumerate(_other_chips(x, y)):
            q = 2 * cx + cy
            _remote(a_ref.at[q], o_ref.at[q], send.at[k], recv.at[k], (cx, cy, c)).wait_recv()
        for cp in cps:
            cp.wait_send()

    return pl.pallas_call(
        body, name=name, in_specs=[ANY], out_specs=ANY,
        out_shape=jax.ShapeDtypeStruct((N_CHIPS, rh, cols), a.dtype),
        scratch_shapes=[pltpu.SemaphoreType.DMA((3,)), pltpu.SemaphoreType.DMA((3,))],
    )(a)


def _rs_sum(rb, a, chip, name):
    _, rh, cols = rb.shape
    tr = _row_tile(rh, 512)

    def body(p_ref, r0, r1, r2, r3, own, o_ref):
        p = p_ref[0]
        ownv = own[...].astype(F32)
        acc = None
        for q, r in enumerate((r0, r1, r2, r3)):
            v = jnp.where(p == q, ownv, r[...].astype(F32))
            acc = v if acc is None else acc + v
        o_ref[...] = acc

    def spec(q):
        return pl.BlockSpec((None, tr, cols), lambda i, pr: (jnp.where(pr[0] == q, (q + 1) % N_CHIPS, q), i, 0))

    grid_spec = pltpu.PrefetchScalarGridSpec(
        num_scalar_prefetch=1, grid=(rh // tr,),
        in_specs=[spec(0), spec(1), spec(2), spec(3), pl.BlockSpec((None, tr, cols), lambda i, pr: (pr[0], i, 0))],
        out_specs=pl.BlockSpec((tr, cols), lambda i, pr: (i, 0)),
    )
    return pl.pallas_call(
        body, name=name, grid_spec=grid_spec,
        out_shape=jax.ShapeDtypeStruct((rh, cols), F32),
        compiler_params=_cparams(("parallel",)),
    )(chip, rb, rb, rb, rb, a)


def _rs_share(hsum, name):
    rh, cols = hsum.shape

    def body(h_ref, o_ref, send, recv):
        x, y, c = _me()
        cp = _remote(h_ref, o_ref, send, recv, (x, y, 1 - c))
        cp.start()
        cp.wait()

    return pl.pallas_call(
        body, name=name, in_specs=[ANY], out_specs=ANY,
        out_shape=jax.ShapeDtypeStruct((rh, cols), F32),
        scratch_shapes=[pltpu.SemaphoreType.DMA, pltpu.SemaphoreType.DMA],
    )(hsum)


def _adamw(w, g, m, v, name):
    rows, cols = w.shape
    tr = _row_tile(rows, 512)

    def body(w_ref, g_ref, m_ref, v_ref, d_ref, nm_ref, nv_ref):
        gv = g_ref[...]
        nm = ADAM_B1 * m_ref[...] + (1.0 - ADAM_B1) * gv
        nv = ADAM_B2 * v_ref[...] + (1.0 - ADAM_B2) * (gv * gv)
        m_hat = nm / (1.0 - ADAM_B1 ** ADAM_STEP)
        v_hat = nv / (1.0 - ADAM_B2 ** ADAM_STEP)
        d_ref[...] = -ADAM_LR * (m_hat / (jnp.sqrt(v_hat) + ADAM_EPS) + ADAM_WD * w_ref[...])
        nm_ref[...] = nm
        nv_ref[...] = nv

    spec = pl.BlockSpec((tr, cols), lambda i: (i, 0))
    shp = jax.ShapeDtypeStruct((rows, cols), F32)
    return pl.pallas_call(
        body, name=name, grid=(rows // tr,), in_specs=[spec] * 4, out_specs=[spec] * 3, out_shape=[shp] * 3,
        compiler_params=_cparams(("parallel",)),
    )(w, g, m, v)


WEIGHTS = ("mix_norm_pre", "mix_norm_post", "ffn_norm_pre", "ffn_norm_post", "ev_w_in", "ev_a_conv_w", "ev_a_conv_b",
           "ev_a_ln_g", "ev_a_ln_b", "ev_sinks", "ev_w_out", "od_w_in", "od_conv_w", "od_w_out", "ffn_w_up",
           "ffn_conv_w", "ffn_w_down")
MATS = (("ev_w_in", 2), ("ev_w_out", 1), ("od_w_in", 2), ("od_w_out", 1), ("ffn_w_up", 2), ("ffn_w_down", 1))
SMALL_SHARDED = ("ev_a_conv_w", "od_conv_w", "ffn_conv_w")
REPLICATED = ("mix_norm_pre", "mix_norm_post", "ffn_norm_pre", "ffn_norm_post", "ev_a_conv_b", "ev_a_ln_g",
              "ev_a_ln_b", "ev_sinks")


def _pack(parts, rows_multiple):
    flat = jnp.concatenate([p.reshape(-1) for p in parts])
    unit = rows_multiple * PACK_COLS
    pad = (-flat.shape[0]) % unit
    if pad:
        flat = jnp.concatenate([flat, jnp.zeros((pad,), flat.dtype)])
    return flat.reshape(-1, PACK_COLS)


def _unpack(buf, shapes):
    flat = buf.reshape(-1)
    out, off = [], 0
    for shp in shapes:
        n = 1
        for d in shp:
            n *= d
        out.append(flat[off:off + n].reshape(shp))
        off += n
    return out


def _shard_rows(shard, axis):
    if axis == 2:
        shard = jnp.swapaxes(shard, 1, 2)
    return shard.reshape(-1, PACK_COLS)


def kernel(x, positions, mix_norm_pre, mix_norm_post, ffn_norm_pre, ffn_norm_post, ev_w_in, ev_a_conv_w, ev_a_conv_b, ev_a_ln_g, ev_a_ln_b, ev_sinks, ev_w_out, od_w_in, od_conv_w, od_w_out, ffn_w_up, ffn_conv_w, ffn_w_down, loss_target, m_mix_norm_pre, m_mix_norm_post, m_ffn_norm_pre, m_ffn_norm_post, m_ev_w_in, m_ev_a_conv_w, m_ev_a_conv_b, m_ev_a_ln_g, m_ev_a_ln_b, m_ev_sinks, m_ev_w_out, m_od_w_in, m_od_conv_w, m_od_w_out, m_ffn_w_up, m_ffn_conv_w, m_ffn_w_down, v_mix_norm_pre, v_mix_norm_post, v_ffn_norm_pre, v_ffn_norm_post, v_ev_w_in, v_ev_a_conv_w, v_ev_a_conv_b, v_ev_a_ln_g, v_ev_a_ln_b, v_ev_sinks, v_ev_w_out, v_od_w_in, v_od_conv_w, v_od_w_out, v_ffn_w_up, v_ffn_conv_w, v_ffn_w_down):
    wts = dict(zip(WEIGHTS, (mix_norm_pre, mix_norm_post, ffn_norm_pre, ffn_norm_post, ev_w_in, ev_a_conv_w, ev_a_conv_b,
                             ev_a_ln_g, ev_a_ln_b, ev_sinks, ev_w_out, od_w_in, od_conv_w, od_w_out, ffn_w_up, ffn_conv_w,
                             ffn_w_down)))
    mom = dict(zip(WEIGHTS, (m_mix_norm_pre, m_mix_norm_post, m_ffn_norm_pre, m_ffn_norm_post, m_ev_w_in, m_ev_a_conv_w,
                             m_ev_a_conv_b, m_ev_a_ln_g, m_ev_a_ln_b, m_ev_sinks, m_ev_w_out, m_od_w_in, m_od_conv_w,
                             m_od_w_out, m_ffn_w_up, m_ffn_conv_w, m_ffn_w_down)))
    var = dict(zip(WEIGHTS, (v_mix_norm_pre, v_mix_norm_post, v_ffn_norm_pre, v_ffn_norm_post, v_ev_w_in, v_ev_a_conv_w,
                             v_ev_a_conv_b, v_ev_a_ln_g, v_ev_a_ln_b, v_ev_sinks, v_ev_w_out, v_od_w_in, v_od_conv_w,
                             v_od_w_out, v_ffn_w_up, v_ffn_conv_w, v_ffn_w_down)))
    xi, yi, ci = _me()
    chip = 2 * xi + yi

    mat_rows = [_shard_rows(wts[k].astype(BF16), axis) for k, axis in MATS]
    wp = jnp.concatenate(mat_rows, axis=0)
    gathered = lax.dynamic_update_slice(_gather_chips(wp, "gather_mats"), wp[None], (chip, 0, 0))
    small_shapes = [wts[k].shape for k in SMALL_SHARDED]
    small_all = _exchange8(_pack([wts[k] for k in SMALL_SHARDED], 8), False, "gather_small")
    w = {k: wts[k] for k in REPLICATED}
    off = 0
    for (k, axis), blk in zip(MATS, mat_rows):
        n, layers = blk.shape[0], wts[k].shape[0]
        full = gathered[:, off:off + n].reshape(N_CHIPS, layers, n // layers, PACK_COLS)
        full = jnp.swapaxes(full, 0, 1).reshape(layers, N_CHIPS * (n // layers), PACK_COLS)
        off += n
        w[k + "_t" if axis == 2 else k] = full
        w[k if axis == 2 else k + "_t"] = jnp.swapaxes(full, 1, 2)
    per_chip = [_unpack(small_all[2 * q], small_shapes) for q in range(N_CHIPS)]
    for n, k in enumerate(SMALL_SHARDED):
        w[k] = jnp.concatenate([per_chip[q][n] for q in range(N_CHIPS)], axis=-1)
    for k in ("ev_w_in", "ev_w_out", "od_w_in", "od_w_out"):
        w[k] = w[k][0]
        w[k + "_t"] = w[k + "_t"][0]
    for k in ("ev_a_conv_w", "od_conv_w"):
        w[k] = w[k][0]
    w["ev_sinks"] = w["ev_sinks"][0]

    sq, dx, g = _local_step(x[0], positions[0], loss_target[0], w)
    loss = lax.psum(0.5 * jnp.sum(sq) / D_MODEL, ("x", "y", "c"))

    parts = []
    for k, axis in MATS:
        layers = wts[k].shape[0]
        gk = g[k].reshape(layers, N_CHIPS, -1, PACK_COLS)
        parts.append(jnp.swapaxes(gk, 0, 1).reshape(N_CHIPS, -1, PACK_COLS))
    gp = jnp.concatenate(parts, axis=1)
    rows = gp.shape[1]
    gp = gp.reshape(N_CHIPS, 2, rows // 2, PACK_COLS)
    core = jnp.reshape(ci, (1,)).astype(jnp.int32)
    sib = _rs_swap(gp, "rs_swap")
    pair = _rs_add(gp, sib, core, "rs_add")
    landed = _rs_ici(pair, "rs_ici")
    half = _rs_sum(landed, pair, jnp.reshape(chip, (1,)).astype(jnp.int32), "rs_sum")
    other = _rs_share(half, "rs_share")
    red = jnp.concatenate([jnp.where(ci == 0, half, other), jnp.where(ci == 0, other, half)], axis=0)
    grads = {}
    off = 0
    for (k, axis), blk in zip(MATS, mat_rows):
        n, shp = blk.shape[0], wts[k].shape
        part = red[off:off + n]
        off += n
        grads[k] = jnp.swapaxes(part.reshape(shp[0], shp[2], shp[1]), 1, 2) if axis == 2 else part.reshape(shp)

    small_keys = REPLICATED + SMALL_SHARDED
    full_shapes = [wts[k].shape for k in REPLICATED] + [wts[k].shape[:-1] + (wts[k].shape[-1] * N_CHIPS,) for k in SMALL_SHARDED]
    sm = _exchange8(_pack([g[k] for k in small_keys], 8), True, "reduce_small")
    for k, full in zip(small_keys, _unpack(sm, full_shapes)):
        if k in SMALL_SHARDED:
            n = wts[k].shape[-1]
            full = lax.dynamic_slice_in_dim(full, chip * n, n, axis=full.ndim - 1)
        grads[k] = full

    deltas, new_m, new_v = {}, {}, {}
    for k in WEIGHTS:
        shp = wts[k].shape
        two_d = (-1, shp[-1])
        d, nm, nv = _adamw(wts[k].reshape(two_d), grads[k].reshape(two_d), mom[k].reshape(two_d), var[k].reshape(two_d),
                           "adamw_" + k)
        deltas[k], new_m[k], new_v[k] = d.reshape(shp), nm.reshape(shp), nv.reshape(shp)

    return (loss, dx[None], *[grads[k] for k in WEIGHTS], *[deltas[k] for k in WEIGHTS],
            *[new_m[k] for k in WEIGHTS], *[new_v[k] for k in WEIGHTS])
```

```python
import functools

import jax
import jax.numpy as jnp
import numpy as np
from jax import lax
from jax.experimental import pallas as pl
from jax.experimental.pallas import tpu as pltpu

F32 = jnp.float32
BF16 = jnp.bfloat16
MESH = pl.DeviceIdType.MESH

D_MODEL = 1024
HEAD_DIM = 64
A_CH = 512
A_CONV = 31
N_Q_HEADS = 8
WINDOW = 128
ROPE_THETA = 500000.0
ROPE_DIM = 16
D_FF = 2816
RMS_EPS = 1e-6
LN_EPS = 1e-5
ADAM_LR = 0.001
ADAM_B1 = 0.9
ADAM_B2 = 0.999
ADAM_EPS = 1e-08
ADAM_WD = 0.01
ADAM_STEP = 10

LANES = 128
HALO16 = 16
HALO32 = 32
VMEM_LIMIT = 56 * 1024 * 1024
FFN_BWD_VMEM = 60 * 1024 * 1024
N_CHIPS = 4


def _cparams(sem):
    return pltpu.CompilerParams(dimension_semantics=sem, vmem_limit_bytes=VMEM_LIMIT)


def _tile(n, pref):
    if n <= pref:
        return n
    t = (pref // LANES) * LANES
    while t >= LANES:
        if n % t == 0:
            return t
        t -= LANES
    return n


MM_ROWS = 512


def _rms_scale(v):
    return lax.rsqrt(jnp.mean(v * v, axis=-1, keepdims=True) + RMS_EPS)


def _rms_bwd(dy, v, g):
    r = _rms_scale(v)
    nrm = v * r
    dn = dy * g
    return r * (dn - nrm * jnp.mean(dn * nrm, axis=-1, keepdims=True)), jnp.sum(dy * nrm, axis=0, keepdims=True)


def _mm_pre(x, g, w, name):
    s, d = x.shape
    n = w.shape[1]
    tm = min(MM_ROWS, s)
    tn = _tile(n, 1408)

    def body(x_ref, g_ref, w_ref, h_ref, z_ref):
        @pl.when(pl.program_id(1) == 0)
        def _():
            xv = x_ref[...]
            h_ref[...] = (xv * _rms_scale(xv) * g_ref[...]).astype(BF16)

        z_ref[...] = jnp.dot(h_ref[...], w_ref[...], preferred_element_type=F32).astype(BF16)

    return pl.pallas_call(
        body, name=name, grid=(s // tm, n // tn),
        in_specs=[pl.BlockSpec((tm, d), lambda i, j: (i, 0)), pl.BlockSpec((1, d), lambda i, j: (0, 0)),
                  pl.BlockSpec((d, tn), lambda i, j: (0, j))],
        out_specs=[pl.BlockSpec((tm, d), lambda i, j: (i, 0)), pl.BlockSpec((tm, tn), lambda i, j: (i, j))],
        out_shape=[jax.ShapeDtypeStruct((s, d), BF16), jax.ShapeDtypeStruct((s, n), BF16)],
        compiler_params=_cparams(("parallel", "arbitrary")),
    )(x, g, w)


def _mm_post(a, w, g, xres, name):
    s, k = a.shape
    d = w.shape[1]
    tm = min(MM_ROWS, s)

    def body(a_ref, w_ref, g_ref, x_ref, m_ref, o_ref):
        mv = jnp.dot(a_ref[...], w_ref[...], preferred_element_type=F32)
        m_ref[...] = mv
        o_ref[...] = x_ref[...] + mv * _rms_scale(mv) * g_ref[...]

    row = pl.BlockSpec((tm, d), lambda i: (i, 0))
    return pl.pallas_call(
        body, name=name, grid=(s // tm,),
        in_specs=[pl.BlockSpec((tm, k), lambda i: (i, 0)), _full((k, d)), _full((1, d)), row],
        out_specs=[row, row],
        out_shape=[jax.ShapeDtypeStruct((s, d), F32), jax.ShapeDtypeStruct((s, d), F32)],
        compiler_params=_cparams(("parallel",)),
    )(a, w, g, xres)


def _mm_post_bwd(dy, m, g, w_t, name):
    s, d = m.shape
    k = w_t.shape[1]
    tm = min(MM_ROWS, s)

    def body(dy_ref, m_ref, g_ref, wt_ref, dm_ref, da_ref, dg_ref):
        @pl.when(pl.program_id(0) == 0)
        def _():
            dg_ref[...] = jnp.zeros_like(dg_ref)

        dm, dg = _rms_bwd(dy_ref[...], m_ref[...], g_ref[...])
        dg_ref[...] += dg
        dmb = dm.astype(BF16)
        dm_ref[...] = dmb
        da_ref[...] = jnp.dot(dmb, wt_ref[...], preferred_element_type=F32)

    row = pl.BlockSpec((tm, d), lambda i: (i, 0))
    return pl.pallas_call(
        body, name=name, grid=(s // tm,),
        in_specs=[row, row, _full((1, d)), _full((d, k))],
        out_specs=[row, pl.BlockSpec((tm, k), lambda i: (i, 0)), _full((1, d))],
        out_shape=[jax.ShapeDtypeStruct((s, d), BF16), jax.ShapeDtypeStruct((s, k), F32),
                   jax.ShapeDtypeStruct((1, d), F32)],
        compiler_params=_cparams(("arbitrary",)),
    )(dy, m, g, w_t)


def _mm_pre_bwd(dz, w_t, x, g, res, name):
    s, k = dz.shape
    d = w_t.shape[1]
    tm = min(MM_ROWS, s)

    def body(dz_ref, wt_ref, x_ref, g_ref, res_ref, dx_ref, dg_ref):
        @pl.when(pl.program_id(0) == 0)
        def _():
            dg_ref[...] = jnp.zeros_like(dg_ref)

        dh = jnp.dot(dz_ref[...], wt_ref[...], preferred_element_type=F32)
        dx, dg = _rms_bwd(dh, x_ref[...], g_ref[...])
        dg_ref[...] += dg
        dx_ref[...] = res_ref[...] + dx

    row = pl.BlockSpec((tm, d), lambda i: (i, 0))
    return pl.pallas_call(
        body, name=name, grid=(s // tm,),
        in_specs=[pl.BlockSpec((tm, k), lambda i: (i, 0)), _full((k, d)), row, _full((1, d)), row],
        out_specs=[row, _full((1, d))],
        out_shape=[jax.ShapeDtypeStruct((s, d), F32), jax.ShapeDtypeStruct((1, d), F32)],
        compiler_params=_cparams(("arbitrary",)),
    )(dz, w_t, x, g, res)


def _mm_tn(a, b, name):
    s, k = a.shape
    _, n = b.shape
    tk = _tile(k, 1408)
    tn = _tile(n, 1408)
    ts = min(512, s)

    def body(a_ref, b_ref, o_ref):
        @pl.when(pl.program_id(2) == 0)
        def _():
            o_ref[...] = jnp.zeros_like(o_ref)

        o_ref[...] += lax.dot_general(a_ref[...], b_ref[...], (((0,), (0,)), ((), ())),
                                      preferred_element_type=F32)

    return pl.pallas_call(
        body, name=name, grid=(k // tk, n // tn, s // ts),
        in_specs=[pl.BlockSpec((ts, tk), lambda i, j, l: (l, i)), pl.BlockSpec((ts, tn), lambda i, j, l: (l, j))],
        out_specs=pl.BlockSpec((tk, tn), lambda i, j, l: (i, j)),
        out_shape=jax.ShapeDtypeStruct((k, n), F32),
        compiler_params=_cparams(("parallel", "parallel", "arbitrary")),
    )(a, b)


def _loss_bwd(y, target, name):
    s, d = y.shape
    tr = min(512, s)

    def body(y_ref, t_ref, acc_ref, dy_ref):
        @pl.when(pl.program_id(0) == 0)
        def _():
            acc_ref[...] = jnp.zeros_like(acc_ref)

        e = y_ref[...] - t_ref[...]
        dy_ref[...] = e * (1.0 / d)
        acc_ref[...] += jnp.sum(e * e, axis=0, keepdims=True)

    row = pl.BlockSpec((tr, d), lambda i: (i, 0))
    vec = pl.BlockSpec((1, d), lambda i: (0, 0))
    return pl.pallas_call(
        body, name=name, grid=(s // tr,),
        in_specs=[row, row], out_specs=[vec, row],
        out_shape=[jax.ShapeDtypeStruct((1, d), F32), jax.ShapeDtypeStruct((s, d), F32)],
        compiler_params=_cparams(("arbitrary",)),
    )(y, target)


def _cur(tr, w, col=0):
    return pl.BlockSpec((tr, w), lambda i: (i, col))


def _prev(tr, h, w, col=0):
    return pl.BlockSpec((h, w), lambda i: (jnp.maximum(i * (tr // h) - 1, 0), col))


def _next(tr, h, w, nrows, col=0):
    last = nrows // h - 1
    return pl.BlockSpec((h, w), lambda i: (jnp.minimum((i + 1) * (tr // h), last), col))


def _full(shape):
    return pl.BlockSpec(shape, lambda i: tuple(0 for _ in shape))


def _silu_parts(g):
    sig = jax.nn.sigmoid(g)
    return sig, g * sig


FFN_CW = 256
FFN_NBUF = 3


def _conv3_taps(buf, w, off, rows):
    return (w[0:1] * buf[pl.ds(off, rows), :] + w[1:2] * buf[pl.ds(off + 1, rows), :]
            + w[2:3] * buf[pl.ds(off + 2, rows), :])


WHOLE_VMEM = pl.BlockSpec(memory_space=pltpu.VMEM)


def _ffn_fwd(x, g_pre, wu, conv_w, wd, g_post, name):
    s, d = x.shape
    f2 = wu.shape[1]
    f = f2 // 2
    tr = min(256, s)
    h = HALO16
    cw = FFN_CW

    def body(x_ref, gpre_ref, wu_ref, cw_ref, wd_ref, gpost_ref, xo_ref, f_ref, h_ref, up_ref, u_ref,
             carry, gbuf, vbuf, facc):
        @pl.when(pl.program_id(0) == 0)
        def _():
            carry[...] = jnp.zeros_like(carry)

        xv = x_ref[...]
        r = lax.rsqrt(jnp.mean(xv * xv, axis=-1, keepdims=True) + RMS_EPS)
        hv = (xv * r * gpre_ref[...]).astype(BF16)
        h_ref[...] = hv
        nchunk = f // cw

        def up_proj(j):
            for buf, base in ((gbuf, 0), (vbuf, f)):
                cs = slice(base + j * cw, base + (j + 1) * cw)
                dst = buf.at[j % FFN_NBUF]
                upc = jnp.dot(hv, wu_ref[:, cs], preferred_element_type=F32)
                up_ref[:, cs] = upc.astype(BF16)
                dst[0:h, :] = carry[:, cs]
                dst[h:h + tr, :] = upc
                carry[:, cs] = upc[tr - h:tr, :]

        def down_proj(j, act):
            part = jnp.dot(act, wd_ref[j * cw:(j + 1) * cw, :], preferred_element_type=F32)
            if j == 0:
                facc[...] = part
            else:
                facc[...] += part

        up_proj(0)
        pending = None
        for j in range(nchunk):
            cg = slice(j * cw, (j + 1) * cw)
            cv = slice(f + j * cw, f + (j + 1) * cw)
            if j + 1 < nchunk:
                up_proj(j + 1)
            if pending is not None:
                down_proj(*pending)
            g = _conv3_taps(gbuf.at[j % FFN_NBUF], cw_ref[:, cg], h - 2, tr)
            v = _conv3_taps(vbuf.at[j % FFN_NBUF], cw_ref[:, cv], h - 2, tr)
            u_ref[:, cg] = g.astype(BF16)
            u_ref[:, cv] = v.astype(BF16)
            act = (g * jax.nn.sigmoid(g) * v).astype(BF16)
            pending = (j, act)
        down_proj(*pending)
        fv = facc[...]
        f_ref[...] = fv
        r2 = lax.rsqrt(jnp.mean(fv * fv, axis=-1, keepdims=True) + RMS_EPS)
        xo_ref[...] = xv + fv * r2 * gpost_ref[...]

    row = _cur(tr, d)
    wide = _cur(tr, f2)
    return pl.pallas_call(
        body, name=name, grid=(s // tr,),
        in_specs=[row, _full((1, d)), WHOLE_VMEM, _full((3, f2)), WHOLE_VMEM, _full((1, d))],
        out_specs=[row, row, row, wide, wide],
        out_shape=[jax.ShapeDtypeStruct((s, d), F32), jax.ShapeDtypeStruct((s, d), F32),
                   jax.ShapeDtypeStruct((s, d), BF16), jax.ShapeDtypeStruct((s, f2), BF16),
                   jax.ShapeDtypeStruct((s, f2), BF16)],
        scratch_shapes=[pltpu.VMEM((h, f2), F32), pltpu.VMEM((FFN_NBUF, h + tr, cw), F32),
                        pltpu.VMEM((FFN_NBUF, h + tr, cw), F32), pltpu.VMEM((tr, d), F32)],
        compiler_params=_cparams(("arbitrary",)),
    )(x, g_pre, wu, conv_w, wd, g_post)


def _ffn_bwd(dxo, fout, x, up, u, g_pre, g_post, wd_t, wu_t, conv_w, name):
    s, d = x.shape
    f2 = up.shape[1]
    f = f2 // 2
    tr = min(256, s)
    nt = s // tr
    h = HALO16
    cw = FFN_CW

    def body(dy_ref, f_ref, x_ref, up_ref, u_ref, gpre_ref, gpost_ref, wdt_ref, wut_ref, cw_ref,
             dx_ref, dup_ref, act_ref, df_ref, dcw_ref, dgpost_ref, dgpre_ref, carry, dgbuf, dvbuf, dhacc):
        @pl.when(pl.program_id(0) == 0)
        def _():
            carry[...] = jnp.zeros_like(carry)
            dcw_ref[...] = jnp.zeros_like(dcw_ref)
            dgpost_ref[...] = jnp.zeros_like(dgpost_ref)
            dgpre_ref[...] = jnp.zeros_like(dgpre_ref)

        dy = dy_ref[...]
        fv = f_ref[...]
        r = lax.rsqrt(jnp.mean(fv * fv, axis=-1, keepdims=True) + RMS_EPS)
        nrm = fv * r
        dn = dy * gpost_ref[...]
        dfv = (r * (dn - nrm * jnp.mean(dn * nrm, axis=-1, keepdims=True))).astype(BF16)
        dgpost_ref[...] += jnp.sum(dy * nrm, axis=0, keepdims=True)
        df_ref[...] = dfv
        nchunk = f // cw

        def dh_part(dupb, cs, first):
            part = jnp.dot(dupb, wut_ref[cs, :], preferred_element_type=F32)
            if first:
                dhacc[...] = part
            else:
                dhacc[...] += part

        dact_next = jnp.dot(dfv, wdt_ref[:, 0:cw], preferred_element_type=F32)
        for j in range(nchunk):
            ch = slice(j * cw, (j + 1) * cw)
            cg = ch
            cv = slice(f + j * cw, f + (j + 1) * cw)
            dact = dact_next
            if j + 1 < nchunk:
                dact_next = jnp.dot(dfv, wdt_ref[:, (j + 1) * cw:(j + 2) * cw], preferred_element_type=F32)
            g = u_ref[:, cg].astype(F32)
            v = u_ref[:, cv].astype(F32)
            sig, sil = _silu_parts(g)
            act_ref[:, ch] = (sil * v).astype(BF16)
            du_g = dact * v * (sig * (1.0 + g * (1.0 - sig)))
            du_v = dact * sil
            for k, (dbuf, du, cs) in enumerate(((dgbuf.at[j % FFN_NBUF], du_g, cg), (dvbuf.at[j % FFN_NBUF], du_v, cv))):
                dbuf[0:tr, :] = du
                dbuf[tr:tr + h, :] = carry[:, cs]
                carry[:, cs] = du[0:h, :]
                w = cw_ref[:, cs]
                xin = up_ref[:, cs].astype(F32)
                acc = None
                for sh in range(3):
                    dsh = dbuf[pl.ds(sh, tr), :]
                    term = w[2 - sh:3 - sh] * dsh
                    acc = term if acc is None else acc + term
                    dcw_ref[2 - sh:3 - sh, cs] += jnp.sum(xin * dsh, axis=0, keepdims=True)
                dupb = acc.astype(BF16)
                dup_ref[:, cs] = dupb
                dh_part(dupb, cs, j == 0 and k == 0)
        dh = dhacc[...]
        xv = x_ref[...]
        r1 = lax.rsqrt(jnp.mean(xv * xv, axis=-1, keepdims=True) + RMS_EPS)
        n1 = xv * r1
        dn1 = dh * gpre_ref[...]
        dx_ref[...] = dy + r1 * (dn1 - n1 * jnp.mean(dn1 * n1, axis=-1, keepdims=True))
        dgpre_ref[...] += jnp.sum(dh * n1, axis=0, keepdims=True)

    def rev(w):
        return pl.BlockSpec((tr, w), lambda i: (nt - 1 - i, 0))

    vec = _full((1, d))
    return pl.pallas_call(
        body, name=name, grid=(nt,),
        in_specs=[rev(d), rev(d), rev(d), rev(f2), rev(f2), vec, vec, WHOLE_VMEM, WHOLE_VMEM, _full((3, f2))],
        out_specs=[rev(d), rev(f2), rev(f), rev(d), _full((3, f2)), vec, vec],
        out_shape=[jax.ShapeDtypeStruct((s, d), F32), jax.ShapeDtypeStruct((s, f2), BF16),
                   jax.ShapeDtypeStruct((s, f), BF16), jax.ShapeDtypeStruct((s, d), BF16),
                   jax.ShapeDtypeStruct((3, f2), F32), jax.ShapeDtypeStruct((1, d), F32),
                   jax.ShapeDtypeStruct((1, d), F32)],
        scratch_shapes=[pltpu.VMEM((h, f2), F32), pltpu.VMEM((FFN_NBUF, tr + h, cw), F32),
                        pltpu.VMEM((FFN_NBUF, tr + h, cw), F32), pltpu.VMEM((tr, d), F32)],
        compiler_params=pltpu.CompilerParams(dimension_semantics=("arbitrary",), vmem_limit_bytes=FFN_BWD_VMEM),
    )(dxo, fout, x, up, u, g_pre, g_post, wd_t, wu_t, conv_w)


def _od_gate_fwd(z, conv_w, name):
    s, d3 = z.shape
    d = d3 // 3
    tr = min(256, s)
    h = HALO16
    cw = FFN_CW

    def body(z_ref, prev_ref, w_ref, o_ref, buf):
        first = pl.program_id(0) == 0
        for j in range(d // cw):
            cb = slice(j * cw, (j + 1) * cw)
            cc = slice(d + j * cw, d + (j + 1) * cw)
            cu = slice(2 * d + j * cw, 2 * d + (j + 1) * cw)
            buf[0:h, :] = jnp.where(first, 0.0, prev_ref[:, cc].astype(F32) * prev_ref[:, cu].astype(F32))
            buf[h:h + tr, :] = z_ref[:, cc].astype(F32) * z_ref[:, cu].astype(F32)
            k = _conv3_taps(buf, w_ref[:, cb], h - 2, tr)
            o_ref[:, cb] = (z_ref[:, cb].astype(F32) * k).astype(BF16)

    return pl.pallas_call(
        body, name=name, grid=(s // tr,),
        in_specs=[_cur(tr, d3), _prev(tr, h, d3), _full((3, d))],
        out_specs=_cur(tr, d),
        out_shape=jax.ShapeDtypeStruct((s, d), BF16),
        scratch_shapes=[pltpu.VMEM((h + tr, cw), F32)],
        compiler_params=_cparams(("parallel",)),
    )(z, z, conv_w)


def _od_gate_bwd(dy, z, conv_w, name):
    s, d3 = z.shape
    d = d3 // 3
    tr = min(256, s)
    h = HALO16
    cw = FFN_CW
    ext = tr + h

    def body(dy_ref, dyn_ref, z_ref, zp_ref, zn_ref, w_ref, o_ref, dw_ref, buf, dbuf):
        i = pl.program_id(0)
        first = i == 0
        last = i == pl.num_programs(0) - 1

        @pl.when(first)
        def _():
            dw_ref[...] = jnp.zeros_like(dw_ref)

        for j in range(d // cw):
            cb = slice(j * cw, (j + 1) * cw)
            cc = slice(d + j * cw, d + (j + 1) * cw)
            cu = slice(2 * d + j * cw, 2 * d + (j + 1) * cw)
            w = w_ref[:, cb]
            cval = z_ref[:, cc].astype(F32)
            uval = z_ref[:, cu].astype(F32)
            buf[0:h, :] = jnp.where(first, 0.0, zp_ref[:, cc].astype(F32) * zp_ref[:, cu].astype(F32))
            buf[h:h + tr, :] = cval * uval
            k = _conv3_taps(buf, w, h - 2, tr)
            dyv = dy_ref[:, cb]
            o_ref[:, cb] = (dyv * k).astype(BF16)
            dbuf[0:tr, :] = dyv * z_ref[:, cb].astype(F32)
            dbuf[tr:ext, :] = jnp.where(last, 0.0, dyn_ref[:, cb] * zn_ref[:, cb].astype(F32))
            dcu = w[2:3] * dbuf[pl.ds(0, tr), :] + w[1:2] * dbuf[pl.ds(1, tr), :] + w[0:1] * dbuf[pl.ds(2, tr), :]
            o_ref[:, cc] = (dcu * uval).astype(BF16)
            o_ref[:, cu] = (dcu * cval).astype(BF16)
            dk = dbuf[pl.ds(0, tr), :]
            for t in range(3):
                dw_ref[t:t + 1, cb] += jnp.sum(dk * buf[pl.ds(h - 2 + t, tr), :], axis=0, keepdims=True)

    return pl.pallas_call(
        body, name=name, grid=(s // tr,),
        in_specs=[_cur(tr, d), _next(tr, h, d, s), _cur(tr, d3), _prev(tr, h, d3), _next(tr, h, d3, s), _full((3, d))],
        out_specs=[_cur(tr, d3), _full((3, d))],
        out_shape=[jax.ShapeDtypeStruct((s, d3), BF16), jax.ShapeDtypeStruct((3, d), F32)],
        scratch_shapes=[pltpu.VMEM((h + tr, cw), F32), pltpu.VMEM((ext, cw), F32)],
        compiler_params=_cparams(("arbitrary",)),
    )(dy, dy, z, z, z, conv_w)


Q0 = 2 * A_CH
K0 = Q0 + N_Q_HEADS * HEAD_DIM
V0 = K0 + 2 * HEAD_DIM
EVEN_IN = V0 + 2 * HEAD_DIM


def _rope_tables(positions):
    half = ROPE_DIM // 2
    inv_freq = ROPE_THETA ** (-(jnp.arange(half, dtype=F32) * 2.0 / ROPE_DIM))
    ang = positions.astype(F32)[:, None] * inv_freq
    cs = jnp.concatenate([jnp.cos(ang), jnp.sin(ang)], axis=1)
    spread = np.zeros((2 * half, 3 * LANES), np.float32)
    const = np.zeros((1, 3 * LANES), np.float32)
    for lane in range(3 * LANES):
        dim, part = lane % HEAD_DIM, lane // LANES
        if part == 0:
            if dim < ROPE_DIM:
                spread[dim % half, lane] = 1.0
            else:
                const[0, lane] = 1.0
        elif part == 1 and half <= dim < ROPE_DIM:
            spread[half + dim - half, lane] = 1.0
        elif part == 2 and dim < half:
            spread[half + dim, lane] = -1.0
    return jnp.dot(cs, jnp.asarray(spread), precision=lax.Precision.HIGHEST) + jnp.asarray(const)


def _rope_fwd(x, tab):
    c, sa, sb = tab[:, 0:LANES], tab[:, LANES:2 * LANES], tab[:, 2 * LANES:3 * LANES]
    return x * c + pltpu.roll(x, 8, 1) * sa + pltpu.roll(x, LANES - 8, 1) * sb


def _rope_bwd(dy, tab):
    c, sa, sb = tab[:, 0:LANES], tab[:, LANES:2 * LANES], tab[:, 2 * LANES:3 * LANES]
    return dy * c + pltpu.roll(dy * sa, LANES - 8, 1) + pltpu.roll(dy * sb, 8, 1)


def _ln_fwd(c, g, b):
    mu = jnp.mean(c, axis=-1, keepdims=True)
    xc = c - mu
    r = lax.rsqrt(jnp.mean(xc * xc, axis=-1, keepdims=True) + LN_EPS)
    nrm = xc * r
    return nrm, r, nrm * g + b


def _phase_fill(buf, ph, rows):
    for k in range(1, 8):
        ph[k - 1, 0:rows - 8, :] = buf[pl.ds(k, rows - 8), :]


def _phase_rows(buf, ph, off, n, cs):
    k = off % 8
    src = buf if k == 0 else ph.at[k - 1]
    return src[pl.ds(off - k, n), cs]


def _ev_mid_fwd(z, tab, conv_w, conv_b, ln_g, ln_b, name):
    s = z.shape[0]
    tr = min(256, s)
    h = HALO32
    cw = LANES

    def body(z_ref, zp_ref, tab_ref, w_ref, b_ref, g_ref, lb_ref, c_ref, a_ref, qkv_ref, gbuf, cbuf, gph):
        first = pl.program_id(0) == 0
        glu_p = zp_ref[:, 0:A_CH].astype(F32) * jax.nn.sigmoid(zp_ref[:, A_CH:2 * A_CH].astype(F32))
        gbuf[0:h, :] = jnp.where(first, 0.0, glu_p)
        gbuf[h:h + tr, :] = z_ref[:, 0:A_CH].astype(F32) * jax.nn.sigmoid(z_ref[:, A_CH:2 * A_CH].astype(F32))
        _phase_fill(gbuf, gph, h + tr)
        for j in range(A_CH // cw):
            cs = slice(j * cw, (j + 1) * cw)
            acc = jnp.broadcast_to(b_ref[:, cs], (tr, cw))
            for t in range(A_CONV):
                acc = acc + w_ref[t:t + 1, cs] * _phase_rows(gbuf, gph, h - (A_CONV - 1) + t, tr, cs)
            cbuf[:, cs] = acc
        c = cbuf[...]
        c_ref[...] = c.astype(BF16)
        _, _, l = _ln_fwd(c, g_ref[...], lb_ref[...])
        a_ref[...] = (l * jax.nn.sigmoid(l)).astype(BF16)
        tab_v = tab_ref[...]
        for p in range(4):
            xq = z_ref[:, Q0 + p * LANES:Q0 + (p + 1) * LANES].astype(F32)
            qkv_ref[:, p * LANES:(p + 1) * LANES] = _rope_fwd(xq, tab_v).astype(BF16)
        lane = lax.broadcasted_iota(jnp.int32, (tr, LANES), 1)
        lo = lane < HEAD_DIM
        kr = _rope_fwd(z_ref[:, K0:K0 + LANES].astype(F32), tab_v)
        vr = z_ref[:, V0:V0 + LANES].astype(F32)
        for base, val in ((4 * LANES, kr), (6 * LANES, vr)):
            sw = pltpu.roll(val, HEAD_DIM, 1)
            qkv_ref[:, base:base + LANES] = jnp.where(lo, val, sw).astype(BF16)
            qkv_ref[:, base + LANES:base + 2 * LANES] = jnp.where(lo, sw, val).astype(BF16)

    return pl.pallas_call(
        body, name=name, grid=(s // tr,),
        in_specs=[_cur(tr, EVEN_IN), _prev(tr, h, 2 * A_CH), _cur(tr, 3 * LANES), _full((A_CONV, A_CH)),
                  _full((1, A_CH)), _full((1, A_CH)), _full((1, A_CH))],
        out_specs=[_cur(tr, A_CH), _cur(tr, A_CH), _cur(tr, 2 * A_CH)],
        out_shape=[jax.ShapeDtypeStruct((s, A_CH), BF16), jax.ShapeDtypeStruct((s, A_CH), BF16),
                   jax.ShapeDtypeStruct((s, 2 * A_CH), BF16)],
        scratch_shapes=[pltpu.VMEM((h + tr, A_CH), F32), pltpu.VMEM((tr, A_CH), F32),
                        pltpu.VMEM((7, h + tr, A_CH), F32)],
        compiler_params=_cparams(("parallel",)),
    )(z, z, tab, conv_w, conv_b, ln_g, ln_b)


def _ev_mid_bwd(dcat, c, z, dq, dkv, tab, conv_w, ln_g, ln_b, name):
    s = z.shape[0]
    tr = min(256, s)
    h = HALO32
    cw = LANES
    ext = tr + h

    def body(da_ref, dan_ref, c_ref, cn_ref, z_ref, dq_ref, dkv_ref, tab_ref, w_ref, g_ref, lb_ref,
             dz_ref, dw_ref, dvec_ref, dcbuf, dcph):
        i = pl.program_id(0)
        first = i == 0
        last = i == pl.num_programs(0) - 1

        @pl.when(first)
        def _():
            dw_ref[...] = jnp.zeros_like(dw_ref)
            dvec_ref[...] = jnp.zeros_like(dvec_ref)

        gv = g_ref[...]

        def ln_silu_bwd(cv, dav):
            nrm, r, l = _ln_fwd(cv, gv, lb_ref[...])
            sig = jax.nn.sigmoid(l)
            dl = dav * (sig * (1.0 + l * (1.0 - sig)))
            dn = dl * gv
            dc = r * (dn - jnp.mean(dn, axis=-1, keepdims=True) - nrm * jnp.mean(dn * nrm, axis=-1, keepdims=True))
            return dc, dl, nrm

        dc, dl, nrm = ln_silu_bwd(c_ref[...].astype(F32), da_ref[...])
        dcn, _, _ = ln_silu_bwd(cn_ref[...].astype(F32), dan_ref[...])
        dcbuf[0:tr, :] = dc
        dcbuf[tr:ext, :] = jnp.where(last, 0.0, dcn)
        dvec_ref[0:1, :] += jnp.sum(dc, axis=0, keepdims=True)
        dvec_ref[1:2, :] += jnp.sum(dl * nrm, axis=0, keepdims=True)
        dvec_ref[2:3, :] += jnp.sum(dl, axis=0, keepdims=True)

        _phase_fill(dcbuf, dcph, ext)
        a_lin = z_ref[:, 0:A_CH].astype(F32)
        sig_g = jax.nn.sigmoid(z_ref[:, A_CH:2 * A_CH].astype(F32))
        glu = a_lin * sig_g
        for j in range(A_CH // cw):
            cs = slice(j * cw, (j + 1) * cw)
            gluj = glu[:, cs]
            acc = jnp.zeros((tr, cw), F32)
            for t in range(A_CONV):
                dsh = _phase_rows(dcbuf, dcph, A_CONV - 1 - t, tr, cs)
                acc = acc + w_ref[t:t + 1, cs] * dsh
                dw_ref[t:t + 1, cs] += jnp.sum(gluj * dsh, axis=0, keepdims=True)
            dz_ref[:, cs] = (acc * sig_g[:, cs]).astype(BF16)
            dz_ref[:, A_CH + j * cw:A_CH + (j + 1) * cw] = (
                acc * a_lin[:, cs] * sig_g[:, cs] * (1.0 - sig_g[:, cs])).astype(BF16)

        tab_v = tab_ref[...]
        for p in range(4):
            cs = slice(p * LANES, (p + 1) * LANES)
            dz_ref[:, Q0 + p * LANES:Q0 + (p + 1) * LANES] = _rope_bwd(dq_ref[:, cs], tab_v).astype(BF16)
        lane = lax.broadcasted_iota(jnp.int32, (tr, LANES), 1)
        lo = lane < HEAD_DIM

        def fold(base):
            p0 = dkv_ref[:, base:base + LANES]
            p1 = dkv_ref[:, base + LANES:base + 2 * LANES]
            s0 = p0 + pltpu.roll(p0, HEAD_DIM, 1)
            s1 = p1 + pltpu.roll(p1, HEAD_DIM, 1)
            return jnp.where(lo, s0, s1)

        dz_ref[:, K0:K0 + LANES] = _rope_bwd(fold(0), tab_v).astype(BF16)
        dz_ref[:, V0:V0 + LANES] = fold(2 * LANES).astype(BF16)

    return pl.pallas_call(
        body, name=name, grid=(s // tr,),
        in_specs=[_cur(tr, A_CH), _next(tr, h, A_CH, s), _cur(tr, A_CH), _next(tr, h, A_CH, s),
                  _cur(tr, EVEN_IN), _cur(tr, A_CH), _cur(tr, A_CH), _cur(tr, 3 * LANES),
                  _full((A_CONV, A_CH)), _full((1, A_CH)), _full((1, A_CH))],
        out_specs=[_cur(tr, EVEN_IN), _full((A_CONV, A_CH)), _full((8, A_CH))],
        out_shape=[jax.ShapeDtypeStruct((s, EVEN_IN), BF16), jax.ShapeDtypeStruct((A_CONV, A_CH), F32),
                   jax.ShapeDtypeStruct((8, A_CH), F32)],
        scratch_shapes=[pltpu.VMEM((ext, A_CH), F32), pltpu.VMEM((7, ext, A_CH), F32)],
        compiler_params=_cparams(("arbitrary",)),
    )(dcat, dcat, c, c, z, dq, dkv, tab, conv_w, ln_g, ln_b)


NT = (((1,), (1,)), ((), ()))
TN = (((0,), (0,)), ((), ()))
QB = WINDOW
SCALE = HEAD_DIM ** -0.5


def _att_scores(q2m, kwin):
    return lax.dot_general(q2m, kwin, NT, preferred_element_type=F32)


def _att_probs(raw, sink, mask):
    sc = jnp.where(mask, raw * SCALE, -jnp.inf)
    mx = jnp.maximum(jnp.max(sc, axis=-1, keepdims=True), sink)
    p = jnp.exp(sc - mx)
    ps = jnp.exp(sink - mx)
    inv = 1.0 / (jnp.sum(p, axis=-1, keepdims=True) + ps)
    return p * inv, ps * inv


def _att_mask(i):
    r = lax.broadcasted_iota(jnp.int32, (QB, 2 * QB), 0)
    kc = lax.broadcasted_iota(jnp.int32, (QB, 2 * QB), 1)
    diff = r + QB - kc
    return (diff >= 0) & (diff < WINDOW) & ((kc >= QB) | (i > 0))


def _half_masks(dtype):
    lane = lax.broadcasted_iota(jnp.int32, (1, LANES), 1)
    return (lane < HEAD_DIM).astype(dtype), (lane >= HEAD_DIM).astype(dtype)


def _att_fwd(qkv, a, sinks, name):
    s = qkv.shape[0]
    nb = s // QB

    def body(sink_ref, qkv_ref, kvp_ref, a_ref, o_ref):
        i = pl.program_id(0)
        mask = _att_mask(i)
        mlo, mhi = _half_masks(BF16)
        o_ref[:, 0:A_CH] = a_ref[...]

        def window(col):
            return jnp.concatenate([kvp_ref[:, col * LANES:(col + 1) * LANES],
                                    qkv_ref[:, A_CH + col * LANES:A_CH + (col + 1) * LANES]], axis=0)

        def raw_scores(p):
            q2 = qkv_ref[:, p * LANES:(p + 1) * LANES]
            kwin = window(p // 2)
            return _att_scores(q2 * mlo, kwin), _att_scores(q2 * mhi, kwin)

        nxt = raw_scores(0)
        for p in range(4):
            raw_e, raw_o = nxt
            if p + 1 < 4:
                nxt = raw_scores(p + 1)
            vwin = window(2 + p // 2)
            pe, _ = _att_probs(raw_e, sink_ref[2 * p], mask)
            po, _ = _att_probs(raw_o, sink_ref[2 * p + 1], mask)
            o = (jnp.dot(pe.astype(BF16), vwin * mlo, preferred_element_type=F32)
                 + jnp.dot(po.astype(BF16), vwin * mhi, preferred_element_type=F32))
            o_ref[:, A_CH + p * LANES:A_CH + (p + 1) * LANES] = o.astype(BF16)

    grid_spec = pltpu.PrefetchScalarGridSpec(
        num_scalar_prefetch=1, grid=(nb,),
        in_specs=[pl.BlockSpec((QB, 2 * A_CH), lambda i, sk: (i, 0)),
                  pl.BlockSpec((QB, A_CH), lambda i, sk: (jnp.maximum(i - 1, 0), 1)),
                  pl.BlockSpec((QB, A_CH), lambda i, sk: (i, 0))],
        out_specs=pl.BlockSpec((QB, 2 * A_CH), lambda i, sk: (i, 0)),
    )
    return pl.pallas_call(
        body, name=name, grid_spec=grid_spec,
        out_shape=jax.ShapeDtypeStruct((s, 2 * A_CH), BF16),
        compiler_params=_cparams(("parallel",)),
    )(sinks, qkv, qkv, a)


def _att_bwd(qkv, dcat, sinks, name):
    s = qkv.shape[0]
    nb = s // QB

    def body(sink_ref, qkv_ref, kvp_ref, do_ref, dq_ref, dkv_ref, ds_ref, carry):
        i = pl.program_id(0)

        @pl.when(i == 0)
        def _():
            ds_ref[...] = jnp.zeros_like(ds_ref)
            carry[...] = jnp.zeros_like(carry)

        @pl.when(i < nb)
        def _():
            mask = _att_mask(i)
            mlo, mhi = _half_masks(BF16)
            dwin = [jnp.zeros((2 * QB, LANES), F32) for _ in range(4)]

            def window(col):
                return jnp.concatenate([kvp_ref[:, col * LANES:(col + 1) * LANES],
                                        qkv_ref[:, A_CH + col * LANES:A_CH + (col + 1) * LANES]], axis=0)

            def first_products(n):
                p, hm = n // 2, (mlo, mhi)[n % 2]
                qm = qkv_ref[:, p * LANES:(p + 1) * LANES] * hm
                dom = do_ref[:, p * LANES:(p + 1) * LANES].astype(BF16) * hm
                kwin, vwin = window(p // 2), window(2 + p // 2)
                return (qm, dom, kwin * hm, _att_scores(qm, kwin),
                        lax.dot_general(dom, vwin, NT, preferred_element_type=F32))

            nxt = first_products(0)
            dq2 = None
            for n in range(N_Q_HEADS):
                g = n // 4
                qm, dom, kwm, raw, dp = nxt
                if n + 1 < N_Q_HEADS:
                    nxt = first_products(n + 1)
                prob, psink = _att_probs(raw, sink_ref[n], mask)
                delta = jnp.sum(prob * dp, axis=-1, keepdims=True)
                dsc = (prob * (dp - delta) * SCALE).astype(BF16)
                ds_ref[n:n + 1, :] += jnp.broadcast_to(jnp.sum(-psink * delta, axis=0, keepdims=True), (1, LANES))
                part = jnp.dot(dsc, kwm, preferred_element_type=F32)
                dq2 = part if n % 2 == 0 else dq2 + part
                dwin[g] = dwin[g] + lax.dot_general(dsc, qm, TN, preferred_element_type=F32)
                dwin[2 + g] = dwin[2 + g] + lax.dot_general(prob.astype(BF16), dom, TN, preferred_element_type=F32)
                if n % 2 == 1:
                    dq_ref[:, (n // 2) * LANES:(n // 2 + 1) * LANES] = dq2
            for n in range(4):
                cs = slice(n * LANES, (n + 1) * LANES)
                dkv_ref[:, cs] = carry[:, cs] + dwin[n][0:QB, :]
                carry[:, cs] = dwin[n][QB:2 * QB, :]

        @pl.when(i == nb)
        def _():
            dkv_ref[...] = carry[...]

    grid_spec = pltpu.PrefetchScalarGridSpec(
        num_scalar_prefetch=1, grid=(nb + 1,),
        in_specs=[pl.BlockSpec((QB, 2 * A_CH), lambda i, sk: (jnp.minimum(i, nb - 1), 0)),
                  pl.BlockSpec((QB, A_CH), lambda i, sk: (jnp.maximum(jnp.minimum(i, nb - 1) - 1, 0), 1)),
                  pl.BlockSpec((QB, A_CH), lambda i, sk: (jnp.minimum(i, nb - 1), 1))],
        out_specs=[pl.BlockSpec((QB, A_CH), lambda i, sk: (jnp.minimum(i, nb - 1), 0)),
                   pl.BlockSpec((QB, A_CH), lambda i, sk: (jnp.maximum(i - 1, 0), 0)),
                   pl.BlockSpec((8, LANES), lambda i, sk: (0, 0))],
        scratch_shapes=[pltpu.VMEM((QB, A_CH), F32)],
    )
    return pl.pallas_call(
        body, name=name, grid_spec=grid_spec,
        out_shape=[jax.ShapeDtypeStruct((s, A_CH), F32), jax.ShapeDtypeStruct((s, A_CH), F32),
                   jax.ShapeDtypeStruct((8, LANES), F32)],
        compiler_params=_cparams(("arbitrary",)),
    )(sinks, qkv, qkv, dcat)


def _local_step(x, positions, target, w):
    row = lambda a, i: a[i:i + 1]
    tab = _rope_tables(positions)
    g = {}

    def ffn_fwd(xin, i):
        xout, f, h, up, u = _ffn_fwd(xin, row(w["ffn_norm_pre"], i), w["ffn_w_up"][i], w["ffn_conv_w"][i],
                                     w["ffn_w_down"][i], row(w["ffn_norm_post"], i), f"ffn{i}_fwd")
        return xout, (xin, f, h, up, u)

    def ffn_bwd(dxout, saved, i):
        xin, f, h, up, u = saved
        dxin, dup, act, df, d_cw, dg_post, dg_pre = _ffn_bwd(
            dxout, f, xin, up, u, row(w["ffn_norm_pre"], i), row(w["ffn_norm_post"], i), w["ffn_w_down_t"][i],
            w["ffn_w_up_t"][i], w["ffn_conv_w"][i], f"ffn{i}_bwd")
        d_down = _mm_tn(act, df, f"ffn{i}_down_dw")
        d_up = _mm_tn(dup, h, f"ffn{i}_up_dw")
        return dxin, dict(ffn_norm_post=dg_post, ffn_norm_pre=dg_pre, ffn_w_up=d_up, ffn_conv_w=d_cw, ffn_w_down=d_down)

    h0, z0 = _mm_pre(x, row(w["mix_norm_pre"], 0), w["ev_w_in"], "ev_in")
    c0, a0, qkv = _ev_mid_fwd(z0, tab, w["ev_a_conv_w"], w["ev_a_conv_b"], w["ev_a_ln_g"], w["ev_a_ln_b"], "ev_mid")
    cat = _att_fwd(qkv, a0, w["ev_sinks"], "ev_att")
    m0, x1 = _mm_post(cat, w["ev_w_out"], row(w["mix_norm_post"], 0), x, "ev_out")
    x2, ffn0 = ffn_fwd(x1, 0)
    h2, z1 = _mm_pre(x2, row(w["mix_norm_pre"], 1), w["od_w_in"], "od_in")
    y1 = _od_gate_fwd(z1, w["od_conv_w"], "od_mid")
    m1, x3 = _mm_post(y1, w["od_w_out"], row(w["mix_norm_post"], 1), x2, "od_out")
    x4, ffn1 = ffn_fwd(x3, 1)

    sq, dx4 = _loss_bwd(x4, target, "loss")

    dx3, gf1 = ffn_bwd(dx4, ffn1, 1)
    dm1, dy1, dg_mo1 = _mm_post_bwd(dx3, m1, row(w["mix_norm_post"], 1), w["od_w_out_t"], "od_out_bwd")
    g["od_w_out"] = _mm_tn(y1, dm1, "od_out_dw")
    dz1, g["od_conv_w"] = _od_gate_bwd(dy1, z1, w["od_conv_w"], "od_mid_bwd")
    g["od_w_in"] = _mm_tn(dz1, h2, "od_in_dw")
    dx2, dg_mp1 = _mm_pre_bwd(dz1, w["od_w_in_t"], x2, row(w["mix_norm_pre"], 1), dx3, "od_in_bwd")

    dx1, gf0 = ffn_bwd(dx2, ffn0, 0)
    dm0, dcat, dg_mo0 = _mm_post_bwd(dx1, m0, row(w["mix_norm_post"], 0), w["ev_w_out_t"], "ev_out_bwd")
    g["ev_w_out"] = _mm_tn(cat, dm0, "ev_out_dw")
    dq, dkv, dsk = _att_bwd(qkv, dcat, w["ev_sinks"], "ev_att_bwd")
    dz0, g["ev_a_conv_w"], dvec = _ev_mid_bwd(dcat, c0, z0, dq, dkv, tab, w["ev_a_conv_w"], w["ev_a_ln_g"],
                                              w["ev_a_ln_b"], "ev_mid_bwd")
    g["ev_w_in"] = _mm_tn(dz0, h0, "ev_in_dw")
    dx0, dg_mp0 = _mm_pre_bwd(dz0, w["ev_w_in_t"], x, row(w["mix_norm_pre"], 0), dx1, "ev_in_bwd")

    g["ev_a_conv_b"] = dvec[0:1]
    g["ev_a_ln_g"] = dvec[1:2]
    g["ev_a_ln_b"] = dvec[2:3]
    g["ev_sinks"] = dsk[:, 0]
    g["mix_norm_pre"] = jnp.concatenate([dg_mp0, dg_mp1], axis=0)
    g["mix_norm_post"] = jnp.concatenate([dg_mo0, dg_mo1], axis=0)
    for k in ("ffn_norm_pre", "ffn_norm_post", "ffn_w_up", "ffn_conv_w", "ffn_w_down"):
        g[k] = jnp.stack([gf0[k], gf1[k]], axis=0) if gf0[k].shape[0] != 1 else jnp.concatenate([gf0[k], gf1[k]], axis=0)
    return sq, dx0, g


ANY = pl.BlockSpec(memory_space=pl.ANY)
PACK_COLS = 1024


def _me():
    return lax.axis_index("x"), lax.axis_index("y"), lax.axis_index("c")


def _other_chips(x, y):
    return [(1 - x, y), (x, 1 - y), (1 - x, 1 - y)]


def _remote(src, dst, send, recv, dev):
    return pltpu.make_async_remote_copy(src_ref=src, dst_ref=dst, send_sem=send, recv_sem=recv,
                                        device_id=dev, device_id_type=MESH)


def _gather_chips(wp, name):
    r, cols = wp.shape
    rh = r // 2

    def body(w_ref, o_ref, send, recv):
        x, y, c = _me()
        p = 2 * x + y
        sib = (x, y, 1 - c)
        chips = _other_chips(x, y)
        half = pl.ds(c * rh, rh)
        other = pl.ds((1 - c) * rh, rh)
        sent = [_remote(w_ref.at[half], o_ref.at[p, half], send.at[k], recv.at[k], (cx, cy, c))
                for k, (cx, cy) in enumerate(chips)]
        for cp in sent:
            cp.start()
        for k, (cx, cy) in enumerate(chips):
            q = 2 * cx + cy
            _remote(w_ref.at[half], o_ref.at[q, half], send.at[k], recv.at[k], (cx, cy, c)).wait_recv()
            fwd = _remote(o_ref.at[q, half], o_ref.at[q, half], send.at[3 + k], recv.at[3 + k], sib)
            fwd.start()
            sent.append(fwd)
        for k, (cx, cy) in enumerate(chips):
            q = 2 * cx + cy
            _remote(o_ref.at[q, other], o_ref.at[q, other], send.at[3 + k], recv.at[3 + k], sib).wait_recv()
        for cp in sent:
            cp.wait_send()

    return pl.pallas_call(
        body, name=name, in_specs=[ANY], out_specs=ANY,
        out_shape=jax.ShapeDtypeStruct((N_CHIPS, r, cols), wp.dtype),
        scratch_shapes=[pltpu.SemaphoreType.DMA((6,)), pltpu.SemaphoreType.DMA((6,))],
    )(wp)


def _exchange8(v, reduce, name):
    r, cols = v.shape
    rel = [(a, b, d) for a in (0, 1) for b in (0, 1) for d in (0, 1) if (a, b, d) != (0, 0, 0)]

    def body(v_ref, o_ref, *rest):
        if reduce:
            gbuf, send, recv = rest
        else:
            gbuf = o_ref
            send, recv = rest
        x, y, c = _me()
        me = 4 * x + 2 * y + c
        gbuf[me] = v_ref[...]
        sent = []
        for k, (a, b, d) in enumerate(rel):
            cp = _remote(v_ref, gbuf.at[me], send.at[k], recv.at[k], ((x + a) % 2, (y + b) % 2, (c + d) % 2))
            cp.start()
            sent.append(cp)
        for k, (a, b, d) in enumerate(rel):
            src = 4 * ((x + a) % 2) + 2 * ((y + b) % 2) + (c + d) % 2
            _remote(v_ref, gbuf.at[src], send.at[k], recv.at[k], (x, y, c)).wait_recv()
        for cp in sent:
            cp.wait_send()
        if reduce:
            acc = gbuf[0]
            for n in range(1, 8):
                acc = acc + gbuf[n]
            o_ref[...] = acc

    vmem = pl.BlockSpec(memory_space=pltpu.VMEM)
    sems = [pltpu.SemaphoreType.DMA((7,)), pltpu.SemaphoreType.DMA((7,))]
    if reduce:
        out_shape = jax.ShapeDtypeStruct((r, cols), F32)
        scratch = [pltpu.VMEM((8, r, cols), F32)] + sems
    else:
        out_shape = jax.ShapeDtypeStruct((8, r, cols), F32)
        scratch = sems
    return pl.pallas_call(body, name=name, in_specs=[vmem], out_specs=vmem, out_shape=out_shape,
                          scratch_shapes=scratch)(v)


def _rs_swap(g, name):
    _, _, rh, cols = g.shape

    def body(g_ref, o_ref, send, recv):
        x, y, c = _me()
        cps = [_remote(g_ref.at[q, 1 - c], o_ref.at[q], send.at[q], recv.at[q], (x, y, 1 - c)) for q in range(N_CHIPS)]
        for cp in cps:
            cp.start()
        for cp in cps:
            cp.wait()

    return pl.pallas_call(
        body, name=name, in_specs=[ANY], out_specs=ANY,
        out_shape=jax.ShapeDtypeStruct((N_CHIPS, rh, cols), F32),
        scratch_shapes=[pltpu.SemaphoreType.DMA((N_CHIPS,)), pltpu.SemaphoreType.DMA((N_CHIPS,))],
    )(g)


def _row_tile(rows, pref):
    if rows <= pref:
        return rows
    t = (pref // 8) * 8
    while t >= 8:
        if rows % t == 0:
            return t
        t -= 8
    return rows


def _rs_add(g, sib, c, name):
    _, _, rh, cols = g.shape
    tr = _row_tile(rh, 512)

    def body(c_ref, g_ref, s_ref, o_ref):
        o_ref[...] = (g_ref[...] + s_ref[...]).astype(BF16)

    grid_spec = pltpu.PrefetchScalarGridSpec(
        num_scalar_prefetch=1, grid=(N_CHIPS, rh // tr),
        in_specs=[pl.BlockSpec((None, None, tr, cols), lambda q, i, cr: (q, cr[0], i, 0)),
                  pl.BlockSpec((None, tr, cols), lambda q, i, cr: (q, i, 0))],
        out_specs=pl.BlockSpec((None, tr, cols), lambda q, i, cr: (q, i, 0)),
    )
    return pl.pallas_call(
        body, name=name, grid_spec=grid_spec,
        out_shape=jax.ShapeDtypeStruct((N_CHIPS, rh, cols), BF16),
        compiler_params=_cparams(("parallel", "parallel")),
    )(c, g, sib)


def _rs_ici(a, name):
    _, rh, cols = a.shape

    def body(a_ref, o_ref, send, recv):
        x, y, c = _me()
        p = 2 * x + y
        cps = []
        for k, (cx, cy) in enumerate(_other_chips(x, y)):
            cp = _remote(a_ref.at[2 * cx + cy], o_ref.at[p], send.at[k], recv.at[k], (cx, cy, c))
            cp.start()
            cps.append(cp)
        for k, (cx, cy) in enumerate(_other_chips(x, y)):
            q = 2 * cx + cy
            _remote(a_ref.at[q], o_ref.at[q], send.at[k], recv.at[k], (cx, cy, c)).wait_recv()
        for cp in cps:
            cp.wait_send()

    return pl.pallas_call(
        body, name=name, in_specs=[ANY], out_specs=ANY,
        out_shape=jax.ShapeDtypeStruct((N_CHIPS, rh, cols), a.dtype),
        scratch_shapes=[pltpu.SemaphoreType.DMA((3,)), pltpu.SemaphoreType.DMA((3,))],
    )(a)


def _rs_sum(rb, a, chip, name):
    _, rh, cols = rb.shape
    tr = _row_tile(rh, 512)

    def body(p_ref, r0, r1, r2, r3, own, o_ref):
        p = p_ref[0]
        ownv = own[...].astype(F32)
        acc = None
        for q, r in enumerate((r0, r1, r2, r3)):
            v = jnp.where(p == q, ownv, r[...].astype(F32))
            acc = v if acc is None else acc + v
        o_ref[...] = acc

    def spec(q):
        return pl.BlockSpec((None, tr, cols), lambda i, pr: (jnp.where(pr[0] == q, (q + 1) % N_CHIPS, q), i, 0))

    grid_spec = pltpu.PrefetchScalarGridSpec(
        num_scalar_prefetch=1, grid=(rh // tr,),
        in_specs=[spec(0), spec(1), spec(2), spec(3), pl.BlockSpec((None, tr, cols), lambda i, pr: (pr[0], i, 0))],
        out_specs=pl.BlockSpec((tr, cols), lambda i, pr: (i, 0)),
    )
    return pl.pallas_call(
        body, name=name, grid_spec=grid_spec,
        out_shape=jax.ShapeDtypeStruct((rh, cols), F32),
        compiler_params=_cparams(("parallel",)),
    )(chip, rb, rb, rb, rb, a)


def _rs_share(hsum, name):
    rh, cols = hsum.shape

    def body(h_ref, o_ref, send, recv):
        x, y, c = _me()
        cp = _remote(h_ref, o_ref, send, recv, (x, y, 1 - c))
        cp.start()
        cp.wait()

    return pl.pallas_call(
        body, name=name, in_specs=[ANY], out_specs=ANY,
        out_shape=jax.ShapeDtypeStruct((rh, cols), F32),
        scratch_shapes=[pltpu.SemaphoreType.DMA, pltpu.SemaphoreType.DMA],
    )(hsum)


def _adamw(w, g, m, v, name):
    rows, cols = w.shape
    tr = _row_tile(rows, 512)

    def body(w_ref, g_ref, m_ref, v_ref, d_ref, nm_ref, nv_ref):
        gv = g_ref[...]
        nm = ADAM_B1 * m_ref[...] + (1.0 - ADAM_B1) * gv
        nv = ADAM_B2 * v_ref[...] + (1.0 - ADAM_B2) * (gv * gv)
        m_hat = nm / (1.0 - ADAM_B1 ** ADAM_STEP)
        v_hat = nv / (1.0 - ADAM_B2 ** ADAM_STEP)
        d_ref[...] = -ADAM_LR * (m_hat / (jnp.sqrt(v_hat) + ADAM_EPS) + ADAM_WD * w_ref[...])
        nm_ref[...] = nm
        nv_ref[...] = nv

    spec = pl.BlockSpec((tr, cols), lambda i: (i, 0))
    shp = jax.ShapeDtypeStruct((rows, cols), F32)
    return pl.pallas_call(
        body, name=name, grid=(rows // tr,), in_specs=[spec] * 4, out_specs=[spec] * 3, out_shape=[shp] * 3,
        compiler_params=_cparams(("parallel",)),
    )(w, g, m, v)


WEIGHTS = ("mix_norm_pre", "mix_norm_post", "ffn_norm_pre", "ffn_norm_post", "ev_w_in", "ev_a_conv_w", "ev_a_conv_b",
           "ev_a_ln_g", "ev_a_ln_b", "ev_sinks", "ev_w_out", "od_w_in", "od_conv_w", "od_w_out", "ffn_w_up",
           "ffn_conv_w", "ffn_w_down")
MATS = (("ev_w_in", 2), ("ev_w_out", 1), ("od_w_in", 2), ("od_w_out", 1), ("ffn_w_up", 2), ("ffn_w_down", 1))
SMALL_SHARDED = ("ev_a_conv_w", "od_conv_w", "ffn_conv_w")
REPLICATED = ("mix_norm_pre", "mix_norm_post", "ffn_norm_pre", "ffn_norm_post", "ev_a_conv_b", "ev_a_ln_g",
              "ev_a_ln_b", "ev_sinks")


def _pack(parts, rows_multiple):
    flat = jnp.concatenate([p.reshape(-1) for p in parts])
    unit = rows_multiple * PACK_COLS
    pad = (-flat.shape[0]) % unit
    if pad:
        flat = jnp.concatenate([flat, jnp.zeros((pad,), flat.dtype)])
    return flat.reshape(-1, PACK_COLS)


def _unpack(buf, shapes):
    flat = buf.reshape(-1)
    out, off = [], 0
    for shp in shapes:
        n = 1
        for d in shp:
            n *= d
        out.append(flat[off:off + n].reshape(shp))
        off += n
    return out


def _shard_rows(shard, axis):
    if axis == 2:
        shard = jnp.swapaxes(shard, 1, 2)
    return shard.reshape(-1, PACK_COLS)


def kernel(x, positions, mix_norm_pre, mix_norm_post, ffn_norm_pre, ffn_norm_post, ev_w_in, ev_a_conv_w, ev_a_conv_b, ev_a_ln_g, ev_a_ln_b, ev_sinks, ev_w_out, od_w_in, od_conv_w, od_w_out, ffn_w_up, ffn_conv_w, ffn_w_down, loss_target, m_mix_norm_pre, m_mix_norm_post, m_ffn_norm_pre, m_ffn_norm_post, m_ev_w_in, m_ev_a_conv_w, m_ev_a_conv_b, m_ev_a_ln_g, m_ev_a_ln_b, m_ev_sinks, m_ev_w_out, m_od_w_in, m_od_conv_w, m_od_w_out, m_ffn_w_up, m_ffn_conv_w, m_ffn_w_down, v_mix_norm_pre, v_mix_norm_post, v_ffn_norm_pre, v_ffn_norm_post, v_ev_w_in, v_ev_a_conv_w, v_ev_a_conv_b, v_ev_a_ln_g, v_ev_a_ln_b, v_ev_sinks, v_ev_w_out, v_od_w_in, v_od_conv_w, v_od_w_out, v_ffn_w_up, v_ffn_conv_w, v_ffn_w_down):
    wts = dict(zip(WEIGHTS, (mix_norm_pre, mix_norm_post, ffn_norm_pre, ffn_norm_post, ev_w_in, ev_a_conv_w, ev_a_conv_b,
                             ev_a_ln_g, ev_a_ln_b, ev_sinks, ev_w_out, od_w_in, od_conv_w, od_w_out, ffn_w_up, ffn_conv_w,
                             ffn_w_down)))
    mom = dict(zip(WEIGHTS, (m_mix_norm_pre, m_mix_norm_post, m_ffn_norm_pre, m_ffn_norm_post, m_ev_w_in, m_ev_a_conv_w,
                             m_ev_a_conv_b, m_ev_a_ln_g, m_ev_a_ln_b, m_ev_sinks, m_ev_w_out, m_od_w_in, m_od_conv_w,
                             m_od_w_out, m_ffn_w_up, m_ffn_conv_w, m_ffn_w_down)))
    var = dict(zip(WEIGHTS, (v_mix_norm_pre, v_mix_norm_post, v_ffn_norm_pre, v_ffn_norm_post, v_ev_w_in, v_ev_a_conv_w,
                             v_ev_a_conv_b, v_ev_a_ln_g, v_ev_a_ln_b, v_ev_sinks, v_ev_w_out, v_od_w_in, v_od_conv_w,
                             v_od_w_out, v_ffn_w_up, v_ffn_conv_w, v_ffn_w_down)))
    xi, yi, ci = _me()
    chip = 2 * xi + yi

    mat_rows = [_shard_rows(wts[k].astype(BF16), axis) for k, axis in MATS]
    wp = jnp.concatenate(mat_rows, axis=0)
    gathered = lax.dynamic_update_slice(_gather_chips(wp, "gather_mats"), wp[None], (chip, 0, 0))
    small_shapes = [wts[k].shape for k in SMALL_SHARDED]
    small_all = _exchange8(_pack([wts[k] for k in SMALL_SHARDED], 8), False, "gather_small")
    w = {k: wts[k] for k in REPLICATED}
    off = 0
    for (k, axis), blk in zip(MATS, mat_rows):
        n, layers = blk.shape[0], wts[k].shape[0]
        full = gathered[:, off:off + n].reshape(N_CHIPS, layers, n // layers, PACK_COLS)
        full = jnp.swapaxes(full, 0, 1).reshape(layers, N_CHIPS * (n // layers), PACK_COLS)
        off += n
        w[k + "_t" if axis == 2 else k] = full
        w[k if axis == 2 else k + "_t"] = jnp.swapaxes(full, 1, 2)
    per_chip = [_unpack(small_all[2 * q], small_shapes) for q in range(N_CHIPS)]
    for n, k in enumerate(SMALL_SHARDED):
        w[k] = jnp.concatenate([per_chip[q][n] for q in range(N_CHIPS)], axis=-1)
    for k in ("ev_w_in", "ev_w_out", "od_w_in", "od_w_out"):
        w[k] = w[k][0]
        w[k + "_t"] = w[k + "_t"][0]
    for k in ("ev_a_conv_w", "od_conv_w"):
        w[k] = w[k][0]
    w["ev_sinks"] = w["ev_sinks"][0]

    sq, dx, g = _local_step(x[0], positions[0], loss_target[0], w)
    loss = lax.psum(0.5 * jnp.sum(sq) / D_MODEL, ("x", "y", "c"))

    parts = []
    for k, axis in MATS:
        layers = wts[k].shape[0]
        gk = g[k].reshape(layers, N_CHIPS, -1, PACK_COLS)
        parts.append(jnp.swapaxes(gk, 0, 1).reshape(N_CHIPS, -1, PACK_COLS))
    gp = jnp.concatenate(parts, axis=1)
    rows = gp.shape[1]
    gp = gp.reshape(N_CHIPS, 2, rows // 2, PACK_COLS)
    core = jnp.reshape(ci, (1,)).astype(jnp.int32)
    sib = _rs_swap(gp, "rs_swap")
    pair = _rs_add(gp, sib, core, "rs_add")
    landed = _rs_ici(pair, "rs_ici")
    half = _rs_sum(landed, pair, jnp.reshape(chip, (1,)).astype(jnp.int32), "rs_sum")
    other = _rs_share(half, "rs_share")
    red = jnp.concatenate([jnp.where(ci == 0, half, other), jnp.where(ci == 0, other, half)], axis=0)
    grads = {}
    off = 0
    for (k, axis), blk in zip(MATS, mat_rows):
        n, shp = blk.shape[0], wts[k].shape
        part = red[off:off + n]
        off += n
        grads[k] = jnp.swapaxes(part.reshape(shp[0], shp[2], shp[1]), 1, 2) if axis == 2 else part.reshape(shp)

    small_keys = REPLICATED + SMALL_SHARDED
    full_shapes = [wts[k].shape for k in REPLICATED] + [wts[k].shape[:-1] + (wts[k].shape[-1] * N_CHIPS,) for k in SMALL_SHARDED]
    sm = _exchange8(_pack([g[k] for k in small_keys], 8), True, "reduce_small")
    for k, full in zip(small_keys, _unpack(sm, full_shapes)):
        if k in SMALL_SHARDED:
            n = wts[k].shape[-1]
            full = lax.dynamic_slice_in_dim(full, chip * n, n, axis=full.ndim - 1)
        grads[k] = full

    deltas, new_m, new_v = {}, {}, {}
    for k in WEIGHTS:
        shp = wts[k].shape
        two_d = (-1, shp[-1])
        d, nm, nv = _adamw(wts[k].reshape(two_d), grads[k].reshape(two_d), mom[k].reshape(two_d), var[k].reshape(two_d),
                           "adamw_" + k)
        deltas[k], new_m[k], new_v[k] = d.reshape(shp), nm.reshape(shp), nv.reshape(shp)

    return (loss, dx[None], *[grads[k] for k in WEIGHTS], *[deltas[k] for k in WEIGHTS],
            *[new_m[k] for k in WEIGHTS], *[new_v[k] for k in WEIGHTS])
```

```python
import functools

import jax
import jax.numpy as jnp
import numpy as np
from jax import lax
from jax.experimental import pallas as pl
from jax.experimental.pallas import tpu as pltpu

F32 = jnp.float32
BF16 = jnp.bfloat16
MESH = pl.DeviceIdType.MESH

D_MODEL = 1024
HEAD_DIM = 64
A_CH = 512
A_CONV = 31
N_Q_HEADS = 8
WINDOW = 128
ROPE_THETA = 500000.0
ROPE_DIM = 16
D_FF = 2816
RMS_EPS = 1e-6
LN_EPS = 1e-5
ADAM_LR = 0.001
ADAM_B1 = 0.9
ADAM_B2 = 0.999
ADAM_EPS = 1e-08
ADAM_WD = 0.01
ADAM_STEP = 10

LANES = 128
HALO16 = 16
HALO32 = 32
VMEM_LIMIT = 56 * 1024 * 1024
FFN_BWD_VMEM = 60 * 1024 * 1024
N_CHIPS = 4


def _cparams(sem):
    return pltpu.CompilerParams(dimension_semantics=sem, vmem_limit_bytes=VMEM_LIMIT)


def _tile(n, pref):
    if n <= pref:
        return n
    t = (pref // LANES) * LANES
    while t >= LANES:
        if n % t == 0:
            return t
        t -= LANES
    return n


MM_ROWS = 512


def _rms_scale(v):
    return lax.rsqrt(jnp.mean(v * v, axis=-1, keepdims=True) + RMS_EPS)


def _rms_bwd(dy, v, g):
    r = _rms_scale(v)
    nrm = v * r
    dn = dy * g
    return r * (dn - nrm * jnp.mean(dn * nrm, axis=-1, keepdims=True)), jnp.sum(dy * nrm, axis=0, keepdims=True)


def _mm_pre(x, g, w, name):
    s, d = x.shape
    n = w.shape[1]
    tm = min(MM_ROWS, s)
    tn = _tile(n, 1408)

    def body(x_ref, g_ref, w_ref, h_ref, z_ref):
        xv = x_ref[...]
        hv = (xv * _rms_scale(xv) * g_ref[...]).astype(BF16)
        h_ref[...] = hv
        z_ref[...] = jnp.dot(hv, w_ref[...], preferred_element_type=F32).astype(BF16)

    return pl.pallas_call(
        body, name=name, grid=(s // tm, n // tn),
        in_specs=[pl.BlockSpec((tm, d), lambda i, j: (i, 0)), pl.BlockSpec((1, d), lambda i, j: (0, 0)),
                  pl.BlockSpec((d, tn), lambda i, j: (0, j))],
        out_specs=[pl.BlockSpec((tm, d), lambda i, j: (i, 0)), pl.BlockSpec((tm, tn), lambda i, j: (i, j))],
        out_shape=[jax.ShapeDtypeStruct((s, d), BF16), jax.ShapeDtypeStruct((s, n), BF16)],
        compiler_params=_cparams(("parallel", "arbitrary")),
    )(x, g, w)


def _mm_post(a, w, g, xres, name):
    s, k = a.shape
    d = w.shape[1]
    tm = min(MM_ROWS, s)

    def body(a_ref, w_ref, g_ref, x_ref, m_ref, o_ref):
        mv = jnp.dot(a_ref[...], w_ref[...], preferred_element_type=F32)
        m_ref[...] = mv
        o_ref[...] = x_ref[...] + mv * _rms_scale(mv) * g_ref[...]

    row = pl.BlockSpec((tm, d), lambda i: (i, 0))
    return pl.pallas_call(
        body, name=name, grid=(s // tm,),
        in_specs=[pl.BlockSpec((tm, k), lambda i: (i, 0)), _full((k, d)), _full((1, d)), row],
        out_specs=[row, row],
        out_shape=[jax.ShapeDtypeStruct((s, d), F32), jax.ShapeDtypeStruct((s, d), F32)],
        compiler_params=_cparams(("parallel",)),
    )(a, w, g, xres)


def _mm_post_bwd(dy, m, g, w_t, name):
    s, d = m.shape
    k = w_t.shape[1]
    tm = min(MM_ROWS, s)

    def body(dy_ref, m_ref, g_ref, wt_ref, dm_ref, da_ref, dg_ref):
        @pl.when(pl.program_id(0) == 0)
        def _():
            dg_ref[...] = jnp.zeros_like(dg_ref)

        dm, dg = _rms_bwd(dy_ref[...], m_ref[...], g_ref[...])
        dg_ref[...] += dg
        dmb = dm.astype(BF16)
        dm_ref[...] = dmb
        da_ref[...] = jnp.dot(dmb, wt_ref[...], preferred_element_type=F32)

    row = pl.BlockSpec((tm, d), lambda i: (i, 0))
    return pl.pallas_call(
        body, name=name, grid=(s // tm,),
        in_specs=[row, row, _full((1, d)), _full((d, k))],
        out_specs=[row, pl.BlockSpec((tm, k), lambda i: (i, 0)), _full((1, d))],
        out_shape=[jax.ShapeDtypeStruct((s, d), BF16), jax.ShapeDtypeStruct((s, k), F32),
                   jax.ShapeDtypeStruct((1, d), F32)],
        compiler_params=_cparams(("arbitrary",)),
    )(dy, m, g, w_t)


def _mm_pre_bwd(dz, w_t, x, g, res, name):
    s, k = dz.shape
    d = w_t.shape[1]
    tm = min(MM_ROWS, s)

    def body(dz_ref, wt_ref, x_ref, g_ref, res_ref, dx_ref, dg_ref):
        @pl.when(pl.program_id(0) == 0)
        def _():
            dg_ref[...] = jnp.zeros_like(dg_ref)

        dh = jnp.dot(dz_ref[...], wt_ref[...], preferred_element_type=F32)
        dx, dg = _rms_bwd(dh, x_ref[...], g_ref[...])
        dg_ref[...] += dg
        dx_ref[...] = res_ref[...] + dx

    row = pl.BlockSpec((tm, d), lambda i: (i, 0))
    return pl.pallas_call(
        body, name=name, grid=(s // tm,),
        in_specs=[pl.BlockSpec((tm, k), lambda i: (i, 0)), _full((k, d)), row, _full((1, d)), row],
        out_specs=[row, _full((1, d))],
        out_shape=[jax.ShapeDtypeStruct((s, d), F32), jax.ShapeDtypeStruct((1, d), F32)],
        compiler_params=_cparams(("arbitrary",)),
    )(dz, w_t, x, g, res)


def _mm_tn(a, b, name):
    s, k = a.shape
    _, n = b.shape
    tk = _tile(k, 1408)
    tn = _tile(n, 1408)
    ts = min(2048, s)

    def body(a_ref, b_ref, o_ref):
        @pl.when(pl.program_id(2) == 0)
        def _():
            o_ref[...] = jnp.zeros_like(o_ref)

        o_ref[...] += lax.dot_general(a_ref[...], b_ref[...], (((0,), (0,)), ((), ())),
                                      preferred_element_type=F32)

    return pl.pallas_call(
        body, name=name, grid=(k // tk, n // tn, s // ts),
        in_specs=[pl.BlockSpec((ts, tk), lambda i, j, l: (l, i)), pl.BlockSpec((ts, tn), lambda i, j, l: (l, j))],
        out_specs=pl.BlockSpec((tk, tn), lambda i, j, l: (i, j)),
        out_shape=jax.ShapeDtypeStruct((k, n), F32),
        compiler_params=_cparams(("parallel", "parallel", "arbitrary")),
    )(a, b)


def _loss_bwd(y, target, name):
    s, d = y.shape
    tr = min(512, s)

    def body(y_ref, t_ref, acc_ref, dy_ref):
        @pl.when(pl.program_id(0) == 0)
        def _():
            acc_ref[...] = jnp.zeros_like(acc_ref)

        e = y_ref[...] - t_ref[...]
        dy_ref[...] = e * (1.0 / d)
        acc_ref[...] += jnp.sum(e * e, axis=0, keepdims=True)

    row = pl.BlockSpec((tr, d), lambda i: (i, 0))
    vec = pl.BlockSpec((1, d), lambda i: (0, 0))
    return pl.pallas_call(
        body, name=name, grid=(s // tr,),
        in_specs=[row, row], out_specs=[vec, row],
        out_shape=[jax.ShapeDtypeStruct((1, d), F32), jax.ShapeDtypeStruct((s, d), F32)],
        compiler_params=_cparams(("arbitrary",)),
    )(y, target)


def _cur(tr, w, col=0):
    return pl.BlockSpec((tr, w), lambda i: (i, col))


def _prev(tr, h, w, col=0):
    return pl.BlockSpec((h, w), lambda i: (jnp.maximum(i * (tr // h) - 1, 0), col))


def _next(tr, h, w, nrows, col=0):
    last = nrows // h - 1
    return pl.BlockSpec((h, w), lambda i: (jnp.minimum((i + 1) * (tr // h), last), col))


def _full(shape):
    return pl.BlockSpec(shape, lambda i: tuple(0 for _ in shape))


def _silu_parts(g):
    sig = jax.nn.sigmoid(g)
    return sig, g * sig


FFN_CW = 256
FFN_NBUF = 3


def _conv3_taps(buf, w, off, rows):
    return (w[0:1] * buf[pl.ds(off, rows), :] + w[1:2] * buf[pl.ds(off + 1, rows), :]
            + w[2:3] * buf[pl.ds(off + 2, rows), :])


WHOLE_VMEM = pl.BlockSpec(memory_space=pltpu.VMEM)


def _ffn_fwd(x, g_pre, wu, conv_w, wd, g_post, name):
    s, d = x.shape
    f2 = wu.shape[1]
    f = f2 // 2
    tr = min(256, s)
    h = HALO16
    cw = FFN_CW

    def body(x_ref, gpre_ref, wu_ref, cw_ref, wd_ref, gpost_ref, xo_ref, f_ref, h_ref, up_ref, u_ref,
             carry, gbuf, vbuf, facc):
        @pl.when(pl.program_id(0) == 0)
        def _():
            carry[...] = jnp.zeros_like(carry)

        xv = x_ref[...]
        r = lax.rsqrt(jnp.mean(xv * xv, axis=-1, keepdims=True) + RMS_EPS)
        hv = (xv * r * gpre_ref[...]).astype(BF16)
        h_ref[...] = hv
        nchunk = f // cw

        def up_proj(j):
            for buf, base in ((gbuf, 0), (vbuf, f)):
                cs = slice(base + j * cw, base + (j + 1) * cw)
                dst = buf.at[j % FFN_NBUF]
                upc = jnp.dot(hv, wu_ref[:, cs], preferred_element_type=F32)
                up_ref[:, cs] = upc.astype(BF16)
                dst[0:h, :] = carry[:, cs]
                dst[h:h + tr, :] = upc
                carry[:, cs] = upc[tr - h:tr, :]

        def down_proj(j, act):
            part = jnp.dot(act, wd_ref[j * cw:(j + 1) * cw, :], preferred_element_type=F32)
            if j == 0:
                facc[...] = part
            else:
                facc[...] += part

        up_proj(0)
        pending = None
        for j in range(nchunk):
            cg = slice(j * cw, (j + 1) * cw)
            cv = slice(f + j * cw, f + (j + 1) * cw)
            if j + 1 < nchunk:
                up_proj(j + 1)
            if pending is not None:
                down_proj(*pending)
            g = _conv3_taps(gbuf.at[j % FFN_NBUF], cw_ref[:, cg], h - 2, tr)
            v = _conv3_taps(vbuf.at[j % FFN_NBUF], cw_ref[:, cv], h - 2, tr)
            u_ref[:, cg] = g.astype(BF16)
            u_ref[:, cv] = v.astype(BF16)
            act = (g * jax.nn.sigmoid(g) * v).astype(BF16)
            pending = (j, act)
        down_proj(*pending)
        fv = facc[...]
        f_ref[...] = fv
        r2 = lax.rsqrt(jnp.mean(fv * fv, axis=-1, keepdims=True) + RMS_EPS)
        xo_ref[...] = xv + fv * r2 * gpost_ref[...]

    row = _cur(tr, d)
    wide = _cur(tr, f2)
    return pl.pallas_call(
        body, name=name, grid=(s // tr,),
        in_specs=[row, _full((1, d)), WHOLE_VMEM, _full((3, f2)), WHOLE_VMEM, _full((1, d))],
        out_specs=[row, row, row, wide, wide],
        out_shape=[jax.ShapeDtypeStruct((s, d), F32), jax.ShapeDtypeStruct((s, d), F32),
                   jax.ShapeDtypeStruct((s, d), BF16), jax.ShapeDtypeStruct((s, f2), BF16),
                   jax.ShapeDtypeStruct((s, f2), BF16)],
        scratch_shapes=[pltpu.VMEM((h, f2), F32), pltpu.VMEM((FFN_NBUF, h + tr, cw), F32),
                        pltpu.VMEM((FFN_NBUF, h + tr, cw), F32), pltpu.VMEM((tr, d), F32)],
        compiler_params=_cparams(("arbitrary",)),
    )(x, g_pre, wu, conv_w, wd, g_post)


def _ffn_bwd(dxo, fout, x, up, u, g_pre, g_post, wd_t, wu_t, conv_w, name):
    s, d = x.shape
    f2 = up.shape[1]
    f = f2 // 2
    tr = min(256, s)
    nt = s // tr
    h = HALO16
    cw = FFN_CW

    def body(dy_ref, f_ref, x_ref, up_ref, u_ref, gpre_ref, gpost_ref, wdt_ref, wut_ref, cw_ref,
             dx_ref, dup_ref, act_ref, df_ref, dcw_ref, dgpost_ref, dgpre_ref, carry, dgbuf, dvbuf, dhacc):
        @pl.when(pl.program_id(0) == 0)
        def _():
            carry[...] = jnp.zeros_like(carry)
            dcw_ref[...] = jnp.zeros_like(dcw_ref)
            dgpost_ref[...] = jnp.zeros_like(dgpost_ref)
            dgpre_ref[...] = jnp.zeros_like(dgpre_ref)

        dy = dy_ref[...]
        fv = f_ref[...]
        r = lax.rsqrt(jnp.mean(fv * fv, axis=-1, keepdims=True) + RMS_EPS)
        nrm = fv * r
        dn = dy * gpost_ref[...]
        dfv = (r * (dn - nrm * jnp.mean(dn * nrm, axis=-1, keepdims=True))).astype(BF16)
        dgpost_ref[...] += jnp.sum(dy * nrm, axis=0, keepdims=True)
        df_ref[...] = dfv
        nchunk = f // cw

        def dh_part(dupb, cs, first):
            part = jnp.dot(dupb, wut_ref[cs, :], preferred_element_type=F32)
            if first:
                dhacc[...] = part
            else:
                dhacc[...] += part

        dact_next = jnp.dot(dfv, wdt_ref[:, 0:cw], preferred_element_type=F32)
        for j in range(nchunk):
            ch = slice(j * cw, (j + 1) * cw)
            cg = ch
            cv = slice(f + j * cw, f + (j + 1) * cw)
            dact = dact_next
            if j + 1 < nchunk:
                dact_next = jnp.dot(dfv, wdt_ref[:, (j + 1) * cw:(j + 2) * cw], preferred_element_type=F32)
            g = u_ref[:, cg].astype(F32)
            v = u_ref[:, cv].astype(F32)
            sig, sil = _silu_parts(g)
            act_ref[:, ch] = (sil * v).astype(BF16)
            du_g = dact * v * (sig * (1.0 + g * (1.0 - sig)))
            du_v = dact * sil
            for k, (dbuf, du, cs) in enumerate(((dgbuf.at[j % FFN_NBUF], du_g, cg), (dvbuf.at[j % FFN_NBUF], du_v, cv))):
                dbuf[0:tr, :] = du
                dbuf[tr:tr + h, :] = carry[:, cs]
                carry[:, cs] = du[0:h, :]
                w = cw_ref[:, cs]
                xin = up_ref[:, cs].astype(F32)
                acc = None
                for sh in range(3):
                    dsh = dbuf[pl.ds(sh, tr), :]
                    term = w[2 - sh:3 - sh] * dsh
                    acc = term if acc is None else acc + term
                    dcw_ref[2 - sh:3 - sh, cs] += jnp.sum(xin * dsh, axis=0, keepdims=True)
                dupb = acc.astype(BF16)
                dup_ref[:, cs] = dupb
                dh_part(dupb, cs, j == 0 and k == 0)
        dh = dhacc[...]
        xv = x_ref[...]
        r1 = lax.rsqrt(jnp.mean(xv * xv, axis=-1, keepdims=True) + RMS_EPS)
        n1 = xv * r1
        dn1 = dh * gpre_ref[...]
        dx_ref[...] = dy + r1 * (dn1 - n1 * jnp.mean(dn1 * n1, axis=-1, keepdims=True))
        dgpre_ref[...] += jnp.sum(dh * n1, axis=0, keepdims=True)

    def rev(w):
        return pl.BlockSpec((tr, w), lambda i: (nt - 1 - i, 0))

    vec = _full((1, d))
    return pl.pallas_call(
        body, name=name, grid=(nt,),
        in_specs=[rev(d), rev(d), rev(d), rev(f2), rev(f2), vec, vec, WHOLE_VMEM, WHOLE_VMEM, _full((3, f2))],
        out_specs=[rev(d), rev(f2), rev(f), rev(d), _full((3, f2)), vec, vec],
        out_shape=[jax.ShapeDtypeStruct((s, d), F32), jax.ShapeDtypeStruct((s, f2), BF16),
                   jax.ShapeDtypeStruct((s, f), BF16), jax.ShapeDtypeStruct((s, d), BF16),
                   jax.ShapeDtypeStruct((3, f2), F32), jax.ShapeDtypeStruct((1, d), F32),
                   jax.ShapeDtypeStruct((1, d), F32)],
        scratch_shapes=[pltpu.VMEM((h, f2), F32), pltpu.VMEM((FFN_NBUF, tr + h, cw), F32),
                        pltpu.VMEM((FFN_NBUF, tr + h, cw), F32), pltpu.VMEM((tr, d), F32)],
        compiler_params=pltpu.CompilerParams(dimension_semantics=("arbitrary",), vmem_limit_bytes=FFN_BWD_VMEM),
    )(dxo, fout, x, up, u, g_pre, g_post, wd_t, wu_t, conv_w)


def _od_gate_fwd(z, conv_w, name):
    s, d3 = z.shape
    d = d3 // 3
    tr = min(256, s)
    h = HALO16
    cw = FFN_CW

    def body(z_ref, prev_ref, w_ref, o_ref, buf):
        first = pl.program_id(0) == 0
        for j in range(d // cw):
            cb = slice(j * cw, (j + 1) * cw)
            cc = slice(d + j * cw, d + (j + 1) * cw)
            cu = slice(2 * d + j * cw, 2 * d + (j + 1) * cw)
            buf[0:h, :] = jnp.where(first, 0.0, prev_ref[:, cc].astype(F32) * prev_ref[:, cu].astype(F32))
            buf[h:h + tr, :] = z_ref[:, cc].astype(F32) * z_ref[:, cu].astype(F32)
            k = _conv3_taps(buf, w_ref[:, cb], h - 2, tr)
            o_ref[:, cb] = (z_ref[:, cb].astype(F32) * k).astype(BF16)

    return pl.pallas_call(
        body, name=name, grid=(s // tr,),
        in_specs=[_cur(tr, d3), _prev(tr, h, d3), _full((3, d))],
        out_specs=_cur(tr, d),
        out_shape=jax.ShapeDtypeStruct((s, d), BF16),
        scratch_shapes=[pltpu.VMEM((h + tr, cw), F32)],
        compiler_params=_cparams(("parallel",)),
    )(z, z, conv_w)


def _od_gate_bwd(dy, z, conv_w, name):
    s, d3 = z.shape
    d = d3 // 3
    tr = min(256, s)
    h = HALO16
    cw = FFN_CW
    ext = tr + h

    def body(dy_ref, dyn_ref, z_ref, zp_ref, zn_ref, w_ref, o_ref, dw_ref, buf, dbuf):
        i = pl.program_id(0)
        first = i == 0
        last = i == pl.num_programs(0) - 1

        @pl.when(first)
        def _():
            dw_ref[...] = jnp.zeros_like(dw_ref)

        for j in range(d // cw):
            cb = slice(j * cw, (j + 1) * cw)
            cc = slice(d + j * cw, d + (j + 1) * cw)
            cu = slice(2 * d + j * cw, 2 * d + (j + 1) * cw)
            w = w_ref[:, cb]
            cval = z_ref[:, cc].astype(F32)
            uval = z_ref[:, cu].astype(F32)
            buf[0:h, :] = jnp.where(first, 0.0, zp_ref[:, cc].astype(F32) * zp_ref[:, cu].astype(F32))
            buf[h:h + tr, :] = cval * uval
            k = _conv3_taps(buf, w, h - 2, tr)
            dyv = dy_ref[:, cb]
            o_ref[:, cb] = (dyv * k).astype(BF16)
            dbuf[0:tr, :] = dyv * z_ref[:, cb].astype(F32)
            dbuf[tr:ext, :] = jnp.where(last, 0.0, dyn_ref[:, cb] * zn_ref[:, cb].astype(F32))
            dcu = w[2:3] * dbuf[pl.ds(0, tr), :] + w[1:2] * dbuf[pl.ds(1, tr), :] + w[0:1] * dbuf[pl.ds(2, tr), :]
            o_ref[:, cc] = (dcu * uval).astype(BF16)
            o_ref[:, cu] = (dcu * cval).astype(BF16)
            dk = dbuf[pl.ds(0, tr), :]
            for t in range(3):
                dw_ref[t:t + 1, cb] += jnp.sum(dk * buf[pl.ds(h - 2 + t, tr), :], axis=0, keepdims=True)

    return pl.pallas_call(
        body, name=name, grid=(s // tr,),
        in_specs=[_cur(tr, d), _next(tr, h, d, s), _cur(tr, d3), _prev(tr, h, d3), _next(tr, h, d3, s), _full((3, d))],
        out_specs=[_cur(tr, d3), _full((3, d))],
        out_shape=[jax.ShapeDtypeStruct((s, d3), BF16), jax.ShapeDtypeStruct((3, d), F32)],
        scratch_shapes=[pltpu.VMEM((h + tr, cw), F32), pltpu.VMEM((ext, cw), F32)],
        compiler_params=_cparams(("arbitrary",)),
    )(dy, dy, z, z, z, conv_w)


Q0 = 2 * A_CH
K0 = Q0 + N_Q_HEADS * HEAD_DIM
V0 = K0 + 2 * HEAD_DIM
EVEN_IN = V0 + 2 * HEAD_DIM


def _rope_tables(positions):
    half = ROPE_DIM // 2
    inv_freq = ROPE_THETA ** (-(jnp.arange(half, dtype=F32) * 2.0 / ROPE_DIM))
    ang = positions.astype(F32)[:, None] * inv_freq
    cs = jnp.concatenate([jnp.cos(ang), jnp.sin(ang)], axis=1)
    spread = np.zeros((2 * half, 3 * LANES), np.float32)
    const = np.zeros((1, 3 * LANES), np.float32)
    for lane in range(3 * LANES):
        dim, part = lane % HEAD_DIM, lane // LANES
        if part == 0:
            if dim < ROPE_DIM:
                spread[dim % half, lane] = 1.0
            else:
                const[0, lane] = 1.0
        elif part == 1 and half <= dim < ROPE_DIM:
            spread[half + dim - half, lane] = 1.0
        elif part == 2 and dim < half:
            spread[half + dim, lane] = -1.0
    return jnp.dot(cs, jnp.asarray(spread), precision=lax.Precision.HIGHEST) + jnp.asarray(const)


def _rope_fwd(x, tab):
    c, sa, sb = tab[:, 0:LANES], tab[:, LANES:2 * LANES], tab[:, 2 * LANES:3 * LANES]
    return x * c + pltpu.roll(x, 8, 1) * sa + pltpu.roll(x, LANES - 8, 1) * sb


def _rope_bwd(dy, tab):
    c, sa, sb = tab[:, 0:LANES], tab[:, LANES:2 * LANES], tab[:, 2 * LANES:3 * LANES]
    return dy * c + pltpu.roll(dy * sa, LANES - 8, 1) + pltpu.roll(dy * sb, 8, 1)


def _ln_fwd(c, g, b):
    mu = jnp.mean(c, axis=-1, keepdims=True)
    xc = c - mu
    r = lax.rsqrt(jnp.mean(xc * xc, axis=-1, keepdims=True) + LN_EPS)
    nrm = xc * r
    return nrm, r, nrm * g + b


def _phase_fill(buf, ph, rows):
    for k in range(1, 8):
        ph[k - 1, 0:rows - 8, :] = buf[pl.ds(k, rows - 8), :]


def _phase_rows(buf, ph, off, n, cs):
    k = off % 8
    src = buf if k == 0 else ph.at[k - 1]
    return src[pl.ds(off - k, n), cs]


def _ev_mid_fwd(z, tab, conv_w, conv_b, ln_g, ln_b, name):
    s = z.shape[0]
    tr = min(256, s)
    h = HALO32
    cw = LANES

    def body(z_ref, zp_ref, tab_ref, w_ref, b_ref, g_ref, lb_ref, c_ref, a_ref, qkv_ref, gbuf, cbuf, gph):
        first = pl.program_id(0) == 0
        glu_p = zp_ref[:, 0:A_CH].astype(F32) * jax.nn.sigmoid(zp_ref[:, A_CH:2 * A_CH].astype(F32))
        gbuf[0:h, :] = jnp.where(first, 0.0, glu_p)
        gbuf[h:h + tr, :] = z_ref[:, 0:A_CH].astype(F32) * jax.nn.sigmoid(z_ref[:, A_CH:2 * A_CH].astype(F32))
        _phase_fill(gbuf, gph, h + tr)
        for j in range(A_CH // cw):
            cs = slice(j * cw, (j + 1) * cw)
            acc = jnp.broadcast_to(b_ref[:, cs], (tr, cw))
            for t in range(A_CONV):
                acc = acc + w_ref[t:t + 1, cs] * _phase_rows(gbuf, gph, h - (A_CONV - 1) + t, tr, cs)
            cbuf[:, cs] = acc
        c = cbuf[...]
        c_ref[...] = c.astype(BF16)
        _, _, l = _ln_fwd(c, g_ref[...], lb_ref[...])
        a_ref[...] = (l * jax.nn.sigmoid(l)).astype(BF16)
        tab_v = tab_ref[...]
        for p in range(4):
            xq = z_ref[:, Q0 + p * LANES:Q0 + (p + 1) * LANES].astype(F32)
            qkv_ref[:, p * LANES:(p + 1) * LANES] = _rope_fwd(xq, tab_v).astype(BF16)
        lane = lax.broadcasted_iota(jnp.int32, (tr, LANES), 1)
        lo = lane < HEAD_DIM
        kr = _rope_fwd(z_ref[:, K0:K0 + LANES].astype(F32), tab_v)
        vr = z_ref[:, V0:V0 + LANES].astype(F32)
        for base, val in ((4 * LANES, kr), (6 * LANES, vr)):
            sw = pltpu.roll(val, HEAD_DIM, 1)
            qkv_ref[:, base:base + LANES] = jnp.where(lo, val, sw).astype(BF16)
            qkv_ref[:, base + LANES:base + 2 * LANES] = jnp.where(lo, sw, val).astype(BF16)

    return pl.pallas_call(
        body, name=name, grid=(s // tr,),
        in_specs=[_cur(tr, EVEN_IN), _prev(tr, h, 2 * A_CH), _cur(tr, 3 * LANES), _full((A_CONV, A_CH)),
                  _full((1, A_CH)), _full((1, A_CH)), _full((1, A_CH))],
        out_specs=[_cur(tr, A_CH), _cur(tr, A_CH), _cur(tr, 2 * A_CH)],
        out_shape=[jax.ShapeDtypeStruct((s, A_CH), BF16), jax.ShapeDtypeStruct((s, A_CH), BF16),
                   jax.ShapeDtypeStruct((s, 2 * A_CH), BF16)],
        scratch_shapes=[pltpu.VMEM((h + tr, A_CH), F32), pltpu.VMEM((tr, A_CH), F32),
                        pltpu.VMEM((7, h + tr, A_CH), F32)],
        compiler_params=_cparams(("parallel",)),
    )(z, z, tab, conv_w, conv_b, ln_g, ln_b)


def _ev_mid_bwd(dcat, c, z, dq, dkv, tab, conv_w, ln_g, ln_b, name):
    s = z.shape[0]
    tr = min(256, s)
    h = HALO32
    cw = LANES
    ext = tr + h

    def body(da_ref, dan_ref, c_ref, cn_ref, z_ref, dq_ref, dkv_ref, tab_ref, w_ref, g_ref, lb_ref,
             dz_ref, dw_ref, dvec_ref, dcbuf, dcph):
        i = pl.program_id(0)
        first = i == 0
        last = i == pl.num_programs(0) - 1

        @pl.when(first)
        def _():
            dw_ref[...] = jnp.zeros_like(dw_ref)
            dvec_ref[...] = jnp.zeros_like(dvec_ref)

        gv = g_ref[...]

        def ln_silu_bwd(cv, dav):
            nrm, r, l = _ln_fwd(cv, gv, lb_ref[...])
            sig = jax.nn.sigmoid(l)
            dl = dav * (sig * (1.0 + l * (1.0 - sig)))
            dn = dl * gv
            dc = r * (dn - jnp.mean(dn, axis=-1, keepdims=True) - nrm * jnp.mean(dn * nrm, axis=-1, keepdims=True))
            return dc, dl, nrm

        dc, dl, nrm = ln_silu_bwd(c_ref[...].astype(F32), da_ref[...])
        dcn, _, _ = ln_silu_bwd(cn_ref[...].astype(F32), dan_ref[...])
        dcbuf[0:tr, :] = dc
        dcbuf[tr:ext, :] = jnp.where(last, 0.0, dcn)
        dvec_ref[0:1, :] += jnp.sum(dc, axis=0, keepdims=True)
        dvec_ref[1:2, :] += jnp.sum(dl * nrm, axis=0, keepdims=True)
        dvec_ref[2:3, :] += jnp.sum(dl, axis=0, keepdims=True)

        _phase_fill(dcbuf, dcph, ext)
        a_lin = z_ref[:, 0:A_CH].astype(F32)
        sig_g = jax.nn.sigmoid(z_ref[:, A_CH:2 * A_CH].astype(F32))
        glu = a_lin * sig_g
        for j in range(A_CH // cw):
            cs = slice(j * cw, (j + 1) * cw)
            gluj = glu[:, cs]
            acc = jnp.zeros((tr, cw), F32)
            for t in range(A_CONV):
                dsh = _phase_rows(dcbuf, dcph, A_CONV - 1 - t, tr, cs)
                acc = acc + w_ref[t:t + 1, cs] * dsh
                dw_ref[t:t + 1, cs] += jnp.sum(gluj * dsh, axis=0, keepdims=True)
            dz_ref[:, cs] = (acc * sig_g[:, cs]).astype(BF16)
            dz_ref[:, A_CH + j * cw:A_CH + (j + 1) * cw] = (
                acc * a_lin[:, cs] * sig_g[:, cs] * (1.0 - sig_g[:, cs])).astype(BF16)

        tab_v = tab_ref[...]
        for p in range(4):
            cs = slice(p * LANES, (p + 1) * LANES)
            dz_ref[:, Q0 + p * LANES:Q0 + (p + 1) * LANES] = _rope_bwd(dq_ref[:, cs], tab_v).astype(BF16)
        lane = lax.broadcasted_iota(jnp.int32, (tr, LANES), 1)
        lo = lane < HEAD_DIM

        def fold(base):
            p0 = dkv_ref[:, base:base + LANES]
            p1 = dkv_ref[:, base + LANES:base + 2 * LANES]
            s0 = p0 + pltpu.roll(p0, HEAD_DIM, 1)
            s1 = p1 + pltpu.roll(p1, HEAD_DIM, 1)
            return jnp.where(lo, s0, s1)

        dz_ref[:, K0:K0 + LANES] = _rope_bwd(fold(0), tab_v).astype(BF16)
        dz_ref[:, V0:V0 + LANES] = fold(2 * LANES).astype(BF16)

    return pl.pallas_call(
        body, name=name, grid=(s // tr,),
        in_specs=[_cur(tr, A_CH), _next(tr, h, A_CH, s), _cur(tr, A_CH), _next(tr, h, A_CH, s),
                  _cur(tr, EVEN_IN), _cur(tr, A_CH), _cur(tr, A_CH), _cur(tr, 3 * LANES),
                  _full((A_CONV, A_CH)), _full((1, A_CH)), _full((1, A_CH))],
        out_specs=[_cur(tr, EVEN_IN), _full((A_CONV, A_CH)), _full((8, A_CH))],
        out_shape=[jax.ShapeDtypeStruct((s, EVEN_IN), BF16), jax.ShapeDtypeStruct((A_CONV, A_CH), F32),
                   jax.ShapeDtypeStruct((8, A_CH), F32)],
        scratch_shapes=[pltpu.VMEM((ext, A_CH), F32), pltpu.VMEM((7, ext, A_CH), F32)],
        compiler_params=_cparams(("arbitrary",)),
    )(dcat, dcat, c, c, z, dq, dkv, tab, conv_w, ln_g, ln_b)


NT = (((1,), (1,)), ((), ()))
TN = (((0,), (0,)), ((), ()))
QB = WINDOW
SCALE = HEAD_DIM ** -0.5


def _att_scores(q2m, kwin):
    return lax.dot_general(q2m, kwin, NT, preferred_element_type=F32)


def _att_probs(raw, sink, mask):
    sc = jnp.where(mask, raw * SCALE, -jnp.inf)
    mx = jnp.maximum(jnp.max(sc, axis=-1, keepdims=True), sink)
    p = jnp.exp(sc - mx)
    ps = jnp.exp(sink - mx)
    inv = 1.0 / (jnp.sum(p, axis=-1, keepdims=True) + ps)
    return p * inv, ps * inv


def _att_mask(i):
    r = lax.broadcasted_iota(jnp.int32, (QB, 2 * QB), 0)
    kc = lax.broadcasted_iota(jnp.int32, (QB, 2 * QB), 1)
    diff = r + QB - kc
    return (diff >= 0) & (diff < WINDOW) & ((kc >= QB) | (i > 0))


def _half_masks(dtype):
    lane = lax.broadcasted_iota(jnp.int32, (1, LANES), 1)
    return (lane < HEAD_DIM).astype(dtype), (lane >= HEAD_DIM).astype(dtype)


def _att_fwd(qkv, a, sinks, name):
    s = qkv.shape[0]
    nb = s // QB

    def body(sink_ref, qkv_ref, kvp_ref, a_ref, o_ref):
        i = pl.program_id(0)
        mask = _att_mask(i)
        mlo, mhi = _half_masks(BF16)
        o_ref[:, 0:A_CH] = a_ref[...]

        def window(col):
            return jnp.concatenate([kvp_ref[:, col * LANES:(col + 1) * LANES],
                                    qkv_ref[:, A_CH + col * LANES:A_CH + (col + 1) * LANES]], axis=0)

        def raw_scores(p):
            q2 = qkv_ref[:, p * LANES:(p + 1) * LANES]
            kwin = window(p // 2)
            return _att_scores(q2 * mlo, kwin), _att_scores(q2 * mhi, kwin)

        nxt = raw_scores(0)
        for p in range(4):
            raw_e, raw_o = nxt
            if p + 1 < 4:
                nxt = raw_scores(p + 1)
            vwin = window(2 + p // 2)
            pe, _ = _att_probs(raw_e, sink_ref[2 * p], mask)
            po, _ = _att_probs(raw_o, sink_ref[2 * p + 1], mask)
            o = (jnp.dot(pe.astype(BF16), vwin * mlo, preferred_element_type=F32)
                 + jnp.dot(po.astype(BF16), vwin * mhi, preferred_element_type=F32))
            o_ref[:, A_CH + p * LANES:A_CH + (p + 1) * LANES] = o.astype(BF16)

    grid_spec = pltpu.PrefetchScalarGridSpec(
        num_scalar_prefetch=1, grid=(nb,),
        in_specs=[pl.BlockSpec((QB, 2 * A_CH), lambda i, sk: (i, 0)),
                  pl.BlockSpec((QB, A_CH), lambda i, sk: (jnp.maximum(i - 1, 0), 1)),
                  pl.BlockSpec((QB, A_CH), lambda i, sk: (i, 0))],
        out_specs=pl.BlockSpec((QB, 2 * A_CH), lambda i, sk: (i, 0)),
    )
    return pl.pallas_call(
        body, name=name, grid_spec=grid_spec,
        out_shape=jax.ShapeDtypeStruct((s, 2 * A_CH), BF16),
        compiler_params=_cparams(("parallel",)),
    )(sinks, qkv, qkv, a)


def _att_bwd(qkv, dcat, sinks, name):
    s = qkv.shape[0]
    nb = s // QB

    def body(sink_ref, qkv_ref, kvp_ref, do_ref, dq_ref, dkv_ref, ds_ref, carry):
        i = pl.program_id(0)

        @pl.when(i == 0)
        def _():
            ds_ref[...] = jnp.zeros_like(ds_ref)
            carry[...] = jnp.zeros_like(carry)

        @pl.when(i < nb)
        def _():
            mask = _att_mask(i)
            mlo, mhi = _half_masks(BF16)
            dwin = [jnp.zeros((2 * QB, LANES), F32) for _ in range(4)]

            def window(col):
                return jnp.concatenate([kvp_ref[:, col * LANES:(col + 1) * LANES],
                                        qkv_ref[:, A_CH + col * LANES:A_CH + (col + 1) * LANES]], axis=0)

            def first_products(n):
                p, hm = n // 2, (mlo, mhi)[n % 2]
                qm = qkv_ref[:, p * LANES:(p + 1) * LANES] * hm
                dom = do_ref[:, p * LANES:(p + 1) * LANES].astype(BF16) * hm
                kwin, vwin = window(p // 2), window(2 + p // 2)
                return (qm, dom, kwin * hm, _att_scores(qm, kwin),
                        lax.dot_general(dom, vwin, NT, preferred_element_type=F32))

            nxt = first_products(0)
            dq2 = None
            for n in range(N_Q_HEADS):
                g = n // 4
                qm, dom, kwm, raw, dp = nxt
                if n + 1 < N_Q_HEADS:
                    nxt = first_products(n + 1)
                prob, psink = _att_probs(raw, sink_ref[n], mask)
                delta = jnp.sum(prob * dp, axis=-1, keepdims=True)
                dsc = (prob * (dp - delta) * SCALE).astype(BF16)
                ds_ref[n:n + 1, :] += jnp.broadcast_to(jnp.sum(-psink * delta, axis=0, keepdims=True), (1, LANES))
                part = jnp.dot(dsc, kwm, preferred_element_type=F32)
                dq2 = part if n % 2 == 0 else dq2 + part
                dwin[g] = dwin[g] + lax.dot_general(dsc, qm, TN, preferred_element_type=F32)
                dwin[2 + g] = dwin[2 + g] + lax.dot_general(prob.astype(BF16), dom, TN, preferred_element_type=F32)
                if n % 2 == 1:
                    dq_ref[:, (n // 2) * LANES:(n // 2 + 1) * LANES] = dq2
            for n in range(4):
                cs = slice(n * LANES, (n + 1) * LANES)
                dkv_ref[:, cs] = carry[:, cs] + dwin[n][0:QB, :]
                carry[:, cs] = dwin[n][QB:2 * QB, :]

        @pl.when(i == nb)
        def _():
            dkv_ref[...] = carry[...]

    grid_spec = pltpu.PrefetchScalarGridSpec(
        num_scalar_prefetch=1, grid=(nb + 1,),
        in_specs=[pl.BlockSpec((QB, 2 * A_CH), lambda i, sk: (jnp.minimum(i, nb - 1), 0)),
                  pl.BlockSpec((QB, A_CH), lambda i, sk: (jnp.maximum(jnp.minimum(i, nb - 1) - 1, 0), 1)),
                  pl.BlockSpec((QB, A_CH), lambda i, sk: (jnp.minimum(i, nb - 1), 1))],
        out_specs=[pl.BlockSpec((QB, A_CH), lambda i, sk: (jnp.minimum(i, nb - 1), 0)),
                   pl.BlockSpec((QB, A_CH), lambda i, sk: (jnp.maximum(i - 1, 0), 0)),
                   pl.BlockSpec((8, LANES), lambda i, sk: (0, 0))],
        scratch_shapes=[pltpu.VMEM((QB, A_CH), F32)],
    )
    return pl.pallas_call(
        body, name=name, grid_spec=grid_spec,
        out_shape=[jax.ShapeDtypeStruct((s, A_CH), F32), jax.ShapeDtypeStruct((s, A_CH), F32),
                   jax.ShapeDtypeStruct((8, LANES), F32)],
        compiler_params=_cparams(("arbitrary",)),
    )(sinks, qkv, qkv, dcat)


def _local_step(x, positions, target, w, fetch=None):
    row = lambda a, i: a[i:i + 1]
    tab = _rope_tables(positions)
    g = {}

    def ffn_fwd(xin, i):
        xout, f, h, up, u = _ffn_fwd(xin, row(w["ffn_norm_pre"], i), w["ffn_w_up", i], w["ffn_conv_w"][i],
                                     w["ffn_w_down", i], row(w["ffn_norm_post"], i), f"ffn{i}_fwd")
        return xout, (xin, f, h, up, u)

    def ffn_bwd(dxout, saved, i):
        xin, f, h, up, u = saved
        dxin, dup, act, df, d_cw, dg_post, dg_pre = _ffn_bwd(
            dxout, f, xin, up, u, row(w["ffn_norm_pre"], i), row(w["ffn_norm_post"], i), w["ffn_w_down_t", i],
            w["ffn_w_up_t", i], w["ffn_conv_w"][i], f"ffn{i}_bwd")
        d_down = _mm_tn(act, df, f"ffn{i}_down_dw")
        d_up = _mm_tn(dup, h, f"ffn{i}_up_dw")
        return dxin, dict(ffn_norm_post=dg_post, ffn_norm_pre=dg_pre, ffn_w_up=d_up, ffn_conv_w=d_cw, ffn_w_down=d_down)

    h0, z0 = _mm_pre(x, row(w["mix_norm_pre"], 0), w["ev_w_in"], "ev_in")
    c0, a0, qkv = _ev_mid_fwd(z0, tab, w["ev_a_conv_w"], w["ev_a_conv_b"], w["ev_a_ln_g"], w["ev_a_ln_b"], "ev_mid")
    cat = _att_fwd(qkv, a0, w["ev_sinks"], "ev_att")
    m0, x1 = _mm_post(cat, w["ev_w_out"], row(w["mix_norm_post"], 0), x, "ev_out")
    if fetch is not None:
        w = {**w, **fetch("ffn0", x1)}
    x2, ffn0 = ffn_fwd(x1, 0)
    if fetch is not None:
        w = {**w, **fetch("layer1", x2)}
    h2, z1 = _mm_pre(x2, row(w["mix_norm_pre"], 1), w["od_w_in"], "od_in")
    y1 = _od_gate_fwd(z1, w["od_conv_w"], "od_mid")
    m1, x3 = _mm_post(y1, w["od_w_out"], row(w["mix_norm_post"], 1), x2, "od_out")
    x4, ffn1 = ffn_fwd(x3, 1)

    sq, dx4 = _loss_bwd(x4, target, "loss")

    dx3, gf1 = ffn_bwd(dx4, ffn1, 1)
    dm1, dy1, dg_mo1 = _mm_post_bwd(dx3, m1, row(w["mix_norm_post"], 1), w["od_w_out_t"], "od_out_bwd")
    g["od_w_out"] = _mm_tn(y1, dm1, "od_out_dw")
    dz1, g["od_conv_w"] = _od_gate_bwd(dy1, z1, w["od_conv_w"], "od_mid_bwd")
    g["od_w_in"] = _mm_tn(dz1, h2, "od_in_dw")
    dx2, dg_mp1 = _mm_pre_bwd(dz1, w["od_w_in_t"], x2, row(w["mix_norm_pre"], 1), dx3, "od_in_bwd")

    dx1, gf0 = ffn_bwd(dx2, ffn0, 0)
    dm0, dcat, dg_mo0 = _mm_post_bwd(dx1, m0, row(w["mix_norm_post"], 0), w["ev_w_out_t"], "ev_out_bwd")
    g["ev_w_out"] = _mm_tn(cat, dm0, "ev_out_dw")
    dq, dkv, dsk = _att_bwd(qkv, dcat, w["ev_sinks"], "ev_att_bwd")
    dz0, g["ev_a_conv_w"], dvec = _ev_mid_bwd(dcat, c0, z0, dq, dkv, tab, w["ev_a_conv_w"], w["ev_a_ln_g"],
                                              w["ev_a_ln_b"], "ev_mid_bwd")
    g["ev_w_in"] = _mm_tn(dz0, h0, "ev_in_dw")
    dx0, dg_mp0 = _mm_pre_bwd(dz0, w["ev_w_in_t"], x, row(w["mix_norm_pre"], 0), dx1, "ev_in_bwd")

    g["ev_a_conv_b"] = dvec[0:1]
    g["ev_a_ln_g"] = dvec[1:2]
    g["ev_a_ln_b"] = dvec[2:3]
    g["ev_sinks"] = dsk[:, 0]
    g["mix_norm_pre"] = jnp.concatenate([dg_mp0, dg_mp1], axis=0)
    g["mix_norm_post"] = jnp.concatenate([dg_mo0, dg_mo1], axis=0)
    for k in ("ffn_norm_pre", "ffn_norm_post", "ffn_w_up", "ffn_conv_w", "ffn_w_down"):
        g[k] = jnp.stack([gf0[k], gf1[k]], axis=0) if gf0[k].shape[0] != 1 else jnp.concatenate([gf0[k], gf1[k]], axis=0)
    return sq, dx0, g


ANY = pl.BlockSpec(memory_space=pl.ANY)
PACK_COLS = 1024


def _me():
    return lax.axis_index("x"), lax.axis_index("y"), lax.axis_index("c")


def _other_chips(x, y):
    return [(1 - x, y), (x, 1 - y), (1 - x, 1 - y)]


def _remote(src, dst, send, recv, dev):
    return pltpu.make_async_remote_copy(src_ref=src, dst_ref=dst, send_sem=send, recv_sem=recv,
                                        device_id=dev, device_id_type=MESH)


def _gather_chips(wp, name):
    r, cols = wp.shape
    rh = r // 2

    def body(w_ref, o_ref, send, recv):
        x, y, c = _me()
        p = 2 * x + y
        sib = (x, y, 1 - c)
        chips = _other_chips(x, y)
        half = pl.ds(c * rh, rh)
        other = pl.ds((1 - c) * rh, rh)
        sent = [_remote(w_ref.at[half], o_ref.at[p, half], send.at[k], recv.at[k], (cx, cy, c))
                for k, (cx, cy) in enumerate(chips)]
        for cp in sent:
            cp.start()
        for k, (cx, cy) in enumerate(chips):
            q = 2 * cx + cy
            _remote(w_ref.at[half], o_ref.at[q, half], send.at[k], recv.at[k], (cx, cy, c)).wait_recv()
            fwd = _remote(o_ref.at[q, half], o_ref.at[q, half], send.at[3 + k], recv.at[3 + k], sib)
            fwd.start()
            sent.append(fwd)
        for k, (cx, cy) in enumerate(chips):
            q = 2 * cx + cy
            _remote(o_ref.at[q, other], o_ref.at[q, other], send.at[3 + k], recv.at[3 + k], sib).wait_recv()
        for cp in sent:
            cp.wait_send()

    return pl.pallas_call(
        body, name=name, in_specs=[ANY], out_specs=ANY,
        out_shape=jax.ShapeDtypeStruct((N_CHIPS, r, cols), wp.dtype),
        scratch_shapes=[pltpu.SemaphoreType.DMA((6,)), pltpu.SemaphoreType.DMA((6,))],
    )(wp)


HBM_SPEC = pl.BlockSpec(memory_space=pltpu.HBM)
SEM_SPEC = pl.BlockSpec(memory_space=pltpu.SEMAPHORE)
DATAFLOW = pltpu.SideEffectType.DATAFLOW_SIDE_EFFECTING


def _gather_plan(w_ref, land_ref):
    x, y, c = _me()
    return [(w_ref, land_ref.at[2 * x + y], (cx, cy, c)) for cx, cy in _other_chips(x, y)]


def _copies_start(src, land_shape, plan, n, name):
    def body(src_ref, land_ref, send, recv, src_thru, land_thru, token):
        for k, (s_view, d_view, dev) in enumerate(plan(src_ref, land_ref)):
            _remote(s_view, d_view, send.at[k], recv.at[k], dev).start()
        token[...] = jnp.zeros_like(token)

    return pl.pallas_call(
        body, name=name,
        out_shape=(pltpu.SemaphoreType.DMA((n,)), pltpu.SemaphoreType.DMA((n,)), pltpu.HBM(src.shape, src.dtype),
                   pltpu.HBM(land_shape, src.dtype), jax.ShapeDtypeStruct((8, LANES), F32)),
        in_specs=(HBM_SPEC, HBM_SPEC),
        out_specs=(SEM_SPEC, SEM_SPEC, HBM_SPEC, HBM_SPEC, pl.BlockSpec(memory_space=pltpu.VMEM)),
        input_output_aliases={0: 2, 1: 3},
        compiler_params=pltpu.CompilerParams(has_side_effects=DATAFLOW),
    )(pltpu.with_memory_space_constraint(src, pltpu.HBM),
      pltpu.with_memory_space_constraint(lax.empty(land_shape, src.dtype), pltpu.HBM))


def _copies_wait(started, after, plan, name):
    send, recv, src_thru, land_thru, _ = started

    def body(src_ref, land_ref, send, recv, after_ref, src_dead, land_out):
        for k, (s_view, d_view, dev) in enumerate(plan(src_ref, land_ref)):
            cp = _remote(s_view, d_view, send.at[k], recv.at[k], dev)
            cp.wait_send()
            cp.wait_recv()

    return pl.pallas_call(
        body, name=name,
        out_shape=(pltpu.HBM(src_thru.shape, src_thru.dtype), pltpu.HBM(land_thru.shape, land_thru.dtype)),
        in_specs=(HBM_SPEC, HBM_SPEC, SEM_SPEC, SEM_SPEC, ANY),
        out_specs=(HBM_SPEC, HBM_SPEC),
        input_output_aliases={0: 0, 1: 1},
        compiler_params=pltpu.CompilerParams(has_side_effects=DATAFLOW),
    )(src_thru, land_thru, send, recv, after)[1]


def _exchange8(v, reduce, name):
    r, cols = v.shape
    rel = [(a, b, d) for a in (0, 1) for b in (0, 1) for d in (0, 1) if (a, b, d) != (0, 0, 0)]

    def body(v_ref, o_ref, *rest):
        if reduce:
            gbuf, send, recv = rest
        else:
            gbuf = o_ref
            send, recv = rest
        x, y, c = _me()
        me = 4 * x + 2 * y + c
        gbuf[me] = v_ref[...]
        sent = []
        for k, (a, b, d) in enumerate(rel):
            cp = _remote(v_ref, gbuf.at[me], send.at[k], recv.at[k], ((x + a) % 2, (y + b) % 2, (c + d) % 2))
            cp.start()
            sent.append(cp)
        for k, (a, b, d) in enumerate(rel):
            src = 4 * ((x + a) % 2) + 2 * ((y + b) % 2) + (c + d) % 2
            _remote(v_ref, gbuf.at[src], send.at[k], recv.at[k], (x, y, c)).wait_recv()
        for cp in sent:
            cp.wait_send()
        if reduce:
            acc = gbuf[0]
            for n in range(1, 8):
                acc = acc + gbuf[n]
            o_ref[...] = acc

    vmem = pl.BlockSpec(memory_space=pltpu.VMEM)
    sems = [pltpu.SemaphoreType.DMA((7,)), pltpu.SemaphoreType.DMA((7,))]
    if reduce:
        out_shape = jax.ShapeDtypeStruct((r, cols), F32)
        scratch = [pltpu.VMEM((8, r, cols), F32)] + sems
    else:
        out_shape = jax.ShapeDtypeStruct((8, r, cols), F32)
        scratch = sems
    return pl.pallas_call(body, name=name, in_specs=[vmem], out_specs=vmem, out_shape=out_shape,
                          scratch_shapes=scratch)(v)


def _rs_swap(g, name):
    _, _, rh, cols = g.shape

    def body(g_ref, o_ref, send, recv):
        x, y, c = _me()
        cps = [_remote(g_ref.at[q, 1 - c], o_ref.at[q], send.at[q], recv.at[q], (x, y, 1 - c)) for q in range(N_CHIPS)]
        for cp in cps:
            cp.start()
        for cp in cps:
            cp.wait()

    return pl.pallas_call(
        body, name=name, in_specs=[ANY], out_specs=ANY,
        out_shape=jax.ShapeDtypeStruct((N_CHIPS, rh, cols), F32),
        scratch_shapes=[pltpu.SemaphoreType.DMA((N_CHIPS,)), pltpu.SemaphoreType.DMA((N_CHIPS,))],
    )(g)


def _row_tile(rows, pref):
    if rows <= pref:
        return rows
    t = (pref // 8) * 8
    while t >= 8:
        if rows % t == 0:
            return t
        t -= 8
    return rows


def _rs_add(g, sib, c, name):
    _, _, rh, cols = g.shape
    tr = _row_tile(rh, 512)

    def body(c_ref, g_ref, s_ref, o_ref):
        o_ref[...] = (g_ref[...] + s_ref[...]).astype(BF16)

    grid_spec = pltpu.PrefetchScalarGridSpec(
        num_scalar_prefetch=1, grid=(N_CHIPS, rh // tr),
        in_specs=[pl.BlockSpec((None, None, tr, cols), lambda q, i, cr: (q, cr[0], i, 0)),
                  pl.BlockSpec((None, tr, cols), lambda q, i, cr: (q, i, 0))],
        out_specs=pl.BlockSpec((None, tr, cols), lambda q, i, cr: (q, i, 0)),
    )
    return pl.pallas_call(
        body, name=name, grid_spec=grid_spec,
        out_shape=jax.ShapeDtypeStruct((N_CHIPS, rh, cols), BF16),
        compiler_params=_cparams(("parallel", "parallel")),
    )(c, g, sib)


def _rs_ici(a, name):
    _, rh, cols = a.shape

    def body(a_ref, o_ref, send, recv):
        x, y, c = _me()
        p = 2 * x + y
        cps = []
        for k, (cx, cy) in enumerate(_other_chips(x, y)):
            cp = _remote(a_ref.at[2 * cx + cy], o_ref.at[p], send.at[k], recv.at[k], (cx, cy, c))
            cp.start()
            cps.append(cp)
        for k, (cx, cy) in enumerate(_other_chips(x, y)):
            q = 2 * cx + cy
            _remote(a_ref.at[q], o_ref.at[q], send.at[k], recv.at[k], (cx, cy, c)).wait_recv()
        for cp in cps:
            cp.wait_send()

    return pl.pallas_call(
        body, name=name, in_specs=[ANY], out_specs=ANY,
        out_shape=jax.ShapeDtypeStruct((N_CHIPS, rh, cols), a.dtype),
        scratch_shapes=[pltpu.SemaphoreType.DMA((3,)), pltpu.SemaphoreType.DMA((3,))],
    )(a)


def _rs_sum(rb, a, chip, name):
    _, rh, cols = rb.shape
    tr = _row_tile(rh, 512)

    def body(p_ref, r0, r1, r2, r3, own, o_ref):
        p = p_ref[0]
        ownv = own[...].astype(F32)
        acc = None
        for q, r in enumerate((r0, r1, r2, r3)):
            v = jnp.where(p == q, ownv, r[...].astype(F32))
            acc = v if acc is None else acc + v
        o_ref[...] = acc

    def spec(q):
        return pl.BlockSpec((None, tr, cols), lambda i, pr: (jnp.where(pr[0] == q, (q + 1) % N_CHIPS, q), i, 0))

    grid_spec = pltpu.PrefetchScalarGridSpec(
        num_scalar_prefetch=1, grid=(rh // tr,),
        in_specs=[spec(0), spec(1), spec(2), spec(3), pl.BlockSpec((None, tr, cols), lambda i, pr: (pr[0], i, 0))],
        out_specs=pl.BlockSpec((tr, cols), lambda i, pr: (i, 0)),
    )
    return pl.pallas_call(
        body, name=name, grid_spec=grid_spec,
        out_shape=jax.ShapeDtypeStruct((rh, cols), F32),
        compiler_params=_cparams(("parallel",)),
    )(chip, rb, rb, rb, rb, a)


def _rs_share(hsum, name):
    rh, cols = hsum.shape

    def body(h_ref, o_ref, send, recv):
        x, y, c = _me()
        cp = _remote(h_ref, o_ref, send, recv, (x, y, 1 - c))
        cp.start()
        cp.wait()

    return pl.pallas_call(
        body, name=name, in_specs=[ANY], out_specs=ANY,
        out_shape=jax.ShapeDtypeStruct((rh, cols), F32),
        scratch_shapes=[pltpu.SemaphoreType.DMA, pltpu.SemaphoreType.DMA],
    )(hsum)


def _adamw(w, g, m, v, name):
    rows, cols = w.shape
    tr = _row_tile(rows, 512)

    def body(w_ref, g_ref, m_ref, v_ref, d_ref, nm_ref, nv_ref):
        gv = g_ref[...]
        nm = ADAM_B1 * m_ref[...] + (1.0 - ADAM_B1) * gv
        nv = ADAM_B2 * v_ref[...] + (1.0 - ADAM_B2) * (gv * gv)
        m_hat = nm / (1.0 - ADAM_B1 ** ADAM_STEP)
        v_hat = nv / (1.0 - ADAM_B2 ** ADAM_STEP)
        d_ref[...] = -ADAM_LR * (m_hat / (jnp.sqrt(v_hat) + ADAM_EPS) + ADAM_WD * w_ref[...])
        nm_ref[...] = nm
        nv_ref[...] = nv

    spec = pl.BlockSpec((tr, cols), lambda i: (i, 0))
    shp = jax.ShapeDtypeStruct((rows, cols), F32)
    return pl.pallas_call(
        body, name=name, grid=(rows // tr,), in_specs=[spec] * 4, out_specs=[spec] * 3, out_shape=[shp] * 3,
        compiler_params=_cparams(("parallel",)),
    )(w, g, m, v)


WEIGHTS = ("mix_norm_pre", "mix_norm_post", "ffn_norm_pre", "ffn_norm_post", "ev_w_in", "ev_a_conv_w", "ev_a_conv_b",
           "ev_a_ln_g", "ev_a_ln_b", "ev_sinks", "ev_w_out", "od_w_in", "od_conv_w", "od_w_out", "ffn_w_up",
           "ffn_conv_w", "ffn_w_down")
MATS = (("ev_w_in", 2), ("ev_w_out", 1), ("od_w_in", 2), ("od_w_out", 1), ("ffn_w_up", 2), ("ffn_w_down", 1))
UNITS = (("ev_w_in", 0, 2), ("ev_w_out", 0, 1), ("ffn_w_up", 0, 2), ("ffn_w_down", 0, 1),
         ("od_w_in", 0, 2), ("od_w_out", 0, 1), ("ffn_w_up", 1, 2), ("ffn_w_down", 1, 1))
GATHER_GROUPS = ((0, 1), (2, 3), (4, 5, 6, 7))
SMALL_SHARDED = ("ev_a_conv_w", "od_conv_w", "ffn_conv_w")
REPLICATED = ("mix_norm_pre", "mix_norm_post", "ffn_norm_pre", "ffn_norm_post", "ev_a_conv_b", "ev_a_ln_g",
              "ev_a_ln_b", "ev_sinks")


def _pack(parts, rows_multiple):
    flat = jnp.concatenate([p.reshape(-1) for p in parts])
    unit = rows_multiple * PACK_COLS
    pad = (-flat.shape[0]) % unit
    if pad:
        flat = jnp.concatenate([flat, jnp.zeros((pad,), flat.dtype)])
    return flat.reshape(-1, PACK_COLS)


def _unpack(buf, shapes):
    flat = buf.reshape(-1)
    out, off = [], 0
    for shp in shapes:
        n = 1
        for d in shp:
            n *= d
        out.append(flat[off:off + n].reshape(shp))
        off += n
    return out


def _shard_rows(shard, axis):
    if axis == 2:
        shard = jnp.swapaxes(shard, 1, 2)
    return shard.reshape(-1, PACK_COLS)


def kernel(x, positions, mix_norm_pre, mix_norm_post, ffn_norm_pre, ffn_norm_post, ev_w_in, ev_a_conv_w, ev_a_conv_b, ev_a_ln_g, ev_a_ln_b, ev_sinks, ev_w_out, od_w_in, od_conv_w, od_w_out, ffn_w_up, ffn_conv_w, ffn_w_down, loss_target, m_mix_norm_pre, m_mix_norm_post, m_ffn_norm_pre, m_ffn_norm_post, m_ev_w_in, m_ev_a_conv_w, m_ev_a_conv_b, m_ev_a_ln_g, m_ev_a_ln_b, m_ev_sinks, m_ev_w_out, m_od_w_in, m_od_conv_w, m_od_w_out, m_ffn_w_up, m_ffn_conv_w, m_ffn_w_down, v_mix_norm_pre, v_mix_norm_post, v_ffn_norm_pre, v_ffn_norm_post, v_ev_w_in, v_ev_a_conv_w, v_ev_a_conv_b, v_ev_a_ln_g, v_ev_a_ln_b, v_ev_sinks, v_ev_w_out, v_od_w_in, v_od_conv_w, v_od_w_out, v_ffn_w_up, v_ffn_conv_w, v_ffn_w_down):
    wts = dict(zip(WEIGHTS, (mix_norm_pre, mix_norm_post, ffn_norm_pre, ffn_norm_post, ev_w_in, ev_a_conv_w, ev_a_conv_b,
                             ev_a_ln_g, ev_a_ln_b, ev_sinks, ev_w_out, od_w_in, od_conv_w, od_w_out, ffn_w_up, ffn_conv_w,
                             ffn_w_down)))
    mom = dict(zip(WEIGHTS, (m_mix_norm_pre, m_mix_norm_post, m_ffn_norm_pre, m_ffn_norm_post, m_ev_w_in, m_ev_a_conv_w,
                             m_ev_a_conv_b, m_ev_a_ln_g, m_ev_a_ln_b, m_ev_sinks, m_ev_w_out, m_od_w_in, m_od_conv_w,
                             m_od_w_out, m_ffn_w_up, m_ffn_conv_w, m_ffn_w_down)))
    var = dict(zip(WEIGHTS, (v_mix_norm_pre, v_mix_norm_post, v_ffn_norm_pre, v_ffn_norm_post, v_ev_w_in, v_ev_a_conv_w,
                             v_ev_a_conv_b, v_ev_a_ln_g, v_ev_a_ln_b, v_ev_sinks, v_ev_w_out, v_od_w_in, v_od_conv_w,
                             v_od_w_out, v_ffn_w_up, v_ffn_conv_w, v_ffn_w_down)))
    xi, yi, ci = _me()
    chip = 2 * xi + yi

    unit_rows = [_shard_rows(wts[k][l:l + 1].astype(BF16), axis) for k, l, axis in UNITS]

    def group_block(group):
        return jnp.concatenate([unit_rows[u] for u in group], axis=0)

    def unpack_group(group, landed, own):
        full = lax.dynamic_update_slice(landed, own[None], (chip, 0, 0))
        out, off = {}, 0
        for u in group:
            k, l, axis = UNITS[u]
            n = unit_rows[u].shape[0]
            native = full[:, off:off + n].reshape(N_CHIPS * n, PACK_COLS)
            off += n
            key = (lambda name: (name, l)) if k.startswith("ffn") else (lambda name: name)
            out[key(k + "_t" if axis == 2 else k)] = native
            out[key(k if axis == 2 else k + "_t")] = native.T
        return out

    small_shapes = [wts[k].shape for k in SMALL_SHARDED]
    small_all = _exchange8(_pack([wts[k] for k in SMALL_SHARDED], 8), False, "gather_small")
    blocks = [group_block(grp) for grp in GATHER_GROUPS]
    first = _gather_chips(blocks[0], "gather_mats")
    later = {}
    for stage, grp, blk in zip(("ffn0", "layer1"), GATHER_GROUPS[1:], blocks[1:]):
        later[stage] = (grp, blk, _copies_start(blk, (N_CHIPS,) + blk.shape, _gather_plan, 3, "gather_" + stage + "_start"))

    def fetch(stage, after):
        grp, blk, started = later[stage]
        return unpack_group(grp, _copies_wait(started, after, _gather_plan, "gather_" + stage + "_wait"), blk)

    w = {k: wts[k] for k in REPLICATED}
    w.update(unpack_group(GATHER_GROUPS[0], first, blocks[0]))
    per_chip = [_unpack(small_all[2 * q], small_shapes) for q in range(N_CHIPS)]
    for n, k in enumerate(SMALL_SHARDED):
        w[k] = jnp.concatenate([per_chip[q][n] for q in range(N_CHIPS)], axis=-1)
    for k in ("ev_a_conv_w", "od_conv_w"):
        w[k] = w[k][0]
    w["ev_sinks"] = w["ev_sinks"][0]
    w["mix_norm_pre"] = w["mix_norm_pre"] + sum(later[s][2][4][0, 0] for s in later)

    sq, dx, g = _local_step(x[0], positions[0], loss_target[0], w, fetch)
    loss = lax.psum(0.5 * jnp.sum(sq) / D_MODEL, ("x", "y", "c"))

    gp = jnp.concatenate([(g[k][l] if k.startswith("ffn") else g[k]).reshape(N_CHIPS, -1, PACK_COLS)
                          for k, l, _ in UNITS], axis=1)
    rows = gp.shape[1]
    gp = gp.reshape(N_CHIPS, 2, rows // 2, PACK_COLS)
    core = jnp.reshape(ci, (1,)).astype(jnp.int32)
    sib = _rs_swap(gp, "rs_swap")
    pair = _rs_add(gp, sib, core, "rs_add")
    landed = _rs_ici(pair, "rs_ici")
    half = _rs_sum(landed, pair, jnp.reshape(chip, (1,)).astype(jnp.int32), "rs_sum")
    other = _rs_share(half, "rs_share")
    red = jnp.concatenate([jnp.where(ci == 0, half, other), jnp.where(ci == 0, other, half)], axis=0)
    per_layer = {}
    off = 0
    for (k, l, axis), blk in zip(UNITS, unit_rows):
        n = blk.shape[0]
        part = red[off:off + n]
        off += n
        per_layer[k, l] = part.T if axis == 2 else part
    grads = {k: jnp.stack([per_layer[k, l] for l in range(wts[k].shape[0])], axis=0) for k, _ in MATS}

    small_keys = REPLICATED + SMALL_SHARDED
    full_shapes = [wts[k].shape for k in REPLICATED] + [wts[k].shape[:-1] + (wts[k].shape[-1] * N_CHIPS,) for k in SMALL_SHARDED]
    sm = _exchange8(_pack([g[k] for k in small_keys], 8), True, "reduce_small")
    for k, full in zip(small_keys, _unpack(sm, full_shapes)):
        if k in SMALL_SHARDED:
            n = wts[k].shape[-1]
            full = lax.dynamic_slice_in_dim(full, chip * n, n, axis=full.ndim - 1)
        grads[k] = full

    deltas, new_m, new_v = {}, {}, {}
    for k in WEIGHTS:
        shp = wts[k].shape
        two_d = (-1, shp[-1])
        d, nm, nv = _adamw(wts[k].reshape(two_d), grads[k].reshape(two_d), mom[k].reshape(two_d), var[k].reshape(two_d),
                           "adamw_" + k)
        deltas[k], new_m[k], new_v[k] = d.reshape(shp), nm.reshape(shp), nv.reshape(shp)

    return (loss, dx[None], *[grads[k] for k in WEIGHTS], *[deltas[k] for k in WEIGHTS],
            *[new_m[k] for k in WEIGHTS], *[new_v[k] for k in WEIGHTS])
```

```python
import functools

import jax
import jax.numpy as jnp
import numpy as np
from jax import lax
from jax.experimental import pallas as pl
from jax.experimental.pallas import tpu as pltpu

F32 = jnp.float32
BF16 = jnp.bfloat16
MESH = pl.DeviceIdType.MESH

D_MODEL = 1024
HEAD_DIM = 64
A_CH = 512
A_CONV = 31
N_Q_HEADS = 8
WINDOW = 128
ROPE_THETA = 500000.0
ROPE_DIM = 16
D_FF = 2816
RMS_EPS = 1e-6
LN_EPS = 1e-5
ADAM_LR = 0.001
ADAM_B1 = 0.9
ADAM_B2 = 0.999
ADAM_EPS = 1e-08
ADAM_WD = 0.01
ADAM_STEP = 10

LANES = 128
HALO16 = 16
HALO32 = 32
VMEM_LIMIT = 56 * 1024 * 1024
FFN_BWD_VMEM = 60 * 1024 * 1024
N_CHIPS = 4


def _cparams(sem):
    return pltpu.CompilerParams(dimension_semantics=sem, vmem_limit_bytes=VMEM_LIMIT)


def _tile(n, pref):
    if n <= pref:
        return n
    t = (pref // LANES) * LANES
    while t >= LANES:
        if n % t == 0:
            return t
        t -= LANES
    return n


MM_ROWS = 512


def _rms_scale(v):
    return lax.rsqrt(jnp.mean(v * v, axis=-1, keepdims=True) + RMS_EPS)


def _rms_bwd(dy, v, g):
    r = _rms_scale(v)
    nrm = v * r
    dn = dy * g
    return r * (dn - nrm * jnp.mean(dn * nrm, axis=-1, keepdims=True)), jnp.sum(dy * nrm, axis=0, keepdims=True)


def _mm_pre(x, g, w, name):
    s, d = x.shape
    n = w.shape[1]
    tm = min(MM_ROWS, s)
    tn = _tile(n, 1024)

    def body(x_ref, g_ref, w_ref, h_ref, z_ref):
        xv = x_ref[...]
        hv = (xv * _rms_scale(xv) * g_ref[...]).astype(BF16)
        h_ref[...] = hv
        for j in range(n // tn):
            cs = slice(j * tn, (j + 1) * tn)
            z_ref[:, cs] = jnp.dot(hv, w_ref[:, cs], preferred_element_type=F32).astype(BF16)

    return pl.pallas_call(
        body, name=name, grid=(s // tm,),
        in_specs=[pl.BlockSpec((tm, d), lambda i: (i, 0)), pl.BlockSpec((1, d), lambda i: (0, 0)),
                  pl.BlockSpec(memory_space=pltpu.VMEM)],
        out_specs=[pl.BlockSpec((tm, d), lambda i: (i, 0)), pl.BlockSpec((tm, n), lambda i: (i, 0))],
        out_shape=[jax.ShapeDtypeStruct((s, d), BF16), jax.ShapeDtypeStruct((s, n), BF16)],
        compiler_params=_cparams(("parallel",)),
    )(x, g, w)


def _mm_post(a, w, g, xres, name):
    s, k = a.shape
    d = w.shape[1]
    tm = min(MM_ROWS, s)

    def body(a_ref, w_ref, g_ref, x_ref, m_ref, o_ref):
        mv = jnp.dot(a_ref[...], w_ref[...], preferred_element_type=F32)
        m_ref[...] = mv
        o_ref[...] = x_ref[...] + mv * _rms_scale(mv) * g_ref[...]

    row = pl.BlockSpec((tm, d), lambda i: (i, 0))
    return pl.pallas_call(
        body, name=name, grid=(s // tm,),
        in_specs=[pl.BlockSpec((tm, k), lambda i: (i, 0)), _full((k, d)), _full((1, d)), row],
        out_specs=[row, row],
        out_shape=[jax.ShapeDtypeStruct((s, d), F32), jax.ShapeDtypeStruct((s, d), F32)],
        compiler_params=_cparams(("parallel",)),
    )(a, w, g, xres)


def _mm_post_bwd(dy, m, g, w_t, name):
    s, d = m.shape
    k = w_t.shape[1]
    tm = min(MM_ROWS, s)

    def body(dy_ref, m_ref, g_ref, wt_ref, dm_ref, da_ref, dg_ref):
        @pl.when(pl.program_id(0) == 0)
        def _():
            dg_ref[...] = jnp.zeros_like(dg_ref)

        dm, dg = _rms_bwd(dy_ref[...], m_ref[...], g_ref[...])
        dg_ref[...] += dg
        dmb = dm.astype(BF16)
        dm_ref[...] = dmb
        da_ref[...] = jnp.dot(dmb, wt_ref[...], preferred_element_type=F32)

    row = pl.BlockSpec((tm, d), lambda i: (i, 0))
    return pl.pallas_call(
        body, name=name, grid=(s // tm,),
        in_specs=[row, row, _full((1, d)), _full((d, k))],
        out_specs=[row, pl.BlockSpec((tm, k), lambda i: (i, 0)), _full((1, d))],
        out_shape=[jax.ShapeDtypeStruct((s, d), BF16), jax.ShapeDtypeStruct((s, k), F32),
                   jax.ShapeDtypeStruct((1, d), F32)],
        compiler_params=_cparams(("arbitrary",)),
    )(dy, m, g, w_t)


def _mm_pre_bwd(dz, w_t, x, g, res, name):
    s, k = dz.shape
    d = w_t.shape[1]
    tm = min(MM_ROWS, s)

    def body(dz_ref, wt_ref, x_ref, g_ref, res_ref, dx_ref, dg_ref):
        @pl.when(pl.program_id(0) == 0)
        def _():
            dg_ref[...] = jnp.zeros_like(dg_ref)

        dh = jnp.dot(dz_ref[...], wt_ref[...], preferred_element_type=F32)
        dx, dg = _rms_bwd(dh, x_ref[...], g_ref[...])
        dg_ref[...] += dg
        dx_ref[...] = res_ref[...] + dx

    row = pl.BlockSpec((tm, d), lambda i: (i, 0))
    return pl.pallas_call(
        body, name=name, grid=(s // tm,),
        in_specs=[pl.BlockSpec((tm, k), lambda i: (i, 0)), _full((k, d)), row, _full((1, d)), row],
        out_specs=[row, _full((1, d))],
        out_shape=[jax.ShapeDtypeStruct((s, d), F32), jax.ShapeDtypeStruct((1, d), F32)],
        compiler_params=_cparams(("arbitrary",)),
    )(dz, w_t, x, g, res)


def _mm_tn(a, b, name):
    s, k = a.shape
    _, n = b.shape
    tk = _tile(k, 1408)
    tn = _tile(n, 1408)
    ts = min(2048, s)

    def body(a_ref, b_ref, o_ref):
        @pl.when(pl.program_id(2) == 0)
        def _():
            o_ref[...] = jnp.zeros_like(o_ref)

        o_ref[...] += lax.dot_general(a_ref[...], b_ref[...], (((0,), (0,)), ((), ())),
                                      preferred_element_type=F32)

    return pl.pallas_call(
        body, name=name, grid=(k // tk, n // tn, s // ts),
        in_specs=[pl.BlockSpec((ts, tk), lambda i, j, l: (l, i)), pl.BlockSpec((ts, tn), lambda i, j, l: (l, j))],
        out_specs=pl.BlockSpec((tk, tn), lambda i, j, l: (i, j)),
        out_shape=jax.ShapeDtypeStruct((k, n), F32),
        compiler_params=_cparams(("parallel", "parallel", "arbitrary")),
    )(a, b)


def _cur(tr, w, col=0):
    return pl.BlockSpec((tr, w), lambda i: (i, col))


def _prev(tr, h, w, col=0):
    return pl.BlockSpec((h, w), lambda i: (jnp.maximum(i * (tr // h) - 1, 0), col))


def _next(tr, h, w, nrows, col=0):
    last = nrows // h - 1
    return pl.BlockSpec((h, w), lambda i: (jnp.minimum((i + 1) * (tr // h), last), col))


def _full(shape):
    return pl.BlockSpec(shape, lambda i: tuple(0 for _ in shape))


def _silu_parts(g):
    sig = jax.nn.sigmoid(g)
    return sig, g * sig


FFN_CW = 256
FFN_NBUF = 3


def _conv3_taps(buf, w, off, rows):
    return (w[0:1] * buf[pl.ds(off, rows), :] + w[1:2] * buf[pl.ds(off + 1, rows), :]
            + w[2:3] * buf[pl.ds(off + 2, rows), :])


WHOLE_VMEM = pl.BlockSpec(memory_space=pltpu.VMEM)


def _ffn_fwd(x, g_pre, wu, conv_w, wd, g_post, name, target=None):
    s, d = x.shape
    f2 = wu.shape[1]
    f = f2 // 2
    tr = min(256, s)
    h = HALO16
    cw = FFN_CW
    head = target is not None

    def body(*refs):
        if head:
            (x_ref, gpre_ref, wu_ref, cw_ref, wd_ref, gpost_ref, t_ref, xo_ref, sq_ref, f_ref, h_ref, up_ref, u_ref,
             carry, gbuf, vbuf, facc) = refs
        else:
            (x_ref, gpre_ref, wu_ref, cw_ref, wd_ref, gpost_ref, xo_ref, f_ref, h_ref, up_ref, u_ref,
             carry, gbuf, vbuf, facc) = refs

        @pl.when(pl.program_id(0) == 0)
        def _():
            carry[...] = jnp.zeros_like(carry)
            if head:
                sq_ref[...] = jnp.zeros_like(sq_ref)

        xv = x_ref[...]
        r = lax.rsqrt(jnp.mean(xv * xv, axis=-1, keepdims=True) + RMS_EPS)
        hv = (xv * r * gpre_ref[...]).astype(BF16)
        h_ref[...] = hv
        nchunk = f // cw

        def up_proj(j):
            for buf, base in ((gbuf, 0), (vbuf, f)):
                cs = slice(base + j * cw, base + (j + 1) * cw)
                dst = buf.at[j % FFN_NBUF]
                upc = jnp.dot(hv, wu_ref[:, cs], preferred_element_type=F32)
                up_ref[:, cs] = upc.astype(BF16)
                dst[0:h, :] = carry[:, cs]
                dst[h:h + tr, :] = upc
                carry[:, cs] = upc[tr - h:tr, :]

        def down_proj(j, act):
            part = jnp.dot(act, wd_ref[j * cw:(j + 1) * cw, :], preferred_element_type=F32)
            if j == 0:
                facc[...] = part
            else:
                facc[...] += part

        up_proj(0)
        pending = None
        for j in range(nchunk):
            cg = slice(j * cw, (j + 1) * cw)
            cv = slice(f + j * cw, f + (j + 1) * cw)
            if j + 1 < nchunk:
                up_proj(j + 1)
            if pending is not None:
                down_proj(*pending)
            g = _conv3_taps(gbuf.at[j % FFN_NBUF], cw_ref[:, cg], h - 2, tr)
            v = _conv3_taps(vbuf.at[j % FFN_NBUF], cw_ref[:, cv], h - 2, tr)
            u_ref[:, cg] = g.astype(BF16)
            u_ref[:, cv] = v.astype(BF16)
            act = (g * jax.nn.sigmoid(g) * v).astype(BF16)
            pending = (j, act)
        down_proj(*pending)
        fv = facc[...]
        f_ref[...] = fv
        r2 = lax.rsqrt(jnp.mean(fv * fv, axis=-1, keepdims=True) + RMS_EPS)
        xo = xv + fv * r2 * gpost_ref[...]
        if head:
            err = xo - t_ref[...]
            xo_ref[...] = err * (1.0 / d)
            sq_ref[...] += jnp.sum(err * err, axis=0, keepdims=True)
        else:
            xo_ref[...] = xo

    row = _cur(tr, d)
    wide = _cur(tr, f2)
    vec = _full((1, d))
    out_specs = [row] + ([vec] if head else []) + [row, row, wide, wide]
    out_shape = ([jax.ShapeDtypeStruct((s, d), F32)] + ([jax.ShapeDtypeStruct((1, d), F32)] if head else [])
                 + [jax.ShapeDtypeStruct((s, d), F32), jax.ShapeDtypeStruct((s, d), BF16),
                    jax.ShapeDtypeStruct((s, f2), BF16), jax.ShapeDtypeStruct((s, f2), BF16)])
    return pl.pallas_call(
        body, name=name, grid=(s // tr,),
        in_specs=[row, vec, WHOLE_VMEM, _full((3, f2)), WHOLE_VMEM, vec] + ([row] if head else []),
        out_specs=out_specs, out_shape=out_shape,
        scratch_shapes=[pltpu.VMEM((h, f2), F32), pltpu.VMEM((FFN_NBUF, h + tr, cw), F32),
                        pltpu.VMEM((FFN_NBUF, h + tr, cw), F32), pltpu.VMEM((tr, d), F32)],
        compiler_params=_cparams(("arbitrary",)),
    )(*((x, g_pre, wu, conv_w, wd, g_post) + ((target,) if head else ())))


def _ffn_bwd(dxo, fout, x, up, u, g_pre, g_post, wd_t, wu_t, conv_w, name):
    s, d = x.shape
    f2 = up.shape[1]
    f = f2 // 2
    tr = min(256, s)
    nt = s // tr
    h = HALO16
    cw = FFN_CW

    def body(dy_ref, f_ref, x_ref, up_ref, u_ref, gpre_ref, gpost_ref, wdt_ref, wut_ref, cw_ref,
             dx_ref, dup_ref, act_ref, df_ref, dcw_ref, dgpost_ref, dgpre_ref, carry, dgbuf, dvbuf, dhacc):
        @pl.when(pl.program_id(0) == 0)
        def _():
            carry[...] = jnp.zeros_like(carry)
            dcw_ref[...] = jnp.zeros_like(dcw_ref)
            dgpost_ref[...] = jnp.zeros_like(dgpost_ref)
            dgpre_ref[...] = jnp.zeros_like(dgpre_ref)

        dy = dy_ref[...]
        fv = f_ref[...]
        r = lax.rsqrt(jnp.mean(fv * fv, axis=-1, keepdims=True) + RMS_EPS)
        nrm = fv * r
        dn = dy * gpost_ref[...]
        dfv = (r * (dn - nrm * jnp.mean(dn * nrm, axis=-1, keepdims=True))).astype(BF16)
        dgpost_ref[...] += jnp.sum(dy * nrm, axis=0, keepdims=True)
        df_ref[...] = dfv
        nchunk = f // cw

        def dh_part(dupb, cs, first):
            part = jnp.dot(dupb, wut_ref[cs, :], preferred_element_type=F32)
            if first:
                dhacc[...] = part
            else:
                dhacc[...] += part

        dact_next = jnp.dot(dfv, wdt_ref[:, 0:cw], preferred_element_type=F32)
        for j in range(nchunk):
            ch = slice(j * cw, (j + 1) * cw)
            cg = ch
            cv = slice(f + j * cw, f + (j + 1) * cw)
            dact = dact_next
            if j + 1 < nchunk:
                dact_next = jnp.dot(dfv, wdt_ref[:, (j + 1) * cw:(j + 2) * cw], preferred_element_type=F32)
            g = u_ref[:, cg].astype(F32)
            v = u_ref[:, cv].astype(F32)
            sig, sil = _silu_parts(g)
            act_ref[:, ch] = (sil * v).astype(BF16)
            du_g = dact * v * (sig * (1.0 + g * (1.0 - sig)))
            du_v = dact * sil
            for k, (dbuf, du, cs) in enumerate(((dgbuf.at[j % FFN_NBUF], du_g, cg), (dvbuf.at[j % FFN_NBUF], du_v, cv))):
                dbuf[0:tr, :] = du
                dbuf[tr:tr + h, :] = carry[:, cs]
                carry[:, cs] = du[0:h, :]
                w = cw_ref[:, cs]
                xin = up_ref[:, cs].astype(F32)
                acc = None
                for sh in range(3):
                    dsh = dbuf[pl.ds(sh, tr), :]
                    term = w[2 - sh:3 - sh] * dsh
                    acc = term if acc is None else acc + term
                    dcw_ref[2 - sh:3 - sh, cs] += jnp.sum(xin * dsh, axis=0, keepdims=True)
                dupb = acc.astype(BF16)
                dup_ref[:, cs] = dupb
                dh_part(dupb, cs, j == 0 and k == 0)
        dh = dhacc[...]
        xv = x_ref[...]
        r1 = lax.rsqrt(jnp.mean(xv * xv, axis=-1, keepdims=True) + RMS_EPS)
        n1 = xv * r1
        dn1 = dh * gpre_ref[...]
        dx_ref[...] = dy + r1 * (dn1 - n1 * jnp.mean(dn1 * n1, axis=-1, keepdims=True))
        dgpre_ref[...] += jnp.sum(dh * n1, axis=0, keepdims=True)

    def rev(w):
        return pl.BlockSpec((tr, w), lambda i: (nt - 1 - i, 0))

    vec = _full((1, d))
    return pl.pallas_call(
        body, name=name, grid=(nt,),
        in_specs=[rev(d), rev(d), rev(d), rev(f2), rev(f2), vec, vec, WHOLE_VMEM, WHOLE_VMEM, _full((3, f2))],
        out_specs=[rev(d), rev(f2), rev(f), rev(d), _full((3, f2)), vec, vec],
        out_shape=[jax.ShapeDtypeStruct((s, d), F32), jax.ShapeDtypeStruct((s, f2), BF16),
                   jax.ShapeDtypeStruct((s, f), BF16), jax.ShapeDtypeStruct((s, d), BF16),
                   jax.ShapeDtypeStruct((3, f2), F32), jax.ShapeDtypeStruct((1, d), F32),
                   jax.ShapeDtypeStruct((1, d), F32)],
        scratch_shapes=[pltpu.VMEM((h, f2), F32), pltpu.VMEM((FFN_NBUF, tr + h, cw), F32),
                        pltpu.VMEM((FFN_NBUF, tr + h, cw), F32), pltpu.VMEM((tr, d), F32)],
        compiler_params=pltpu.CompilerParams(dimension_semantics=("arbitrary",), vmem_limit_bytes=FFN_BWD_VMEM),
    )(dxo, fout, x, up, u, g_pre, g_post, wd_t, wu_t, conv_w)


def _od_gate_fwd(z, conv_w, name):
    s, d3 = z.shape
    d = d3 // 3
    tr = min(256, s)
    h = HALO16
    cw = FFN_CW

    def body(z_ref, prev_ref, w_ref, o_ref, buf):
        first = pl.program_id(0) == 0
        for j in range(d // cw):
            cb = slice(j * cw, (j + 1) * cw)
            cc = slice(d + j * cw, d + (j + 1) * cw)
            cu = slice(2 * d + j * cw, 2 * d + (j + 1) * cw)
            buf[0:h, :] = jnp.where(first, 0.0, prev_ref[:, cc].astype(F32) * prev_ref[:, cu].astype(F32))
            buf[h:h + tr, :] = z_ref[:, cc].astype(F32) * z_ref[:, cu].astype(F32)
            k = _conv3_taps(buf, w_ref[:, cb], h - 2, tr)
            o_ref[:, cb] = (z_ref[:, cb].astype(F32) * k).astype(BF16)

    return pl.pallas_call(
        body, name=name, grid=(s // tr,),
        in_specs=[_cur(tr, d3), _prev(tr, h, d3), _full((3, d))],
        out_specs=_cur(tr, d),
        out_shape=jax.ShapeDtypeStruct((s, d), BF16),
        scratch_shapes=[pltpu.VMEM((h + tr, cw), F32)],
        compiler_params=_cparams(("parallel",)),
    )(z, z, conv_w)


def _od_gate_bwd(dy, z, conv_w, name):
    s, d3 = z.shape
    d = d3 // 3
    tr = min(256, s)
    h = HALO16
    cw = FFN_CW
    ext = tr + h

    def body(dy_ref, dyn_ref, z_ref, zp_ref, zn_ref, w_ref, o_ref, dw_ref, buf, dbuf):
        i = pl.program_id(0)
        first = i == 0
        last = i == pl.num_programs(0) - 1

        @pl.when(first)
        def _():
            dw_ref[...] = jnp.zeros_like(dw_ref)

        for j in range(d // cw):
            cb = slice(j * cw, (j + 1) * cw)
            cc = slice(d + j * cw, d + (j + 1) * cw)
            cu = slice(2 * d + j * cw, 2 * d + (j + 1) * cw)
            w = w_ref[:, cb]
            cval = z_ref[:, cc].astype(F32)
            uval = z_ref[:, cu].astype(F32)
            buf[0:h, :] = jnp.where(first, 0.0, zp_ref[:, cc].astype(F32) * zp_ref[:, cu].astype(F32))
            buf[h:h + tr, :] = cval * uval
            k = _conv3_taps(buf, w, h - 2, tr)
            dyv = dy_ref[:, cb]
            o_ref[:, cb] = (dyv * k).astype(BF16)
            dbuf[0:tr, :] = dyv * z_ref[:, cb].astype(F32)
            dbuf[tr:ext, :] = jnp.where(last, 0.0, dyn_ref[:, cb] * zn_ref[:, cb].astype(F32))
            dcu = w[2:3] * dbuf[pl.ds(0, tr), :] + w[1:2] * dbuf[pl.ds(1, tr), :] + w[0:1] * dbuf[pl.ds(2, tr), :]
            o_ref[:, cc] = (dcu * uval).astype(BF16)
            o_ref[:, cu] = (dcu * cval).astype(BF16)
            dk = dbuf[pl.ds(0, tr), :]
            for t in range(3):
                dw_ref[t:t + 1, cb] += jnp.sum(dk * buf[pl.ds(h - 2 + t, tr), :], axis=0, keepdims=True)

    return pl.pallas_call(
        body, name=name, grid=(s // tr,),
        in_specs=[_cur(tr, d), _next(tr, h, d, s), _cur(tr, d3), _prev(tr, h, d3), _next(tr, h, d3, s), _full((3, d))],
        out_specs=[_cur(tr, d3), _full((3, d))],
        out_shape=[jax.ShapeDtypeStruct((s, d3), BF16), jax.ShapeDtypeStruct((3, d), F32)],
        scratch_shapes=[pltpu.VMEM((h + tr, cw), F32), pltpu.VMEM((ext, cw), F32)],
        compiler_params=_cparams(("arbitrary",)),
    )(dy, dy, z, z, z, conv_w)


Q0 = 2 * A_CH
K0 = Q0 + N_Q_HEADS * HEAD_DIM
V0 = K0 + 2 * HEAD_DIM
EVEN_IN = V0 + 2 * HEAD_DIM


def _rope_tables(positions):
    half = ROPE_DIM // 2
    inv_freq = ROPE_THETA ** (-(jnp.arange(half, dtype=F32) * 2.0 / ROPE_DIM))
    ang = positions.astype(F32)[:, None] * inv_freq
    cs = jnp.concatenate([jnp.cos(ang), jnp.sin(ang)], axis=1)
    spread = np.zeros((2 * half, 3 * LANES), np.float32)
    const = np.zeros((1, 3 * LANES), np.float32)
    for lane in range(3 * LANES):
        dim, part = lane % HEAD_DIM, lane // LANES
        if part == 0:
            if dim < ROPE_DIM:
                spread[dim % half, lane] = 1.0
            else:
                const[0, lane] = 1.0
        elif part == 1 and half <= dim < ROPE_DIM:
            spread[half + dim - half, lane] = 1.0
        elif part == 2 and dim < half:
            spread[half + dim, lane] = -1.0
    return jnp.dot(cs, jnp.asarray(spread), precision=lax.Precision.HIGHEST) + jnp.asarray(const)


def _rope_fwd(x, tab):
    c, sa, sb = tab[:, 0:LANES], tab[:, LANES:2 * LANES], tab[:, 2 * LANES:3 * LANES]
    return x * c + pltpu.roll(x, 8, 1) * sa + pltpu.roll(x, LANES - 8, 1) * sb


def _rope_bwd(dy, tab):
    c, sa, sb = tab[:, 0:LANES], tab[:, LANES:2 * LANES], tab[:, 2 * LANES:3 * LANES]
    return dy * c + pltpu.roll(dy * sa, LANES - 8, 1) + pltpu.roll(dy * sb, 8, 1)


def _ln_fwd(c, g, b):
    mu = jnp.mean(c, axis=-1, keepdims=True)
    xc = c - mu
    r = lax.rsqrt(jnp.mean(xc * xc, axis=-1, keepdims=True) + LN_EPS)
    nrm = xc * r
    return nrm, r, nrm * g + b


def _phase_fill(buf, ph, rows):
    for k in range(1, 8):
        ph[k - 1, 0:rows - 8, :] = buf[pl.ds(k, rows - 8), :]


def _phase_rows(buf, ph, off, n, cs):
    k = off % 8
    src = buf if k == 0 else ph.at[k - 1]
    return src[pl.ds(off - k, n), cs]


def _ev_mid_fwd(z, tab, conv_w, conv_b, ln_g, ln_b, name):
    s = z.shape[0]
    tr = min(256, s)
    h = HALO32
    cw = LANES

    def body(z_ref, zp_ref, tab_ref, w_ref, b_ref, g_ref, lb_ref, c_ref, a_ref, qkv_ref, gbuf, cbuf, gph):
        first = pl.program_id(0) == 0
        glu_p = zp_ref[:, 0:A_CH].astype(F32) * jax.nn.sigmoid(zp_ref[:, A_CH:2 * A_CH].astype(F32))
        gbuf[0:h, :] = jnp.where(first, 0.0, glu_p)
        gbuf[h:h + tr, :] = z_ref[:, 0:A_CH].astype(F32) * jax.nn.sigmoid(z_ref[:, A_CH:2 * A_CH].astype(F32))
        _phase_fill(gbuf, gph, h + tr)
        for j in range(A_CH // cw):
            cs = slice(j * cw, (j + 1) * cw)
            acc = jnp.broadcast_to(b_ref[:, cs], (tr, cw))
            for t in range(A_CONV):
                acc = acc + w_ref[t:t + 1, cs] * _phase_rows(gbuf, gph, h - (A_CONV - 1) + t, tr, cs)
            cbuf[:, cs] = acc
        c = cbuf[...]
        c_ref[...] = c.astype(BF16)
        _, _, l = _ln_fwd(c, g_ref[...], lb_ref[...])
        a_ref[...] = (l * jax.nn.sigmoid(l)).astype(BF16)
        tab_v = tab_ref[...]
        for p in range(4):
            xq = z_ref[:, Q0 + p * LANES:Q0 + (p + 1) * LANES].astype(F32)
            qkv_ref[:, p * LANES:(p + 1) * LANES] = _rope_fwd(xq, tab_v).astype(BF16)
        lane = lax.broadcasted_iota(jnp.int32, (tr, LANES), 1)
        lo = lane < HEAD_DIM
        kr = _rope_fwd(z_ref[:, K0:K0 + LANES].astype(F32), tab_v)
        vr = z_ref[:, V0:V0 + LANES].astype(F32)
        for base, val in ((4 * LANES, kr), (6 * LANES, vr)):
            sw = pltpu.roll(val, HEAD_DIM, 1)
            qkv_ref[:, base:base + LANES] = jnp.where(lo, val, sw).astype(BF16)
            qkv_ref[:, base + LANES:base + 2 * LANES] = jnp.where(lo, sw, val).astype(BF16)

    return pl.pallas_call(
        body, name=name, grid=(s // tr,),
        in_specs=[_cur(tr, EVEN_IN), _prev(tr, h, 2 * A_CH), _cur(tr, 3 * LANES), _full((A_CONV, A_CH)),
                  _full((1, A_CH)), _full((1, A_CH)), _full((1, A_CH))],
        out_specs=[_cur(tr, A_CH), _cur(tr, A_CH), _cur(tr, 2 * A_CH)],
        out_shape=[jax.ShapeDtypeStruct((s, A_CH), BF16), jax.ShapeDtypeStruct((s, A_CH), BF16),
                   jax.ShapeDtypeStruct((s, 2 * A_CH), BF16)],
        scratch_shapes=[pltpu.VMEM((h + tr, A_CH), F32), pltpu.VMEM((tr, A_CH), F32),
                        pltpu.VMEM((7, h + tr, A_CH), F32)],
        compiler_params=_cparams(("parallel",)),
    )(z, z, tab, conv_w, conv_b, ln_g, ln_b)


def _ev_mid_bwd(dcat, c, z, dq, dkv, tab, conv_w, ln_g, ln_b, name):
    s = z.shape[0]
    tr = min(256, s)
    h = HALO32
    cw = LANES
    ext = tr + h

    def body(da_ref, dan_ref, c_ref, cn_ref, z_ref, dq_ref, dkv_ref, tab_ref, w_ref, g_ref, lb_ref,
             dz_ref, dw_ref, dvec_ref, dcbuf, dcph):
        i = pl.program_id(0)
        first = i == 0
        last = i == pl.num_programs(0) - 1

        @pl.when(first)
        def _():
            dw_ref[...] = jnp.zeros_like(dw_ref)
            dvec_ref[...] = jnp.zeros_like(dvec_ref)

        gv = g_ref[...]

        def ln_silu_bwd(cv, dav):
            nrm, r, l = _ln_fwd(cv, gv, lb_ref[...])
            sig = jax.nn.sigmoid(l)
            dl = dav * (sig * (1.0 + l * (1.0 - sig)))
            dn = dl * gv
            dc = r * (dn - jnp.mean(dn, axis=-1, keepdims=True) - nrm * jnp.mean(dn * nrm, axis=-1, keepdims=True))
            return dc, dl, nrm

        dc, dl, nrm = ln_silu_bwd(c_ref[...].astype(F32), da_ref[...])
        dcn, _, _ = ln_silu_bwd(cn_ref[...].astype(F32), dan_ref[...])
        dcbuf[0:tr, :] = dc
        dcbuf[tr:ext, :] = jnp.where(last, 0.0, dcn)
        dvec_ref[0:1, :] += jnp.sum(dc, axis=0, keepdims=True)
        dvec_ref[1:2, :] += jnp.sum(dl * nrm, axis=0, keepdims=True)
        dvec_ref[2:3, :] += jnp.sum(dl, axis=0, keepdims=True)

        _phase_fill(dcbuf, dcph, ext)
        a_lin = z_ref[:, 0:A_CH].astype(F32)
        sig_g = jax.nn.sigmoid(z_ref[:, A_CH:2 * A_CH].astype(F32))
        glu = a_lin * sig_g
        for j in range(A_CH // cw):
            cs = slice(j * cw, (j + 1) * cw)
            gluj = glu[:, cs]
            acc = jnp.zeros((tr, cw), F32)
            for t in range(A_CONV):
                dsh = _phase_rows(dcbuf, dcph, A_CONV - 1 - t, tr, cs)
                acc = acc + w_ref[t:t + 1, cs] * dsh
                dw_ref[t:t + 1, cs] += jnp.sum(gluj * dsh, axis=0, keepdims=True)
            dz_ref[:, cs] = (acc * sig_g[:, cs]).astype(BF16)
            dz_ref[:, A_CH + j * cw:A_CH + (j + 1) * cw] = (
                acc * a_lin[:, cs] * sig_g[:, cs] * (1.0 - sig_g[:, cs])).astype(BF16)

        tab_v = tab_ref[...]
        for p in range(4):
            cs = slice(p * LANES, (p + 1) * LANES)
            dz_ref[:, Q0 + p * LANES:Q0 + (p + 1) * LANES] = _rope_bwd(dq_ref[:, cs], tab_v).astype(BF16)
        lane = lax.broadcasted_iota(jnp.int32, (tr, LANES), 1)
        lo = lane < HEAD_DIM

        def fold(base):
            p0 = dkv_ref[:, base:base + LANES]
            p1 = dkv_ref[:, base + LANES:base + 2 * LANES]
            s0 = p0 + pltpu.roll(p0, HEAD_DIM, 1)
            s1 = p1 + pltpu.roll(p1, HEAD_DIM, 1)
            return jnp.where(lo, s0, s1)

        dz_ref[:, K0:K0 + LANES] = _rope_bwd(fold(0), tab_v).astype(BF16)
        dz_ref[:, V0:V0 + LANES] = fold(2 * LANES).astype(BF16)

    return pl.pallas_call(
        body, name=name, grid=(s // tr,),
        in_specs=[_cur(tr, A_CH), _next(tr, h, A_CH, s), _cur(tr, A_CH), _next(tr, h, A_CH, s),
                  _cur(tr, EVEN_IN), _cur(tr, A_CH), _cur(tr, A_CH), _cur(tr, 3 * LANES),
                  _full((A_CONV, A_CH)), _full((1, A_CH)), _full((1, A_CH))],
        out_specs=[_cur(tr, EVEN_IN), _full((A_CONV, A_CH)), _full((8, A_CH))],
        out_shape=[jax.ShapeDtypeStruct((s, EVEN_IN), BF16), jax.ShapeDtypeStruct((A_CONV, A_CH), F32),
                   jax.ShapeDtypeStruct((8, A_CH), F32)],
        scratch_shapes=[pltpu.VMEM((ext, A_CH), F32), pltpu.VMEM((7, ext, A_CH), F32)],
        compiler_params=_cparams(("arbitrary",)),
    )(dcat, dcat, c, c, z, dq, dkv, tab, conv_w, ln_g, ln_b)


NT = (((1,), (1,)), ((), ()))
TN = (((0,), (0,)), ((), ()))
QB = WINDOW
SCALE = HEAD_DIM ** -0.5


def _att_scores(q2m, kwin):
    return lax.dot_general(q2m, kwin, NT, preferred_element_type=F32)


def _att_probs(raw, sink, mask):
    sc = jnp.where(mask, raw * SCALE, -jnp.inf)
    mx = jnp.maximum(jnp.max(sc, axis=-1, keepdims=True), sink)
    p = jnp.exp(sc - mx)
    ps = jnp.exp(sink - mx)
    inv = 1.0 / (jnp.sum(p, axis=-1, keepdims=True) + ps)
    return p * inv, ps * inv


def _att_mask(i):
    r = lax.broadcasted_iota(jnp.int32, (QB, 2 * QB), 0)
    kc = lax.broadcasted_iota(jnp.int32, (QB, 2 * QB), 1)
    diff = r + QB - kc
    return (diff >= 0) & (diff < WINDOW) & ((kc >= QB) | (i > 0))


def _half_masks(dtype):
    lane = lax.broadcasted_iota(jnp.int32, (1, LANES), 1)
    return (lane < HEAD_DIM).astype(dtype), (lane >= HEAD_DIM).astype(dtype)


def _att_fwd(qkv, a, sinks, name):
    s = qkv.shape[0]
    nb = s // QB

    def body(sink_ref, qkv_ref, kvp_ref, a_ref, o_ref):
        i = pl.program_id(0)
        mask = _att_mask(i)
        mlo, mhi = _half_masks(BF16)
        o_ref[:, 0:A_CH] = a_ref[...]

        def window(col):
            return jnp.concatenate([kvp_ref[:, col * LANES:(col + 1) * LANES],
                                    qkv_ref[:, A_CH + col * LANES:A_CH + (col + 1) * LANES]], axis=0)

        def raw_scores(p):
            q2 = qkv_ref[:, p * LANES:(p + 1) * LANES]
            kwin = window(p // 2)
            return _att_scores(q2 * mlo, kwin), _att_scores(q2 * mhi, kwin)

        nxt = raw_scores(0)
        for p in range(4):
            raw_e, raw_o = nxt
            if p + 1 < 4:
                nxt = raw_scores(p + 1)
            vwin = window(2 + p // 2)
            pe, _ = _att_probs(raw_e, sink_ref[2 * p], mask)
            po, _ = _att_probs(raw_o, sink_ref[2 * p + 1], mask)
            o = (jnp.dot(pe.astype(BF16), vwin * mlo, preferred_element_type=F32)
                 + jnp.dot(po.astype(BF16), vwin * mhi, preferred_element_type=F32))
            o_ref[:, A_CH + p * LANES:A_CH + (p + 1) * LANES] = o.astype(BF16)

    grid_spec = pltpu.PrefetchScalarGridSpec(
        num_scalar_prefetch=1, grid=(nb,),
        in_specs=[pl.BlockSpec((QB, 2 * A_CH), lambda i, sk: (i, 0)),
                  pl.BlockSpec((QB, A_CH), lambda i, sk: (jnp.maximum(i - 1, 0), 1)),
                  pl.BlockSpec((QB, A_CH), lambda i, sk: (i, 0))],
        out_specs=pl.BlockSpec((QB, 2 * A_CH), lambda i, sk: (i, 0)),
    )
    return pl.pallas_call(
        body, name=name, grid_spec=grid_spec,
        out_shape=jax.ShapeDtypeStruct((s, 2 * A_CH), BF16),
        compiler_params=_cparams(("parallel",)),
    )(sinks, qkv, qkv, a)


def _att_bwd(qkv, dcat, sinks, name):
    s = qkv.shape[0]
    nb = s // QB

    def body(sink_ref, qkv_ref, kvp_ref, do_ref, dq_ref, dkv_ref, ds_ref, carry):
        i = pl.program_id(0)

        @pl.when(i == 0)
        def _():
            ds_ref[...] = jnp.zeros_like(ds_ref)
            carry[...] = jnp.zeros_like(carry)

        @pl.when(i < nb)
        def _():
            mask = _att_mask(i)
            mlo, mhi = _half_masks(BF16)
            dwin = [jnp.zeros((2 * QB, LANES), F32) for _ in range(4)]

            def window(col):
                return jnp.concatenate([kvp_ref[:, col * LANES:(col + 1) * LANES],
                                        qkv_ref[:, A_CH + col * LANES:A_CH + (col + 1) * LANES]], axis=0)

            def first_products(n):
                p, hm = n // 2, (mlo, mhi)[n % 2]
                qm = qkv_ref[:, p * LANES:(p + 1) * LANES] * hm
                dom = do_ref[:, p * LANES:(p + 1) * LANES].astype(BF16) * hm
                kwin, vwin = window(p // 2), window(2 + p // 2)
                return (qm, dom, kwin * hm, _att_scores(qm, kwin),
                        lax.dot_general(dom, vwin, NT, preferred_element_type=F32))

            nxt = first_products(0)
            dq2 = None
            for n in range(N_Q_HEADS):
                g = n // 4
                qm, dom, kwm, raw, dp = nxt
                if n + 1 < N_Q_HEADS:
                    nxt = first_products(n + 1)
                prob, psink = _att_probs(raw, sink_ref[n], mask)
                delta = jnp.sum(prob * dp, axis=-1, keepdims=True)
                dsc = (prob * (dp - delta) * SCALE).astype(BF16)
                ds_ref[n:n + 1, :] += jnp.broadcast_to(jnp.sum(-psink * delta, axis=0, keepdims=True), (1, LANES))
                part = jnp.dot(dsc, kwm, preferred_element_type=F32)
                dq2 = part if n % 2 == 0 else dq2 + part
                dwin[g] = dwin[g] + lax.dot_general(dsc, qm, TN, preferred_element_type=F32)
                dwin[2 + g] = dwin[2 + g] + lax.dot_general(prob.astype(BF16), dom, TN, preferred_element_type=F32)
                if n % 2 == 1:
                    dq_ref[:, (n // 2) * LANES:(n // 2 + 1) * LANES] = dq2
            for n in range(4):
                cs = slice(n * LANES, (n + 1) * LANES)
                dkv_ref[:, cs] = carry[:, cs] + dwin[n][0:QB, :]
                carry[:, cs] = dwin[n][QB:2 * QB, :]

        @pl.when(i == nb)
        def _():
            dkv_ref[...] = carry[...]

    grid_spec = pltpu.PrefetchScalarGridSpec(
        num_scalar_prefetch=1, grid=(nb + 1,),
        in_specs=[pl.BlockSpec((QB, 2 * A_CH), lambda i, sk: (jnp.minimum(i, nb - 1), 0)),
                  pl.BlockSpec((QB, A_CH), lambda i, sk: (jnp.maximum(jnp.minimum(i, nb - 1) - 1, 0), 1)),
                  pl.BlockSpec((QB, A_CH), lambda i, sk: (jnp.minimum(i, nb - 1), 1))],
        out_specs=[pl.BlockSpec((QB, A_CH), lambda i, sk: (jnp.minimum(i, nb - 1), 0)),
                   pl.BlockSpec((QB, A_CH), lambda i, sk: (jnp.maximum(i - 1, 0), 0)),
                   pl.BlockSpec((8, LANES), lambda i, sk: (0, 0))],
        scratch_shapes=[pltpu.VMEM((QB, A_CH), F32)],
    )
    return pl.pallas_call(
        body, name=name, grid_spec=grid_spec,
        out_shape=[jax.ShapeDtypeStruct((s, A_CH), F32), jax.ShapeDtypeStruct((s, A_CH), F32),
                   jax.ShapeDtypeStruct((8, LANES), F32)],
        compiler_params=_cparams(("arbitrary",)),
    )(sinks, qkv, qkv, dcat)


def _local_step(x, positions, target, w, fetch=None, emit=None):
    row = lambda a, i: a[i:i + 1]
    tab = _rope_tables(positions)
    g = {}

    def ffn_fwd(xin, i, tgt=None):
        outs = _ffn_fwd(xin, row(w["ffn_norm_pre"], i), w["ffn_w_up", i], w["ffn_conv_w"][i],
                        w["ffn_w_down", i], row(w["ffn_norm_post"], i), f"ffn{i}_fwd", tgt)
        f, h, up, u = outs[-4:]
        return outs[:-4], (xin, f, h, up, u)

    def point(name, after):
        return emit(name, after, g) if emit is not None else 0.0

    def ffn_bwd(dxout, saved, i, tok):
        xin, f, h, up, u = saved
        dxin, dup, act, df, d_cw, dg_post, dg_pre = _ffn_bwd(
            dxout, f, xin, up, u, row(w["ffn_norm_pre"], i), row(w["ffn_norm_post"], i) + tok, w["ffn_w_down_t", i],
            w["ffn_w_up_t", i], w["ffn_conv_w"][i], f"ffn{i}_bwd")
        tok = point(f"ffn{i}_bwd_done", dxin)
        g["ffn_w_down", i] = _mm_tn(act, df, f"ffn{i}_down_dw")
        g["ffn_w_up", i] = _mm_tn(dup, h, f"ffn{i}_up_dw")
        return dxin, tok, dict(ffn_norm_post=dg_post, ffn_norm_pre=dg_pre, ffn_conv_w=d_cw)

    h0, z0 = _mm_pre(x, row(w["mix_norm_pre"], 0), w["ev_w_in"], "ev_in")
    c0, a0, qkv = _ev_mid_fwd(z0, tab, w["ev_a_conv_w"], w["ev_a_conv_b"], w["ev_a_ln_g"], w["ev_a_ln_b"], "ev_mid")
    cat = _att_fwd(qkv, a0, w["ev_sinks"], "ev_att")
    m0, x1 = _mm_post(cat, w["ev_w_out"], row(w["mix_norm_post"], 0), x, "ev_out")
    if fetch is not None:
        w = {**w, **fetch("ffn0", x1)}
    (x2,), ffn0 = ffn_fwd(x1, 0)
    if fetch is not None:
        w = {**w, **fetch("layer1", x2)}
    h2, z1 = _mm_pre(x2, row(w["mix_norm_pre"], 1), w["od_w_in"], "od_in")
    y1 = _od_gate_fwd(z1, w["od_conv_w"], "od_mid")
    m1, x3 = _mm_post(y1, w["od_w_out"], row(w["mix_norm_post"], 1), x2, "od_out")
    (dx4, sq), ffn1 = ffn_fwd(x3, 1, target)

    dx3, _, gf1 = ffn_bwd(dx4, ffn1, 1, 0.0)
    dm1, dy1, dg_mo1 = _mm_post_bwd(dx3, m1, row(w["mix_norm_post"], 1), w["od_w_out_t"], "od_out_bwd")
    g["od_w_out"] = _mm_tn(y1, dm1, "od_out_dw")
    dz1, g["od_conv_w"] = _od_gate_bwd(dy1, z1, w["od_conv_w"], "od_mid_bwd")
    g["od_w_in"] = _mm_tn(dz1, h2, "od_in_dw")
    tok = point("layer1_grads", dz1)
    dx2, dg_mp1 = _mm_pre_bwd(dz1, w["od_w_in_t"], x2, row(w["mix_norm_pre"], 1) + tok, dx3, "od_in_bwd")

    dx1, tok, gf0 = ffn_bwd(dx2, ffn0, 0, 0.0)
    tok = tok + point("ffn0_grads", dx1)
    dm0, dcat, dg_mo0 = _mm_post_bwd(dx1, m0, row(w["mix_norm_post"], 0) + tok, w["ev_w_out_t"], "ev_out_bwd")
    tok = point("ev_out_bwd_done", dcat)
    g["ev_w_out"] = _mm_tn(cat, dm0, "ev_out_dw")
    dq, dkv, dsk = _att_bwd(qkv, dcat, w["ev_sinks"] + tok, "ev_att_bwd")
    tok = point("ev_att_bwd_done", dq)
    dz0, g["ev_a_conv_w"], dvec = _ev_mid_bwd(dcat, c0, z0, dq, dkv, tab, w["ev_a_conv_w"], w["ev_a_ln_g"] + tok,
                                              w["ev_a_ln_b"], "ev_mid_bwd")
    point("ev_mid_bwd_done", dz0)
    g["ev_w_in"] = _mm_tn(dz0, h0, "ev_in_dw")
    dx0, dg_mp0 = _mm_pre_bwd(dz0, w["ev_w_in_t"], x, row(w["mix_norm_pre"], 0), dx1, "ev_in_bwd")

    g["ev_a_conv_b"] = dvec[0:1]
    g["ev_a_ln_g"] = dvec[1:2]
    g["ev_a_ln_b"] = dvec[2:3]
    g["ev_sinks"] = dsk[:, 0]
    g["mix_norm_pre"] = jnp.concatenate([dg_mp0, dg_mp1], axis=0)
    g["mix_norm_post"] = jnp.concatenate([dg_mo0, dg_mo1], axis=0)
    g["ffn_norm_pre"] = jnp.concatenate([gf0["ffn_norm_pre"], gf1["ffn_norm_pre"]], axis=0)
    g["ffn_norm_post"] = jnp.concatenate([gf0["ffn_norm_post"], gf1["ffn_norm_post"]], axis=0)
    g["ffn_conv_w"] = jnp.stack([gf0["ffn_conv_w"], gf1["ffn_conv_w"]], axis=0)
    return sq, dx0, g


ANY = pl.BlockSpec(memory_space=pl.ANY)
PACK_COLS = 1024


def _me():
    return lax.axis_index("x"), lax.axis_index("y"), lax.axis_index("c")


def _other_chips(x, y):
    return [(1 - x, y), (x, 1 - y), (1 - x, 1 - y)]


def _remote(src, dst, send, recv, dev):
    return pltpu.make_async_remote_copy(src_ref=src, dst_ref=dst, send_sem=send, recv_sem=recv,
                                        device_id=dev, device_id_type=MESH)


def _gather_chips(wp, name):
    r, cols = wp.shape
    rh = r // 2

    def body(w_ref, o_ref, send, recv):
        x, y, c = _me()
        p = 2 * x + y
        sib = (x, y, 1 - c)
        chips = _other_chips(x, y)
        half = pl.ds(c * rh, rh)
        other = pl.ds((1 - c) * rh, rh)
        sent = [_remote(w_ref.at[half], o_ref.at[p, half], send.at[k], recv.at[k], (cx, cy, c))
                for k, (cx, cy) in enumerate(chips)]
        for cp in sent:
            cp.start()
        for k, (cx, cy) in enumerate(chips):
            q = 2 * cx + cy
            _remote(w_ref.at[half], o_ref.at[q, half], send.at[k], recv.at[k], (cx, cy, c)).wait_recv()
            fwd = _remote(o_ref.at[q, half], o_ref.at[q, half], send.at[3 + k], recv.at[3 + k], sib)
            fwd.start()
            sent.append(fwd)
        for k, (cx, cy) in enumerate(chips):
            q = 2 * cx + cy
            _remote(o_ref.at[q, other], o_ref.at[q, other], send.at[3 + k], recv.at[3 + k], sib).wait_recv()
        for cp in sent:
            cp.wait_send()

    return pl.pallas_call(
        body, name=name, in_specs=[ANY], out_specs=ANY,
        out_shape=jax.ShapeDtypeStruct((N_CHIPS, r, cols), wp.dtype),
        scratch_shapes=[pltpu.SemaphoreType.DMA((6,)), pltpu.SemaphoreType.DMA((6,))],
    )(wp)


HBM_SPEC = pl.BlockSpec(memory_space=pltpu.HBM)
SEM_SPEC = pl.BlockSpec(memory_space=pltpu.SEMAPHORE)
DATAFLOW = pltpu.SideEffectType.DATAFLOW_SIDE_EFFECTING


def _gather_plan(w_ref, land_ref):
    x, y, c = _me()
    return [(w_ref, land_ref.at[2 * x + y], (cx, cy, c)) for cx, cy in _other_chips(x, y)]


def _copies_start(src, land_shape, plan, n, name):
    def body(src_ref, land_ref, send, recv, src_thru, land_thru, token):
        for k, (s_view, d_view, dev) in enumerate(plan(src_ref, land_ref)):
            _remote(s_view, d_view, send.at[k], recv.at[k], dev).start()
        token[...] = jnp.zeros_like(token)

    return pl.pallas_call(
        body, name=name,
        out_shape=(pltpu.SemaphoreType.DMA((n,)), pltpu.SemaphoreType.DMA((n,)), pltpu.HBM(src.shape, src.dtype),
                   pltpu.HBM(land_shape, src.dtype), jax.ShapeDtypeStruct((8, LANES), F32)),
        in_specs=(HBM_SPEC, HBM_SPEC),
        out_specs=(SEM_SPEC, SEM_SPEC, HBM_SPEC, HBM_SPEC, pl.BlockSpec(memory_space=pltpu.VMEM)),
        input_output_aliases={0: 2, 1: 3},
        compiler_params=pltpu.CompilerParams(has_side_effects=DATAFLOW),
    )(pltpu.with_memory_space_constraint(src, pltpu.HBM),
      pltpu.with_memory_space_constraint(lax.empty(land_shape, src.dtype), pltpu.HBM))


def _copies_wait(started, after, plan, name):
    send, recv, src_thru, land_thru, _ = started

    def body(src_ref, land_ref, send, recv, after_ref, src_dead, land_out):
        for k, (s_view, d_view, dev) in enumerate(plan(src_ref, land_ref)):
            cp = _remote(s_view, d_view, send.at[k], recv.at[k], dev)
            cp.wait_send()
            cp.wait_recv()

    return pl.pallas_call(
        body, name=name,
        out_shape=(pltpu.HBM(src_thru.shape, src_thru.dtype), pltpu.HBM(land_thru.shape, land_thru.dtype)),
        in_specs=(HBM_SPEC, HBM_SPEC, SEM_SPEC, SEM_SPEC, ANY),
        out_specs=(HBM_SPEC, HBM_SPEC),
        input_output_aliases={0: 0, 1: 1},
        compiler_params=pltpu.CompilerParams(has_side_effects=DATAFLOW),
    )(src_thru, land_thru, send, recv, after)


def _swap_plan(g_ref, land_ref):
    x, y, c = _me()
    return [(g_ref.at[q, 1 - c], land_ref.at[q], (x, y, 1 - c)) for q in range(N_CHIPS)]


def _ici_plan(a_ref, land_ref):
    x, y, c = _me()
    return [(a_ref.at[2 * cx + cy], land_ref.at[2 * x + y], (cx, cy, c)) for cx, cy in _other_chips(x, y)]


def _share_plan(h_ref, land_ref):
    x, y, c = _me()
    return [(h_ref, land_ref, (x, y, 1 - c))]


def _exchange8(v, reduce, name):
    r, cols = v.shape
    rel = [(a, b, d) for a in (0, 1) for b in (0, 1) for d in (0, 1) if (a, b, d) != (0, 0, 0)]

    def body(v_ref, o_ref, *rest):
        if reduce:
            gbuf, send, recv = rest
        else:
            gbuf = o_ref
            send, recv = rest
        x, y, c = _me()
        me = 4 * x + 2 * y + c
        gbuf[me] = v_ref[...]
        sent = []
        for k, (a, b, d) in enumerate(rel):
            cp = _remote(v_ref, gbuf.at[me], send.at[k], recv.at[k], ((x + a) % 2, (y + b) % 2, (c + d) % 2))
            cp.start()
            sent.append(cp)
        for k, (a, b, d) in enumerate(rel):
            src = 4 * ((x + a) % 2) + 2 * ((y + b) % 2) + (c + d) % 2
            _remote(v_ref, gbuf.at[src], send.at[k], recv.at[k], (x, y, c)).wait_recv()
        for cp in sent:
            cp.wait_send()
        if reduce:
            acc = gbuf[0]
            for n in range(1, 8):
                acc = acc + gbuf[n]
            o_ref[...] = acc

    vmem = pl.BlockSpec(memory_space=pltpu.VMEM)
    sems = [pltpu.SemaphoreType.DMA((7,)), pltpu.SemaphoreType.DMA((7,))]
    if reduce:
        out_shape = jax.ShapeDtypeStruct((r, cols), F32)
        scratch = [pltpu.VMEM((8, r, cols), F32)] + sems
    else:
        out_shape = jax.ShapeDtypeStruct((8, r, cols), F32)
        scratch = sems
    return pl.pallas_call(body, name=name, in_specs=[vmem], out_specs=vmem, out_shape=out_shape,
                          scratch_shapes=scratch)(v)


def _rs_swap(g, name):
    _, _, rh, cols = g.shape

    def body(g_ref, o_ref, send, recv):
        x, y, c = _me()
        cps = [_remote(g_ref.at[q, 1 - c], o_ref.at[q], send.at[q], recv.at[q], (x, y, 1 - c)) for q in range(N_CHIPS)]
        for cp in cps:
            cp.start()
        for cp in cps:
            cp.wait()

    return pl.pallas_call(
        body, name=name, in_specs=[ANY], out_specs=ANY,
        out_shape=jax.ShapeDtypeStruct((N_CHIPS, rh, cols), F32),
        scratch_shapes=[pltpu.SemaphoreType.DMA((N_CHIPS,)), pltpu.SemaphoreType.DMA((N_CHIPS,))],
    )(g)


def _row_tile(rows, pref, mult=8):
    if rows <= pref:
        return rows
    t = (pref // mult) * mult
    while t >= mult:
        if rows % t == 0:
            return t
        t -= mult
    return rows


def _rs_add(g, sib, c, name):
    _, _, rh, cols = g.shape
    tr = _row_tile(rh, 512, 16)

    def body(c_ref, g_ref, s_ref, o_ref):
        o_ref[...] = (g_ref[...] + s_ref[...]).astype(BF16)

    grid_spec = pltpu.PrefetchScalarGridSpec(
        num_scalar_prefetch=1, grid=(N_CHIPS, rh // tr),
        in_specs=[pl.BlockSpec((None, None, tr, cols), lambda q, i, cr: (q, cr[0], i, 0)),
                  pl.BlockSpec((None, tr, cols), lambda q, i, cr: (q, i, 0))],
        out_specs=pl.BlockSpec((None, tr, cols), lambda q, i, cr: (q, i, 0)),
    )
    return pl.pallas_call(
        body, name=name, grid_spec=grid_spec,
        out_shape=jax.ShapeDtypeStruct((N_CHIPS, rh, cols), BF16),
        compiler_params=_cparams(("parallel", "parallel")),
    )(c, g, sib)


def _rs_ici(a, name):
    _, rh, cols = a.shape

    def body(a_ref, o_ref, send, recv):
        x, y, c = _me()
        p = 2 * x + y
        cps = []
        for k, (cx, cy) in enumerate(_other_chips(x, y)):
            cp = _remote(a_ref.at[2 * cx + cy], o_ref.at[p], send.at[k], recv.at[k], (cx, cy, c))
            cp.start()
            cps.append(cp)
        for k, (cx, cy) in enumerate(_other_chips(x, y)):
            q = 2 * cx + cy
            _remote(a_ref.at[q], o_ref.at[q], send.at[k], recv.at[k], (cx, cy, c)).wait_recv()
        for cp in cps:
            cp.wait_send()

    return pl.pallas_call(
        body, name=name, in_specs=[ANY], out_specs=ANY,
        out_shape=jax.ShapeDtypeStruct((N_CHIPS, rh, cols), a.dtype),
        scratch_shapes=[pltpu.SemaphoreType.DMA((3,)), pltpu.SemaphoreType.DMA((3,))],
    )(a)


def _rs_sum(rb, a, chip, name):
    _, rh, cols = rb.shape
    tr = _row_tile(rh, 512, 16)

    def body(p_ref, r0, r1, r2, r3, own, o_ref):
        p = p_ref[0]
        ownv = own[...].astype(F32)
        acc = None
        for q, r in enumerate((r0, r1, r2, r3)):
            v = jnp.where(p == q, ownv, r[...].astype(F32))
            acc = v if acc is None else acc + v
        o_ref[...] = acc

    def spec(q):
        return pl.BlockSpec((None, tr, cols), lambda i, pr: (jnp.where(pr[0] == q, (q + 1) % N_CHIPS, q), i, 0))

    grid_spec = pltpu.PrefetchScalarGridSpec(
        num_scalar_prefetch=1, grid=(rh // tr,),
        in_specs=[spec(0), spec(1), spec(2), spec(3), pl.BlockSpec((None, tr, cols), lambda i, pr: (pr[0], i, 0))],
        out_specs=pl.BlockSpec((tr, cols), lambda i, pr: (i, 0)),
    )
    return pl.pallas_call(
        body, name=name, grid_spec=grid_spec,
        out_shape=jax.ShapeDtypeStruct((rh, cols), F32),
        compiler_params=_cparams(("parallel",)),
    )(chip, rb, rb, rb, rb, a)


def _rs_share(hsum, name):
    rh, cols = hsum.shape

    def body(h_ref, o_ref, send, recv):
        x, y, c = _me()
        cp = _remote(h_ref, o_ref, send, recv, (x, y, 1 - c))
        cp.start()
        cp.wait()

    return pl.pallas_call(
        body, name=name, in_specs=[ANY], out_specs=ANY,
        out_shape=jax.ShapeDtypeStruct((rh, cols), F32),
        scratch_shapes=[pltpu.SemaphoreType.DMA, pltpu.SemaphoreType.DMA],
    )(hsum)


def _adamw(w, g, m, v, name):
    rows, cols = w.shape
    tr = _row_tile(rows, 512)

    def body(w_ref, g_ref, m_ref, v_ref, d_ref, nm_ref, nv_ref):
        gv = g_ref[...]
        nm = ADAM_B1 * m_ref[...] + (1.0 - ADAM_B1) * gv
        nv = ADAM_B2 * v_ref[...] + (1.0 - ADAM_B2) * (gv * gv)
        m_hat = nm / (1.0 - ADAM_B1 ** ADAM_STEP)
        v_hat = nv / (1.0 - ADAM_B2 ** ADAM_STEP)
        d_ref[...] = -ADAM_LR * (m_hat / (jnp.sqrt(v_hat) + ADAM_EPS) + ADAM_WD * w_ref[...])
        nm_ref[...] = nm
        nv_ref[...] = nv

    spec = pl.BlockSpec((tr, cols), lambda i: (i, 0))
    shp = jax.ShapeDtypeStruct((rows, cols), F32)
    return pl.pallas_call(
        body, name=name, grid=(rows // tr,), in_specs=[spec] * 4, out_specs=[spec] * 3, out_shape=[shp] * 3,
        compiler_params=_cparams(("parallel",)),
    )(w, g, m, v)


WEIGHTS = ("mix_norm_pre", "mix_norm_post", "ffn_norm_pre", "ffn_norm_post", "ev_w_in", "ev_a_conv_w", "ev_a_conv_b",
           "ev_a_ln_g", "ev_a_ln_b", "ev_sinks", "ev_w_out", "od_w_in", "od_conv_w", "od_w_out", "ffn_w_up",
           "ffn_conv_w", "ffn_w_down")
MATS = (("ev_w_in", 2), ("ev_w_out", 1), ("od_w_in", 2), ("od_w_out", 1), ("ffn_w_up", 2), ("ffn_w_down", 1))
UNITS = (("ev_w_in", 0, 2), ("ev_w_out", 0, 1), ("ffn_w_up", 0, 2), ("ffn_w_down", 0, 1),
         ("od_w_in", 0, 2), ("od_w_out", 0, 1), ("ffn_w_up", 1, 2), ("ffn_w_down", 1, 1))
GATHER_GROUPS = ((0, 1), (2, 3), (4, 5, 6, 7))
REDUCE_GROUPS = {"layer1": (4, 5, 6, 7), "ffn0": (2, 3), "ev": (0, 1)}
SMALL_SHARDED = ("ev_a_conv_w", "od_conv_w", "ffn_conv_w")
REPLICATED = ("mix_norm_pre", "mix_norm_post", "ffn_norm_pre", "ffn_norm_post", "ev_a_conv_b", "ev_a_ln_g",
              "ev_a_ln_b", "ev_sinks")


def _pack(parts, rows_multiple):
    flat = jnp.concatenate([p.reshape(-1) for p in parts])
    unit = rows_multiple * PACK_COLS
    pad = (-flat.shape[0]) % unit
    if pad:
        flat = jnp.concatenate([flat, jnp.zeros((pad,), flat.dtype)])
    return flat.reshape(-1, PACK_COLS)


def _unpack(buf, shapes):
    flat = buf.reshape(-1)
    out, off = [], 0
    for shp in shapes:
        n = 1
        for d in shp:
            n *= d
        out.append(flat[off:off + n].reshape(shp))
        off += n
    return out


def _shard_rows(shard, axis):
    if axis == 2:
        shard = jnp.swapaxes(shard, 1, 2)
    return shard.reshape(-1, PACK_COLS)


def kernel(x, positions, mix_norm_pre, mix_norm_post, ffn_norm_pre, ffn_norm_post, ev_w_in, ev_a_conv_w, ev_a_conv_b, ev_a_ln_g, ev_a_ln_b, ev_sinks, ev_w_out, od_w_in, od_conv_w, od_w_out, ffn_w_up, ffn_conv_w, ffn_w_down, loss_target, m_mix_norm_pre, m_mix_norm_post, m_ffn_norm_pre, m_ffn_norm_post, m_ev_w_in, m_ev_a_conv_w, m_ev_a_conv_b, m_ev_a_ln_g, m_ev_a_ln_b, m_ev_sinks, m_ev_w_out, m_od_w_in, m_od_conv_w, m_od_w_out, m_ffn_w_up, m_ffn_conv_w, m_ffn_w_down, v_mix_norm_pre, v_mix_norm_post, v_ffn_norm_pre, v_ffn_norm_post, v_ev_w_in, v_ev_a_conv_w, v_ev_a_conv_b, v_ev_a_ln_g, v_ev_a_ln_b, v_ev_sinks, v_ev_w_out, v_od_w_in, v_od_conv_w, v_od_w_out, v_ffn_w_up, v_ffn_conv_w, v_ffn_w_down):
    wts = dict(zip(WEIGHTS, (mix_norm_pre, mix_norm_post, ffn_norm_pre, ffn_norm_post, ev_w_in, ev_a_conv_w, ev_a_conv_b,
                             ev_a_ln_g, ev_a_ln_b, ev_sinks, ev_w_out, od_w_in, od_conv_w, od_w_out, ffn_w_up, ffn_conv_w,
                             ffn_w_down)))
    mom = dict(zip(WEIGHTS, (m_mix_norm_pre, m_mix_norm_post, m_ffn_norm_pre, m_ffn_norm_post, m_ev_w_in, m_ev_a_conv_w,
                             m_ev_a_conv_b, m_ev_a_ln_g, m_ev_a_ln_b, m_ev_sinks, m_ev_w_out, m_od_w_in, m_od_conv_w,
                             m_od_w_out, m_ffn_w_up, m_ffn_conv_w, m_ffn_w_down)))
    var = dict(zip(WEIGHTS, (v_mix_norm_pre, v_mix_norm_post, v_ffn_norm_pre, v_ffn_norm_post, v_ev_w_in, v_ev_a_conv_w,
                             v_ev_a_conv_b, v_ev_a_ln_g, v_ev_a_ln_b, v_ev_sinks, v_ev_w_out, v_od_w_in, v_od_conv_w,
                             v_od_w_out, v_ffn_w_up, v_ffn_conv_w, v_ffn_w_down)))
    xi, yi, ci = _me()
    chip = 2 * xi + yi

    unit_rows = [_shard_rows(wts[k][l:l + 1].astype(BF16), axis) for k, l, axis in UNITS]

    def group_block(group):
        return jnp.concatenate([unit_rows[u] for u in group], axis=0)

    def unpack_group(group, landed, own):
        full = lax.dynamic_update_slice(landed, own[None], (chip, 0, 0))
        out, off = {}, 0
        for u in group:
            k, l, axis = UNITS[u]
            n = unit_rows[u].shape[0]
            native = full[:, off:off + n].reshape(N_CHIPS * n, PACK_COLS)
            off += n
            key = (lambda name: (name, l)) if k.startswith("ffn") else (lambda name: name)
            out[key(k + "_t" if axis == 2 else k)] = native
            out[key(k if axis == 2 else k + "_t")] = native.T
        return out

    small_shapes = [wts[k].shape for k in SMALL_SHARDED]
    small_all = _exchange8(_pack([wts[k] for k in SMALL_SHARDED], 8), False, "gather_small")
    blocks = [group_block(grp) for grp in GATHER_GROUPS]
    first = _gather_chips(blocks[0], "gather_mats")
    later = {}
    for stage, grp, blk in zip(("ffn0", "layer1"), GATHER_GROUPS[1:], blocks[1:]):
        later[stage] = (grp, blk, _copies_start(blk, (N_CHIPS,) + blk.shape, _gather_plan, 3, "gather_" + stage + "_start"))

    def fetch(stage, after):
        grp, blk, started = later[stage]
        own, landed = _copies_wait(started, after, _gather_plan, "gather_" + stage + "_wait")
        return unpack_group(grp, landed, own)

    w = {k: wts[k] for k in REPLICATED}
    w.update(unpack_group(GATHER_GROUPS[0], first, blocks[0]))
    per_chip = [_unpack(small_all[2 * q], small_shapes) for q in range(N_CHIPS)]
    for n, k in enumerate(SMALL_SHARDED):
        w[k] = jnp.concatenate([per_chip[q][n] for q in range(N_CHIPS)], axis=-1)
    for k in ("ev_a_conv_w", "od_conv_w"):
        w[k] = w[k][0]
    w["ev_sinks"] = w["ev_sinks"][0]
    w["mix_norm_pre"] = w["mix_norm_pre"] + sum(later[s][2][4][0, 0] for s in later)

    core = jnp.reshape(ci, (1,)).astype(jnp.int32)
    chip_arr = jnp.reshape(chip, (1,)).astype(jnp.int32)
    per_layer = {}

    def group_grads(group, g):
        gp = jnp.concatenate([(g[k, l] if k.startswith("ffn") else g[k]).reshape(N_CHIPS, -1, PACK_COLS)
                              for k, l, _ in (UNITS[u] for u in group)], axis=1)
        return gp.reshape(N_CHIPS, 2, gp.shape[1] // 2, PACK_COLS)

    def finish(group, half, other):
        red = jnp.concatenate([jnp.where(ci == 0, half, other), jnp.where(ci == 0, other, half)], axis=0)
        off = 0
        for u in group:
            k, l, axis = UNITS[u]
            n = unit_rows[u].shape[0]
            part = red[off:off + n]
            off += n
            per_layer[k, l] = part.T if axis == 2 else part

    chains = {}

    def chain_step(tag, after, g):
        group = REDUCE_GROUPS[tag]
        st = chains.setdefault(tag, {"step": 0})
        step = st["step"]
        st["step"] = step + 1
        if step == 0:
            gp = group_grads(group, g)
            rh = gp.shape[2]
            st["swap"] = _copies_start(gp, (N_CHIPS, rh, PACK_COLS), _swap_plan, N_CHIPS, f"rs_{tag}_swap_start")
            return st["swap"][4][0, 0]
        if step == 1:
            gp, sib = _copies_wait(st["swap"], after, _swap_plan, f"rs_{tag}_swap_wait")
            pair = _rs_add(gp, sib, core, f"rs_{tag}_add")
            st["ici"] = _copies_start(pair, pair.shape, _ici_plan, 3, f"rs_{tag}_ici_start")
            return st["ici"][4][0, 0]
        if step == 2:
            pair, landed = _copies_wait(st["ici"], after, _ici_plan, f"rs_{tag}_ici_wait")
            half = _rs_sum(landed, pair, chip_arr, f"rs_{tag}_sum")
            st["share"] = _copies_start(half, half.shape, _share_plan, 1, f"rs_{tag}_share_start")
            return st["share"][4][0, 0]
        half, other = _copies_wait(st["share"], after, _share_plan, f"rs_{tag}_share_wait")
        finish(group, half, other)
        return 0.0

    schedule = {"layer1_grads": ("layer1",), "ffn0_bwd_done": ("layer1",), "ffn0_grads": ("ffn0",),
                "ev_out_bwd_done": ("layer1", "ffn0"), "ev_att_bwd_done": ("layer1", "ffn0"),
                "ev_mid_bwd_done": ("ffn0",)}

    def emit(place, after, g):
        return sum(chain_step(tag, after, g) for tag in schedule.get(place, ()))

    sq, dx, g = _local_step(x[0], positions[0], loss_target[0], w, fetch, emit)
    loss = lax.psum(0.5 * jnp.sum(sq) / D_MODEL, ("x", "y", "c"))

    gp = group_grads(REDUCE_GROUPS["ev"], g)
    sib = _rs_swap(gp, "rs_swap")
    pair = _rs_add(gp, sib, core, "rs_add")
    landed = _rs_ici(pair, "rs_ici")
    half = _rs_sum(landed, pair, chip_arr, "rs_sum")
    finish(REDUCE_GROUPS["ev"], half, _rs_share(half, "rs_share"))
    grads = {k: jnp.stack([per_layer[k, l] for l in range(wts[k].shape[0])], axis=0) for k, _ in MATS}

    small_keys = REPLICATED + SMALL_SHARDED
    full_shapes = [wts[k].shape for k in REPLICATED] + [wts[k].shape[:-1] + (wts[k].shape[-1] * N_CHIPS,) for k in SMALL_SHARDED]
    sm = _exchange8(_pack([g[k] for k in small_keys], 8), True, "reduce_small")
    for k, full in zip(small_keys, _unpack(sm, full_shapes)):
        if k in SMALL_SHARDED:
            n = wts[k].shape[-1]
            full = lax.dynamic_slice_in_dim(full, chip * n, n, axis=full.ndim - 1)
        grads[k] = full

    deltas, new_m, new_v = {}, {}, {}
    for k in WEIGHTS:
        shp = wts[k].shape
        two_d = (-1, shp[-1])
        d, nm, nv = _adamw(wts[k].reshape(two_d), grads[k].reshape(two_d), mom[k].reshape(two_d), var[k].reshape(two_d),
                           "adamw_" + k)
        deltas[k], new_m[k], new_v[k] = d.reshape(shp), nm.reshape(shp), nv.reshape(shp)

    return (loss, dx[None], *[grads[k] for k in WEIGHTS], *[deltas[k] for k in WEIGHTS],
            *[new_m[k] for k in WEIGHTS], *[new_v[k] for k in WEIGHTS])
```

```python
import functools

import jax
import jax.numpy as jnp
import numpy as np
from jax import lax
from jax.experimental import pallas as pl
from jax.experimental.pallas import tpu as pltpu

F32 = jnp.float32
BF16 = jnp.bfloat16
MESH = pl.DeviceIdType.MESH

D_MODEL = 1024
HEAD_DIM = 64
A_CH = 512
A_CONV = 31
N_Q_HEADS = 8
WINDOW = 128
ROPE_THETA = 500000.0
ROPE_DIM = 16
D_FF = 2816
RMS_EPS = 1e-6
LN_EPS = 1e-5
ADAM_LR = 0.001
ADAM_B1 = 0.9
ADAM_B2 = 0.999
ADAM_EPS = 1e-08
ADAM_WD = 0.01
ADAM_STEP = 10

LANES = 128
HALO16 = 16
HALO32 = 32
VMEM_LIMIT = 56 * 1024 * 1024
FFN_BWD_VMEM = 60 * 1024 * 1024
N_CHIPS = 4


def _cparams(sem):
    return pltpu.CompilerParams(dimension_semantics=sem, vmem_limit_bytes=VMEM_LIMIT)


def _tile(n, pref):
    if n <= pref:
        return n
    t = (pref // LANES) * LANES
    while t >= LANES:
        if n % t == 0:
            return t
        t -= LANES
    return n


MM_ROWS = 512


def _rms_scale(v):
    return lax.rsqrt(jnp.mean(v * v, axis=-1, keepdims=True) + RMS_EPS)


def _rms_bwd(dy, v, g):
    r = _rms_scale(v)
    nrm = v * r
    dn = dy * g
    return r * (dn - nrm * jnp.mean(dn * nrm, axis=-1, keepdims=True)), jnp.sum(dy * nrm, axis=0, keepdims=True)


def _mm_post(a, w, g, xres, name):
    s, k = a.shape
    d = w.shape[1]
    tm = min(MM_ROWS, s)

    def body(a_ref, w_ref, g_ref, x_ref, m_ref, o_ref):
        mv = jnp.dot(a_ref[...], w_ref[...], preferred_element_type=F32)
        m_ref[...] = mv
        o_ref[...] = x_ref[...] + mv * _rms_scale(mv) * g_ref[...]

    row = pl.BlockSpec((tm, d), lambda i: (i, 0))
    return pl.pallas_call(
        body, name=name, grid=(s // tm,),
        in_specs=[pl.BlockSpec((tm, k), lambda i: (i, 0)), _full((k, d)), _full((1, d)), row],
        out_specs=[row, row],
        out_shape=[jax.ShapeDtypeStruct((s, d), F32), jax.ShapeDtypeStruct((s, d), F32)],
        compiler_params=_cparams(("parallel",)),
    )(a, w, g, xres)


def _mm_post_bwd(dy, m, g, w_t, name):
    s, d = m.shape
    k = w_t.shape[1]
    tm = min(MM_ROWS, s)

    def body(dy_ref, m_ref, g_ref, wt_ref, dm_ref, da_ref, dg_ref):
        @pl.when(pl.program_id(0) == 0)
        def _():
            dg_ref[...] = jnp.zeros_like(dg_ref)

        dm, dg = _rms_bwd(dy_ref[...], m_ref[...], g_ref[...])
        dg_ref[...] += dg
        dmb = dm.astype(BF16)
        dm_ref[...] = dmb
        da_ref[...] = jnp.dot(dmb, wt_ref[...], preferred_element_type=F32)

    row = pl.BlockSpec((tm, d), lambda i: (i, 0))
    return pl.pallas_call(
        body, name=name, grid=(s // tm,),
        in_specs=[row, row, _full((1, d)), _full((d, k))],
        out_specs=[row, pl.BlockSpec((tm, k), lambda i: (i, 0)), _full((1, d))],
        out_shape=[jax.ShapeDtypeStruct((s, d), BF16), jax.ShapeDtypeStruct((s, k), F32),
                   jax.ShapeDtypeStruct((1, d), F32)],
        compiler_params=_cparams(("arbitrary",)),
    )(dy, m, g, w_t)


def _mm_tn(a, b, name):
    s, k = a.shape
    _, n = b.shape
    tk = _tile(k, 1408)
    tn = _tile(n, 1408)
    ts = min(2048, s)

    def body(a_ref, b_ref, o_ref):
        @pl.when(pl.program_id(2) == 0)
        def _():
            o_ref[...] = jnp.zeros_like(o_ref)

        o_ref[...] += lax.dot_general(a_ref[...], b_ref[...], (((0,), (0,)), ((), ())),
                                      preferred_element_type=F32)

    return pl.pallas_call(
        body, name=name, grid=(k // tk, n // tn, s // ts),
        in_specs=[pl.BlockSpec((ts, tk), lambda i, j, l: (l, i)), pl.BlockSpec((ts, tn), lambda i, j, l: (l, j))],
        out_specs=pl.BlockSpec((tk, tn), lambda i, j, l: (i, j)),
        out_shape=jax.ShapeDtypeStruct((k, n), F32),
        compiler_params=_cparams(("parallel", "parallel", "arbitrary")),
    )(a, b)


def _cur(tr, w, col=0):
    return pl.BlockSpec((tr, w), lambda i: (i, col))


def _prev(tr, h, w, col=0):
    return pl.BlockSpec((h, w), lambda i: (jnp.maximum(i * (tr // h) - 1, 0), col))


def _next(tr, h, w, nrows, col=0):
    last = nrows // h - 1
    return pl.BlockSpec((h, w), lambda i: (jnp.minimum((i + 1) * (tr // h), last), col))


def _full(shape):
    return pl.BlockSpec(shape, lambda i: tuple(0 for _ in shape))


def _silu_parts(g):
    sig = jax.nn.sigmoid(g)
    return sig, g * sig


FFN_CW = 256
FFN_NBUF = 3


def _conv3_taps(buf, w, off, rows):
    return (w[0:1] * buf[pl.ds(off, rows), :] + w[1:2] * buf[pl.ds(off + 1, rows), :]
            + w[2:3] * buf[pl.ds(off + 2, rows), :])


WHOLE_VMEM = pl.BlockSpec(memory_space=pltpu.VMEM)


def _ffn_fwd(x, g_pre, wu, conv_w, wd, g_post, name, target=None):
    s, d = x.shape
    f2 = wu.shape[1]
    f = f2 // 2
    tr = min(256, s)
    h = HALO16
    cw = FFN_CW
    head = target is not None

    def body(*refs):
        if head:
            (x_ref, gpre_ref, wu_ref, cw_ref, wd_ref, gpost_ref, t_ref, xo_ref, sq_ref, f_ref, h_ref, up_ref, u_ref,
             carry, gbuf, vbuf, facc) = refs
        else:
            (x_ref, gpre_ref, wu_ref, cw_ref, wd_ref, gpost_ref, xo_ref, f_ref, h_ref, up_ref, u_ref,
             carry, gbuf, vbuf, facc) = refs

        @pl.when(pl.program_id(0) == 0)
        def _():
            carry[...] = jnp.zeros_like(carry)
            if head:
                sq_ref[...] = jnp.zeros_like(sq_ref)

        xv = x_ref[...]
        r = lax.rsqrt(jnp.mean(xv * xv, axis=-1, keepdims=True) + RMS_EPS)
        hv = (xv * r * gpre_ref[...]).astype(BF16)
        h_ref[...] = hv
        nchunk = f // cw

        def up_proj(j):
            for buf, base in ((gbuf, 0), (vbuf, f)):
                cs = slice(base + j * cw, base + (j + 1) * cw)
                dst = buf.at[j % FFN_NBUF]
                upc = jnp.dot(hv, wu_ref[:, cs], preferred_element_type=F32)
                up_ref[:, cs] = upc.astype(BF16)
                dst[0:h, :] = carry[:, cs]
                dst[h:h + tr, :] = upc
                carry[:, cs] = upc[tr - h:tr, :]

        def down_proj(j, act):
            part = jnp.dot(act, wd_ref[j * cw:(j + 1) * cw, :], preferred_element_type=F32)
            if j == 0:
                facc[...] = part
            else:
                facc[...] += part

        up_proj(0)
        pending = None
        for j in range(nchunk):
            cg = slice(j * cw, (j + 1) * cw)
            cv = slice(f + j * cw, f + (j + 1) * cw)
            if j + 1 < nchunk:
                up_proj(j + 1)
            if pending is not None:
                down_proj(*pending)
            g = _conv3_taps(gbuf.at[j % FFN_NBUF], cw_ref[:, cg], h - 2, tr)
            v = _conv3_taps(vbuf.at[j % FFN_NBUF], cw_ref[:, cv], h - 2, tr)
            u_ref[:, cg] = g.astype(BF16)
            u_ref[:, cv] = v.astype(BF16)
            act = (g * jax.nn.sigmoid(g) * v).astype(BF16)
            pending = (j, act)
        down_proj(*pending)
        fv = facc[...]
        f_ref[...] = fv
        r2 = lax.rsqrt(jnp.mean(fv * fv, axis=-1, keepdims=True) + RMS_EPS)
        xo = xv + fv * r2 * gpost_ref[...]
        if head:
            err = xo - t_ref[...]
            xo_ref[...] = err * (1.0 / d)
            sq_ref[...] += jnp.sum(err * err, axis=0, keepdims=True)
        else:
            xo_ref[...] = xo

    row = _cur(tr, d)
    wide = _cur(tr, f2)
    vec = _full((1, d))
    out_specs = [row] + ([vec] if head else []) + [row, row, wide, wide]
    out_shape = ([jax.ShapeDtypeStruct((s, d), F32)] + ([jax.ShapeDtypeStruct((1, d), F32)] if head else [])
                 + [jax.ShapeDtypeStruct((s, d), F32), jax.ShapeDtypeStruct((s, d), BF16),
                    jax.ShapeDtypeStruct((s, f2), BF16), jax.ShapeDtypeStruct((s, f2), BF16)])
    return pl.pallas_call(
        body, name=name, grid=(s // tr,),
        in_specs=[row, vec, WHOLE_VMEM, _full((3, f2)), WHOLE_VMEM, vec] + ([row] if head else []),
        out_specs=out_specs, out_shape=out_shape,
        scratch_shapes=[pltpu.VMEM((h, f2), F32), pltpu.VMEM((FFN_NBUF, h + tr, cw), F32),
                        pltpu.VMEM((FFN_NBUF, h + tr, cw), F32), pltpu.VMEM((tr, d), F32)],
        compiler_params=_cparams(("arbitrary",)),
    )(*((x, g_pre, wu, conv_w, wd, g_post) + ((target,) if head else ())))


def _ffn_bwd(dxo, fout, x, up, u, g_pre, g_post, wd_t, wu_t, conv_w, name):
    s, d = x.shape
    f2 = up.shape[1]
    f = f2 // 2
    tr = min(256, s)
    nt = s // tr
    h = HALO16
    cw = FFN_CW

    def body(dy_ref, f_ref, x_ref, up_ref, u_ref, gpre_ref, gpost_ref, wdt_ref, wut_ref, cw_ref,
             dx_ref, dup_ref, act_ref, df_ref, dcw_ref, dgpost_ref, dgpre_ref, carry, dgbuf, dvbuf, dhacc):
        @pl.when(pl.program_id(0) == 0)
        def _():
            carry[...] = jnp.zeros_like(carry)
            dcw_ref[...] = jnp.zeros_like(dcw_ref)
            dgpost_ref[...] = jnp.zeros_like(dgpost_ref)
            dgpre_ref[...] = jnp.zeros_like(dgpre_ref)

        dy = dy_ref[...]
        fv = f_ref[...]
        r = lax.rsqrt(jnp.mean(fv * fv, axis=-1, keepdims=True) + RMS_EPS)
        nrm = fv * r
        dn = dy * gpost_ref[...]
        dfv = (r * (dn - nrm * jnp.mean(dn * nrm, axis=-1, keepdims=True))).astype(BF16)
        dgpost_ref[...] += jnp.sum(dy * nrm, axis=0, keepdims=True)
        df_ref[...] = dfv
        nchunk = f // cw

        def dh_part(dupb, cs, first):
            part = jnp.dot(dupb, wut_ref[cs, :], preferred_element_type=F32)
            if first:
                dhacc[...] = part
            else:
                dhacc[...] += part

        dact_next = jnp.dot(dfv, wdt_ref[:, 0:cw], preferred_element_type=F32)
        for j in range(nchunk):
            ch = slice(j * cw, (j + 1) * cw)
            cg = ch
            cv = slice(f + j * cw, f + (j + 1) * cw)
            dact = dact_next
            if j + 1 < nchunk:
                dact_next = jnp.dot(dfv, wdt_ref[:, (j + 1) * cw:(j + 2) * cw], preferred_element_type=F32)
            g = u_ref[:, cg].astype(F32)
            v = u_ref[:, cv].astype(F32)
            sig, sil = _silu_parts(g)
            act_ref[:, ch] = (sil * v).astype(BF16)
            du_g = dact * v * (sig * (1.0 + g * (1.0 - sig)))
            du_v = dact * sil
            for k, (dbuf, du, cs) in enumerate(((dgbuf.at[j % FFN_NBUF], du_g, cg), (dvbuf.at[j % FFN_NBUF], du_v, cv))):
                dbuf[0:tr, :] = du
                dbuf[tr:tr + h, :] = carry[:, cs]
                carry[:, cs] = du[0:h, :]
                w = cw_ref[:, cs]
                xin = up_ref[:, cs].astype(F32)
                acc = None
                for sh in range(3):
                    dsh = dbuf[pl.ds(sh, tr), :]
                    term = w[2 - sh:3 - sh] * dsh
                    acc = term if acc is None else acc + term
                    dcw_ref[2 - sh:3 - sh, cs] += jnp.sum(xin * dsh, axis=0, keepdims=True)
                dupb = acc.astype(BF16)
                dup_ref[:, cs] = dupb
                dh_part(dupb, cs, j == 0 and k == 0)
        dh = dhacc[...]
        xv = x_ref[...]
        r1 = lax.rsqrt(jnp.mean(xv * xv, axis=-1, keepdims=True) + RMS_EPS)
        n1 = xv * r1
        dn1 = dh * gpre_ref[...]
        dx_ref[...] = dy + r1 * (dn1 - n1 * jnp.mean(dn1 * n1, axis=-1, keepdims=True))
        dgpre_ref[...] += jnp.sum(dh * n1, axis=0, keepdims=True)

    def rev(w):
        return pl.BlockSpec((tr, w), lambda i: (nt - 1 - i, 0))

    vec = _full((1, d))
    return pl.pallas_call(
        body, name=name, grid=(nt,),
        in_specs=[rev(d), rev(d), rev(d), rev(f2), rev(f2), vec, vec, WHOLE_VMEM, WHOLE_VMEM, _full((3, f2))],
        out_specs=[rev(d), rev(f2), rev(f), rev(d), _full((3, f2)), vec, vec],
        out_shape=[jax.ShapeDtypeStruct((s, d), F32), jax.ShapeDtypeStruct((s, f2), BF16),
                   jax.ShapeDtypeStruct((s, f), BF16), jax.ShapeDtypeStruct((s, d), BF16),
                   jax.ShapeDtypeStruct((3, f2), F32), jax.ShapeDtypeStruct((1, d), F32),
                   jax.ShapeDtypeStruct((1, d), F32)],
        scratch_shapes=[pltpu.VMEM((h, f2), F32), pltpu.VMEM((FFN_NBUF, tr + h, cw), F32),
                        pltpu.VMEM((FFN_NBUF, tr + h, cw), F32), pltpu.VMEM((tr, d), F32)],
        compiler_params=pltpu.CompilerParams(dimension_semantics=("arbitrary",), vmem_limit_bytes=FFN_BWD_VMEM),
    )(dxo, fout, x, up, u, g_pre, g_post, wd_t, wu_t, conv_w)


def _od_in_fwd(x, g, w, conv_w, name):
    s, d = x.shape
    d3 = w.shape[1]
    tr = min(256, s)
    h = HALO16
    cw = FFN_CW
    nchunk = d // cw

    def body(x_ref, g_ref, w_ref, cw_ref, h_ref, z_ref, y_ref, carry, buf):
        @pl.when(pl.program_id(0) == 0)
        def _():
            carry[...] = jnp.zeros_like(carry)

        xv = x_ref[...]
        hv = (xv * _rms_scale(xv) * g_ref[...]).astype(BF16)
        h_ref[...] = hv

        def project(j):
            out = []
            for part in range(3):
                cs = slice(part * d + j * cw, part * d + (j + 1) * cw)
                zc = jnp.dot(hv, w_ref[:, cs], preferred_element_type=F32).astype(BF16)
                z_ref[:, cs] = zc
                out.append(zc.astype(F32))
            return out

        nxt = project(0)
        for j in range(nchunk):
            cb = slice(j * cw, (j + 1) * cw)
            bval, cval, uval = nxt
            if j + 1 < nchunk:
                nxt = project(j + 1)
            bf = buf.at[j % FFN_NBUF]
            cu = cval * uval
            bf[0:h, :] = carry[:, cb]
            bf[h:h + tr, :] = cu
            carry[:, cb] = cu[tr - h:tr, :]
            y_ref[:, cb] = (bval * _conv3_taps(bf, cw_ref[:, cb], h - 2, tr)).astype(BF16)

    row = _cur(tr, d)
    return pl.pallas_call(
        body, name=name, grid=(s // tr,),
        in_specs=[row, _full((1, d)), WHOLE_VMEM, _full((3, d))],
        out_specs=[row, _cur(tr, d3), row],
        out_shape=[jax.ShapeDtypeStruct((s, d), BF16), jax.ShapeDtypeStruct((s, d3), BF16),
                   jax.ShapeDtypeStruct((s, d), BF16)],
        scratch_shapes=[pltpu.VMEM((h, d), F32), pltpu.VMEM((FFN_NBUF, h + tr, cw), F32)],
        compiler_params=_cparams(("arbitrary",)),
    )(x, g, w, conv_w)


def _od_in_bwd(dy, z, conv_w, w_t, x, g, res, name):
    s, d3 = z.shape
    d = d3 // 3
    tr = min(256, s)
    h = HALO16
    cw = FFN_CW
    ext = tr + h

    def body(dy_ref, dyn_ref, z_ref, zp_ref, zn_ref, w_ref, wt_ref, x_ref, g_ref, res_ref,
             o_ref, dw_ref, dx_ref, dg_ref, buf, dbuf, dhacc):
        i = pl.program_id(0)
        first = i == 0
        last = i == pl.num_programs(0) - 1

        @pl.when(first)
        def _():
            dw_ref[...] = jnp.zeros_like(dw_ref)
            dg_ref[...] = jnp.zeros_like(dg_ref)

        started = False
        for j in range(d // cw):
            cb = slice(j * cw, (j + 1) * cw)
            cc = slice(d + j * cw, d + (j + 1) * cw)
            cu = slice(2 * d + j * cw, 2 * d + (j + 1) * cw)
            bf = buf.at[j % FFN_NBUF]
            db = dbuf.at[j % FFN_NBUF]
            w = w_ref[:, cb]
            cval = z_ref[:, cc].astype(F32)
            uval = z_ref[:, cu].astype(F32)
            bf[0:h, :] = jnp.where(first, 0.0, zp_ref[:, cc].astype(F32) * zp_ref[:, cu].astype(F32))
            bf[h:h + tr, :] = cval * uval
            k = _conv3_taps(bf, w, h - 2, tr)
            dyv = dy_ref[:, cb]
            db[0:tr, :] = dyv * z_ref[:, cb].astype(F32)
            db[tr:ext, :] = jnp.where(last, 0.0, dyn_ref[:, cb] * zn_ref[:, cb].astype(F32))
            dcu = w[2:3] * db[pl.ds(0, tr), :] + w[1:2] * db[pl.ds(1, tr), :] + w[0:1] * db[pl.ds(2, tr), :]
            dk = db[pl.ds(0, tr), :]
            for t in range(3):
                dw_ref[t:t + 1, cb] += jnp.sum(dk * bf[pl.ds(h - 2 + t, tr), :], axis=0, keepdims=True)
            for cs, val in ((cb, dyv * k), (cc, dcu * uval), (cu, dcu * cval)):
                piece = val.astype(BF16)
                o_ref[:, cs] = piece
                part = jnp.dot(piece, wt_ref[cs, :], preferred_element_type=F32)
                if started:
                    dhacc[...] += part
                else:
                    dhacc[...] = part
                    started = True
        dx, dg = _rms_bwd(dhacc[...], x_ref[...], g_ref[...])
        dg_ref[...] += dg
        dx_ref[...] = res_ref[...] + dx

    row = _cur(tr, d)
    vec = _full((1, d))
    return pl.pallas_call(
        body, name=name, grid=(s // tr,),
        in_specs=[row, _next(tr, h, d, s), _cur(tr, d3), _prev(tr, h, d3), _next(tr, h, d3, s), _full((3, d)),
                  WHOLE_VMEM, row, vec, row],
        out_specs=[_cur(tr, d3), _full((3, d)), row, vec],
        out_shape=[jax.ShapeDtypeStruct((s, d3), BF16), jax.ShapeDtypeStruct((3, d), F32),
                   jax.ShapeDtypeStruct((s, d), F32), jax.ShapeDtypeStruct((1, d), F32)],
        scratch_shapes=[pltpu.VMEM((FFN_NBUF, h + tr, cw), F32), pltpu.VMEM((FFN_NBUF, ext, cw), F32),
                        pltpu.VMEM((tr, d), F32)],
        compiler_params=_cparams(("arbitrary",)),
    )(dy, dy, z, z, z, conv_w, w_t, x, g, res)


Q0 = 2 * A_CH
K0 = Q0 + N_Q_HEADS * HEAD_DIM
V0 = K0 + 2 * HEAD_DIM
EVEN_IN = V0 + 2 * HEAD_DIM


def _rope_tables(positions):
    half = ROPE_DIM // 2
    inv_freq = ROPE_THETA ** (-(jnp.arange(half, dtype=F32) * 2.0 / ROPE_DIM))
    ang = positions.astype(F32)[:, None] * inv_freq
    cs = jnp.concatenate([jnp.cos(ang), jnp.sin(ang)], axis=1)
    spread = np.zeros((2 * half, 3 * LANES), np.float32)
    const = np.zeros((1, 3 * LANES), np.float32)
    for lane in range(3 * LANES):
        dim, part = lane % HEAD_DIM, lane // LANES
        if part == 0:
            if dim < ROPE_DIM:
                spread[dim % half, lane] = 1.0
            else:
                const[0, lane] = 1.0
        elif part == 1 and half <= dim < ROPE_DIM:
            spread[half + dim - half, lane] = 1.0
        elif part == 2 and dim < half:
            spread[half + dim, lane] = -1.0
    return jnp.dot(cs, jnp.asarray(spread), precision=lax.Precision.HIGHEST) + jnp.asarray(const)


def _rope_fwd(x, tab):
    c, sa, sb = tab[:, 0:LANES], tab[:, LANES:2 * LANES], tab[:, 2 * LANES:3 * LANES]
    return x * c + pltpu.roll(x, 8, 1) * sa + pltpu.roll(x, LANES - 8, 1) * sb


def _rope_bwd(dy, tab):
    c, sa, sb = tab[:, 0:LANES], tab[:, LANES:2 * LANES], tab[:, 2 * LANES:3 * LANES]
    return dy * c + pltpu.roll(dy * sa, LANES - 8, 1) + pltpu.roll(dy * sb, 8, 1)


def _ln_fwd(c, g, b):
    mu = jnp.mean(c, axis=-1, keepdims=True)
    xc = c - mu
    r = lax.rsqrt(jnp.mean(xc * xc, axis=-1, keepdims=True) + LN_EPS)
    nrm = xc * r
    return nrm, r, nrm * g + b


def _phase_fill(buf, ph, rows):
    for k in range(1, 8):
        ph[k - 1, 0:rows - 8, :] = buf[pl.ds(k, rows - 8), :]


def _phase_rows(buf, ph, off, n, cs):
    k = off % 8
    src = buf if k == 0 else ph.at[k - 1]
    return src[pl.ds(off - k, n), cs]


def _ev_in_fwd(x, g_pre, w_in, tab, conv_w, conv_b, ln_g, ln_b, name):
    s, d = x.shape
    tr = min(256, s)
    h = HALO32
    cw = LANES
    pw = 2 * LANES

    def body(x_ref, gpre_ref, win_ref, tab_ref, w_ref, b_ref, g_ref, lb_ref, h_ref, z_ref, c_ref, a_ref, qkv_ref,
             gbuf, cbuf, gph, carry):
        @pl.when(pl.program_id(0) == 0)
        def _():
            carry[...] = jnp.zeros_like(carry)

        xv = x_ref[...]
        hv = (xv * _rms_scale(xv) * gpre_ref[...]).astype(BF16)
        h_ref[...] = hv

        def project(lo_col, hi_col):
            for c0 in range(lo_col, hi_col, pw):
                cs = slice(c0, c0 + pw)
                z_ref[:, cs] = jnp.dot(hv, win_ref[:, cs], preferred_element_type=F32).astype(BF16)

        project(0, 2 * A_CH)
        glu = z_ref[:, 0:A_CH].astype(F32) * jax.nn.sigmoid(z_ref[:, A_CH:2 * A_CH].astype(F32))
        project(2 * A_CH, EVEN_IN)
        gbuf[0:h, :] = carry[...]
        gbuf[h:h + tr, :] = glu
        carry[...] = glu[tr - h:tr, :]
        _phase_fill(gbuf, gph, h + tr)
        for j in range(A_CH // cw):
            cs = slice(j * cw, (j + 1) * cw)
            acc = jnp.broadcast_to(b_ref[:, cs], (tr, cw))
            for t in range(A_CONV):
                acc = acc + w_ref[t:t + 1, cs] * _phase_rows(gbuf, gph, h - (A_CONV - 1) + t, tr, cs)
            cbuf[:, cs] = acc
        c = cbuf[...]
        c_ref[...] = c.astype(BF16)
        _, _, l = _ln_fwd(c, g_ref[...], lb_ref[...])
        a_ref[...] = (l * jax.nn.sigmoid(l)).astype(BF16)
        tab_v = tab_ref[...]
        for p in range(4):
            xq = z_ref[:, Q0 + p * LANES:Q0 + (p + 1) * LANES].astype(F32)
            qkv_ref[:, p * LANES:(p + 1) * LANES] = _rope_fwd(xq, tab_v).astype(BF16)
        lane = lax.broadcasted_iota(jnp.int32, (tr, LANES), 1)
        lo = lane < HEAD_DIM
        kr = _rope_fwd(z_ref[:, K0:K0 + LANES].astype(F32), tab_v)
        vr = z_ref[:, V0:V0 + LANES].astype(F32)
        for base, val in ((4 * LANES, kr), (6 * LANES, vr)):
            sw = pltpu.roll(val, HEAD_DIM, 1)
            qkv_ref[:, base:base + LANES] = jnp.where(lo, val, sw).astype(BF16)
            qkv_ref[:, base + LANES:base + 2 * LANES] = jnp.where(lo, sw, val).astype(BF16)

    return pl.pallas_call(
        body, name=name, grid=(s // tr,),
        in_specs=[_cur(tr, d), _full((1, d)), WHOLE_VMEM, _cur(tr, 3 * LANES), _full((A_CONV, A_CH)),
                  _full((1, A_CH)), _full((1, A_CH)), _full((1, A_CH))],
        out_specs=[_cur(tr, d), _cur(tr, EVEN_IN), _cur(tr, A_CH), _cur(tr, A_CH), _cur(tr, 2 * A_CH)],
        out_shape=[jax.ShapeDtypeStruct((s, d), BF16), jax.ShapeDtypeStruct((s, EVEN_IN), BF16),
                   jax.ShapeDtypeStruct((s, A_CH), BF16), jax.ShapeDtypeStruct((s, A_CH), BF16),
                   jax.ShapeDtypeStruct((s, 2 * A_CH), BF16)],
        scratch_shapes=[pltpu.VMEM((h + tr, A_CH), F32), pltpu.VMEM((tr, A_CH), F32),
                        pltpu.VMEM((7, h + tr, A_CH), F32), pltpu.VMEM((h, A_CH), F32)],
        compiler_params=_cparams(("arbitrary",)),
    )(x, g_pre, w_in, tab, conv_w, conv_b, ln_g, ln_b)


def _ev_mid_bwd(dcat, c, z, dq, dkv, tab, conv_w, ln_g, ln_b, w_t, x, g_pre, res, name):
    s = z.shape[0]
    tr = min(256, s)
    h = HALO32
    cw = LANES
    ext = tr + h

    def body(da_ref, dan_ref, c_ref, cn_ref, z_ref, dq_ref, dkv_ref, tab_ref, w_ref, g_ref, lb_ref,
             wt_ref, x_ref, gpre_ref, res_ref, dz_ref, dw_ref, dvec_ref, dx_ref, dg_ref, dcbuf, dcph, dhacc):
        i = pl.program_id(0)
        first = i == 0
        last = i == pl.num_programs(0) - 1

        @pl.when(first)
        def _():
            dw_ref[...] = jnp.zeros_like(dw_ref)
            dvec_ref[...] = jnp.zeros_like(dvec_ref)
            dg_ref[...] = jnp.zeros_like(dg_ref)

        started = []

        def dh_part(cs):
            part = jnp.dot(dz_ref[:, cs], wt_ref[cs, :], preferred_element_type=F32)
            if started:
                dhacc[...] += part
            else:
                dhacc[...] = part
                started.append(True)

        tab_v = tab_ref[...]
        for p in range(4):
            cs = slice(p * LANES, (p + 1) * LANES)
            dz_ref[:, Q0 + p * LANES:Q0 + (p + 1) * LANES] = _rope_bwd(dq_ref[:, cs], tab_v).astype(BF16)
        lane = lax.broadcasted_iota(jnp.int32, (tr, LANES), 1)
        lo = lane < HEAD_DIM

        def fold(base):
            p0 = dkv_ref[:, base:base + LANES]
            p1 = dkv_ref[:, base + LANES:base + 2 * LANES]
            s0 = p0 + pltpu.roll(p0, HEAD_DIM, 1)
            s1 = p1 + pltpu.roll(p1, HEAD_DIM, 1)
            return jnp.where(lo, s0, s1)

        dz_ref[:, K0:K0 + LANES] = _rope_bwd(fold(0), tab_v).astype(BF16)
        dz_ref[:, V0:V0 + LANES] = fold(2 * LANES).astype(BF16)
        dh_part(slice(Q0, EVEN_IN))

        gv = g_ref[...]

        def ln_silu_bwd(cv, dav):
            nrm, r, l = _ln_fwd(cv, gv, lb_ref[...])
            sig = jax.nn.sigmoid(l)
            dl = dav * (sig * (1.0 + l * (1.0 - sig)))
            dn = dl * gv
            dc = r * (dn - jnp.mean(dn, axis=-1, keepdims=True) - nrm * jnp.mean(dn * nrm, axis=-1, keepdims=True))
            return dc, dl, nrm

        dc, dl, nrm = ln_silu_bwd(c_ref[...].astype(F32), da_ref[...])
        dcn, _, _ = ln_silu_bwd(cn_ref[...].astype(F32), dan_ref[...])
        dcbuf[0:tr, :] = dc
        dcbuf[tr:ext, :] = jnp.where(last, 0.0, dcn)
        dvec_ref[0:1, :] += jnp.sum(dc, axis=0, keepdims=True)
        dvec_ref[1:2, :] += jnp.sum(dl * nrm, axis=0, keepdims=True)
        dvec_ref[2:3, :] += jnp.sum(dl, axis=0, keepdims=True)

        _phase_fill(dcbuf, dcph, ext)
        a_lin = z_ref[:, 0:A_CH].astype(F32)
        sig_g = jax.nn.sigmoid(z_ref[:, A_CH:2 * A_CH].astype(F32))
        glu = a_lin * sig_g
        for j in range(A_CH // cw):
            cs = slice(j * cw, (j + 1) * cw)
            gluj = glu[:, cs]
            acc = jnp.zeros((tr, cw), F32)
            for t in range(A_CONV):
                dsh = _phase_rows(dcbuf, dcph, A_CONV - 1 - t, tr, cs)
                acc = acc + w_ref[t:t + 1, cs] * dsh
                dw_ref[t:t + 1, cs] += jnp.sum(gluj * dsh, axis=0, keepdims=True)
            dz_ref[:, cs] = (acc * sig_g[:, cs]).astype(BF16)
            dz_ref[:, A_CH + j * cw:A_CH + (j + 1) * cw] = (
                acc * a_lin[:, cs] * sig_g[:, cs] * (1.0 - sig_g[:, cs])).astype(BF16)
            if j % 2 == 1:
                dh_part(slice((j - 1) * cw, (j + 1) * cw))
                dh_part(slice(A_CH + (j - 1) * cw, A_CH + (j + 1) * cw))

        dx, dg = _rms_bwd(dhacc[...], x_ref[...], gpre_ref[...])
        dg_ref[...] += dg
        dx_ref[...] = res_ref[...] + dx

    row = _cur(tr, D_MODEL)
    vec = _full((1, D_MODEL))
    return pl.pallas_call(
        body, name=name, grid=(s // tr,),
        in_specs=[_cur(tr, A_CH), _next(tr, h, A_CH, s), _cur(tr, A_CH), _next(tr, h, A_CH, s),
                  _cur(tr, EVEN_IN), _cur(tr, A_CH), _cur(tr, A_CH), _cur(tr, 3 * LANES),
                  _full((A_CONV, A_CH)), _full((1, A_CH)), _full((1, A_CH)), WHOLE_VMEM, row, vec, row],
        out_specs=[_cur(tr, EVEN_IN), _full((A_CONV, A_CH)), _full((8, A_CH)), row, vec],
        out_shape=[jax.ShapeDtypeStruct((s, EVEN_IN), BF16), jax.ShapeDtypeStruct((A_CONV, A_CH), F32),
                   jax.ShapeDtypeStruct((8, A_CH), F32), jax.ShapeDtypeStruct((s, D_MODEL), F32),
                   jax.ShapeDtypeStruct((1, D_MODEL), F32)],
        scratch_shapes=[pltpu.VMEM((ext, A_CH), F32), pltpu.VMEM((7, ext, A_CH), F32),
                        pltpu.VMEM((tr, D_MODEL), F32)],
        compiler_params=_cparams(("arbitrary",)),
    )(dcat, dcat, c, c, z, dq, dkv, tab, conv_w, ln_g, ln_b, w_t, x, g_pre, res)


NT = (((1,), (1,)), ((), ()))
TN = (((0,), (0,)), ((), ()))
QB = WINDOW
SCALE = HEAD_DIM ** -0.5


def _att_scores(q2m, kwin):
    return lax.dot_general(q2m, kwin, NT, preferred_element_type=F32)


def _att_probs(raw, sink, mask):
    sc = jnp.where(mask, raw * SCALE, -jnp.inf)
    mx = jnp.maximum(jnp.max(sc, axis=-1, keepdims=True), sink)
    p = jnp.exp(sc - mx)
    ps = jnp.exp(sink - mx)
    inv = 1.0 / (jnp.sum(p, axis=-1, keepdims=True) + ps)
    return p * inv, ps * inv


def _att_mask(i):
    r = lax.broadcasted_iota(jnp.int32, (QB, 2 * QB), 0)
    kc = lax.broadcasted_iota(jnp.int32, (QB, 2 * QB), 1)
    diff = r + QB - kc
    return (diff >= 0) & (diff < WINDOW) & ((kc >= QB) | (i > 0))


def _half_masks(dtype):
    lane = lax.broadcasted_iota(jnp.int32, (1, LANES), 1)
    return (lane < HEAD_DIM).astype(dtype), (lane >= HEAD_DIM).astype(dtype)


def _att_fwd(qkv, a, sinks, name):
    s = qkv.shape[0]
    nb = s // QB

    def body(sink_ref, qkv_ref, kvp_ref, a_ref, o_ref):
        i = pl.program_id(0)
        mask = _att_mask(i)
        mlo, mhi = _half_masks(BF16)
        o_ref[:, 0:A_CH] = a_ref[...]

        def window(col):
            return jnp.concatenate([kvp_ref[:, col * LANES:(col + 1) * LANES],
                                    qkv_ref[:, A_CH + col * LANES:A_CH + (col + 1) * LANES]], axis=0)

        def raw_scores(p):
            q2 = qkv_ref[:, p * LANES:(p + 1) * LANES]
            kwin = window(p // 2)
            return _att_scores(q2 * mlo, kwin), _att_scores(q2 * mhi, kwin)

        nxt = raw_scores(0)
        for p in range(4):
            raw_e, raw_o = nxt
            if p + 1 < 4:
                nxt = raw_scores(p + 1)
            vwin = window(2 + p // 2)
            pe, _ = _att_probs(raw_e, sink_ref[2 * p], mask)
            po, _ = _att_probs(raw_o, sink_ref[2 * p + 1], mask)
            o = (jnp.dot(pe.astype(BF16), vwin * mlo, preferred_element_type=F32)
                 + jnp.dot(po.astype(BF16), vwin * mhi, preferred_element_type=F32))
            o_ref[:, A_CH + p * LANES:A_CH + (p + 1) * LANES] = o.astype(BF16)

    grid_spec = pltpu.PrefetchScalarGridSpec(
        num_scalar_prefetch=1, grid=(nb,),
        in_specs=[pl.BlockSpec((QB, 2 * A_CH), lambda i, sk: (i, 0)),
                  pl.BlockSpec((QB, A_CH), lambda i, sk: (jnp.maximum(i - 1, 0), 1)),
                  pl.BlockSpec((QB, A_CH), lambda i, sk: (i, 0))],
        out_specs=pl.BlockSpec((QB, 2 * A_CH), lambda i, sk: (i, 0)),
    )
    return pl.pallas_call(
        body, name=name, grid_spec=grid_spec,
        out_shape=jax.ShapeDtypeStruct((s, 2 * A_CH), BF16),
        compiler_params=_cparams(("parallel",)),
    )(sinks, qkv, qkv, a)


def _att_bwd(qkv, dcat, sinks, name):
    s = qkv.shape[0]
    nb = s // QB

    def body(sink_ref, qkv_ref, kvp_ref, do_ref, dq_ref, dkv_ref, ds_ref, carry):
        i = pl.program_id(0)

        @pl.when(i == 0)
        def _():
            ds_ref[...] = jnp.zeros_like(ds_ref)
            carry[...] = jnp.zeros_like(carry)

        @pl.when(i < nb)
        def _():
            mask = _att_mask(i)
            mlo, mhi = _half_masks(BF16)
            dwin = [jnp.zeros((2 * QB, LANES), F32) for _ in range(4)]

            def window(col):
                return jnp.concatenate([kvp_ref[:, col * LANES:(col + 1) * LANES],
                                        qkv_ref[:, A_CH + col * LANES:A_CH + (col + 1) * LANES]], axis=0)

            def first_products(n):
                p, hm = n // 2, (mlo, mhi)[n % 2]
                qm = qkv_ref[:, p * LANES:(p + 1) * LANES] * hm
                dom = do_ref[:, p * LANES:(p + 1) * LANES].astype(BF16) * hm
                kwin, vwin = window(p // 2), window(2 + p // 2)
                return (qm, dom, kwin * hm, _att_scores(qm, kwin),
                        lax.dot_general(dom, vwin, NT, preferred_element_type=F32))

            nxt = first_products(0)
            dq2 = None
            for n in range(N_Q_HEADS):
                g = n // 4
                qm, dom, kwm, raw, dp = nxt
                if n + 1 < N_Q_HEADS:
                    nxt = first_products(n + 1)
                prob, psink = _att_probs(raw, sink_ref[n], mask)
                delta = jnp.sum(prob * dp, axis=-1, keepdims=True)
                dsc = (prob * (dp - delta) * SCALE).astype(BF16)
                ds_ref[n:n + 1, :] += jnp.broadcast_to(jnp.sum(-psink * delta, axis=0, keepdims=True), (1, LANES))
                part = jnp.dot(dsc, kwm, preferred_element_type=F32)
                dq2 = part if n % 2 == 0 else dq2 + part
                dwin[g] = dwin[g] + lax.dot_general(dsc, qm, TN, preferred_element_type=F32)
                dwin[2 + g] = dwin[2 + g] + lax.dot_general(prob.astype(BF16), dom, TN, preferred_element_type=F32)
                if n % 2 == 1:
                    dq_ref[:, (n // 2) * LANES:(n // 2 + 1) * LANES] = dq2
            for n in range(4):
                cs = slice(n * LANES, (n + 1) * LANES)
                dkv_ref[:, cs] = carry[:, cs] + dwin[n][0:QB, :]
                carry[:, cs] = dwin[n][QB:2 * QB, :]

        @pl.when(i == nb)
        def _():
            dkv_ref[...] = carry[...]

    grid_spec = pltpu.PrefetchScalarGridSpec(
        num_scalar_prefetch=1, grid=(nb + 1,),
        in_specs=[pl.BlockSpec((QB, 2 * A_CH), lambda i, sk: (jnp.minimum(i, nb - 1), 0)),
                  pl.BlockSpec((QB, A_CH), lambda i, sk: (jnp.maximum(jnp.minimum(i, nb - 1) - 1, 0), 1)),
                  pl.BlockSpec((QB, A_CH), lambda i, sk: (jnp.minimum(i, nb - 1), 1))],
        out_specs=[pl.BlockSpec((QB, A_CH), lambda i, sk: (jnp.minimum(i, nb - 1), 0)),
                   pl.BlockSpec((QB, A_CH), lambda i, sk: (jnp.maximum(i - 1, 0), 0)),
                   pl.BlockSpec((8, LANES), lambda i, sk: (0, 0))],
        scratch_shapes=[pltpu.VMEM((QB, A_CH), F32)],
    )
    return pl.pallas_call(
        body, name=name, grid_spec=grid_spec,
        out_shape=[jax.ShapeDtypeStruct((s, A_CH), F32), jax.ShapeDtypeStruct((s, A_CH), F32),
                   jax.ShapeDtypeStruct((8, LANES), F32)],
        compiler_params=_cparams(("arbitrary",)),
    )(sinks, qkv, qkv, dcat)


def _local_step(x, positions, target, w, fetch=None, emit=None):
    row = lambda a, i: a[i:i + 1]
    tab = _rope_tables(positions)
    g = {}

    def ffn_fwd(xin, i, tgt=None):
        outs = _ffn_fwd(xin, row(w["ffn_norm_pre"], i), w["ffn_w_up", i], w["ffn_conv_w"][i],
                        w["ffn_w_down", i], row(w["ffn_norm_post"], i), f"ffn{i}_fwd", tgt)
        f, h, up, u = outs[-4:]
        return outs[:-4], (xin, f, h, up, u)

    def point(name, after):
        return emit(name, after, g) if emit is not None else 0.0

    def ffn_bwd(dxout, saved, i, tok):
        xin, f, h, up, u = saved
        dxin, dup, act, df, d_cw, dg_post, dg_pre = _ffn_bwd(
            dxout, f, xin, up, u, row(w["ffn_norm_pre"], i), row(w["ffn_norm_post"], i) + tok, w["ffn_w_down_t", i],
            w["ffn_w_up_t", i], w["ffn_conv_w"][i], f"ffn{i}_bwd")
        tok = point(f"ffn{i}_bwd_done", dxin)
        g["ffn_w_down", i] = _mm_tn(act, df, f"ffn{i}_down_dw")
        g["ffn_w_up", i] = _mm_tn(dup, h, f"ffn{i}_up_dw")
        return dxin, tok, dict(ffn_norm_post=dg_post, ffn_norm_pre=dg_pre, ffn_conv_w=d_cw)

    h0, z0, c0, a0, qkv = _ev_in_fwd(x, row(w["mix_norm_pre"], 0), w["ev_w_in"], tab, w["ev_a_conv_w"],
                                     w["ev_a_conv_b"], w["ev_a_ln_g"], w["ev_a_ln_b"], "ev_in")
    cat = _att_fwd(qkv, a0, w["ev_sinks"], "ev_att")
    m0, x1 = _mm_post(cat, w["ev_w_out"], row(w["mix_norm_post"], 0), x, "ev_out")
    if fetch is not None:
        w = {**w, **fetch("ffn0", x1)}
    (x2,), ffn0 = ffn_fwd(x1, 0)
    if fetch is not None:
        w = {**w, **fetch("layer1", x2)}
    h2, z1, y1 = _od_in_fwd(x2, row(w["mix_norm_pre"], 1), w["od_w_in"], w["od_conv_w"], "od_in")
    m1, x3 = _mm_post(y1, w["od_w_out"], row(w["mix_norm_post"], 1), x2, "od_out")
    (dx4, sq), ffn1 = ffn_fwd(x3, 1, target)

    dx3, _, gf1 = ffn_bwd(dx4, ffn1, 1, 0.0)
    dm1, dy1, dg_mo1 = _mm_post_bwd(dx3, m1, row(w["mix_norm_post"], 1), w["od_w_out_t"], "od_out_bwd")
    g["od_w_out"] = _mm_tn(y1, dm1, "od_out_dw")
    dz1, g["od_conv_w"], dx2, dg_mp1 = _od_in_bwd(dy1, z1, w["od_conv_w"], w["od_w_in_t"], x2,
                                                  row(w["mix_norm_pre"], 1), dx3, "od_in_bwd")
    g["od_w_in"] = _mm_tn(dz1, h2, "od_in_dw")
    tok = point("layer1_grads", dx2)

    dx1, tok, gf0 = ffn_bwd(dx2, ffn0, 0, tok)
    tok = tok + point("ffn0_grads", dx1)
    dm0, dcat, dg_mo0 = _mm_post_bwd(dx1, m0, row(w["mix_norm_post"], 0) + tok, w["ev_w_out_t"], "ev_out_bwd")
    tok = point("ev_out_bwd_done", dcat)
    g["ev_w_out"] = _mm_tn(cat, dm0, "ev_out_dw")
    dq, dkv, dsk = _att_bwd(qkv, dcat, w["ev_sinks"] + tok, "ev_att_bwd")
    tok = point("ev_att_bwd_done", dq)
    dz0, g["ev_a_conv_w"], dvec, dx0, dg_mp0 = _ev_mid_bwd(
        dcat, c0, z0, dq, dkv, tab, w["ev_a_conv_w"], w["ev_a_ln_g"] + tok, w["ev_a_ln_b"], w["ev_w_in_t"], x,
        row(w["mix_norm_pre"], 0), dx1, "ev_in_bwd")
    point("ev_mid_bwd_done", dz0)
    g["ev_w_in"] = _mm_tn(dz0, h0, "ev_in_dw")

    g["ev_a_conv_b"] = dvec[0:1]
    g["ev_a_ln_g"] = dvec[1:2]
    g["ev_a_ln_b"] = dvec[2:3]
    g["ev_sinks"] = dsk[:, 0]
    g["mix_norm_pre"] = jnp.concatenate([dg_mp0, dg_mp1], axis=0)
    g["mix_norm_post"] = jnp.concatenate([dg_mo0, dg_mo1], axis=0)
    g["ffn_norm_pre"] = jnp.concatenate([gf0["ffn_norm_pre"], gf1["ffn_norm_pre"]], axis=0)
    g["ffn_norm_post"] = jnp.concatenate([gf0["ffn_norm_post"], gf1["ffn_norm_post"]], axis=0)
    g["ffn_conv_w"] = jnp.stack([gf0["ffn_conv_w"], gf1["ffn_conv_w"]], axis=0)
    return sq, dx0, g


ANY = pl.BlockSpec(memory_space=pl.ANY)
PACK_COLS = 1024


def _me():
    return lax.axis_index("x"), lax.axis_index("y"), lax.axis_index("c")


def _other_chips(x, y):
    return [(1 - x, y), (x, 1 - y), (1 - x, 1 - y)]


def _remote(src, dst, send, recv, dev):
    return pltpu.make_async_remote_copy(src_ref=src, dst_ref=dst, send_sem=send, recv_sem=recv,
                                        device_id=dev, device_id_type=MESH)


def _gather_chips(wp, name):
    r, cols = wp.shape
    rh = r // 2

    def body(w_ref, o_ref, send, recv):
        x, y, c = _me()
        p = 2 * x + y
        sib = (x, y, 1 - c)
        chips = _other_chips(x, y)
        half = pl.ds(c * rh, rh)
        other = pl.ds((1 - c) * rh, rh)
        sent = [_remote(w_ref.at[half], o_ref.at[p, half], send.at[k], recv.at[k], (cx, cy, c))
                for k, (cx, cy) in enumerate(chips)]
        for cp in sent:
            cp.start()
        for k, (cx, cy) in enumerate(chips):
            q = 2 * cx + cy
            _remote(w_ref.at[half], o_ref.at[q, half], send.at[k], recv.at[k], (cx, cy, c)).wait_recv()
            fwd = _remote(o_ref.at[q, half], o_ref.at[q, half], send.at[3 + k], recv.at[3 + k], sib)
            fwd.start()
            sent.append(fwd)
        for k, (cx, cy) in enumerate(chips):
            q = 2 * cx + cy
            _remote(o_ref.at[q, other], o_ref.at[q, other], send.at[3 + k], recv.at[3 + k], sib).wait_recv()
        for cp in sent:
            cp.wait_send()

    return pl.pallas_call(
        body, name=name, in_specs=[ANY], out_specs=ANY,
        out_shape=jax.ShapeDtypeStruct((N_CHIPS, r, cols), wp.dtype),
        scratch_shapes=[pltpu.SemaphoreType.DMA((6,)), pltpu.SemaphoreType.DMA((6,))],
    )(wp)


HBM_SPEC = pl.BlockSpec(memory_space=pltpu.HBM)
SEM_SPEC = pl.BlockSpec(memory_space=pltpu.SEMAPHORE)
DATAFLOW = pltpu.SideEffectType.DATAFLOW_SIDE_EFFECTING


def _gather_plan(w_ref, land_ref):
    x, y, c = _me()
    return [(w_ref, land_ref.at[2 * x + y], (cx, cy, c)) for cx, cy in _other_chips(x, y)]


def _copies_start(src, land_shape, plan, n, name):
    def body(src_ref, land_ref, send, recv, src_thru, land_thru, token):
        for k, (s_view, d_view, dev) in enumerate(plan(src_ref, land_ref)):
            _remote(s_view, d_view, send.at[k], recv.at[k], dev).start()
        token[...] = jnp.zeros_like(token)

    return pl.pallas_call(
        body, name=name,
        out_shape=(pltpu.SemaphoreType.DMA((n,)), pltpu.SemaphoreType.DMA((n,)), pltpu.HBM(src.shape, src.dtype),
                   pltpu.HBM(land_shape, src.dtype), jax.ShapeDtypeStruct((8, LANES), F32)),
        in_specs=(HBM_SPEC, HBM_SPEC),
        out_specs=(SEM_SPEC, SEM_SPEC, HBM_SPEC, HBM_SPEC, pl.BlockSpec(memory_space=pltpu.VMEM)),
        input_output_aliases={0: 2, 1: 3},
        compiler_params=pltpu.CompilerParams(has_side_effects=DATAFLOW),
    )(pltpu.with_memory_space_constraint(src, pltpu.HBM),
      pltpu.with_memory_space_constraint(lax.empty(land_shape, src.dtype), pltpu.HBM))


def _copies_wait(started, after, plan, name):
    send, recv, src_thru, land_thru, _ = started

    def body(src_ref, land_ref, send, recv, after_ref, src_dead, land_out):
        for k, (s_view, d_view, dev) in enumerate(plan(src_ref, land_ref)):
            cp = _remote(s_view, d_view, send.at[k], recv.at[k], dev)
            cp.wait_send()
            cp.wait_recv()

    return pl.pallas_call(
        body, name=name,
        out_shape=(pltpu.HBM(src_thru.shape, src_thru.dtype), pltpu.HBM(land_thru.shape, land_thru.dtype)),
        in_specs=(HBM_SPEC, HBM_SPEC, SEM_SPEC, SEM_SPEC, ANY),
        out_specs=(HBM_SPEC, HBM_SPEC),
        input_output_aliases={0: 0, 1: 1},
        compiler_params=pltpu.CompilerParams(has_side_effects=DATAFLOW),
    )(src_thru, land_thru, send, recv, after)


def _swap_plan(g_ref, land_ref):
    x, y, c = _me()
    return [(g_ref.at[q, 1 - c], land_ref.at[q], (x, y, 1 - c)) for q in range(N_CHIPS)]


def _ici_plan(a_ref, land_ref):
    x, y, c = _me()
    return [(a_ref.at[2 * cx + cy], land_ref.at[2 * x + y], (cx, cy, c)) for cx, cy in _other_chips(x, y)]


def _share_plan(h_ref, land_ref):
    x, y, c = _me()
    return [(h_ref, land_ref, (x, y, 1 - c))]


def _exchange8(v, reduce, name):
    r, cols = v.shape
    rel = [(a, b, d) for a in (0, 1) for b in (0, 1) for d in (0, 1) if (a, b, d) != (0, 0, 0)]

    def body(v_ref, o_ref, *rest):
        if reduce:
            gbuf, send, recv = rest
        else:
            gbuf = o_ref
            send, recv = rest
        x, y, c = _me()
        me = 4 * x + 2 * y + c
        gbuf[me] = v_ref[...]
        sent = []
        for k, (a, b, d) in enumerate(rel):
            cp = _remote(v_ref, gbuf.at[me], send.at[k], recv.at[k], ((x + a) % 2, (y + b) % 2, (c + d) % 2))
            cp.start()
            sent.append(cp)
        for k, (a, b, d) in enumerate(rel):
            src = 4 * ((x + a) % 2) + 2 * ((y + b) % 2) + (c + d) % 2
            _remote(v_ref, gbuf.at[src], send.at[k], recv.at[k], (x, y, c)).wait_recv()
        for cp in sent:
            cp.wait_send()
        if reduce:
            acc = gbuf[0]
            for n in range(1, 8):
                acc = acc + gbuf[n]
            o_ref[...] = acc

    vmem = pl.BlockSpec(memory_space=pltpu.VMEM)
    sems = [pltpu.SemaphoreType.DMA((7,)), pltpu.SemaphoreType.DMA((7,))]
    if reduce:
        out_shape = jax.ShapeDtypeStruct((r, cols), F32)
        scratch = [pltpu.VMEM((8, r, cols), F32)] + sems
    else:
        out_shape = jax.ShapeDtypeStruct((8, r, cols), F32)
        scratch = sems
    return pl.pallas_call(body, name=name, in_specs=[vmem], out_specs=vmem, out_shape=out_shape,
                          scratch_shapes=scratch)(v)


def _rs_swap(g, name):
    _, _, rh, cols = g.shape

    def body(g_ref, o_ref, send, recv):
        x, y, c = _me()
        cps = [_remote(g_ref.at[q, 1 - c], o_ref.at[q], send.at[q], recv.at[q], (x, y, 1 - c)) for q in range(N_CHIPS)]
        for cp in cps:
            cp.start()
        for cp in cps:
            cp.wait()

    return pl.pallas_call(
        body, name=name, in_specs=[ANY], out_specs=ANY,
        out_shape=jax.ShapeDtypeStruct((N_CHIPS, rh, cols), F32),
        scratch_shapes=[pltpu.SemaphoreType.DMA((N_CHIPS,)), pltpu.SemaphoreType.DMA((N_CHIPS,))],
    )(g)


def _row_tile(rows, pref, mult=8):
    if rows <= pref:
        return rows
    t = (pref // mult) * mult
    while t >= mult:
        if rows % t == 0:
            return t
        t -= mult
    return rows


def _rs_add(g, sib, c, name):
    _, _, rh, cols = g.shape
    tr = _row_tile(rh, 512, 16)

    def body(c_ref, g_ref, s_ref, o_ref):
        o_ref[...] = (g_ref[...] + s_ref[...]).astype(BF16)

    grid_spec = pltpu.PrefetchScalarGridSpec(
        num_scalar_prefetch=1, grid=(N_CHIPS, rh // tr),
        in_specs=[pl.BlockSpec((None, None, tr, cols), lambda q, i, cr: (q, cr[0], i, 0)),
                  pl.BlockSpec((None, tr, cols), lambda q, i, cr: (q, i, 0))],
        out_specs=pl.BlockSpec((None, tr, cols), lambda q, i, cr: (q, i, 0)),
    )
    return pl.pallas_call(
        body, name=name, grid_spec=grid_spec,
        out_shape=jax.ShapeDtypeStruct((N_CHIPS, rh, cols), BF16),
        compiler_params=_cparams(("parallel", "parallel")),
    )(c, g, sib)


def _rs_ici(a, name):
    _, rh, cols = a.shape

    def body(a_ref, o_ref, send, recv):
        x, y, c = _me()
        p = 2 * x + y
        cps = []
        for k, (cx, cy) in enumerate(_other_chips(x, y)):
            cp = _remote(a_ref.at[2 * cx + cy], o_ref.at[p], send.at[k], recv.at[k], (cx, cy, c))
            cp.start()
            cps.append(cp)
        for k, (cx, cy) in enumerate(_other_chips(x, y)):
            q = 2 * cx + cy
            _remote(a_ref.at[q], o_ref.at[q], send.at[k], recv.at[k], (cx, cy, c)).wait_recv()
        for cp in cps:
            cp.wait_send()

    return pl.pallas_call(
        body, name=name, in_specs=[ANY], out_specs=ANY,
        out_shape=jax.ShapeDtypeStruct((N_CHIPS, rh, cols), a.dtype),
        scratch_shapes=[pltpu.SemaphoreType.DMA((3,)), pltpu.SemaphoreType.DMA((3,))],
    )(a)


def _rs_sum(rb, a, chip, name):
    _, rh, cols = rb.shape
    tr = _row_tile(rh, 512, 16)

    def body(p_ref, r0, r1, r2, r3, own, o_ref):
        p = p_ref[0]
        ownv = own[...].astype(F32)
        acc = None
        for q, r in enumerate((r0, r1, r2, r3)):
            v = jnp.where(p == q, ownv, r[...].astype(F32))
            acc = v if acc is None else acc + v
        o_ref[...] = acc

    def spec(q):
        return pl.BlockSpec((None, tr, cols), lambda i, pr: (jnp.where(pr[0] == q, (q + 1) % N_CHIPS, q), i, 0))

    grid_spec = pltpu.PrefetchScalarGridSpec(
        num_scalar_prefetch=1, grid=(rh // tr,),
        in_specs=[spec(0), spec(1), spec(2), spec(3), pl.BlockSpec((None, tr, cols), lambda i, pr: (pr[0], i, 0))],
        out_specs=pl.BlockSpec((tr, cols), lambda i, pr: (i, 0)),
    )
    return pl.pallas_call(
        body, name=name, grid_spec=grid_spec,
        out_shape=jax.ShapeDtypeStruct((rh, cols), F32),
        compiler_params=_cparams(("parallel",)),
    )(chip, rb, rb, rb, rb, a)


def _rs_share(hsum, name):
    rh, cols = hsum.shape

    def body(h_ref, o_ref, send, recv):
        x, y, c = _me()
        cp = _remote(h_ref, o_ref, send, recv, (x, y, 1 - c))
        cp.start()
        cp.wait()

    return pl.pallas_call(
        body, name=name, in_specs=[ANY], out_specs=ANY,
        out_shape=jax.ShapeDtypeStruct((rh, cols), F32),
        scratch_shapes=[pltpu.SemaphoreType.DMA, pltpu.SemaphoreType.DMA],
    )(hsum)


def _adamw(w, g, m, v, name):
    rows, cols = w.shape
    tr = _row_tile(rows, 512)

    def body(w_ref, g_ref, m_ref, v_ref, d_ref, nm_ref, nv_ref):
        gv = g_ref[...]
        nm = ADAM_B1 * m_ref[...] + (1.0 - ADAM_B1) * gv
        nv = ADAM_B2 * v_ref[...] + (1.0 - ADAM_B2) * (gv * gv)
        m_hat = nm / (1.0 - ADAM_B1 ** ADAM_STEP)
        v_hat = nv / (1.0 - ADAM_B2 ** ADAM_STEP)
        d_ref[...] = -ADAM_LR * (m_hat / (jnp.sqrt(v_hat) + ADAM_EPS) + ADAM_WD * w_ref[...])
        nm_ref[...] = nm
        nv_ref[...] = nv

    spec = pl.BlockSpec((tr, cols), lambda i: (i, 0))
    shp = jax.ShapeDtypeStruct((rows, cols), F32)
    return pl.pallas_call(
        body, name=name, grid=(rows // tr,), in_specs=[spec] * 4, out_specs=[spec] * 3, out_shape=[shp] * 3,
        compiler_params=_cparams(("parallel",)),
    )(w, g, m, v)


WEIGHTS = ("mix_norm_pre", "mix_norm_post", "ffn_norm_pre", "ffn_norm_post", "ev_w_in", "ev_a_conv_w", "ev_a_conv_b",
           "ev_a_ln_g", "ev_a_ln_b", "ev_sinks", "ev_w_out", "od_w_in", "od_conv_w", "od_w_out", "ffn_w_up",
           "ffn_conv_w", "ffn_w_down")
MATS = (("ev_w_in", 2), ("ev_w_out", 1), ("od_w_in", 2), ("od_w_out", 1), ("ffn_w_up", 2), ("ffn_w_down", 1))
UNITS = (("ev_w_in", 0, 2), ("ev_w_out", 0, 1), ("ffn_w_up", 0, 2), ("ffn_w_down", 0, 1),
         ("od_w_in", 0, 2), ("od_w_out", 0, 1), ("ffn_w_up", 1, 2), ("ffn_w_down", 1, 1))
GATHER_GROUPS = ((0, 1), (2, 3), (4, 5, 6, 7))
REDUCE_GROUPS = {"layer1": (4, 5, 6, 7), "ffn0": (2, 3), "ev": (0, 1)}
SMALL_SHARDED = ("ev_a_conv_w", "od_conv_w", "ffn_conv_w")
REPLICATED = ("mix_norm_pre", "mix_norm_post", "ffn_norm_pre", "ffn_norm_post", "ev_a_conv_b", "ev_a_ln_g",
              "ev_a_ln_b", "ev_sinks")


def _pack(parts, rows_multiple):
    flat = jnp.concatenate([p.reshape(-1) for p in parts])
    unit = rows_multiple * PACK_COLS
    pad = (-flat.shape[0]) % unit
    if pad:
        flat = jnp.concatenate([flat, jnp.zeros((pad,), flat.dtype)])
    return flat.reshape(-1, PACK_COLS)


def _unpack(buf, shapes):
    flat = buf.reshape(-1)
    out, off = [], 0
    for shp in shapes:
        n = 1
        for d in shp:
            n *= d
        out.append(flat[off:off + n].reshape(shp))
        off += n
    return out


def _shard_rows(shard, axis):
    if axis == 2:
        shard = jnp.swapaxes(shard, 1, 2)
    return shard.reshape(-1, PACK_COLS)


def kernel(x, positions, mix_norm_pre, mix_norm_post, ffn_norm_pre, ffn_norm_post, ev_w_in, ev_a_conv_w, ev_a_conv_b, ev_a_ln_g, ev_a_ln_b, ev_sinks, ev_w_out, od_w_in, od_conv_w, od_w_out, ffn_w_up, ffn_conv_w, ffn_w_down, loss_target, m_mix_norm_pre, m_mix_norm_post, m_ffn_norm_pre, m_ffn_norm_post, m_ev_w_in, m_ev_a_conv_w, m_ev_a_conv_b, m_ev_a_ln_g, m_ev_a_ln_b, m_ev_sinks, m_ev_w_out, m_od_w_in, m_od_conv_w, m_od_w_out, m_ffn_w_up, m_ffn_conv_w, m_ffn_w_down, v_mix_norm_pre, v_mix_norm_post, v_ffn_norm_pre, v_ffn_norm_post, v_ev_w_in, v_ev_a_conv_w, v_ev_a_conv_b, v_ev_a_ln_g, v_ev_a_ln_b, v_ev_sinks, v_ev_w_out, v_od_w_in, v_od_conv_w, v_od_w_out, v_ffn_w_up, v_ffn_conv_w, v_ffn_w_down):
    wts = dict(zip(WEIGHTS, (mix_norm_pre, mix_norm_post, ffn_norm_pre, ffn_norm_post, ev_w_in, ev_a_conv_w, ev_a_conv_b,
                             ev_a_ln_g, ev_a_ln_b, ev_sinks, ev_w_out, od_w_in, od_conv_w, od_w_out, ffn_w_up, ffn_conv_w,
                             ffn_w_down)))
    mom = dict(zip(WEIGHTS, (m_mix_norm_pre, m_mix_norm_post, m_ffn_norm_pre, m_ffn_norm_post, m_ev_w_in, m_ev_a_conv_w,
                             m_ev_a_conv_b, m_ev_a_ln_g, m_ev_a_ln_b, m_ev_sinks, m_ev_w_out, m_od_w_in, m_od_conv_w,
                             m_od_w_out, m_ffn_w_up, m_ffn_conv_w, m_ffn_w_down)))
    var = dict(zip(WEIGHTS, (v_mix_norm_pre, v_mix_norm_post, v_ffn_norm_pre, v_ffn_norm_post, v_ev_w_in, v_ev_a_conv_w,
                             v_ev_a_conv_b, v_ev_a_ln_g, v_ev_a_ln_b, v_ev_sinks, v_ev_w_out, v_od_w_in, v_od_conv_w,
                             v_od_w_out, v_ffn_w_up, v_ffn_conv_w, v_ffn_w_down)))
    xi, yi, ci = _me()
    chip = 2 * xi + yi

    unit_rows = [_shard_rows(wts[k][l:l + 1].astype(BF16), axis) for k, l, axis in UNITS]

    def group_block(group):
        return jnp.concatenate([unit_rows[u] for u in group], axis=0)

    def unpack_group(group, landed, own):
        full = lax.dynamic_update_slice(landed, own[None], (chip, 0, 0))
        out, off = {}, 0
        for u in group:
            k, l, axis = UNITS[u]
            n = unit_rows[u].shape[0]
            native = full[:, off:off + n].reshape(N_CHIPS * n, PACK_COLS)
            off += n
            key = (lambda name: (name, l)) if k.startswith("ffn") else (lambda name: name)
            out[key(k + "_t" if axis == 2 else k)] = native
            out[key(k if axis == 2 else k + "_t")] = native.T
        return out

    small_shapes = [wts[k].shape for k in SMALL_SHARDED]
    small_all = _exchange8(_pack([wts[k] for k in SMALL_SHARDED], 8), False, "gather_small")
    blocks = [group_block(grp) for grp in GATHER_GROUPS]
    first = _gather_chips(blocks[0], "gather_mats")
    later = {}
    for stage, grp, blk in zip(("ffn0", "layer1"), GATHER_GROUPS[1:], blocks[1:]):
        later[stage] = (grp, blk, _copies_start(blk, (N_CHIPS,) + blk.shape, _gather_plan, 3, "gather_" + stage + "_start"))

    def fetch(stage, after):
        grp, blk, started = later[stage]
        own, landed = _copies_wait(started, after, _gather_plan, "gather_" + stage + "_wait")
        return unpack_group(grp, landed, own)

    w = {k: wts[k] for k in REPLICATED}
    w.update(unpack_group(GATHER_GROUPS[0], first, blocks[0]))
    per_chip = [_unpack(small_all[2 * q], small_shapes) for q in range(N_CHIPS)]
    for n, k in enumerate(SMALL_SHARDED):
        w[k] = jnp.concatenate([per_chip[q][n] for q in range(N_CHIPS)], axis=-1)
    for k in ("ev_a_conv_w", "od_conv_w"):
        w[k] = w[k][0]
    w["ev_sinks"] = w["ev_sinks"][0]
    w["mix_norm_pre"] = w["mix_norm_pre"] + sum(later[s][2][4][0, 0] for s in later)

    core = jnp.reshape(ci, (1,)).astype(jnp.int32)
    chip_arr = jnp.reshape(chip, (1,)).astype(jnp.int32)
    per_layer = {}

    def group_grads(group, g):
        gp = jnp.concatenate([(g[k, l] if k.startswith("ffn") else g[k]).reshape(N_CHIPS, -1, PACK_COLS)
                              for k, l, _ in (UNITS[u] for u in group)], axis=1)
        return gp.reshape(N_CHIPS, 2, gp.shape[1] // 2, PACK_COLS)

    def finish(group, half, other):
        red = jnp.concatenate([jnp.where(ci == 0, half, other), jnp.where(ci == 0, other, half)], axis=0)
        off = 0
        for u in group:
            k, l, axis = UNITS[u]
            n = unit_rows[u].shape[0]
            part = red[off:off + n]
            off += n
            per_layer[k, l] = part.T if axis == 2 else part

    chains = {}

    def chain_step(tag, after, g):
        group = REDUCE_GROUPS[tag]
        st = chains.setdefault(tag, {"step": 0})
        step = st["step"]
        st["step"] = step + 1
        if step == 0:
            gp = group_grads(group, g)
            rh = gp.shape[2]
            st["swap"] = _copies_start(gp, (N_CHIPS, rh, PACK_COLS), _swap_plan, N_CHIPS, f"rs_{tag}_swap_start")
            return st["swap"][4][0, 0]
        if step == 1:
            gp, sib = _copies_wait(st["swap"], after, _swap_plan, f"rs_{tag}_swap_wait")
            pair = _rs_add(gp, sib, core, f"rs_{tag}_add")
            st["ici"] = _copies_start(pair, pair.shape, _ici_plan, 3, f"rs_{tag}_ici_start")
            return st["ici"][4][0, 0]
        if step == 2:
            pair, landed = _copies_wait(st["ici"], after, _ici_plan, f"rs_{tag}_ici_wait")
            half = _rs_sum(landed, pair, chip_arr, f"rs_{tag}_sum")
            st["share"] = _copies_start(half, half.shape, _share_plan, 1, f"rs_{tag}_share_start")
            return st["share"][4][0, 0]
        half, other = _copies_wait(st["share"], after, _share_plan, f"rs_{tag}_share_wait")
        finish(group, half, other)
        return 0.0

    schedule = {"layer1_grads": ("layer1",), "ffn0_bwd_done": ("layer1",), "ffn0_grads": ("ffn0",),
                "ev_out_bwd_done": ("layer1", "ffn0"), "ev_att_bwd_done": ("layer1", "ffn0"),
                "ev_mid_bwd_done": ("ffn0",)}

    def emit(place, after, g):
        return sum(chain_step(tag, after, g) for tag in schedule.get(place, ()))

    sq, dx, g = _local_step(x[0], positions[0], loss_target[0], w, fetch, emit)
    loss = lax.psum(0.5 * jnp.sum(sq) / D_MODEL, ("x", "y", "c"))

    gp = group_grads(REDUCE_GROUPS["ev"], g)
    sib = _rs_swap(gp, "rs_swap")
    pair = _rs_add(gp, sib, core, "rs_add")
    landed = _rs_ici(pair, "rs_ici")
    half = _rs_sum(landed, pair, chip_arr, "rs_sum")
    finish(REDUCE_GROUPS["ev"], half, _rs_share(half, "rs_share"))
    grads = {k: jnp.stack([per_layer[k, l] for l in range(wts[k].shape[0])], axis=0) for k, _ in MATS}

    small_keys = REPLICATED + SMALL_SHARDED
    full_shapes = [wts[k].shape for k in REPLICATED] + [wts[k].shape[:-1] + (wts[k].shape[-1] * N_CHIPS,) for k in SMALL_SHARDED]
    sm = _exchange8(_pack([g[k] for k in small_keys], 8), True, "reduce_small")
    for k, full in zip(small_keys, _unpack(sm, full_shapes)):
        if k in SMALL_SHARDED:
            n = wts[k].shape[-1]
            full = lax.dynamic_slice_in_dim(full, chip * n, n, axis=full.ndim - 1)
        grads[k] = full

    deltas, new_m, new_v = {}, {}, {}
    for k in WEIGHTS:
        shp = wts[k].shape
        two_d = (-1, shp[-1])
        d, nm, nv = _adamw(wts[k].reshape(two_d), grads[k].reshape(two_d), mom[k].reshape(two_d), var[k].reshape(two_d),
                           "adamw_" + k)
        deltas[k], new_m[k], new_v[k] = d.reshape(shp), nm.reshape(shp), nv.reshape(shp)

    return (loss, dx[None], *[grads[k] for k in WEIGHTS], *[deltas[k] for k in WEIGHTS],
            *[new_m[k] for k in WEIGHTS], *[new_v[k] for k in WEIGHTS])
```

```python
import functools

import jax
import jax.numpy as jnp
import numpy as np
from jax import lax
from jax.experimental import pallas as pl
from jax.experimental.pallas import tpu as pltpu

F32 = jnp.float32
BF16 = jnp.bfloat16
MESH = pl.DeviceIdType.MESH

D_MODEL = 1024
HEAD_DIM = 64
A_CH = 512
A_CONV = 31
N_Q_HEADS = 8
WINDOW = 128
ROPE_THETA = 500000.0
ROPE_DIM = 16
D_FF = 2816
RMS_EPS = 1e-6
LN_EPS = 1e-5
ADAM_LR = 0.001
ADAM_B1 = 0.9
ADAM_B2 = 0.999
ADAM_EPS = 1e-08
ADAM_WD = 0.01
ADAM_STEP = 10

LANES = 128
HALO16 = 16
HALO32 = 32
VMEM_LIMIT = 56 * 1024 * 1024
FFN_BWD_VMEM = 60 * 1024 * 1024
N_CHIPS = 4


def _cparams(sem):
    return pltpu.CompilerParams(dimension_semantics=sem, vmem_limit_bytes=VMEM_LIMIT)


def _tile(n, pref):
    if n <= pref:
        return n
    t = (pref // LANES) * LANES
    while t >= LANES:
        if n % t == 0:
            return t
        t -= LANES
    return n


MM_ROWS = 512


def _rms_scale(v):
    return lax.rsqrt(jnp.mean(v * v, axis=-1, keepdims=True) + RMS_EPS)


def _rms_bwd(dy, v, g):
    r = _rms_scale(v)
    nrm = v * r
    dn = dy * g
    return r * (dn - nrm * jnp.mean(dn * nrm, axis=-1, keepdims=True)), jnp.sum(dy * nrm, axis=0, keepdims=True)


def _mm_post(a, w, g, xres, name):
    s, k = a.shape
    d = w.shape[1]
    tm = min(MM_ROWS, s)

    def body(a_ref, w_ref, g_ref, x_ref, m_ref, o_ref):
        mv = jnp.dot(a_ref[...], w_ref[...], preferred_element_type=F32)
        m_ref[...] = mv
        o_ref[...] = x_ref[...] + mv * _rms_scale(mv) * g_ref[...]

    row = pl.BlockSpec((tm, d), lambda i: (i, 0))
    return pl.pallas_call(
        body, name=name, grid=(s // tm,),
        in_specs=[pl.BlockSpec((tm, k), lambda i: (i, 0)), _full((k, d)), _full((1, d)), row],
        out_specs=[row, row],
        out_shape=[jax.ShapeDtypeStruct((s, d), F32), jax.ShapeDtypeStruct((s, d), F32)],
        compiler_params=_cparams(("parallel",)),
    )(a, w, g, xres)


def _mm_post_bwd(dy, m, g, w_t, name):
    s, d = m.shape
    k = w_t.shape[1]
    tm = min(MM_ROWS, s)

    def body(dy_ref, m_ref, g_ref, wt_ref, dm_ref, da_ref, dg_ref):
        @pl.when(pl.program_id(0) == 0)
        def _():
            dg_ref[...] = jnp.zeros_like(dg_ref)

        dm, dg = _rms_bwd(dy_ref[...], m_ref[...], g_ref[...])
        dg_ref[...] += dg
        dmb = dm.astype(BF16)
        dm_ref[...] = dmb
        da_ref[...] = jnp.dot(dmb, wt_ref[...], preferred_element_type=F32)

    row = pl.BlockSpec((tm, d), lambda i: (i, 0))
    return pl.pallas_call(
        body, name=name, grid=(s // tm,),
        in_specs=[row, row, _full((1, d)), _full((d, k))],
        out_specs=[row, pl.BlockSpec((tm, k), lambda i: (i, 0)), _full((1, d))],
        out_shape=[jax.ShapeDtypeStruct((s, d), BF16), jax.ShapeDtypeStruct((s, k), F32),
                   jax.ShapeDtypeStruct((1, d), F32)],
        compiler_params=_cparams(("arbitrary",)),
    )(dy, m, g, w_t)


def _mm_tn(a, b, name):
    s, k = a.shape
    _, n = b.shape
    tk = _tile(k, 1408)
    tn = _tile(n, 1408)
    ts = min(2048, s)

    def body(a_ref, b_ref, o_ref):
        @pl.when(pl.program_id(2) == 0)
        def _():
            o_ref[...] = jnp.zeros_like(o_ref)

        o_ref[...] += lax.dot_general(a_ref[...], b_ref[...], (((0,), (0,)), ((), ())),
                                      preferred_element_type=F32)

    return pl.pallas_call(
        body, name=name, grid=(k // tk, n // tn, s // ts),
        in_specs=[pl.BlockSpec((ts, tk), lambda i, j, l: (l, i)), pl.BlockSpec((ts, tn), lambda i, j, l: (l, j))],
        out_specs=pl.BlockSpec((tk, tn), lambda i, j, l: (i, j)),
        out_shape=jax.ShapeDtypeStruct((k, n), F32),
        compiler_params=_cparams(("parallel", "parallel", "arbitrary")),
    )(a, b)


def _cur(tr, w, col=0):
    return pl.BlockSpec((tr, w), lambda i: (i, col))


def _prev(tr, h, w, col=0):
    return pl.BlockSpec((h, w), lambda i: (jnp.maximum(i * (tr // h) - 1, 0), col))


def _next(tr, h, w, nrows, col=0):
    last = nrows // h - 1
    return pl.BlockSpec((h, w), lambda i: (jnp.minimum((i + 1) * (tr // h), last), col))


def _full(shape):
    return pl.BlockSpec(shape, lambda i: tuple(0 for _ in shape))


def _silu_parts(g):
    sig = jax.nn.sigmoid(g)
    return sig, g * sig


FFN_CW = 256
FFN_NBUF = 3


def _conv3_taps(buf, w, off, rows):
    return (w[0:1] * buf[pl.ds(off, rows), :] + w[1:2] * buf[pl.ds(off + 1, rows), :]
            + w[2:3] * buf[pl.ds(off + 2, rows), :])


WHOLE_VMEM = pl.BlockSpec(memory_space=pltpu.VMEM)


def _ffn_fwd(x, g_pre, wu, conv_w, wd, g_post, name, target=None):
    s, d = x.shape
    f2 = wu.shape[1]
    f = f2 // 2
    tr = min(256, s)
    h = HALO16
    cw = FFN_CW
    head = target is not None

    def body(*refs):
        if head:
            (x_ref, gpre_ref, wu_ref, cw_ref, wd_ref, gpost_ref, t_ref, xo_ref, sq_ref, f_ref, h_ref, up_ref, u_ref,
             carry, gbuf, vbuf, facc) = refs
        else:
            (x_ref, gpre_ref, wu_ref, cw_ref, wd_ref, gpost_ref, xo_ref, f_ref, h_ref, up_ref, u_ref,
             carry, gbuf, vbuf, facc) = refs

        @pl.when(pl.program_id(0) == 0)
        def _():
            carry[...] = jnp.zeros_like(carry)
            if head:
                sq_ref[...] = jnp.zeros_like(sq_ref)

        xv = x_ref[...]
        r = lax.rsqrt(jnp.mean(xv * xv, axis=-1, keepdims=True) + RMS_EPS)
        hv = (xv * r * gpre_ref[...]).astype(BF16)
        h_ref[...] = hv
        nchunk = f // cw

        def up_proj(j):
            for buf, base in ((gbuf, 0), (vbuf, f)):
                cs = slice(base + j * cw, base + (j + 1) * cw)
                dst = buf.at[j % FFN_NBUF]
                upc = jnp.dot(hv, wu_ref[:, cs], preferred_element_type=F32)
                up_ref[:, cs] = upc.astype(BF16)
                dst[0:h, :] = carry[:, cs]
                dst[h:h + tr, :] = upc
                carry[:, cs] = upc[tr - h:tr, :]

        def down_proj(j, act):
            part = jnp.dot(act, wd_ref[j * cw:(j + 1) * cw, :], preferred_element_type=F32)
            if j == 0:
                facc[...] = part
            else:
                facc[...] += part

        for j in range(FFN_NBUF - 1):
            up_proj(j)
        pending = None
        for j in range(nchunk):
            cg = slice(j * cw, (j + 1) * cw)
            cv = slice(f + j * cw, f + (j + 1) * cw)
            if j + FFN_NBUF - 1 < nchunk:
                up_proj(j + FFN_NBUF - 1)
            if pending is not None:
                down_proj(*pending)
            g = _conv3_taps(gbuf.at[j % FFN_NBUF], cw_ref[:, cg], h - 2, tr)
            v = _conv3_taps(vbuf.at[j % FFN_NBUF], cw_ref[:, cv], h - 2, tr)
            u_ref[:, cg] = g.astype(BF16)
            u_ref[:, cv] = v.astype(BF16)
            act = (g * jax.nn.sigmoid(g) * v).astype(BF16)
            pending = (j, act)
        down_proj(*pending)
        fv = facc[...]
        f_ref[...] = fv
        r2 = lax.rsqrt(jnp.mean(fv * fv, axis=-1, keepdims=True) + RMS_EPS)
        xo = xv + fv * r2 * gpost_ref[...]
        if head:
            err = xo - t_ref[...]
            xo_ref[...] = err * (1.0 / d)
            sq_ref[...] += jnp.sum(err * err, axis=0, keepdims=True)
        else:
            xo_ref[...] = xo

    row = _cur(tr, d)
    wide = _cur(tr, f2)
    vec = _full((1, d))
    out_specs = [row] + ([vec] if head else []) + [row, row, wide, wide]
    out_shape = ([jax.ShapeDtypeStruct((s, d), F32)] + ([jax.ShapeDtypeStruct((1, d), F32)] if head else [])
                 + [jax.ShapeDtypeStruct((s, d), F32), jax.ShapeDtypeStruct((s, d), BF16),
                    jax.ShapeDtypeStruct((s, f2), BF16), jax.ShapeDtypeStruct((s, f2), BF16)])
    return pl.pallas_call(
        body, name=name, grid=(s // tr,),
        in_specs=[row, vec, WHOLE_VMEM, _full((3, f2)), WHOLE_VMEM, vec] + ([row] if head else []),
        out_specs=out_specs, out_shape=out_shape,
        scratch_shapes=[pltpu.VMEM((h, f2), F32), pltpu.VMEM((FFN_NBUF, h + tr, cw), F32),
                        pltpu.VMEM((FFN_NBUF, h + tr, cw), F32), pltpu.VMEM((tr, d), F32)],
        compiler_params=_cparams(("arbitrary",)),
    )(*((x, g_pre, wu, conv_w, wd, g_post) + ((target,) if head else ())))


def _ffn_bwd(dxo, fout, x, up, u, g_pre, g_post, wd_t, wu_t, conv_w, name):
    s, d = x.shape
    f2 = up.shape[1]
    f = f2 // 2
    tr = min(256, s)
    nt = s // tr
    h = HALO16
    cw = FFN_CW

    def body(dy_ref, f_ref, x_ref, up_ref, u_ref, gpre_ref, gpost_ref, wdt_ref, wut_ref, cw_ref,
             dx_ref, dup_ref, act_ref, df_ref, dcw_ref, dgpost_ref, dgpre_ref, carry, dgbuf, dvbuf, dhacc):
        @pl.when(pl.program_id(0) == 0)
        def _():
            carry[...] = jnp.zeros_like(carry)
            dcw_ref[...] = jnp.zeros_like(dcw_ref)
            dgpost_ref[...] = jnp.zeros_like(dgpost_ref)
            dgpre_ref[...] = jnp.zeros_like(dgpre_ref)

        dy = dy_ref[...]
        fv = f_ref[...]
        r = lax.rsqrt(jnp.mean(fv * fv, axis=-1, keepdims=True) + RMS_EPS)
        nrm = fv * r
        dn = dy * gpost_ref[...]
        dfv = (r * (dn - nrm * jnp.mean(dn * nrm, axis=-1, keepdims=True))).astype(BF16)
        dgpost_ref[...] += jnp.sum(dy * nrm, axis=0, keepdims=True)
        df_ref[...] = dfv
        nchunk = f // cw

        def dh_part(dupb, cs, first):
            part = jnp.dot(dupb, wut_ref[cs, :], preferred_element_type=F32)
            if first:
                dhacc[...] = part
            else:
                dhacc[...] += part

        def dact_of(j):
            return jnp.dot(dfv, wdt_ref[:, j * cw:(j + 1) * cw], preferred_element_type=F32)

        ahead = [dact_of(0)]
        for j in range(nchunk):
            ch = slice(j * cw, (j + 1) * cw)
            cg = ch
            cv = slice(f + j * cw, f + (j + 1) * cw)
            dact = ahead.pop(0)
            if j + 1 < nchunk:
                ahead.append(dact_of(j + 1))
            g = u_ref[:, cg].astype(F32)
            v = u_ref[:, cv].astype(F32)
            sig, sil = _silu_parts(g)
            act_ref[:, ch] = (sil * v).astype(BF16)
            du_g = dact * v * (sig * (1.0 + g * (1.0 - sig)))
            du_v = dact * sil
            for k, (dbuf, du, cs) in enumerate(((dgbuf.at[j % FFN_NBUF], du_g, cg), (dvbuf.at[j % FFN_NBUF], du_v, cv))):
                dbuf[0:tr, :] = du
                dbuf[tr:tr + h, :] = carry[:, cs]
                carry[:, cs] = du[0:h, :]
                w = cw_ref[:, cs]
                xin = up_ref[:, cs].astype(F32)
                acc = None
                for sh in range(3):
                    dsh = dbuf[pl.ds(sh, tr), :]
                    term = w[2 - sh:3 - sh] * dsh
                    acc = term if acc is None else acc + term
                    dcw_ref[2 - sh:3 - sh, cs] += jnp.sum(xin * dsh, axis=0, keepdims=True)
                dupb = acc.astype(BF16)
                dup_ref[:, cs] = dupb
                dh_part(dupb, cs, j == 0 and k == 0)
        dh = dhacc[...]
        xv = x_ref[...]
        r1 = lax.rsqrt(jnp.mean(xv * xv, axis=-1, keepdims=True) + RMS_EPS)
        n1 = xv * r1
        dn1 = dh * gpre_ref[...]
        dx_ref[...] = dy + r1 * (dn1 - n1 * jnp.mean(dn1 * n1, axis=-1, keepdims=True))
        dgpre_ref[...] += jnp.sum(dh * n1, axis=0, keepdims=True)

    def rev(w):
        return pl.BlockSpec((tr, w), lambda i: (nt - 1 - i, 0))

    vec = _full((1, d))
    return pl.pallas_call(
        body, name=name, grid=(nt,),
        in_specs=[rev(d), rev(d), rev(d), rev(f2), rev(f2), vec, vec, WHOLE_VMEM, WHOLE_VMEM, _full((3, f2))],
        out_specs=[rev(d), rev(f2), rev(f), rev(d), _full((3, f2)), vec, vec],
        out_shape=[jax.ShapeDtypeStruct((s, d), F32), jax.ShapeDtypeStruct((s, f2), BF16),
                   jax.ShapeDtypeStruct((s, f), BF16), jax.ShapeDtypeStruct((s, d), BF16),
                   jax.ShapeDtypeStruct((3, f2), F32), jax.ShapeDtypeStruct((1, d), F32),
                   jax.ShapeDtypeStruct((1, d), F32)],
        scratch_shapes=[pltpu.VMEM((h, f2), F32), pltpu.VMEM((FFN_NBUF, tr + h, cw), F32),
                        pltpu.VMEM((FFN_NBUF, tr + h, cw), F32), pltpu.VMEM((tr, d), F32)],
        compiler_params=pltpu.CompilerParams(dimension_semantics=("arbitrary",), vmem_limit_bytes=FFN_BWD_VMEM),
    )(dxo, fout, x, up, u, g_pre, g_post, wd_t, wu_t, conv_w)


def _od_in_fwd(x, g, w, conv_w, name):
    s, d = x.shape
    d3 = w.shape[1]
    tr = min(256, s)
    h = HALO16
    cw = FFN_CW
    nchunk = d // cw

    def body(x_ref, g_ref, w_ref, cw_ref, h_ref, z_ref, y_ref, carry, buf):
        @pl.when(pl.program_id(0) == 0)
        def _():
            carry[...] = jnp.zeros_like(carry)

        xv = x_ref[...]
        hv = (xv * _rms_scale(xv) * g_ref[...]).astype(BF16)
        h_ref[...] = hv

        def project(j):
            out = []
            for part in range(3):
                cs = slice(part * d + j * cw, part * d + (j + 1) * cw)
                zc = jnp.dot(hv, w_ref[:, cs], preferred_element_type=F32).astype(BF16)
                z_ref[:, cs] = zc
                out.append(zc.astype(F32))
            return out

        ahead = [project(0), project(1)]
        for j in range(nchunk):
            cb = slice(j * cw, (j + 1) * cw)
            bval, cval, uval = ahead.pop(0)
            if j + 2 < nchunk:
                ahead.append(project(j + 2))
            bf = buf.at[j % FFN_NBUF]
            cu = cval * uval
            bf[0:h, :] = carry[:, cb]
            bf[h:h + tr, :] = cu
            carry[:, cb] = cu[tr - h:tr, :]
            y_ref[:, cb] = (bval * _conv3_taps(bf, cw_ref[:, cb], h - 2, tr)).astype(BF16)

    row = _cur(tr, d)
    return pl.pallas_call(
        body, name=name, grid=(s // tr,),
        in_specs=[row, _full((1, d)), WHOLE_VMEM, _full((3, d))],
        out_specs=[row, _cur(tr, d3), row],
        out_shape=[jax.ShapeDtypeStruct((s, d), BF16), jax.ShapeDtypeStruct((s, d3), BF16),
                   jax.ShapeDtypeStruct((s, d), BF16)],
        scratch_shapes=[pltpu.VMEM((h, d), F32), pltpu.VMEM((FFN_NBUF, h + tr, cw), F32)],
        compiler_params=_cparams(("arbitrary",)),
    )(x, g, w, conv_w)


def _od_in_bwd(dy, z, conv_w, w_t, x, g, res, name):
    s, d3 = z.shape
    d = d3 // 3
    tr = min(256, s)
    h = HALO16
    cw = FFN_CW
    ext = tr + h

    def body(dy_ref, dyn_ref, z_ref, zp_ref, zn_ref, w_ref, wt_ref, x_ref, g_ref, res_ref,
             o_ref, dw_ref, dx_ref, dg_ref, buf, dbuf, dhacc):
        i = pl.program_id(0)
        first = i == 0
        last = i == pl.num_programs(0) - 1

        @pl.when(first)
        def _():
            dw_ref[...] = jnp.zeros_like(dw_ref)
            dg_ref[...] = jnp.zeros_like(dg_ref)

        started = False
        for j in range(d // cw):
            cb = slice(j * cw, (j + 1) * cw)
            cc = slice(d + j * cw, d + (j + 1) * cw)
            cu = slice(2 * d + j * cw, 2 * d + (j + 1) * cw)
            bf = buf.at[j % FFN_NBUF]
            db = dbuf.at[j % FFN_NBUF]
            w = w_ref[:, cb]
            cval = z_ref[:, cc].astype(F32)
            uval = z_ref[:, cu].astype(F32)
            bf[0:h, :] = jnp.where(first, 0.0, zp_ref[:, cc].astype(F32) * zp_ref[:, cu].astype(F32))
            bf[h:h + tr, :] = cval * uval
            k = _conv3_taps(bf, w, h - 2, tr)
            dyv = dy_ref[:, cb]
            db[0:tr, :] = dyv * z_ref[:, cb].astype(F32)
            db[tr:ext, :] = jnp.where(last, 0.0, dyn_ref[:, cb] * zn_ref[:, cb].astype(F32))
            dcu = w[2:3] * db[pl.ds(0, tr), :] + w[1:2] * db[pl.ds(1, tr), :] + w[0:1] * db[pl.ds(2, tr), :]
            dk = db[pl.ds(0, tr), :]
            for t in range(3):
                dw_ref[t:t + 1, cb] += jnp.sum(dk * bf[pl.ds(h - 2 + t, tr), :], axis=0, keepdims=True)
            for cs, val in ((cb, dyv * k), (cc, dcu * uval), (cu, dcu * cval)):
                piece = val.astype(BF16)
                o_ref[:, cs] = piece
                part = jnp.dot(piece, wt_ref[cs, :], preferred_element_type=F32)
                if started:
                    dhacc[...] += part
                else:
                    dhacc[...] = part
                    started = True
        dx, dg = _rms_bwd(dhacc[...], x_ref[...], g_ref[...])
        dg_ref[...] += dg
        dx_ref[...] = res_ref[...] + dx

    row = _cur(tr, d)
    vec = _full((1, d))
    return pl.pallas_call(
        body, name=name, grid=(s // tr,),
        in_specs=[row, _next(tr, h, d, s), _cur(tr, d3), _prev(tr, h, d3), _next(tr, h, d3, s), _full((3, d)),
                  WHOLE_VMEM, row, vec, row],
        out_specs=[_cur(tr, d3), _full((3, d)), row, vec],
        out_shape=[jax.ShapeDtypeStruct((s, d3), BF16), jax.ShapeDtypeStruct((3, d), F32),
                   jax.ShapeDtypeStruct((s, d), F32), jax.ShapeDtypeStruct((1, d), F32)],
        scratch_shapes=[pltpu.VMEM((FFN_NBUF, h + tr, cw), F32), pltpu.VMEM((FFN_NBUF, ext, cw), F32),
                        pltpu.VMEM((tr, d), F32)],
        compiler_params=_cparams(("arbitrary",)),
    )(dy, dy, z, z, z, conv_w, w_t, x, g, res)


Q0 = 2 * A_CH
K0 = Q0 + N_Q_HEADS * HEAD_DIM
V0 = K0 + 2 * HEAD_DIM
EVEN_IN = V0 + 2 * HEAD_DIM


def _rope_tables(positions):
    half = ROPE_DIM // 2
    inv_freq = ROPE_THETA ** (-(jnp.arange(half, dtype=F32) * 2.0 / ROPE_DIM))
    ang = positions.astype(F32)[:, None] * inv_freq
    cs = jnp.concatenate([jnp.cos(ang), jnp.sin(ang)], axis=1)
    spread = np.zeros((2 * half, 3 * LANES), np.float32)
    const = np.zeros((1, 3 * LANES), np.float32)
    for lane in range(3 * LANES):
        dim, part = lane % HEAD_DIM, lane // LANES
        if part == 0:
            if dim < ROPE_DIM:
                spread[dim % half, lane] = 1.0
            else:
                const[0, lane] = 1.0
        elif part == 1 and half <= dim < ROPE_DIM:
            spread[half + dim - half, lane] = 1.0
        elif part == 2 and dim < half:
            spread[half + dim, lane] = -1.0
    return jnp.dot(cs, jnp.asarray(spread), precision=lax.Precision.HIGHEST) + jnp.asarray(const)


def _rope_fwd(x, tab):
    c, sa, sb = tab[:, 0:LANES], tab[:, LANES:2 * LANES], tab[:, 2 * LANES:3 * LANES]
    return x * c + pltpu.roll(x, 8, 1) * sa + pltpu.roll(x, LANES - 8, 1) * sb


def _rope_bwd(dy, tab):
    c, sa, sb = tab[:, 0:LANES], tab[:, LANES:2 * LANES], tab[:, 2 * LANES:3 * LANES]
    return dy * c + pltpu.roll(dy * sa, LANES - 8, 1) + pltpu.roll(dy * sb, 8, 1)


def _ln_fwd(c, g, b):
    mu = jnp.mean(c, axis=-1, keepdims=True)
    xc = c - mu
    r = lax.rsqrt(jnp.mean(xc * xc, axis=-1, keepdims=True) + LN_EPS)
    nrm = xc * r
    return nrm, r, nrm * g + b


def _phase_fill(buf, ph, rows):
    for k in range(1, 8):
        ph[k - 1, 0:rows - 8, :] = buf[pl.ds(k, rows - 8), :]


def _phase_rows(buf, ph, off, n, cs):
    k = off % 8
    src = buf if k == 0 else ph.at[k - 1]
    return src[pl.ds(off - k, n), cs]


def _ev_in_fwd(x, g_pre, w_in, tab, conv_w, conv_b, ln_g, ln_b, name):
    s, d = x.shape
    tr = min(256, s)
    h = HALO32
    cw = LANES
    pw = 2 * LANES

    def body(x_ref, gpre_ref, win_ref, tab_ref, w_ref, b_ref, g_ref, lb_ref, h_ref, z_ref, c_ref, a_ref, qkv_ref,
             gbuf, cbuf, gph, carry):
        @pl.when(pl.program_id(0) == 0)
        def _():
            carry[...] = jnp.zeros_like(carry)

        xv = x_ref[...]
        hv = (xv * _rms_scale(xv) * gpre_ref[...]).astype(BF16)
        h_ref[...] = hv

        def project(lo_col, hi_col):
            for c0 in range(lo_col, hi_col, pw):
                cs = slice(c0, c0 + pw)
                z_ref[:, cs] = jnp.dot(hv, win_ref[:, cs], preferred_element_type=F32).astype(BF16)

        project(0, 2 * A_CH)
        glu = z_ref[:, 0:A_CH].astype(F32) * jax.nn.sigmoid(z_ref[:, A_CH:2 * A_CH].astype(F32))
        project(2 * A_CH, EVEN_IN)
        gbuf[0:h, :] = carry[...]
        gbuf[h:h + tr, :] = glu
        carry[...] = glu[tr - h:tr, :]
        _phase_fill(gbuf, gph, h + tr)
        for j in range(A_CH // cw):
            cs = slice(j * cw, (j + 1) * cw)
            acc = jnp.broadcast_to(b_ref[:, cs], (tr, cw))
            for t in range(A_CONV):
                acc = acc + w_ref[t:t + 1, cs] * _phase_rows(gbuf, gph, h - (A_CONV - 1) + t, tr, cs)
            cbuf[:, cs] = acc
        c = cbuf[...]
        c_ref[...] = c.astype(BF16)
        _, _, l = _ln_fwd(c, g_ref[...], lb_ref[...])
        a_ref[...] = (l * jax.nn.sigmoid(l)).astype(BF16)
        tab_v = tab_ref[...]
        for p in range(4):
            xq = z_ref[:, Q0 + p * LANES:Q0 + (p + 1) * LANES].astype(F32)
            qkv_ref[:, p * LANES:(p + 1) * LANES] = _rope_fwd(xq, tab_v).astype(BF16)
        lane = lax.broadcasted_iota(jnp.int32, (tr, LANES), 1)
        lo = lane < HEAD_DIM
        kr = _rope_fwd(z_ref[:, K0:K0 + LANES].astype(F32), tab_v)
        vr = z_ref[:, V0:V0 + LANES].astype(F32)
        for base, val in ((4 * LANES, kr), (6 * LANES, vr)):
            sw = pltpu.roll(val, HEAD_DIM, 1)
            qkv_ref[:, base:base + LANES] = jnp.where(lo, val, sw).astype(BF16)
            qkv_ref[:, base + LANES:base + 2 * LANES] = jnp.where(lo, sw, val).astype(BF16)

    return pl.pallas_call(
        body, name=name, grid=(s // tr,),
        in_specs=[_cur(tr, d), _full((1, d)), WHOLE_VMEM, _cur(tr, 3 * LANES), _full((A_CONV, A_CH)),
                  _full((1, A_CH)), _full((1, A_CH)), _full((1, A_CH))],
        out_specs=[_cur(tr, d), _cur(tr, EVEN_IN), _cur(tr, A_CH), _cur(tr, A_CH), _cur(tr, 2 * A_CH)],
        out_shape=[jax.ShapeDtypeStruct((s, d), BF16), jax.ShapeDtypeStruct((s, EVEN_IN), BF16),
                   jax.ShapeDtypeStruct((s, A_CH), BF16), jax.ShapeDtypeStruct((s, A_CH), BF16),
                   jax.ShapeDtypeStruct((s, 2 * A_CH), BF16)],
        scratch_shapes=[pltpu.VMEM((h + tr, A_CH), F32), pltpu.VMEM((tr, A_CH), F32),
                        pltpu.VMEM((7, h + tr, A_CH), F32), pltpu.VMEM((h, A_CH), F32)],
        compiler_params=_cparams(("arbitrary",)),
    )(x, g_pre, w_in, tab, conv_w, conv_b, ln_g, ln_b)


def _ev_mid_bwd(dcat, c, z, dq, dkv, tab, conv_w, ln_g, ln_b, w_t, x, g_pre, res, name):
    s = z.shape[0]
    tr = min(256, s)
    h = HALO32
    cw = LANES
    ext = tr + h

    def body(da_ref, dan_ref, c_ref, cn_ref, z_ref, dq_ref, dkv_ref, tab_ref, w_ref, g_ref, lb_ref,
             wt_ref, x_ref, gpre_ref, res_ref, dz_ref, dw_ref, dvec_ref, dx_ref, dg_ref, dcbuf, dcph, dhacc):
        i = pl.program_id(0)
        first = i == 0
        last = i == pl.num_programs(0) - 1

        @pl.when(first)
        def _():
            dw_ref[...] = jnp.zeros_like(dw_ref)
            dvec_ref[...] = jnp.zeros_like(dvec_ref)
            dg_ref[...] = jnp.zeros_like(dg_ref)

        started = []

        def dh_part(cs):
            part = jnp.dot(dz_ref[:, cs], wt_ref[cs, :], preferred_element_type=F32)
            if started:
                dhacc[...] += part
            else:
                dhacc[...] = part
                started.append(True)

        tab_v = tab_ref[...]
        for p in range(4):
            cs = slice(p * LANES, (p + 1) * LANES)
            dz_ref[:, Q0 + p * LANES:Q0 + (p + 1) * LANES] = _rope_bwd(dq_ref[:, cs], tab_v).astype(BF16)
        lane = lax.broadcasted_iota(jnp.int32, (tr, LANES), 1)
        lo = lane < HEAD_DIM

        def fold(base):
            p0 = dkv_ref[:, base:base + LANES]
            p1 = dkv_ref[:, base + LANES:base + 2 * LANES]
            s0 = p0 + pltpu.roll(p0, HEAD_DIM, 1)
            s1 = p1 + pltpu.roll(p1, HEAD_DIM, 1)
            return jnp.where(lo, s0, s1)

        dz_ref[:, K0:K0 + LANES] = _rope_bwd(fold(0), tab_v).astype(BF16)
        dz_ref[:, V0:V0 + LANES] = fold(2 * LANES).astype(BF16)
        dh_part(slice(Q0, EVEN_IN))

        gv = g_ref[...]

        def ln_silu_bwd(cv, dav):
            nrm, r, l = _ln_fwd(cv, gv, lb_ref[...])
            sig = jax.nn.sigmoid(l)
            dl = dav * (sig * (1.0 + l * (1.0 - sig)))
            dn = dl * gv
            dc = r * (dn - jnp.mean(dn, axis=-1, keepdims=True) - nrm * jnp.mean(dn * nrm, axis=-1, keepdims=True))
            return dc, dl, nrm

        dc, dl, nrm = ln_silu_bwd(c_ref[...].astype(F32), da_ref[...])
        dcn, _, _ = ln_silu_bwd(cn_ref[...].astype(F32), dan_ref[...])
        dcbuf[0:tr, :] = dc
        dcbuf[tr:ext, :] = jnp.where(last, 0.0, dcn)
        dvec_ref[0:1, :] += jnp.sum(dc, axis=0, keepdims=True)
        dvec_ref[1:2, :] += jnp.sum(dl * nrm, axis=0, keepdims=True)
        dvec_ref[2:3, :] += jnp.sum(dl, axis=0, keepdims=True)

        _phase_fill(dcbuf, dcph, ext)
        a_lin = z_ref[:, 0:A_CH].astype(F32)
        sig_g = jax.nn.sigmoid(z_ref[:, A_CH:2 * A_CH].astype(F32))
        glu = a_lin * sig_g
        for j in range(A_CH // cw):
            cs = slice(j * cw, (j + 1) * cw)
            gluj = glu[:, cs]
            acc = jnp.zeros((tr, cw), F32)
            for t in range(A_CONV):
                dsh = _phase_rows(dcbuf, dcph, A_CONV - 1 - t, tr, cs)
                acc = acc + w_ref[t:t + 1, cs] * dsh
                dw_ref[t:t + 1, cs] += jnp.sum(gluj * dsh, axis=0, keepdims=True)
            dz_ref[:, cs] = (acc * sig_g[:, cs]).astype(BF16)
            dz_ref[:, A_CH + j * cw:A_CH + (j + 1) * cw] = (
                acc * a_lin[:, cs] * sig_g[:, cs] * (1.0 - sig_g[:, cs])).astype(BF16)
            if j % 2 == 1:
                dh_part(slice((j - 1) * cw, (j + 1) * cw))
                dh_part(slice(A_CH + (j - 1) * cw, A_CH + (j + 1) * cw))

        dx, dg = _rms_bwd(dhacc[...], x_ref[...], gpre_ref[...])
        dg_ref[...] += dg
        dx_ref[...] = res_ref[...] + dx

    row = _cur(tr, D_MODEL)
    vec = _full((1, D_MODEL))
    return pl.pallas_call(
        body, name=name, grid=(s // tr,),
        in_specs=[_cur(tr, A_CH), _next(tr, h, A_CH, s), _cur(tr, A_CH), _next(tr, h, A_CH, s),
                  _cur(tr, EVEN_IN), _cur(tr, A_CH), _cur(tr, A_CH), _cur(tr, 3 * LANES),
                  _full((A_CONV, A_CH)), _full((1, A_CH)), _full((1, A_CH)), WHOLE_VMEM, row, vec, row],
        out_specs=[_cur(tr, EVEN_IN), _full((A_CONV, A_CH)), _full((8, A_CH)), row, vec],
        out_shape=[jax.ShapeDtypeStruct((s, EVEN_IN), BF16), jax.ShapeDtypeStruct((A_CONV, A_CH), F32),
                   jax.ShapeDtypeStruct((8, A_CH), F32), jax.ShapeDtypeStruct((s, D_MODEL), F32),
                   jax.ShapeDtypeStruct((1, D_MODEL), F32)],
        scratch_shapes=[pltpu.VMEM((ext, A_CH), F32), pltpu.VMEM((7, ext, A_CH), F32),
                        pltpu.VMEM((tr, D_MODEL), F32)],
        compiler_params=_cparams(("arbitrary",)),
    )(dcat, dcat, c, c, z, dq, dkv, tab, conv_w, ln_g, ln_b, w_t, x, g_pre, res)


NT = (((1,), (1,)), ((), ()))
TN = (((0,), (0,)), ((), ()))
QB = WINDOW
SCALE = HEAD_DIM ** -0.5
ATT_AHEAD = 2


def _att_scores(q2m, kwin):
    return lax.dot_general(q2m, kwin, NT, preferred_element_type=F32)


def _att_probs(raw, sink, mask):
    sc = jnp.where(mask, raw * SCALE, -jnp.inf)
    mx = jnp.maximum(jnp.max(sc, axis=-1, keepdims=True), sink)
    p = jnp.exp(sc - mx)
    ps = jnp.exp(sink - mx)
    inv = 1.0 / (jnp.sum(p, axis=-1, keepdims=True) + ps)
    return p * inv, ps * inv


def _att_mask(i):
    r = lax.broadcasted_iota(jnp.int32, (QB, 2 * QB), 0)
    kc = lax.broadcasted_iota(jnp.int32, (QB, 2 * QB), 1)
    diff = r + QB - kc
    return (diff >= 0) & (diff < WINDOW) & ((kc >= QB) | (i > 0))


def _half_masks(dtype):
    lane = lax.broadcasted_iota(jnp.int32, (1, LANES), 1)
    return (lane < HEAD_DIM).astype(dtype), (lane >= HEAD_DIM).astype(dtype)


def _att_fwd(qkv, a, sinks, name):
    s = qkv.shape[0]
    nb = s // QB

    def body(sink_ref, qkv_ref, kvp_ref, a_ref, o_ref):
        i = pl.program_id(0)
        mask = _att_mask(i)
        mlo, mhi = _half_masks(BF16)
        o_ref[:, 0:A_CH] = a_ref[...]

        def window(col):
            return jnp.concatenate([kvp_ref[:, col * LANES:(col + 1) * LANES],
                                    qkv_ref[:, A_CH + col * LANES:A_CH + (col + 1) * LANES]], axis=0)

        def raw_scores(p):
            q2 = qkv_ref[:, p * LANES:(p + 1) * LANES]
            kwin = window(p // 2)
            return _att_scores(q2 * mlo, kwin), _att_scores(q2 * mhi, kwin)

        ahead = [raw_scores(p) for p in range(4)]
        for p in range(4):
            raw_e, raw_o = ahead[p]
            vwin = window(2 + p // 2)
            pe, _ = _att_probs(raw_e, sink_ref[2 * p], mask)
            po, _ = _att_probs(raw_o, sink_ref[2 * p + 1], mask)
            o = (jnp.dot(pe.astype(BF16), vwin * mlo, preferred_element_type=F32)
                 + jnp.dot(po.astype(BF16), vwin * mhi, preferred_element_type=F32))
            o_ref[:, A_CH + p * LANES:A_CH + (p + 1) * LANES] = o.astype(BF16)

    grid_spec = pltpu.PrefetchScalarGridSpec(
        num_scalar_prefetch=1, grid=(nb,),
        in_specs=[pl.BlockSpec((QB, 2 * A_CH), lambda i, sk: (i, 0)),
                  pl.BlockSpec((QB, A_CH), lambda i, sk: (jnp.maximum(i - 1, 0), 1)),
                  pl.BlockSpec((QB, A_CH), lambda i, sk: (i, 0))],
        out_specs=pl.BlockSpec((QB, 2 * A_CH), lambda i, sk: (i, 0)),
    )
    return pl.pallas_call(
        body, name=name, grid_spec=grid_spec,
        out_shape=jax.ShapeDtypeStruct((s, 2 * A_CH), BF16),
        compiler_params=_cparams(("parallel",)),
    )(sinks, qkv, qkv, a)


def _att_bwd(qkv, dcat, sinks, name):
    s = qkv.shape[0]
    nb = s // QB

    def body(sink_ref, qkv_ref, kvp_ref, do_ref, dq_ref, dkv_ref, ds_ref, carry):
        i = pl.program_id(0)

        @pl.when(i == 0)
        def _():
            ds_ref[...] = jnp.zeros_like(ds_ref)
            carry[...] = jnp.zeros_like(carry)

        @pl.when(i < nb)
        def _():
            mask = _att_mask(i)
            mlo, mhi = _half_masks(BF16)
            dwin = [jnp.zeros((2 * QB, LANES), F32) for _ in range(4)]

            def window(col):
                return jnp.concatenate([kvp_ref[:, col * LANES:(col + 1) * LANES],
                                        qkv_ref[:, A_CH + col * LANES:A_CH + (col + 1) * LANES]], axis=0)

            def first_products(n):
                p, hm = n // 2, (mlo, mhi)[n % 2]
                qm = qkv_ref[:, p * LANES:(p + 1) * LANES] * hm
                dom = do_ref[:, p * LANES:(p + 1) * LANES].astype(BF16) * hm
                kwin, vwin = window(p // 2), window(2 + p // 2)
                return (qm, dom, kwin * hm, _att_scores(qm, kwin),
                        lax.dot_general(dom, vwin, NT, preferred_element_type=F32))

            ahead = [first_products(n) for n in range(ATT_AHEAD)]
            dq2 = None
            for n in range(N_Q_HEADS):
                g = n // 4
                qm, dom, kwm, raw, dp = ahead.pop(0)
                if n + ATT_AHEAD < N_Q_HEADS:
                    ahead.append(first_products(n + ATT_AHEAD))
                prob, psink = _att_probs(raw, sink_ref[n], mask)
                delta = jnp.sum(prob * dp, axis=-1, keepdims=True)
                dsc = (prob * (dp - delta) * SCALE).astype(BF16)
                ds_ref[n:n + 1, :] += jnp.broadcast_to(jnp.sum(-psink * delta, axis=0, keepdims=True), (1, LANES))
                part = jnp.dot(dsc, kwm, preferred_element_type=F32)
                dq2 = part if n % 2 == 0 else dq2 + part
                dwin[g] = dwin[g] + lax.dot_general(dsc, qm, TN, preferred_element_type=F32)
                dwin[2 + g] = dwin[2 + g] + lax.dot_general(prob.astype(BF16), dom, TN, preferred_element_type=F32)
                if n % 2 == 1:
                    dq_ref[:, (n // 2) * LANES:(n // 2 + 1) * LANES] = dq2
            for n in range(4):
                cs = slice(n * LANES, (n + 1) * LANES)
                dkv_ref[:, cs] = carry[:, cs] + dwin[n][0:QB, :]
                carry[:, cs] = dwin[n][QB:2 * QB, :]

        @pl.when(i == nb)
        def _():
            dkv_ref[...] = carry[...]

    grid_spec = pltpu.PrefetchScalarGridSpec(
        num_scalar_prefetch=1, grid=(nb + 1,),
        in_specs=[pl.BlockSpec((QB, 2 * A_CH), lambda i, sk: (jnp.minimum(i, nb - 1), 0)),
                  pl.BlockSpec((QB, A_CH), lambda i, sk: (jnp.maximum(jnp.minimum(i, nb - 1) - 1, 0), 1)),
                  pl.BlockSpec((QB, A_CH), lambda i, sk: (jnp.minimum(i, nb - 1), 1))],
        out_specs=[pl.BlockSpec((QB, A_CH), lambda i, sk: (jnp.minimum(i, nb - 1), 0)),
                   pl.BlockSpec((QB, A_CH), lambda i, sk: (jnp.maximum(i - 1, 0), 0)),
                   pl.BlockSpec((8, LANES), lambda i, sk: (0, 0))],
        scratch_shapes=[pltpu.VMEM((QB, A_CH), F32)],
    )
    return pl.pallas_call(
        body, name=name, grid_spec=grid_spec,
        out_shape=[jax.ShapeDtypeStruct((s, A_CH), F32), jax.ShapeDtypeStruct((s, A_CH), F32),
                   jax.ShapeDtypeStruct((8, LANES), F32)],
        compiler_params=_cparams(("arbitrary",)),
    )(sinks, qkv, qkv, dcat)


def _local_step(x, positions, target, w, fetch=None, emit=None):
    row = lambda a, i: a[i:i + 1]
    tab = _rope_tables(positions)
    g = {}

    def ffn_fwd(xin, i, tgt=None):
        outs = _ffn_fwd(xin, row(w["ffn_norm_pre"], i), w["ffn_w_up", i], w["ffn_conv_w"][i],
                        w["ffn_w_down", i], row(w["ffn_norm_post"], i), f"ffn{i}_fwd", tgt)
        f, h, up, u = outs[-4:]
        return outs[:-4], (xin, f, h, up, u)

    def point(name, after):
        return emit(name, after, g) if emit is not None else 0.0

    def ffn_bwd(dxout, saved, i, tok):
        xin, f, h, up, u = saved
        dxin, dup, act, df, d_cw, dg_post, dg_pre = _ffn_bwd(
            dxout, f, xin, up, u, row(w["ffn_norm_pre"], i), row(w["ffn_norm_post"], i) + tok, w["ffn_w_down_t", i],
            w["ffn_w_up_t", i], w["ffn_conv_w"][i], f"ffn{i}_bwd")
        tok = point(f"ffn{i}_bwd_done", dxin)
        g["ffn_w_down", i] = _mm_tn(act, df, f"ffn{i}_down_dw")
        g["ffn_w_up", i] = _mm_tn(dup, h, f"ffn{i}_up_dw")
        return dxin, tok, dict(ffn_norm_post=dg_post, ffn_norm_pre=dg_pre, ffn_conv_w=d_cw)

    h0, z0, c0, a0, qkv = _ev_in_fwd(x, row(w["mix_norm_pre"], 0), w["ev_w_in"], tab, w["ev_a_conv_w"],
                                     w["ev_a_conv_b"], w["ev_a_ln_g"], w["ev_a_ln_b"], "ev_in")
    cat = _att_fwd(qkv, a0, w["ev_sinks"], "ev_att")
    m0, x1 = _mm_post(cat, w["ev_w_out"], row(w["mix_norm_post"], 0), x, "ev_out")
    if fetch is not None:
        w = {**w, **fetch("ffn0", x1)}
    (x2,), ffn0 = ffn_fwd(x1, 0)
    if fetch is not None:
        w = {**w, **fetch("layer1", x2)}
    h2, z1, y1 = _od_in_fwd(x2, row(w["mix_norm_pre"], 1), w["od_w_in"], w["od_conv_w"], "od_in")
    m1, x3 = _mm_post(y1, w["od_w_out"], row(w["mix_norm_post"], 1), x2, "od_out")
    (dx4, sq), ffn1 = ffn_fwd(x3, 1, target)

    dx3, _, gf1 = ffn_bwd(dx4, ffn1, 1, 0.0)
    dm1, dy1, dg_mo1 = _mm_post_bwd(dx3, m1, row(w["mix_norm_post"], 1), w["od_w_out_t"], "od_out_bwd")
    g["od_w_out"] = _mm_tn(y1, dm1, "od_out_dw")
    dz1, g["od_conv_w"], dx2, dg_mp1 = _od_in_bwd(dy1, z1, w["od_conv_w"], w["od_w_in_t"], x2,
                                                  row(w["mix_norm_pre"], 1), dx3, "od_in_bwd")
    g["od_w_in"] = _mm_tn(dz1, h2, "od_in_dw")
    tok = point("layer1_grads", dx2)

    dx1, tok, gf0 = ffn_bwd(dx2, ffn0, 0, tok)
    tok = tok + point("ffn0_grads", dx1)
    dm0, dcat, dg_mo0 = _mm_post_bwd(dx1, m0, row(w["mix_norm_post"], 0) + tok, w["ev_w_out_t"], "ev_out_bwd")
    tok = point("ev_out_bwd_done", dcat)
    g["ev_w_out"] = _mm_tn(cat, dm0, "ev_out_dw")
    dq, dkv, dsk = _att_bwd(qkv, dcat, w["ev_sinks"] + tok, "ev_att_bwd")
    tok = point("ev_att_bwd_done", dq)
    dz0, g["ev_a_conv_w"], dvec, dx0, dg_mp0 = _ev_mid_bwd(
        dcat, c0, z0, dq, dkv, tab, w["ev_a_conv_w"], w["ev_a_ln_g"] + tok, w["ev_a_ln_b"], w["ev_w_in_t"], x,
        row(w["mix_norm_pre"], 0), dx1, "ev_in_bwd")
    point("ev_mid_bwd_done", dz0)
    g["ev_w_in"] = _mm_tn(dz0, h0, "ev_in_dw")

    g["ev_a_conv_b"] = dvec[0:1]
    g["ev_a_ln_g"] = dvec[1:2]
    g["ev_a_ln_b"] = dvec[2:3]
    g["ev_sinks"] = dsk[:, 0]
    g["mix_norm_pre"] = jnp.concatenate([dg_mp0, dg_mp1], axis=0)
    g["mix_norm_post"] = jnp.concatenate([dg_mo0, dg_mo1], axis=0)
    g["ffn_norm_pre"] = jnp.concatenate([gf0["ffn_norm_pre"], gf1["ffn_norm_pre"]], axis=0)
    g["ffn_norm_post"] = jnp.concatenate([gf0["ffn_norm_post"], gf1["ffn_norm_post"]], axis=0)
    g["ffn_conv_w"] = jnp.stack([gf0["ffn_conv_w"], gf1["ffn_conv_w"]], axis=0)
    return sq, dx0, g


ANY = pl.BlockSpec(memory_space=pl.ANY)
PACK_COLS = 1024


def _me():
    return lax.axis_index("x"), lax.axis_index("y"), lax.axis_index("c")


def _other_chips(x, y):
    return [(1 - x, y), (x, 1 - y), (1 - x, 1 - y)]


def _remote(src, dst, send, recv, dev):
    return pltpu.make_async_remote_copy(src_ref=src, dst_ref=dst, send_sem=send, recv_sem=recv,
                                        device_id=dev, device_id_type=MESH)


def _gather_chips(wp, name):
    r, cols = wp.shape
    rh = r // 2

    def body(w_ref, o_ref, send, recv):
        x, y, c = _me()
        p = 2 * x + y
        sib = (x, y, 1 - c)
        chips = _other_chips(x, y)
        half = pl.ds(c * rh, rh)
        other = pl.ds((1 - c) * rh, rh)
        sent = [_remote(w_ref.at[half], o_ref.at[p, half], send.at[k], recv.at[k], (cx, cy, c))
                for k, (cx, cy) in enumerate(chips)]
        for cp in sent:
            cp.start()
        for k, (cx, cy) in enumerate(chips):
            q = 2 * cx + cy
            _remote(w_ref.at[half], o_ref.at[q, half], send.at[k], recv.at[k], (cx, cy, c)).wait_recv()
            fwd = _remote(o_ref.at[q, half], o_ref.at[q, half], send.at[3 + k], recv.at[3 + k], sib)
            fwd.start()
            sent.append(fwd)
        for k, (cx, cy) in enumerate(chips):
            q = 2 * cx + cy
            _remote(o_ref.at[q, other], o_ref.at[q, other], send.at[3 + k], recv.at[3 + k], sib).wait_recv()
        for cp in sent:
            cp.wait_send()

    return pl.pallas_call(
        body, name=name, in_specs=[ANY], out_specs=ANY,
        out_shape=jax.ShapeDtypeStruct((N_CHIPS, r, cols), wp.dtype),
        scratch_shapes=[pltpu.SemaphoreType.DMA((6,)), pltpu.SemaphoreType.DMA((6,))],
    )(wp)


HBM_SPEC = pl.BlockSpec(memory_space=pltpu.HBM)
SEM_SPEC = pl.BlockSpec(memory_space=pltpu.SEMAPHORE)
DATAFLOW = pltpu.SideEffectType.DATAFLOW_SIDE_EFFECTING


def _gather_plan(w_ref, land_ref):
    x, y, c = _me()
    return [(w_ref, land_ref.at[2 * x + y], (cx, cy, c)) for cx, cy in _other_chips(x, y)]


def _copies_start(src, land_shape, plan, n, name):
    def body(src_ref, land_ref, send, recv, src_thru, land_thru, token):
        for k, (s_view, d_view, dev) in enumerate(plan(src_ref, land_ref)):
            _remote(s_view, d_view, send.at[k], recv.at[k], dev).start()
        token[...] = jnp.zeros_like(token)

    return pl.pallas_call(
        body, name=name,
        out_shape=(pltpu.SemaphoreType.DMA((n,)), pltpu.SemaphoreType.DMA((n,)), pltpu.HBM(src.shape, src.dtype),
                   pltpu.HBM(land_shape, src.dtype), jax.ShapeDtypeStruct((8, LANES), F32)),
        in_specs=(HBM_SPEC, HBM_SPEC),
        out_specs=(SEM_SPEC, SEM_SPEC, HBM_SPEC, HBM_SPEC, pl.BlockSpec(memory_space=pltpu.VMEM)),
        input_output_aliases={0: 2, 1: 3},
        compiler_params=pltpu.CompilerParams(has_side_effects=DATAFLOW),
    )(pltpu.with_memory_space_constraint(src, pltpu.HBM),
      pltpu.with_memory_space_constraint(lax.empty(land_shape, src.dtype), pltpu.HBM))


def _copies_wait(started, after, plan, name):
    send, recv, src_thru, land_thru, _ = started

    def body(src_ref, land_ref, send, recv, after_ref, src_dead, land_out):
        for k, (s_view, d_view, dev) in enumerate(plan(src_ref, land_ref)):
            cp = _remote(s_view, d_view, send.at[k], recv.at[k], dev)
            cp.wait_send()
            cp.wait_recv()

    return pl.pallas_call(
        body, name=name,
        out_shape=(pltpu.HBM(src_thru.shape, src_thru.dtype), pltpu.HBM(land_thru.shape, land_thru.dtype)),
        in_specs=(HBM_SPEC, HBM_SPEC, SEM_SPEC, SEM_SPEC, ANY),
        out_specs=(HBM_SPEC, HBM_SPEC),
        input_output_aliases={0: 0, 1: 1},
        compiler_params=pltpu.CompilerParams(has_side_effects=DATAFLOW),
    )(src_thru, land_thru, send, recv, after)


def _swap_plan(g_ref, land_ref):
    x, y, c = _me()
    return [(g_ref.at[q, 1 - c], land_ref.at[q], (x, y, 1 - c)) for q in range(N_CHIPS)]


def _ici_plan(a_ref, land_ref):
    x, y, c = _me()
    return [(a_ref.at[2 * cx + cy], land_ref.at[2 * x + y], (cx, cy, c)) for cx, cy in _other_chips(x, y)]


def _share_plan(h_ref, land_ref):
    x, y, c = _me()
    return [(h_ref, land_ref, (x, y, 1 - c))]


def _exchange8(v, reduce, name):
    r, cols = v.shape
    rel = [(a, b, d) for a in (0, 1) for b in (0, 1) for d in (0, 1) if (a, b, d) != (0, 0, 0)]

    def body(v_ref, o_ref, *rest):
        if reduce:
            gbuf, send, recv = rest
        else:
            gbuf = o_ref
            send, recv = rest
        x, y, c = _me()
        me = 4 * x + 2 * y + c
        gbuf[me] = v_ref[...]
        sent = []
        for k, (a, b, d) in enumerate(rel):
            cp = _remote(v_ref, gbuf.at[me], send.at[k], recv.at[k], ((x + a) % 2, (y + b) % 2, (c + d) % 2))
            cp.start()
            sent.append(cp)
        for k, (a, b, d) in enumerate(rel):
            src = 4 * ((x + a) % 2) + 2 * ((y + b) % 2) + (c + d) % 2
            _remote(v_ref, gbuf.at[src], send.at[k], recv.at[k], (x, y, c)).wait_recv()
        for cp in sent:
            cp.wait_send()
        if reduce:
            acc = gbuf[0]
            for n in range(1, 8):
                acc = acc + gbuf[n]
            o_ref[...] = acc

    vmem = pl.BlockSpec(memory_space=pltpu.VMEM)
    sems = [pltpu.SemaphoreType.DMA((7,)), pltpu.SemaphoreType.DMA((7,))]
    if reduce:
        out_shape = jax.ShapeDtypeStruct((r, cols), F32)
        scratch = [pltpu.VMEM((8, r, cols), F32)] + sems
    else:
        out_shape = jax.ShapeDtypeStruct((8, r, cols), F32)
        scratch = sems
    return pl.pallas_call(body, name=name, in_specs=[vmem], out_specs=vmem, out_shape=out_shape,
                          scratch_shapes=scratch)(v)


def _rs_swap(g, name):
    _, _, rh, cols = g.shape

    def body(g_ref, o_ref, send, recv):
        x, y, c = _me()
        cps = [_remote(g_ref.at[q, 1 - c], o_ref.at[q], send.at[q], recv.at[q], (x, y, 1 - c)) for q in range(N_CHIPS)]
        for cp in cps:
            cp.start()
        for cp in cps:
            cp.wait()

    return pl.pallas_call(
        body, name=name, in_specs=[ANY], out_specs=ANY,
        out_shape=jax.ShapeDtypeStruct((N_CHIPS, rh, cols), F32),
        scratch_shapes=[pltpu.SemaphoreType.DMA((N_CHIPS,)), pltpu.SemaphoreType.DMA((N_CHIPS,))],
    )(g)


def _row_tile(rows, pref, mult=8):
    if rows <= pref:
        return rows
    t = (pref // mult) * mult
    while t >= mult:
        if rows % t == 0:
            return t
        t -= mult
    return rows


def _rs_add(g, sib, c, name):
    _, _, rh, cols = g.shape
    tr = _row_tile(rh, 512, 16)

    def body(c_ref, g_ref, s_ref, o_ref):
        o_ref[...] = (g_ref[...] + s_ref[...]).astype(BF16)

    grid_spec = pltpu.PrefetchScalarGridSpec(
        num_scalar_prefetch=1, grid=(N_CHIPS, rh // tr),
        in_specs=[pl.BlockSpec((None, None, tr, cols), lambda q, i, cr: (q, cr[0], i, 0)),
                  pl.BlockSpec((None, tr, cols), lambda q, i, cr: (q, i, 0))],
        out_specs=pl.BlockSpec((None, tr, cols), lambda q, i, cr: (q, i, 0)),
    )
    return pl.pallas_call(
        body, name=name, grid_spec=grid_spec,
        out_shape=jax.ShapeDtypeStruct((N_CHIPS, rh, cols), BF16),
        compiler_params=_cparams(("parallel", "parallel")),
    )(c, g, sib)


def _rs_ici(a, name):
    _, rh, cols = a.shape

    def body(a_ref, o_ref, send, recv):
        x, y, c = _me()
        p = 2 * x + y
        cps = []
        for k, (cx, cy) in enumerate(_other_chips(x, y)):
            cp = _remote(a_ref.at[2 * cx + cy], o_ref.at[p], send.at[k], recv.at[k], (cx, cy, c))
            cp.start()
            cps.append(cp)
        for k, (cx, cy) in enumerate(_other_chips(x, y)):
            q = 2 * cx + cy
            _remote(a_ref.at[q], o_ref.at[q], send.at[k], recv.at[k], (cx, cy, c)).wait_recv()
        for cp in cps:
            cp.wait_send()

    return pl.pallas_call(
        body, name=name, in_specs=[ANY], out_specs=ANY,
        out_shape=jax.ShapeDtypeStruct((N_CHIPS, rh, cols), a.dtype),
        scratch_shapes=[pltpu.SemaphoreType.DMA((3,)), pltpu.SemaphoreType.DMA((3,))],
    )(a)


def _rs_sum(rb, a, chip, name):
    _, rh, cols = rb.shape
    tr = _row_tile(rh, 512, 16)

    def body(p_ref, r0, r1, r2, r3, own, o_ref):
        p = p_ref[0]
        ownv = own[...].astype(F32)
        acc = None
        for q, r in enumerate((r0, r1, r2, r3)):
            v = jnp.where(p == q, ownv, r[...].astype(F32))
            acc = v if acc is None else acc + v
        o_ref[...] = acc

    def spec(q):
        return pl.BlockSpec((None, tr, cols), lambda i, pr: (jnp.where(pr[0] == q, (q + 1) % N_CHIPS, q), i, 0))

    grid_spec = pltpu.PrefetchScalarGridSpec(
        num_scalar_prefetch=1, grid=(rh // tr,),
        in_specs=[spec(0), spec(1), spec(2), spec(3), pl.BlockSpec((None, tr, cols), lambda i, pr: (pr[0], i, 0))],
        out_specs=pl.BlockSpec((tr, cols), lambda i, pr: (i, 0)),
    )
    return pl.pallas_call(
        body, name=name, grid_spec=grid_spec,
        out_shape=jax.ShapeDtypeStruct((rh, cols), F32),
        compiler_params=_cparams(("parallel",)),
    )(chip, rb, rb, rb, rb, a)


def _rs_share(hsum, name):
    rh, cols = hsum.shape

    def body(h_ref, o_ref, send, recv):
        x, y, c = _me()
        cp = _remote(h_ref, o_ref, send, recv, (x, y, 1 - c))
        cp.start()
        cp.wait()

    return pl.pallas_call(
        body, name=name, in_specs=[ANY], out_specs=ANY,
        out_shape=jax.ShapeDtypeStruct((rh, cols), F32),
        scratch_shapes=[pltpu.SemaphoreType.DMA, pltpu.SemaphoreType.DMA],
    )(hsum)


def _adamw(w, g, m, v, name):
    rows, cols = w.shape
    tr = _row_tile(rows, 512)

    def body(w_ref, g_ref, m_ref, v_ref, d_ref, nm_ref, nv_ref):
        gv = g_ref[...]
        nm = ADAM_B1 * m_ref[...] + (1.0 - ADAM_B1) * gv
        nv = ADAM_B2 * v_ref[...] + (1.0 - ADAM_B2) * (gv * gv)
        m_hat = nm / (1.0 - ADAM_B1 ** ADAM_STEP)
        v_hat = nv / (1.0 - ADAM_B2 ** ADAM_STEP)
        d_ref[...] = -ADAM_LR * (m_hat / (jnp.sqrt(v_hat) + ADAM_EPS) + ADAM_WD * w_ref[...])
        nm_ref[...] = nm
        nv_ref[...] = nv

    spec = pl.BlockSpec((tr, cols), lambda i: (i, 0))
    shp = jax.ShapeDtypeStruct((rows, cols), F32)
    return pl.pallas_call(
        body, name=name, grid=(rows // tr,), in_specs=[spec] * 4, out_specs=[spec] * 3, out_shape=[shp] * 3,
        compiler_params=_cparams(("parallel",)),
    )(w, g, m, v)


WEIGHTS = ("mix_norm_pre", "mix_norm_post", "ffn_norm_pre", "ffn_norm_post", "ev_w_in", "ev_a_conv_w", "ev_a_conv_b",
           "ev_a_ln_g", "ev_a_ln_b", "ev_sinks", "ev_w_out", "od_w_in", "od_conv_w", "od_w_out", "ffn_w_up",
           "ffn_conv_w", "ffn_w_down")
MATS = (("ev_w_in", 2), ("ev_w_out", 1), ("od_w_in", 2), ("od_w_out", 1), ("ffn_w_up", 2), ("ffn_w_down", 1))
UNITS = (("ev_w_in", 0, 2), ("ev_w_out", 0, 1), ("ffn_w_up", 0, 2), ("ffn_w_down", 0, 1),
         ("od_w_in", 0, 2), ("od_w_out", 0, 1), ("ffn_w_up", 1, 2), ("ffn_w_down", 1, 1))
GATHER_GROUPS = ((0, 1), (2, 3), (4, 5, 6, 7))
REDUCE_GROUPS = {"layer1": (4, 5, 6, 7), "ffn0": (2, 3), "ev": (0, 1)}
SMALL_SHARDED = ("ev_a_conv_w", "od_conv_w", "ffn_conv_w")
REPLICATED = ("mix_norm_pre", "mix_norm_post", "ffn_norm_pre", "ffn_norm_post", "ev_a_conv_b", "ev_a_ln_g",
              "ev_a_ln_b", "ev_sinks")


def _pack(parts, rows_multiple):
    flat = jnp.concatenate([p.reshape(-1) for p in parts])
    unit = rows_multiple * PACK_COLS
    pad = (-flat.shape[0]) % unit
    if pad:
        flat = jnp.concatenate([flat, jnp.zeros((pad,), flat.dtype)])
    return flat.reshape(-1, PACK_COLS)


def _unpack(buf, shapes):
    flat = buf.reshape(-1)
    out, off = [], 0
    for shp in shapes:
        n = 1
        for d in shp:
            n *= d
        out.append(flat[off:off + n].reshape(shp))
        off += n
    return out


def _shard_rows(shard, axis):
    if axis == 2:
        shard = jnp.swapaxes(shard, 1, 2)
    return shard.reshape(-1, PACK_COLS)


def kernel(x, positions, mix_norm_pre, mix_norm_post, ffn_norm_pre, ffn_norm_post, ev_w_in, ev_a_conv_w, ev_a_conv_b, ev_a_ln_g, ev_a_ln_b, ev_sinks, ev_w_out, od_w_in, od_conv_w, od_w_out, ffn_w_up, ffn_conv_w, ffn_w_down, loss_target, m_mix_norm_pre, m_mix_norm_post, m_ffn_norm_pre, m_ffn_norm_post, m_ev_w_in, m_ev_a_conv_w, m_ev_a_conv_b, m_ev_a_ln_g, m_ev_a_ln_b, m_ev_sinks, m_ev_w_out, m_od_w_in, m_od_conv_w, m_od_w_out, m_ffn_w_up, m_ffn_conv_w, m_ffn_w_down, v_mix_norm_pre, v_mix_norm_post, v_ffn_norm_pre, v_ffn_norm_post, v_ev_w_in, v_ev_a_conv_w, v_ev_a_conv_b, v_ev_a_ln_g, v_ev_a_ln_b, v_ev_sinks, v_ev_w_out, v_od_w_in, v_od_conv_w, v_od_w_out, v_ffn_w_up, v_ffn_conv_w, v_ffn_w_down):
    wts = dict(zip(WEIGHTS, (mix_norm_pre, mix_norm_post, ffn_norm_pre, ffn_norm_post, ev_w_in, ev_a_conv_w, ev_a_conv_b,
                             ev_a_ln_g, ev_a_ln_b, ev_sinks, ev_w_out, od_w_in, od_conv_w, od_w_out, ffn_w_up, ffn_conv_w,
                             ffn_w_down)))
    mom = dict(zip(WEIGHTS, (m_mix_norm_pre, m_mix_norm_post, m_ffn_norm_pre, m_ffn_norm_post, m_ev_w_in, m_ev_a_conv_w,
                             m_ev_a_conv_b, m_ev_a_ln_g, m_ev_a_ln_b, m_ev_sinks, m_ev_w_out, m_od_w_in, m_od_conv_w,
                             m_od_w_out, m_ffn_w_up, m_ffn_conv_w, m_ffn_w_down)))
    var = dict(zip(WEIGHTS, (v_mix_norm_pre, v_mix_norm_post, v_ffn_norm_pre, v_ffn_norm_post, v_ev_w_in, v_ev_a_conv_w,
                             v_ev_a_conv_b, v_ev_a_ln_g, v_ev_a_ln_b, v_ev_sinks, v_ev_w_out, v_od_w_in, v_od_conv_w,
                             v_od_w_out, v_ffn_w_up, v_ffn_conv_w, v_ffn_w_down)))
    xi, yi, ci = _me()
    chip = 2 * xi + yi

    unit_rows = [_shard_rows(wts[k][l:l + 1].astype(BF16), axis) for k, l, axis in UNITS]

    def group_block(group):
        return jnp.concatenate([unit_rows[u] for u in group], axis=0)

    def unpack_group(group, landed, own):
        full = lax.dynamic_update_slice(landed, own[None], (chip, 0, 0))
        out, off = {}, 0
        for u in group:
            k, l, axis = UNITS[u]
            n = unit_rows[u].shape[0]
            native = full[:, off:off + n].reshape(N_CHIPS * n, PACK_COLS)
            off += n
            key = (lambda name: (name, l)) if k.startswith("ffn") else (lambda name: name)
            out[key(k + "_t" if axis == 2 else k)] = native
            out[key(k if axis == 2 else k + "_t")] = native.T
        return out

    small_shapes = [wts[k].shape for k in SMALL_SHARDED]
    small_all = _exchange8(_pack([wts[k] for k in SMALL_SHARDED], 8), False, "gather_small")
    blocks = [group_block(grp) for grp in GATHER_GROUPS]
    first = _gather_chips(blocks[0], "gather_mats")
    later = {}
    for stage, grp, blk in zip(("ffn0", "layer1"), GATHER_GROUPS[1:], blocks[1:]):
        later[stage] = (grp, blk, _copies_start(blk, (N_CHIPS,) + blk.shape, _gather_plan, 3, "gather_" + stage + "_start"))

    def fetch(stage, after):
        grp, blk, started = later[stage]
        own, landed = _copies_wait(started, after, _gather_plan, "gather_" + stage + "_wait")
        return unpack_group(grp, landed, own)

    w = {k: wts[k] for k in REPLICATED}
    w.update(unpack_group(GATHER_GROUPS[0], first, blocks[0]))
    per_chip = [_unpack(small_all[2 * q], small_shapes) for q in range(N_CHIPS)]
    for n, k in enumerate(SMALL_SHARDED):
        w[k] = jnp.concatenate([per_chip[q][n] for q in range(N_CHIPS)], axis=-1)
    for k in ("ev_a_conv_w", "od_conv_w"):
        w[k] = w[k][0]
    w["ev_sinks"] = w["ev_sinks"][0]
    w["mix_norm_pre"] = w["mix_norm_pre"] + sum(later[s][2][4][0, 0] for s in later)

    core = jnp.reshape(ci, (1,)).astype(jnp.int32)
    chip_arr = jnp.reshape(chip, (1,)).astype(jnp.int32)
    per_layer = {}

    def group_grads(group, g):
        gp = jnp.concatenate([(g[k, l] if k.startswith("ffn") else g[k]).reshape(N_CHIPS, -1, PACK_COLS)
                              for k, l, _ in (UNITS[u] for u in group)], axis=1)
        return gp.reshape(N_CHIPS, 2, gp.shape[1] // 2, PACK_COLS)

    def finish(group, half, other):
        red = jnp.concatenate([jnp.where(ci == 0, half, other), jnp.where(ci == 0, other, half)], axis=0)
        off = 0
        for u in group:
            k, l, axis = UNITS[u]
            n = unit_rows[u].shape[0]
            part = red[off:off + n]
            off += n
            per_layer[k, l] = part.T if axis == 2 else part

    chains = {}

    def chain_step(tag, after, g):
        group = REDUCE_GROUPS[tag]
        st = chains.setdefault(tag, {"step": 0})
        step = st["step"]
        st["step"] = step + 1
        if step == 0:
            gp = group_grads(group, g)
            rh = gp.shape[2]
            st["swap"] = _copies_start(gp, (N_CHIPS, rh, PACK_COLS), _swap_plan, N_CHIPS, f"rs_{tag}_swap_start")
            return st["swap"][4][0, 0]
        if step == 1:
            gp, sib = _copies_wait(st["swap"], after, _swap_plan, f"rs_{tag}_swap_wait")
            pair = _rs_add(gp, sib, core, f"rs_{tag}_add")
            st["ici"] = _copies_start(pair, pair.shape, _ici_plan, 3, f"rs_{tag}_ici_start")
            return st["ici"][4][0, 0]
        if step == 2:
            pair, landed = _copies_wait(st["ici"], after, _ici_plan, f"rs_{tag}_ici_wait")
            half = _rs_sum(landed, pair, chip_arr, f"rs_{tag}_sum")
            st["share"] = _copies_start(half, half.shape, _share_plan, 1, f"rs_{tag}_share_start")
            return st["share"][4][0, 0]
        half, other = _copies_wait(st["share"], after, _share_plan, f"rs_{tag}_share_wait")
        finish(group, half, other)
        return 0.0

    schedule = {"layer1_grads": ("layer1",), "ffn0_bwd_done": ("layer1",), "ffn0_grads": ("ffn0",),
                "ev_out_bwd_done": ("layer1", "ffn0"), "ev_att_bwd_done": ("layer1", "ffn0"),
                "ev_mid_bwd_done": ("ffn0",)}

    def emit(place, after, g):
        return sum(chain_step(tag, after, g) for tag in schedule.get(place, ()))

    sq, dx, g = _local_step(x[0], positions[0], loss_target[0], w, fetch, emit)
    loss = lax.psum(0.5 * jnp.sum(sq) / D_MODEL, ("x", "y", "c"))

    gp = group_grads(REDUCE_GROUPS["ev"], g)
    sib = _rs_swap(gp, "rs_swap")
    pair = _rs_add(gp, sib, core, "rs_add")
    landed = _rs_ici(pair, "rs_ici")
    half = _rs_sum(landed, pair, chip_arr, "rs_sum")
    finish(REDUCE_GROUPS["ev"], half, _rs_share(half, "rs_share"))
    grads = {k: jnp.stack([per_layer[k, l] for l in range(wts[k].shape[0])], axis=0) for k, _ in MATS}

    small_keys = REPLICATED + SMALL_SHARDED
    full_shapes = [wts[k].shape for k in REPLICATED] + [wts[k].shape[:-1] + (wts[k].shape[-1] * N_CHIPS,) for k in SMALL_SHARDED]
    sm = _exchange8(_pack([g[k] for k in small_keys], 8), True, "reduce_small")
    for k, full in zip(small_keys, _unpack(sm, full_shapes)):
        if k in SMALL_SHARDED:
            n = wts[k].shape[-1]
            full = lax.dynamic_slice_in_dim(full, chip * n, n, axis=full.ndim - 1)
        grads[k] = full

    deltas, new_m, new_v = {}, {}, {}
    for k in WEIGHTS:
        shp = wts[k].shape
        two_d = (-1, shp[-1])
        d, nm, nv = _adamw(wts[k].reshape(two_d), grads[k].reshape(two_d), mom[k].reshape(two_d), var[k].reshape(two_d),
                           "adamw_" + k)
        deltas[k], new_m[k], new_v[k] = d.reshape(shp), nm.reshape(shp), nv.reshape(shp)

    return (loss, dx[None], *[grads[k] for k in WEIGHTS], *[deltas[k] for k in WEIGHTS],
            *[new_m[k] for k in WEIGHTS], *[new_v[k] for k in WEIGHTS])
```

```python
import functools

import jax
import jax.numpy as jnp
import numpy as np
from jax import lax
from jax.experimental import pallas as pl
from jax.experimental.pallas import tpu as pltpu

F32 = jnp.float32
BF16 = jnp.bfloat16
MESH = pl.DeviceIdType.MESH

D_MODEL = 1024
HEAD_DIM = 64
A_CH = 512
A_CONV = 31
N_Q_HEADS = 8
WINDOW = 128
ROPE_THETA = 500000.0
ROPE_DIM = 16
D_FF = 2816
RMS_EPS = 1e-6
LN_EPS = 1e-5
ADAM_LR = 0.001
ADAM_B1 = 0.9
ADAM_B2 = 0.999
ADAM_EPS = 1e-08
ADAM_WD = 0.01
ADAM_STEP = 10

LANES = 128
HALO16 = 16
HALO32 = 32
VMEM_LIMIT = 56 * 1024 * 1024
FFN_BWD_VMEM = 60 * 1024 * 1024
N_CHIPS = 4


def _cparams(sem):
    return pltpu.CompilerParams(dimension_semantics=sem, vmem_limit_bytes=VMEM_LIMIT)


def _tile(n, pref):
    if n <= pref:
        return n
    t = (pref // LANES) * LANES
    while t >= LANES:
        if n % t == 0:
            return t
        t -= LANES
    return n


MM_ROWS = 512


def _rms_scale(v):
    return lax.rsqrt(jnp.mean(v * v, axis=-1, keepdims=True) + RMS_EPS)


def _rms_bwd(dy, v, g):
    r = _rms_scale(v)
    nrm = v * r
    dn = dy * g
    return r * (dn - nrm * jnp.mean(dn * nrm, axis=-1, keepdims=True)), jnp.sum(dy * nrm, axis=0, keepdims=True)


def _mm_post(a, w, g, xres, name):
    s, k = a.shape
    d = w.shape[1]
    tm = min(MM_ROWS, s)

    def body(a_ref, w_ref, g_ref, x_ref, m_ref, o_ref):
        mv = jnp.dot(a_ref[...], w_ref[...], preferred_element_type=F32)
        m_ref[...] = mv
        o_ref[...] = x_ref[...] + mv * _rms_scale(mv) * g_ref[...]

    row = pl.BlockSpec((tm, d), lambda i: (i, 0))
    return pl.pallas_call(
        body, name=name, grid=(s // tm,),
        in_specs=[pl.BlockSpec((tm, k), lambda i: (i, 0)), _full((k, d)), _full((1, d)), row],
        out_specs=[row, row],
        out_shape=[jax.ShapeDtypeStruct((s, d), F32), jax.ShapeDtypeStruct((s, d), F32)],
        compiler_params=_cparams(("parallel",)),
    )(a, w, g, xres)


def _mm_post_bwd(dy, m, g, w_t, name):
    s, d = m.shape
    k = w_t.shape[1]
    tm = min(MM_ROWS, s)

    def body(dy_ref, m_ref, g_ref, wt_ref, dm_ref, da_ref, dg_ref):
        @pl.when(pl.program_id(0) == 0)
        def _():
            dg_ref[...] = jnp.zeros_like(dg_ref)

        dm, dg = _rms_bwd(dy_ref[...], m_ref[...], g_ref[...])
        dg_ref[...] += dg
        dmb = dm.astype(BF16)
        dm_ref[...] = dmb
        da_ref[...] = jnp.dot(dmb, wt_ref[...], preferred_element_type=F32)

    row = pl.BlockSpec((tm, d), lambda i: (i, 0))
    return pl.pallas_call(
        body, name=name, grid=(s // tm,),
        in_specs=[row, row, _full((1, d)), _full((d, k))],
        out_specs=[row, pl.BlockSpec((tm, k), lambda i: (i, 0)), _full((1, d))],
        out_shape=[jax.ShapeDtypeStruct((s, d), BF16), jax.ShapeDtypeStruct((s, k), F32),
                   jax.ShapeDtypeStruct((1, d), F32)],
        compiler_params=_cparams(("arbitrary",)),
    )(dy, m, g, w_t)


def _mm_tn(a, b, name):
    s, k = a.shape
    _, n = b.shape
    tk = _tile(k, 1408)
    tn = _tile(n, 1408)
    ts = min(2048, s)

    def body(a_ref, b_ref, o_ref):
        @pl.when(pl.program_id(2) == 0)
        def _():
            o_ref[...] = jnp.zeros_like(o_ref)

        o_ref[...] += lax.dot_general(a_ref[...], b_ref[...], (((0,), (0,)), ((), ())),
                                      preferred_element_type=F32)

    return pl.pallas_call(
        body, name=name, grid=(k // tk, n // tn, s // ts),
        in_specs=[pl.BlockSpec((ts, tk), lambda i, j, l: (l, i)), pl.BlockSpec((ts, tn), lambda i, j, l: (l, j))],
        out_specs=pl.BlockSpec((tk, tn), lambda i, j, l: (i, j)),
        out_shape=jax.ShapeDtypeStruct((k, n), F32),
        compiler_params=_cparams(("parallel", "parallel", "arbitrary")),
    )(a, b)


def _cur(tr, w, col=0):
    return pl.BlockSpec((tr, w), lambda i: (i, col))


def _prev(tr, h, w, col=0):
    return pl.BlockSpec((h, w), lambda i: (jnp.maximum(i * (tr // h) - 1, 0), col))


def _next(tr, h, w, nrows, col=0):
    last = nrows // h - 1
    return pl.BlockSpec((h, w), lambda i: (jnp.minimum((i + 1) * (tr // h), last), col))


def _full(shape):
    return pl.BlockSpec(shape, lambda i: tuple(0 for _ in shape))


def _silu_parts(g):
    sig = jax.nn.sigmoid(g)
    return sig, g * sig


FFN_CW = 256
FFN_NBUF = 3


def _conv3_taps(buf, w, off, rows):
    return (w[0:1] * buf[pl.ds(off, rows), :] + w[1:2] * buf[pl.ds(off + 1, rows), :]
            + w[2:3] * buf[pl.ds(off + 2, rows), :])


WHOLE_VMEM = pl.BlockSpec(memory_space=pltpu.VMEM)


def _ffn_fwd(x, g_pre, wu, conv_w, wd, g_post, name, target=None):
    s, d = x.shape
    f2 = wu.shape[1]
    f = f2 // 2
    tr = min(256, s)
    h = HALO16
    cw = FFN_CW
    head = target is not None

    def body(*refs):
        if head:
            (x_ref, gpre_ref, wu_ref, cw_ref, wd_ref, gpost_ref, t_ref, xo_ref, sq_ref, f_ref, h_ref, up_ref, u_ref,
             carry, gbuf, vbuf, facc) = refs
        else:
            (x_ref, gpre_ref, wu_ref, cw_ref, wd_ref, gpost_ref, xo_ref, f_ref, h_ref, up_ref, u_ref,
             carry, gbuf, vbuf, facc) = refs

        @pl.when(pl.program_id(0) == 0)
        def _():
            carry[...] = jnp.zeros_like(carry)
            if head:
                sq_ref[...] = jnp.zeros_like(sq_ref)

        xv = x_ref[...]
        r = lax.rsqrt(jnp.mean(xv * xv, axis=-1, keepdims=True) + RMS_EPS)
        hv = (xv * r * gpre_ref[...]).astype(BF16)
        h_ref[...] = hv
        nchunk = f // cw

        def up_proj(j):
            for buf, base in ((gbuf, 0), (vbuf, f)):
                cs = slice(base + j * cw, base + (j + 1) * cw)
                dst = buf.at[j % FFN_NBUF]
                upc = jnp.dot(hv, wu_ref[:, cs], preferred_element_type=F32)
                up_ref[:, cs] = upc.astype(BF16)
                dst[0:h, :] = carry[:, cs]
                dst[h:h + tr, :] = upc
                carry[:, cs] = upc[tr - h:tr, :]

        def down_proj(j, act):
            part = jnp.dot(act, wd_ref[j * cw:(j + 1) * cw, :], preferred_element_type=F32)
            if j == 0:
                facc[...] = part
            else:
                facc[...] += part

        for j in range(FFN_NBUF - 1):
            up_proj(j)
        pending = None
        for j in range(nchunk):
            cg = slice(j * cw, (j + 1) * cw)
            cv = slice(f + j * cw, f + (j + 1) * cw)
            if j + FFN_NBUF - 1 < nchunk:
                up_proj(j + FFN_NBUF - 1)
            if pending is not None:
                down_proj(*pending)
            g = _conv3_taps(gbuf.at[j % FFN_NBUF], cw_ref[:, cg], h - 2, tr)
            v = _conv3_taps(vbuf.at[j % FFN_NBUF], cw_ref[:, cv], h - 2, tr)
            u_ref[:, cg] = g.astype(BF16)
            u_ref[:, cv] = v.astype(BF16)
            act = (g * jax.nn.sigmoid(g) * v).astype(BF16)
            pending = (j, act)
        down_proj(*pending)
        fv = facc[...]
        f_ref[...] = fv
        r2 = lax.rsqrt(jnp.mean(fv * fv, axis=-1, keepdims=True) + RMS_EPS)
        xo = xv + fv * r2 * gpost_ref[...]
        if head:
            err = xo - t_ref[...]
            xo_ref[...] = err * (1.0 / d)
            sq_ref[...] += jnp.sum(err * err, axis=0, keepdims=True)
        else:
            xo_ref[...] = xo

    row = _cur(tr, d)
    wide = _cur(tr, f2)
    vec = _full((1, d))
    out_specs = [row] + ([vec] if head else []) + [row, row, wide, wide]
    out_shape = ([jax.ShapeDtypeStruct((s, d), F32)] + ([jax.ShapeDtypeStruct((1, d), F32)] if head else [])
                 + [jax.ShapeDtypeStruct((s, d), F32), jax.ShapeDtypeStruct((s, d), BF16),
                    jax.ShapeDtypeStruct((s, f2), BF16), jax.ShapeDtypeStruct((s, f2), BF16)])
    return pl.pallas_call(
        body, name=name, grid=(s // tr,),
        in_specs=[row, vec, WHOLE_VMEM, _full((3, f2)), WHOLE_VMEM, vec] + ([row] if head else []),
        out_specs=out_specs, out_shape=out_shape,
        scratch_shapes=[pltpu.VMEM((h, f2), F32), pltpu.VMEM((FFN_NBUF, h + tr, cw), F32),
                        pltpu.VMEM((FFN_NBUF, h + tr, cw), F32), pltpu.VMEM((tr, d), F32)],
        compiler_params=_cparams(("arbitrary",)),
    )(*((x, g_pre, wu, conv_w, wd, g_post) + ((target,) if head else ())))


def _ffn_bwd(dxo, fout, x, up, u, g_pre, g_post, wd_t, wu_t, conv_w, name):
    s, d = x.shape
    f2 = up.shape[1]
    f = f2 // 2
    tr = min(256, s)
    nt = s // tr
    h = HALO16
    cw = FFN_CW

    def body(dy_ref, f_ref, x_ref, up_ref, u_ref, gpre_ref, gpost_ref, wdt_ref, wut_ref, cw_ref,
             dx_ref, dup_ref, act_ref, df_ref, dcw_ref, dgpost_ref, dgpre_ref, carry, dgbuf, dvbuf, dhacc):
        @pl.when(pl.program_id(0) == 0)
        def _():
            carry[...] = jnp.zeros_like(carry)
            dcw_ref[...] = jnp.zeros_like(dcw_ref)
            dgpost_ref[...] = jnp.zeros_like(dgpost_ref)
            dgpre_ref[...] = jnp.zeros_like(dgpre_ref)

        dy = dy_ref[...]
        fv = f_ref[...]
        r = lax.rsqrt(jnp.mean(fv * fv, axis=-1, keepdims=True) + RMS_EPS)
        nrm = fv * r
        dn = dy * gpost_ref[...]
        dfv = (r * (dn - nrm * jnp.mean(dn * nrm, axis=-1, keepdims=True))).astype(BF16)
        dgpost_ref[...] += jnp.sum(dy * nrm, axis=0, keepdims=True)
        df_ref[...] = dfv
        nchunk = f // cw

        def dh_part(dupb, cs, first):
            part = jnp.dot(dupb, wut_ref[cs, :], preferred_element_type=F32)
            if first:
                dhacc[...] = part
            else:
                dhacc[...] += part

        def dact_of(j):
            return jnp.dot(dfv, wdt_ref[:, j * cw:(j + 1) * cw], preferred_element_type=F32)

        ahead = [dact_of(0)]
        for j in range(nchunk):
            ch = slice(j * cw, (j + 1) * cw)
            cg = ch
            cv = slice(f + j * cw, f + (j + 1) * cw)
            dact = ahead.pop(0)
            if j + 1 < nchunk:
                ahead.append(dact_of(j + 1))
            g = u_ref[:, cg].astype(F32)
            v = u_ref[:, cv].astype(F32)
            sig, sil = _silu_parts(g)
            act_ref[:, ch] = (sil * v).astype(BF16)
            du_g = dact * v * (sig * (1.0 + g * (1.0 - sig)))
            du_v = dact * sil
            for k, (dbuf, du, cs) in enumerate(((dgbuf.at[j % FFN_NBUF], du_g, cg), (dvbuf.at[j % FFN_NBUF], du_v, cv))):
                dbuf[0:tr, :] = du
                dbuf[tr:tr + h, :] = carry[:, cs]
                carry[:, cs] = du[0:h, :]
                w = cw_ref[:, cs]
                xin = up_ref[:, cs].astype(F32)
                acc = None
                for sh in range(3):
                    dsh = dbuf[pl.ds(sh, tr), :]
                    term = w[2 - sh:3 - sh] * dsh
                    acc = term if acc is None else acc + term
                    dcw_ref[2 - sh:3 - sh, cs] += jnp.sum(xin * dsh, axis=0, keepdims=True)
                dupb = acc.astype(BF16)
                dup_ref[:, cs] = dupb
                dh_part(dupb, cs, j == 0 and k == 0)
        dh = dhacc[...]
        xv = x_ref[...]
        r1 = lax.rsqrt(jnp.mean(xv * xv, axis=-1, keepdims=True) + RMS_EPS)
        n1 = xv * r1
        dn1 = dh * gpre_ref[...]
        dx_ref[...] = dy + r1 * (dn1 - n1 * jnp.mean(dn1 * n1, axis=-1, keepdims=True))
        dgpre_ref[...] += jnp.sum(dh * n1, axis=0, keepdims=True)

    def rev(w):
        return pl.BlockSpec((tr, w), lambda i: (nt - 1 - i, 0))

    vec = _full((1, d))
    return pl.pallas_call(
        body, name=name, grid=(nt,),
        in_specs=[rev(d), rev(d), rev(d), rev(f2), rev(f2), vec, vec, WHOLE_VMEM, WHOLE_VMEM, _full((3, f2))],
        out_specs=[rev(d), rev(f2), rev(f), rev(d), _full((3, f2)), vec, vec],
        out_shape=[jax.ShapeDtypeStruct((s, d), F32), jax.ShapeDtypeStruct((s, f2), BF16),
                   jax.ShapeDtypeStruct((s, f), BF16), jax.ShapeDtypeStruct((s, d), BF16),
                   jax.ShapeDtypeStruct((3, f2), F32), jax.ShapeDtypeStruct((1, d), F32),
                   jax.ShapeDtypeStruct((1, d), F32)],
        scratch_shapes=[pltpu.VMEM((h, f2), F32), pltpu.VMEM((FFN_NBUF, tr + h, cw), F32),
                        pltpu.VMEM((FFN_NBUF, tr + h, cw), F32), pltpu.VMEM((tr, d), F32)],
        compiler_params=pltpu.CompilerParams(dimension_semantics=("arbitrary",), vmem_limit_bytes=FFN_BWD_VMEM),
    )(dxo, fout, x, up, u, g_pre, g_post, wd_t, wu_t, conv_w)


def _od_in_fwd(x, g, w, conv_w, name):
    s, d = x.shape
    d3 = w.shape[1]
    tr = min(512, s)
    h = HALO16
    cw = FFN_CW
    nchunk = d // cw

    def body(x_ref, g_ref, w_ref, cw_ref, h_ref, z_ref, y_ref, carry, buf):
        @pl.when(pl.program_id(0) == 0)
        def _():
            carry[...] = jnp.zeros_like(carry)

        xv = x_ref[...]
        hv = (xv * _rms_scale(xv) * g_ref[...]).astype(BF16)
        h_ref[...] = hv

        def project(j):
            out = []
            for part in range(3):
                cs = slice(part * d + j * cw, part * d + (j + 1) * cw)
                zc = jnp.dot(hv, w_ref[:, cs], preferred_element_type=F32).astype(BF16)
                z_ref[:, cs] = zc
                out.append(zc.astype(F32))
            return out

        ahead = [project(0), project(1)]
        for j in range(nchunk):
            cb = slice(j * cw, (j + 1) * cw)
            bval, cval, uval = ahead.pop(0)
            if j + 2 < nchunk:
                ahead.append(project(j + 2))
            bf = buf.at[j % FFN_NBUF]
            cu = cval * uval
            bf[0:h, :] = carry[:, cb]
            bf[h:h + tr, :] = cu
            carry[:, cb] = cu[tr - h:tr, :]
            y_ref[:, cb] = (bval * _conv3_taps(bf, cw_ref[:, cb], h - 2, tr)).astype(BF16)

    row = _cur(tr, d)
    return pl.pallas_call(
        body, name=name, grid=(s // tr,),
        in_specs=[row, _full((1, d)), WHOLE_VMEM, _full((3, d))],
        out_specs=[row, _cur(tr, d3), row],
        out_shape=[jax.ShapeDtypeStruct((s, d), BF16), jax.ShapeDtypeStruct((s, d3), BF16),
                   jax.ShapeDtypeStruct((s, d), BF16)],
        scratch_shapes=[pltpu.VMEM((h, d), F32), pltpu.VMEM((FFN_NBUF, h + tr, cw), F32)],
        compiler_params=_cparams(("arbitrary",)),
    )(x, g, w, conv_w)


def _od_in_bwd(dy, z, conv_w, w_t, x, g, res, name):
    s, d3 = z.shape
    d = d3 // 3
    tr = min(512, s)
    h = HALO16
    cw = FFN_CW
    ext = tr + h

    def body(dy_ref, dyn_ref, z_ref, zp_ref, zn_ref, w_ref, wt_ref, x_ref, g_ref, res_ref,
             o_ref, dw_ref, dx_ref, dg_ref, buf, dbuf, dhacc):
        i = pl.program_id(0)
        first = i == 0
        last = i == pl.num_programs(0) - 1

        @pl.when(first)
        def _():
            dw_ref[...] = jnp.zeros_like(dw_ref)
            dg_ref[...] = jnp.zeros_like(dg_ref)

        started = False
        for j in range(d // cw):
            cb = slice(j * cw, (j + 1) * cw)
            cc = slice(d + j * cw, d + (j + 1) * cw)
            cu = slice(2 * d + j * cw, 2 * d + (j + 1) * cw)
            bf = buf.at[j % FFN_NBUF]
            db = dbuf.at[j % FFN_NBUF]
            w = w_ref[:, cb]
            cval = z_ref[:, cc].astype(F32)
            uval = z_ref[:, cu].astype(F32)
            bf[0:h, :] = jnp.where(first, 0.0, zp_ref[:, cc].astype(F32) * zp_ref[:, cu].astype(F32))
            bf[h:h + tr, :] = cval * uval
            k = _conv3_taps(bf, w, h - 2, tr)
            dyv = dy_ref[:, cb]
            db[0:tr, :] = dyv * z_ref[:, cb].astype(F32)
            db[tr:ext, :] = jnp.where(last, 0.0, dyn_ref[:, cb] * zn_ref[:, cb].astype(F32))
            dcu = w[2:3] * db[pl.ds(0, tr), :] + w[1:2] * db[pl.ds(1, tr), :] + w[0:1] * db[pl.ds(2, tr), :]
            dk = db[pl.ds(0, tr), :]
            for t in range(3):
                dw_ref[t:t + 1, cb] += jnp.sum(dk * bf[pl.ds(h - 2 + t, tr), :], axis=0, keepdims=True)
            for cs, val in ((cb, dyv * k), (cc, dcu * uval), (cu, dcu * cval)):
                piece = val.astype(BF16)
                o_ref[:, cs] = piece
                part = jnp.dot(piece, wt_ref[cs, :], preferred_element_type=F32)
                if started:
                    dhacc[...] += part
                else:
                    dhacc[...] = part
                    started = True
        dx, dg = _rms_bwd(dhacc[...], x_ref[...], g_ref[...])
        dg_ref[...] += dg
        dx_ref[...] = res_ref[...] + dx

    row = _cur(tr, d)
    vec = _full((1, d))
    return pl.pallas_call(
        body, name=name, grid=(s // tr,),
        in_specs=[row, _next(tr, h, d, s), _cur(tr, d3), _prev(tr, h, d3), _next(tr, h, d3, s), _full((3, d)),
                  WHOLE_VMEM, row, vec, row],
        out_specs=[_cur(tr, d3), _full((3, d)), row, vec],
        out_shape=[jax.ShapeDtypeStruct((s, d3), BF16), jax.ShapeDtypeStruct((3, d), F32),
                   jax.ShapeDtypeStruct((s, d), F32), jax.ShapeDtypeStruct((1, d), F32)],
        scratch_shapes=[pltpu.VMEM((FFN_NBUF, h + tr, cw), F32), pltpu.VMEM((FFN_NBUF, ext, cw), F32),
                        pltpu.VMEM((tr, d), F32)],
        compiler_params=_cparams(("arbitrary",)),
    )(dy, dy, z, z, z, conv_w, w_t, x, g, res)


Q0 = 2 * A_CH
K0 = Q0 + N_Q_HEADS * HEAD_DIM
V0 = K0 + 2 * HEAD_DIM
EVEN_IN = V0 + 2 * HEAD_DIM


def _rope_tables(positions):
    half = ROPE_DIM // 2
    inv_freq = ROPE_THETA ** (-(jnp.arange(half, dtype=F32) * 2.0 / ROPE_DIM))
    ang = positions.astype(F32)[:, None] * inv_freq
    cs = jnp.concatenate([jnp.cos(ang), jnp.sin(ang)], axis=1)
    spread = np.zeros((2 * half, 3 * LANES), np.float32)
    const = np.zeros((1, 3 * LANES), np.float32)
    for lane in range(3 * LANES):
        dim, part = lane % HEAD_DIM, lane // LANES
        if part == 0:
            if dim < ROPE_DIM:
                spread[dim % half, lane] = 1.0
            else:
                const[0, lane] = 1.0
        elif part == 1 and half <= dim < ROPE_DIM:
            spread[half + dim - half, lane] = 1.0
        elif part == 2 and dim < half:
            spread[half + dim, lane] = -1.0
    return jnp.dot(cs, jnp.asarray(spread), precision=lax.Precision.HIGHEST) + jnp.asarray(const)


def _rope_fwd(x, tab):
    c, sa, sb = tab[:, 0:LANES], tab[:, LANES:2 * LANES], tab[:, 2 * LANES:3 * LANES]
    return x * c + pltpu.roll(x, 8, 1) * sa + pltpu.roll(x, LANES - 8, 1) * sb


def _rope_bwd(dy, tab):
    c, sa, sb = tab[:, 0:LANES], tab[:, LANES:2 * LANES], tab[:, 2 * LANES:3 * LANES]
    return dy * c + pltpu.roll(dy * sa, LANES - 8, 1) + pltpu.roll(dy * sb, 8, 1)


def _ln_fwd(c, g, b):
    mu = jnp.mean(c, axis=-1, keepdims=True)
    xc = c - mu
    r = lax.rsqrt(jnp.mean(xc * xc, axis=-1, keepdims=True) + LN_EPS)
    nrm = xc * r
    return nrm, r, nrm * g + b


def _phase_fill(buf, ph, rows):
    for k in range(1, 8):
        ph[k - 1, 0:rows - 8, :] = buf[pl.ds(k, rows - 8), :]


def _phase_rows(buf, ph, off, n, cs):
    k = off % 8
    src = buf if k == 0 else ph.at[k - 1]
    return src[pl.ds(off - k, n), cs]


def _ev_in_fwd(x, g_pre, w_in, tab, conv_w, conv_b, ln_g, ln_b, name):
    s, d = x.shape
    tr = min(512, s)
    h = HALO32
    cw = LANES
    pw = 2 * LANES

    def body(x_ref, gpre_ref, win_ref, tab_ref, w_ref, b_ref, g_ref, lb_ref, h_ref, z_ref, c_ref, a_ref, qkv_ref,
             gbuf, cbuf, gph, carry):
        @pl.when(pl.program_id(0) == 0)
        def _():
            carry[...] = jnp.zeros_like(carry)

        xv = x_ref[...]
        hv = (xv * _rms_scale(xv) * gpre_ref[...]).astype(BF16)
        h_ref[...] = hv

        def project(lo_col, hi_col):
            for c0 in range(lo_col, hi_col, pw):
                cs = slice(c0, c0 + pw)
                z_ref[:, cs] = jnp.dot(hv, win_ref[:, cs], preferred_element_type=F32).astype(BF16)

        project(0, 2 * A_CH)
        glu = z_ref[:, 0:A_CH].astype(F32) * jax.nn.sigmoid(z_ref[:, A_CH:2 * A_CH].astype(F32))
        project(2 * A_CH, EVEN_IN)
        gbuf[0:h, :] = carry[...]
        gbuf[h:h + tr, :] = glu
        carry[...] = glu[tr - h:tr, :]
        _phase_fill(gbuf, gph, h + tr)
        for j in range(A_CH // cw):
            cs = slice(j * cw, (j + 1) * cw)
            acc = jnp.broadcast_to(b_ref[:, cs], (tr, cw))
            for t in range(A_CONV):
                acc = acc + w_ref[t:t + 1, cs] * _phase_rows(gbuf, gph, h - (A_CONV - 1) + t, tr, cs)
            cbuf[:, cs] = acc
        c = cbuf[...]
        c_ref[...] = c.astype(BF16)
        _, _, l = _ln_fwd(c, g_ref[...], lb_ref[...])
        a_ref[...] = (l * jax.nn.sigmoid(l)).astype(BF16)
        tab_v = tab_ref[...]
        for p in range(4):
            xq = z_ref[:, Q0 + p * LANES:Q0 + (p + 1) * LANES].astype(F32)
            qkv_ref[:, p * LANES:(p + 1) * LANES] = _rope_fwd(xq, tab_v).astype(BF16)
        lane = lax.broadcasted_iota(jnp.int32, (tr, LANES), 1)
        lo = lane < HEAD_DIM
        kr = _rope_fwd(z_ref[:, K0:K0 + LANES].astype(F32), tab_v)
        vr = z_ref[:, V0:V0 + LANES].astype(F32)
        for base, val in ((4 * LANES, kr), (6 * LANES, vr)):
            sw = pltpu.roll(val, HEAD_DIM, 1)
            qkv_ref[:, base:base + LANES] = jnp.where(lo, val, sw).astype(BF16)
            qkv_ref[:, base + LANES:base + 2 * LANES] = jnp.where(lo, sw, val).astype(BF16)

    return pl.pallas_call(
        body, name=name, grid=(s // tr,),
        in_specs=[_cur(tr, d), _full((1, d)), WHOLE_VMEM, _cur(tr, 3 * LANES), _full((A_CONV, A_CH)),
                  _full((1, A_CH)), _full((1, A_CH)), _full((1, A_CH))],
        out_specs=[_cur(tr, d), _cur(tr, EVEN_IN), _cur(tr, A_CH), _cur(tr, A_CH), _cur(tr, 2 * A_CH)],
        out_shape=[jax.ShapeDtypeStruct((s, d), BF16), jax.ShapeDtypeStruct((s, EVEN_IN), BF16),
                   jax.ShapeDtypeStruct((s, A_CH), BF16), jax.ShapeDtypeStruct((s, A_CH), BF16),
                   jax.ShapeDtypeStruct((s, 2 * A_CH), BF16)],
        scratch_shapes=[pltpu.VMEM((h + tr, A_CH), F32), pltpu.VMEM((tr, A_CH), F32),
                        pltpu.VMEM((7, h + tr, A_CH), F32), pltpu.VMEM((h, A_CH), F32)],
        compiler_params=_cparams(("arbitrary",)),
    )(x, g_pre, w_in, tab, conv_w, conv_b, ln_g, ln_b)


def _ev_mid_bwd(dcat, c, z, dq, dkv, tab, conv_w, ln_g, ln_b, w_t, x, g_pre, res, name):
    s = z.shape[0]
    tr = min(512, s)
    h = HALO32
    cw = LANES
    ext = tr + h

    def body(da_ref, dan_ref, c_ref, cn_ref, z_ref, dq_ref, dkv_ref, tab_ref, w_ref, g_ref, lb_ref,
             wt_ref, x_ref, gpre_ref, res_ref, dz_ref, dw_ref, dvec_ref, dx_ref, dg_ref, dcbuf, dcph, dhacc):
        i = pl.program_id(0)
        first = i == 0
        last = i == pl.num_programs(0) - 1

        @pl.when(first)
        def _():
            dw_ref[...] = jnp.zeros_like(dw_ref)
            dvec_ref[...] = jnp.zeros_like(dvec_ref)
            dg_ref[...] = jnp.zeros_like(dg_ref)

        started = []

        def dh_part(cs):
            part = jnp.dot(dz_ref[:, cs], wt_ref[cs, :], preferred_element_type=F32)
            if started:
                dhacc[...] += part
            else:
                dhacc[...] = part
                started.append(True)

        tab_v = tab_ref[...]
        for p in range(4):
            cs = slice(p * LANES, (p + 1) * LANES)
            dz_ref[:, Q0 + p * LANES:Q0 + (p + 1) * LANES] = _rope_bwd(dq_ref[:, cs], tab_v).astype(BF16)
        lane = lax.broadcasted_iota(jnp.int32, (tr, LANES), 1)
        lo = lane < HEAD_DIM

        def fold(base):
            p0 = dkv_ref[:, base:base + LANES]
            p1 = dkv_ref[:, base + LANES:base + 2 * LANES]
            s0 = p0 + pltpu.roll(p0, HEAD_DIM, 1)
            s1 = p1 + pltpu.roll(p1, HEAD_DIM, 1)
            return jnp.where(lo, s0, s1)

        dz_ref[:, K0:K0 + LANES] = _rope_bwd(fold(0), tab_v).astype(BF16)
        dz_ref[:, V0:V0 + LANES] = fold(2 * LANES).astype(BF16)
        dh_part(slice(Q0, EVEN_IN))

        gv = g_ref[...]

        def ln_silu_bwd(cv, dav):
            nrm, r, l = _ln_fwd(cv, gv, lb_ref[...])
            sig = jax.nn.sigmoid(l)
            dl = dav * (sig * (1.0 + l * (1.0 - sig)))
            dn = dl * gv
            dc = r * (dn - jnp.mean(dn, axis=-1, keepdims=True) - nrm * jnp.mean(dn * nrm, axis=-1, keepdims=True))
            return dc, dl, nrm

        dc, dl, nrm = ln_silu_bwd(c_ref[...].astype(F32), da_ref[...])
        dcn, _, _ = ln_silu_bwd(cn_ref[...].astype(F32), dan_ref[...])
        dcbuf[0:tr, :] = dc
        dcbuf[tr:ext, :] = jnp.where(last, 0.0, dcn)
        dvec_ref[0:1, :] += jnp.sum(dc, axis=0, keepdims=True)
        dvec_ref[1:2, :] += jnp.sum(dl * nrm, axis=0, keepdims=True)
        dvec_ref[2:3, :] += jnp.sum(dl, axis=0, keepdims=True)

        _phase_fill(dcbuf, dcph, ext)
        a_lin = z_ref[:, 0:A_CH].astype(F32)
        sig_g = jax.nn.sigmoid(z_ref[:, A_CH:2 * A_CH].astype(F32))
        glu = a_lin * sig_g
        for j in range(A_CH // cw):
            cs = slice(j * cw, (j + 1) * cw)
            gluj = glu[:, cs]
            acc = jnp.zeros((tr, cw), F32)
            for t in range(A_CONV):
                dsh = _phase_rows(dcbuf, dcph, A_CONV - 1 - t, tr, cs)
                acc = acc + w_ref[t:t + 1, cs] * dsh
                dw_ref[t:t + 1, cs] += jnp.sum(gluj * dsh, axis=0, keepdims=True)
            dz_ref[:, cs] = (acc * sig_g[:, cs]).astype(BF16)
            dz_ref[:, A_CH + j * cw:A_CH + (j + 1) * cw] = (
                acc * a_lin[:, cs] * sig_g[:, cs] * (1.0 - sig_g[:, cs])).astype(BF16)
            if j % 2 == 1:
                dh_part(slice((j - 1) * cw, (j + 1) * cw))
                dh_part(slice(A_CH + (j - 1) * cw, A_CH + (j + 1) * cw))

        dx, dg = _rms_bwd(dhacc[...], x_ref[...], gpre_ref[...])
        dg_ref[...] += dg
        dx_ref[...] = res_ref[...] + dx

    row = _cur(tr, D_MODEL)
    vec = _full((1, D_MODEL))
    return pl.pallas_call(
        body, name=name, grid=(s // tr,),
        in_specs=[_cur(tr, A_CH), _next(tr, h, A_CH, s), _cur(tr, A_CH), _next(tr, h, A_CH, s),
                  _cur(tr, EVEN_IN), _cur(tr, A_CH), _cur(tr, A_CH), _cur(tr, 3 * LANES),
                  _full((A_CONV, A_CH)), _full((1, A_CH)), _full((1, A_CH)), WHOLE_VMEM, row, vec, row],
        out_specs=[_cur(tr, EVEN_IN), _full((A_CONV, A_CH)), _full((8, A_CH)), row, vec],
        out_shape=[jax.ShapeDtypeStruct((s, EVEN_IN), BF16), jax.ShapeDtypeStruct((A_CONV, A_CH), F32),
                   jax.ShapeDtypeStruct((8, A_CH), F32), jax.ShapeDtypeStruct((s, D_MODEL), F32),
                   jax.ShapeDtypeStruct((1, D_MODEL), F32)],
        scratch_shapes=[pltpu.VMEM((ext, A_CH), F32), pltpu.VMEM((7, ext, A_CH), F32),
                        pltpu.VMEM((tr, D_MODEL), F32)],
        compiler_params=_cparams(("arbitrary",)),
    )(dcat, dcat, c, c, z, dq, dkv, tab, conv_w, ln_g, ln_b, w_t, x, g_pre, res)


NT = (((1,), (1,)), ((), ()))
TN = (((0,), (0,)), ((), ()))
QB = WINDOW
SCALE = HEAD_DIM ** -0.5
ATT_AHEAD = 2


def _att_scores(q2m, kwin):
    return lax.dot_general(q2m, kwin, NT, preferred_element_type=F32)


def _att_probs(raw, sink, mask):
    sc = jnp.where(mask, raw * SCALE, -jnp.inf)
    mx = jnp.maximum(jnp.max(sc, axis=-1, keepdims=True), sink)
    p = jnp.exp(sc - mx)
    ps = jnp.exp(sink - mx)
    inv = 1.0 / (jnp.sum(p, axis=-1, keepdims=True) + ps)
    return p * inv, ps * inv


def _att_mask(i):
    r = lax.broadcasted_iota(jnp.int32, (QB, 2 * QB), 0)
    kc = lax.broadcasted_iota(jnp.int32, (QB, 2 * QB), 1)
    diff = r + QB - kc
    return (diff >= 0) & (diff < WINDOW) & ((kc >= QB) | (i > 0))


def _half_masks(dtype):
    lane = lax.broadcasted_iota(jnp.int32, (1, LANES), 1)
    return (lane < HEAD_DIM).astype(dtype), (lane >= HEAD_DIM).astype(dtype)


def _att_fwd(qkv, a, sinks, name):
    s = qkv.shape[0]
    nb = s // QB

    def body(sink_ref, qkv_ref, kvp_ref, a_ref, o_ref):
        i = pl.program_id(0)
        mask = _att_mask(i)
        mlo, mhi = _half_masks(BF16)
        o_ref[:, 0:A_CH] = a_ref[...]

        def window(col):
            return jnp.concatenate([kvp_ref[:, col * LANES:(col + 1) * LANES],
                                    qkv_ref[:, A_CH + col * LANES:A_CH + (col + 1) * LANES]], axis=0)

        def raw_scores(p):
            q2 = qkv_ref[:, p * LANES:(p + 1) * LANES]
            kwin = window(p // 2)
            return _att_scores(q2 * mlo, kwin), _att_scores(q2 * mhi, kwin)

        ahead = [raw_scores(p) for p in range(4)]
        for p in range(4):
            raw_e, raw_o = ahead[p]
            vwin = window(2 + p // 2)
            pe, _ = _att_probs(raw_e, sink_ref[2 * p], mask)
            po, _ = _att_probs(raw_o, sink_ref[2 * p + 1], mask)
            o = (jnp.dot(pe.astype(BF16), vwin * mlo, preferred_element_type=F32)
                 + jnp.dot(po.astype(BF16), vwin * mhi, preferred_element_type=F32))
            o_ref[:, A_CH + p * LANES:A_CH + (p + 1) * LANES] = o.astype(BF16)

    grid_spec = pltpu.PrefetchScalarGridSpec(
        num_scalar_prefetch=1, grid=(nb,),
        in_specs=[pl.BlockSpec((QB, 2 * A_CH), lambda i, sk: (i, 0)),
                  pl.BlockSpec((QB, A_CH), lambda i, sk: (jnp.maximum(i - 1, 0), 1)),
                  pl.BlockSpec((QB, A_CH), lambda i, sk: (i, 0))],
        out_specs=pl.BlockSpec((QB, 2 * A_CH), lambda i, sk: (i, 0)),
    )
    return pl.pallas_call(
        body, name=name, grid_spec=grid_spec,
        out_shape=jax.ShapeDtypeStruct((s, 2 * A_CH), BF16),
        compiler_params=_cparams(("parallel",)),
    )(sinks, qkv, qkv, a)


def _att_bwd(qkv, dcat, sinks, name):
    s = qkv.shape[0]
    nb = s // QB

    def body(sink_ref, qkv_ref, kvp_ref, do_ref, dq_ref, dkv_ref, ds_ref, carry):
        i = pl.program_id(0)

        @pl.when(i == 0)
        def _():
            ds_ref[...] = jnp.zeros_like(ds_ref)
            carry[...] = jnp.zeros_like(carry)

        @pl.when(i < nb)
        def _():
            mask = _att_mask(i)
            mlo, mhi = _half_masks(BF16)
            dwin = [jnp.zeros((2 * QB, LANES), F32) for _ in range(4)]

            def window(col):
                return jnp.concatenate([kvp_ref[:, col * LANES:(col + 1) * LANES],
                                        qkv_ref[:, A_CH + col * LANES:A_CH + (col + 1) * LANES]], axis=0)

            def first_products(n):
                p, hm = n // 2, (mlo, mhi)[n % 2]
                qm = qkv_ref[:, p * LANES:(p + 1) * LANES] * hm
                dom = do_ref[:, p * LANES:(p + 1) * LANES].astype(BF16) * hm
                kwin, vwin = window(p // 2), window(2 + p // 2)
                return (qm, dom, kwin * hm, _att_scores(qm, kwin),
                        lax.dot_general(dom, vwin, NT, preferred_element_type=F32))

            ahead = [first_products(n) for n in range(ATT_AHEAD)]
            dq2 = None
            for n in range(N_Q_HEADS):
                g = n // 4
                qm, dom, kwm, raw, dp = ahead.pop(0)
                if n + ATT_AHEAD < N_Q_HEADS:
                    ahead.append(first_products(n + ATT_AHEAD))
                prob, psink = _att_probs(raw, sink_ref[n], mask)
                delta = jnp.sum(prob * dp, axis=-1, keepdims=True)
                dsc = (prob * (dp - delta) * SCALE).astype(BF16)
                ds_ref[n:n + 1, :] += jnp.broadcast_to(jnp.sum(-psink * delta, axis=0, keepdims=True), (1, LANES))
                part = jnp.dot(dsc, kwm, preferred_element_type=F32)
                dq2 = part if n % 2 == 0 else dq2 + part
                dwin[g] = dwin[g] + lax.dot_general(dsc, qm, TN, preferred_element_type=F32)
                dwin[2 + g] = dwin[2 + g] + lax.dot_general(prob.astype(BF16), dom, TN, preferred_element_type=F32)
                if n % 2 == 1:
                    dq_ref[:, (n // 2) * LANES:(n // 2 + 1) * LANES] = dq2
            for n in range(4):
                cs = slice(n * LANES, (n + 1) * LANES)
                dkv_ref[:, cs] = carry[:, cs] + dwin[n][0:QB, :]
                carry[:, cs] = dwin[n][QB:2 * QB, :]

        @pl.when(i == nb)
        def _():
            dkv_ref[...] = carry[...]

    grid_spec = pltpu.PrefetchScalarGridSpec(
        num_scalar_prefetch=1, grid=(nb + 1,),
        in_specs=[pl.BlockSpec((QB, 2 * A_CH), lambda i, sk: (jnp.minimum(i, nb - 1), 0)),
                  pl.BlockSpec((QB, A_CH), lambda i, sk: (jnp.maximum(jnp.minimum(i, nb - 1) - 1, 0), 1)),
                  pl.BlockSpec((QB, A_CH), lambda i, sk: (jnp.minimum(i, nb - 1), 1))],
        out_specs=[pl.BlockSpec((QB, A_CH), lambda i, sk: (jnp.minimum(i, nb - 1), 0)),
                   pl.BlockSpec((QB, A_CH), lambda i, sk: (jnp.maximum(i - 1, 0), 0)),
                   pl.BlockSpec((8, LANES), lambda i, sk: (0, 0))],
        scratch_shapes=[pltpu.VMEM((QB, A_CH), F32)],
    )
    return pl.pallas_call(
        body, name=name, grid_spec=grid_spec,
        out_shape=[jax.ShapeDtypeStruct((s, A_CH), F32), jax.ShapeDtypeStruct((s, A_CH), F32),
                   jax.ShapeDtypeStruct((8, LANES), F32)],
        compiler_params=_cparams(("arbitrary",)),
    )(sinks, qkv, qkv, dcat)


def _local_step(x, positions, target, w, fetch=None, emit=None):
    row = lambda a, i: a[i:i + 1]
    tab = _rope_tables(positions)
    g = {}

    def ffn_fwd(xin, i, tgt=None):
        outs = _ffn_fwd(xin, row(w["ffn_norm_pre"], i), w["ffn_w_up", i], w["ffn_conv_w"][i],
                        w["ffn_w_down", i], row(w["ffn_norm_post"], i), f"ffn{i}_fwd", tgt)
        f, h, up, u = outs[-4:]
        return outs[:-4], (xin, f, h, up, u)

    def point(name, after):
        return emit(name, after, g) if emit is not None else 0.0

    def ffn_bwd(dxout, saved, i, tok):
        xin, f, h, up, u = saved
        dxin, dup, act, df, d_cw, dg_post, dg_pre = _ffn_bwd(
            dxout, f, xin, up, u, row(w["ffn_norm_pre"], i), row(w["ffn_norm_post"], i) + tok, w["ffn_w_down_t", i],
            w["ffn_w_up_t", i], w["ffn_conv_w"][i], f"ffn{i}_bwd")
        tok = point(f"ffn{i}_bwd_done", dxin)
        g["ffn_w_down", i] = _mm_tn(act, df, f"ffn{i}_down_dw")
        g["ffn_w_up", i] = _mm_tn(dup, h, f"ffn{i}_up_dw")
        return dxin, tok, dict(ffn_norm_post=dg_post, ffn_norm_pre=dg_pre, ffn_conv_w=d_cw)

    h0, z0, c0, a0, qkv = _ev_in_fwd(x, row(w["mix_norm_pre"], 0), w["ev_w_in"], tab, w["ev_a_conv_w"],
                                     w["ev_a_conv_b"], w["ev_a_ln_g"], w["ev_a_ln_b"], "ev_in")
    cat = _att_fwd(qkv, a0, w["ev_sinks"], "ev_att")
    m0, x1 = _mm_post(cat, w["ev_w_out"], row(w["mix_norm_post"], 0), x, "ev_out")
    if fetch is not None:
        w = {**w, **fetch("ffn0", x1)}
    (x2,), ffn0 = ffn_fwd(x1, 0)
    if fetch is not None:
        w = {**w, **fetch("layer1", x2)}
    h2, z1, y1 = _od_in_fwd(x2, row(w["mix_norm_pre"], 1), w["od_w_in"], w["od_conv_w"], "od_in")
    m1, x3 = _mm_post(y1, w["od_w_out"], row(w["mix_norm_post"], 1), x2, "od_out")
    (dx4, sq), ffn1 = ffn_fwd(x3, 1, target)

    dx3, _, gf1 = ffn_bwd(dx4, ffn1, 1, 0.0)
    dm1, dy1, dg_mo1 = _mm_post_bwd(dx3, m1, row(w["mix_norm_post"], 1), w["od_w_out_t"], "od_out_bwd")
    g["od_w_out"] = _mm_tn(y1, dm1, "od_out_dw")
    dz1, g["od_conv_w"], dx2, dg_mp1 = _od_in_bwd(dy1, z1, w["od_conv_w"], w["od_w_in_t"], x2,
                                                  row(w["mix_norm_pre"], 1), dx3, "od_in_bwd")
    g["od_w_in"] = _mm_tn(dz1, h2, "od_in_dw")
    tok = point("layer1_grads", dx2)

    dx1, tok, gf0 = ffn_bwd(dx2, ffn0, 0, tok)
    tok = tok + point("ffn0_grads", dx1)
    dm0, dcat, dg_mo0 = _mm_post_bwd(dx1, m0, row(w["mix_norm_post"], 0) + tok, w["ev_w_out_t"], "ev_out_bwd")
    tok = point("ev_out_bwd_done", dcat)
    g["ev_w_out"] = _mm_tn(cat, dm0, "ev_out_dw")
    dq, dkv, dsk = _att_bwd(qkv, dcat, w["ev_sinks"] + tok, "ev_att_bwd")
    tok = point("ev_att_bwd_done", dq)
    dz0, g["ev_a_conv_w"], dvec, dx0, dg_mp0 = _ev_mid_bwd(
        dcat, c0, z0, dq, dkv, tab, w["ev_a_conv_w"], w["ev_a_ln_g"] + tok, w["ev_a_ln_b"], w["ev_w_in_t"], x,
        row(w["mix_norm_pre"], 0), dx1, "ev_in_bwd")
    point("ev_mid_bwd_done", dz0)
    g["ev_w_in"] = _mm_tn(dz0, h0, "ev_in_dw")

    g["ev_a_conv_b"] = dvec[0:1]
    g["ev_a_ln_g"] = dvec[1:2]
    g["ev_a_ln_b"] = dvec[2:3]
    g["ev_sinks"] = dsk[:, 0]
    g["mix_norm_pre"] = jnp.concatenate([dg_mp0, dg_mp1], axis=0)
    g["mix_norm_post"] = jnp.concatenate([dg_mo0, dg_mo1], axis=0)
    g["ffn_norm_pre"] = jnp.concatenate([gf0["ffn_norm_pre"], gf1["ffn_norm_pre"]], axis=0)
    g["ffn_norm_post"] = jnp.concatenate([gf0["ffn_norm_post"], gf1["ffn_norm_post"]], axis=0)
    g["ffn_conv_w"] = jnp.stack([gf0["ffn_conv_w"], gf1["ffn_conv_w"]], axis=0)
    return sq, dx0, g


ANY = pl.BlockSpec(memory_space=pl.ANY)
PACK_COLS = 1024


def _me():
    return lax.axis_index("x"), lax.axis_index("y"), lax.axis_index("c")


def _other_chips(x, y):
    return [(1 - x, y), (x, 1 - y), (1 - x, 1 - y)]


def _remote(src, dst, send, recv, dev):
    return pltpu.make_async_remote_copy(src_ref=src, dst_ref=dst, send_sem=send, recv_sem=recv,
                                        device_id=dev, device_id_type=MESH)


def _gather_chips(wp, name):
    r, cols = wp.shape
    rh = r // 2

    def body(w_ref, o_ref, send, recv):
        x, y, c = _me()
        p = 2 * x + y
        sib = (x, y, 1 - c)
        chips = _other_chips(x, y)
        half = pl.ds(c * rh, rh)
        other = pl.ds((1 - c) * rh, rh)
        sent = [_remote(w_ref.at[half], o_ref.at[p, half], send.at[k], recv.at[k], (cx, cy, c))
                for k, (cx, cy) in enumerate(chips)]
        for cp in sent:
            cp.start()
        for k, (cx, cy) in enumerate(chips):
            q = 2 * cx + cy
            _remote(w_ref.at[half], o_ref.at[q, half], send.at[k], recv.at[k], (cx, cy, c)).wait_recv()
            fwd = _remote(o_ref.at[q, half], o_ref.at[q, half], send.at[3 + k], recv.at[3 + k], sib)
            fwd.start()
            sent.append(fwd)
        for k, (cx, cy) in enumerate(chips):
            q = 2 * cx + cy
            _remote(o_ref.at[q, other], o_ref.at[q, other], send.at[3 + k], recv.at[3 + k], sib).wait_recv()
        for cp in sent:
            cp.wait_send()

    return pl.pallas_call(
        body, name=name, in_specs=[ANY], out_specs=ANY,
        out_shape=jax.ShapeDtypeStruct((N_CHIPS, r, cols), wp.dtype),
        scratch_shapes=[pltpu.SemaphoreType.DMA((6,)), pltpu.SemaphoreType.DMA((6,))],
    )(wp)


HBM_SPEC = pl.BlockSpec(memory_space=pltpu.HBM)
SEM_SPEC = pl.BlockSpec(memory_space=pltpu.SEMAPHORE)
DATAFLOW = pltpu.SideEffectType.DATAFLOW_SIDE_EFFECTING


def _gather_plan(w_ref, land_ref):
    x, y, c = _me()
    return [(w_ref, land_ref.at[2 * x + y], (cx, cy, c)) for cx, cy in _other_chips(x, y)]


def _copies_start(src, land_shape, plan, n, name):
    def body(src_ref, land_ref, send, recv, src_thru, land_thru, token):
        for k, (s_view, d_view, dev) in enumerate(plan(src_ref, land_ref)):
            _remote(s_view, d_view, send.at[k], recv.at[k], dev).start()
        token[...] = jnp.zeros_like(token)

    return pl.pallas_call(
        body, name=name,
        out_shape=(pltpu.SemaphoreType.DMA((n,)), pltpu.SemaphoreType.DMA((n,)), pltpu.HBM(src.shape, src.dtype),
                   pltpu.HBM(land_shape, src.dtype), jax.ShapeDtypeStruct((8, LANES), F32)),
        in_specs=(HBM_SPEC, HBM_SPEC),
        out_specs=(SEM_SPEC, SEM_SPEC, HBM_SPEC, HBM_SPEC, pl.BlockSpec(memory_space=pltpu.VMEM)),
        input_output_aliases={0: 2, 1: 3},
        compiler_params=pltpu.CompilerParams(has_side_effects=DATAFLOW),
    )(pltpu.with_memory_space_constraint(src, pltpu.HBM),
      pltpu.with_memory_space_constraint(lax.empty(land_shape, src.dtype), pltpu.HBM))


def _copies_wait(started, after, plan, name):
    send, recv, src_thru, land_thru, _ = started

    def body(src_ref, land_ref, send, recv, after_ref, src_dead, land_out):
        for k, (s_view, d_view, dev) in enumerate(plan(src_ref, land_ref)):
            cp = _remote(s_view, d_view, send.at[k], recv.at[k], dev)
            cp.wait_send()
            cp.wait_recv()

    return pl.pallas_call(
        body, name=name,
        out_shape=(pltpu.HBM(src_thru.shape, src_thru.dtype), pltpu.HBM(land_thru.shape, land_thru.dtype)),
        in_specs=(HBM_SPEC, HBM_SPEC, SEM_SPEC, SEM_SPEC, ANY),
        out_specs=(HBM_SPEC, HBM_SPEC),
        input_output_aliases={0: 0, 1: 1},
        compiler_params=pltpu.CompilerParams(has_side_effects=DATAFLOW),
    )(src_thru, land_thru, send, recv, after)


def _swap_plan(g_ref, land_ref):
    x, y, c = _me()
    return [(g_ref.at[q, 1 - c], land_ref.at[q], (x, y, 1 - c)) for q in range(N_CHIPS)]


def _ici_plan(a_ref, land_ref):
    x, y, c = _me()
    return [(a_ref.at[2 * cx + cy], land_ref.at[2 * x + y], (cx, cy, c)) for cx, cy in _other_chips(x, y)]


def _share_plan(h_ref, land_ref):
    x, y, c = _me()
    return [(h_ref, land_ref, (x, y, 1 - c))]


def _exchange8(v, reduce, name):
    r, cols = v.shape
    rel = [(a, b, d) for a in (0, 1) for b in (0, 1) for d in (0, 1) if (a, b, d) != (0, 0, 0)]

    def body(v_ref, o_ref, *rest):
        if reduce:
            gbuf, send, recv = rest
        else:
            gbuf = o_ref
            send, recv = rest
        x, y, c = _me()
        me = 4 * x + 2 * y + c
        gbuf[me] = v_ref[...]
        sent = []
        for k, (a, b, d) in enumerate(rel):
            cp = _remote(v_ref, gbuf.at[me], send.at[k], recv.at[k], ((x + a) % 2, (y + b) % 2, (c + d) % 2))
            cp.start()
            sent.append(cp)
        for k, (a, b, d) in enumerate(rel):
            src = 4 * ((x + a) % 2) + 2 * ((y + b) % 2) + (c + d) % 2
            _remote(v_ref, gbuf.at[src], send.at[k], recv.at[k], (x, y, c)).wait_recv()
        for cp in sent:
            cp.wait_send()
        if reduce:
            acc = gbuf[0]
            for n in range(1, 8):
                acc = acc + gbuf[n]
            o_ref[...] = acc

    vmem = pl.BlockSpec(memory_space=pltpu.VMEM)
    sems = [pltpu.SemaphoreType.DMA((7,)), pltpu.SemaphoreType.DMA((7,))]
    if reduce:
        out_shape = jax.ShapeDtypeStruct((r, cols), F32)
        scratch = [pltpu.VMEM((8, r, cols), F32)] + sems
    else:
        out_shape = jax.ShapeDtypeStruct((8, r, cols), F32)
        scratch = sems
    return pl.pallas_call(body, name=name, in_specs=[vmem], out_specs=vmem, out_shape=out_shape,
                          scratch_shapes=scratch)(v)


def _rs_swap(g, name):
    _, _, rh, cols = g.shape

    def body(g_ref, o_ref, send, recv):
        x, y, c = _me()
        cps = [_remote(g_ref.at[q, 1 - c], o_ref.at[q], send.at[q], recv.at[q], (x, y, 1 - c)) for q in range(N_CHIPS)]
        for cp in cps:
            cp.start()
        for cp in cps:
            cp.wait()

    return pl.pallas_call(
        body, name=name, in_specs=[ANY], out_specs=ANY,
        out_shape=jax.ShapeDtypeStruct((N_CHIPS, rh, cols), F32),
        scratch_shapes=[pltpu.SemaphoreType.DMA((N_CHIPS,)), pltpu.SemaphoreType.DMA((N_CHIPS,))],
    )(g)


def _row_tile(rows, pref, mult=8):
    if rows <= pref:
        return rows
    t = (pref // mult) * mult
    while t >= mult:
        if rows % t == 0:
            return t
        t -= mult
    return rows


def _rs_add(g, sib, c, name):
    _, _, rh, cols = g.shape
    tr = _row_tile(rh, 512, 16)

    def body(c_ref, g_ref, s_ref, o_ref):
        o_ref[...] = (g_ref[...] + s_ref[...]).astype(BF16)

    grid_spec = pltpu.PrefetchScalarGridSpec(
        num_scalar_prefetch=1, grid=(N_CHIPS, rh // tr),
        in_specs=[pl.BlockSpec((None, None, tr, cols), lambda q, i, cr: (q, cr[0], i, 0)),
                  pl.BlockSpec((None, tr, cols), lambda q, i, cr: (q, i, 0))],
        out_specs=pl.BlockSpec((None, tr, cols), lambda q, i, cr: (q, i, 0)),
    )
    return pl.pallas_call(
        body, name=name, grid_spec=grid_spec,
        out_shape=jax.ShapeDtypeStruct((N_CHIPS, rh, cols), BF16),
        compiler_params=_cparams(("parallel", "parallel")),
    )(c, g, sib)


def _rs_ici(a, name):
    _, rh, cols = a.shape

    def body(a_ref, o_ref, send, recv):
        x, y, c = _me()
        p = 2 * x + y
        cps = []
        for k, (cx, cy) in enumerate(_other_chips(x, y)):
            cp = _remote(a_ref.at[2 * cx + cy], o_ref.at[p], send.at[k], recv.at[k], (cx, cy, c))
            cp.start()
            cps.append(cp)
        for k, (cx, cy) in enumerate(_other_chips(x, y)):
            q = 2 * cx + cy
            _remote(a_ref.at[q], o_ref.at[q], send.at[k], recv.at[k], (cx, cy, c)).wait_recv()
        for cp in cps:
            cp.wait_send()

    return pl.pallas_call(
        body, name=name, in_specs=[ANY], out_specs=ANY,
        out_shape=jax.ShapeDtypeStruct((N_CHIPS, rh, cols), a.dtype),
        scratch_shapes=[pltpu.SemaphoreType.DMA((3,)), pltpu.SemaphoreType.DMA((3,))],
    )(a)


def _rs_sum(rb, a, chip, name):
    _, rh, cols = rb.shape
    tr = _row_tile(rh, 512, 16)

    def body(p_ref, r0, r1, r2, r3, own, o_ref):
        p = p_ref[0]
        ownv = own[...].astype(F32)
        acc = None
        for q, r in enumerate((r0, r1, r2, r3)):
            v = jnp.where(p == q, ownv, r[...].astype(F32))
            acc = v if acc is None else acc + v
        o_ref[...] = acc

    def spec(q):
        return pl.BlockSpec((None, tr, cols), lambda i, pr: (jnp.where(pr[0] == q, (q + 1) % N_CHIPS, q), i, 0))

    grid_spec = pltpu.PrefetchScalarGridSpec(
        num_scalar_prefetch=1, grid=(rh // tr,),
        in_specs=[spec(0), spec(1), spec(2), spec(3), pl.BlockSpec((None, tr, cols), lambda i, pr: (pr[0], i, 0))],
        out_specs=pl.BlockSpec((tr, cols), lambda i, pr: (i, 0)),
    )
    return pl.pallas_call(
        body, name=name, grid_spec=grid_spec,
        out_shape=jax.ShapeDtypeStruct((rh, cols), F32),
        compiler_params=_cparams(("parallel",)),
    )(chip, rb, rb, rb, rb, a)


def _rs_share(hsum, name):
    rh, cols = hsum.shape

    def body(h_ref, o_ref, send, recv):
        x, y, c = _me()
        cp = _remote(h_ref, o_ref, send, recv, (x, y, 1 - c))
        cp.start()
        cp.wait()

    return pl.pallas_call(
        body, name=name, in_specs=[ANY], out_specs=ANY,
        out_shape=jax.ShapeDtypeStruct((rh, cols), F32),
        scratch_shapes=[pltpu.SemaphoreType.DMA, pltpu.SemaphoreType.DMA],
    )(hsum)


def _adamw(w, g, m, v, name):
    rows, cols = w.shape
    tr = _row_tile(rows, 512)

    def body(w_ref, g_ref, m_ref, v_ref, d_ref, nm_ref, nv_ref):
        gv = g_ref[...]
        nm = ADAM_B1 * m_ref[...] + (1.0 - ADAM_B1) * gv
        nv = ADAM_B2 * v_ref[...] + (1.0 - ADAM_B2) * (gv * gv)
        m_hat = nm / (1.0 - ADAM_B1 ** ADAM_STEP)
        v_hat = nv / (1.0 - ADAM_B2 ** ADAM_STEP)
        d_ref[...] = -ADAM_LR * (m_hat / (jnp.sqrt(v_hat) + ADAM_EPS) + ADAM_WD * w_ref[...])
        nm_ref[...] = nm
        nv_ref[...] = nv

    spec = pl.BlockSpec((tr, cols), lambda i: (i, 0))
    shp = jax.ShapeDtypeStruct((rows, cols), F32)
    return pl.pallas_call(
        body, name=name, grid=(rows // tr,), in_specs=[spec] * 4, out_specs=[spec] * 3, out_shape=[shp] * 3,
        compiler_params=_cparams(("parallel",)),
    )(w, g, m, v)


WEIGHTS = ("mix_norm_pre", "mix_norm_post", "ffn_norm_pre", "ffn_norm_post", "ev_w_in", "ev_a_conv_w", "ev_a_conv_b",
           "ev_a_ln_g", "ev_a_ln_b", "ev_sinks", "ev_w_out", "od_w_in", "od_conv_w", "od_w_out", "ffn_w_up",
           "ffn_conv_w", "ffn_w_down")
MATS = (("ev_w_in", 2), ("ev_w_out", 1), ("od_w_in", 2), ("od_w_out", 1), ("ffn_w_up", 2), ("ffn_w_down", 1))
UNITS = (("ev_w_in", 0, 2), ("ev_w_out", 0, 1), ("ffn_w_up", 0, 2), ("ffn_w_down", 0, 1),
         ("od_w_in", 0, 2), ("od_w_out", 0, 1), ("ffn_w_up", 1, 2), ("ffn_w_down", 1, 1))
GATHER_GROUPS = ((0, 1), (2, 3), (4, 5, 6, 7))
REDUCE_GROUPS = {"layer1": (4, 5, 6, 7), "ffn0": (2, 3), "ev": (0, 1)}
SMALL_SHARDED = ("ev_a_conv_w", "od_conv_w", "ffn_conv_w")
REPLICATED = ("mix_norm_pre", "mix_norm_post", "ffn_norm_pre", "ffn_norm_post", "ev_a_conv_b", "ev_a_ln_g",
              "ev_a_ln_b", "ev_sinks")


def _pack(parts, rows_multiple):
    flat = jnp.concatenate([p.reshape(-1) for p in parts])
    unit = rows_multiple * PACK_COLS
    pad = (-flat.shape[0]) % unit
    if pad:
        flat = jnp.concatenate([flat, jnp.zeros((pad,), flat.dtype)])
    return flat.reshape(-1, PACK_COLS)


def _unpack(buf, shapes):
    flat = buf.reshape(-1)
    out, off = [], 0
    for shp in shapes:
        n = 1
        for d in shp:
            n *= d
        out.append(flat[off:off + n].reshape(shp))
        off += n
    return out


def _shard_rows(shard, axis):
    if axis == 2:
        shard = jnp.swapaxes(shard, 1, 2)
    return shard.reshape(-1, PACK_COLS)


def kernel(x, positions, mix_norm_pre, mix_norm_post, ffn_norm_pre, ffn_norm_post, ev_w_in, ev_a_conv_w, ev_a_conv_b, ev_a_ln_g, ev_a_ln_b, ev_sinks, ev_w_out, od_w_in, od_conv_w, od_w_out, ffn_w_up, ffn_conv_w, ffn_w_down, loss_target, m_mix_norm_pre, m_mix_norm_post, m_ffn_norm_pre, m_ffn_norm_post, m_ev_w_in, m_ev_a_conv_w, m_ev_a_conv_b, m_ev_a_ln_g, m_ev_a_ln_b, m_ev_sinks, m_ev_w_out, m_od_w_in, m_od_conv_w, m_od_w_out, m_ffn_w_up, m_ffn_conv_w, m_ffn_w_down, v_mix_norm_pre, v_mix_norm_post, v_ffn_norm_pre, v_ffn_norm_post, v_ev_w_in, v_ev_a_conv_w, v_ev_a_conv_b, v_ev_a_ln_g, v_ev_a_ln_b, v_ev_sinks, v_ev_w_out, v_od_w_in, v_od_conv_w, v_od_w_out, v_ffn_w_up, v_ffn_conv_w, v_ffn_w_down):
    wts = dict(zip(WEIGHTS, (mix_norm_pre, mix_norm_post, ffn_norm_pre, ffn_norm_post, ev_w_in, ev_a_conv_w, ev_a_conv_b,
                             ev_a_ln_g, ev_a_ln_b, ev_sinks, ev_w_out, od_w_in, od_conv_w, od_w_out, ffn_w_up, ffn_conv_w,
                             ffn_w_down)))
    mom = dict(zip(WEIGHTS, (m_mix_norm_pre, m_mix_norm_post, m_ffn_norm_pre, m_ffn_norm_post, m_ev_w_in, m_ev_a_conv_w,
                             m_ev_a_conv_b, m_ev_a_ln_g, m_ev_a_ln_b, m_ev_sinks, m_ev_w_out, m_od_w_in, m_od_conv_w,
                             m_od_w_out, m_ffn_w_up, m_ffn_conv_w, m_ffn_w_down)))
    var = dict(zip(WEIGHTS, (v_mix_norm_pre, v_mix_norm_post, v_ffn_norm_pre, v_ffn_norm_post, v_ev_w_in, v_ev_a_conv_w,
                             v_ev_a_conv_b, v_ev_a_ln_g, v_ev_a_ln_b, v_ev_sinks, v_ev_w_out, v_od_w_in, v_od_conv_w,
                             v_od_w_out, v_ffn_w_up, v_ffn_conv_w, v_ffn_w_down)))
    xi, yi, ci = _me()
    chip = 2 * xi + yi

    unit_rows = [_shard_rows(wts[k][l:l + 1].astype(BF16), axis) for k, l, axis in UNITS]

    def group_block(group):
        return jnp.concatenate([unit_rows[u] for u in group], axis=0)

    def unpack_group(group, landed, own):
        full = lax.dynamic_update_slice(landed, own[None], (chip, 0, 0))
        out, off = {}, 0
        for u in group:
            k, l, axis = UNITS[u]
            n = unit_rows[u].shape[0]
            native = full[:, off:off + n].reshape(N_CHIPS * n, PACK_COLS)
            off += n
            key = (lambda name: (name, l)) if k.startswith("ffn") else (lambda name: name)
            out[key(k + "_t" if axis == 2 else k)] = native
            out[key(k if axis == 2 else k + "_t")] = native.T
        return out

    small_shapes = [wts[k].shape for k in SMALL_SHARDED]
    small_all = _exchange8(_pack([wts[k] for k in SMALL_SHARDED], 8), False, "gather_small")
    blocks = [group_block(grp) for grp in GATHER_GROUPS]
    first = _gather_chips(blocks[0], "gather_mats")
    later = {}
    for stage, grp, blk in zip(("ffn0", "layer1"), GATHER_GROUPS[1:], blocks[1:]):
        later[stage] = (grp, blk, _copies_start(blk, (N_CHIPS,) + blk.shape, _gather_plan, 3, "gather_" + stage + "_start"))

    def fetch(stage, after):
        grp, blk, started = later[stage]
        own, landed = _copies_wait(started, after, _gather_plan, "gather_" + stage + "_wait")
        return unpack_group(grp, landed, own)

    w = {k: wts[k] for k in REPLICATED}
    w.update(unpack_group(GATHER_GROUPS[0], first, blocks[0]))
    per_chip = [_unpack(small_all[2 * q], small_shapes) for q in range(N_CHIPS)]
    for n, k in enumerate(SMALL_SHARDED):
        w[k] = jnp.concatenate([per_chip[q][n] for q in range(N_CHIPS)], axis=-1)
    for k in ("ev_a_conv_w", "od_conv_w"):
        w[k] = w[k][0]
    w["ev_sinks"] = w["ev_sinks"][0]
    w["mix_norm_pre"] = w["mix_norm_pre"] + sum(later[s][2][4][0, 0] for s in later)

    core = jnp.reshape(ci, (1,)).astype(jnp.int32)
    chip_arr = jnp.reshape(chip, (1,)).astype(jnp.int32)
    per_layer = {}

    def group_grads(group, g):
        gp = jnp.concatenate([(g[k, l] if k.startswith("ffn") else g[k]).reshape(N_CHIPS, -1, PACK_COLS)
                              for k, l, _ in (UNITS[u] for u in group)], axis=1)
        return gp.reshape(N_CHIPS, 2, gp.shape[1] // 2, PACK_COLS)

    def finish(group, half, other):
        red = jnp.concatenate([jnp.where(ci == 0, half, other), jnp.where(ci == 0, other, half)], axis=0)
        off = 0
        for u in group:
            k, l, axis = UNITS[u]
            n = unit_rows[u].shape[0]
            part = red[off:off + n]
            off += n
            per_layer[k, l] = part.T if axis == 2 else part

    chains = {}

    def chain_step(tag, after, g):
        group = REDUCE_GROUPS[tag]
        st = chains.setdefault(tag, {"step": 0})
        step = st["step"]
        st["step"] = step + 1
        if step == 0:
            gp = group_grads(group, g)
            rh = gp.shape[2]
            st["swap"] = _copies_start(gp, (N_CHIPS, rh, PACK_COLS), _swap_plan, N_CHIPS, f"rs_{tag}_swap_start")
            return st["swap"][4][0, 0]
        if step == 1:
            gp, sib = _copies_wait(st["swap"], after, _swap_plan, f"rs_{tag}_swap_wait")
            pair = _rs_add(gp, sib, core, f"rs_{tag}_add")
            st["ici"] = _copies_start(pair, pair.shape, _ici_plan, 3, f"rs_{tag}_ici_start")
            return st["ici"][4][0, 0]
        if step == 2:
            pair, landed = _copies_wait(st["ici"], after, _ici_plan, f"rs_{tag}_ici_wait")
            half = _rs_sum(landed, pair, chip_arr, f"rs_{tag}_sum")
            st["share"] = _copies_start(half, half.shape, _share_plan, 1, f"rs_{tag}_share_start")
            return st["share"][4][0, 0]
        half, other = _copies_wait(st["share"], after, _share_plan, f"rs_{tag}_share_wait")
        finish(group, half, other)
        return 0.0

    schedule = {"layer1_grads": ("layer1",), "ffn0_bwd_done": ("layer1",), "ffn0_grads": ("ffn0",),
                "ev_out_bwd_done": ("layer1", "ffn0"), "ev_att_bwd_done": ("layer1", "ffn0"),
                "ev_mid_bwd_done": ("ffn0",)}

    def emit(place, after, g):
        return sum(chain_step(tag, after, g) for tag in schedule.get(place, ()))

    sq, dx, g = _local_step(x[0], positions[0], loss_target[0], w, fetch, emit)
    loss = lax.psum(0.5 * jnp.sum(sq) / D_MODEL, ("x", "y", "c"))

    gp = group_grads(REDUCE_GROUPS["ev"], g)
    sib = _rs_swap(gp, "rs_swap")
    pair = _rs_add(gp, sib, core, "rs_add")
    landed = _rs_ici(pair, "rs_ici")
    half = _rs_sum(landed, pair, chip_arr, "rs_sum")
    finish(REDUCE_GROUPS["ev"], half, _rs_share(half, "rs_share"))
    grads = {k: jnp.stack([per_layer[k, l] for l in range(wts[k].shape[0])], axis=0) for k, _ in MATS}

    small_keys = REPLICATED + SMALL_SHARDED
    full_shapes = [wts[k].shape for k in REPLICATED] + [wts[k].shape[:-1] + (wts[k].shape[-1] * N_CHIPS,) for k in SMALL_SHARDED]
    sm = _exchange8(_pack([g[k] for k in small_keys], 8), True, "reduce_small")
    for k, full in zip(small_keys, _unpack(sm, full_shapes)):
        if k in SMALL_SHARDED:
            n = wts[k].shape[-1]
            full = lax.dynamic_slice_in_dim(full, chip * n, n, axis=full.ndim - 1)
        grads[k] = full

    deltas, new_m, new_v = {}, {}, {}
    for k in WEIGHTS:
        shp = wts[k].shape
        two_d = (-1, shp[-1])
        d, nm, nv = _adamw(wts[k].reshape(two_d), grads[k].reshape(two_d), mom[k].reshape(two_d), var[k].reshape(two_d),
                           "adamw_" + k)
        deltas[k], new_m[k], new_v[k] = d.reshape(shp), nm.reshape(shp), nv.reshape(shp)

    return (loss, dx[None], *[grads[k] for k in WEIGHTS], *[deltas[k] for k in WEIGHTS],
            *[new_m[k] for k in WEIGHTS], *[new_v[k] for k in WEIGHTS])
```

```python
import functools

import jax
import jax.numpy as jnp
import numpy as np
from jax import lax
from jax.experimental import pallas as pl
from jax.experimental.pallas import tpu as pltpu

F32 = jnp.float32
BF16 = jnp.bfloat16
MESH = pl.DeviceIdType.MESH

D_MODEL = 1024
HEAD_DIM = 64
A_CH = 512
A_CONV = 31
N_Q_HEADS = 8
WINDOW = 128
ROPE_THETA = 500000.0
ROPE_DIM = 16
D_FF = 2816
RMS_EPS = 1e-6
LN_EPS = 1e-5
ADAM_LR = 0.001
ADAM_B1 = 0.9
ADAM_B2 = 0.999
ADAM_EPS = 1e-08
ADAM_WD = 0.01
ADAM_STEP = 10

LANES = 128
HALO16 = 16
HALO32 = 32
VMEM_LIMIT = 56 * 1024 * 1024
FFN_BWD_VMEM = 60 * 1024 * 1024
N_CHIPS = 4


def _cparams(sem):
    return pltpu.CompilerParams(dimension_semantics=sem, vmem_limit_bytes=VMEM_LIMIT)


def _tile(n, pref):
    if n <= pref:
        return n
    t = (pref // LANES) * LANES
    while t >= LANES:
        if n % t == 0:
            return t
        t -= LANES
    return n


MM_ROWS = 512


def _rms_scale(v):
    return lax.rsqrt(jnp.mean(v * v, axis=-1, keepdims=True) + RMS_EPS)


def _rms_bwd(dy, v, g):
    r = _rms_scale(v)
    nrm = v * r
    dn = dy * g
    return r * (dn - nrm * jnp.mean(dn * nrm, axis=-1, keepdims=True)), jnp.sum(dy * nrm, axis=0, keepdims=True)


def _mm_post(a, w, g, xres, name):
    s, k = a.shape
    d = w.shape[1]
    tm = min(MM_ROWS, s)

    def body(a_ref, w_ref, g_ref, x_ref, m_ref, o_ref):
        mv = jnp.dot(a_ref[...], w_ref[...], preferred_element_type=F32)
        m_ref[...] = mv
        o_ref[...] = x_ref[...] + mv * _rms_scale(mv) * g_ref[...]

    row = pl.BlockSpec((tm, d), lambda i: (i, 0))
    return pl.pallas_call(
        body, name=name, grid=(s // tm,),
        in_specs=[pl.BlockSpec((tm, k), lambda i: (i, 0)), _full((k, d)), _full((1, d)), row],
        out_specs=[row, row],
        out_shape=[jax.ShapeDtypeStruct((s, d), F32), jax.ShapeDtypeStruct((s, d), F32)],
        compiler_params=_cparams(("parallel",)),
    )(a, w, g, xres)


def _mm_post_bwd(dy, m, g, w_t, name):
    s, d = m.shape
    k = w_t.shape[1]
    tm = min(MM_ROWS, s)

    def body(dy_ref, m_ref, g_ref, wt_ref, dm_ref, da_ref, dg_ref):
        @pl.when(pl.program_id(0) == 0)
        def _():
            dg_ref[...] = jnp.zeros_like(dg_ref)

        dm, dg = _rms_bwd(dy_ref[...], m_ref[...], g_ref[...])
        dg_ref[...] += dg
        dmb = dm.astype(BF16)
        dm_ref[...] = dmb
        da_ref[...] = jnp.dot(dmb, wt_ref[...], preferred_element_type=F32)

    row = pl.BlockSpec((tm, d), lambda i: (i, 0))
    return pl.pallas_call(
        body, name=name, grid=(s // tm,),
        in_specs=[row, row, _full((1, d)), _full((d, k))],
        out_specs=[row, pl.BlockSpec((tm, k), lambda i: (i, 0)), _full((1, d))],
        out_shape=[jax.ShapeDtypeStruct((s, d), BF16), jax.ShapeDtypeStruct((s, k), F32),
                   jax.ShapeDtypeStruct((1, d), F32)],
        compiler_params=_cparams(("arbitrary",)),
    )(dy, m, g, w_t)


def _mm_tn(a, b, name):
    s, k = a.shape
    _, n = b.shape
    tk = _tile(k, 1408)
    tn = _tile(n, 1408)
    ts = min(2048, s)

    def body(a_ref, b_ref, o_ref):
        @pl.when(pl.program_id(2) == 0)
        def _():
            o_ref[...] = jnp.zeros_like(o_ref)

        o_ref[...] += lax.dot_general(a_ref[...], b_ref[...], (((0,), (0,)), ((), ())),
                                      preferred_element_type=F32)

    return pl.pallas_call(
        body, name=name, grid=(k // tk, n // tn, s // ts),
        in_specs=[pl.BlockSpec((ts, tk), lambda i, j, l: (l, i)), pl.BlockSpec((ts, tn), lambda i, j, l: (l, j))],
        out_specs=pl.BlockSpec((tk, tn), lambda i, j, l: (i, j)),
        out_shape=jax.ShapeDtypeStruct((k, n), F32),
        compiler_params=_cparams(("parallel", "parallel", "arbitrary")),
    )(a, b)


def _cur(tr, w, col=0):
    return pl.BlockSpec((tr, w), lambda i: (i, col))


def _prev(tr, h, w, col=0):
    return pl.BlockSpec((h, w), lambda i: (jnp.maximum(i * (tr // h) - 1, 0), col))


def _next(tr, h, w, nrows, col=0):
    last = nrows // h - 1
    return pl.BlockSpec((h, w), lambda i: (jnp.minimum((i + 1) * (tr // h), last), col))


def _full(shape):
    return pl.BlockSpec(shape, lambda i: tuple(0 for _ in shape))


def _silu_parts(g):
    sig = jax.nn.sigmoid(g)
    return sig, g * sig


FFN_CW = 256
FFN_NBUF = 3


def _conv3_taps(buf, w, off, rows):
    return (w[0:1] * buf[pl.ds(off, rows), :] + w[1:2] * buf[pl.ds(off + 1, rows), :]
            + w[2:3] * buf[pl.ds(off + 2, rows), :])


WHOLE_VMEM = pl.BlockSpec(memory_space=pltpu.VMEM)


def _ffn_fwd(x, g_pre, wu, conv_w, wd, g_post, name, target=None):
    s, d = x.shape
    f2 = wu.shape[1]
    f = f2 // 2
    tr = min(256, s)
    h = HALO16
    cw = FFN_CW
    head = target is not None

    def body(*refs):
        if head:
            (x_ref, gpre_ref, wu_ref, cw_ref, wd_ref, gpost_ref, t_ref, xo_ref, sq_ref, f_ref, h_ref, up_ref, u_ref,
             carry, gbuf, vbuf, facc) = refs
        else:
            (x_ref, gpre_ref, wu_ref, cw_ref, wd_ref, gpost_ref, xo_ref, f_ref, h_ref, up_ref, u_ref,
             carry, gbuf, vbuf, facc) = refs

        @pl.when(pl.program_id(0) == 0)
        def _():
            carry[...] = jnp.zeros_like(carry)
            if head:
                sq_ref[...] = jnp.zeros_like(sq_ref)

        xv = x_ref[...]
        r = lax.rsqrt(jnp.mean(xv * xv, axis=-1, keepdims=True) + RMS_EPS)
        hv = (xv * r * gpre_ref[...]).astype(BF16)
        h_ref[...] = hv
        nchunk = f // cw

        def up_proj(j):
            for buf, base in ((gbuf, 0), (vbuf, f)):
                cs = slice(base + j * cw, base + (j + 1) * cw)
                dst = buf.at[j % FFN_NBUF]
                upc = jnp.dot(hv, wu_ref[:, cs], preferred_element_type=F32)
                up_ref[:, cs] = upc.astype(BF16)
                dst[0:h, :] = carry[:, cs]
                dst[h:h + tr, :] = upc
                carry[:, cs] = upc[tr - h:tr, :]

        def down_proj(j, act):
            part = jnp.dot(act, wd_ref[j * cw:(j + 1) * cw, :], preferred_element_type=F32)
            if j == 0:
                facc[...] = part
            else:
                facc[...] += part

        for j in range(FFN_NBUF - 1):
            up_proj(j)
        pending = None
        for j in range(nchunk):
            cg = slice(j * cw, (j + 1) * cw)
            cv = slice(f + j * cw, f + (j + 1) * cw)
            if j + FFN_NBUF - 1 < nchunk:
                up_proj(j + FFN_NBUF - 1)
            if pending is not None:
                down_proj(*pending)
            g = _conv3_taps(gbuf.at[j % FFN_NBUF], cw_ref[:, cg], h - 2, tr)
            v = _conv3_taps(vbuf.at[j % FFN_NBUF], cw_ref[:, cv], h - 2, tr)
            u_ref[:, cg] = g.astype(BF16)
            u_ref[:, cv] = v.astype(BF16)
            act = (g * jax.nn.sigmoid(g) * v).astype(BF16)
            pending = (j, act)
        down_proj(*pending)
        fv = facc[...]
        f_ref[...] = fv
        r2 = lax.rsqrt(jnp.mean(fv * fv, axis=-1, keepdims=True) + RMS_EPS)
        xo = xv + fv * r2 * gpost_ref[...]
        if head:
            err = xo - t_ref[...]
            xo_ref[...] = err * (1.0 / d)
            sq_ref[...] += jnp.sum(err * err, axis=0, keepdims=True)
        else:
            xo_ref[...] = xo

    row = _cur(tr, d)
    wide = _cur(tr, f2)
    vec = _full((1, d))
    out_specs = [row] + ([vec] if head else []) + [row, row, wide, wide]
    out_shape = ([jax.ShapeDtypeStruct((s, d), F32)] + ([jax.ShapeDtypeStruct((1, d), F32)] if head else [])
                 + [jax.ShapeDtypeStruct((s, d), F32), jax.ShapeDtypeStruct((s, d), BF16),
                    jax.ShapeDtypeStruct((s, f2), BF16), jax.ShapeDtypeStruct((s, f2), BF16)])
    return pl.pallas_call(
        body, name=name, grid=(s // tr,),
        in_specs=[row, vec, WHOLE_VMEM, _full((3, f2)), WHOLE_VMEM, vec] + ([row] if head else []),
        out_specs=out_specs, out_shape=out_shape,
        scratch_shapes=[pltpu.VMEM((h, f2), F32), pltpu.VMEM((FFN_NBUF, h + tr, cw), F32),
                        pltpu.VMEM((FFN_NBUF, h + tr, cw), F32), pltpu.VMEM((tr, d), F32)],
        compiler_params=_cparams(("arbitrary",)),
    )(*((x, g_pre, wu, conv_w, wd, g_post) + ((target,) if head else ())))


def _ffn_bwd(dxo, fout, x, up, u, g_pre, g_post, wd_t, wu_t, conv_w, name):
    s, d = x.shape
    f2 = up.shape[1]
    f = f2 // 2
    tr = min(256, s)
    nt = s // tr
    h = HALO16
    cw = FFN_CW

    def body(dy_ref, f_ref, x_ref, up_ref, u_ref, gpre_ref, gpost_ref, wdt_ref, wut_ref, cw_ref,
             dx_ref, dup_ref, act_ref, df_ref, dcw_ref, dgpost_ref, dgpre_ref, carry, dgbuf, dvbuf, dhacc):
        @pl.when(pl.program_id(0) == 0)
        def _():
            carry[...] = jnp.zeros_like(carry)
            dcw_ref[...] = jnp.zeros_like(dcw_ref)
            dgpost_ref[...] = jnp.zeros_like(dgpost_ref)
            dgpre_ref[...] = jnp.zeros_like(dgpre_ref)

        dy = dy_ref[...]
        fv = f_ref[...]
        r = lax.rsqrt(jnp.mean(fv * fv, axis=-1, keepdims=True) + RMS_EPS)
        nrm = fv * r
        dn = dy * gpost_ref[...]
        dfv = (r * (dn - nrm * jnp.mean(dn * nrm, axis=-1, keepdims=True))).astype(BF16)
        dgpost_ref[...] += jnp.sum(dy * nrm, axis=0, keepdims=True)
        df_ref[...] = dfv
        nchunk = f // cw

        def dh_part(dupb, cs, first):
            part = jnp.dot(dupb, wut_ref[cs, :], preferred_element_type=F32)
            if first:
                dhacc[...] = part
            else:
                dhacc[...] += part

        def dact_of(j):
            return jnp.dot(dfv, wdt_ref[:, j * cw:(j + 1) * cw], preferred_element_type=F32)

        ahead = [dact_of(0)]
        for j in range(nchunk):
            ch = slice(j * cw, (j + 1) * cw)
            cg = ch
            cv = slice(f + j * cw, f + (j + 1) * cw)
            dact = ahead.pop(0)
            if j + 1 < nchunk:
                ahead.append(dact_of(j + 1))
            g = u_ref[:, cg].astype(F32)
            v = u_ref[:, cv].astype(F32)
            sig, sil = _silu_parts(g)
            act_ref[:, ch] = (sil * v).astype(BF16)
            du_g = dact * v * (sig * (1.0 + g * (1.0 - sig)))
            du_v = dact * sil
            for k, (dbuf, du, cs) in enumerate(((dgbuf.at[j % FFN_NBUF], du_g, cg), (dvbuf.at[j % FFN_NBUF], du_v, cv))):
                dbuf[0:tr, :] = du
                dbuf[tr:tr + h, :] = carry[:, cs]
                carry[:, cs] = du[0:h, :]
                w = cw_ref[:, cs]
                xin = up_ref[:, cs].astype(F32)
                acc = None
                for sh in range(3):
                    dsh = dbuf[pl.ds(sh, tr), :]
                    term = w[2 - sh:3 - sh] * dsh
                    acc = term if acc is None else acc + term
                    dcw_ref[2 - sh:3 - sh, cs] += jnp.sum(xin * dsh, axis=0, keepdims=True)
                dupb = acc.astype(BF16)
                dup_ref[:, cs] = dupb
                dh_part(dupb, cs, j == 0 and k == 0)
        dh = dhacc[...]
        xv = x_ref[...]
        r1 = lax.rsqrt(jnp.mean(xv * xv, axis=-1, keepdims=True) + RMS_EPS)
        n1 = xv * r1
        dn1 = dh * gpre_ref[...]
        dx_ref[...] = dy + r1 * (dn1 - n1 * jnp.mean(dn1 * n1, axis=-1, keepdims=True))
        dgpre_ref[...] += jnp.sum(dh * n1, axis=0, keepdims=True)

    def rev(w):
        return pl.BlockSpec((tr, w), lambda i: (nt - 1 - i, 0))

    vec = _full((1, d))
    return pl.pallas_call(
        body, name=name, grid=(nt,),
        in_specs=[rev(d), rev(d), rev(d), rev(f2), rev(f2), vec, vec, WHOLE_VMEM, WHOLE_VMEM, _full((3, f2))],
        out_specs=[rev(d), rev(f2), rev(f), rev(d), _full((3, f2)), vec, vec],
        out_shape=[jax.ShapeDtypeStruct((s, d), F32), jax.ShapeDtypeStruct((s, f2), BF16),
                   jax.ShapeDtypeStruct((s, f), BF16), jax.ShapeDtypeStruct((s, d), BF16),
                   jax.ShapeDtypeStruct((3, f2), F32), jax.ShapeDtypeStruct((1, d), F32),
                   jax.ShapeDtypeStruct((1, d), F32)],
        scratch_shapes=[pltpu.VMEM((h, f2), F32), pltpu.VMEM((FFN_NBUF, tr + h, cw), F32),
                        pltpu.VMEM((FFN_NBUF, tr + h, cw), F32), pltpu.VMEM((tr, d), F32)],
        compiler_params=pltpu.CompilerParams(dimension_semantics=("arbitrary",), vmem_limit_bytes=FFN_BWD_VMEM),
    )(dxo, fout, x, up, u, g_pre, g_post, wd_t, wu_t, conv_w)


def _od_fwd(x, g, w, conv_w, w_out, g_post, name):
    s, d = x.shape
    d3 = w.shape[1]
    tr = min(512, s)
    h = HALO16
    cw = FFN_CW
    nchunk = d // cw

    def body(x_ref, g_ref, w_ref, cw_ref, wout_ref, gpost_ref, h_ref, z_ref, y_ref, m_ref, xo_ref, carry, buf, macc):
        @pl.when(pl.program_id(0) == 0)
        def _():
            carry[...] = jnp.zeros_like(carry)

        xv = x_ref[...]
        hv = (xv * _rms_scale(xv) * g_ref[...]).astype(BF16)
        h_ref[...] = hv

        def project(j):
            out = []
            for part in range(3):
                cs = slice(part * d + j * cw, part * d + (j + 1) * cw)
                zc = jnp.dot(hv, w_ref[:, cs], preferred_element_type=F32).astype(BF16)
                z_ref[:, cs] = zc
                out.append(zc.astype(F32))
            return out

        ahead = [project(0), project(1)]
        for j in range(nchunk):
            cb = slice(j * cw, (j + 1) * cw)
            bval, cval, uval = ahead.pop(0)
            if j + 2 < nchunk:
                ahead.append(project(j + 2))
            bf = buf.at[j % FFN_NBUF]
            cu = cval * uval
            bf[0:h, :] = carry[:, cb]
            bf[h:h + tr, :] = cu
            carry[:, cb] = cu[tr - h:tr, :]
            yv = (bval * _conv3_taps(bf, cw_ref[:, cb], h - 2, tr)).astype(BF16)
            y_ref[:, cb] = yv
            part = jnp.dot(yv, wout_ref[cb, :], preferred_element_type=F32)
            if j == 0:
                macc[...] = part
            else:
                macc[...] += part
        mv = macc[...]
        m_ref[...] = mv
        xo_ref[...] = xv + mv * _rms_scale(mv) * gpost_ref[...]

    row = _cur(tr, d)
    vec = _full((1, d))
    return pl.pallas_call(
        body, name=name, grid=(s // tr,),
        in_specs=[row, vec, WHOLE_VMEM, _full((3, d)), WHOLE_VMEM, vec],
        out_specs=[row, _cur(tr, d3), row, row, row],
        out_shape=[jax.ShapeDtypeStruct((s, d), BF16), jax.ShapeDtypeStruct((s, d3), BF16),
                   jax.ShapeDtypeStruct((s, d), BF16), jax.ShapeDtypeStruct((s, d), F32),
                   jax.ShapeDtypeStruct((s, d), F32)],
        scratch_shapes=[pltpu.VMEM((h, d), F32), pltpu.VMEM((FFN_NBUF, h + tr, cw), F32), pltpu.VMEM((tr, d), F32)],
        compiler_params=_cparams(("arbitrary",)),
    )(x, g, w, conv_w, w_out, g_post)


def _od_in_bwd(dy, z, conv_w, w_t, x, g, res, name):
    s, d3 = z.shape
    d = d3 // 3
    tr = min(512, s)
    h = HALO16
    cw = FFN_CW
    ext = tr + h

    def body(dy_ref, dyn_ref, z_ref, zp_ref, zn_ref, w_ref, wt_ref, x_ref, g_ref, res_ref,
             o_ref, dw_ref, dx_ref, dg_ref, buf, dbuf, dhacc):
        i = pl.program_id(0)
        first = i == 0
        last = i == pl.num_programs(0) - 1

        @pl.when(first)
        def _():
            dw_ref[...] = jnp.zeros_like(dw_ref)
            dg_ref[...] = jnp.zeros_like(dg_ref)

        started = False
        for j in range(d // cw):
            cb = slice(j * cw, (j + 1) * cw)
            cc = slice(d + j * cw, d + (j + 1) * cw)
            cu = slice(2 * d + j * cw, 2 * d + (j + 1) * cw)
            bf = buf.at[j % FFN_NBUF]
            db = dbuf.at[j % FFN_NBUF]
            w = w_ref[:, cb]
            cval = z_ref[:, cc].astype(F32)
            uval = z_ref[:, cu].astype(F32)
            bf[0:h, :] = jnp.where(first, 0.0, zp_ref[:, cc].astype(F32) * zp_ref[:, cu].astype(F32))
            bf[h:h + tr, :] = cval * uval
            k = _conv3_taps(bf, w, h - 2, tr)
            dyv = dy_ref[:, cb]
            db[0:tr, :] = dyv * z_ref[:, cb].astype(F32)
            db[tr:ext, :] = jnp.where(last, 0.0, dyn_ref[:, cb] * zn_ref[:, cb].astype(F32))
            dcu = w[2:3] * db[pl.ds(0, tr), :] + w[1:2] * db[pl.ds(1, tr), :] + w[0:1] * db[pl.ds(2, tr), :]
            dk = db[pl.ds(0, tr), :]
            for t in range(3):
                dw_ref[t:t + 1, cb] += jnp.sum(dk * bf[pl.ds(h - 2 + t, tr), :], axis=0, keepdims=True)
            for cs, val in ((cb, dyv * k), (cc, dcu * uval), (cu, dcu * cval)):
                piece = val.astype(BF16)
                o_ref[:, cs] = piece
                part = jnp.dot(piece, wt_ref[cs, :], preferred_element_type=F32)
                if started:
                    dhacc[...] += part
                else:
                    dhacc[...] = part
                    started = True
        dx, dg = _rms_bwd(dhacc[...], x_ref[...], g_ref[...])
        dg_ref[...] += dg
        dx_ref[...] = res_ref[...] + dx

    row = _cur(tr, d)
    vec = _full((1, d))
    return pl.pallas_call(
        body, name=name, grid=(s // tr,),
        in_specs=[row, _next(tr, h, d, s), _cur(tr, d3), _prev(tr, h, d3), _next(tr, h, d3, s), _full((3, d)),
                  WHOLE_VMEM, row, vec, row],
        out_specs=[_cur(tr, d3), _full((3, d)), row, vec],
        out_shape=[jax.ShapeDtypeStruct((s, d3), BF16), jax.ShapeDtypeStruct((3, d), F32),
                   jax.ShapeDtypeStruct((s, d), F32), jax.ShapeDtypeStruct((1, d), F32)],
        scratch_shapes=[pltpu.VMEM((FFN_NBUF, h + tr, cw), F32), pltpu.VMEM((FFN_NBUF, ext, cw), F32),
                        pltpu.VMEM((tr, d), F32)],
        compiler_params=_cparams(("arbitrary",)),
    )(dy, dy, z, z, z, conv_w, w_t, x, g, res)


Q0 = 2 * A_CH
K0 = Q0 + N_Q_HEADS * HEAD_DIM
V0 = K0 + 2 * HEAD_DIM
EVEN_IN = V0 + 2 * HEAD_DIM


def _rope_tables(positions):
    half = ROPE_DIM // 2
    inv_freq = ROPE_THETA ** (-(jnp.arange(half, dtype=F32) * 2.0 / ROPE_DIM))
    ang = positions.astype(F32)[:, None] * inv_freq
    cs = jnp.concatenate([jnp.cos(ang), jnp.sin(ang)], axis=1)
    spread = np.zeros((2 * half, 3 * LANES), np.float32)
    const = np.zeros((1, 3 * LANES), np.float32)
    for lane in range(3 * LANES):
        dim, part = lane % HEAD_DIM, lane // LANES
        if part == 0:
            if dim < ROPE_DIM:
                spread[dim % half, lane] = 1.0
            else:
                const[0, lane] = 1.0
        elif part == 1 and half <= dim < ROPE_DIM:
            spread[half + dim - half, lane] = 1.0
        elif part == 2 and dim < half:
            spread[half + dim, lane] = -1.0
    return jnp.dot(cs, jnp.asarray(spread), precision=lax.Precision.HIGHEST) + jnp.asarray(const)


def _rope_fwd(x, tab):
    c, sa, sb = tab[:, 0:LANES], tab[:, LANES:2 * LANES], tab[:, 2 * LANES:3 * LANES]
    return x * c + pltpu.roll(x, 8, 1) * sa + pltpu.roll(x, LANES - 8, 1) * sb


def _rope_bwd(dy, tab):
    c, sa, sb = tab[:, 0:LANES], tab[:, LANES:2 * LANES], tab[:, 2 * LANES:3 * LANES]
    return dy * c + pltpu.roll(dy * sa, LANES - 8, 1) + pltpu.roll(dy * sb, 8, 1)


def _ln_fwd(c, g, b):
    mu = jnp.mean(c, axis=-1, keepdims=True)
    xc = c - mu
    r = lax.rsqrt(jnp.mean(xc * xc, axis=-1, keepdims=True) + LN_EPS)
    nrm = xc * r
    return nrm, r, nrm * g + b


def _phase_fill(buf, ph, rows):
    for k in range(1, 8):
        ph[k - 1, 0:rows - 8, :] = buf[pl.ds(k, rows - 8), :]


def _phase_rows(buf, ph, off, n, cs):
    k = off % 8
    src = buf if k == 0 else ph.at[k - 1]
    return src[pl.ds(off - k, n), cs]


def _ev_in_fwd(x, g_pre, w_in, tab, conv_w, conv_b, ln_g, ln_b, name):
    s, d = x.shape
    tr = min(512, s)
    h = HALO32
    cw = LANES
    pw = 2 * LANES

    def body(x_ref, gpre_ref, win_ref, tab_ref, w_ref, b_ref, g_ref, lb_ref, h_ref, z_ref, c_ref, a_ref, qkv_ref,
             gbuf, cbuf, gph, carry):
        @pl.when(pl.program_id(0) == 0)
        def _():
            carry[...] = jnp.zeros_like(carry)

        xv = x_ref[...]
        hv = (xv * _rms_scale(xv) * gpre_ref[...]).astype(BF16)
        h_ref[...] = hv

        def project(lo_col, hi_col):
            for c0 in range(lo_col, hi_col, pw):
                cs = slice(c0, c0 + pw)
                z_ref[:, cs] = jnp.dot(hv, win_ref[:, cs], preferred_element_type=F32).astype(BF16)

        project(0, 2 * A_CH)
        glu = z_ref[:, 0:A_CH].astype(F32) * jax.nn.sigmoid(z_ref[:, A_CH:2 * A_CH].astype(F32))
        project(2 * A_CH, EVEN_IN)
        gbuf[0:h, :] = carry[...]
        gbuf[h:h + tr, :] = glu
        carry[...] = glu[tr - h:tr, :]
        _phase_fill(gbuf, gph, h + tr)
        for j in range(A_CH // cw):
            cs = slice(j * cw, (j + 1) * cw)
            acc = jnp.broadcast_to(b_ref[:, cs], (tr, cw))
            for t in range(A_CONV):
                acc = acc + w_ref[t:t + 1, cs] * _phase_rows(gbuf, gph, h - (A_CONV - 1) + t, tr, cs)
            cbuf[:, cs] = acc
        c = cbuf[...]
        c_ref[...] = c.astype(BF16)
        _, _, l = _ln_fwd(c, g_ref[...], lb_ref[...])
        a_ref[...] = (l * jax.nn.sigmoid(l)).astype(BF16)
        tab_v = tab_ref[...]
        for p in range(4):
            xq = z_ref[:, Q0 + p * LANES:Q0 + (p + 1) * LANES].astype(F32)
            qkv_ref[:, p * LANES:(p + 1) * LANES] = _rope_fwd(xq, tab_v).astype(BF16)
        lane = lax.broadcasted_iota(jnp.int32, (tr, LANES), 1)
        lo = lane < HEAD_DIM
        kr = _rope_fwd(z_ref[:, K0:K0 + LANES].astype(F32), tab_v)
        vr = z_ref[:, V0:V0 + LANES].astype(F32)
        for base, val in ((4 * LANES, kr), (6 * LANES, vr)):
            sw = pltpu.roll(val, HEAD_DIM, 1)
            qkv_ref[:, base:base + LANES] = jnp.where(lo, val, sw).astype(BF16)
            qkv_ref[:, base + LANES:base + 2 * LANES] = jnp.where(lo, sw, val).astype(BF16)

    return pl.pallas_call(
        body, name=name, grid=(s // tr,),
        in_specs=[_cur(tr, d), _full((1, d)), WHOLE_VMEM, _cur(tr, 3 * LANES), _full((A_CONV, A_CH)),
                  _full((1, A_CH)), _full((1, A_CH)), _full((1, A_CH))],
        out_specs=[_cur(tr, d), _cur(tr, EVEN_IN), _cur(tr, A_CH), _cur(tr, A_CH), _cur(tr, 2 * A_CH)],
        out_shape=[jax.ShapeDtypeStruct((s, d), BF16), jax.ShapeDtypeStruct((s, EVEN_IN), BF16),
                   jax.ShapeDtypeStruct((s, A_CH), BF16), jax.ShapeDtypeStruct((s, A_CH), BF16),
                   jax.ShapeDtypeStruct((s, 2 * A_CH), BF16)],
        scratch_shapes=[pltpu.VMEM((h + tr, A_CH), F32), pltpu.VMEM((tr, A_CH), F32),
                        pltpu.VMEM((7, h + tr, A_CH), F32), pltpu.VMEM((h, A_CH), F32)],
        compiler_params=_cparams(("arbitrary",)),
    )(x, g_pre, w_in, tab, conv_w, conv_b, ln_g, ln_b)


def _ev_mid_bwd(dcat, c, z, dq, dkv, tab, conv_w, ln_g, ln_b, w_t, x, g_pre, res, name):
    s = z.shape[0]
    tr = min(512, s)
    h = HALO32
    cw = LANES
    ext = tr + h

    def body(da_ref, dan_ref, c_ref, cn_ref, z_ref, dq_ref, dkv_ref, tab_ref, w_ref, g_ref, lb_ref,
             wt_ref, x_ref, gpre_ref, res_ref, dz_ref, dw_ref, dvec_ref, dx_ref, dg_ref, dcbuf, dcph, dhacc):
        i = pl.program_id(0)
        first = i == 0
        last = i == pl.num_programs(0) - 1

        @pl.when(first)
        def _():
            dw_ref[...] = jnp.zeros_like(dw_ref)
            dvec_ref[...] = jnp.zeros_like(dvec_ref)
            dg_ref[...] = jnp.zeros_like(dg_ref)

        started = []

        def dh_part(cs):
            part = jnp.dot(dz_ref[:, cs], wt_ref[cs, :], preferred_element_type=F32)
            if started:
                dhacc[...] += part
            else:
                dhacc[...] = part
                started.append(True)

        tab_v = tab_ref[...]
        for p in range(4):
            cs = slice(p * LANES, (p + 1) * LANES)
            dz_ref[:, Q0 + p * LANES:Q0 + (p + 1) * LANES] = _rope_bwd(dq_ref[:, cs], tab_v).astype(BF16)
        lane = lax.broadcasted_iota(jnp.int32, (tr, LANES), 1)
        lo = lane < HEAD_DIM

        def fold(base):
            p0 = dkv_ref[:, base:base + LANES]
            p1 = dkv_ref[:, base + LANES:base + 2 * LANES]
            s0 = p0 + pltpu.roll(p0, HEAD_DIM, 1)
            s1 = p1 + pltpu.roll(p1, HEAD_DIM, 1)
            return jnp.where(lo, s0, s1)

        dz_ref[:, K0:K0 + LANES] = _rope_bwd(fold(0), tab_v).astype(BF16)
        dz_ref[:, V0:V0 + LANES] = fold(2 * LANES).astype(BF16)
        dh_part(slice(Q0, EVEN_IN))

        gv = g_ref[...]

        def ln_silu_bwd(cv, dav):
            nrm, r, l = _ln_fwd(cv, gv, lb_ref[...])
            sig = jax.nn.sigmoid(l)
            dl = dav * (sig * (1.0 + l * (1.0 - sig)))
            dn = dl * gv
            dc = r * (dn - jnp.mean(dn, axis=-1, keepdims=True) - nrm * jnp.mean(dn * nrm, axis=-1, keepdims=True))
            return dc, dl, nrm

        dc, dl, nrm = ln_silu_bwd(c_ref[...].astype(F32), da_ref[...])
        dcn, _, _ = ln_silu_bwd(cn_ref[...].astype(F32), dan_ref[...])
        dcbuf[0:tr, :] = dc
        dcbuf[tr:ext, :] = jnp.where(last, 0.0, dcn)
        dvec_ref[0:1, :] += jnp.sum(dc, axis=0, keepdims=True)
        dvec_ref[1:2, :] += jnp.sum(dl * nrm, axis=0, keepdims=True)
        dvec_ref[2:3, :] += jnp.sum(dl, axis=0, keepdims=True)

        _phase_fill(dcbuf, dcph, ext)
        a_lin = z_ref[:, 0:A_CH].astype(F32)
        sig_g = jax.nn.sigmoid(z_ref[:, A_CH:2 * A_CH].astype(F32))
        glu = a_lin * sig_g
        for j in range(A_CH // cw):
            cs = slice(j * cw, (j + 1) * cw)
            gluj = glu[:, cs]
            acc = jnp.zeros((tr, cw), F32)
            for t in range(A_CONV):
                dsh = _phase_rows(dcbuf, dcph, A_CONV - 1 - t, tr, cs)
                acc = acc + w_ref[t:t + 1, cs] * dsh
                dw_ref[t:t + 1, cs] += jnp.sum(gluj * dsh, axis=0, keepdims=True)
            dz_ref[:, cs] = (acc * sig_g[:, cs]).astype(BF16)
            dz_ref[:, A_CH + j * cw:A_CH + (j + 1) * cw] = (
                acc * a_lin[:, cs] * sig_g[:, cs] * (1.0 - sig_g[:, cs])).astype(BF16)
            if j % 2 == 1:
                dh_part(slice((j - 1) * cw, (j + 1) * cw))
                dh_part(slice(A_CH + (j - 1) * cw, A_CH + (j + 1) * cw))

        dx, dg = _rms_bwd(dhacc[...], x_ref[...], gpre_ref[...])
        dg_ref[...] += dg
        dx_ref[...] = res_ref[...] + dx

    row = _cur(tr, D_MODEL)
    vec = _full((1, D_MODEL))
    return pl.pallas_call(
        body, name=name, grid=(s // tr,),
        in_specs=[_cur(tr, A_CH), _next(tr, h, A_CH, s), _cur(tr, A_CH), _next(tr, h, A_CH, s),
                  _cur(tr, EVEN_IN), _cur(tr, A_CH), _cur(tr, A_CH), _cur(tr, 3 * LANES),
                  _full((A_CONV, A_CH)), _full((1, A_CH)), _full((1, A_CH)), WHOLE_VMEM, row, vec, row],
        out_specs=[_cur(tr, EVEN_IN), _full((A_CONV, A_CH)), _full((8, A_CH)), row, vec],
        out_shape=[jax.ShapeDtypeStruct((s, EVEN_IN), BF16), jax.ShapeDtypeStruct((A_CONV, A_CH), F32),
                   jax.ShapeDtypeStruct((8, A_CH), F32), jax.ShapeDtypeStruct((s, D_MODEL), F32),
                   jax.ShapeDtypeStruct((1, D_MODEL), F32)],
        scratch_shapes=[pltpu.VMEM((ext, A_CH), F32), pltpu.VMEM((7, ext, A_CH), F32),
                        pltpu.VMEM((tr, D_MODEL), F32)],
        compiler_params=_cparams(("arbitrary",)),
    )(dcat, dcat, c, c, z, dq, dkv, tab, conv_w, ln_g, ln_b, w_t, x, g_pre, res)


NT = (((1,), (1,)), ((), ()))
TN = (((0,), (0,)), ((), ()))
QB = WINDOW
SCALE = HEAD_DIM ** -0.5
ATT_AHEAD = 2


def _att_scores(q2m, kwin):
    return lax.dot_general(q2m, kwin, NT, preferred_element_type=F32)


def _att_probs(raw, sink, mask):
    sc = jnp.where(mask, raw * SCALE, -jnp.inf)
    mx = jnp.maximum(jnp.max(sc, axis=-1, keepdims=True), sink)
    p = jnp.exp(sc - mx)
    ps = jnp.exp(sink - mx)
    inv = 1.0 / (jnp.sum(p, axis=-1, keepdims=True) + ps)
    return p * inv, ps * inv


def _att_mask(i):
    r = lax.broadcasted_iota(jnp.int32, (QB, 2 * QB), 0)
    kc = lax.broadcasted_iota(jnp.int32, (QB, 2 * QB), 1)
    diff = r + QB - kc
    return (diff >= 0) & (diff < WINDOW) & ((kc >= QB) | (i > 0))


def _half_masks(dtype):
    lane = lax.broadcasted_iota(jnp.int32, (1, LANES), 1)
    return (lane < HEAD_DIM).astype(dtype), (lane >= HEAD_DIM).astype(dtype)


def _att_fwd(qkv, a, sinks, name):
    s = qkv.shape[0]
    nb = s // QB

    def body(sink_ref, qkv_ref, kvp_ref, a_ref, o_ref):
        i = pl.program_id(0)
        mask = _att_mask(i)
        mlo, mhi = _half_masks(BF16)
        o_ref[:, 0:A_CH] = a_ref[...]

        def window(col):
            return jnp.concatenate([kvp_ref[:, col * LANES:(col + 1) * LANES],
                                    qkv_ref[:, A_CH + col * LANES:A_CH + (col + 1) * LANES]], axis=0)

        def raw_scores(p):
            q2 = qkv_ref[:, p * LANES:(p + 1) * LANES]
            kwin = window(p // 2)
            return _att_scores(q2 * mlo, kwin), _att_scores(q2 * mhi, kwin)

        ahead = [raw_scores(p) for p in range(4)]
        for p in range(4):
            raw_e, raw_o = ahead[p]
            vwin = window(2 + p // 2)
            pe, _ = _att_probs(raw_e, sink_ref[2 * p], mask)
            po, _ = _att_probs(raw_o, sink_ref[2 * p + 1], mask)
            o = (jnp.dot(pe.astype(BF16), vwin * mlo, preferred_element_type=F32)
                 + jnp.dot(po.astype(BF16), vwin * mhi, preferred_element_type=F32))
            o_ref[:, A_CH + p * LANES:A_CH + (p + 1) * LANES] = o.astype(BF16)

    grid_spec = pltpu.PrefetchScalarGridSpec(
        num_scalar_prefetch=1, grid=(nb,),
        in_specs=[pl.BlockSpec((QB, 2 * A_CH), lambda i, sk: (i, 0)),
                  pl.BlockSpec((QB, A_CH), lambda i, sk: (jnp.maximum(i - 1, 0), 1)),
                  pl.BlockSpec((QB, A_CH), lambda i, sk: (i, 0))],
        out_specs=pl.BlockSpec((QB, 2 * A_CH), lambda i, sk: (i, 0)),
    )
    return pl.pallas_call(
        body, name=name, grid_spec=grid_spec,
        out_shape=jax.ShapeDtypeStruct((s, 2 * A_CH), BF16),
        compiler_params=_cparams(("parallel",)),
    )(sinks, qkv, qkv, a)


def _att_bwd(qkv, dcat, sinks, name):
    s = qkv.shape[0]
    nb = s // QB

    def body(sink_ref, qkv_ref, kvp_ref, do_ref, dq_ref, dkv_ref, ds_ref, carry):
        i = pl.program_id(0)

        @pl.when(i == 0)
        def _():
            ds_ref[...] = jnp.zeros_like(ds_ref)
            carry[...] = jnp.zeros_like(carry)

        @pl.when(i < nb)
        def _():
            mask = _att_mask(i)
            mlo, mhi = _half_masks(BF16)
            dwin = [jnp.zeros((2 * QB, LANES), F32) for _ in range(4)]

            def window(col):
                return jnp.concatenate([kvp_ref[:, col * LANES:(col + 1) * LANES],
                                        qkv_ref[:, A_CH + col * LANES:A_CH + (col + 1) * LANES]], axis=0)

            def first_products(n):
                p, hm = n // 2, (mlo, mhi)[n % 2]
                qm = qkv_ref[:, p * LANES:(p + 1) * LANES] * hm
                dom = do_ref[:, p * LANES:(p + 1) * LANES].astype(BF16) * hm
                kwin, vwin = window(p // 2), window(2 + p // 2)
                return (qm, dom, kwin * hm, _att_scores(qm, kwin),
                        lax.dot_general(dom, vwin, NT, preferred_element_type=F32))

            ahead = [first_products(n) for n in range(ATT_AHEAD)]
            dq2 = None
            for n in range(N_Q_HEADS):
                g = n // 4
                qm, dom, kwm, raw, dp = ahead.pop(0)
                if n + ATT_AHEAD < N_Q_HEADS:
                    ahead.append(first_products(n + ATT_AHEAD))
                prob, psink = _att_probs(raw, sink_ref[n], mask)
                delta = jnp.sum(prob * dp, axis=-1, keepdims=True)
                dsc = (prob * (dp - delta) * SCALE).astype(BF16)
                ds_ref[n:n + 1, :] += jnp.broadcast_to(jnp.sum(-psink * delta, axis=0, keepdims=True), (1, LANES))
                part = jnp.dot(dsc, kwm, preferred_element_type=F32)
                dq2 = part if n % 2 == 0 else dq2 + part
                dwin[g] = dwin[g] + lax.dot_general(dsc, qm, TN, preferred_element_type=F32)
                dwin[2 + g] = dwin[2 + g] + lax.dot_general(prob.astype(BF16), dom, TN, preferred_element_type=F32)
                if n % 2 == 1:
                    dq_ref[:, (n // 2) * LANES:(n // 2 + 1) * LANES] = dq2
            for n in range(4):
                cs = slice(n * LANES, (n + 1) * LANES)
                dkv_ref[:, cs] = carry[:, cs] + dwin[n][0:QB, :]
                carry[:, cs] = dwin[n][QB:2 * QB, :]

        @pl.when(i == nb)
        def _():
            dkv_ref[...] = carry[...]

    grid_spec = pltpu.PrefetchScalarGridSpec(
        num_scalar_prefetch=1, grid=(nb + 1,),
        in_specs=[pl.BlockSpec((QB, 2 * A_CH), lambda i, sk: (jnp.minimum(i, nb - 1), 0)),
                  pl.BlockSpec((QB, A_CH), lambda i, sk: (jnp.maximum(jnp.minimum(i, nb - 1) - 1, 0), 1)),
                  pl.BlockSpec((QB, A_CH), lambda i, sk: (jnp.minimum(i, nb - 1), 1))],
        out_specs=[pl.BlockSpec((QB, A_CH), lambda i, sk: (jnp.minimum(i, nb - 1), 0)),
                   pl.BlockSpec((QB, A_CH), lambda i, sk: (jnp.maximum(i - 1, 0), 0)),
                   pl.BlockSpec((8, LANES), lambda i, sk: (0, 0))],
        scratch_shapes=[pltpu.VMEM((QB, A_CH), F32)],
    )
    return pl.pallas_call(
        body, name=name, grid_spec=grid_spec,
        out_shape=[jax.ShapeDtypeStruct((s, A_CH), F32), jax.ShapeDtypeStruct((s, A_CH), F32),
                   jax.ShapeDtypeStruct((8, LANES), F32)],
        compiler_params=_cparams(("arbitrary",)),
    )(sinks, qkv, qkv, dcat)


def _local_step(x, positions, target, w, fetch=None, emit=None):
    row = lambda a, i: a[i:i + 1]
    tab = _rope_tables(positions)
    g = {}

    def ffn_fwd(xin, i, tgt=None):
        outs = _ffn_fwd(xin, row(w["ffn_norm_pre"], i), w["ffn_w_up", i], w["ffn_conv_w"][i],
                        w["ffn_w_down", i], row(w["ffn_norm_post"], i), f"ffn{i}_fwd", tgt)
        f, h, up, u = outs[-4:]
        return outs[:-4], (xin, f, h, up, u)

    def point(name, after):
        return emit(name, after, g) if emit is not None else 0.0

    def ffn_bwd(dxout, saved, i, tok):
        xin, f, h, up, u = saved
        dxin, dup, act, df, d_cw, dg_post, dg_pre = _ffn_bwd(
            dxout, f, xin, up, u, row(w["ffn_norm_pre"], i), row(w["ffn_norm_post"], i) + tok, w["ffn_w_down_t", i],
            w["ffn_w_up_t", i], w["ffn_conv_w"][i], f"ffn{i}_bwd")
        tok = point(f"ffn{i}_bwd_done", dxin)
        g["ffn_w_down", i] = _mm_tn(act, df, f"ffn{i}_down_dw")
        g["ffn_w_up", i] = _mm_tn(dup, h, f"ffn{i}_up_dw")
        return dxin, tok, dict(ffn_norm_post=dg_post, ffn_norm_pre=dg_pre, ffn_conv_w=d_cw)

    h0, z0, c0, a0, qkv = _ev_in_fwd(x, row(w["mix_norm_pre"], 0), w["ev_w_in"], tab, w["ev_a_conv_w"],
                                     w["ev_a_conv_b"], w["ev_a_ln_g"], w["ev_a_ln_b"], "ev_in")
    cat = _att_fwd(qkv, a0, w["ev_sinks"], "ev_att")
    m0, x1 = _mm_post(cat, w["ev_w_out"], row(w["mix_norm_post"], 0), x, "ev_out")
    if fetch is not None:
        w = {**w, **fetch("ffn0", x1)}
    (x2,), ffn0 = ffn_fwd(x1, 0)
    if fetch is not None:
        w = {**w, **fetch("layer1", x2)}
    h2, z1, y1, m1, x3 = _od_fwd(x2, row(w["mix_norm_pre"], 1), w["od_w_in"], w["od_conv_w"], w["od_w_out"],
                                 row(w["mix_norm_post"], 1), "od_fwd")
    (dx4, sq), ffn1 = ffn_fwd(x3, 1, target)

    dx3, _, gf1 = ffn_bwd(dx4, ffn1, 1, 0.0)
    dm1, dy1, dg_mo1 = _mm_post_bwd(dx3, m1, row(w["mix_norm_post"], 1), w["od_w_out_t"], "od_out_bwd")
    g["od_w_out"] = _mm_tn(y1, dm1, "od_out_dw")
    dz1, g["od_conv_w"], dx2, dg_mp1 = _od_in_bwd(dy1, z1, w["od_conv_w"], w["od_w_in_t"], x2,
                                                  row(w["mix_norm_pre"], 1), dx3, "od_in_bwd")
    g["od_w_in"] = _mm_tn(dz1, h2, "od_in_dw")
    tok = point("layer1_grads", dx2)

    dx1, tok, gf0 = ffn_bwd(dx2, ffn0, 0, tok)
    tok = tok + point("ffn0_grads", dx1)
    dm0, dcat, dg_mo0 = _mm_post_bwd(dx1, m0, row(w["mix_norm_post"], 0) + tok, w["ev_w_out_t"], "ev_out_bwd")
    tok = point("ev_out_bwd_done", dcat)
    g["ev_w_out"] = _mm_tn(cat, dm0, "ev_out_dw")
    dq, dkv, dsk = _att_bwd(qkv, dcat, w["ev_sinks"] + tok, "ev_att_bwd")
    tok = point("ev_att_bwd_done", dq)
    dz0, g["ev_a_conv_w"], dvec, dx0, dg_mp0 = _ev_mid_bwd(
        dcat, c0, z0, dq, dkv, tab, w["ev_a_conv_w"], w["ev_a_ln_g"] + tok, w["ev_a_ln_b"], w["ev_w_in_t"], x,
        row(w["mix_norm_pre"], 0), dx1, "ev_in_bwd")
    point("ev_mid_bwd_done", dz0)
    g["ev_w_in"] = _mm_tn(dz0, h0, "ev_in_dw")

    g["ev_a_conv_b"] = dvec[0:1]
    g["ev_a_ln_g"] = dvec[1:2]
    g["ev_a_ln_b"] = dvec[2:3]
    g["ev_sinks"] = dsk[:, 0]
    g["mix_norm_pre"] = jnp.concatenate([dg_mp0, dg_mp1], axis=0)
    g["mix_norm_post"] = jnp.concatenate([dg_mo0, dg_mo1], axis=0)
    g["ffn_norm_pre"] = jnp.concatenate([gf0["ffn_norm_pre"], gf1["ffn_norm_pre"]], axis=0)
    g["ffn_norm_post"] = jnp.concatenate([gf0["ffn_norm_post"], gf1["ffn_norm_post"]], axis=0)
    g["ffn_conv_w"] = jnp.stack([gf0["ffn_conv_w"], gf1["ffn_conv_w"]], axis=0)
    return sq, dx0, g


ANY = pl.BlockSpec(memory_space=pl.ANY)
PACK_COLS = 1024


def _me():
    return lax.axis_index("x"), lax.axis_index("y"), lax.axis_index("c")


def _other_chips(x, y):
    return [(1 - x, y), (x, 1 - y), (1 - x, 1 - y)]


def _remote(src, dst, send, recv, dev):
    return pltpu.make_async_remote_copy(src_ref=src, dst_ref=dst, send_sem=send, recv_sem=recv,
                                        device_id=dev, device_id_type=MESH)


def _gather_chips(wp, name):
    r, cols = wp.shape
    rh = r // 2

    def body(w_ref, o_ref, send, recv):
        x, y, c = _me()
        p = 2 * x + y
        sib = (x, y, 1 - c)
        chips = _other_chips(x, y)
        half = pl.ds(c * rh, rh)
        other = pl.ds((1 - c) * rh, rh)
        sent = [_remote(w_ref.at[half], o_ref.at[p, half], send.at[k], recv.at[k], (cx, cy, c))
                for k, (cx, cy) in enumerate(chips)]
        for cp in sent:
            cp.start()
        for k, (cx, cy) in enumerate(chips):
            q = 2 * cx + cy
            _remote(w_ref.at[half], o_ref.at[q, half], send.at[k], recv.at[k], (cx, cy, c)).wait_recv()
            fwd = _remote(o_ref.at[q, half], o_ref.at[q, half], send.at[3 + k], recv.at[3 + k], sib)
            fwd.start()
            sent.append(fwd)
        for k, (cx, cy) in enumerate(chips):
            q = 2 * cx + cy
            _remote(o_ref.at[q, other], o_ref.at[q, other], send.at[3 + k], recv.at[3 + k], sib).wait_recv()
        for cp in sent:
            cp.wait_send()

    return pl.pallas_call(
        body, name=name, in_specs=[ANY], out_specs=ANY,
        out_shape=jax.ShapeDtypeStruct((N_CHIPS, r, cols), wp.dtype),
        scratch_shapes=[pltpu.SemaphoreType.DMA((6,)), pltpu.SemaphoreType.DMA((6,))],
    )(wp)


HBM_SPEC = pl.BlockSpec(memory_space=pltpu.HBM)
SEM_SPEC = pl.BlockSpec(memory_space=pltpu.SEMAPHORE)
DATAFLOW = pltpu.SideEffectType.DATAFLOW_SIDE_EFFECTING


def _gather_plan(w_ref, land_ref):
    x, y, c = _me()
    return [(w_ref, land_ref.at[2 * x + y], (cx, cy, c)) for cx, cy in _other_chips(x, y)]


def _copies_start(src, land_shape, plan, n, name):
    def body(src_ref, land_ref, send, recv, src_thru, land_thru, token):
        for k, (s_view, d_view, dev) in enumerate(plan(src_ref, land_ref)):
            _remote(s_view, d_view, send.at[k], recv.at[k], dev).start()
        token[...] = jnp.zeros_like(token)

    return pl.pallas_call(
        body, name=name,
        out_shape=(pltpu.SemaphoreType.DMA((n,)), pltpu.SemaphoreType.DMA((n,)), pltpu.HBM(src.shape, src.dtype),
                   pltpu.HBM(land_shape, src.dtype), jax.ShapeDtypeStruct((8, LANES), F32)),
        in_specs=(HBM_SPEC, HBM_SPEC),
        out_specs=(SEM_SPEC, SEM_SPEC, HBM_SPEC, HBM_SPEC, pl.BlockSpec(memory_space=pltpu.VMEM)),
        input_output_aliases={0: 2, 1: 3},
        compiler_params=pltpu.CompilerParams(has_side_effects=DATAFLOW),
    )(pltpu.with_memory_space_constraint(src, pltpu.HBM),
      pltpu.with_memory_space_constraint(lax.empty(land_shape, src.dtype), pltpu.HBM))


def _copies_wait(started, after, plan, name):
    send, recv, src_thru, land_thru, _ = started

    def body(src_ref, land_ref, send, recv, after_ref, src_dead, land_out):
        for k, (s_view, d_view, dev) in enumerate(plan(src_ref, land_ref)):
            cp = _remote(s_view, d_view, send.at[k], recv.at[k], dev)
            cp.wait_send()
            cp.wait_recv()

    return pl.pallas_call(
        body, name=name,
        out_shape=(pltpu.HBM(src_thru.shape, src_thru.dtype), pltpu.HBM(land_thru.shape, land_thru.dtype)),
        in_specs=(HBM_SPEC, HBM_SPEC, SEM_SPEC, SEM_SPEC, ANY),
        out_specs=(HBM_SPEC, HBM_SPEC),
        input_output_aliases={0: 0, 1: 1},
        compiler_params=pltpu.CompilerParams(has_side_effects=DATAFLOW),
    )(src_thru, land_thru, send, recv, after)


def _swap_plan(g_ref, land_ref):
    x, y, c = _me()
    return [(g_ref.at[q, 1 - c], land_ref.at[q], (x, y, 1 - c)) for q in range(N_CHIPS)]


def _ici_plan(a_ref, land_ref):
    x, y, c = _me()
    return [(a_ref.at[2 * cx + cy], land_ref.at[2 * x + y], (cx, cy, c)) for cx, cy in _other_chips(x, y)]


def _share_plan(h_ref, land_ref):
    x, y, c = _me()
    return [(h_ref, land_ref, (x, y, 1 - c))]


def _exchange8(v, reduce, name):
    r, cols = v.shape
    rel = [(a, b, d) for a in (0, 1) for b in (0, 1) for d in (0, 1) if (a, b, d) != (0, 0, 0)]

    def body(v_ref, o_ref, *rest):
        if reduce:
            gbuf, send, recv = rest
        else:
            gbuf = o_ref
            send, recv = rest
        x, y, c = _me()
        me = 4 * x + 2 * y + c
        gbuf[me] = v_ref[...]
        sent = []
        for k, (a, b, d) in enumerate(rel):
            cp = _remote(v_ref, gbuf.at[me], send.at[k], recv.at[k], ((x + a) % 2, (y + b) % 2, (c + d) % 2))
            cp.start()
            sent.append(cp)
        for k, (a, b, d) in enumerate(rel):
            src = 4 * ((x + a) % 2) + 2 * ((y + b) % 2) + (c + d) % 2
            _remote(v_ref, gbuf.at[src], send.at[k], recv.at[k], (x, y, c)).wait_recv()
        for cp in sent:
            cp.wait_send()
        if reduce:
            acc = gbuf[0]
            for n in range(1, 8):
                acc = acc + gbuf[n]
            o_ref[...] = acc

    vmem = pl.BlockSpec(memory_space=pltpu.VMEM)
    sems = [pltpu.SemaphoreType.DMA((7,)), pltpu.SemaphoreType.DMA((7,))]
    if reduce:
        out_shape = jax.ShapeDtypeStruct((r, cols), F32)
        scratch = [pltpu.VMEM((8, r, cols), F32)] + sems
    else:
        out_shape = jax.ShapeDtypeStruct((8, r, cols), F32)
        scratch = sems
    return pl.pallas_call(body, name=name, in_specs=[vmem], out_specs=vmem, out_shape=out_shape,
                          scratch_shapes=scratch)(v)


def _rs_swap(g, name):
    _, _, rh, cols = g.shape

    def body(g_ref, o_ref, send, recv):
        x, y, c = _me()
        cps = [_remote(g_ref.at[q, 1 - c], o_ref.at[q], send.at[q], recv.at[q], (x, y, 1 - c)) for q in range(N_CHIPS)]
        for cp in cps:
            cp.start()
        for cp in cps:
            cp.wait()

    return pl.pallas_call(
        body, name=name, in_specs=[ANY], out_specs=ANY,
        out_shape=jax.ShapeDtypeStruct((N_CHIPS, rh, cols), F32),
        scratch_shapes=[pltpu.SemaphoreType.DMA((N_CHIPS,)), pltpu.SemaphoreType.DMA((N_CHIPS,))],
    )(g)


def _row_tile(rows, pref, mult=8):
    if rows <= pref:
        return rows
    t = (pref // mult) * mult
    while t >= mult:
        if rows % t == 0:
            return t
        t -= mult
    return rows


def _rs_add(g, sib, c, name):
    _, _, rh, cols = g.shape
    tr = _row_tile(rh, 512, 16)

    def body(c_ref, g_ref, s_ref, o_ref):
        o_ref[...] = (g_ref[...] + s_ref[...]).astype(BF16)

    grid_spec = pltpu.PrefetchScalarGridSpec(
        num_scalar_prefetch=1, grid=(N_CHIPS, rh // tr),
        in_specs=[pl.BlockSpec((None, None, tr, cols), lambda q, i, cr: (q, cr[0], i, 0)),
                  pl.BlockSpec((None, tr, cols), lambda q, i, cr: (q, i, 0))],
        out_specs=pl.BlockSpec((None, tr, cols), lambda q, i, cr: (q, i, 0)),
    )
    return pl.pallas_call(
        body, name=name, grid_spec=grid_spec,
        out_shape=jax.ShapeDtypeStruct((N_CHIPS, rh, cols), BF16),
        compiler_params=_cparams(("parallel", "parallel")),
    )(c, g, sib)


def _rs_ici(a, name):
    _, rh, cols = a.shape

    def body(a_ref, o_ref, send, recv):
        x, y, c = _me()
        p = 2 * x + y
        cps = []
        for k, (cx, cy) in enumerate(_other_chips(x, y)):
            cp = _remote(a_ref.at[2 * cx + cy], o_ref.at[p], send.at[k], recv.at[k], (cx, cy, c))
            cp.start()
            cps.append(cp)
        for k, (cx, cy) in enumerate(_other_chips(x, y)):
            q = 2 * cx + cy
            _remote(a_ref.at[q], o_ref.at[q], send.at[k], recv.at[k], (cx, cy, c)).wait_recv()
        for cp in cps:
            cp.wait_send()

    return pl.pallas_call(
        body, name=name, in_specs=[ANY], out_specs=ANY,
        out_shape=jax.ShapeDtypeStruct((N_CHIPS, rh, cols), a.dtype),
        scratch_shapes=[pltpu.SemaphoreType.DMA((3,)), pltpu.SemaphoreType.DMA((3,))],
    )(a)


def _rs_sum(rb, a, chip, name):
    _, rh, cols = rb.shape
    tr = _row_tile(rh, 512, 16)

    def body(p_ref, r0, r1, r2, r3, own, o_ref):
        p = p_ref[0]
        ownv = own[...].astype(F32)
        acc = None
        for q, r in enumerate((r0, r1, r2, r3)):
            v = jnp.where(p == q, ownv, r[...].astype(F32))
            acc = v if acc is None else acc + v
        o_ref[...] = acc

    def spec(q):
        return pl.BlockSpec((None, tr, cols), lambda i, pr: (jnp.where(pr[0] == q, (q + 1) % N_CHIPS, q), i, 0))

    grid_spec = pltpu.PrefetchScalarGridSpec(
        num_scalar_prefetch=1, grid=(rh // tr,),
        in_specs=[spec(0), spec(1), spec(2), spec(3), pl.BlockSpec((None, tr, cols), lambda i, pr: (pr[0], i, 0))],
        out_specs=pl.BlockSpec((tr, cols), lambda i, pr: (i, 0)),
    )
    return pl.pallas_call(
        body, name=name, grid_spec=grid_spec,
        out_shape=jax.ShapeDtypeStruct((rh, cols), F32),
        compiler_params=_cparams(("parallel",)),
    )(chip, rb, rb, rb, rb, a)


def _rs_share(hsum, name):
    rh, cols = hsum.shape

    def body(h_ref, o_ref, send, recv):
        x, y, c = _me()
        cp = _remote(h_ref, o_ref, send, recv, (x, y, 1 - c))
        cp.start()
        cp.wait()

    return pl.pallas_call(
        body, name=name, in_specs=[ANY], out_specs=ANY,
        out_shape=jax.ShapeDtypeStruct((rh, cols), F32),
        scratch_shapes=[pltpu.SemaphoreType.DMA, pltpu.SemaphoreType.DMA],
    )(hsum)


def _adamw(w, g, m, v, name):
    rows, cols = w.shape
    tr = _row_tile(rows, 512)

    def body(w_ref, g_ref, m_ref, v_ref, d_ref, nm_ref, nv_ref):
        gv = g_ref[...]
        nm = ADAM_B1 * m_ref[...] + (1.0 - ADAM_B1) * gv
        nv = ADAM_B2 * v_ref[...] + (1.0 - ADAM_B2) * (gv * gv)
        m_hat = nm / (1.0 - ADAM_B1 ** ADAM_STEP)
        v_hat = nv / (1.0 - ADAM_B2 ** ADAM_STEP)
        d_ref[...] = -ADAM_LR * (m_hat / (jnp.sqrt(v_hat) + ADAM_EPS) + ADAM_WD * w_ref[...])
        nm_ref[...] = nm
        nv_ref[...] = nv

    spec = pl.BlockSpec((tr, cols), lambda i: (i, 0))
    shp = jax.ShapeDtypeStruct((rows, cols), F32)
    return pl.pallas_call(
        body, name=name, grid=(rows // tr,), in_specs=[spec] * 4, out_specs=[spec] * 3, out_shape=[shp] * 3,
        compiler_params=_cparams(("parallel",)),
    )(w, g, m, v)


WEIGHTS = ("mix_norm_pre", "mix_norm_post", "ffn_norm_pre", "ffn_norm_post", "ev_w_in", "ev_a_conv_w", "ev_a_conv_b",
           "ev_a_ln_g", "ev_a_ln_b", "ev_sinks", "ev_w_out", "od_w_in", "od_conv_w", "od_w_out", "ffn_w_up",
           "ffn_conv_w", "ffn_w_down")
MATS = (("ev_w_in", 2), ("ev_w_out", 1), ("od_w_in", 2), ("od_w_out", 1), ("ffn_w_up", 2), ("ffn_w_down", 1))
UNITS = (("ev_w_in", 0, 2), ("ev_w_out", 0, 1), ("ffn_w_up", 0, 2), ("ffn_w_down", 0, 1),
         ("od_w_in", 0, 2), ("od_w_out", 0, 1), ("ffn_w_up", 1, 2), ("ffn_w_down", 1, 1))
GATHER_GROUPS = ((0, 1), (2, 3), (4, 5, 6, 7))
REDUCE_GROUPS = {"layer1": (4, 5, 6, 7), "ffn0": (2, 3), "ev": (0, 1)}
SMALL_SHARDED = ("ev_a_conv_w", "od_conv_w", "ffn_conv_w")
REPLICATED = ("mix_norm_pre", "mix_norm_post", "ffn_norm_pre", "ffn_norm_post", "ev_a_conv_b", "ev_a_ln_g",
              "ev_a_ln_b", "ev_sinks")


def _pack(parts, rows_multiple):
    flat = jnp.concatenate([p.reshape(-1) for p in parts])
    unit = rows_multiple * PACK_COLS
    pad = (-flat.shape[0]) % unit
    if pad:
        flat = jnp.concatenate([flat, jnp.zeros((pad,), flat.dtype)])
    return flat.reshape(-1, PACK_COLS)


def _unpack(buf, shapes):
    flat = buf.reshape(-1)
    out, off = [], 0
    for shp in shapes:
        n = 1
        for d in shp:
            n *= d
        out.append(flat[off:off + n].reshape(shp))
        off += n
    return out


def _shard_rows(shard, axis):
    if axis == 2:
        shard = jnp.swapaxes(shard, 1, 2)
    return shard.reshape(-1, PACK_COLS)


def kernel(x, positions, mix_norm_pre, mix_norm_post, ffn_norm_pre, ffn_norm_post, ev_w_in, ev_a_conv_w, ev_a_conv_b, ev_a_ln_g, ev_a_ln_b, ev_sinks, ev_w_out, od_w_in, od_conv_w, od_w_out, ffn_w_up, ffn_conv_w, ffn_w_down, loss_target, m_mix_norm_pre, m_mix_norm_post, m_ffn_norm_pre, m_ffn_norm_post, m_ev_w_in, m_ev_a_conv_w, m_ev_a_conv_b, m_ev_a_ln_g, m_ev_a_ln_b, m_ev_sinks, m_ev_w_out, m_od_w_in, m_od_conv_w, m_od_w_out, m_ffn_w_up, m_ffn_conv_w, m_ffn_w_down, v_mix_norm_pre, v_mix_norm_post, v_ffn_norm_pre, v_ffn_norm_post, v_ev_w_in, v_ev_a_conv_w, v_ev_a_conv_b, v_ev_a_ln_g, v_ev_a_ln_b, v_ev_sinks, v_ev_w_out, v_od_w_in, v_od_conv_w, v_od_w_out, v_ffn_w_up, v_ffn_conv_w, v_ffn_w_down):
    wts = dict(zip(WEIGHTS, (mix_norm_pre, mix_norm_post, ffn_norm_pre, ffn_norm_post, ev_w_in, ev_a_conv_w, ev_a_conv_b,
                             ev_a_ln_g, ev_a_ln_b, ev_sinks, ev_w_out, od_w_in, od_conv_w, od_w_out, ffn_w_up, ffn_conv_w,
                             ffn_w_down)))
    mom = dict(zip(WEIGHTS, (m_mix_norm_pre, m_mix_norm_post, m_ffn_norm_pre, m_ffn_norm_post, m_ev_w_in, m_ev_a_conv_w,
                             m_ev_a_conv_b, m_ev_a_ln_g, m_ev_a_ln_b, m_ev_sinks, m_ev_w_out, m_od_w_in, m_od_conv_w,
                             m_od_w_out, m_ffn_w_up, m_ffn_conv_w, m_ffn_w_down)))
    var = dict(zip(WEIGHTS, (v_mix_norm_pre, v_mix_norm_post, v_ffn_norm_pre, v_ffn_norm_post, v_ev_w_in, v_ev_a_conv_w,
                             v_ev_a_conv_b, v_ev_a_ln_g, v_ev_a_ln_b, v_ev_sinks, v_ev_w_out, v_od_w_in, v_od_conv_w,
                             v_od_w_out, v_ffn_w_up, v_ffn_conv_w, v_ffn_w_down)))
    xi, yi, ci = _me()
    chip = 2 * xi + yi

    unit_rows = [_shard_rows(wts[k][l:l + 1].astype(BF16), axis) for k, l, axis in UNITS]

    def group_block(group):
        return jnp.concatenate([unit_rows[u] for u in group], axis=0)

    def unpack_group(group, landed, own):
        full = lax.dynamic_update_slice(landed, own[None], (chip, 0, 0))
        out, off = {}, 0
        for u in group:
            k, l, axis = UNITS[u]
            n = unit_rows[u].shape[0]
            native = full[:, off:off + n].reshape(N_CHIPS * n, PACK_COLS)
            off += n
            key = (lambda name: (name, l)) if k.startswith("ffn") else (lambda name: name)
            out[key(k + "_t" if axis == 2 else k)] = native
            out[key(k if axis == 2 else k + "_t")] = native.T
        return out

    small_shapes = [wts[k].shape for k in SMALL_SHARDED]
    small_all = _exchange8(_pack([wts[k] for k in SMALL_SHARDED], 8), False, "gather_small")
    blocks = [group_block(grp) for grp in GATHER_GROUPS]
    first = _gather_chips(blocks[0], "gather_mats")
    later = {}
    for stage, grp, blk in zip(("ffn0", "layer1"), GATHER_GROUPS[1:], blocks[1:]):
        later[stage] = (grp, blk, _copies_start(blk, (N_CHIPS,) + blk.shape, _gather_plan, 3, "gather_" + stage + "_start"))

    def fetch(stage, after):
        grp, blk, started = later[stage]
        own, landed = _copies_wait(started, after, _gather_plan, "gather_" + stage + "_wait")
        return unpack_group(grp, landed, own)

    w = {k: wts[k] for k in REPLICATED}
    w.update(unpack_group(GATHER_GROUPS[0], first, blocks[0]))
    per_chip = [_unpack(small_all[2 * q], small_shapes) for q in range(N_CHIPS)]
    for n, k in enumerate(SMALL_SHARDED):
        w[k] = jnp.concatenate([per_chip[q][n] for q in range(N_CHIPS)], axis=-1)
    for k in ("ev_a_conv_w", "od_conv_w"):
        w[k] = w[k][0]
    w["ev_sinks"] = w["ev_sinks"][0]
    w["mix_norm_pre"] = w["mix_norm_pre"] + sum(later[s][2][4][0, 0] for s in later)

    core = jnp.reshape(ci, (1,)).astype(jnp.int32)
    chip_arr = jnp.reshape(chip, (1,)).astype(jnp.int32)
    per_layer = {}

    def group_grads(group, g):
        gp = jnp.concatenate([(g[k, l] if k.startswith("ffn") else g[k]).reshape(N_CHIPS, -1, PACK_COLS)
                              for k, l, _ in (UNITS[u] for u in group)], axis=1)
        return gp.reshape(N_CHIPS, 2, gp.shape[1] // 2, PACK_COLS)

    def finish(group, half, other):
        red = jnp.concatenate([jnp.where(ci == 0, half, other), jnp.where(ci == 0, other, half)], axis=0)
        off = 0
        for u in group:
            k, l, axis = UNITS[u]
            n = unit_rows[u].shape[0]
            part = red[off:off + n]
            off += n
            per_layer[k, l] = part.T if axis == 2 else part

    chains = {}

    def chain_step(tag, after, g):
        group = REDUCE_GROUPS[tag]
        st = chains.setdefault(tag, {"step": 0})
        step = st["step"]
        st["step"] = step + 1
        if step == 0:
            gp = group_grads(group, g)
            rh = gp.shape[2]
            st["swap"] = _copies_start(gp, (N_CHIPS, rh, PACK_COLS), _swap_plan, N_CHIPS, f"rs_{tag}_swap_start")
            return st["swap"][4][0, 0]
        if step == 1:
            gp, sib = _copies_wait(st["swap"], after, _swap_plan, f"rs_{tag}_swap_wait")
            pair = _rs_add(gp, sib, core, f"rs_{tag}_add")
            st["ici"] = _copies_start(pair, pair.shape, _ici_plan, 3, f"rs_{tag}_ici_start")
            return st["ici"][4][0, 0]
        if step == 2:
            pair, landed = _copies_wait(st["ici"], after, _ici_plan, f"rs_{tag}_ici_wait")
            half = _rs_sum(landed, pair, chip_arr, f"rs_{tag}_sum")
            st["share"] = _copies_start(half, half.shape, _share_plan, 1, f"rs_{tag}_share_start")
            return st["share"][4][0, 0]
        half, other = _copies_wait(st["share"], after, _share_plan, f"rs_{tag}_share_wait")
        finish(group, half, other)
        return 0.0

    schedule = {"layer1_grads": ("layer1",), "ffn0_bwd_done": ("layer1",), "ffn0_grads": ("ffn0",),
                "ev_out_bwd_done": ("layer1", "ffn0"), "ev_att_bwd_done": ("layer1", "ffn0"),
                "ev_mid_bwd_done": ("ffn0",)}

    def emit(place, after, g):
        return sum(chain_step(tag, after, g) for tag in schedule.get(place, ()))

    sq, dx, g = _local_step(x[0], positions[0], loss_target[0], w, fetch, emit)
    loss = lax.psum(0.5 * jnp.sum(sq) / D_MODEL, ("x", "y", "c"))

    gp = group_grads(REDUCE_GROUPS["ev"], g)
    sib = _rs_swap(gp, "rs_swap")
    pair = _rs_add(gp, sib, core, "rs_add")
    landed = _rs_ici(pair, "rs_ici")
    half = _rs_sum(landed, pair, chip_arr, "rs_sum")
    finish(REDUCE_GROUPS["ev"], half, _rs_share(half, "rs_share"))
    grads = {k: jnp.stack([per_layer[k, l] for l in range(wts[k].shape[0])], axis=0) for k, _ in MATS}

    small_keys = REPLICATED + SMALL_SHARDED
    full_shapes = [wts[k].shape for k in REPLICATED] + [wts[k].shape[:-1] + (wts[k].shape[-1] * N_CHIPS,) for k in SMALL_SHARDED]
    sm = _exchange8(_pack([g[k] for k in small_keys], 8), True, "reduce_small")
    for k, full in zip(small_keys, _unpack(sm, full_shapes)):
        if k in SMALL_SHARDED:
            n = wts[k].shape[-1]
            full = lax.dynamic_slice_in_dim(full, chip * n, n, axis=full.ndim - 1)
        grads[k] = full

    deltas, new_m, new_v = {}, {}, {}
    for k in WEIGHTS:
        shp = wts[k].shape
        two_d = (-1, shp[-1])
        d, nm, nv = _adamw(wts[k].reshape(two_d), grads[k].reshape(two_d), mom[k].reshape(two_d), var[k].reshape(two_d),
                           "adamw_" + k)
        deltas[k], new_m[k], new_v[k] = d.reshape(shp), nm.reshape(shp), nv.reshape(shp)

    return (loss, dx[None], *[grads[k] for k in WEIGHTS], *[deltas[k] for k in WEIGHTS],
            *[new_m[k] for k in WEIGHTS], *[new_v[k] for k in WEIGHTS])
```

```python
import functools

import jax
import jax.numpy as jnp
import numpy as np
from jax import lax
from jax.experimental import pallas as pl
from jax.experimental.pallas import tpu as pltpu

F32 = jnp.float32
BF16 = jnp.bfloat16
MESH = pl.DeviceIdType.MESH

D_MODEL = 1024
HEAD_DIM = 64
A_CH = 512
A_CONV = 31
N_Q_HEADS = 8
WINDOW = 128
ROPE_THETA = 500000.0
ROPE_DIM = 16
D_FF = 2816
RMS_EPS = 1e-6
LN_EPS = 1e-5
ADAM_LR = 0.001
ADAM_B1 = 0.9
ADAM_B2 = 0.999
ADAM_EPS = 1e-08
ADAM_WD = 0.01
ADAM_STEP = 10

LANES = 128
HALO16 = 16
HALO32 = 32
VMEM_LIMIT = 56 * 1024 * 1024
FFN_BWD_VMEM = 60 * 1024 * 1024
N_CHIPS = 4


def _cparams(sem):
    return pltpu.CompilerParams(dimension_semantics=sem, vmem_limit_bytes=VMEM_LIMIT)


def _tile(n, pref):
    if n <= pref:
        return n
    t = (pref // LANES) * LANES
    while t >= LANES:
        if n % t == 0:
            return t
        t -= LANES
    return n


MM_ROWS = 512


def _rms_scale(v):
    return lax.rsqrt(jnp.mean(v * v, axis=-1, keepdims=True) + RMS_EPS)


def _rms_bwd(dy, v, g):
    r = _rms_scale(v)
    nrm = v * r
    dn = dy * g
    return r * (dn - nrm * jnp.mean(dn * nrm, axis=-1, keepdims=True)), jnp.sum(dy * nrm, axis=0, keepdims=True)


def _mm_post(a, w, g, xres, name):
    s, k = a.shape
    d = w.shape[1]
    tm = min(MM_ROWS, s)

    def body(a_ref, w_ref, g_ref, x_ref, m_ref, o_ref):
        mv = jnp.dot(a_ref[...], w_ref[...], preferred_element_type=F32)
        m_ref[...] = mv
        o_ref[...] = x_ref[...] + mv * _rms_scale(mv) * g_ref[...]

    row = pl.BlockSpec((tm, d), lambda i: (i, 0))
    return pl.pallas_call(
        body, name=name, grid=(s // tm,),
        in_specs=[pl.BlockSpec((tm, k), lambda i: (i, 0)), _full((k, d)), _full((1, d)), row],
        out_specs=[row, row],
        out_shape=[jax.ShapeDtypeStruct((s, d), F32), jax.ShapeDtypeStruct((s, d), F32)],
        compiler_params=_cparams(("parallel",)),
    )(a, w, g, xres)


def _mm_post_bwd(dy, m, g, w_t, name):
    s, d = m.shape
    k = w_t.shape[1]
    tm = min(MM_ROWS, s)

    def body(dy_ref, m_ref, g_ref, wt_ref, dm_ref, da_ref, dg_ref):
        @pl.when(pl.program_id(0) == 0)
        def _():
            dg_ref[...] = jnp.zeros_like(dg_ref)

        dm, dg = _rms_bwd(dy_ref[...], m_ref[...], g_ref[...])
        dg_ref[...] += dg
        dmb = dm.astype(BF16)
        dm_ref[...] = dmb
        da_ref[...] = jnp.dot(dmb, wt_ref[...], preferred_element_type=F32)

    row = pl.BlockSpec((tm, d), lambda i: (i, 0))
    return pl.pallas_call(
        body, name=name, grid=(s // tm,),
        in_specs=[row, row, _full((1, d)), _full((d, k))],
        out_specs=[row, pl.BlockSpec((tm, k), lambda i: (i, 0)), _full((1, d))],
        out_shape=[jax.ShapeDtypeStruct((s, d), BF16), jax.ShapeDtypeStruct((s, k), F32),
                   jax.ShapeDtypeStruct((1, d), F32)],
        compiler_params=_cparams(("arbitrary",)),
    )(dy, m, g, w_t)


def _mm_tn(a, b, name):
    s, k = a.shape
    _, n = b.shape
    tk = _tile(k, 1408)
    tn = _tile(n, 1408)
    ts = min(2048, s)

    def body(a_ref, b_ref, o_ref):
        @pl.when(pl.program_id(2) == 0)
        def _():
            o_ref[...] = jnp.zeros_like(o_ref)

        o_ref[...] += lax.dot_general(a_ref[...], b_ref[...], (((0,), (0,)), ((), ())),
                                      preferred_element_type=F32)

    return pl.pallas_call(
        body, name=name, grid=(k // tk, n // tn, s // ts),
        in_specs=[pl.BlockSpec((ts, tk), lambda i, j, l: (l, i)), pl.BlockSpec((ts, tn), lambda i, j, l: (l, j))],
        out_specs=pl.BlockSpec((tk, tn), lambda i, j, l: (i, j)),
        out_shape=jax.ShapeDtypeStruct((k, n), F32),
        compiler_params=_cparams(("parallel", "parallel", "arbitrary")),
    )(a, b)


def _cur(tr, w, col=0):
    return pl.BlockSpec((tr, w), lambda i: (i, col))


def _prev(tr, h, w, col=0):
    return pl.BlockSpec((h, w), lambda i: (jnp.maximum(i * (tr // h) - 1, 0), col))


def _next(tr, h, w, nrows, col=0):
    last = nrows // h - 1
    return pl.BlockSpec((h, w), lambda i: (jnp.minimum((i + 1) * (tr // h), last), col))


def _full(shape):
    return pl.BlockSpec(shape, lambda i: tuple(0 for _ in shape))


def _silu_parts(g):
    sig = jax.nn.sigmoid(g)
    return sig, g * sig


FFN_CW = 256
FFN_NBUF = 3


def _conv3_taps(buf, w, off, rows):
    return (w[0:1] * buf[pl.ds(off, rows), :] + w[1:2] * buf[pl.ds(off + 1, rows), :]
            + w[2:3] * buf[pl.ds(off + 2, rows), :])


WHOLE_VMEM = pl.BlockSpec(memory_space=pltpu.VMEM)


def _ffn_fwd(x, g_pre, wu, conv_w, wd, g_post, name, target=None):
    s, d = x.shape
    f2 = wu.shape[1]
    f = f2 // 2
    tr = min(256, s)
    h = HALO16
    cw = FFN_CW
    head = target is not None

    def body(*refs):
        if head:
            (x_ref, gpre_ref, wu_ref, cw_ref, wd_ref, gpost_ref, t_ref, xo_ref, sq_ref, f_ref, h_ref, up_ref, u_ref,
             carry, gbuf, vbuf, facc) = refs
        else:
            (x_ref, gpre_ref, wu_ref, cw_ref, wd_ref, gpost_ref, xo_ref, f_ref, h_ref, up_ref, u_ref,
             carry, gbuf, vbuf, facc) = refs

        @pl.when(pl.program_id(0) == 0)
        def _():
            carry[...] = jnp.zeros_like(carry)
            if head:
                sq_ref[...] = jnp.zeros_like(sq_ref)

        xv = x_ref[...]
        r = lax.rsqrt(jnp.mean(xv * xv, axis=-1, keepdims=True) + RMS_EPS)
        hv = (xv * r * gpre_ref[...]).astype(BF16)
        h_ref[...] = hv
        nchunk = f // cw

        def up_proj(j):
            for buf, base in ((gbuf, 0), (vbuf, f)):
                cs = slice(base + j * cw, base + (j + 1) * cw)
                dst = buf.at[j % FFN_NBUF]
                upc = jnp.dot(hv, wu_ref[:, cs], preferred_element_type=F32)
                up_ref[:, cs] = upc.astype(BF16)
                dst[0:h, :] = carry[:, cs]
                dst[h:h + tr, :] = upc
                carry[:, cs] = upc[tr - h:tr, :]

        def down_proj(j, act):
            part = jnp.dot(act, wd_ref[j * cw:(j + 1) * cw, :], preferred_element_type=F32)
            if j == 0:
                facc[...] = part
            else:
                facc[...] += part

        for j in range(FFN_NBUF - 1):
            up_proj(j)
        pending = None
        for j in range(nchunk):
            cg = slice(j * cw, (j + 1) * cw)
            cv = slice(f + j * cw, f + (j + 1) * cw)
            if j + FFN_NBUF - 1 < nchunk:
                up_proj(j + FFN_NBUF - 1)
            if pending is not None:
                down_proj(*pending)
            g = _conv3_taps(gbuf.at[j % FFN_NBUF], cw_ref[:, cg], h - 2, tr)
            v = _conv3_taps(vbuf.at[j % FFN_NBUF], cw_ref[:, cv], h - 2, tr)
            u_ref[:, cg] = g.astype(BF16)
            u_ref[:, cv] = v.astype(BF16)
            act = (g * jax.nn.sigmoid(g) * v).astype(BF16)
            pending = (j, act)
        down_proj(*pending)
        fv = facc[...]
        f_ref[...] = fv
        r2 = lax.rsqrt(jnp.mean(fv * fv, axis=-1, keepdims=True) + RMS_EPS)
        xo = xv + fv * r2 * gpost_ref[...]
        if head:
            err = xo - t_ref[...]
            xo_ref[...] = err * (1.0 / d)
            sq_ref[...] += jnp.sum(err * err, axis=0, keepdims=True)
        else:
            xo_ref[...] = xo

    row = _cur(tr, d)
    wide = _cur(tr, f2)
    vec = _full((1, d))
    out_specs = [row] + ([vec] if head else []) + [row, row, wide, wide]
    out_shape = ([jax.ShapeDtypeStruct((s, d), F32)] + ([jax.ShapeDtypeStruct((1, d), F32)] if head else [])
                 + [jax.ShapeDtypeStruct((s, d), F32), jax.ShapeDtypeStruct((s, d), BF16),
                    jax.ShapeDtypeStruct((s, f2), BF16), jax.ShapeDtypeStruct((s, f2), BF16)])
    return pl.pallas_call(
        body, name=name, grid=(s // tr,),
        in_specs=[row, vec, WHOLE_VMEM, _full((3, f2)), WHOLE_VMEM, vec] + ([row] if head else []),
        out_specs=out_specs, out_shape=out_shape,
        scratch_shapes=[pltpu.VMEM((h, f2), F32), pltpu.VMEM((FFN_NBUF, h + tr, cw), F32),
                        pltpu.VMEM((FFN_NBUF, h + tr, cw), F32), pltpu.VMEM((tr, d), F32)],
        compiler_params=_cparams(("arbitrary",)),
    )(*((x, g_pre, wu, conv_w, wd, g_post) + ((target,) if head else ())))


def _ffn_bwd(dxo, fout, x, up, u, g_pre, g_post, wd_t, wu_t, conv_w, name):
    s, d = x.shape
    f2 = up.shape[1]
    f = f2 // 2
    tr = min(256, s)
    nt = s // tr
    h = HALO16
    cw = FFN_CW

    def body(dy_ref, f_ref, x_ref, up_ref, u_ref, gpre_ref, gpost_ref, wdt_ref, wut_ref, cw_ref,
             dx_ref, dup_ref, act_ref, df_ref, dcw_ref, dgpost_ref, dgpre_ref, carry, dgbuf, dvbuf, dhacc):
        @pl.when(pl.program_id(0) == 0)
        def _():
            carry[...] = jnp.zeros_like(carry)
            dcw_ref[...] = jnp.zeros_like(dcw_ref)
            dgpost_ref[...] = jnp.zeros_like(dgpost_ref)
            dgpre_ref[...] = jnp.zeros_like(dgpre_ref)

        dy = dy_ref[...]
        fv = f_ref[...]
        r = lax.rsqrt(jnp.mean(fv * fv, axis=-1, keepdims=True) + RMS_EPS)
        nrm = fv * r
        dn = dy * gpost_ref[...]
        dfv = (r * (dn - nrm * jnp.mean(dn * nrm, axis=-1, keepdims=True))).astype(BF16)
        dgpost_ref[...] += jnp.sum(dy * nrm, axis=0, keepdims=True)
        df_ref[...] = dfv
        nchunk = f // cw

        def dh_part(dupb, cs, first):
            part = jnp.dot(dupb, wut_ref[cs, :], preferred_element_type=F32)
            if first:
                dhacc[...] = part
            else:
                dhacc[...] += part

        def dact_of(j):
            return jnp.dot(dfv, wdt_ref[:, j * cw:(j + 1) * cw], preferred_element_type=F32)

        ahead = [dact_of(0)]
        for j in range(nchunk):
            ch = slice(j * cw, (j + 1) * cw)
            cg = ch
            cv = slice(f + j * cw, f + (j + 1) * cw)
            dact = ahead.pop(0)
            if j + 1 < nchunk:
                ahead.append(dact_of(j + 1))
            g = u_ref[:, cg].astype(F32)
            v = u_ref[:, cv].astype(F32)
            sig, sil = _silu_parts(g)
            act_ref[:, ch] = (sil * v).astype(BF16)
            du_g = dact * v * (sig * (1.0 + g * (1.0 - sig)))
            du_v = dact * sil
            for k, (dbuf, du, cs) in enumerate(((dgbuf.at[j % FFN_NBUF], du_g, cg), (dvbuf.at[j % FFN_NBUF], du_v, cv))):
                dbuf[0:tr, :] = du
                dbuf[tr:tr + h, :] = carry[:, cs]
                carry[:, cs] = du[0:h, :]
                w = cw_ref[:, cs]
                xin = up_ref[:, cs].astype(F32)
                acc = None
                for sh in range(3):
                    dsh = dbuf[pl.ds(sh, tr), :]
                    term = w[2 - sh:3 - sh] * dsh
                    acc = term if acc is None else acc + term
                    dcw_ref[2 - sh:3 - sh, cs] += jnp.sum(xin * dsh, axis=0, keepdims=True)
                dupb = acc.astype(BF16)
                dup_ref[:, cs] = dupb
                dh_part(dupb, cs, j == 0 and k == 0)
        dh = dhacc[...]
        xv = x_ref[...]
        r1 = lax.rsqrt(jnp.mean(xv * xv, axis=-1, keepdims=True) + RMS_EPS)
        n1 = xv * r1
        dn1 = dh * gpre_ref[...]
        dx_ref[...] = dy + r1 * (dn1 - n1 * jnp.mean(dn1 * n1, axis=-1, keepdims=True))
        dgpre_ref[...] += jnp.sum(dh * n1, axis=0, keepdims=True)

    def rev(w):
        return pl.BlockSpec((tr, w), lambda i: (nt - 1 - i, 0))

    vec = _full((1, d))
    return pl.pallas_call(
        body, name=name, grid=(nt,),
        in_specs=[rev(d), rev(d), rev(d), rev(f2), rev(f2), vec, vec, WHOLE_VMEM, WHOLE_VMEM, _full((3, f2))],
        out_specs=[rev(d), rev(f2), rev(f), rev(d), _full((3, f2)), vec, vec],
        out_shape=[jax.ShapeDtypeStruct((s, d), F32), jax.ShapeDtypeStruct((s, f2), BF16),
                   jax.ShapeDtypeStruct((s, f), BF16), jax.ShapeDtypeStruct((s, d), BF16),
                   jax.ShapeDtypeStruct((3, f2), F32), jax.ShapeDtypeStruct((1, d), F32),
                   jax.ShapeDtypeStruct((1, d), F32)],
        scratch_shapes=[pltpu.VMEM((h, f2), F32), pltpu.VMEM((FFN_NBUF, tr + h, cw), F32),
                        pltpu.VMEM((FFN_NBUF, tr + h, cw), F32), pltpu.VMEM((tr, d), F32)],
        compiler_params=pltpu.CompilerParams(dimension_semantics=("arbitrary",), vmem_limit_bytes=FFN_BWD_VMEM),
    )(dxo, fout, x, up, u, g_pre, g_post, wd_t, wu_t, conv_w)


def _od_fwd(x, g, w, conv_w, w_out, g_post, name):
    s, d = x.shape
    d3 = w.shape[1]
    tr = min(512, s)
    h = HALO16
    cw = FFN_CW
    nchunk = d // cw

    def body(x_ref, g_ref, w_ref, cw_ref, wout_ref, gpost_ref, h_ref, z_ref, y_ref, m_ref, xo_ref, carry, buf, macc):
        @pl.when(pl.program_id(0) == 0)
        def _():
            carry[...] = jnp.zeros_like(carry)

        xv = x_ref[...]
        hv = (xv * _rms_scale(xv) * g_ref[...]).astype(BF16)
        h_ref[...] = hv

        def project(j):
            out = []
            for part in range(3):
                cs = slice(part * d + j * cw, part * d + (j + 1) * cw)
                zc = jnp.dot(hv, w_ref[:, cs], preferred_element_type=F32).astype(BF16)
                z_ref[:, cs] = zc
                out.append(zc.astype(F32))
            return out

        ahead = [project(0), project(1)]
        for j in range(nchunk):
            cb = slice(j * cw, (j + 1) * cw)
            bval, cval, uval = ahead.pop(0)
            if j + 2 < nchunk:
                ahead.append(project(j + 2))
            bf = buf.at[j % FFN_NBUF]
            cu = cval * uval
            bf[0:h, :] = carry[:, cb]
            bf[h:h + tr, :] = cu
            carry[:, cb] = cu[tr - h:tr, :]
            yv = (bval * _conv3_taps(bf, cw_ref[:, cb], h - 2, tr)).astype(BF16)
            y_ref[:, cb] = yv
            part = jnp.dot(yv, wout_ref[cb, :], preferred_element_type=F32)
            if j == 0:
                macc[...] = part
            else:
                macc[...] += part
        mv = macc[...]
        m_ref[...] = mv
        xo_ref[...] = xv + mv * _rms_scale(mv) * gpost_ref[...]

    row = _cur(tr, d)
    vec = _full((1, d))
    return pl.pallas_call(
        body, name=name, grid=(s // tr,),
        in_specs=[row, vec, WHOLE_VMEM, _full((3, d)), WHOLE_VMEM, vec],
        out_specs=[row, _cur(tr, d3), row, row, row],
        out_shape=[jax.ShapeDtypeStruct((s, d), BF16), jax.ShapeDtypeStruct((s, d3), BF16),
                   jax.ShapeDtypeStruct((s, d), BF16), jax.ShapeDtypeStruct((s, d), F32),
                   jax.ShapeDtypeStruct((s, d), F32)],
        scratch_shapes=[pltpu.VMEM((h, d), F32), pltpu.VMEM((FFN_NBUF, h + tr, cw), F32), pltpu.VMEM((tr, d), F32)],
        compiler_params=_cparams(("arbitrary",)),
    )(x, g, w, conv_w, w_out, g_post)


def _od_bwd(dxo, m, g_post, w_out_t, z, conv_w, w_t, x, g, name):
    s, d3 = z.shape
    d = d3 // 3
    tr = min(512, s)
    nt = s // tr
    h = HALO16
    cw = FFN_CW
    ext = tr + h
    nchunk = d // cw

    def body(dxo_ref, m_ref, gpost_ref, wot_ref, z_ref, zp_ref, w_ref, wt_ref, x_ref, g_ref,
             dm_ref, o_ref, dw_ref, dx_ref, dgpost_ref, dg_ref, carry, buf, dbuf, dhacc):
        i = pl.program_id(0)
        row0 = i == nt - 1

        @pl.when(i == 0)
        def _():
            carry[...] = jnp.zeros_like(carry)
            dw_ref[...] = jnp.zeros_like(dw_ref)
            dg_ref[...] = jnp.zeros_like(dg_ref)
            dgpost_ref[...] = jnp.zeros_like(dgpost_ref)

        dyo = dxo_ref[...]
        dm, dgp = _rms_bwd(dyo, m_ref[...], gpost_ref[...])
        dgpost_ref[...] += dgp
        dmb = dm.astype(BF16)
        dm_ref[...] = dmb

        def dy_of(j):
            return jnp.dot(dmb, wot_ref[:, j * cw:(j + 1) * cw], preferred_element_type=F32)

        ahead = [dy_of(0)]
        started = False
        for j in range(nchunk):
            cb = slice(j * cw, (j + 1) * cw)
            cc = slice(d + j * cw, d + (j + 1) * cw)
            cu = slice(2 * d + j * cw, 2 * d + (j + 1) * cw)
            dyv = ahead.pop(0)
            if j + 1 < nchunk:
                ahead.append(dy_of(j + 1))
            bf = buf.at[j % FFN_NBUF]
            db = dbuf.at[j % FFN_NBUF]
            w = w_ref[:, cb]
            cval = z_ref[:, cc].astype(F32)
            uval = z_ref[:, cu].astype(F32)
            bf[0:h, :] = jnp.where(row0, 0.0, zp_ref[:, cc].astype(F32) * zp_ref[:, cu].astype(F32))
            bf[h:h + tr, :] = cval * uval
            k = _conv3_taps(bf, w, h - 2, tr)
            dk = dyv * z_ref[:, cb].astype(F32)
            db[0:tr, :] = dk
            db[tr:ext, :] = carry[:, cb]
            carry[:, cb] = dk[0:h, :]
            dcu = w[2:3] * db[pl.ds(0, tr), :] + w[1:2] * db[pl.ds(1, tr), :] + w[0:1] * db[pl.ds(2, tr), :]
            for t in range(3):
                dw_ref[t:t + 1, cb] += jnp.sum(dk * bf[pl.ds(h - 2 + t, tr), :], axis=0, keepdims=True)
            for cs, val in ((cb, dyv * k), (cc, dcu * uval), (cu, dcu * cval)):
                piece = val.astype(BF16)
                o_ref[:, cs] = piece
                part = jnp.dot(piece, wt_ref[cs, :], preferred_element_type=F32)
                if started:
                    dhacc[...] += part
                else:
                    dhacc[...] = part
                    started = True
        dx, dg = _rms_bwd(dhacc[...], x_ref[...], g_ref[...])
        dg_ref[...] += dg
        dx_ref[...] = dyo + dx

    def rev(w):
        return pl.BlockSpec((tr, w), lambda i: (nt - 1 - i, 0))

    prev = pl.BlockSpec((h, d3), lambda i: (jnp.maximum((nt - 1 - i) * (tr // h) - 1, 0), 0))
    vec = _full((1, d))
    return pl.pallas_call(
        body, name=name, grid=(nt,),
        in_specs=[rev(d), rev(d), vec, WHOLE_VMEM, rev(d3), prev, _full((3, d)), WHOLE_VMEM, rev(d), vec],
        out_specs=[rev(d), rev(d3), _full((3, d)), rev(d), vec, vec],
        out_shape=[jax.ShapeDtypeStruct((s, d), BF16), jax.ShapeDtypeStruct((s, d3), BF16),
                   jax.ShapeDtypeStruct((3, d), F32), jax.ShapeDtypeStruct((s, d), F32),
                   jax.ShapeDtypeStruct((1, d), F32), jax.ShapeDtypeStruct((1, d), F32)],
        scratch_shapes=[pltpu.VMEM((h, d), F32), pltpu.VMEM((FFN_NBUF, h + tr, cw), F32),
                        pltpu.VMEM((FFN_NBUF, ext, cw), F32), pltpu.VMEM((tr, d), F32)],
        compiler_params=_cparams(("arbitrary",)),
    )(dxo, m, g_post, w_out_t, z, z, conv_w, w_t, x, g)


Q0 = 2 * A_CH
K0 = Q0 + N_Q_HEADS * HEAD_DIM
V0 = K0 + 2 * HEAD_DIM
EVEN_IN = V0 + 2 * HEAD_DIM


def _rope_tables(positions):
    half = ROPE_DIM // 2
    inv_freq = ROPE_THETA ** (-(jnp.arange(half, dtype=F32) * 2.0 / ROPE_DIM))
    ang = positions.astype(F32)[:, None] * inv_freq
    cs = jnp.concatenate([jnp.cos(ang), jnp.sin(ang)], axis=1)
    spread = np.zeros((2 * half, 3 * LANES), np.float32)
    const = np.zeros((1, 3 * LANES), np.float32)
    for lane in range(3 * LANES):
        dim, part = lane % HEAD_DIM, lane // LANES
        if part == 0:
            if dim < ROPE_DIM:
                spread[dim % half, lane] = 1.0
            else:
                const[0, lane] = 1.0
        elif part == 1 and half <= dim < ROPE_DIM:
            spread[half + dim - half, lane] = 1.0
        elif part == 2 and dim < half:
            spread[half + dim, lane] = -1.0
    return jnp.dot(cs, jnp.asarray(spread), precision=lax.Precision.HIGHEST) + jnp.asarray(const)


def _rope_fwd(x, tab):
    c, sa, sb = tab[:, 0:LANES], tab[:, LANES:2 * LANES], tab[:, 2 * LANES:3 * LANES]
    return x * c + pltpu.roll(x, 8, 1) * sa + pltpu.roll(x, LANES - 8, 1) * sb


def _rope_bwd(dy, tab):
    c, sa, sb = tab[:, 0:LANES], tab[:, LANES:2 * LANES], tab[:, 2 * LANES:3 * LANES]
    return dy * c + pltpu.roll(dy * sa, LANES - 8, 1) + pltpu.roll(dy * sb, 8, 1)


def _ln_fwd(c, g, b):
    mu = jnp.mean(c, axis=-1, keepdims=True)
    xc = c - mu
    r = lax.rsqrt(jnp.mean(xc * xc, axis=-1, keepdims=True) + LN_EPS)
    nrm = xc * r
    return nrm, r, nrm * g + b


def _phase_fill(buf, ph, rows):
    for k in range(1, 8):
        ph[k - 1, 0:rows - 8, :] = buf[pl.ds(k, rows - 8), :]


def _phase_rows(buf, ph, off, n, cs):
    k = off % 8
    src = buf if k == 0 else ph.at[k - 1]
    return src[pl.ds(off - k, n), cs]


def _ev_in_fwd(x, g_pre, w_in, tab, conv_w, conv_b, ln_g, ln_b, name):
    s, d = x.shape
    tr = min(512, s)
    h = HALO32
    cw = LANES
    pw = 2 * LANES

    def body(x_ref, gpre_ref, win_ref, tab_ref, w_ref, b_ref, g_ref, lb_ref, h_ref, z_ref, c_ref, a_ref, qkv_ref,
             gbuf, cbuf, gph, carry):
        @pl.when(pl.program_id(0) == 0)
        def _():
            carry[...] = jnp.zeros_like(carry)

        xv = x_ref[...]
        hv = (xv * _rms_scale(xv) * gpre_ref[...]).astype(BF16)
        h_ref[...] = hv

        def project(lo_col, hi_col):
            for c0 in range(lo_col, hi_col, pw):
                cs = slice(c0, c0 + pw)
                z_ref[:, cs] = jnp.dot(hv, win_ref[:, cs], preferred_element_type=F32).astype(BF16)

        project(0, 2 * A_CH)
        glu = z_ref[:, 0:A_CH].astype(F32) * jax.nn.sigmoid(z_ref[:, A_CH:2 * A_CH].astype(F32))
        project(2 * A_CH, EVEN_IN)
        gbuf[0:h, :] = carry[...]
        gbuf[h:h + tr, :] = glu
        carry[...] = glu[tr - h:tr, :]
        _phase_fill(gbuf, gph, h + tr)
        for j in range(A_CH // cw):
            cs = slice(j * cw, (j + 1) * cw)
            acc = jnp.broadcast_to(b_ref[:, cs], (tr, cw))
            for t in range(A_CONV):
                acc = acc + w_ref[t:t + 1, cs] * _phase_rows(gbuf, gph, h - (A_CONV - 1) + t, tr, cs)
            cbuf[:, cs] = acc
        c = cbuf[...]
        c_ref[...] = c.astype(BF16)
        _, _, l = _ln_fwd(c, g_ref[...], lb_ref[...])
        a_ref[...] = (l * jax.nn.sigmoid(l)).astype(BF16)
        tab_v = tab_ref[...]
        for p in range(4):
            xq = z_ref[:, Q0 + p * LANES:Q0 + (p + 1) * LANES].astype(F32)
            qkv_ref[:, p * LANES:(p + 1) * LANES] = _rope_fwd(xq, tab_v).astype(BF16)
        lane = lax.broadcasted_iota(jnp.int32, (tr, LANES), 1)
        lo = lane < HEAD_DIM
        kr = _rope_fwd(z_ref[:, K0:K0 + LANES].astype(F32), tab_v)
        vr = z_ref[:, V0:V0 + LANES].astype(F32)
        for base, val in ((4 * LANES, kr), (6 * LANES, vr)):
            sw = pltpu.roll(val, HEAD_DIM, 1)
            qkv_ref[:, base:base + LANES] = jnp.where(lo, val, sw).astype(BF16)
            qkv_ref[:, base + LANES:base + 2 * LANES] = jnp.where(lo, sw, val).astype(BF16)

    return pl.pallas_call(
        body, name=name, grid=(s // tr,),
        in_specs=[_cur(tr, d), _full((1, d)), WHOLE_VMEM, _cur(tr, 3 * LANES), _full((A_CONV, A_CH)),
                  _full((1, A_CH)), _full((1, A_CH)), _full((1, A_CH))],
        out_specs=[_cur(tr, d), _cur(tr, EVEN_IN), _cur(tr, A_CH), _cur(tr, A_CH), _cur(tr, 2 * A_CH)],
        out_shape=[jax.ShapeDtypeStruct((s, d), BF16), jax.ShapeDtypeStruct((s, EVEN_IN), BF16),
                   jax.ShapeDtypeStruct((s, A_CH), BF16), jax.ShapeDtypeStruct((s, A_CH), BF16),
                   jax.ShapeDtypeStruct((s, 2 * A_CH), BF16)],
        scratch_shapes=[pltpu.VMEM((h + tr, A_CH), F32), pltpu.VMEM((tr, A_CH), F32),
                        pltpu.VMEM((7, h + tr, A_CH), F32), pltpu.VMEM((h, A_CH), F32)],
        compiler_params=_cparams(("arbitrary",)),
    )(x, g_pre, w_in, tab, conv_w, conv_b, ln_g, ln_b)


def _ev_mid_bwd(dcat, c, z, dq, dkv, tab, conv_w, ln_g, ln_b, w_t, x, g_pre, res, name):
    s = z.shape[0]
    tr = min(512, s)
    h = HALO32
    cw = LANES
    ext = tr + h

    def body(da_ref, dan_ref, c_ref, cn_ref, z_ref, dq_ref, dkv_ref, tab_ref, w_ref, g_ref, lb_ref,
             wt_ref, x_ref, gpre_ref, res_ref, dz_ref, dw_ref, dvec_ref, dx_ref, dg_ref, dcbuf, dcph, dhacc):
        i = pl.program_id(0)
        first = i == 0
        last = i == pl.num_programs(0) - 1

        @pl.when(first)
        def _():
            dw_ref[...] = jnp.zeros_like(dw_ref)
            dvec_ref[...] = jnp.zeros_like(dvec_ref)
            dg_ref[...] = jnp.zeros_like(dg_ref)

        started = []

        def dh_part(cs):
            part = jnp.dot(dz_ref[:, cs], wt_ref[cs, :], preferred_element_type=F32)
            if started:
                dhacc[...] += part
            else:
                dhacc[...] = part
                started.append(True)

        tab_v = tab_ref[...]
        for p in range(4):
            cs = slice(p * LANES, (p + 1) * LANES)
            dz_ref[:, Q0 + p * LANES:Q0 + (p + 1) * LANES] = _rope_bwd(dq_ref[:, cs], tab_v).astype(BF16)
        lane = lax.broadcasted_iota(jnp.int32, (tr, LANES), 1)
        lo = lane < HEAD_DIM

        def fold(base):
            p0 = dkv_ref[:, base:base + LANES]
            p1 = dkv_ref[:, base + LANES:base + 2 * LANES]
            s0 = p0 + pltpu.roll(p0, HEAD_DIM, 1)
            s1 = p1 + pltpu.roll(p1, HEAD_DIM, 1)
            return jnp.where(lo, s0, s1)

        dz_ref[:, K0:K0 + LANES] = _rope_bwd(fold(0), tab_v).astype(BF16)
        dz_ref[:, V0:V0 + LANES] = fold(2 * LANES).astype(BF16)
        dh_part(slice(Q0, EVEN_IN))

        gv = g_ref[...]

        def ln_silu_bwd(cv, dav):
            nrm, r, l = _ln_fwd(cv, gv, lb_ref[...])
            sig = jax.nn.sigmoid(l)
            dl = dav * (sig * (1.0 + l * (1.0 - sig)))
            dn = dl * gv
            dc = r * (dn - jnp.mean(dn, axis=-1, keepdims=True) - nrm * jnp.mean(dn * nrm, axis=-1, keepdims=True))
            return dc, dl, nrm

        dc, dl, nrm = ln_silu_bwd(c_ref[...].astype(F32), da_ref[...])
        dcn, _, _ = ln_silu_bwd(cn_ref[...].astype(F32), dan_ref[...])
        dcbuf[0:tr, :] = dc
        dcbuf[tr:ext, :] = jnp.where(last, 0.0, dcn)
        dvec_ref[0:1, :] += jnp.sum(dc, axis=0, keepdims=True)
        dvec_ref[1:2, :] += jnp.sum(dl * nrm, axis=0, keepdims=True)
        dvec_ref[2:3, :] += jnp.sum(dl, axis=0, keepdims=True)

        _phase_fill(dcbuf, dcph, ext)
        a_lin = z_ref[:, 0:A_CH].astype(F32)
        sig_g = jax.nn.sigmoid(z_ref[:, A_CH:2 * A_CH].astype(F32))
        glu = a_lin * sig_g
        for j in range(A_CH // cw):
            cs = slice(j * cw, (j + 1) * cw)
            gluj = glu[:, cs]
            acc = jnp.zeros((tr, cw), F32)
            for t in range(A_CONV):
                dsh = _phase_rows(dcbuf, dcph, A_CONV - 1 - t, tr, cs)
                acc = acc + w_ref[t:t + 1, cs] * dsh
                dw_ref[t:t + 1, cs] += jnp.sum(gluj * dsh, axis=0, keepdims=True)
            dz_ref[:, cs] = (acc * sig_g[:, cs]).astype(BF16)
            dz_ref[:, A_CH + j * cw:A_CH + (j + 1) * cw] = (
                acc * a_lin[:, cs] * sig_g[:, cs] * (1.0 - sig_g[:, cs])).astype(BF16)
            if j % 2 == 1:
                dh_part(slice((j - 1) * cw, (j + 1) * cw))
                dh_part(slice(A_CH + (j - 1) * cw, A_CH + (j + 1) * cw))

        dx, dg = _rms_bwd(dhacc[...], x_ref[...], gpre_ref[...])
        dg_ref[...] += dg
        dx_ref[...] = res_ref[...] + dx

    row = _cur(tr, D_MODEL)
    vec = _full((1, D_MODEL))
    return pl.pallas_call(
        body, name=name, grid=(s // tr,),
        in_specs=[_cur(tr, A_CH), _next(tr, h, A_CH, s), _cur(tr, A_CH), _next(tr, h, A_CH, s),
                  _cur(tr, EVEN_IN), _cur(tr, A_CH), _cur(tr, A_CH), _cur(tr, 3 * LANES),
                  _full((A_CONV, A_CH)), _full((1, A_CH)), _full((1, A_CH)), WHOLE_VMEM, row, vec, row],
        out_specs=[_cur(tr, EVEN_IN), _full((A_CONV, A_CH)), _full((8, A_CH)), row, vec],
        out_shape=[jax.ShapeDtypeStruct((s, EVEN_IN), BF16), jax.ShapeDtypeStruct((A_CONV, A_CH), F32),
                   jax.ShapeDtypeStruct((8, A_CH), F32), jax.ShapeDtypeStruct((s, D_MODEL), F32),
                   jax.ShapeDtypeStruct((1, D_MODEL), F32)],
        scratch_shapes=[pltpu.VMEM((ext, A_CH), F32), pltpu.VMEM((7, ext, A_CH), F32),
                        pltpu.VMEM((tr, D_MODEL), F32)],
        compiler_params=_cparams(("arbitrary",)),
    )(dcat, dcat, c, c, z, dq, dkv, tab, conv_w, ln_g, ln_b, w_t, x, g_pre, res)


NT = (((1,), (1,)), ((), ()))
TN = (((0,), (0,)), ((), ()))
QB = WINDOW
SCALE = HEAD_DIM ** -0.5
ATT_AHEAD = 2


def _att_scores(q2m, kwin):
    return lax.dot_general(q2m, kwin, NT, preferred_element_type=F32)


def _att_probs(raw, sink, mask):
    sc = jnp.where(mask, raw * SCALE, -jnp.inf)
    mx = jnp.maximum(jnp.max(sc, axis=-1, keepdims=True), sink)
    p = jnp.exp(sc - mx)
    ps = jnp.exp(sink - mx)
    inv = 1.0 / (jnp.sum(p, axis=-1, keepdims=True) + ps)
    return p * inv, ps * inv


def _att_mask(i):
    r = lax.broadcasted_iota(jnp.int32, (QB, 2 * QB), 0)
    kc = lax.broadcasted_iota(jnp.int32, (QB, 2 * QB), 1)
    diff = r + QB - kc
    return (diff >= 0) & (diff < WINDOW) & ((kc >= QB) | (i > 0))


def _half_masks(dtype):
    lane = lax.broadcasted_iota(jnp.int32, (1, LANES), 1)
    return (lane < HEAD_DIM).astype(dtype), (lane >= HEAD_DIM).astype(dtype)


def _att_fwd(qkv, a, sinks, name):
    s = qkv.shape[0]
    nb = s // QB

    def body(sink_ref, qkv_ref, kvp_ref, a_ref, o_ref):
        i = pl.program_id(0)
        mask = _att_mask(i)
        mlo, mhi = _half_masks(BF16)
        o_ref[:, 0:A_CH] = a_ref[...]

        def window(col):
            return jnp.concatenate([kvp_ref[:, col * LANES:(col + 1) * LANES],
                                    qkv_ref[:, A_CH + col * LANES:A_CH + (col + 1) * LANES]], axis=0)

        def raw_scores(p):
            q2 = qkv_ref[:, p * LANES:(p + 1) * LANES]
            kwin = window(p // 2)
            return _att_scores(q2 * mlo, kwin), _att_scores(q2 * mhi, kwin)

        ahead = [raw_scores(p) for p in range(4)]
        for p in range(4):
            raw_e, raw_o = ahead[p]
            vwin = window(2 + p // 2)
            pe, _ = _att_probs(raw_e, sink_ref[2 * p], mask)
            po, _ = _att_probs(raw_o, sink_ref[2 * p + 1], mask)
            o = (jnp.dot(pe.astype(BF16), vwin * mlo, preferred_element_type=F32)
                 + jnp.dot(po.astype(BF16), vwin * mhi, preferred_element_type=F32))
            o_ref[:, A_CH + p * LANES:A_CH + (p + 1) * LANES] = o.astype(BF16)

    grid_spec = pltpu.PrefetchScalarGridSpec(
        num_scalar_prefetch=1, grid=(nb,),
        in_specs=[pl.BlockSpec((QB, 2 * A_CH), lambda i, sk: (i, 0)),
                  pl.BlockSpec((QB, A_CH), lambda i, sk: (jnp.maximum(i - 1, 0), 1)),
                  pl.BlockSpec((QB, A_CH), lambda i, sk: (i, 0))],
        out_specs=pl.BlockSpec((QB, 2 * A_CH), lambda i, sk: (i, 0)),
    )
    return pl.pallas_call(
        body, name=name, grid_spec=grid_spec,
        out_shape=jax.ShapeDtypeStruct((s, 2 * A_CH), BF16),
        compiler_params=_cparams(("parallel",)),
    )(sinks, qkv, qkv, a)


def _att_bwd(qkv, dcat, sinks, name):
    s = qkv.shape[0]
    nb = s // QB

    def body(sink_ref, qkv_ref, kvp_ref, do_ref, dq_ref, dkv_ref, ds_ref, carry):
        i = pl.program_id(0)

        @pl.when(i == 0)
        def _():
            ds_ref[...] = jnp.zeros_like(ds_ref)
            carry[...] = jnp.zeros_like(carry)

        @pl.when(i < nb)
        def _():
            mask = _att_mask(i)
            mlo, mhi = _half_masks(BF16)
            dwin = [jnp.zeros((2 * QB, LANES), F32) for _ in range(4)]

            def window(col):
                return jnp.concatenate([kvp_ref[:, col * LANES:(col + 1) * LANES],
                                        qkv_ref[:, A_CH + col * LANES:A_CH + (col + 1) * LANES]], axis=0)

            def first_products(n):
                p, hm = n // 2, (mlo, mhi)[n % 2]
                qm = qkv_ref[:, p * LANES:(p + 1) * LANES] * hm
                dom = do_ref[:, p * LANES:(p + 1) * LANES].astype(BF16) * hm
                kwin, vwin = window(p // 2), window(2 + p // 2)
                return (qm, dom, kwin * hm, _att_scores(qm, kwin),
                        lax.dot_general(dom, vwin, NT, preferred_element_type=F32))

            ahead = [first_products(n) for n in range(ATT_AHEAD)]
            dq2 = None
            for n in range(N_Q_HEADS):
                g = n // 4
                qm, dom, kwm, raw, dp = ahead.pop(0)
                if n + ATT_AHEAD < N_Q_HEADS:
                    ahead.append(first_products(n + ATT_AHEAD))
                prob, psink = _att_probs(raw, sink_ref[n], mask)
                delta = jnp.sum(prob * dp, axis=-1, keepdims=True)
                dsc = (prob * (dp - delta) * SCALE).astype(BF16)
                ds_ref[n:n + 1, :] += jnp.broadcast_to(jnp.sum(-psink * delta, axis=0, keepdims=True), (1, LANES))
                part = jnp.dot(dsc, kwm, preferred_element_type=F32)
                dq2 = part if n % 2 == 0 else dq2 + part
                dwin[g] = dwin[g] + lax.dot_general(dsc, qm, TN, preferred_element_type=F32)
                dwin[2 + g] = dwin[2 + g] + lax.dot_general(prob.astype(BF16), dom, TN, preferred_element_type=F32)
                if n % 2 == 1:
                    dq_ref[:, (n // 2) * LANES:(n // 2 + 1) * LANES] = dq2
            for n in range(4):
                cs = slice(n * LANES, (n + 1) * LANES)
                dkv_ref[:, cs] = carry[:, cs] + dwin[n][0:QB, :]
                carry[:, cs] = dwin[n][QB:2 * QB, :]

        @pl.when(i == nb)
        def _():
            dkv_ref[...] = carry[...]

    grid_spec = pltpu.PrefetchScalarGridSpec(
        num_scalar_prefetch=1, grid=(nb + 1,),
        in_specs=[pl.BlockSpec((QB, 2 * A_CH), lambda i, sk: (jnp.minimum(i, nb - 1), 0)),
                  pl.BlockSpec((QB, A_CH), lambda i, sk: (jnp.maximum(jnp.minimum(i, nb - 1) - 1, 0), 1)),
                  pl.BlockSpec((QB, A_CH), lambda i, sk: (jnp.minimum(i, nb - 1), 1))],
        out_specs=[pl.BlockSpec((QB, A_CH), lambda i, sk: (jnp.minimum(i, nb - 1), 0)),
                   pl.BlockSpec((QB, A_CH), lambda i, sk: (jnp.maximum(i - 1, 0), 0)),
                   pl.BlockSpec((8, LANES), lambda i, sk: (0, 0))],
        scratch_shapes=[pltpu.VMEM((QB, A_CH), F32)],
    )
    return pl.pallas_call(
        body, name=name, grid_spec=grid_spec,
        out_shape=[jax.ShapeDtypeStruct((s, A_CH), F32), jax.ShapeDtypeStruct((s, A_CH), F32),
                   jax.ShapeDtypeStruct((8, LANES), F32)],
        compiler_params=_cparams(("arbitrary",)),
    )(sinks, qkv, qkv, dcat)


def _local_step(x, positions, target, w, fetch=None, emit=None):
    row = lambda a, i: a[i:i + 1]
    tab = _rope_tables(positions)
    g = {}

    def ffn_fwd(xin, i, tgt=None):
        outs = _ffn_fwd(xin, row(w["ffn_norm_pre"], i), w["ffn_w_up", i], w["ffn_conv_w"][i],
                        w["ffn_w_down", i], row(w["ffn_norm_post"], i), f"ffn{i}_fwd", tgt)
        f, h, up, u = outs[-4:]
        return outs[:-4], (xin, f, h, up, u)

    def point(name, after):
        return emit(name, after, g) if emit is not None else 0.0

    def ffn_bwd(dxout, saved, i, tok):
        xin, f, h, up, u = saved
        dxin, dup, act, df, d_cw, dg_post, dg_pre = _ffn_bwd(
            dxout, f, xin, up, u, row(w["ffn_norm_pre"], i), row(w["ffn_norm_post"], i) + tok, w["ffn_w_down_t", i],
            w["ffn_w_up_t", i], w["ffn_conv_w"][i], f"ffn{i}_bwd")
        tok = point(f"ffn{i}_bwd_done", dxin)
        g["ffn_w_down", i] = _mm_tn(act, df, f"ffn{i}_down_dw")
        g["ffn_w_up", i] = _mm_tn(dup, h, f"ffn{i}_up_dw")
        return dxin, tok, dict(ffn_norm_post=dg_post, ffn_norm_pre=dg_pre, ffn_conv_w=d_cw)

    h0, z0, c0, a0, qkv = _ev_in_fwd(x, row(w["mix_norm_pre"], 0), w["ev_w_in"], tab, w["ev_a_conv_w"],
                                     w["ev_a_conv_b"], w["ev_a_ln_g"], w["ev_a_ln_b"], "ev_in")
    cat = _att_fwd(qkv, a0, w["ev_sinks"], "ev_att")
    m0, x1 = _mm_post(cat, w["ev_w_out"], row(w["mix_norm_post"], 0), x, "ev_out")
    if fetch is not None:
        w = {**w, **fetch("ffn0", x1)}
    (x2,), ffn0 = ffn_fwd(x1, 0)
    if fetch is not None:
        w = {**w, **fetch("layer1", x2)}
    h2, z1, y1, m1, x3 = _od_fwd(x2, row(w["mix_norm_pre"], 1), w["od_w_in"], w["od_conv_w"], w["od_w_out"],
                                 row(w["mix_norm_post"], 1), "od_fwd")
    (dx4, sq), ffn1 = ffn_fwd(x3, 1, target)

    dx3, _, gf1 = ffn_bwd(dx4, ffn1, 1, 0.0)
    dm1, dz1, g["od_conv_w"], dx2, dg_mo1, dg_mp1 = _od_bwd(
        dx3, m1, row(w["mix_norm_post"], 1), w["od_w_out_t"], z1, w["od_conv_w"], w["od_w_in_t"], x2,
        row(w["mix_norm_pre"], 1), "od_bwd")
    g["od_w_out"] = _mm_tn(y1, dm1, "od_out_dw")
    g["od_w_in"] = _mm_tn(dz1, h2, "od_in_dw")
    tok = point("layer1_grads", dx2)

    dx1, tok, gf0 = ffn_bwd(dx2, ffn0, 0, tok)
    tok = tok + point("ffn0_grads", dx1)
    dm0, dcat, dg_mo0 = _mm_post_bwd(dx1, m0, row(w["mix_norm_post"], 0) + tok, w["ev_w_out_t"], "ev_out_bwd")
    tok = point("ev_out_bwd_done", dcat)
    g["ev_w_out"] = _mm_tn(cat, dm0, "ev_out_dw")
    dq, dkv, dsk = _att_bwd(qkv, dcat, w["ev_sinks"] + tok, "ev_att_bwd")
    tok = point("ev_att_bwd_done", dq)
    dz0, g["ev_a_conv_w"], dvec, dx0, dg_mp0 = _ev_mid_bwd(
        dcat, c0, z0, dq, dkv, tab, w["ev_a_conv_w"], w["ev_a_ln_g"] + tok, w["ev_a_ln_b"], w["ev_w_in_t"], x,
        row(w["mix_norm_pre"], 0), dx1, "ev_in_bwd")
    point("ev_mid_bwd_done", dz0)
    g["ev_w_in"] = _mm_tn(dz0, h0, "ev_in_dw")

    g["ev_a_conv_b"] = dvec[0:1]
    g["ev_a_ln_g"] = dvec[1:2]
    g["ev_a_ln_b"] = dvec[2:3]
    g["ev_sinks"] = dsk[:, 0]
    g["mix_norm_pre"] = jnp.concatenate([dg_mp0, dg_mp1], axis=0)
    g["mix_norm_post"] = jnp.concatenate([dg_mo0, dg_mo1], axis=0)
    g["ffn_norm_pre"] = jnp.concatenate([gf0["ffn_norm_pre"], gf1["ffn_norm_pre"]], axis=0)
    g["ffn_norm_post"] = jnp.concatenate([gf0["ffn_norm_post"], gf1["ffn_norm_post"]], axis=0)
    g["ffn_conv_w"] = jnp.stack([gf0["ffn_conv_w"], gf1["ffn_conv_w"]], axis=0)
    return sq, dx0, g


ANY = pl.BlockSpec(memory_space=pl.ANY)
PACK_COLS = 1024


def _me():
    return lax.axis_index("x"), lax.axis_index("y"), lax.axis_index("c")


def _other_chips(x, y):
    return [(1 - x, y), (x, 1 - y), (1 - x, 1 - y)]


def _remote(src, dst, send, recv, dev):
    return pltpu.make_async_remote_copy(src_ref=src, dst_ref=dst, send_sem=send, recv_sem=recv,
                                        device_id=dev, device_id_type=MESH)


def _gather_chips(wp, name):
    r, cols = wp.shape
    rh = r // 2

    def body(w_ref, o_ref, send, recv):
        x, y, c = _me()
        p = 2 * x + y
        sib = (x, y, 1 - c)
        chips = _other_chips(x, y)
        half = pl.ds(c * rh, rh)
        other = pl.ds((1 - c) * rh, rh)
        sent = [_remote(w_ref.at[half], o_ref.at[p, half], send.at[k], recv.at[k], (cx, cy, c))
                for k, (cx, cy) in enumerate(chips)]
        for cp in sent:
            cp.start()
        for k, (cx, cy) in enumerate(chips):
            q = 2 * cx + cy
            _remote(w_ref.at[half], o_ref.at[q, half], send.at[k], recv.at[k], (cx, cy, c)).wait_recv()
            fwd = _remote(o_ref.at[q, half], o_ref.at[q, half], send.at[3 + k], recv.at[3 + k], sib)
            fwd.start()
            sent.append(fwd)
        for k, (cx, cy) in enumerate(chips):
            q = 2 * cx + cy
            _remote(o_ref.at[q, other], o_ref.at[q, other], send.at[3 + k], recv.at[3 + k], sib).wait_recv()
        for cp in sent:
            cp.wait_send()

    return pl.pallas_call(
        body, name=name, in_specs=[ANY], out_specs=ANY,
        out_shape=jax.ShapeDtypeStruct((N_CHIPS, r, cols), wp.dtype),
        scratch_shapes=[pltpu.SemaphoreType.DMA((6,)), pltpu.SemaphoreType.DMA((6,))],
    )(wp)


HBM_SPEC = pl.BlockSpec(memory_space=pltpu.HBM)
SEM_SPEC = pl.BlockSpec(memory_space=pltpu.SEMAPHORE)
DATAFLOW = pltpu.SideEffectType.DATAFLOW_SIDE_EFFECTING


def _gather_plan(w_ref, land_ref):
    x, y, c = _me()
    return [(w_ref, land_ref.at[2 * x + y], (cx, cy, c)) for cx, cy in _other_chips(x, y)]


def _copies_start(src, land_shape, plan, n, name):
    def body(src_ref, land_ref, send, recv, src_thru, land_thru, token):
        for k, (s_view, d_view, dev) in enumerate(plan(src_ref, land_ref)):
            _remote(s_view, d_view, send.at[k], recv.at[k], dev).start()
        token[...] = jnp.zeros_like(token)

    return pl.pallas_call(
        body, name=name,
        out_shape=(pltpu.SemaphoreType.DMA((n,)), pltpu.SemaphoreType.DMA((n,)), pltpu.HBM(src.shape, src.dtype),
                   pltpu.HBM(land_shape, src.dtype), jax.ShapeDtypeStruct((8, LANES), F32)),
        in_specs=(HBM_SPEC, HBM_SPEC),
        out_specs=(SEM_SPEC, SEM_SPEC, HBM_SPEC, HBM_SPEC, pl.BlockSpec(memory_space=pltpu.VMEM)),
        input_output_aliases={0: 2, 1: 3},
        compiler_params=pltpu.CompilerParams(has_side_effects=DATAFLOW),
    )(pltpu.with_memory_space_constraint(src, pltpu.HBM),
      pltpu.with_memory_space_constraint(lax.empty(land_shape, src.dtype), pltpu.HBM))


def _copies_wait(started, after, plan, name):
    send, recv, src_thru, land_thru, _ = started

    def body(src_ref, land_ref, send, recv, after_ref, src_dead, land_out):
        for k, (s_view, d_view, dev) in enumerate(plan(src_ref, land_ref)):
            cp = _remote(s_view, d_view, send.at[k], recv.at[k], dev)
            cp.wait_send()
            cp.wait_recv()

    return pl.pallas_call(
        body, name=name,
        out_shape=(pltpu.HBM(src_thru.shape, src_thru.dtype), pltpu.HBM(land_thru.shape, land_thru.dtype)),
        in_specs=(HBM_SPEC, HBM_SPEC, SEM_SPEC, SEM_SPEC, ANY),
        out_specs=(HBM_SPEC, HBM_SPEC),
        input_output_aliases={0: 0, 1: 1},
        compiler_params=pltpu.CompilerParams(has_side_effects=DATAFLOW),
    )(src_thru, land_thru, send, recv, after)


def _swap_plan(g_ref, land_ref):
    x, y, c = _me()
    return [(g_ref.at[q, 1 - c], land_ref.at[q], (x, y, 1 - c)) for q in range(N_CHIPS)]


def _ici_plan(a_ref, land_ref):
    x, y, c = _me()
    return [(a_ref.at[2 * cx + cy], land_ref.at[2 * x + y], (cx, cy, c)) for cx, cy in _other_chips(x, y)]


def _share_plan(h_ref, land_ref):
    x, y, c = _me()
    return [(h_ref, land_ref, (x, y, 1 - c))]


def _exchange8(v, reduce, name):
    r, cols = v.shape
    rel = [(a, b, d) for a in (0, 1) for b in (0, 1) for d in (0, 1) if (a, b, d) != (0, 0, 0)]

    def body(v_ref, o_ref, *rest):
        if reduce:
            gbuf, send, recv = rest
        else:
            gbuf = o_ref
            send, recv = rest
        x, y, c = _me()
        me = 4 * x + 2 * y + c
        gbuf[me] = v_ref[...]
        sent = []
        for k, (a, b, d) in enumerate(rel):
            cp = _remote(v_ref, gbuf.at[me], send.at[k], recv.at[k], ((x + a) % 2, (y + b) % 2, (c + d) % 2))
            cp.start()
            sent.append(cp)
        for k, (a, b, d) in enumerate(rel):
            src = 4 * ((x + a) % 2) + 2 * ((y + b) % 2) + (c + d) % 2
            _remote(v_ref, gbuf.at[src], send.at[k], recv.at[k], (x, y, c)).wait_recv()
        for cp in sent:
            cp.wait_send()
        if reduce:
            acc = gbuf[0]
            for n in range(1, 8):
                acc = acc + gbuf[n]
            o_ref[...] = acc

    vmem = pl.BlockSpec(memory_space=pltpu.VMEM)
    sems = [pltpu.SemaphoreType.DMA((7,)), pltpu.SemaphoreType.DMA((7,))]
    if reduce:
        out_shape = jax.ShapeDtypeStruct((r, cols), F32)
        scratch = [pltpu.VMEM((8, r, cols), F32)] + sems
    else:
        out_shape = jax.ShapeDtypeStruct((8, r, cols), F32)
        scratch = sems
    return pl.pallas_call(body, name=name, in_specs=[vmem], out_specs=vmem, out_shape=out_shape,
                          scratch_shapes=scratch)(v)


def _rs_swap(g, name):
    _, _, rh, cols = g.shape

    def body(g_ref, o_ref, send, recv):
        x, y, c = _me()
        cps = [_remote(g_ref.at[q, 1 - c], o_ref.at[q], send.at[q], recv.at[q], (x, y, 1 - c)) for q in range(N_CHIPS)]
        for cp in cps:
            cp.start()
        for cp in cps:
            cp.wait()

    return pl.pallas_call(
        body, name=name, in_specs=[ANY], out_specs=ANY,
        out_shape=jax.ShapeDtypeStruct((N_CHIPS, rh, cols), F32),
        scratch_shapes=[pltpu.SemaphoreType.DMA((N_CHIPS,)), pltpu.SemaphoreType.DMA((N_CHIPS,))],
    )(g)


def _row_tile(rows, pref, mult=8):
    if rows <= pref:
        return rows
    t = (pref // mult) * mult
    while t >= mult:
        if rows % t == 0:
            return t
        t -= mult
    return rows


def _rs_add(g, sib, c, name):
    _, _, rh, cols = g.shape
    tr = _row_tile(rh, 512, 16)

    def body(c_ref, g_ref, s_ref, o_ref):
        o_ref[...] = (g_ref[...] + s_ref[...]).astype(BF16)

    grid_spec = pltpu.PrefetchScalarGridSpec(
        num_scalar_prefetch=1, grid=(N_CHIPS, rh // tr),
        in_specs=[pl.BlockSpec((None, None, tr, cols), lambda q, i, cr: (q, cr[0], i, 0)),
                  pl.BlockSpec((None, tr, cols), lambda q, i, cr: (q, i, 0))],
        out_specs=pl.BlockSpec((None, tr, cols), lambda q, i, cr: (q, i, 0)),
    )
    return pl.pallas_call(
        body, name=name, grid_spec=grid_spec,
        out_shape=jax.ShapeDtypeStruct((N_CHIPS, rh, cols), BF16),
        compiler_params=_cparams(("parallel", "parallel")),
    )(c, g, sib)


def _rs_ici(a, name):
    _, rh, cols = a.shape

    def body(a_ref, o_ref, send, recv):
        x, y, c = _me()
        p = 2 * x + y
        cps = []
        for k, (cx, cy) in enumerate(_other_chips(x, y)):
            cp = _remote(a_ref.at[2 * cx + cy], o_ref.at[p], send.at[k], recv.at[k], (cx, cy, c))
            cp.start()
            cps.append(cp)
        for k, (cx, cy) in enumerate(_other_chips(x, y)):
            q = 2 * cx + cy
            _remote(a_ref.at[q], o_ref.at[q], send.at[k], recv.at[k], (cx, cy, c)).wait_recv()
        for cp in cps:
            cp.wait_send()

    return pl.pallas_call(
        body, name=name, in_specs=[ANY], out_specs=ANY,
        out_shape=jax.ShapeDtypeStruct((N_CHIPS, rh, cols), a.dtype),
        scratch_shapes=[pltpu.SemaphoreType.DMA((3,)), pltpu.SemaphoreType.DMA((3,))],
    )(a)


def _rs_sum(rb, a, chip, name):
    _, rh, cols = rb.shape
    tr = _row_tile(rh, 512, 16)

    def body(p_ref, r0, r1, r2, r3, own, o_ref):
        p = p_ref[0]
        ownv = own[...].astype(F32)
        acc = None
        for q, r in enumerate((r0, r1, r2, r3)):
            v = jnp.where(p == q, ownv, r[...].astype(F32))
            acc = v if acc is None else acc + v
        o_ref[...] = acc

    def spec(q):
        return pl.BlockSpec((None, tr, cols), lambda i, pr: (jnp.where(pr[0] == q, (q + 1) % N_CHIPS, q), i, 0))

    grid_spec = pltpu.PrefetchScalarGridSpec(
        num_scalar_prefetch=1, grid=(rh // tr,),
        in_specs=[spec(0), spec(1), spec(2), spec(3), pl.BlockSpec((None, tr, cols), lambda i, pr: (pr[0], i, 0))],
        out_specs=pl.BlockSpec((tr, cols), lambda i, pr: (i, 0)),
    )
    return pl.pallas_call(
        body, name=name, grid_spec=grid_spec,
        out_shape=jax.ShapeDtypeStruct((rh, cols), F32),
        compiler_params=_cparams(("parallel",)),
    )(chip, rb, rb, rb, rb, a)


def _rs_share(hsum, name):
    rh, cols = hsum.shape

    def body(h_ref, o_ref, send, recv):
        x, y, c = _me()
        cp = _remote(h_ref, o_ref, send, recv, (x, y, 1 - c))
        cp.start()
        cp.wait()

    return pl.pallas_call(
        body, name=name, in_specs=[ANY], out_specs=ANY,
        out_shape=jax.ShapeDtypeStruct((rh, cols), F32),
        scratch_shapes=[pltpu.SemaphoreType.DMA, pltpu.SemaphoreType.DMA],
    )(hsum)


def _adamw(w, g, m, v, name):
    rows, cols = w.shape
    tr = _row_tile(rows, 512)

    def body(w_ref, g_ref, m_ref, v_ref, d_ref, nm_ref, nv_ref):
        gv = g_ref[...]
        nm = ADAM_B1 * m_ref[...] + (1.0 - ADAM_B1) * gv
        nv = ADAM_B2 * v_ref[...] + (1.0 - ADAM_B2) * (gv * gv)
        m_hat = nm / (1.0 - ADAM_B1 ** ADAM_STEP)
        v_hat = nv / (1.0 - ADAM_B2 ** ADAM_STEP)
        d_ref[...] = -ADAM_LR * (m_hat / (jnp.sqrt(v_hat) + ADAM_EPS) + ADAM_WD * w_ref[...])
        nm_ref[...] = nm
        nv_ref[...] = nv

    spec = pl.BlockSpec((tr, cols), lambda i: (i, 0))
    shp = jax.ShapeDtypeStruct((rows, cols), F32)
    return pl.pallas_call(
        body, name=name, grid=(rows // tr,), in_specs=[spec] * 4, out_specs=[spec] * 3, out_shape=[shp] * 3,
        compiler_params=_cparams(("parallel",)),
    )(w, g, m, v)


WEIGHTS = ("mix_norm_pre", "mix_norm_post", "ffn_norm_pre", "ffn_norm_post", "ev_w_in", "ev_a_conv_w", "ev_a_conv_b",
           "ev_a_ln_g", "ev_a_ln_b", "ev_sinks", "ev_w_out", "od_w_in", "od_conv_w", "od_w_out", "ffn_w_up",
           "ffn_conv_w", "ffn_w_down")
MATS = (("ev_w_in", 2), ("ev_w_out", 1), ("od_w_in", 2), ("od_w_out", 1), ("ffn_w_up", 2), ("ffn_w_down", 1))
UNITS = (("ev_w_in", 0, 2), ("ev_w_out", 0, 1), ("ffn_w_up", 0, 2), ("ffn_w_down", 0, 1),
         ("od_w_in", 0, 2), ("od_w_out", 0, 1), ("ffn_w_up", 1, 2), ("ffn_w_down", 1, 1))
GATHER_GROUPS = ((0, 1), (2, 3), (4, 5, 6, 7))
REDUCE_GROUPS = {"layer1": (4, 5, 6, 7), "ffn0": (2, 3), "ev": (0, 1)}
SMALL_SHARDED = ("ev_a_conv_w", "od_conv_w", "ffn_conv_w")
REPLICATED = ("mix_norm_pre", "mix_norm_post", "ffn_norm_pre", "ffn_norm_post", "ev_a_conv_b", "ev_a_ln_g",
              "ev_a_ln_b", "ev_sinks")


def _pack(parts, rows_multiple):
    flat = jnp.concatenate([p.reshape(-1) for p in parts])
    unit = rows_multiple * PACK_COLS
    pad = (-flat.shape[0]) % unit
    if pad:
        flat = jnp.concatenate([flat, jnp.zeros((pad,), flat.dtype)])
    return flat.reshape(-1, PACK_COLS)


def _unpack(buf, shapes):
    flat = buf.reshape(-1)
    out, off = [], 0
    for shp in shapes:
        n = 1
        for d in shp:
            n *= d
        out.append(flat[off:off + n].reshape(shp))
        off += n
    return out


def _shard_rows(shard, axis):
    if axis == 2:
        shard = jnp.swapaxes(shard, 1, 2)
    return shard.reshape(-1, PACK_COLS)


def kernel(x, positions, mix_norm_pre, mix_norm_post, ffn_norm_pre, ffn_norm_post, ev_w_in, ev_a_conv_w, ev_a_conv_b, ev_a_ln_g, ev_a_ln_b, ev_sinks, ev_w_out, od_w_in, od_conv_w, od_w_out, ffn_w_up, ffn_conv_w, ffn_w_down, loss_target, m_mix_norm_pre, m_mix_norm_post, m_ffn_norm_pre, m_ffn_norm_post, m_ev_w_in, m_ev_a_conv_w, m_ev_a_conv_b, m_ev_a_ln_g, m_ev_a_ln_b, m_ev_sinks, m_ev_w_out, m_od_w_in, m_od_conv_w, m_od_w_out, m_ffn_w_up, m_ffn_conv_w, m_ffn_w_down, v_mix_norm_pre, v_mix_norm_post, v_ffn_norm_pre, v_ffn_norm_post, v_ev_w_in, v_ev_a_conv_w, v_ev_a_conv_b, v_ev_a_ln_g, v_ev_a_ln_b, v_ev_sinks, v_ev_w_out, v_od_w_in, v_od_conv_w, v_od_w_out, v_ffn_w_up, v_ffn_conv_w, v_ffn_w_down):
    wts = dict(zip(WEIGHTS, (mix_norm_pre, mix_norm_post, ffn_norm_pre, ffn_norm_post, ev_w_in, ev_a_conv_w, ev_a_conv_b,
                             ev_a_ln_g, ev_a_ln_b, ev_sinks, ev_w_out, od_w_in, od_conv_w, od_w_out, ffn_w_up, ffn_conv_w,
                             ffn_w_down)))
    mom = dict(zip(WEIGHTS, (m_mix_norm_pre, m_mix_norm_post, m_ffn_norm_pre, m_ffn_norm_post, m_ev_w_in, m_ev_a_conv_w,
                             m_ev_a_conv_b, m_ev_a_ln_g, m_ev_a_ln_b, m_ev_sinks, m_ev_w_out, m_od_w_in, m_od_conv_w,
                             m_od_w_out, m_ffn_w_up, m_ffn_conv_w, m_ffn_w_down)))
    var = dict(zip(WEIGHTS, (v_mix_norm_pre, v_mix_norm_post, v_ffn_norm_pre, v_ffn_norm_post, v_ev_w_in, v_ev_a_conv_w,
                             v_ev_a_conv_b, v_ev_a_ln_g, v_ev_a_ln_b, v_ev_sinks, v_ev_w_out, v_od_w_in, v_od_conv_w,
                             v_od_w_out, v_ffn_w_up, v_ffn_conv_w, v_ffn_w_down)))
    xi, yi, ci = _me()
    chip = 2 * xi + yi

    unit_rows = [_shard_rows(wts[k][l:l + 1].astype(BF16), axis) for k, l, axis in UNITS]

    def group_block(group):
        return jnp.concatenate([unit_rows[u] for u in group], axis=0)

    def unpack_group(group, landed, own):
        full = lax.dynamic_update_slice(landed, own[None], (chip, 0, 0))
        out, off = {}, 0
        for u in group:
            k, l, axis = UNITS[u]
            n = unit_rows[u].shape[0]
            native = full[:, off:off + n].reshape(N_CHIPS * n, PACK_COLS)
            off += n
            key = (lambda name: (name, l)) if k.startswith("ffn") else (lambda name: name)
            out[key(k + "_t" if axis == 2 else k)] = native
            out[key(k if axis == 2 else k + "_t")] = native.T
        return out

    small_shapes = [wts[k].shape for k in SMALL_SHARDED]
    small_all = _exchange8(_pack([wts[k] for k in SMALL_SHARDED], 8), False, "gather_small")
    blocks = [group_block(grp) for grp in GATHER_GROUPS]
    first = _gather_chips(blocks[0], "gather_mats")
    later = {}
    for stage, grp, blk in zip(("ffn0", "layer1"), GATHER_GROUPS[1:], blocks[1:]):
        later[stage] = (grp, blk, _copies_start(blk, (N_CHIPS,) + blk.shape, _gather_plan, 3, "gather_" + stage + "_start"))

    def fetch(stage, after):
        grp, blk, started = later[stage]
        own, landed = _copies_wait(started, after, _gather_plan, "gather_" + stage + "_wait")
        return unpack_group(grp, landed, own)

    w = {k: wts[k] for k in REPLICATED}
    w.update(unpack_group(GATHER_GROUPS[0], first, blocks[0]))
    per_chip = [_unpack(small_all[2 * q], small_shapes) for q in range(N_CHIPS)]
    for n, k in enumerate(SMALL_SHARDED):
        w[k] = jnp.concatenate([per_chip[q][n] for q in range(N_CHIPS)], axis=-1)
    for k in ("ev_a_conv_w", "od_conv_w"):
        w[k] = w[k][0]
    w["ev_sinks"] = w["ev_sinks"][0]
    w["mix_norm_pre"] = w["mix_norm_pre"] + sum(later[s][2][4][0, 0] for s in later)

    core = jnp.reshape(ci, (1,)).astype(jnp.int32)
    chip_arr = jnp.reshape(chip, (1,)).astype(jnp.int32)
    per_layer = {}

    def group_grads(group, g):
        gp = jnp.concatenate([(g[k, l] if k.startswith("ffn") else g[k]).reshape(N_CHIPS, -1, PACK_COLS)
                              for k, l, _ in (UNITS[u] for u in group)], axis=1)
        return gp.reshape(N_CHIPS, 2, gp.shape[1] // 2, PACK_COLS)

    def finish(group, half, other):
        red = jnp.concatenate([jnp.where(ci == 0, half, other), jnp.where(ci == 0, other, half)], axis=0)
        off = 0
        for u in group:
            k, l, axis = UNITS[u]
            n = unit_rows[u].shape[0]
            part = red[off:off + n]
            off += n
            per_layer[k, l] = part.T if axis == 2 else part

    chains = {}

    def chain_step(tag, after, g):
        group = REDUCE_GROUPS[tag]
        st = chains.setdefault(tag, {"step": 0})
        step = st["step"]
        st["step"] = step + 1
        if step == 0:
            gp = group_grads(group, g)
            rh = gp.shape[2]
            st["swap"] = _copies_start(gp, (N_CHIPS, rh, PACK_COLS), _swap_plan, N_CHIPS, f"rs_{tag}_swap_start")
            return st["swap"][4][0, 0]
        if step == 1:
            gp, sib = _copies_wait(st["swap"], after, _swap_plan, f"rs_{tag}_swap_wait")
            pair = _rs_add(gp, sib, core, f"rs_{tag}_add")
            st["ici"] = _copies_start(pair, pair.shape, _ici_plan, 3, f"rs_{tag}_ici_start")
            return st["ici"][4][0, 0]
        if step == 2:
            pair, landed = _copies_wait(st["ici"], after, _ici_plan, f"rs_{tag}_ici_wait")
            half = _rs_sum(landed, pair, chip_arr, f"rs_{tag}_sum")
            st["share"] = _copies_start(half, half.shape, _share_plan, 1, f"rs_{tag}_share_start")
            return st["share"][4][0, 0]
        half, other = _copies_wait(st["share"], after, _share_plan, f"rs_{tag}_share_wait")
        finish(group, half, other)
        return 0.0

    schedule = {"layer1_grads": ("layer1",), "ffn0_bwd_done": ("layer1",), "ffn0_grads": ("ffn0",),
                "ev_out_bwd_done": ("layer1", "ffn0"), "ev_att_bwd_done": ("layer1", "ffn0"),
                "ev_mid_bwd_done": ("ffn0",)}

    def emit(place, after, g):
        return sum(chain_step(tag, after, g) for tag in schedule.get(place, ()))

    sq, dx, g = _local_step(x[0], positions[0], loss_target[0], w, fetch, emit)
    loss = lax.psum(0.5 * jnp.sum(sq) / D_MODEL, ("x", "y", "c"))

    gp = group_grads(REDUCE_GROUPS["ev"], g)
    sib = _rs_swap(gp, "rs_swap")
    pair = _rs_add(gp, sib, core, "rs_add")
    landed = _rs_ici(pair, "rs_ici")
    half = _rs_sum(landed, pair, chip_arr, "rs_sum")
    finish(REDUCE_GROUPS["ev"], half, _rs_share(half, "rs_share"))
    grads = {k: jnp.stack([per_layer[k, l] for l in range(wts[k].shape[0])], axis=0) for k, _ in MATS}

    small_keys = REPLICATED + SMALL_SHARDED
    full_shapes = [wts[k].shape for k in REPLICATED] + [wts[k].shape[:-1] + (wts[k].shape[-1] * N_CHIPS,) for k in SMALL_SHARDED]
    sm = _exchange8(_pack([g[k] for k in small_keys], 8), True, "reduce_small")
    for k, full in zip(small_keys, _unpack(sm, full_shapes)):
        if k in SMALL_SHARDED:
            n = wts[k].shape[-1]
            full = lax.dynamic_slice_in_dim(full, chip * n, n, axis=full.ndim - 1)
        grads[k] = full

    deltas, new_m, new_v = {}, {}, {}
    for k in WEIGHTS:
        shp = wts[k].shape
        two_d = (-1, shp[-1])
        d, nm, nv = _adamw(wts[k].reshape(two_d), grads[k].reshape(two_d), mom[k].reshape(two_d), var[k].reshape(two_d),
                           "adamw_" + k)
        deltas[k], new_m[k], new_v[k] = d.reshape(shp), nm.reshape(shp), nv.reshape(shp)

    return (loss, dx[None], *[grads[k] for k in WEIGHTS], *[deltas[k] for k in WEIGHTS],
            *[new_m[k] for k in WEIGHTS], *[new_v[k] for k in WEIGHTS])
```

```python
import functools

import jax
import jax.numpy as jnp
import numpy as np
from jax import lax
from jax.experimental import pallas as pl
from jax.experimental.pallas import tpu as pltpu

F32 = jnp.float32
BF16 = jnp.bfloat16
MESH = pl.DeviceIdType.MESH

D_MODEL = 1024
HEAD_DIM = 64
A_CH = 512
A_CONV = 31
N_Q_HEADS = 8
WINDOW = 128
ROPE_THETA = 500000.0
ROPE_DIM = 16
D_FF = 2816
RMS_EPS = 1e-6
LN_EPS = 1e-5
ADAM_LR = 0.001
ADAM_B1 = 0.9
ADAM_B2 = 0.999
ADAM_EPS = 1e-08
ADAM_WD = 0.01
ADAM_STEP = 10

LANES = 128
HALO16 = 16
HALO32 = 32
VMEM_LIMIT = 56 * 1024 * 1024
FFN_BWD_VMEM = 60 * 1024 * 1024
N_CHIPS = 4


def _cparams(sem):
    return pltpu.CompilerParams(dimension_semantics=sem, vmem_limit_bytes=VMEM_LIMIT)


def _tile(n, pref):
    if n <= pref:
        return n
    t = (pref // LANES) * LANES
    while t >= LANES:
        if n % t == 0:
            return t
        t -= LANES
    return n


MM_ROWS = 512


def _rms_scale(v):
    return lax.rsqrt(jnp.mean(v * v, axis=-1, keepdims=True) + RMS_EPS)


def _rms_bwd(dy, v, g):
    r = _rms_scale(v)
    nrm = v * r
    dn = dy * g
    return r * (dn - nrm * jnp.mean(dn * nrm, axis=-1, keepdims=True)), jnp.sum(dy * nrm, axis=0, keepdims=True)


def _mm_post(a, w, g, xres, name):
    s, k = a.shape
    d = w.shape[1]
    tm = min(MM_ROWS, s)

    def body(a_ref, w_ref, g_ref, x_ref, m_ref, o_ref):
        mv = jnp.dot(a_ref[...], w_ref[...], preferred_element_type=F32)
        m_ref[...] = mv
        o_ref[...] = x_ref[...] + mv * _rms_scale(mv) * g_ref[...]

    row = pl.BlockSpec((tm, d), lambda i: (i, 0))
    return pl.pallas_call(
        body, name=name, grid=(s // tm,),
        in_specs=[pl.BlockSpec((tm, k), lambda i: (i, 0)), _full((k, d)), _full((1, d)), row],
        out_specs=[row, row],
        out_shape=[jax.ShapeDtypeStruct((s, d), F32), jax.ShapeDtypeStruct((s, d), F32)],
        compiler_params=_cparams(("parallel",)),
    )(a, w, g, xres)


def _mm_post_bwd(dy, m, g, w_t, name):
    s, d = m.shape
    k = w_t.shape[1]
    tm = min(MM_ROWS, s)

    def body(dy_ref, m_ref, g_ref, wt_ref, dm_ref, da_ref, dg_ref):
        @pl.when(pl.program_id(0) == 0)
        def _():
            dg_ref[...] = jnp.zeros_like(dg_ref)

        dm, dg = _rms_bwd(dy_ref[...], m_ref[...], g_ref[...])
        dg_ref[...] += dg
        dmb = dm.astype(BF16)
        dm_ref[...] = dmb
        da_ref[...] = jnp.dot(dmb, wt_ref[...], preferred_element_type=F32)

    row = pl.BlockSpec((tm, d), lambda i: (i, 0))
    return pl.pallas_call(
        body, name=name, grid=(s // tm,),
        in_specs=[row, row, _full((1, d)), _full((d, k))],
        out_specs=[row, pl.BlockSpec((tm, k), lambda i: (i, 0)), _full((1, d))],
        out_shape=[jax.ShapeDtypeStruct((s, d), BF16), jax.ShapeDtypeStruct((s, k), F32),
                   jax.ShapeDtypeStruct((1, d), F32)],
        compiler_params=_cparams(("arbitrary",)),
    )(dy, m, g, w_t)


def _mm_tn(a, b, name):
    s, k = a.shape
    _, n = b.shape
    tk = _tile(k, 1408)
    tn = _tile(n, 1408)
    ts = min(2048, s)

    def body(a_ref, b_ref, o_ref):
        @pl.when(pl.program_id(2) == 0)
        def _():
            o_ref[...] = jnp.zeros_like(o_ref)

        o_ref[...] += lax.dot_general(a_ref[...], b_ref[...], (((0,), (0,)), ((), ())),
                                      preferred_element_type=F32)

    return pl.pallas_call(
        body, name=name, grid=(k // tk, n // tn, s // ts),
        in_specs=[pl.BlockSpec((ts, tk), lambda i, j, l: (l, i)), pl.BlockSpec((ts, tn), lambda i, j, l: (l, j))],
        out_specs=pl.BlockSpec((tk, tn), lambda i, j, l: (i, j)),
        out_shape=jax.ShapeDtypeStruct((k, n), F32),
        compiler_params=_cparams(("parallel", "parallel", "arbitrary")),
    )(a, b)


def _cur(tr, w, col=0):
    return pl.BlockSpec((tr, w), lambda i: (i, col))


def _prev(tr, h, w, col=0):
    return pl.BlockSpec((h, w), lambda i: (jnp.maximum(i * (tr // h) - 1, 0), col))


def _next(tr, h, w, nrows, col=0):
    last = nrows // h - 1
    return pl.BlockSpec((h, w), lambda i: (jnp.minimum((i + 1) * (tr // h), last), col))


def _full(shape):
    return pl.BlockSpec(shape, lambda i: tuple(0 for _ in shape))


def _silu_parts(g):
    sig = jax.nn.sigmoid(g)
    return sig, g * sig


FFN_CW = 256
FFN_NBUF = 3


def _conv3_taps(buf, w, off, rows):
    return (w[0:1] * buf[pl.ds(off, rows), :] + w[1:2] * buf[pl.ds(off + 1, rows), :]
            + w[2:3] * buf[pl.ds(off + 2, rows), :])


WHOLE_VMEM = pl.BlockSpec(memory_space=pltpu.VMEM)


def _ffn_fwd(x, g_pre, wu, conv_w, wd, g_post, name, target=None):
    s, d = x.shape
    f2 = wu.shape[1]
    f = f2 // 2
    tr = min(256, s)
    h = HALO16
    cw = FFN_CW
    head = target is not None

    def body(*refs):
        if head:
            (x_ref, gpre_ref, wu_ref, cw_ref, wd_ref, gpost_ref, t_ref, xo_ref, sq_ref, f_ref, h_ref, up_ref, u_ref,
             carry, gbuf, vbuf, facc) = refs
        else:
            (x_ref, gpre_ref, wu_ref, cw_ref, wd_ref, gpost_ref, xo_ref, f_ref, h_ref, up_ref, u_ref,
             carry, gbuf, vbuf, facc) = refs

        @pl.when(pl.program_id(0) == 0)
        def _():
            carry[...] = jnp.zeros_like(carry)
            if head:
                sq_ref[...] = jnp.zeros_like(sq_ref)

        xv = x_ref[...]
        r = lax.rsqrt(jnp.mean(xv * xv, axis=-1, keepdims=True) + RMS_EPS)
        hv = (xv * r * gpre_ref[...]).astype(BF16)
        h_ref[...] = hv
        nchunk = f // cw

        def up_proj(j):
            for buf, base in ((gbuf, 0), (vbuf, f)):
                cs = slice(base + j * cw, base + (j + 1) * cw)
                dst = buf.at[j % FFN_NBUF]
                upc = jnp.dot(hv, wu_ref[:, cs], preferred_element_type=F32)
                up_ref[:, cs] = upc.astype(BF16)
                dst[0:h, :] = carry[:, cs]
                dst[h:h + tr, :] = upc
                carry[:, cs] = upc[tr - h:tr, :]

        def down_proj(j, act):
            part = jnp.dot(act, wd_ref[j * cw:(j + 1) * cw, :], preferred_element_type=F32)
            if j == 0:
                facc[...] = part
            else:
                facc[...] += part

        for j in range(FFN_NBUF - 1):
            up_proj(j)
        pending = None
        for j in range(nchunk):
            cg = slice(j * cw, (j + 1) * cw)
            cv = slice(f + j * cw, f + (j + 1) * cw)
            if j + FFN_NBUF - 1 < nchunk:
                up_proj(j + FFN_NBUF - 1)
            if pending is not None:
                down_proj(*pending)
            g = _conv3_taps(gbuf.at[j % FFN_NBUF], cw_ref[:, cg], h - 2, tr)
            v = _conv3_taps(vbuf.at[j % FFN_NBUF], cw_ref[:, cv], h - 2, tr)
            u_ref[:, cg] = g.astype(BF16)
            u_ref[:, cv] = v.astype(BF16)
            act = (g * jax.nn.sigmoid(g) * v).astype(BF16)
            pending = (j, act)
        down_proj(*pending)
        fv = facc[...]
        f_ref[...] = fv
        r2 = lax.rsqrt(jnp.mean(fv * fv, axis=-1, keepdims=True) + RMS_EPS)
        xo = xv + fv * r2 * gpost_ref[...]
        if head:
            err = xo - t_ref[...]
            xo_ref[...] = err * (1.0 / d)
            sq_ref[...] += jnp.sum(err * err, axis=0, keepdims=True)
        else:
            xo_ref[...] = xo

    row = _cur(tr, d)
    wide = _cur(tr, f2)
    vec = _full((1, d))
    out_specs = [row] + ([vec] if head else []) + [row, row, wide, wide]
    out_shape = ([jax.ShapeDtypeStruct((s, d), F32)] + ([jax.ShapeDtypeStruct((1, d), F32)] if head else [])
                 + [jax.ShapeDtypeStruct((s, d), F32), jax.ShapeDtypeStruct((s, d), BF16),
                    jax.ShapeDtypeStruct((s, f2), BF16), jax.ShapeDtypeStruct((s, f2), BF16)])
    return pl.pallas_call(
        body, name=name, grid=(s // tr,),
        in_specs=[row, vec, WHOLE_VMEM, _full((3, f2)), WHOLE_VMEM, vec] + ([row] if head else []),
        out_specs=out_specs, out_shape=out_shape,
        scratch_shapes=[pltpu.VMEM((h, f2), F32), pltpu.VMEM((FFN_NBUF, h + tr, cw), F32),
                        pltpu.VMEM((FFN_NBUF, h + tr, cw), F32), pltpu.VMEM((tr, d), F32)],
        compiler_params=_cparams(("arbitrary",)),
    )(*((x, g_pre, wu, conv_w, wd, g_post) + ((target,) if head else ())))


def _ffn_bwd(dxo, fout, x, up, u, g_pre, g_post, wd_t, wu_t, conv_w, name):
    s, d = x.shape
    f2 = up.shape[1]
    f = f2 // 2
    tr = min(256, s)
    nt = s // tr
    h = HALO16
    cw = FFN_CW

    def body(dy_ref, f_ref, x_ref, up_ref, u_ref, gpre_ref, gpost_ref, wdt_ref, wut_ref, cw_ref,
             dx_ref, dup_ref, act_ref, df_ref, dcw_ref, dgpost_ref, dgpre_ref, carry, dgbuf, dvbuf, dhacc):
        @pl.when(pl.program_id(0) == 0)
        def _():
            carry[...] = jnp.zeros_like(carry)
            dcw_ref[...] = jnp.zeros_like(dcw_ref)
            dgpost_ref[...] = jnp.zeros_like(dgpost_ref)
            dgpre_ref[...] = jnp.zeros_like(dgpre_ref)

        dy = dy_ref[...]
        fv = f_ref[...]
        r = lax.rsqrt(jnp.mean(fv * fv, axis=-1, keepdims=True) + RMS_EPS)
        nrm = fv * r
        dn = dy * gpost_ref[...]
        dfv = (r * (dn - nrm * jnp.mean(dn * nrm, axis=-1, keepdims=True))).astype(BF16)
        dgpost_ref[...] += jnp.sum(dy * nrm, axis=0, keepdims=True)
        df_ref[...] = dfv
        nchunk = f // cw

        def dh_part(dupb, cs, first):
            part = jnp.dot(dupb, wut_ref[cs, :], preferred_element_type=F32)
            if first:
                dhacc[...] = part
            else:
                dhacc[...] += part

        def dact_of(j):
            return jnp.dot(dfv, wdt_ref[:, j * cw:(j + 1) * cw], preferred_element_type=F32)

        ahead = [dact_of(0)]
        for j in range(nchunk):
            ch = slice(j * cw, (j + 1) * cw)
            cg = ch
            cv = slice(f + j * cw, f + (j + 1) * cw)
            dact = ahead.pop(0)
            if j + 1 < nchunk:
                ahead.append(dact_of(j + 1))
            g = u_ref[:, cg].astype(F32)
            v = u_ref[:, cv].astype(F32)
            sig, sil = _silu_parts(g)
            act_ref[:, ch] = (sil * v).astype(BF16)
            du_g = dact * v * (sig * (1.0 + g * (1.0 - sig)))
            du_v = dact * sil
            for k, (dbuf, du, cs) in enumerate(((dgbuf.at[j % FFN_NBUF], du_g, cg), (dvbuf.at[j % FFN_NBUF], du_v, cv))):
                dbuf[0:tr, :] = du
                dbuf[tr:tr + h, :] = carry[:, cs]
                carry[:, cs] = du[0:h, :]
                w = cw_ref[:, cs]
                xin = up_ref[:, cs].astype(F32)
                acc = None
                for sh in range(3):
                    dsh = dbuf[pl.ds(sh, tr), :]
                    term = w[2 - sh:3 - sh] * dsh
                    acc = term if acc is None else acc + term
                    dcw_ref[2 - sh:3 - sh, cs] += jnp.sum(xin * dsh, axis=0, keepdims=True)
                dupb = acc.astype(BF16)
                dup_ref[:, cs] = dupb
                dh_part(dupb, cs, j == 0 and k == 0)
        dh = dhacc[...]
        xv = x_ref[...]
        r1 = lax.rsqrt(jnp.mean(xv * xv, axis=-1, keepdims=True) + RMS_EPS)
        n1 = xv * r1
        dn1 = dh * gpre_ref[...]
        dx_ref[...] = dy + r1 * (dn1 - n1 * jnp.mean(dn1 * n1, axis=-1, keepdims=True))
        dgpre_ref[...] += jnp.sum(dh * n1, axis=0, keepdims=True)

    def rev(w):
        return pl.BlockSpec((tr, w), lambda i: (nt - 1 - i, 0))

    vec = _full((1, d))
    return pl.pallas_call(
        body, name=name, grid=(nt,),
        in_specs=[rev(d), rev(d), rev(d), rev(f2), rev(f2), vec, vec, WHOLE_VMEM, WHOLE_VMEM, _full((3, f2))],
        out_specs=[rev(d), rev(f2), rev(f), rev(d), _full((3, f2)), vec, vec],
        out_shape=[jax.ShapeDtypeStruct((s, d), F32), jax.ShapeDtypeStruct((s, f2), BF16),
                   jax.ShapeDtypeStruct((s, f), BF16), jax.ShapeDtypeStruct((s, d), BF16),
                   jax.ShapeDtypeStruct((3, f2), F32), jax.ShapeDtypeStruct((1, d), F32),
                   jax.ShapeDtypeStruct((1, d), F32)],
        scratch_shapes=[pltpu.VMEM((h, f2), F32), pltpu.VMEM((FFN_NBUF, tr + h, cw), F32),
                        pltpu.VMEM((FFN_NBUF, tr + h, cw), F32), pltpu.VMEM((tr, d), F32)],
        compiler_params=pltpu.CompilerParams(dimension_semantics=("arbitrary",), vmem_limit_bytes=FFN_BWD_VMEM),
    )(dxo, fout, x, up, u, g_pre, g_post, wd_t, wu_t, conv_w)


def _od_fwd(x, g, w, conv_w, w_out, g_post, name):
    s, d = x.shape
    d3 = w.shape[1]
    tr = min(512, s)
    h = HALO16
    cw = FFN_CW
    nchunk = d // cw

    def body(x_ref, g_ref, w_ref, cw_ref, wout_ref, gpost_ref, h_ref, z_ref, y_ref, m_ref, xo_ref, carry, buf, macc):
        @pl.when(pl.program_id(0) == 0)
        def _():
            carry[...] = jnp.zeros_like(carry)

        xv = x_ref[...]
        hv = (xv * _rms_scale(xv) * g_ref[...]).astype(BF16)
        h_ref[...] = hv

        def project(j):
            out = []
            for part in range(3):
                cs = slice(part * d + j * cw, part * d + (j + 1) * cw)
                zc = jnp.dot(hv, w_ref[:, cs], preferred_element_type=F32).astype(BF16)
                z_ref[:, cs] = zc
                out.append(zc.astype(F32))
            return out

        ahead = [project(0), project(1)]
        for j in range(nchunk):
            cb = slice(j * cw, (j + 1) * cw)
            bval, cval, uval = ahead.pop(0)
            if j + 2 < nchunk:
                ahead.append(project(j + 2))
            bf = buf.at[j % FFN_NBUF]
            cu = cval * uval
            bf[0:h, :] = carry[:, cb]
            bf[h:h + tr, :] = cu
            carry[:, cb] = cu[tr - h:tr, :]
            yv = (bval * _conv3_taps(bf, cw_ref[:, cb], h - 2, tr)).astype(BF16)
            y_ref[:, cb] = yv
            part = jnp.dot(yv, wout_ref[cb, :], preferred_element_type=F32)
            if j == 0:
                macc[...] = part
            else:
                macc[...] += part
        mv = macc[...]
        m_ref[...] = mv
        xo_ref[...] = xv + mv * _rms_scale(mv) * gpost_ref[...]

    row = _cur(tr, d)
    vec = _full((1, d))
    return pl.pallas_call(
        body, name=name, grid=(s // tr,),
        in_specs=[row, vec, WHOLE_VMEM, _full((3, d)), WHOLE_VMEM, vec],
        out_specs=[row, _cur(tr, d3), row, row, row],
        out_shape=[jax.ShapeDtypeStruct((s, d), BF16), jax.ShapeDtypeStruct((s, d3), BF16),
                   jax.ShapeDtypeStruct((s, d), BF16), jax.ShapeDtypeStruct((s, d), F32),
                   jax.ShapeDtypeStruct((s, d), F32)],
        scratch_shapes=[pltpu.VMEM((h, d), F32), pltpu.VMEM((FFN_NBUF, h + tr, cw), F32), pltpu.VMEM((tr, d), F32)],
        compiler_params=_cparams(("arbitrary",)),
    )(x, g, w, conv_w, w_out, g_post)


def _od_bwd(dxo, m, g_post, w_out_t, z, conv_w, w_t, x, g, name):
    s, d3 = z.shape
    d = d3 // 3
    tr = min(512, s)
    nt = s // tr
    h = HALO16
    cw = FFN_CW
    ext = tr + h
    nchunk = d // cw

    def body(dxo_ref, m_ref, gpost_ref, wot_ref, z_ref, zp_ref, w_ref, wt_ref, x_ref, g_ref,
             dm_ref, o_ref, dw_ref, dx_ref, dgpost_ref, dg_ref, carry, buf, dbuf, dhacc):
        i = pl.program_id(0)
        row0 = i == nt - 1

        @pl.when(i == 0)
        def _():
            carry[...] = jnp.zeros_like(carry)
            dw_ref[...] = jnp.zeros_like(dw_ref)
            dg_ref[...] = jnp.zeros_like(dg_ref)
            dgpost_ref[...] = jnp.zeros_like(dgpost_ref)

        dyo = dxo_ref[...]
        dm, dgp = _rms_bwd(dyo, m_ref[...], gpost_ref[...])
        dgpost_ref[...] += dgp
        dmb = dm.astype(BF16)
        dm_ref[...] = dmb

        def dy_of(j):
            return jnp.dot(dmb, wot_ref[:, j * cw:(j + 1) * cw], preferred_element_type=F32)

        ahead = [dy_of(0)]
        started = False
        for j in range(nchunk):
            cb = slice(j * cw, (j + 1) * cw)
            cc = slice(d + j * cw, d + (j + 1) * cw)
            cu = slice(2 * d + j * cw, 2 * d + (j + 1) * cw)
            dyv = ahead.pop(0)
            if j + 1 < nchunk:
                ahead.append(dy_of(j + 1))
            bf = buf.at[j % FFN_NBUF]
            db = dbuf.at[j % FFN_NBUF]
            w = w_ref[:, cb]
            cval = z_ref[:, cc].astype(F32)
            uval = z_ref[:, cu].astype(F32)
            bf[0:h, :] = jnp.where(row0, 0.0, zp_ref[:, cc].astype(F32) * zp_ref[:, cu].astype(F32))
            bf[h:h + tr, :] = cval * uval
            k = _conv3_taps(bf, w, h - 2, tr)
            dk = dyv * z_ref[:, cb].astype(F32)
            db[0:tr, :] = dk
            db[tr:ext, :] = carry[:, cb]
            carry[:, cb] = dk[0:h, :]
            dcu = w[2:3] * db[pl.ds(0, tr), :] + w[1:2] * db[pl.ds(1, tr), :] + w[0:1] * db[pl.ds(2, tr), :]
            for t in range(3):
                dw_ref[t:t + 1, cb] += jnp.sum(dk * bf[pl.ds(h - 2 + t, tr), :], axis=0, keepdims=True)
            for cs, val in ((cb, dyv * k), (cc, dcu * uval), (cu, dcu * cval)):
                piece = val.astype(BF16)
                o_ref[:, cs] = piece
                part = jnp.dot(piece, wt_ref[cs, :], preferred_element_type=F32)
                if started:
                    dhacc[...] += part
                else:
                    dhacc[...] = part
                    started = True
        dx, dg = _rms_bwd(dhacc[...], x_ref[...], g_ref[...])
        dg_ref[...] += dg
        dx_ref[...] = dyo + dx

    def rev(w):
        return pl.BlockSpec((tr, w), lambda i: (nt - 1 - i, 0))

    prev = pl.BlockSpec((h, d3), lambda i: (jnp.maximum((nt - 1 - i) * (tr // h) - 1, 0), 0))
    vec = _full((1, d))
    return pl.pallas_call(
        body, name=name, grid=(nt,),
        in_specs=[rev(d), rev(d), vec, WHOLE_VMEM, rev(d3), prev, _full((3, d)), WHOLE_VMEM, rev(d), vec],
        out_specs=[rev(d), rev(d3), _full((3, d)), rev(d), vec, vec],
        out_shape=[jax.ShapeDtypeStruct((s, d), BF16), jax.ShapeDtypeStruct((s, d3), BF16),
                   jax.ShapeDtypeStruct((3, d), F32), jax.ShapeDtypeStruct((s, d), F32),
                   jax.ShapeDtypeStruct((1, d), F32), jax.ShapeDtypeStruct((1, d), F32)],
        scratch_shapes=[pltpu.VMEM((h, d), F32), pltpu.VMEM((FFN_NBUF, h + tr, cw), F32),
                        pltpu.VMEM((FFN_NBUF, ext, cw), F32), pltpu.VMEM((tr, d), F32)],
        compiler_params=_cparams(("arbitrary",)),
    )(dxo, m, g_post, w_out_t, z, z, conv_w, w_t, x, g)


Q0 = 2 * A_CH
K0 = Q0 + N_Q_HEADS * HEAD_DIM
V0 = K0 + 2 * HEAD_DIM
EVEN_IN = V0 + 2 * HEAD_DIM


def _rope_tables(positions):
    half = ROPE_DIM // 2
    inv_freq = ROPE_THETA ** (-(jnp.arange(half, dtype=F32) * 2.0 / ROPE_DIM))
    ang = positions.astype(F32)[:, None] * inv_freq
    cs = jnp.concatenate([jnp.cos(ang), jnp.sin(ang)], axis=1)
    spread = np.zeros((2 * half, 3 * LANES), np.float32)
    const = np.zeros((1, 3 * LANES), np.float32)
    for lane in range(3 * LANES):
        dim, part = lane % HEAD_DIM, lane // LANES
        if part == 0:
            if dim < ROPE_DIM:
                spread[dim % half, lane] = 1.0
            else:
                const[0, lane] = 1.0
        elif part == 1 and half <= dim < ROPE_DIM:
            spread[half + dim - half, lane] = 1.0
        elif part == 2 and dim < half:
            spread[half + dim, lane] = -1.0
    return jnp.dot(cs, jnp.asarray(spread), precision=lax.Precision.HIGHEST) + jnp.asarray(const)


def _rope_fwd(x, tab):
    c, sa, sb = tab[:, 0:LANES], tab[:, LANES:2 * LANES], tab[:, 2 * LANES:3 * LANES]
    return x * c + pltpu.roll(x, 8, 1) * sa + pltpu.roll(x, LANES - 8, 1) * sb


def _rope_bwd(dy, tab):
    c, sa, sb = tab[:, 0:LANES], tab[:, LANES:2 * LANES], tab[:, 2 * LANES:3 * LANES]
    return dy * c + pltpu.roll(dy * sa, LANES - 8, 1) + pltpu.roll(dy * sb, 8, 1)


def _ln_fwd(c, g, b):
    mu = jnp.mean(c, axis=-1, keepdims=True)
    xc = c - mu
    r = lax.rsqrt(jnp.mean(xc * xc, axis=-1, keepdims=True) + LN_EPS)
    nrm = xc * r
    return nrm, r, nrm * g + b


def _phase_fill(buf, ph, rows):
    for k in range(1, 8):
        ph[k - 1, 0:rows - 8, :] = buf[pl.ds(k, rows - 8), :]


def _phase_rows(buf, ph, off, n, cs):
    k = off % 8
    src = buf if k == 0 else ph.at[k - 1]
    return src[pl.ds(off - k, n), cs]


def _ev_in_fwd(x, g_pre, w_in, tab, conv_w, conv_b, ln_g, ln_b, name):
    s, d = x.shape
    tr = min(512, s)
    h = HALO32
    cw = LANES
    pw = 2 * LANES

    def body(x_ref, gpre_ref, win_ref, tab_ref, w_ref, b_ref, g_ref, lb_ref, h_ref, z_ref, c_ref, a_ref, qkv_ref,
             gbuf, cbuf, gph, carry):
        @pl.when(pl.program_id(0) == 0)
        def _():
            carry[...] = jnp.zeros_like(carry)

        xv = x_ref[...]
        hv = (xv * _rms_scale(xv) * gpre_ref[...]).astype(BF16)
        h_ref[...] = hv

        def project(lo_col, hi_col):
            for c0 in range(lo_col, hi_col, pw):
                cs = slice(c0, c0 + pw)
                z_ref[:, cs] = jnp.dot(hv, win_ref[:, cs], preferred_element_type=F32).astype(BF16)

        project(0, 2 * A_CH)
        glu = z_ref[:, 0:A_CH].astype(F32) * jax.nn.sigmoid(z_ref[:, A_CH:2 * A_CH].astype(F32))
        project(2 * A_CH, EVEN_IN)
        gbuf[0:h, :] = carry[...]
        gbuf[h:h + tr, :] = glu
        carry[...] = glu[tr - h:tr, :]
        _phase_fill(gbuf, gph, h + tr)
        for j in range(A_CH // cw):
            cs = slice(j * cw, (j + 1) * cw)
            acc = jnp.broadcast_to(b_ref[:, cs], (tr, cw))
            for t in range(A_CONV):
                acc = acc + w_ref[t:t + 1, cs] * _phase_rows(gbuf, gph, h - (A_CONV - 1) + t, tr, cs)
            cbuf[:, cs] = acc
        c = cbuf[...]
        c_ref[...] = c.astype(BF16)
        _, _, l = _ln_fwd(c, g_ref[...], lb_ref[...])
        a_ref[...] = (l * jax.nn.sigmoid(l)).astype(BF16)
        tab_v = tab_ref[...]
        for p in range(4):
            xq = z_ref[:, Q0 + p * LANES:Q0 + (p + 1) * LANES].astype(F32)
            qkv_ref[:, p * LANES:(p + 1) * LANES] = _rope_fwd(xq, tab_v).astype(BF16)
        lane = lax.broadcasted_iota(jnp.int32, (tr, LANES), 1)
        lo = lane < HEAD_DIM
        kr = _rope_fwd(z_ref[:, K0:K0 + LANES].astype(F32), tab_v)
        vr = z_ref[:, V0:V0 + LANES].astype(F32)
        for base, val in ((4 * LANES, kr), (6 * LANES, vr)):
            sw = pltpu.roll(val, HEAD_DIM, 1)
            qkv_ref[:, base:base + LANES] = jnp.where(lo, val, sw).astype(BF16)
            qkv_ref[:, base + LANES:base + 2 * LANES] = jnp.where(lo, sw, val).astype(BF16)

    return pl.pallas_call(
        body, name=name, grid=(s // tr,),
        in_specs=[_cur(tr, d), _full((1, d)), WHOLE_VMEM, _cur(tr, 3 * LANES), _full((A_CONV, A_CH)),
                  _full((1, A_CH)), _full((1, A_CH)), _full((1, A_CH))],
        out_specs=[_cur(tr, d), _cur(tr, EVEN_IN), _cur(tr, A_CH), _cur(tr, A_CH), _cur(tr, 2 * A_CH)],
        out_shape=[jax.ShapeDtypeStruct((s, d), BF16), jax.ShapeDtypeStruct((s, EVEN_IN), BF16),
                   jax.ShapeDtypeStruct((s, A_CH), BF16), jax.ShapeDtypeStruct((s, A_CH), BF16),
                   jax.ShapeDtypeStruct((s, 2 * A_CH), BF16)],
        scratch_shapes=[pltpu.VMEM((h + tr, A_CH), F32), pltpu.VMEM((tr, A_CH), F32),
                        pltpu.VMEM((7, h + tr, A_CH), F32), pltpu.VMEM((h, A_CH), F32)],
        compiler_params=_cparams(("arbitrary",)),
    )(x, g_pre, w_in, tab, conv_w, conv_b, ln_g, ln_b)


def _ev_mid_bwd(dcat, c, z, dq, dkv, tab, conv_w, ln_g, ln_b, w_t, x, g_pre, res, name):
    s = z.shape[0]
    tr = min(512, s)
    h = HALO32
    cw = LANES
    ext = tr + h

    def body(da_ref, dan_ref, c_ref, cn_ref, z_ref, dq_ref, dkv_ref, tab_ref, w_ref, g_ref, lb_ref,
             wt_ref, x_ref, gpre_ref, res_ref, dz_ref, dw_ref, dvec_ref, dx_ref, dg_ref, dcbuf, dcph, dhacc):
        i = pl.program_id(0)
        first = i == 0
        last = i == pl.num_programs(0) - 1

        @pl.when(first)
        def _():
            dw_ref[...] = jnp.zeros_like(dw_ref)
            dvec_ref[...] = jnp.zeros_like(dvec_ref)
            dg_ref[...] = jnp.zeros_like(dg_ref)

        started = []

        def dh_part(cs):
            part = jnp.dot(dz_ref[:, cs], wt_ref[cs, :], preferred_element_type=F32)
            if started:
                dhacc[...] += part
            else:
                dhacc[...] = part
                started.append(True)

        tab_v = tab_ref[...]
        for p in range(4):
            cs = slice(p * LANES, (p + 1) * LANES)
            dz_ref[:, Q0 + p * LANES:Q0 + (p + 1) * LANES] = _rope_bwd(dq_ref[:, cs], tab_v).astype(BF16)
        lane = lax.broadcasted_iota(jnp.int32, (tr, LANES), 1)
        lo = lane < HEAD_DIM

        def fold(base):
            p0 = dkv_ref[:, base:base + LANES]
            p1 = dkv_ref[:, base + LANES:base + 2 * LANES]
            s0 = p0 + pltpu.roll(p0, HEAD_DIM, 1)
            s1 = p1 + pltpu.roll(p1, HEAD_DIM, 1)
            return jnp.where(lo, s0, s1)

        dz_ref[:, K0:K0 + LANES] = _rope_bwd(fold(0), tab_v).astype(BF16)
        dz_ref[:, V0:V0 + LANES] = fold(2 * LANES).astype(BF16)
        dh_part(slice(Q0, EVEN_IN))

        gv = g_ref[...]

        def ln_silu_bwd(cv, dav):
            nrm, r, l = _ln_fwd(cv, gv, lb_ref[...])
            sig = jax.nn.sigmoid(l)
            dl = dav * (sig * (1.0 + l * (1.0 - sig)))
            dn = dl * gv
            dc = r * (dn - jnp.mean(dn, axis=-1, keepdims=True) - nrm * jnp.mean(dn * nrm, axis=-1, keepdims=True))
            return dc, dl, nrm

        dc, dl, nrm = ln_silu_bwd(c_ref[...].astype(F32), da_ref[...])
        dcn, _, _ = ln_silu_bwd(cn_ref[...].astype(F32), dan_ref[...])
        dcbuf[0:tr, :] = dc
        dcbuf[tr:ext, :] = jnp.where(last, 0.0, dcn)
        dvec_ref[0:1, :] += jnp.sum(dc, axis=0, keepdims=True)
        dvec_ref[1:2, :] += jnp.sum(dl * nrm, axis=0, keepdims=True)
        dvec_ref[2:3, :] += jnp.sum(dl, axis=0, keepdims=True)

        _phase_fill(dcbuf, dcph, ext)
        a_lin = z_ref[:, 0:A_CH].astype(F32)
        sig_g = jax.nn.sigmoid(z_ref[:, A_CH:2 * A_CH].astype(F32))
        glu = a_lin * sig_g
        for j in range(A_CH // cw):
            cs = slice(j * cw, (j + 1) * cw)
            gluj = glu[:, cs]
            acc = jnp.zeros((tr, cw), F32)
            for t in range(A_CONV):
                dsh = _phase_rows(dcbuf, dcph, A_CONV - 1 - t, tr, cs)
                acc = acc + w_ref[t:t + 1, cs] * dsh
                dw_ref[t:t + 1, cs] += jnp.sum(gluj * dsh, axis=0, keepdims=True)
            dz_ref[:, cs] = (acc * sig_g[:, cs]).astype(BF16)
            dz_ref[:, A_CH + j * cw:A_CH + (j + 1) * cw] = (
                acc * a_lin[:, cs] * sig_g[:, cs] * (1.0 - sig_g[:, cs])).astype(BF16)
            if j % 2 == 1:
                dh_part(slice((j - 1) * cw, (j + 1) * cw))
                dh_part(slice(A_CH + (j - 1) * cw, A_CH + (j + 1) * cw))

        dx, dg = _rms_bwd(dhacc[...], x_ref[...], gpre_ref[...])
        dg_ref[...] += dg
        dx_ref[...] = res_ref[...] + dx

    row = _cur(tr, D_MODEL)
    vec = _full((1, D_MODEL))
    return pl.pallas_call(
        body, name=name, grid=(s // tr,),
        in_specs=[_cur(tr, A_CH), _next(tr, h, A_CH, s), _cur(tr, A_CH), _next(tr, h, A_CH, s),
                  _cur(tr, EVEN_IN), _cur(tr, A_CH), _cur(tr, A_CH), _cur(tr, 3 * LANES),
                  _full((A_CONV, A_CH)), _full((1, A_CH)), _full((1, A_CH)), WHOLE_VMEM, row, vec, row],
        out_specs=[_cur(tr, EVEN_IN), _full((A_CONV, A_CH)), _full((8, A_CH)), row, vec],
        out_shape=[jax.ShapeDtypeStruct((s, EVEN_IN), BF16), jax.ShapeDtypeStruct((A_CONV, A_CH), F32),
                   jax.ShapeDtypeStruct((8, A_CH), F32), jax.ShapeDtypeStruct((s, D_MODEL), F32),
                   jax.ShapeDtypeStruct((1, D_MODEL), F32)],
        scratch_shapes=[pltpu.VMEM((ext, A_CH), F32), pltpu.VMEM((7, ext, A_CH), F32),
                        pltpu.VMEM((tr, D_MODEL), F32)],
        compiler_params=_cparams(("arbitrary",)),
    )(dcat, dcat, c, c, z, dq, dkv, tab, conv_w, ln_g, ln_b, w_t, x, g_pre, res)


NT = (((1,), (1,)), ((), ()))
TN = (((0,), (0,)), ((), ()))
QB = WINDOW
SCALE = HEAD_DIM ** -0.5
ATT_AHEAD = 2


def _att_scores(q2m, kwin):
    return lax.dot_general(q2m, kwin, NT, preferred_element_type=F32)


def _att_probs(raw, sink, mask):
    sc = jnp.where(mask, raw * SCALE, -jnp.inf)
    mx = jnp.maximum(jnp.max(sc, axis=-1, keepdims=True), sink)
    p = jnp.exp(sc - mx)
    ps = jnp.exp(sink - mx)
    inv = 1.0 / (jnp.sum(p, axis=-1, keepdims=True) + ps)
    return p * inv, ps * inv


def _att_mask(i):
    r = lax.broadcasted_iota(jnp.int32, (QB, 2 * QB), 0)
    kc = lax.broadcasted_iota(jnp.int32, (QB, 2 * QB), 1)
    diff = r + QB - kc
    return (diff >= 0) & (diff < WINDOW) & ((kc >= QB) | (i > 0))


def _half_masks(dtype):
    lane = lax.broadcasted_iota(jnp.int32, (1, LANES), 1)
    return (lane < HEAD_DIM).astype(dtype), (lane >= HEAD_DIM).astype(dtype)


def _att_fwd(qkv, a, sinks, name):
    s = qkv.shape[0]
    nb = s // QB

    def body(sink_ref, qkv_ref, kvp_ref, a_ref, o_ref):
        i = pl.program_id(0)
        mask = _att_mask(i)
        mlo, mhi = _half_masks(BF16)
        o_ref[:, 0:A_CH] = a_ref[...]

        def window(col):
            return jnp.concatenate([kvp_ref[:, col * LANES:(col + 1) * LANES],
                                    qkv_ref[:, A_CH + col * LANES:A_CH + (col + 1) * LANES]], axis=0)

        def raw_scores(p):
            q2 = qkv_ref[:, p * LANES:(p + 1) * LANES]
            kwin = window(p // 2)
            return _att_scores(q2 * mlo, kwin), _att_scores(q2 * mhi, kwin)

        ahead = [raw_scores(p) for p in range(4)]
        for p in range(4):
            raw_e, raw_o = ahead[p]
            vwin = window(2 + p // 2)
            pe, _ = _att_probs(raw_e, sink_ref[2 * p], mask)
            po, _ = _att_probs(raw_o, sink_ref[2 * p + 1], mask)
            o = (jnp.dot(pe.astype(BF16), vwin * mlo, preferred_element_type=F32)
                 + jnp.dot(po.astype(BF16), vwin * mhi, preferred_element_type=F32))
            o_ref[:, A_CH + p * LANES:A_CH + (p + 1) * LANES] = o.astype(BF16)

    grid_spec = pltpu.PrefetchScalarGridSpec(
        num_scalar_prefetch=1, grid=(nb,),
        in_specs=[pl.BlockSpec((QB, 2 * A_CH), lambda i, sk: (i, 0)),
                  pl.BlockSpec((QB, A_CH), lambda i, sk: (jnp.maximum(i - 1, 0), 1)),
                  pl.BlockSpec((QB, A_CH), lambda i, sk: (i, 0))],
        out_specs=pl.BlockSpec((QB, 2 * A_CH), lambda i, sk: (i, 0)),
    )
    return pl.pallas_call(
        body, name=name, grid_spec=grid_spec,
        out_shape=jax.ShapeDtypeStruct((s, 2 * A_CH), BF16),
        compiler_params=_cparams(("parallel",)),
    )(sinks, qkv, qkv, a)


def _att_bwd(qkv, dcat, sinks, name):
    s = qkv.shape[0]
    nb = s // QB

    def body(sink_ref, qkv_ref, kvp_ref, do_ref, dq_ref, dkv_ref, ds_ref, carry):
        i = pl.program_id(0)

        @pl.when(i == 0)
        def _():
            ds_ref[...] = jnp.zeros_like(ds_ref)
            carry[...] = jnp.zeros_like(carry)

        @pl.when(i < nb)
        def _():
            mask = _att_mask(i)
            mlo, mhi = _half_masks(BF16)
            dwin = [jnp.zeros((2 * QB, LANES), F32) for _ in range(4)]

            def window(col):
                return jnp.concatenate([kvp_ref[:, col * LANES:(col + 1) * LANES],
                                        qkv_ref[:, A_CH + col * LANES:A_CH + (col + 1) * LANES]], axis=0)

            def first_products(n):
                p, hm = n // 2, (mlo, mhi)[n % 2]
                qm = qkv_ref[:, p * LANES:(p + 1) * LANES] * hm
                dom = do_ref[:, p * LANES:(p + 1) * LANES].astype(BF16) * hm
                kwin, vwin = window(p // 2), window(2 + p // 2)
                return (qm, dom, kwin * hm, _att_scores(qm, kwin),
                        lax.dot_general(dom, vwin, NT, preferred_element_type=F32))

            ahead = [first_products(n) for n in range(ATT_AHEAD)]
            dq2 = None
            for n in range(N_Q_HEADS):
                g = n // 4
                qm, dom, kwm, raw, dp = ahead.pop(0)
                if n + ATT_AHEAD < N_Q_HEADS:
                    ahead.append(first_products(n + ATT_AHEAD))
                prob, psink = _att_probs(raw, sink_ref[n], mask)
                delta = jnp.sum(prob * dp, axis=-1, keepdims=True)
                dsc = (prob * (dp - delta) * SCALE).astype(BF16)
                ds_ref[n:n + 1, :] += jnp.broadcast_to(jnp.sum(-psink * delta, axis=0, keepdims=True), (1, LANES))
                part = jnp.dot(dsc, kwm, preferred_element_type=F32)
                dq2 = part if n % 2 == 0 else dq2 + part
                dwin[g] = dwin[g] + lax.dot_general(dsc, qm, TN, preferred_element_type=F32)
                dwin[2 + g] = dwin[2 + g] + lax.dot_general(prob.astype(BF16), dom, TN, preferred_element_type=F32)
                if n % 2 == 1:
                    dq_ref[:, (n // 2) * LANES:(n // 2 + 1) * LANES] = dq2
            for n in range(4):
                cs = slice(n * LANES, (n + 1) * LANES)
                dkv_ref[:, cs] = carry[:, cs] + dwin[n][0:QB, :]
                carry[:, cs] = dwin[n][QB:2 * QB, :]

        @pl.when(i == nb)
        def _():
            dkv_ref[...] = carry[...]

    grid_spec = pltpu.PrefetchScalarGridSpec(
        num_scalar_prefetch=1, grid=(nb + 1,),
        in_specs=[pl.BlockSpec((QB, 2 * A_CH), lambda i, sk: (jnp.minimum(i, nb - 1), 0)),
                  pl.BlockSpec((QB, A_CH), lambda i, sk: (jnp.maximum(jnp.minimum(i, nb - 1) - 1, 0), 1)),
                  pl.BlockSpec((QB, A_CH), lambda i, sk: (jnp.minimum(i, nb - 1), 1))],
        out_specs=[pl.BlockSpec((QB, A_CH), lambda i, sk: (jnp.minimum(i, nb - 1), 0)),
                   pl.BlockSpec((QB, A_CH), lambda i, sk: (jnp.maximum(i - 1, 0), 0)),
                   pl.BlockSpec((8, LANES), lambda i, sk: (0, 0))],
        scratch_shapes=[pltpu.VMEM((QB, A_CH), F32)],
    )
    return pl.pallas_call(
        body, name=name, grid_spec=grid_spec,
        out_shape=[jax.ShapeDtypeStruct((s, A_CH), F32), jax.ShapeDtypeStruct((s, A_CH), F32),
                   jax.ShapeDtypeStruct((8, LANES), F32)],
        compiler_params=_cparams(("arbitrary",)),
    )(sinks, qkv, qkv, dcat)


def _local_step(x, positions, target, w, fetch=None, emit=None):
    row = lambda a, i: a[i:i + 1]
    tab = _rope_tables(positions)
    g = {}

    def ffn_fwd(xin, i, tgt=None):
        outs = _ffn_fwd(xin, row(w["ffn_norm_pre"], i), w["ffn_w_up", i], w["ffn_conv_w"][i],
                        w["ffn_w_down", i], row(w["ffn_norm_post"], i), f"ffn{i}_fwd", tgt)
        f, h, up, u = outs[-4:]
        return outs[:-4], (xin, f, h, up, u)

    def point(name, after):
        return emit(name, after, g) if emit is not None else 0.0

    def ffn_bwd(dxout, saved, i, tok):
        xin, f, h, up, u = saved
        dxin, dup, act, df, d_cw, dg_post, dg_pre = _ffn_bwd(
            dxout, f, xin, up, u, row(w["ffn_norm_pre"], i), row(w["ffn_norm_post"], i) + tok, w["ffn_w_down_t", i],
            w["ffn_w_up_t", i], w["ffn_conv_w"][i], f"ffn{i}_bwd")
        tok = point(f"ffn{i}_bwd_done", dxin)
        g["ffn_w_down", i] = _mm_tn(act, df, f"ffn{i}_down_dw")
        g["ffn_w_up", i] = _mm_tn(dup, h, f"ffn{i}_up_dw")
        return dxin, tok, dict(ffn_norm_post=dg_post, ffn_norm_pre=dg_pre, ffn_conv_w=d_cw)

    h0, z0, c0, a0, qkv = _ev_in_fwd(x, row(w["mix_norm_pre"], 0), w["ev_w_in"], tab, w["ev_a_conv_w"],
                                     w["ev_a_conv_b"], w["ev_a_ln_g"], w["ev_a_ln_b"], "ev_in")
    cat = _att_fwd(qkv, a0, w["ev_sinks"], "ev_att")
    m0, x1 = _mm_post(cat, w["ev_w_out"], row(w["mix_norm_post"], 0), x, "ev_out")
    if fetch is not None:
        w = {**w, **fetch("ffn0", x1)}
    (x2,), ffn0 = ffn_fwd(x1, 0)
    if fetch is not None:
        w = {**w, **fetch("layer1", x2)}
    h2, z1, y1, m1, x3 = _od_fwd(x2, row(w["mix_norm_pre"], 1), w["od_w_in"], w["od_conv_w"], w["od_w_out"],
                                 row(w["mix_norm_post"], 1), "od_fwd")
    (dx4, sq), ffn1 = ffn_fwd(x3, 1, target)

    dx3, _, gf1 = ffn_bwd(dx4, ffn1, 1, 0.0)
    dm1, dz1, g["od_conv_w"], dx2, dg_mo1, dg_mp1 = _od_bwd(
        dx3, m1, row(w["mix_norm_post"], 1), w["od_w_out_t"], z1, w["od_conv_w"], w["od_w_in_t"], x2,
        row(w["mix_norm_pre"], 1), "od_bwd")
    g["od_w_out"] = _mm_tn(y1, dm1, "od_out_dw")
    g["od_w_in"] = _mm_tn(dz1, h2, "od_in_dw")
    tok = point("layer1_grads", dx2)

    dx1, tok, gf0 = ffn_bwd(dx2, ffn0, 0, tok)
    tok = tok + point("ffn0_grads", dx1)
    dm0, dcat, dg_mo0 = _mm_post_bwd(dx1, m0, row(w["mix_norm_post"], 0) + tok, w["ev_w_out_t"], "ev_out_bwd")
    tok = point("ev_out_bwd_done", dcat)
    g["ev_w_out"] = _mm_tn(cat, dm0, "ev_out_dw")
    dq, dkv, dsk = _att_bwd(qkv, dcat, w["ev_sinks"] + tok, "ev_att_bwd")
    tok = point("ev_att_bwd_done", dq)
    dz0, g["ev_a_conv_w"], dvec, dx0, dg_mp0 = _ev_mid_bwd(
        dcat, c0, z0, dq, dkv, tab, w["ev_a_conv_w"], w["ev_a_ln_g"] + tok, w["ev_a_ln_b"], w["ev_w_in_t"], x,
        row(w["mix_norm_pre"], 0), dx1, "ev_in_bwd")
    point("ev_mid_bwd_done", dz0)
    g["ev_w_in"] = _mm_tn(dz0, h0, "ev_in_dw")

    g["ev_a_conv_b"] = dvec[0:1]
    g["ev_a_ln_g"] = dvec[1:2]
    g["ev_a_ln_b"] = dvec[2:3]
    g["ev_sinks"] = dsk[:, 0]
    g["mix_norm_pre"] = jnp.concatenate([dg_mp0, dg_mp1], axis=0)
    g["mix_norm_post"] = jnp.concatenate([dg_mo0, dg_mo1], axis=0)
    g["ffn_norm_pre"] = jnp.concatenate([gf0["ffn_norm_pre"], gf1["ffn_norm_pre"]], axis=0)
    g["ffn_norm_post"] = jnp.concatenate([gf0["ffn_norm_post"], gf1["ffn_norm_post"]], axis=0)
    g["ffn_conv_w"] = jnp.stack([gf0["ffn_conv_w"], gf1["ffn_conv_w"]], axis=0)
    return sq, dx0, g


ANY = pl.BlockSpec(memory_space=pl.ANY)
PACK_COLS = 1024


def _me():
    return lax.axis_index("x"), lax.axis_index("y"), lax.axis_index("c")


def _other_chips(x, y):
    return [(1 - x, y), (x, 1 - y), (1 - x, 1 - y)]


def _remote(src, dst, send, recv, dev):
    return pltpu.make_async_remote_copy(src_ref=src, dst_ref=dst, send_sem=send, recv_sem=recv,
                                        device_id=dev, device_id_type=MESH)


def _gather_chips(wp, name):
    r, cols = wp.shape
    rh = r // 2

    def body(w_ref, o_ref, send, recv):
        x, y, c = _me()
        p = 2 * x + y
        sib = (x, y, 1 - c)
        chips = _other_chips(x, y)
        half = pl.ds(c * rh, rh)
        other = pl.ds((1 - c) * rh, rh)
        sent = [_remote(w_ref.at[half], o_ref.at[p, half], send.at[k], recv.at[k], (cx, cy, c))
                for k, (cx, cy) in enumerate(chips)]
        for cp in sent:
            cp.start()
        for k, (cx, cy) in enumerate(chips):
            q = 2 * cx + cy
            _remote(w_ref.at[half], o_ref.at[q, half], send.at[k], recv.at[k], (cx, cy, c)).wait_recv()
            fwd = _remote(o_ref.at[q, half], o_ref.at[q, half], send.at[3 + k], recv.at[3 + k], sib)
            fwd.start()
            sent.append(fwd)
        for k, (cx, cy) in enumerate(chips):
            q = 2 * cx + cy
            _remote(o_ref.at[q, other], o_ref.at[q, other], send.at[3 + k], recv.at[3 + k], sib).wait_recv()
        for cp in sent:
            cp.wait_send()

    return pl.pallas_call(
        body, name=name, in_specs=[ANY], out_specs=ANY,
        out_shape=jax.ShapeDtypeStruct((N_CHIPS, r, cols), wp.dtype),
        scratch_shapes=[pltpu.SemaphoreType.DMA((6,)), pltpu.SemaphoreType.DMA((6,))],
    )(wp)


HBM_SPEC = pl.BlockSpec(memory_space=pltpu.HBM)
SEM_SPEC = pl.BlockSpec(memory_space=pltpu.SEMAPHORE)
DATAFLOW = pltpu.SideEffectType.DATAFLOW_SIDE_EFFECTING


def _gather_plan(w_ref, land_ref):
    x, y, c = _me()
    return [(w_ref, land_ref.at[2 * x + y], (cx, cy, c)) for cx, cy in _other_chips(x, y)]


def _copies_start(src, land_shape, plan, n, name):
    def body(src_ref, land_ref, send, recv, src_thru, land_thru, token):
        for k, (s_view, d_view, dev) in enumerate(plan(src_ref, land_ref)):
            _remote(s_view, d_view, send.at[k], recv.at[k], dev).start()
        token[...] = jnp.zeros_like(token)

    return pl.pallas_call(
        body, name=name,
        out_shape=(pltpu.SemaphoreType.DMA((n,)), pltpu.SemaphoreType.DMA((n,)), pltpu.HBM(src.shape, src.dtype),
                   pltpu.HBM(land_shape, src.dtype), jax.ShapeDtypeStruct((8, LANES), F32)),
        in_specs=(HBM_SPEC, HBM_SPEC),
        out_specs=(SEM_SPEC, SEM_SPEC, HBM_SPEC, HBM_SPEC, pl.BlockSpec(memory_space=pltpu.VMEM)),
        input_output_aliases={0: 2, 1: 3},
        compiler_params=pltpu.CompilerParams(has_side_effects=DATAFLOW),
    )(pltpu.with_memory_space_constraint(src, pltpu.HBM),
      pltpu.with_memory_space_constraint(lax.empty(land_shape, src.dtype), pltpu.HBM))


def _copies_wait(started, after, plan, name):
    send, recv, src_thru, land_thru, _ = started

    def body(src_ref, land_ref, send, recv, after_ref, src_dead, land_out):
        for k, (s_view, d_view, dev) in enumerate(plan(src_ref, land_ref)):
            cp = _remote(s_view, d_view, send.at[k], recv.at[k], dev)
            cp.wait_send()
            cp.wait_recv()

    return pl.pallas_call(
        body, name=name,
        out_shape=(pltpu.HBM(src_thru.shape, src_thru.dtype), pltpu.HBM(land_thru.shape, land_thru.dtype)),
        in_specs=(HBM_SPEC, HBM_SPEC, SEM_SPEC, SEM_SPEC, ANY),
        out_specs=(HBM_SPEC, HBM_SPEC),
        input_output_aliases={0: 0, 1: 1},
        compiler_params=pltpu.CompilerParams(has_side_effects=DATAFLOW),
    )(src_thru, land_thru, send, recv, after)


def _swap_plan(g_ref, land_ref):
    x, y, c = _me()
    return [(g_ref.at[q, 1 - c], land_ref.at[q], (x, y, 1 - c)) for q in range(N_CHIPS)]


def _ici_plan(a_ref, land_ref):
    x, y, c = _me()
    return [(a_ref.at[2 * cx + cy], land_ref.at[2 * x + y], (cx, cy, c)) for cx, cy in _other_chips(x, y)]


def _share_plan(h_ref, land_ref):
    x, y, c = _me()
    return [(h_ref, land_ref, (x, y, 1 - c))]


def _exchange8(v, reduce, name):
    r, cols = v.shape
    rel = [(a, b, d) for a in (0, 1) for b in (0, 1) for d in (0, 1) if (a, b, d) != (0, 0, 0)]

    def body(v_ref, o_ref, *rest):
        if reduce:
            gbuf, send, recv = rest
        else:
            gbuf = o_ref
            send, recv = rest
        x, y, c = _me()
        me = 4 * x + 2 * y + c
        gbuf[me] = v_ref[...]
        sent = []
        for k, (a, b, d) in enumerate(rel):
            cp = _remote(v_ref, gbuf.at[me], send.at[k], recv.at[k], ((x + a) % 2, (y + b) % 2, (c + d) % 2))
            cp.start()
            sent.append(cp)
        for k, (a, b, d) in enumerate(rel):
            src = 4 * ((x + a) % 2) + 2 * ((y + b) % 2) + (c + d) % 2
            _remote(v_ref, gbuf.at[src], send.at[k], recv.at[k], (x, y, c)).wait_recv()
        for cp in sent:
            cp.wait_send()
        if reduce:
            acc = gbuf[0]
            for n in range(1, 8):
                acc = acc + gbuf[n]
            o_ref[...] = acc

    vmem = pl.BlockSpec(memory_space=pltpu.VMEM)
    sems = [pltpu.SemaphoreType.DMA((7,)), pltpu.SemaphoreType.DMA((7,))]
    if reduce:
        out_shape = jax.ShapeDtypeStruct((r, cols), F32)
        scratch = [pltpu.VMEM((8, r, cols), F32)] + sems
    else:
        out_shape = jax.ShapeDtypeStruct((8, r, cols), F32)
        scratch = sems
    return pl.pallas_call(body, name=name, in_specs=[vmem], out_specs=vmem, out_shape=out_shape,
                          scratch_shapes=scratch)(v)


def _row_tile(rows, pref, mult=8):
    if rows <= pref:
        return rows
    t = (pref // mult) * mult
    while t >= mult:
        if rows % t == 0:
            return t
        t -= mult
    return rows


def _rs_add(g, sib, c, name):
    _, _, rh, cols = g.shape
    tr = _row_tile(rh, 512, 16)

    def body(c_ref, g_ref, s_ref, o_ref):
        o_ref[...] = (g_ref[...] + s_ref[...]).astype(BF16)

    grid_spec = pltpu.PrefetchScalarGridSpec(
        num_scalar_prefetch=1, grid=(N_CHIPS, rh // tr),
        in_specs=[pl.BlockSpec((None, None, tr, cols), lambda q, i, cr: (q, cr[0], i, 0)),
                  pl.BlockSpec((None, tr, cols), lambda q, i, cr: (q, i, 0))],
        out_specs=pl.BlockSpec((None, tr, cols), lambda q, i, cr: (q, i, 0)),
    )
    return pl.pallas_call(
        body, name=name, grid_spec=grid_spec,
        out_shape=jax.ShapeDtypeStruct((N_CHIPS, rh, cols), BF16),
        compiler_params=_cparams(("parallel", "parallel")),
    )(c, g, sib)


def _rs_sum(rb, a, chip, name):
    _, rh, cols = rb.shape
    tr = _row_tile(rh, 512, 16)

    def body(p_ref, r0, r1, r2, r3, own, o_ref):
        p = p_ref[0]
        ownv = own[...].astype(F32)
        acc = None
        for q, r in enumerate((r0, r1, r2, r3)):
            v = jnp.where(p == q, ownv, r[...].astype(F32))
            acc = v if acc is None else acc + v
        o_ref[...] = acc

    def spec(q):
        return pl.BlockSpec((None, tr, cols), lambda i, pr: (jnp.where(pr[0] == q, (q + 1) % N_CHIPS, q), i, 0))

    grid_spec = pltpu.PrefetchScalarGridSpec(
        num_scalar_prefetch=1, grid=(rh // tr,),
        in_specs=[spec(0), spec(1), spec(2), spec(3), pl.BlockSpec((None, tr, cols), lambda i, pr: (pr[0], i, 0))],
        out_specs=pl.BlockSpec((tr, cols), lambda i, pr: (i, 0)),
    )
    return pl.pallas_call(
        body, name=name, grid_spec=grid_spec,
        out_shape=jax.ShapeDtypeStruct((rh, cols), F32),
        compiler_params=_cparams(("parallel",)),
    )(chip, rb, rb, rb, rb, a)


def _adamw(w, g, m, v, name):
    rows, cols = w.shape
    tr = _row_tile(rows, 512)

    def body(w_ref, g_ref, m_ref, v_ref, d_ref, nm_ref, nv_ref):
        gv = g_ref[...]
        nm = ADAM_B1 * m_ref[...] + (1.0 - ADAM_B1) * gv
        nv = ADAM_B2 * v_ref[...] + (1.0 - ADAM_B2) * (gv * gv)
        m_hat = nm / (1.0 - ADAM_B1 ** ADAM_STEP)
        v_hat = nv / (1.0 - ADAM_B2 ** ADAM_STEP)
        d_ref[...] = -ADAM_LR * (m_hat / (jnp.sqrt(v_hat) + ADAM_EPS) + ADAM_WD * w_ref[...])
        nm_ref[...] = nm
        nv_ref[...] = nv

    spec = pl.BlockSpec((tr, cols), lambda i: (i, 0))
    shp = jax.ShapeDtypeStruct((rows, cols), F32)
    return pl.pallas_call(
        body, name=name, grid=(rows // tr,), in_specs=[spec] * 4, out_specs=[spec] * 3, out_shape=[shp] * 3,
        compiler_params=_cparams(("parallel",)),
    )(w, g, m, v)


WEIGHTS = ("mix_norm_pre", "mix_norm_post", "ffn_norm_pre", "ffn_norm_post", "ev_w_in", "ev_a_conv_w", "ev_a_conv_b",
           "ev_a_ln_g", "ev_a_ln_b", "ev_sinks", "ev_w_out", "od_w_in", "od_conv_w", "od_w_out", "ffn_w_up",
           "ffn_conv_w", "ffn_w_down")
MATS = (("ev_w_in", 2), ("ev_w_out", 1), ("od_w_in", 2), ("od_w_out", 1), ("ffn_w_up", 2), ("ffn_w_down", 1))
UNITS = (("ev_w_in", 0, 2), ("ev_w_out", 0, 1), ("ffn_w_up", 0, 2), ("ffn_w_down", 0, 1),
         ("od_w_in", 0, 2), ("od_w_out", 0, 1), ("ffn_w_up", 1, 2), ("ffn_w_down", 1, 1))
GATHER_GROUPS = ((0, 1), (2, 3), (4, 5, 6, 7))
REDUCE_GROUPS = {"layer1": (4, 5, 6, 7), "ffn0": (2, 3), "ev": (0, 1)}
SMALL_SHARDED = ("ev_a_conv_w", "od_conv_w", "ffn_conv_w")
REPLICATED = ("mix_norm_pre", "mix_norm_post", "ffn_norm_pre", "ffn_norm_post", "ev_a_conv_b", "ev_a_ln_g",
              "ev_a_ln_b", "ev_sinks")


def _pack(parts, rows_multiple):
    flat = jnp.concatenate([p.reshape(-1) for p in parts])
    unit = rows_multiple * PACK_COLS
    pad = (-flat.shape[0]) % unit
    if pad:
        flat = jnp.concatenate([flat, jnp.zeros((pad,), flat.dtype)])
    return flat.reshape(-1, PACK_COLS)


def _unpack(buf, shapes):
    flat = buf.reshape(-1)
    out, off = [], 0
    for shp in shapes:
        n = 1
        for d in shp:
            n *= d
        out.append(flat[off:off + n].reshape(shp))
        off += n
    return out


def _shard_rows(shard, axis):
    if axis == 2:
        shard = jnp.swapaxes(shard, 1, 2)
    return shard.reshape(-1, PACK_COLS)


def kernel(x, positions, mix_norm_pre, mix_norm_post, ffn_norm_pre, ffn_norm_post, ev_w_in, ev_a_conv_w, ev_a_conv_b, ev_a_ln_g, ev_a_ln_b, ev_sinks, ev_w_out, od_w_in, od_conv_w, od_w_out, ffn_w_up, ffn_conv_w, ffn_w_down, loss_target, m_mix_norm_pre, m_mix_norm_post, m_ffn_norm_pre, m_ffn_norm_post, m_ev_w_in, m_ev_a_conv_w, m_ev_a_conv_b, m_ev_a_ln_g, m_ev_a_ln_b, m_ev_sinks, m_ev_w_out, m_od_w_in, m_od_conv_w, m_od_w_out, m_ffn_w_up, m_ffn_conv_w, m_ffn_w_down, v_mix_norm_pre, v_mix_norm_post, v_ffn_norm_pre, v_ffn_norm_post, v_ev_w_in, v_ev_a_conv_w, v_ev_a_conv_b, v_ev_a_ln_g, v_ev_a_ln_b, v_ev_sinks, v_ev_w_out, v_od_w_in, v_od_conv_w, v_od_w_out, v_ffn_w_up, v_ffn_conv_w, v_ffn_w_down):
    wts = dict(zip(WEIGHTS, (mix_norm_pre, mix_norm_post, ffn_norm_pre, ffn_norm_post, ev_w_in, ev_a_conv_w, ev_a_conv_b,
                             ev_a_ln_g, ev_a_ln_b, ev_sinks, ev_w_out, od_w_in, od_conv_w, od_w_out, ffn_w_up, ffn_conv_w,
                             ffn_w_down)))
    mom = dict(zip(WEIGHTS, (m_mix_norm_pre, m_mix_norm_post, m_ffn_norm_pre, m_ffn_norm_post, m_ev_w_in, m_ev_a_conv_w,
                             m_ev_a_conv_b, m_ev_a_ln_g, m_ev_a_ln_b, m_ev_sinks, m_ev_w_out, m_od_w_in, m_od_conv_w,
                             m_od_w_out, m_ffn_w_up, m_ffn_conv_w, m_ffn_w_down)))
    var = dict(zip(WEIGHTS, (v_mix_norm_pre, v_mix_norm_post, v_ffn_norm_pre, v_ffn_norm_post, v_ev_w_in, v_ev_a_conv_w,
                             v_ev_a_conv_b, v_ev_a_ln_g, v_ev_a_ln_b, v_ev_sinks, v_ev_w_out, v_od_w_in, v_od_conv_w,
                             v_od_w_out, v_ffn_w_up, v_ffn_conv_w, v_ffn_w_down)))
    xi, yi, ci = _me()
    chip = 2 * xi + yi

    unit_rows = [_shard_rows(wts[k][l:l + 1].astype(BF16), axis) for k, l, axis in UNITS]

    def group_block(group):
        return jnp.concatenate([unit_rows[u] for u in group], axis=0)

    def unpack_group(group, landed, own):
        full = lax.dynamic_update_slice(landed, own[None], (chip, 0, 0))
        out, off = {}, 0
        for u in group:
            k, l, axis = UNITS[u]
            n = unit_rows[u].shape[0]
            native = full[:, off:off + n].reshape(N_CHIPS * n, PACK_COLS)
            off += n
            key = (lambda name: (name, l)) if k.startswith("ffn") else (lambda name: name)
            out[key(k + "_t" if axis == 2 else k)] = native
            out[key(k if axis == 2 else k + "_t")] = native.T
        return out

    small_shapes = [wts[k].shape for k in SMALL_SHARDED]
    small_all = _exchange8(_pack([wts[k] for k in SMALL_SHARDED], 8), False, "gather_small")
    blocks = [group_block(grp) for grp in GATHER_GROUPS]
    first = _gather_chips(blocks[0], "gather_mats")
    later = {}
    for stage, grp, blk in zip(("ffn0", "layer1"), GATHER_GROUPS[1:], blocks[1:]):
        later[stage] = (grp, blk, _copies_start(blk, (N_CHIPS,) + blk.shape, _gather_plan, 3, "gather_" + stage + "_start"))

    def fetch(stage, after):
        grp, blk, started = later[stage]
        own, landed = _copies_wait(started, after, _gather_plan, "gather_" + stage + "_wait")
        return unpack_group(grp, landed, own)

    w = {k: wts[k] for k in REPLICATED}
    w.update(unpack_group(GATHER_GROUPS[0], first, blocks[0]))
    per_chip = [_unpack(small_all[2 * q], small_shapes) for q in range(N_CHIPS)]
    for n, k in enumerate(SMALL_SHARDED):
        w[k] = jnp.concatenate([per_chip[q][n] for q in range(N_CHIPS)], axis=-1)
    for k in ("ev_a_conv_w", "od_conv_w"):
        w[k] = w[k][0]
    w["ev_sinks"] = w["ev_sinks"][0]
    w["mix_norm_pre"] = w["mix_norm_pre"] + sum(later[s][2][4][0, 0] for s in later)

    core = jnp.reshape(ci, (1,)).astype(jnp.int32)
    chip_arr = jnp.reshape(chip, (1,)).astype(jnp.int32)
    per_layer = {}

    def group_grads(group, g):
        gp = jnp.concatenate([(g[k, l] if k.startswith("ffn") else g[k]).reshape(N_CHIPS, -1, PACK_COLS)
                              for k, l, _ in (UNITS[u] for u in group)], axis=1)
        return gp.reshape(N_CHIPS, 2, gp.shape[1] // 2, PACK_COLS)

    def finish(group, half, other):
        red = jnp.concatenate([jnp.where(ci == 0, half, other), jnp.where(ci == 0, other, half)], axis=0)
        off = 0
        for u in group:
            k, l, axis = UNITS[u]
            n = unit_rows[u].shape[0]
            part = red[off:off + n]
            off += n
            per_layer[k, l] = part.T if axis == 2 else part

    chains = {}

    def chain_step(tag, after, g):
        group = REDUCE_GROUPS[tag]
        st = chains.setdefault(tag, {"step": 0})
        step = st["step"]
        st["step"] = step + 1
        if step == 0:
            gp = group_grads(group, g)
            rh = gp.shape[2]
            st["swap"] = _copies_start(gp, (N_CHIPS, rh, PACK_COLS), _swap_plan, N_CHIPS, f"rs_{tag}_swap_start")
            return st["swap"][4][0, 0]
        if step == 1:
            gp, sib = _copies_wait(st["swap"], after, _swap_plan, f"rs_{tag}_swap_wait")
            pair = _rs_add(gp, sib, core, f"rs_{tag}_add")
            st["ici"] = _copies_start(pair, pair.shape, _ici_plan, 3, f"rs_{tag}_ici_start")
            return st["ici"][4][0, 0]
        if step == 2:
            pair, landed = _copies_wait(st["ici"], after, _ici_plan, f"rs_{tag}_ici_wait")
            half = _rs_sum(landed, pair, chip_arr, f"rs_{tag}_sum")
            st["share"] = _copies_start(half, half.shape, _share_plan, 1, f"rs_{tag}_share_start")
            return st["share"][4][0, 0]
        half, other = _copies_wait(st["share"], after, _share_plan, f"rs_{tag}_share_wait")
        finish(group, half, other)
        return 0.0

    schedule = {"layer1_grads": ("layer1",), "ffn0_bwd_done": ("layer1",), "ffn0_grads": ("ffn0",),
                "ev_out_bwd_done": ("layer1", "ffn0"), "ev_att_bwd_done": ("layer1", "ffn0"),
                "ev_mid_bwd_done": ("ffn0",)}

    def emit(place, after, g):
        return sum(chain_step(tag, after, g) for tag in schedule.get(place, ()))

    sq, dx, g = _local_step(x[0], positions[0], loss_target[0], w, fetch, emit)
    loss = lax.psum(0.5 * jnp.sum(sq) / D_MODEL, ("x", "y", "c"))

    grads, deltas, new_m, new_v = {}, {}, {}, {}

    def adamw(k):
        shp = wts[k].shape
        two_d = (-1, shp[-1])
        d, nm, nv = _adamw(wts[k].reshape(two_d), grads[k].reshape(two_d), mom[k].reshape(two_d), var[k].reshape(two_d),
                           "adamw_" + k)
        deltas[k], new_m[k], new_v[k] = d.reshape(shp), nm.reshape(shp), nv.reshape(shp)
        return nv

    def stacked(k):
        return jnp.stack([per_layer[k, l] for l in range(wts[k].shape[0])], axis=0)

    chain_step("ev", None, g)

    small_keys = REPLICATED + SMALL_SHARDED
    full_shapes = [wts[k].shape for k in REPLICATED] + [wts[k].shape[:-1] + (wts[k].shape[-1] * N_CHIPS,) for k in SMALL_SHARDED]
    sm = _exchange8(_pack([g[k] for k in small_keys], 8), True, "reduce_small")
    for k, full in zip(small_keys, _unpack(sm, full_shapes)):
        if k in SMALL_SHARDED:
            n = wts[k].shape[-1]
            full = lax.dynamic_slice_in_dim(full, chip * n, n, axis=full.ndim - 1)
        grads[k] = full
    chain_step("ev", sm, g)

    for k in ("od_w_in", "od_w_out", "ffn_w_up"):
        grads[k] = stacked(k)
        last = adamw(k)
    chain_step("ev", last, g)
    grads["ffn_w_down"] = stacked("ffn_w_down")
    last = adamw("ffn_w_down")
    for k in small_keys:
        last = adamw(k)
    chain_step("ev", last, g)
    for k in ("ev_w_in", "ev_w_out"):
        grads[k] = stacked(k)
        adamw(k)

    return (loss, dx[None], *[grads[k] for k in WEIGHTS], *[deltas[k] for k in WEIGHTS],
            *[new_m[k] for k in WEIGHTS], *[new_v[k] for k in WEIGHTS])
```

```python
import functools

import jax
import jax.numpy as jnp
import numpy as np
from jax import lax
from jax.experimental import pallas as pl
from jax.experimental.pallas import tpu as pltpu

F32 = jnp.float32
BF16 = jnp.bfloat16
MESH = pl.DeviceIdType.MESH

D_MODEL = 1024
HEAD_DIM = 64
A_CH = 512
A_CONV = 31
N_Q_HEADS = 8
WINDOW = 128
ROPE_THETA = 500000.0
ROPE_DIM = 16
D_FF = 2816
RMS_EPS = 1e-6
LN_EPS = 1e-5
ADAM_LR = 0.001
ADAM_B1 = 0.9
ADAM_B2 = 0.999
ADAM_EPS = 1e-08
ADAM_WD = 0.01
ADAM_STEP = 10

LANES = 128
HALO16 = 16
HALO32 = 32
VMEM_LIMIT = 56 * 1024 * 1024
FFN_BWD_VMEM = 60 * 1024 * 1024
N_CHIPS = 4


def _cparams(sem):
    return pltpu.CompilerParams(dimension_semantics=sem, vmem_limit_bytes=VMEM_LIMIT)


def _tile(n, pref):
    if n <= pref:
        return n
    t = (pref // LANES) * LANES
    while t >= LANES:
        if n % t == 0:
            return t
        t -= LANES
    return n


MM_ROWS = 512


def _rms_scale(v):
    return lax.rsqrt(jnp.mean(v * v, axis=-1, keepdims=True) + RMS_EPS)


def _rms_bwd(dy, v, g):
    r = _rms_scale(v)
    nrm = v * r
    dn = dy * g
    return r * (dn - nrm * jnp.mean(dn * nrm, axis=-1, keepdims=True)), jnp.sum(dy * nrm, axis=0, keepdims=True)


def _mm_post(a, w, g, xres, name):
    s, k = a.shape
    d = w.shape[1]
    tm = min(MM_ROWS, s)

    def body(a_ref, w_ref, g_ref, x_ref, m_ref, o_ref):
        mv = jnp.dot(a_ref[...], w_ref[...], preferred_element_type=F32)
        m_ref[...] = mv
        o_ref[...] = x_ref[...] + mv * _rms_scale(mv) * g_ref[...]

    row = pl.BlockSpec((tm, d), lambda i: (i, 0))
    return pl.pallas_call(
        body, name=name, grid=(s // tm,),
        in_specs=[pl.BlockSpec((tm, k), lambda i: (i, 0)), _full((k, d)), _full((1, d)), row],
        out_specs=[row, row],
        out_shape=[jax.ShapeDtypeStruct((s, d), F32), jax.ShapeDtypeStruct((s, d), F32)],
        compiler_params=_cparams(("parallel",)),
    )(a, w, g, xres)


def _mm_post_bwd(dy, m, g, w_t, name):
    s, d = m.shape
    k = w_t.shape[1]
    tm = min(MM_ROWS, s)

    def body(dy_ref, m_ref, g_ref, wt_ref, dm_ref, da_ref, dg_ref):
        @pl.when(pl.program_id(0) == 0)
        def _():
            dg_ref[...] = jnp.zeros_like(dg_ref)

        dm, dg = _rms_bwd(dy_ref[...], m_ref[...], g_ref[...])
        dg_ref[...] += dg
        dmb = dm.astype(BF16)
        dm_ref[...] = dmb
        da_ref[...] = jnp.dot(dmb, wt_ref[...], preferred_element_type=F32)

    row = pl.BlockSpec((tm, d), lambda i: (i, 0))
    return pl.pallas_call(
        body, name=name, grid=(s // tm,),
        in_specs=[row, row, _full((1, d)), _full((d, k))],
        out_specs=[row, pl.BlockSpec((tm, k), lambda i: (i, 0)), _full((1, d))],
        out_shape=[jax.ShapeDtypeStruct((s, d), BF16), jax.ShapeDtypeStruct((s, k), F32),
                   jax.ShapeDtypeStruct((1, d), F32)],
        compiler_params=_cparams(("arbitrary",)),
    )(dy, m, g, w_t)


def _mm_tn(a, b, name):
    s, k = a.shape
    _, n = b.shape
    tk = _tile(k, 1408)
    tn = _tile(n, 1408)
    ts = min(2048, s)

    def body(a_ref, b_ref, o_ref):
        @pl.when(pl.program_id(2) == 0)
        def _():
            o_ref[...] = jnp.zeros_like(o_ref)

        o_ref[...] += lax.dot_general(a_ref[...], b_ref[...], (((0,), (0,)), ((), ())),
                                      preferred_element_type=F32)

    return pl.pallas_call(
        body, name=name, grid=(k // tk, n // tn, s // ts),
        in_specs=[pl.BlockSpec((ts, tk), lambda i, j, l: (l, i)), pl.BlockSpec((ts, tn), lambda i, j, l: (l, j))],
        out_specs=pl.BlockSpec((tk, tn), lambda i, j, l: (i, j)),
        out_shape=jax.ShapeDtypeStruct((k, n), F32),
        compiler_params=_cparams(("parallel", "parallel", "arbitrary")),
    )(a, b)


def _cur(tr, w, col=0):
    return pl.BlockSpec((tr, w), lambda i: (i, col))


def _prev(tr, h, w, col=0):
    return pl.BlockSpec((h, w), lambda i: (jnp.maximum(i * (tr // h) - 1, 0), col))


def _next(tr, h, w, nrows, col=0):
    last = nrows // h - 1
    return pl.BlockSpec((h, w), lambda i: (jnp.minimum((i + 1) * (tr // h), last), col))


def _full(shape):
    return pl.BlockSpec(shape, lambda i: tuple(0 for _ in shape))


def _silu_parts(g):
    sig = jax.nn.sigmoid(g)
    return sig, g * sig


FFN_CW = 256
FFN_NBUF = 3


def _conv3_taps(buf, w, off, rows):
    return (w[0:1] * buf[pl.ds(off, rows), :] + w[1:2] * buf[pl.ds(off + 1, rows), :]
            + w[2:3] * buf[pl.ds(off + 2, rows), :])


WHOLE_VMEM = pl.BlockSpec(memory_space=pltpu.VMEM)


def _ffn_fwd(x, g_pre, wu, conv_w, wd, g_post, name, target=None):
    s, d = x.shape
    f2 = wu.shape[1]
    f = f2 // 2
    tr = min(256, s)
    h = HALO16
    cw = FFN_CW
    head = target is not None

    def body(*refs):
        if head:
            (x_ref, gpre_ref, wu_ref, cw_ref, wd_ref, gpost_ref, t_ref, xo_ref, sq_ref, f_ref, h_ref, up_ref, u_ref,
             carry, gbuf, vbuf, facc) = refs
        else:
            (x_ref, gpre_ref, wu_ref, cw_ref, wd_ref, gpost_ref, xo_ref, f_ref, h_ref, up_ref, u_ref,
             carry, gbuf, vbuf, facc) = refs

        @pl.when(pl.program_id(0) == 0)
        def _():
            carry[...] = jnp.zeros_like(carry)
            if head:
                sq_ref[...] = jnp.zeros_like(sq_ref)

        xv = x_ref[...]
        r = lax.rsqrt(jnp.mean(xv * xv, axis=-1, keepdims=True) + RMS_EPS)
        hv = (xv * r * gpre_ref[...]).astype(BF16)
        h_ref[...] = hv
        nchunk = f // cw

        def up_proj(j):
            for buf, base in ((gbuf, 0), (vbuf, f)):
                cs = slice(base + j * cw, base + (j + 1) * cw)
                dst = buf.at[j % FFN_NBUF]
                upc = jnp.dot(hv, wu_ref[:, cs], preferred_element_type=F32)
                up_ref[:, cs] = upc.astype(BF16)
                dst[0:h, :] = carry[:, cs]
                dst[h:h + tr, :] = upc
                carry[:, cs] = upc[tr - h:tr, :]

        def down_proj(j, act):
            part = jnp.dot(act, wd_ref[j * cw:(j + 1) * cw, :], preferred_element_type=F32)
            if j == 0:
                facc[...] = part
            else:
                facc[...] += part

        for j in range(FFN_NBUF - 1):
            up_proj(j)
        pending = None
        for j in range(nchunk):
            cg = slice(j * cw, (j + 1) * cw)
            cv = slice(f + j * cw, f + (j + 1) * cw)
            if j + FFN_NBUF - 1 < nchunk:
                up_proj(j + FFN_NBUF - 1)
            if pending is not None:
                down_proj(*pending)
            g = _conv3_taps(gbuf.at[j % FFN_NBUF], cw_ref[:, cg], h - 2, tr)
            v = _conv3_taps(vbuf.at[j % FFN_NBUF], cw_ref[:, cv], h - 2, tr)
            u_ref[:, cg] = g.astype(BF16)
            u_ref[:, cv] = v.astype(BF16)
            act = (g * jax.nn.sigmoid(g) * v).astype(BF16)
            pending = (j, act)
        down_proj(*pending)
        fv = facc[...]
        f_ref[...] = fv
        r2 = lax.rsqrt(jnp.mean(fv * fv, axis=-1, keepdims=True) + RMS_EPS)
        xo = xv + fv * r2 * gpost_ref[...]
        if head:
            err = xo - t_ref[...]
            xo_ref[...] = err * (1.0 / d)
            sq_ref[...] += jnp.sum(err * err, axis=0, keepdims=True)
        else:
            xo_ref[...] = xo

    row = _cur(tr, d)
    wide = _cur(tr, f2)
    vec = _full((1, d))
    out_specs = [row] + ([vec] if head else []) + [row, row, wide, wide]
    out_shape = ([jax.ShapeDtypeStruct((s, d), F32)] + ([jax.ShapeDtypeStruct((1, d), F32)] if head else [])
                 + [jax.ShapeDtypeStruct((s, d), F32), jax.ShapeDtypeStruct((s, d), BF16),
                    jax.ShapeDtypeStruct((s, f2), BF16), jax.ShapeDtypeStruct((s, f2), BF16)])
    return pl.pallas_call(
        body, name=name, grid=(s // tr,),
        in_specs=[row, vec, WHOLE_VMEM, _full((3, f2)), WHOLE_VMEM, vec] + ([row] if head else []),
        out_specs=out_specs, out_shape=out_shape,
        scratch_shapes=[pltpu.VMEM((h, f2), F32), pltpu.VMEM((FFN_NBUF, h + tr, cw), F32),
                        pltpu.VMEM((FFN_NBUF, h + tr, cw), F32), pltpu.VMEM((tr, d), F32)],
        compiler_params=_cparams(("arbitrary",)),
    )(*((x, g_pre, wu, conv_w, wd, g_post) + ((target,) if head else ())))


def _ffn_bwd(dxo, fout, x, up, u, g_pre, g_post, wd_t, wu_t, conv_w, name):
    s, d = x.shape
    f2 = up.shape[1]
    f = f2 // 2
    tr = min(256, s)
    nt = s // tr
    h = HALO16
    cw = FFN_CW

    def body(dy_ref, f_ref, x_ref, up_ref, u_ref, gpre_ref, gpost_ref, wdt_ref, wut_ref, cw_ref,
             dx_ref, dup_ref, act_ref, df_ref, dcw_ref, dgpost_ref, dgpre_ref, carry, dgbuf, dvbuf, dhacc):
        @pl.when(pl.program_id(0) == 0)
        def _():
            carry[...] = jnp.zeros_like(carry)
            dcw_ref[...] = jnp.zeros_like(dcw_ref)
            dgpost_ref[...] = jnp.zeros_like(dgpost_ref)
            dgpre_ref[...] = jnp.zeros_like(dgpre_ref)

        dy = dy_ref[...]
        fv = f_ref[...]
        r = lax.rsqrt(jnp.mean(fv * fv, axis=-1, keepdims=True) + RMS_EPS)
        nrm = fv * r
        dn = dy * gpost_ref[...]
        dfv = (r * (dn - nrm * jnp.mean(dn * nrm, axis=-1, keepdims=True))).astype(BF16)
        dgpost_ref[...] += jnp.sum(dy * nrm, axis=0, keepdims=True)
        df_ref[...] = dfv
        nchunk = f // cw

        def dh_part(dupb, cs, first):
            part = jnp.dot(dupb, wut_ref[cs, :], preferred_element_type=F32)
            if first:
                dhacc[...] = part
            else:
                dhacc[...] += part

        def dact_of(j):
            return jnp.dot(dfv, wdt_ref[:, j * cw:(j + 1) * cw], preferred_element_type=F32)

        ahead = [dact_of(0)]
        for j in range(nchunk):
            ch = slice(j * cw, (j + 1) * cw)
            cg = ch
            cv = slice(f + j * cw, f + (j + 1) * cw)
            dact = ahead.pop(0)
            if j + 1 < nchunk:
                ahead.append(dact_of(j + 1))
            g = u_ref[:, cg].astype(F32)
            v = u_ref[:, cv].astype(F32)
            sig, sil = _silu_parts(g)
            act_ref[:, ch] = (sil * v).astype(BF16)
            du_g = dact * v * (sig * (1.0 + g * (1.0 - sig)))
            du_v = dact * sil
            for k, (dbuf, du, cs) in enumerate(((dgbuf.at[j % FFN_NBUF], du_g, cg), (dvbuf.at[j % FFN_NBUF], du_v, cv))):
                dbuf[0:tr, :] = du
                dbuf[tr:tr + h, :] = carry[:, cs]
                carry[:, cs] = du[0:h, :]
                w = cw_ref[:, cs]
                xin = up_ref[:, cs].astype(F32)
                acc = None
                for sh in range(3):
                    dsh = dbuf[pl.ds(sh, tr), :]
                    term = w[2 - sh:3 - sh] * dsh
                    acc = term if acc is None else acc + term
                    dcw_ref[2 - sh:3 - sh, cs] += jnp.sum(xin * dsh, axis=0, keepdims=True)
                dupb = acc.astype(BF16)
                dup_ref[:, cs] = dupb
                dh_part(dupb, cs, j == 0 and k == 0)
        dh = dhacc[...]
        xv = x_ref[...]
        r1 = lax.rsqrt(jnp.mean(xv * xv, axis=-1, keepdims=True) + RMS_EPS)
        n1 = xv * r1
        dn1 = dh * gpre_ref[...]
        dx_ref[...] = dy + r1 * (dn1 - n1 * jnp.mean(dn1 * n1, axis=-1, keepdims=True))
        dgpre_ref[...] += jnp.sum(dh * n1, axis=0, keepdims=True)

    def rev(w):
        return pl.BlockSpec((tr, w), lambda i: (nt - 1 - i, 0))

    vec = _full((1, d))
    return pl.pallas_call(
        body, name=name, grid=(nt,),
        in_specs=[rev(d), rev(d), rev(d), rev(f2), rev(f2), vec, vec, WHOLE_VMEM, WHOLE_VMEM, _full((3, f2))],
        out_specs=[rev(d), rev(f2), rev(f), rev(d), _full((3, f2)), vec, vec],
        out_shape=[jax.ShapeDtypeStruct((s, d), F32), jax.ShapeDtypeStruct((s, f2), BF16),
                   jax.ShapeDtypeStruct((s, f), BF16), jax.ShapeDtypeStruct((s, d), BF16),
                   jax.ShapeDtypeStruct((3, f2), F32), jax.ShapeDtypeStruct((1, d), F32),
                   jax.ShapeDtypeStruct((1, d), F32)],
        scratch_shapes=[pltpu.VMEM((h, f2), F32), pltpu.VMEM((FFN_NBUF, tr + h, cw), F32),
                        pltpu.VMEM((FFN_NBUF, tr + h, cw), F32), pltpu.VMEM((tr, d), F32)],
        compiler_params=pltpu.CompilerParams(dimension_semantics=("arbitrary",), vmem_limit_bytes=FFN_BWD_VMEM),
    )(dxo, fout, x, up, u, g_pre, g_post, wd_t, wu_t, conv_w)


def _od_fwd(x, g, w, conv_w, w_out, g_post, name):
    s, d = x.shape
    d3 = w.shape[1]
    tr = min(512, s)
    h = HALO16
    cw = FFN_CW
    nchunk = d // cw

    def body(x_ref, g_ref, w_ref, cw_ref, wout_ref, gpost_ref, h_ref, z_ref, y_ref, m_ref, xo_ref, carry, buf, macc):
        @pl.when(pl.program_id(0) == 0)
        def _():
            carry[...] = jnp.zeros_like(carry)

        xv = x_ref[...]
        hv = (xv * _rms_scale(xv) * g_ref[...]).astype(BF16)
        h_ref[...] = hv

        def project(j):
            out = []
            for part in range(3):
                cs = slice(part * d + j * cw, part * d + (j + 1) * cw)
                zc = jnp.dot(hv, w_ref[:, cs], preferred_element_type=F32).astype(BF16)
                z_ref[:, cs] = zc
                out.append(zc.astype(F32))
            return out

        ahead = [project(0), project(1)]
        for j in range(nchunk):
            cb = slice(j * cw, (j + 1) * cw)
            bval, cval, uval = ahead.pop(0)
            if j + 2 < nchunk:
                ahead.append(project(j + 2))
            bf = buf.at[j % FFN_NBUF]
            cu = cval * uval
            bf[0:h, :] = carry[:, cb]
            bf[h:h + tr, :] = cu
            carry[:, cb] = cu[tr - h:tr, :]
            yv = (bval * _conv3_taps(bf, cw_ref[:, cb], h - 2, tr)).astype(BF16)
            y_ref[:, cb] = yv
            part = jnp.dot(yv, wout_ref[cb, :], preferred_element_type=F32)
            if j == 0:
                macc[...] = part
            else:
                macc[...] += part
        mv = macc[...]
        m_ref[...] = mv
        xo_ref[...] = xv + mv * _rms_scale(mv) * gpost_ref[...]

    row = _cur(tr, d)
    vec = _full((1, d))
    return pl.pallas_call(
        body, name=name, grid=(s // tr,),
        in_specs=[row, vec, WHOLE_VMEM, _full((3, d)), WHOLE_VMEM, vec],
        out_specs=[row, _cur(tr, d3), row, row, row],
        out_shape=[jax.ShapeDtypeStruct((s, d), BF16), jax.ShapeDtypeStruct((s, d3), BF16),
                   jax.ShapeDtypeStruct((s, d), BF16), jax.ShapeDtypeStruct((s, d), F32),
                   jax.ShapeDtypeStruct((s, d), F32)],
        scratch_shapes=[pltpu.VMEM((h, d), F32), pltpu.VMEM((FFN_NBUF, h + tr, cw), F32), pltpu.VMEM((tr, d), F32)],
        compiler_params=_cparams(("arbitrary",)),
    )(x, g, w, conv_w, w_out, g_post)


def _od_bwd(dxo, m, g_post, w_out_t, z, conv_w, w_t, x, g, name):
    s, d3 = z.shape
    d = d3 // 3
    tr = min(512, s)
    nt = s // tr
    h = HALO16
    cw = FFN_CW
    ext = tr + h
    nchunk = d // cw

    def body(dxo_ref, m_ref, gpost_ref, wot_ref, z_ref, zp_ref, w_ref, wt_ref, x_ref, g_ref,
             dm_ref, o_ref, dw_ref, dx_ref, dgpost_ref, dg_ref, carry, buf, dbuf, dhacc):
        i = pl.program_id(0)
        row0 = i == nt - 1

        @pl.when(i == 0)
        def _():
            carry[...] = jnp.zeros_like(carry)
            dw_ref[...] = jnp.zeros_like(dw_ref)
            dg_ref[...] = jnp.zeros_like(dg_ref)
            dgpost_ref[...] = jnp.zeros_like(dgpost_ref)

        dyo = dxo_ref[...]
        dm, dgp = _rms_bwd(dyo, m_ref[...], gpost_ref[...])
        dgpost_ref[...] += dgp
        dmb = dm.astype(BF16)
        dm_ref[...] = dmb

        def dy_of(j):
            return jnp.dot(dmb, wot_ref[:, j * cw:(j + 1) * cw], preferred_element_type=F32)

        ahead = [dy_of(0)]
        started = False
        for j in range(nchunk):
            cb = slice(j * cw, (j + 1) * cw)
            cc = slice(d + j * cw, d + (j + 1) * cw)
            cu = slice(2 * d + j * cw, 2 * d + (j + 1) * cw)
            dyv = ahead.pop(0)
            if j + 1 < nchunk:
                ahead.append(dy_of(j + 1))
            bf = buf.at[j % FFN_NBUF]
            db = dbuf.at[j % FFN_NBUF]
            w = w_ref[:, cb]
            cval = z_ref[:, cc].astype(F32)
            uval = z_ref[:, cu].astype(F32)
            bf[0:h, :] = jnp.where(row0, 0.0, zp_ref[:, cc].astype(F32) * zp_ref[:, cu].astype(F32))
            bf[h:h + tr, :] = cval * uval
            k = _conv3_taps(bf, w, h - 2, tr)
            dk = dyv * z_ref[:, cb].astype(F32)
            db[0:tr, :] = dk
            db[tr:ext, :] = carry[:, cb]
            carry[:, cb] = dk[0:h, :]
            dcu = w[2:3] * db[pl.ds(0, tr), :] + w[1:2] * db[pl.ds(1, tr), :] + w[0:1] * db[pl.ds(2, tr), :]
            for t in range(3):
                dw_ref[t:t + 1, cb] += jnp.sum(dk * bf[pl.ds(h - 2 + t, tr), :], axis=0, keepdims=True)
            for cs, val in ((cb, dyv * k), (cc, dcu * uval), (cu, dcu * cval)):
                piece = val.astype(BF16)
                o_ref[:, cs] = piece
                part = jnp.dot(piece, wt_ref[cs, :], preferred_element_type=F32)
                if started:
                    dhacc[...] += part
                else:
                    dhacc[...] = part
                    started = True
        dx, dg = _rms_bwd(dhacc[...], x_ref[...], g_ref[...])
        dg_ref[...] += dg
        dx_ref[...] = dyo + dx

    def rev(w):
        return pl.BlockSpec((tr, w), lambda i: (nt - 1 - i, 0))

    prev = pl.BlockSpec((h, d3), lambda i: (jnp.maximum((nt - 1 - i) * (tr // h) - 1, 0), 0))
    vec = _full((1, d))
    return pl.pallas_call(
        body, name=name, grid=(nt,),
        in_specs=[rev(d), rev(d), vec, WHOLE_VMEM, rev(d3), prev, _full((3, d)), WHOLE_VMEM, rev(d), vec],
        out_specs=[rev(d), rev(d3), _full((3, d)), rev(d), vec, vec],
        out_shape=[jax.ShapeDtypeStruct((s, d), BF16), jax.ShapeDtypeStruct((s, d3), BF16),
                   jax.ShapeDtypeStruct((3, d), F32), jax.ShapeDtypeStruct((s, d), F32),
                   jax.ShapeDtypeStruct((1, d), F32), jax.ShapeDtypeStruct((1, d), F32)],
        scratch_shapes=[pltpu.VMEM((h, d), F32), pltpu.VMEM((FFN_NBUF, h + tr, cw), F32),
                        pltpu.VMEM((FFN_NBUF, ext, cw), F32), pltpu.VMEM((tr, d), F32)],
        compiler_params=_cparams(("arbitrary",)),
    )(dxo, m, g_post, w_out_t, z, z, conv_w, w_t, x, g)


Q0 = 2 * A_CH
K0 = Q0 + N_Q_HEADS * HEAD_DIM
V0 = K0 + 2 * HEAD_DIM
EVEN_IN = V0 + 2 * HEAD_DIM


def _rope_tables(positions):
    half = ROPE_DIM // 2
    inv_freq = ROPE_THETA ** (-(jnp.arange(half, dtype=F32) * 2.0 / ROPE_DIM))
    ang = positions.astype(F32)[:, None] * inv_freq
    cs = jnp.concatenate([jnp.cos(ang), jnp.sin(ang)], axis=1)
    spread = np.zeros((2 * half, 3 * LANES), np.float32)
    const = np.zeros((1, 3 * LANES), np.float32)
    for lane in range(3 * LANES):
        dim, part = lane % HEAD_DIM, lane // LANES
        if part == 0:
            if dim < ROPE_DIM:
                spread[dim % half, lane] = 1.0
            else:
                const[0, lane] = 1.0
        elif part == 1 and half <= dim < ROPE_DIM:
            spread[half + dim - half, lane] = 1.0
        elif part == 2 and dim < half:
            spread[half + dim, lane] = -1.0
    return jnp.dot(cs, jnp.asarray(spread), precision=lax.Precision.HIGHEST) + jnp.asarray(const)


def _rope_fwd(x, tab):
    c, sa, sb = tab[:, 0:LANES], tab[:, LANES:2 * LANES], tab[:, 2 * LANES:3 * LANES]
    return x * c + pltpu.roll(x, 8, 1) * sa + pltpu.roll(x, LANES - 8, 1) * sb


def _rope_bwd(dy, tab):
    c, sa, sb = tab[:, 0:LANES], tab[:, LANES:2 * LANES], tab[:, 2 * LANES:3 * LANES]
    return dy * c + pltpu.roll(dy * sa, LANES - 8, 1) + pltpu.roll(dy * sb, 8, 1)


def _ln_fwd(c, g, b):
    mu = jnp.mean(c, axis=-1, keepdims=True)
    xc = c - mu
    r = lax.rsqrt(jnp.mean(xc * xc, axis=-1, keepdims=True) + LN_EPS)
    nrm = xc * r
    return nrm, r, nrm * g + b


def _phase_fill(buf, ph, rows):
    for k in range(1, 8):
        ph[k - 1, 0:rows - 8, :] = buf[pl.ds(k, rows - 8), :]


def _phase_rows(buf, ph, off, n, cs):
    k = off % 8
    src = buf if k == 0 else ph.at[k - 1]
    return src[pl.ds(off - k, n), cs]


def _ev_in_fwd(x, g_pre, w_in, tab, conv_w, conv_b, ln_g, ln_b, name):
    s, d = x.shape
    tr = min(512, s)
    h = HALO32
    cw = LANES
    pw = 2 * LANES

    def body(x_ref, gpre_ref, win_ref, tab_ref, w_ref, b_ref, g_ref, lb_ref, h_ref, z_ref, c_ref, a_ref, qkv_ref,
             gbuf, cbuf, gph, carry):
        @pl.when(pl.program_id(0) == 0)
        def _():
            carry[...] = jnp.zeros_like(carry)

        xv = x_ref[...]
        hv = (xv * _rms_scale(xv) * gpre_ref[...]).astype(BF16)
        h_ref[...] = hv

        def project(lo_col, hi_col):
            for c0 in range(lo_col, hi_col, pw):
                cs = slice(c0, c0 + pw)
                z_ref[:, cs] = jnp.dot(hv, win_ref[:, cs], preferred_element_type=F32).astype(BF16)

        project(0, 2 * A_CH)
        glu = z_ref[:, 0:A_CH].astype(F32) * jax.nn.sigmoid(z_ref[:, A_CH:2 * A_CH].astype(F32))
        project(2 * A_CH, EVEN_IN)
        gbuf[0:h, :] = carry[...]
        gbuf[h:h + tr, :] = glu
        carry[...] = glu[tr - h:tr, :]
        _phase_fill(gbuf, gph, h + tr)
        for j in range(A_CH // cw):
            cs = slice(j * cw, (j + 1) * cw)
            acc = jnp.broadcast_to(b_ref[:, cs], (tr, cw))
            for t in range(A_CONV):
                acc = acc + w_ref[t:t + 1, cs] * _phase_rows(gbuf, gph, h - (A_CONV - 1) + t, tr, cs)
            cbuf[:, cs] = acc
        c = cbuf[...]
        c_ref[...] = c.astype(BF16)
        _, _, l = _ln_fwd(c, g_ref[...], lb_ref[...])
        a_ref[...] = (l * jax.nn.sigmoid(l)).astype(BF16)
        tab_v = tab_ref[...]
        for p in range(4):
            xq = z_ref[:, Q0 + p * LANES:Q0 + (p + 1) * LANES].astype(F32)
            qkv_ref[:, p * LANES:(p + 1) * LANES] = _rope_fwd(xq, tab_v).astype(BF16)
        lane = lax.broadcasted_iota(jnp.int32, (tr, LANES), 1)
        lo = lane < HEAD_DIM
        kr = _rope_fwd(z_ref[:, K0:K0 + LANES].astype(F32), tab_v)
        vr = z_ref[:, V0:V0 + LANES].astype(F32)
        for base, val in ((4 * LANES, kr), (6 * LANES, vr)):
            sw = pltpu.roll(val, HEAD_DIM, 1)
            qkv_ref[:, base:base + LANES] = jnp.where(lo, val, sw).astype(BF16)
            qkv_ref[:, base + LANES:base + 2 * LANES] = jnp.where(lo, sw, val).astype(BF16)

    return pl.pallas_call(
        body, name=name, grid=(s // tr,),
        in_specs=[_cur(tr, d), _full((1, d)), WHOLE_VMEM, _cur(tr, 3 * LANES), _full((A_CONV, A_CH)),
                  _full((1, A_CH)), _full((1, A_CH)), _full((1, A_CH))],
        out_specs=[_cur(tr, d), _cur(tr, EVEN_IN), _cur(tr, A_CH), _cur(tr, A_CH), _cur(tr, 2 * A_CH)],
        out_shape=[jax.ShapeDtypeStruct((s, d), BF16), jax.ShapeDtypeStruct((s, EVEN_IN), BF16),
                   jax.ShapeDtypeStruct((s, A_CH), BF16), jax.ShapeDtypeStruct((s, A_CH), BF16),
                   jax.ShapeDtypeStruct((s, 2 * A_CH), BF16)],
        scratch_shapes=[pltpu.VMEM((h + tr, A_CH), F32), pltpu.VMEM((tr, A_CH), F32),
                        pltpu.VMEM((7, h + tr, A_CH), F32), pltpu.VMEM((h, A_CH), F32)],
        compiler_params=_cparams(("arbitrary",)),
    )(x, g_pre, w_in, tab, conv_w, conv_b, ln_g, ln_b)


def _ev_mid_bwd(dcat, c, z, dq, dkv, tab, conv_w, ln_g, ln_b, w_t, x, g_pre, res, name):
    s = z.shape[0]
    tr = min(512, s)
    h = HALO32
    cw = LANES
    ext = tr + h

    def body(da_ref, dan_ref, c_ref, cn_ref, z_ref, dq_ref, dkv_ref, tab_ref, w_ref, g_ref, lb_ref,
             wt_ref, x_ref, gpre_ref, res_ref, dz_ref, dw_ref, dvec_ref, dx_ref, dg_ref, dcbuf, dcph, dhacc):
        i = pl.program_id(0)
        first = i == 0
        last = i == pl.num_programs(0) - 1

        @pl.when(first)
        def _():
            dw_ref[...] = jnp.zeros_like(dw_ref)
            dvec_ref[...] = jnp.zeros_like(dvec_ref)
            dg_ref[...] = jnp.zeros_like(dg_ref)

        started = []

        def dh_part(cs):
            part = jnp.dot(dz_ref[:, cs], wt_ref[cs, :], preferred_element_type=F32)
            if started:
                dhacc[...] += part
            else:
                dhacc[...] = part
                started.append(True)

        tab_v = tab_ref[...]
        for p in range(4):
            cs = slice(p * LANES, (p + 1) * LANES)
            dz_ref[:, Q0 + p * LANES:Q0 + (p + 1) * LANES] = _rope_bwd(dq_ref[:, cs], tab_v).astype(BF16)
        lane = lax.broadcasted_iota(jnp.int32, (tr, LANES), 1)
        lo = lane < HEAD_DIM

        def fold(base):
            p0 = dkv_ref[:, base:base + LANES]
            p1 = dkv_ref[:, base + LANES:base + 2 * LANES]
            s0 = p0 + pltpu.roll(p0, HEAD_DIM, 1)
            s1 = p1 + pltpu.roll(p1, HEAD_DIM, 1)
            return jnp.where(lo, s0, s1)

        dz_ref[:, K0:K0 + LANES] = _rope_bwd(fold(0), tab_v).astype(BF16)
        dz_ref[:, V0:V0 + LANES] = fold(2 * LANES).astype(BF16)
        dh_part(slice(Q0, EVEN_IN))

        gv = g_ref[...]

        def ln_silu_bwd(cv, dav):
            nrm, r, l = _ln_fwd(cv, gv, lb_ref[...])
            sig = jax.nn.sigmoid(l)
            dl = dav * (sig * (1.0 + l * (1.0 - sig)))
            dn = dl * gv
            dc = r * (dn - jnp.mean(dn, axis=-1, keepdims=True) - nrm * jnp.mean(dn * nrm, axis=-1, keepdims=True))
            return dc, dl, nrm

        dc, dl, nrm = ln_silu_bwd(c_ref[...].astype(F32), da_ref[...])
        dcn, _, _ = ln_silu_bwd(cn_ref[...].astype(F32), dan_ref[...])
        dcbuf[0:tr, :] = dc
        dcbuf[tr:ext, :] = jnp.where(last, 0.0, dcn)
        dvec_ref[0:1, :] += jnp.sum(dc, axis=0, keepdims=True)
        dvec_ref[1:2, :] += jnp.sum(dl * nrm, axis=0, keepdims=True)
        dvec_ref[2:3, :] += jnp.sum(dl, axis=0, keepdims=True)

        _phase_fill(dcbuf, dcph, ext)
        a_lin = z_ref[:, 0:A_CH].astype(F32)
        sig_g = jax.nn.sigmoid(z_ref[:, A_CH:2 * A_CH].astype(F32))
        glu = a_lin * sig_g
        for j in range(A_CH // cw):
            cs = slice(j * cw, (j + 1) * cw)
            gluj = glu[:, cs]
            acc = jnp.zeros((tr, cw), F32)
            for t in range(A_CONV):
                dsh = _phase_rows(dcbuf, dcph, A_CONV - 1 - t, tr, cs)
                acc = acc + w_ref[t:t + 1, cs] * dsh
                dw_ref[t:t + 1, cs] += jnp.sum(gluj * dsh, axis=0, keepdims=True)
            dz_ref[:, cs] = (acc * sig_g[:, cs]).astype(BF16)
            dz_ref[:, A_CH + j * cw:A_CH + (j + 1) * cw] = (
                acc * a_lin[:, cs] * sig_g[:, cs] * (1.0 - sig_g[:, cs])).astype(BF16)
            if j % 2 == 1:
                dh_part(slice((j - 1) * cw, (j + 1) * cw))
                dh_part(slice(A_CH + (j - 1) * cw, A_CH + (j + 1) * cw))

        dx, dg = _rms_bwd(dhacc[...], x_ref[...], gpre_ref[...])
        dg_ref[...] += dg
        dx_ref[...] = res_ref[...] + dx

    row = _cur(tr, D_MODEL)
    vec = _full((1, D_MODEL))
    return pl.pallas_call(
        body, name=name, grid=(s // tr,),
        in_specs=[_cur(tr, A_CH), _next(tr, h, A_CH, s), _cur(tr, A_CH), _next(tr, h, A_CH, s),
                  _cur(tr, EVEN_IN), _cur(tr, A_CH), _cur(tr, A_CH), _cur(tr, 3 * LANES),
                  _full((A_CONV, A_CH)), _full((1, A_CH)), _full((1, A_CH)), WHOLE_VMEM, row, vec, row],
        out_specs=[_cur(tr, EVEN_IN), _full((A_CONV, A_CH)), _full((8, A_CH)), row, vec],
        out_shape=[jax.ShapeDtypeStruct((s, EVEN_IN), BF16), jax.ShapeDtypeStruct((A_CONV, A_CH), F32),
                   jax.ShapeDtypeStruct((8, A_CH), F32), jax.ShapeDtypeStruct((s, D_MODEL), F32),
                   jax.ShapeDtypeStruct((1, D_MODEL), F32)],
        scratch_shapes=[pltpu.VMEM((ext, A_CH), F32), pltpu.VMEM((7, ext, A_CH), F32),
                        pltpu.VMEM((tr, D_MODEL), F32)],
        compiler_params=_cparams(("arbitrary",)),
    )(dcat, dcat, c, c, z, dq, dkv, tab, conv_w, ln_g, ln_b, w_t, x, g_pre, res)


NT = (((1,), (1,)), ((), ()))
TN = (((0,), (0,)), ((), ()))
QB = WINDOW
SCALE = HEAD_DIM ** -0.5
ATT_AHEAD = 2


def _att_scores(q2m, kwin):
    return lax.dot_general(q2m, kwin, NT, preferred_element_type=F32)


def _att_probs(raw, sink, mask):
    sc = jnp.where(mask, raw * SCALE, -jnp.inf)
    mx = jnp.maximum(jnp.max(sc, axis=-1, keepdims=True), sink)
    p = jnp.exp(sc - mx)
    ps = jnp.exp(sink - mx)
    inv = 1.0 / (jnp.sum(p, axis=-1, keepdims=True) + ps)
    return p * inv, ps * inv


def _att_mask(i):
    r = lax.broadcasted_iota(jnp.int32, (QB, 2 * QB), 0)
    kc = lax.broadcasted_iota(jnp.int32, (QB, 2 * QB), 1)
    diff = r + QB - kc
    return (diff >= 0) & (diff < WINDOW) & ((kc >= QB) | (i > 0))


def _half_masks(dtype):
    lane = lax.broadcasted_iota(jnp.int32, (1, LANES), 1)
    return (lane < HEAD_DIM).astype(dtype), (lane >= HEAD_DIM).astype(dtype)


def _att_fwd(qkv, a, sinks, name):
    s = qkv.shape[0]
    nq = 2 if s % (2 * QB) == 0 else 1
    rows = nq * QB
    nb = s // rows

    def body(sink_ref, qkv_ref, kvp_ref, a_ref, o_ref):
        i = pl.program_id(0)
        mlo, mhi = _half_masks(BF16)
        o_ref[:, 0:A_CH] = a_ref[...]

        def window(b, col):
            cur = qkv_ref[b * QB:(b + 1) * QB, A_CH + col * LANES:A_CH + (col + 1) * LANES]
            if b == 0:
                before = kvp_ref[:, col * LANES:(col + 1) * LANES]
            else:
                before = qkv_ref[(b - 1) * QB:b * QB, A_CH + col * LANES:A_CH + (col + 1) * LANES]
            return jnp.concatenate([before, cur], axis=0)

        def raw_scores(b, p):
            q2 = qkv_ref[b * QB:(b + 1) * QB, p * LANES:(p + 1) * LANES]
            kwin = window(b, p // 2)
            return _att_scores(q2 * mlo, kwin), _att_scores(q2 * mhi, kwin)

        units = [(b, p) for p in range(4) for b in range(nq)]
        ahead = [raw_scores(b, p) for b, p in units]
        masks = [_att_mask(i) if b == 0 else _att_mask(1) for b in range(nq)]
        for (b, p), (raw_e, raw_o) in zip(units, ahead):
            vwin = window(b, 2 + p // 2)
            pe, _ = _att_probs(raw_e, sink_ref[2 * p], masks[b])
            po, _ = _att_probs(raw_o, sink_ref[2 * p + 1], masks[b])
            o = (jnp.dot(pe.astype(BF16), vwin * mlo, preferred_element_type=F32)
                 + jnp.dot(po.astype(BF16), vwin * mhi, preferred_element_type=F32))
            o_ref[b * QB:(b + 1) * QB, A_CH + p * LANES:A_CH + (p + 1) * LANES] = o.astype(BF16)

    grid_spec = pltpu.PrefetchScalarGridSpec(
        num_scalar_prefetch=1, grid=(nb,),
        in_specs=[pl.BlockSpec((rows, 2 * A_CH), lambda i, sk: (i, 0)),
                  pl.BlockSpec((QB, A_CH), lambda i, sk: (jnp.maximum(nq * i - 1, 0), 1)),
                  pl.BlockSpec((rows, A_CH), lambda i, sk: (i, 0))],
        out_specs=pl.BlockSpec((rows, 2 * A_CH), lambda i, sk: (i, 0)),
    )
    return pl.pallas_call(
        body, name=name, grid_spec=grid_spec,
        out_shape=jax.ShapeDtypeStruct((s, 2 * A_CH), BF16),
        compiler_params=_cparams(("parallel",)),
    )(sinks, qkv, qkv, a)


def _att_bwd(qkv, dcat, sinks, name):
    s = qkv.shape[0]
    nb = s // QB

    def body(sink_ref, qkv_ref, kvp_ref, do_ref, dq_ref, dkv_ref, ds_ref, carry):
        i = pl.program_id(0)

        @pl.when(i == 0)
        def _():
            ds_ref[...] = jnp.zeros_like(ds_ref)
            carry[...] = jnp.zeros_like(carry)

        @pl.when(i < nb)
        def _():
            mask = _att_mask(i)
            mlo, mhi = _half_masks(BF16)
            dwin = [jnp.zeros((2 * QB, LANES), F32) for _ in range(4)]

            def window(col):
                return jnp.concatenate([kvp_ref[:, col * LANES:(col + 1) * LANES],
                                        qkv_ref[:, A_CH + col * LANES:A_CH + (col + 1) * LANES]], axis=0)

            def first_products(n):
                p, hm = n // 2, (mlo, mhi)[n % 2]
                qm = qkv_ref[:, p * LANES:(p + 1) * LANES] * hm
                dom = do_ref[:, p * LANES:(p + 1) * LANES].astype(BF16) * hm
                kwin, vwin = window(p // 2), window(2 + p // 2)
                return (qm, dom, kwin * hm, _att_scores(qm, kwin),
                        lax.dot_general(dom, vwin, NT, preferred_element_type=F32))

            ahead = [first_products(n) for n in range(ATT_AHEAD)]
            dq2 = None
            for n in range(N_Q_HEADS):
                g = n // 4
                qm, dom, kwm, raw, dp = ahead.pop(0)
                if n + ATT_AHEAD < N_Q_HEADS:
                    ahead.append(first_products(n + ATT_AHEAD))
                prob, psink = _att_probs(raw, sink_ref[n], mask)
                delta = jnp.sum(prob * dp, axis=-1, keepdims=True)
                dsc = (prob * (dp - delta) * SCALE).astype(BF16)
                ds_ref[n:n + 1, :] += jnp.broadcast_to(jnp.sum(-psink * delta, axis=0, keepdims=True), (1, LANES))
                part = jnp.dot(dsc, kwm, preferred_element_type=F32)
                dq2 = part if n % 2 == 0 else dq2 + part
                dwin[g] = dwin[g] + lax.dot_general(dsc, qm, TN, preferred_element_type=F32)
                dwin[2 + g] = dwin[2 + g] + lax.dot_general(prob.astype(BF16), dom, TN, preferred_element_type=F32)
                if n % 2 == 1:
                    dq_ref[:, (n // 2) * LANES:(n // 2 + 1) * LANES] = dq2
            for n in range(4):
                cs = slice(n * LANES, (n + 1) * LANES)
                dkv_ref[:, cs] = carry[:, cs] + dwin[n][0:QB, :]
                carry[:, cs] = dwin[n][QB:2 * QB, :]

        @pl.when(i == nb)
        def _():
            dkv_ref[...] = carry[...]

    grid_spec = pltpu.PrefetchScalarGridSpec(
        num_scalar_prefetch=1, grid=(nb + 1,),
        in_specs=[pl.BlockSpec((QB, 2 * A_CH), lambda i, sk: (jnp.minimum(i, nb - 1), 0)),
                  pl.BlockSpec((QB, A_CH), lambda i, sk: (jnp.maximum(jnp.minimum(i, nb - 1) - 1, 0), 1)),
                  pl.BlockSpec((QB, A_CH), lambda i, sk: (jnp.minimum(i, nb - 1), 1))],
        out_specs=[pl.BlockSpec((QB, A_CH), lambda i, sk: (jnp.minimum(i, nb - 1), 0)),
                   pl.BlockSpec((QB, A_CH), lambda i, sk: (jnp.maximum(i - 1, 0), 0)),
                   pl.BlockSpec((8, LANES), lambda i, sk: (0, 0))],
        scratch_shapes=[pltpu.VMEM((QB, A_CH), F32)],
    )
    return pl.pallas_call(
        body, name=name, grid_spec=grid_spec,
        out_shape=[jax.ShapeDtypeStruct((s, A_CH), F32), jax.ShapeDtypeStruct((s, A_CH), F32),
                   jax.ShapeDtypeStruct((8, LANES), F32)],
        compiler_params=_cparams(("arbitrary",)),
    )(sinks, qkv, qkv, dcat)


def _local_step(x, positions, target, w, fetch=None, emit=None):
    row = lambda a, i: a[i:i + 1]
    tab = _rope_tables(positions)
    g = {}

    def ffn_fwd(xin, i, tgt=None):
        outs = _ffn_fwd(xin, row(w["ffn_norm_pre"], i), w["ffn_w_up", i], w["ffn_conv_w"][i],
                        w["ffn_w_down", i], row(w["ffn_norm_post"], i), f"ffn{i}_fwd", tgt)
        f, h, up, u = outs[-4:]
        return outs[:-4], (xin, f, h, up, u)

    def point(name, after):
        return emit(name, after, g) if emit is not None else 0.0

    def ffn_bwd(dxout, saved, i, tok):
        xin, f, h, up, u = saved
        dxin, dup, act, df, d_cw, dg_post, dg_pre = _ffn_bwd(
            dxout, f, xin, up, u, row(w["ffn_norm_pre"], i), row(w["ffn_norm_post"], i) + tok, w["ffn_w_down_t", i],
            w["ffn_w_up_t", i], w["ffn_conv_w"][i], f"ffn{i}_bwd")
        tok = point(f"ffn{i}_bwd_done", dxin)
        g["ffn_w_down", i] = _mm_tn(act, df, f"ffn{i}_down_dw")
        g["ffn_w_up", i] = _mm_tn(dup, h, f"ffn{i}_up_dw")
        return dxin, tok, dict(ffn_norm_post=dg_post, ffn_norm_pre=dg_pre, ffn_conv_w=d_cw)

    h0, z0, c0, a0, qkv = _ev_in_fwd(x, row(w["mix_norm_pre"], 0), w["ev_w_in"], tab, w["ev_a_conv_w"],
                                     w["ev_a_conv_b"], w["ev_a_ln_g"], w["ev_a_ln_b"], "ev_in")
    cat = _att_fwd(qkv, a0, w["ev_sinks"], "ev_att")
    m0, x1 = _mm_post(cat, w["ev_w_out"], row(w["mix_norm_post"], 0), x, "ev_out")
    if fetch is not None:
        w = {**w, **fetch("ffn0", x1)}
    (x2,), ffn0 = ffn_fwd(x1, 0)
    if fetch is not None:
        w = {**w, **fetch("layer1", x2)}
    h2, z1, y1, m1, x3 = _od_fwd(x2, row(w["mix_norm_pre"], 1), w["od_w_in"], w["od_conv_w"], w["od_w_out"],
                                 row(w["mix_norm_post"], 1), "od_fwd")
    (dx4, sq), ffn1 = ffn_fwd(x3, 1, target)

    dx3, _, gf1 = ffn_bwd(dx4, ffn1, 1, 0.0)
    dm1, dz1, g["od_conv_w"], dx2, dg_mo1, dg_mp1 = _od_bwd(
        dx3, m1, row(w["mix_norm_post"], 1), w["od_w_out_t"], z1, w["od_conv_w"], w["od_w_in_t"], x2,
        row(w["mix_norm_pre"], 1), "od_bwd")
    g["od_w_out"] = _mm_tn(y1, dm1, "od_out_dw")
    g["od_w_in"] = _mm_tn(dz1, h2, "od_in_dw")
    tok = point("layer1_grads", dx2)

    dx1, tok, gf0 = ffn_bwd(dx2, ffn0, 0, tok)
    tok = tok + point("ffn0_grads", dx1)
    dm0, dcat, dg_mo0 = _mm_post_bwd(dx1, m0, row(w["mix_norm_post"], 0) + tok, w["ev_w_out_t"], "ev_out_bwd")
    tok = point("ev_out_bwd_done", dcat)
    g["ev_w_out"] = _mm_tn(cat, dm0, "ev_out_dw")
    dq, dkv, dsk = _att_bwd(qkv, dcat, w["ev_sinks"] + tok, "ev_att_bwd")
    tok = point("ev_att_bwd_done", dq)
    dz0, g["ev_a_conv_w"], dvec, dx0, dg_mp0 = _ev_mid_bwd(
        dcat, c0, z0, dq, dkv, tab, w["ev_a_conv_w"], w["ev_a_ln_g"] + tok, w["ev_a_ln_b"], w["ev_w_in_t"], x,
        row(w["mix_norm_pre"], 0), dx1, "ev_in_bwd")
    point("ev_mid_bwd_done", dz0)
    g["ev_w_in"] = _mm_tn(dz0, h0, "ev_in_dw")

    g["ev_a_conv_b"] = dvec[0:1]
    g["ev_a_ln_g"] = dvec[1:2]
    g["ev_a_ln_b"] = dvec[2:3]
    g["ev_sinks"] = dsk[:, 0]
    g["mix_norm_pre"] = jnp.concatenate([dg_mp0, dg_mp1], axis=0)
    g["mix_norm_post"] = jnp.concatenate([dg_mo0, dg_mo1], axis=0)
    g["ffn_norm_pre"] = jnp.concatenate([gf0["ffn_norm_pre"], gf1["ffn_norm_pre"]], axis=0)
    g["ffn_norm_post"] = jnp.concatenate([gf0["ffn_norm_post"], gf1["ffn_norm_post"]], axis=0)
    g["ffn_conv_w"] = jnp.stack([gf0["ffn_conv_w"], gf1["ffn_conv_w"]], axis=0)
    return sq, dx0, g


ANY = pl.BlockSpec(memory_space=pl.ANY)
PACK_COLS = 1024


def _me():
    return lax.axis_index("x"), lax.axis_index("y"), lax.axis_index("c")


def _other_chips(x, y):
    return [(1 - x, y), (x, 1 - y), (1 - x, 1 - y)]


def _remote(src, dst, send, recv, dev):
    return pltpu.make_async_remote_copy(src_ref=src, dst_ref=dst, send_sem=send, recv_sem=recv,
                                        device_id=dev, device_id_type=MESH)


def _gather_chips(wp, name):
    r, cols = wp.shape
    rh = r // 2

    def body(w_ref, o_ref, send, recv):
        x, y, c = _me()
        p = 2 * x + y
        sib = (x, y, 1 - c)
        chips = _other_chips(x, y)
        half = pl.ds(c * rh, rh)
        other = pl.ds((1 - c) * rh, rh)
        sent = [_remote(w_ref.at[half], o_ref.at[p, half], send.at[k], recv.at[k], (cx, cy, c))
                for k, (cx, cy) in enumerate(chips)]
        for cp in sent:
            cp.start()
        for k, (cx, cy) in enumerate(chips):
            q = 2 * cx + cy
            _remote(w_ref.at[half], o_ref.at[q, half], send.at[k], recv.at[k], (cx, cy, c)).wait_recv()
            fwd = _remote(o_ref.at[q, half], o_ref.at[q, half], send.at[3 + k], recv.at[3 + k], sib)
            fwd.start()
            sent.append(fwd)
        for k, (cx, cy) in enumerate(chips):
            q = 2 * cx + cy
            _remote(o_ref.at[q, other], o_ref.at[q, other], send.at[3 + k], recv.at[3 + k], sib).wait_recv()
        for cp in sent:
            cp.wait_send()

    return pl.pallas_call(
        body, name=name, in_specs=[ANY], out_specs=ANY,
        out_shape=jax.ShapeDtypeStruct((N_CHIPS, r, cols), wp.dtype),
        scratch_shapes=[pltpu.SemaphoreType.DMA((6,)), pltpu.SemaphoreType.DMA((6,))],
    )(wp)


HBM_SPEC = pl.BlockSpec(memory_space=pltpu.HBM)
SEM_SPEC = pl.BlockSpec(memory_space=pltpu.SEMAPHORE)
DATAFLOW = pltpu.SideEffectType.DATAFLOW_SIDE_EFFECTING


def _gather_plan(w_ref, land_ref):
    x, y, c = _me()
    return [(w_ref, land_ref.at[2 * x + y], (cx, cy, c)) for cx, cy in _other_chips(x, y)]


def _copies_start(src, land_shape, plan, n, name):
    def body(src_ref, land_ref, send, recv, src_thru, land_thru, token):
        for k, (s_view, d_view, dev) in enumerate(plan(src_ref, land_ref)):
            _remote(s_view, d_view, send.at[k], recv.at[k], dev).start()
        token[...] = jnp.zeros_like(token)

    return pl.pallas_call(
        body, name=name,
        out_shape=(pltpu.SemaphoreType.DMA((n,)), pltpu.SemaphoreType.DMA((n,)), pltpu.HBM(src.shape, src.dtype),
                   pltpu.HBM(land_shape, src.dtype), jax.ShapeDtypeStruct((8, LANES), F32)),
        in_specs=(HBM_SPEC, HBM_SPEC),
        out_specs=(SEM_SPEC, SEM_SPEC, HBM_SPEC, HBM_SPEC, pl.BlockSpec(memory_space=pltpu.VMEM)),
        input_output_aliases={0: 2, 1: 3},
        compiler_params=pltpu.CompilerParams(has_side_effects=DATAFLOW),
    )(pltpu.with_memory_space_constraint(src, pltpu.HBM),
      pltpu.with_memory_space_constraint(lax.empty(land_shape, src.dtype), pltpu.HBM))


def _copies_wait(started, after, plan, name):
    send, recv, src_thru, land_thru, _ = started

    def body(src_ref, land_ref, send, recv, after_ref, src_dead, land_out):
        for k, (s_view, d_view, dev) in enumerate(plan(src_ref, land_ref)):
            cp = _remote(s_view, d_view, send.at[k], recv.at[k], dev)
            cp.wait_send()
            cp.wait_recv()

    return pl.pallas_call(
        body, name=name,
        out_shape=(pltpu.HBM(src_thru.shape, src_thru.dtype), pltpu.HBM(land_thru.shape, land_thru.dtype)),
        in_specs=(HBM_SPEC, HBM_SPEC, SEM_SPEC, SEM_SPEC, ANY),
        out_specs=(HBM_SPEC, HBM_SPEC),
        input_output_aliases={0: 0, 1: 1},
        compiler_params=pltpu.CompilerParams(has_side_effects=DATAFLOW),
    )(src_thru, land_thru, send, recv, after)


def _swap_plan(g_ref, land_ref):
    x, y, c = _me()
    return [(g_ref.at[q, 1 - c], land_ref.at[q], (x, y, 1 - c)) for q in range(N_CHIPS)]


def _ici_plan(a_ref, land_ref):
    x, y, c = _me()
    return [(a_ref.at[2 * cx + cy], land_ref.at[2 * x + y], (cx, cy, c)) for cx, cy in _other_chips(x, y)]


def _share_plan(h_ref, land_ref):
    x, y, c = _me()
    return [(h_ref, land_ref, (x, y, 1 - c))]


def _exchange8(v, reduce, name):
    r, cols = v.shape
    rel = [(a, b, d) for a in (0, 1) for b in (0, 1) for d in (0, 1) if (a, b, d) != (0, 0, 0)]

    def body(v_ref, o_ref, *rest):
        if reduce:
            gbuf, send, recv = rest
        else:
            gbuf = o_ref
            send, recv = rest
        x, y, c = _me()
        me = 4 * x + 2 * y + c
        gbuf[me] = v_ref[...]
        sent = []
        for k, (a, b, d) in enumerate(rel):
            cp = _remote(v_ref, gbuf.at[me], send.at[k], recv.at[k], ((x + a) % 2, (y + b) % 2, (c + d) % 2))
            cp.start()
            sent.append(cp)
        for k, (a, b, d) in enumerate(rel):
            src = 4 * ((x + a) % 2) + 2 * ((y + b) % 2) + (c + d) % 2
            _remote(v_ref, gbuf.at[src], send.at[k], recv.at[k], (x, y, c)).wait_recv()
        for cp in sent:
            cp.wait_send()
        if reduce:
            acc = gbuf[0]
            for n in range(1, 8):
                acc = acc + gbuf[n]
            o_ref[...] = acc

    vmem = pl.BlockSpec(memory_space=pltpu.VMEM)
    sems = [pltpu.SemaphoreType.DMA((7,)), pltpu.SemaphoreType.DMA((7,))]
    if reduce:
        out_shape = jax.ShapeDtypeStruct((r, cols), F32)
        scratch = [pltpu.VMEM((8, r, cols), F32)] + sems
    else:
        out_shape = jax.ShapeDtypeStruct((8, r, cols), F32)
        scratch = sems
    return pl.pallas_call(body, name=name, in_specs=[vmem], out_specs=vmem, out_shape=out_shape,
                          scratch_shapes=scratch)(v)


def _row_tile(rows, pref, mult=8):
    if rows <= pref:
        return rows
    t = (pref // mult) * mult
    while t >= mult:
        if rows % t == 0:
            return t
        t -= mult
    return rows


def _rs_add(g, sib, c, name):
    _, _, rh, cols = g.shape
    tr = _row_tile(rh, 512, 16)

    def body(c_ref, g_ref, s_ref, o_ref):
        o_ref[...] = (g_ref[...] + s_ref[...]).astype(BF16)

    grid_spec = pltpu.PrefetchScalarGridSpec(
        num_scalar_prefetch=1, grid=(N_CHIPS, rh // tr),
        in_specs=[pl.BlockSpec((None, None, tr, cols), lambda q, i, cr: (q, cr[0], i, 0)),
                  pl.BlockSpec((None, tr, cols), lambda q, i, cr: (q, i, 0))],
        out_specs=pl.BlockSpec((None, tr, cols), lambda q, i, cr: (q, i, 0)),
    )
    return pl.pallas_call(
        body, name=name, grid_spec=grid_spec,
        out_shape=jax.ShapeDtypeStruct((N_CHIPS, rh, cols), BF16),
        compiler_params=_cparams(("parallel", "parallel")),
    )(c, g, sib)


def _rs_sum(rb, a, chip, name):
    _, rh, cols = rb.shape
    tr = _row_tile(rh, 512, 16)

    def body(p_ref, r0, r1, r2, r3, own, o_ref):
        p = p_ref[0]
        ownv = own[...].astype(F32)
        acc = None
        for q, r in enumerate((r0, r1, r2, r3)):
            v = jnp.where(p == q, ownv, r[...].astype(F32))
            acc = v if acc is None else acc + v
        o_ref[...] = acc

    def spec(q):
        return pl.BlockSpec((None, tr, cols), lambda i, pr: (jnp.where(pr[0] == q, (q + 1) % N_CHIPS, q), i, 0))

    grid_spec = pltpu.PrefetchScalarGridSpec(
        num_scalar_prefetch=1, grid=(rh // tr,),
        in_specs=[spec(0), spec(1), spec(2), spec(3), pl.BlockSpec((None, tr, cols), lambda i, pr: (pr[0], i, 0))],
        out_specs=pl.BlockSpec((tr, cols), lambda i, pr: (i, 0)),
    )
    return pl.pallas_call(
        body, name=name, grid_spec=grid_spec,
        out_shape=jax.ShapeDtypeStruct((rh, cols), F32),
        compiler_params=_cparams(("parallel",)),
    )(chip, rb, rb, rb, rb, a)


def _adamw(w, g, m, v, name):
    rows, cols = w.shape
    tr = _row_tile(rows, 512)

    def body(w_ref, g_ref, m_ref, v_ref, d_ref, nm_ref, nv_ref):
        gv = g_ref[...]
        nm = ADAM_B1 * m_ref[...] + (1.0 - ADAM_B1) * gv
        nv = ADAM_B2 * v_ref[...] + (1.0 - ADAM_B2) * (gv * gv)
        m_hat = nm / (1.0 - ADAM_B1 ** ADAM_STEP)
        v_hat = nv / (1.0 - ADAM_B2 ** ADAM_STEP)
        d_ref[...] = -ADAM_LR * (m_hat / (jnp.sqrt(v_hat) + ADAM_EPS) + ADAM_WD * w_ref[...])
        nm_ref[...] = nm
        nv_ref[...] = nv

    spec = pl.BlockSpec((tr, cols), lambda i: (i, 0))
    shp = jax.ShapeDtypeStruct((rows, cols), F32)
    return pl.pallas_call(
        body, name=name, grid=(rows // tr,), in_specs=[spec] * 4, out_specs=[spec] * 3, out_shape=[shp] * 3,
        compiler_params=_cparams(("parallel",)),
    )(w, g, m, v)


WEIGHTS = ("mix_norm_pre", "mix_norm_post", "ffn_norm_pre", "ffn_norm_post", "ev_w_in", "ev_a_conv_w", "ev_a_conv_b",
           "ev_a_ln_g", "ev_a_ln_b", "ev_sinks", "ev_w_out", "od_w_in", "od_conv_w", "od_w_out", "ffn_w_up",
           "ffn_conv_w", "ffn_w_down")
MATS = (("ev_w_in", 2), ("ev_w_out", 1), ("od_w_in", 2), ("od_w_out", 1), ("ffn_w_up", 2), ("ffn_w_down", 1))
UNITS = (("ev_w_in", 0, 2), ("ev_w_out", 0, 1), ("ffn_w_up", 0, 2), ("ffn_w_down", 0, 1),
         ("od_w_in", 0, 2), ("od_w_out", 0, 1), ("ffn_w_up", 1, 2), ("ffn_w_down", 1, 1))
GATHER_GROUPS = ((0, 1), (2, 3), (4, 5, 6, 7))
REDUCE_GROUPS = {"layer1": (4, 5, 6, 7), "ffn0": (2, 3), "ev": (0, 1)}
SMALL_SHARDED = ("ev_a_conv_w", "od_conv_w", "ffn_conv_w")
REPLICATED = ("mix_norm_pre", "mix_norm_post", "ffn_norm_pre", "ffn_norm_post", "ev_a_conv_b", "ev_a_ln_g",
              "ev_a_ln_b", "ev_sinks")


def _pack(parts, rows_multiple):
    flat = jnp.concatenate([p.reshape(-1) for p in parts])
    unit = rows_multiple * PACK_COLS
    pad = (-flat.shape[0]) % unit
    if pad:
        flat = jnp.concatenate([flat, jnp.zeros((pad,), flat.dtype)])
    return flat.reshape(-1, PACK_COLS)


def _unpack(buf, shapes):
    flat = buf.reshape(-1)
    out, off = [], 0
    for shp in shapes:
        n = 1
        for d in shp:
            n *= d
        out.append(flat[off:off + n].reshape(shp))
        off += n
    return out


def _shard_rows(shard, axis):
    if axis == 2:
        shard = jnp.swapaxes(shard, 1, 2)
    return shard.reshape(-1, PACK_COLS)


def kernel(x, positions, mix_norm_pre, mix_norm_post, ffn_norm_pre, ffn_norm_post, ev_w_in, ev_a_conv_w, ev_a_conv_b, ev_a_ln_g, ev_a_ln_b, ev_sinks, ev_w_out, od_w_in, od_conv_w, od_w_out, ffn_w_up, ffn_conv_w, ffn_w_down, loss_target, m_mix_norm_pre, m_mix_norm_post, m_ffn_norm_pre, m_ffn_norm_post, m_ev_w_in, m_ev_a_conv_w, m_ev_a_conv_b, m_ev_a_ln_g, m_ev_a_ln_b, m_ev_sinks, m_ev_w_out, m_od_w_in, m_od_conv_w, m_od_w_out, m_ffn_w_up, m_ffn_conv_w, m_ffn_w_down, v_mix_norm_pre, v_mix_norm_post, v_ffn_norm_pre, v_ffn_norm_post, v_ev_w_in, v_ev_a_conv_w, v_ev_a_conv_b, v_ev_a_ln_g, v_ev_a_ln_b, v_ev_sinks, v_ev_w_out, v_od_w_in, v_od_conv_w, v_od_w_out, v_ffn_w_up, v_ffn_conv_w, v_ffn_w_down):
    wts = dict(zip(WEIGHTS, (mix_norm_pre, mix_norm_post, ffn_norm_pre, ffn_norm_post, ev_w_in, ev_a_conv_w, ev_a_conv_b,
                             ev_a_ln_g, ev_a_ln_b, ev_sinks, ev_w_out, od_w_in, od_conv_w, od_w_out, ffn_w_up, ffn_conv_w,
                             ffn_w_down)))
    mom = dict(zip(WEIGHTS, (m_mix_norm_pre, m_mix_norm_post, m_ffn_norm_pre, m_ffn_norm_post, m_ev_w_in, m_ev_a_conv_w,
                             m_ev_a_conv_b, m_ev_a_ln_g, m_ev_a_ln_b, m_ev_sinks, m_ev_w_out, m_od_w_in, m_od_conv_w,
                             m_od_w_out, m_ffn_w_up, m_ffn_conv_w, m_ffn_w_down)))
    var = dict(zip(WEIGHTS, (v_mix_norm_pre, v_mix_norm_post, v_ffn_norm_pre, v_ffn_norm_post, v_ev_w_in, v_ev_a_conv_w,
                             v_ev_a_conv_b, v_ev_a_ln_g, v_ev_a_ln_b, v_ev_sinks, v_ev_w_out, v_od_w_in, v_od_conv_w,
                             v_od_w_out, v_ffn_w_up, v_ffn_conv_w, v_ffn_w_down)))
    xi, yi, ci = _me()
    chip = 2 * xi + yi

    unit_rows = [_shard_rows(wts[k][l:l + 1].astype(BF16), axis) for k, l, axis in UNITS]

    def group_block(group):
        return jnp.concatenate([unit_rows[u] for u in group], axis=0)

    def unpack_group(group, landed, own):
        full = lax.dynamic_update_slice(landed, own[None], (chip, 0, 0))
        out, off = {}, 0
        for u in group:
            k, l, axis = UNITS[u]
            n = unit_rows[u].shape[0]
            native = full[:, off:off + n].reshape(N_CHIPS * n, PACK_COLS)
            off += n
            key = (lambda name: (name, l)) if k.startswith("ffn") else (lambda name: name)
            out[key(k + "_t" if axis == 2 else k)] = native
            out[key(k if axis == 2 else k + "_t")] = native.T
        return out

    small_shapes = [wts[k].shape for k in SMALL_SHARDED]
    small_all = _exchange8(_pack([wts[k] for k in SMALL_SHARDED], 8), False, "gather_small")
    blocks = [group_block(grp) for grp in GATHER_GROUPS]
    first = _gather_chips(blocks[0], "gather_mats")
    later = {}
    for stage, grp, blk in zip(("ffn0", "layer1"), GATHER_GROUPS[1:], blocks[1:]):
        later[stage] = (grp, blk, _copies_start(blk, (N_CHIPS,) + blk.shape, _gather_plan, 3, "gather_" + stage + "_start"))

    def fetch(stage, after):
        grp, blk, started = later[stage]
        own, landed = _copies_wait(started, after, _gather_plan, "gather_" + stage + "_wait")
        return unpack_group(grp, landed, own)

    w = {k: wts[k] for k in REPLICATED}
    w.update(unpack_group(GATHER_GROUPS[0], first, blocks[0]))
    per_chip = [_unpack(small_all[2 * q], small_shapes) for q in range(N_CHIPS)]
    for n, k in enumerate(SMALL_SHARDED):
        w[k] = jnp.concatenate([per_chip[q][n] for q in range(N_CHIPS)], axis=-1)
    for k in ("ev_a_conv_w", "od_conv_w"):
        w[k] = w[k][0]
    w["ev_sinks"] = w["ev_sinks"][0]
    w["mix_norm_pre"] = w["mix_norm_pre"] + sum(later[s][2][4][0, 0] for s in later)

    core = jnp.reshape(ci, (1,)).astype(jnp.int32)
    chip_arr = jnp.reshape(chip, (1,)).astype(jnp.int32)
    per_layer = {}

    def group_grads(group, g):
        gp = jnp.concatenate([(g[k, l] if k.startswith("ffn") else g[k]).reshape(N_CHIPS, -1, PACK_COLS)
                              for k, l, _ in (UNITS[u] for u in group)], axis=1)
        return gp.reshape(N_CHIPS, 2, gp.shape[1] // 2, PACK_COLS)

    def finish(group, half, other):
        red = jnp.concatenate([jnp.where(ci == 0, half, other), jnp.where(ci == 0, other, half)], axis=0)
        off = 0
        for u in group:
            k, l, axis = UNITS[u]
            n = unit_rows[u].shape[0]
            part = red[off:off + n]
            off += n
            per_layer[k, l] = part.T if axis == 2 else part

    chains = {}

    def chain_step(tag, after, g):
        group = REDUCE_GROUPS[tag]
        st = chains.setdefault(tag, {"step": 0})
        step = st["step"]
        st["step"] = step + 1
        if step == 0:
            gp = group_grads(group, g)
            rh = gp.shape[2]
            st["swap"] = _copies_start(gp, (N_CHIPS, rh, PACK_COLS), _swap_plan, N_CHIPS, f"rs_{tag}_swap_start")
            return st["swap"][4][0, 0]
        if step == 1:
            gp, sib = _copies_wait(st["swap"], after, _swap_plan, f"rs_{tag}_swap_wait")
            pair = _rs_add(gp, sib, core, f"rs_{tag}_add")
            st["ici"] = _copies_start(pair, pair.shape, _ici_plan, 3, f"rs_{tag}_ici_start")
            return st["ici"][4][0, 0]
        if step == 2:
            pair, landed = _copies_wait(st["ici"], after, _ici_plan, f"rs_{tag}_ici_wait")
            half = _rs_sum(landed, pair, chip_arr, f"rs_{tag}_sum")
            st["share"] = _copies_start(half, half.shape, _share_plan, 1, f"rs_{tag}_share_start")
            return st["share"][4][0, 0]
        half, other = _copies_wait(st["share"], after, _share_plan, f"rs_{tag}_share_wait")
        finish(group, half, other)
        return 0.0

    schedule = {"layer1_grads": ("layer1",), "ffn0_bwd_done": ("layer1",), "ffn0_grads": ("ffn0",),
                "ev_out_bwd_done": ("layer1", "ffn0"), "ev_att_bwd_done": ("layer1", "ffn0"),
                "ev_mid_bwd_done": ("ffn0",)}

    def emit(place, after, g):
        return sum(chain_step(tag, after, g) for tag in schedule.get(place, ()))

    sq, dx, g = _local_step(x[0], positions[0], loss_target[0], w, fetch, emit)
    loss = lax.psum(0.5 * jnp.sum(sq) / D_MODEL, ("x", "y", "c"))

    grads, deltas, new_m, new_v = {}, {}, {}, {}

    def adamw(k):
        shp = wts[k].shape
        two_d = (-1, shp[-1])
        d, nm, nv = _adamw(wts[k].reshape(two_d), grads[k].reshape(two_d), mom[k].reshape(two_d), var[k].reshape(two_d),
                           "adamw_" + k)
        deltas[k], new_m[k], new_v[k] = d.reshape(shp), nm.reshape(shp), nv.reshape(shp)
        return nv

    def stacked(k):
        return jnp.stack([per_layer[k, l] for l in range(wts[k].shape[0])], axis=0)

    chain_step("ev", None, g)

    small_keys = REPLICATED + SMALL_SHARDED
    full_shapes = [wts[k].shape for k in REPLICATED] + [wts[k].shape[:-1] + (wts[k].shape[-1] * N_CHIPS,) for k in SMALL_SHARDED]
    sm = _exchange8(_pack([g[k] for k in small_keys], 8), True, "reduce_small")
    for k, full in zip(small_keys, _unpack(sm, full_shapes)):
        if k in SMALL_SHARDED:
            n = wts[k].shape[-1]
            full = lax.dynamic_slice_in_dim(full, chip * n, n, axis=full.ndim - 1)
        grads[k] = full
    chain_step("ev", sm, g)

    for k in ("od_w_in", "od_w_out", "ffn_w_up"):
        grads[k] = stacked(k)
        last = adamw(k)
    chain_step("ev", last, g)
    grads["ffn_w_down"] = stacked("ffn_w_down")
    last = adamw("ffn_w_down")
    for k in small_keys:
        last = adamw(k)
    chain_step("ev", last, g)
    for k in ("ev_w_in", "ev_w_out"):
        grads[k] = stacked(k)
        adamw(k)

    return (loss, dx[None], *[grads[k] for k in WEIGHTS], *[deltas[k] for k in WEIGHTS],
            *[new_m[k] for k in WEIGHTS], *[new_v[k] for k in WEIGHTS])
```

```python
import functools

import jax
import jax.numpy as jnp
import numpy as np
from jax import lax
from jax.experimental import pallas as pl
from jax.experimental.pallas import tpu as pltpu

F32 = jnp.float32
BF16 = jnp.bfloat16
MESH = pl.DeviceIdType.MESH

D_MODEL = 1024
HEAD_DIM = 64
A_CH = 512
A_CONV = 31
N_Q_HEADS = 8
WINDOW = 128
ROPE_THETA = 500000.0
ROPE_DIM = 16
D_FF = 2816
RMS_EPS = 1e-6
LN_EPS = 1e-5
ADAM_LR = 0.001
ADAM_B1 = 0.9
ADAM_B2 = 0.999
ADAM_EPS = 1e-08
ADAM_WD = 0.01
ADAM_STEP = 10

LANES = 128
HALO16 = 16
HALO32 = 32
VMEM_LIMIT = 56 * 1024 * 1024
FFN_BWD_VMEM = 60 * 1024 * 1024
N_CHIPS = 4


def _cparams(sem):
    return pltpu.CompilerParams(dimension_semantics=sem, vmem_limit_bytes=VMEM_LIMIT)


def _tile(n, pref):
    if n <= pref:
        return n
    t = (pref // LANES) * LANES
    while t >= LANES:
        if n % t == 0:
            return t
        t -= LANES
    return n


MM_ROWS = 512


def _rms_scale(v):
    return lax.rsqrt(jnp.mean(v * v, axis=-1, keepdims=True) + RMS_EPS)


def _rms_bwd(dy, v, g):
    r = _rms_scale(v)
    nrm = v * r
    dn = dy * g
    return r * (dn - nrm * jnp.mean(dn * nrm, axis=-1, keepdims=True)), jnp.sum(dy * nrm, axis=0, keepdims=True)


def _mm_post(a, w, g, xres, name):
    s, k = a.shape
    d = w.shape[1]
    tm = min(MM_ROWS, s)

    def body(a_ref, w_ref, g_ref, x_ref, m_ref, o_ref):
        mv = jnp.dot(a_ref[...], w_ref[...], preferred_element_type=F32)
        m_ref[...] = mv
        o_ref[...] = x_ref[...] + mv * _rms_scale(mv) * g_ref[...]

    row = pl.BlockSpec((tm, d), lambda i: (i, 0))
    return pl.pallas_call(
        body, name=name, grid=(s // tm,),
        in_specs=[pl.BlockSpec((tm, k), lambda i: (i, 0)), _full((k, d)), _full((1, d)), row],
        out_specs=[row, row],
        out_shape=[jax.ShapeDtypeStruct((s, d), F32), jax.ShapeDtypeStruct((s, d), F32)],
        compiler_params=_cparams(("parallel",)),
    )(a, w, g, xres)


def _mm_post_bwd(dy, m, g, w_t, name):
    s, d = m.shape
    k = w_t.shape[1]
    tm = min(MM_ROWS, s)

    def body(dy_ref, m_ref, g_ref, wt_ref, dm_ref, da_ref, dg_ref):
        @pl.when(pl.program_id(0) == 0)
        def _():
            dg_ref[...] = jnp.zeros_like(dg_ref)

        dm, dg = _rms_bwd(dy_ref[...], m_ref[...], g_ref[...])
        dg_ref[...] += dg
        dmb = dm.astype(BF16)
        dm_ref[...] = dmb
        da_ref[...] = jnp.dot(dmb, wt_ref[...], preferred_element_type=F32)

    row = pl.BlockSpec((tm, d), lambda i: (i, 0))
    return pl.pallas_call(
        body, name=name, grid=(s // tm,),
        in_specs=[row, row, _full((1, d)), _full((d, k))],
        out_specs=[row, pl.BlockSpec((tm, k), lambda i: (i, 0)), _full((1, d))],
        out_shape=[jax.ShapeDtypeStruct((s, d), BF16), jax.ShapeDtypeStruct((s, k), F32),
                   jax.ShapeDtypeStruct((1, d), F32)],
        compiler_params=_cparams(("arbitrary",)),
    )(dy, m, g, w_t)


def _mm_tn(a, b, name):
    s, k = a.shape
    _, n = b.shape
    tk = _tile(k, 1408)
    tn = _tile(n, 1408)
    ts = min(4096 if tk <= 1024 else 2048, s)

    def body(a_ref, b_ref, o_ref):
        @pl.when(pl.program_id(2) == 0)
        def _():
            o_ref[...] = jnp.zeros_like(o_ref)

        o_ref[...] += lax.dot_general(a_ref[...], b_ref[...], (((0,), (0,)), ((), ())),
                                      preferred_element_type=F32)

    return pl.pallas_call(
        body, name=name, grid=(k // tk, n // tn, s // ts),
        in_specs=[pl.BlockSpec((ts, tk), lambda i, j, l: (l, i)), pl.BlockSpec((ts, tn), lambda i, j, l: (l, j))],
        out_specs=pl.BlockSpec((tk, tn), lambda i, j, l: (i, j)),
        out_shape=jax.ShapeDtypeStruct((k, n), F32),
        compiler_params=_cparams(("parallel", "parallel", "arbitrary")),
    )(a, b)


def _cur(tr, w, col=0):
    return pl.BlockSpec((tr, w), lambda i: (i, col))


def _prev(tr, h, w, col=0):
    return pl.BlockSpec((h, w), lambda i: (jnp.maximum(i * (tr // h) - 1, 0), col))


def _next(tr, h, w, nrows, col=0):
    last = nrows // h - 1
    return pl.BlockSpec((h, w), lambda i: (jnp.minimum((i + 1) * (tr // h), last), col))


def _full(shape):
    return pl.BlockSpec(shape, lambda i: tuple(0 for _ in shape))


def _silu_parts(g):
    sig = jax.nn.sigmoid(g)
    return sig, g * sig


FFN_CW = 256
FFN_NBUF = 3


def _conv3_taps(buf, w, off, rows):
    return (w[0:1] * buf[pl.ds(off, rows), :] + w[1:2] * buf[pl.ds(off + 1, rows), :]
            + w[2:3] * buf[pl.ds(off + 2, rows), :])


WHOLE_VMEM = pl.BlockSpec(memory_space=pltpu.VMEM)


def _ffn_fwd(x, g_pre, wu, conv_w, wd, g_post, name, target=None):
    s, d = x.shape
    f2 = wu.shape[1]
    f = f2 // 2
    tr = min(256, s)
    h = HALO16
    cw = FFN_CW
    head = target is not None

    def body(*refs):
        if head:
            (x_ref, gpre_ref, wu_ref, cw_ref, wd_ref, gpost_ref, t_ref, xo_ref, sq_ref, f_ref, h_ref, up_ref, u_ref,
             carry, gbuf, vbuf, facc) = refs
        else:
            (x_ref, gpre_ref, wu_ref, cw_ref, wd_ref, gpost_ref, xo_ref, f_ref, h_ref, up_ref, u_ref,
             carry, gbuf, vbuf, facc) = refs

        @pl.when(pl.program_id(0) == 0)
        def _():
            carry[...] = jnp.zeros_like(carry)
            if head:
                sq_ref[...] = jnp.zeros_like(sq_ref)

        xv = x_ref[...]
        r = lax.rsqrt(jnp.mean(xv * xv, axis=-1, keepdims=True) + RMS_EPS)
        hv = (xv * r * gpre_ref[...]).astype(BF16)
        h_ref[...] = hv
        nchunk = f // cw

        def up_proj(j):
            for buf, base in ((gbuf, 0), (vbuf, f)):
                cs = slice(base + j * cw, base + (j + 1) * cw)
                dst = buf.at[j % FFN_NBUF]
                upc = jnp.dot(hv, wu_ref[:, cs], preferred_element_type=F32)
                up_ref[:, cs] = upc.astype(BF16)
                dst[0:h, :] = carry[:, cs]
                dst[h:h + tr, :] = upc
                carry[:, cs] = upc[tr - h:tr, :]

        def down_proj(j, act):
            part = jnp.dot(act, wd_ref[j * cw:(j + 1) * cw, :], preferred_element_type=F32)
            if j == 0:
                facc[...] = part
            else:
                facc[...] += part

        for j in range(FFN_NBUF - 1):
            up_proj(j)
        pending = None
        for j in range(nchunk):
            cg = slice(j * cw, (j + 1) * cw)
            cv = slice(f + j * cw, f + (j + 1) * cw)
            if j + FFN_NBUF - 1 < nchunk:
                up_proj(j + FFN_NBUF - 1)
            if pending is not None:
                down_proj(*pending)
            g = _conv3_taps(gbuf.at[j % FFN_NBUF], cw_ref[:, cg], h - 2, tr)
            v = _conv3_taps(vbuf.at[j % FFN_NBUF], cw_ref[:, cv], h - 2, tr)
            u_ref[:, cg] = g.astype(BF16)
            u_ref[:, cv] = v.astype(BF16)
            act = (g * jax.nn.sigmoid(g) * v).astype(BF16)
            pending = (j, act)
        down_proj(*pending)
        fv = facc[...]
        f_ref[...] = fv
        r2 = lax.rsqrt(jnp.mean(fv * fv, axis=-1, keepdims=True) + RMS_EPS)
        xo = xv + fv * r2 * gpost_ref[...]
        if head:
            err = xo - t_ref[...]
            xo_ref[...] = err * (1.0 / d)
            sq_ref[...] += jnp.sum(err * err, axis=0, keepdims=True)
        else:
            xo_ref[...] = xo

    row = _cur(tr, d)
    wide = _cur(tr, f2)
    vec = _full((1, d))
    out_specs = [row] + ([vec] if head else []) + [row, row, wide, wide]
    out_shape = ([jax.ShapeDtypeStruct((s, d), F32)] + ([jax.ShapeDtypeStruct((1, d), F32)] if head else [])
                 + [jax.ShapeDtypeStruct((s, d), F32), jax.ShapeDtypeStruct((s, d), BF16),
                    jax.ShapeDtypeStruct((s, f2), BF16), jax.ShapeDtypeStruct((s, f2), BF16)])
    return pl.pallas_call(
        body, name=name, grid=(s // tr,),
        in_specs=[row, vec, WHOLE_VMEM, _full((3, f2)), WHOLE_VMEM, vec] + ([row] if head else []),
        out_specs=out_specs, out_shape=out_shape,
        scratch_shapes=[pltpu.VMEM((h, f2), F32), pltpu.VMEM((FFN_NBUF, h + tr, cw), F32),
                        pltpu.VMEM((FFN_NBUF, h + tr, cw), F32), pltpu.VMEM((tr, d), F32)],
        compiler_params=_cparams(("arbitrary",)),
    )(*((x, g_pre, wu, conv_w, wd, g_post) + ((target,) if head else ())))


def _ffn_bwd(dxo, fout, x, up, u, g_pre, g_post, wd_t, wu_t, conv_w, name):
    s, d = x.shape
    f2 = up.shape[1]
    f = f2 // 2
    tr = min(256, s)
    nt = s // tr
    h = HALO16
    cw = FFN_CW

    def body(dy_ref, f_ref, x_ref, up_ref, u_ref, gpre_ref, gpost_ref, wdt_ref, wut_ref, cw_ref,
             dx_ref, dup_ref, act_ref, df_ref, dcw_ref, dgpost_ref, dgpre_ref, carry, dgbuf, dvbuf, dhacc):
        @pl.when(pl.program_id(0) == 0)
        def _():
            carry[...] = jnp.zeros_like(carry)
            dcw_ref[...] = jnp.zeros_like(dcw_ref)
            dgpost_ref[...] = jnp.zeros_like(dgpost_ref)
            dgpre_ref[...] = jnp.zeros_like(dgpre_ref)

        dy = dy_ref[...]
        fv = f_ref[...]
        r = lax.rsqrt(jnp.mean(fv * fv, axis=-1, keepdims=True) + RMS_EPS)
        nrm = fv * r
        dn = dy * gpost_ref[...]
        dfv = (r * (dn - nrm * jnp.mean(dn * nrm, axis=-1, keepdims=True))).astype(BF16)
        dgpost_ref[...] += jnp.sum(dy * nrm, axis=0, keepdims=True)
        df_ref[...] = dfv
        nchunk = f // cw

        def dh_part(dupb, cs, first):
            part = jnp.dot(dupb, wut_ref[cs, :], preferred_element_type=F32)
            if first:
                dhacc[...] = part
            else:
                dhacc[...] += part

        def dact_of(j):
            return jnp.dot(dfv, wdt_ref[:, j * cw:(j + 1) * cw], preferred_element_type=F32)

        ahead = [dact_of(0)]
        for j in range(nchunk):
            ch = slice(j * cw, (j + 1) * cw)
            cg = ch
            cv = slice(f + j * cw, f + (j + 1) * cw)
            dact = ahead.pop(0)
            if j + 1 < nchunk:
                ahead.append(dact_of(j + 1))
            g = u_ref[:, cg].astype(F32)
            v = u_ref[:, cv].astype(F32)
            sig, sil = _silu_parts(g)
            act_ref[:, ch] = (sil * v).astype(BF16)
            du_g = dact * v * (sig * (1.0 + g * (1.0 - sig)))
            du_v = dact * sil
            for k, (dbuf, du, cs) in enumerate(((dgbuf.at[j % FFN_NBUF], du_g, cg), (dvbuf.at[j % FFN_NBUF], du_v, cv))):
                dbuf[0:tr, :] = du
                dbuf[tr:tr + h, :] = carry[:, cs]
                carry[:, cs] = du[0:h, :]
                w = cw_ref[:, cs]
                xin = up_ref[:, cs].astype(F32)
                acc = None
                for sh in range(3):
                    dsh = dbuf[pl.ds(sh, tr), :]
                    term = w[2 - sh:3 - sh] * dsh
                    acc = term if acc is None else acc + term
                    dcw_ref[2 - sh:3 - sh, cs] += jnp.sum(xin * dsh, axis=0, keepdims=True)
                dupb = acc.astype(BF16)
                dup_ref[:, cs] = dupb
                dh_part(dupb, cs, j == 0 and k == 0)
        dh = dhacc[...]
        xv = x_ref[...]
        r1 = lax.rsqrt(jnp.mean(xv * xv, axis=-1, keepdims=True) + RMS_EPS)
        n1 = xv * r1
        dn1 = dh * gpre_ref[...]
        dx_ref[...] = dy + r1 * (dn1 - n1 * jnp.mean(dn1 * n1, axis=-1, keepdims=True))
        dgpre_ref[...] += jnp.sum(dh * n1, axis=0, keepdims=True)

    def rev(w):
        return pl.BlockSpec((tr, w), lambda i: (nt - 1 - i, 0))

    vec = _full((1, d))
    return pl.pallas_call(
        body, name=name, grid=(nt,),
        in_specs=[rev(d), rev(d), rev(d), rev(f2), rev(f2), vec, vec, WHOLE_VMEM, WHOLE_VMEM, _full((3, f2))],
        out_specs=[rev(d), rev(f2), rev(f), rev(d), _full((3, f2)), vec, vec],
        out_shape=[jax.ShapeDtypeStruct((s, d), F32), jax.ShapeDtypeStruct((s, f2), BF16),
                   jax.ShapeDtypeStruct((s, f), BF16), jax.ShapeDtypeStruct((s, d), BF16),
                   jax.ShapeDtypeStruct((3, f2), F32), jax.ShapeDtypeStruct((1, d), F32),
                   jax.ShapeDtypeStruct((1, d), F32)],
        scratch_shapes=[pltpu.VMEM((h, f2), F32), pltpu.VMEM((FFN_NBUF, tr + h, cw), F32),
                        pltpu.VMEM((FFN_NBUF, tr + h, cw), F32), pltpu.VMEM((tr, d), F32)],
        compiler_params=pltpu.CompilerParams(dimension_semantics=("arbitrary",), vmem_limit_bytes=FFN_BWD_VMEM),
    )(dxo, fout, x, up, u, g_pre, g_post, wd_t, wu_t, conv_w)


def _od_fwd(x, g, w, conv_w, w_out, g_post, name):
    s, d = x.shape
    d3 = w.shape[1]
    tr = min(512, s)
    h = HALO16
    cw = FFN_CW
    nchunk = d // cw

    def body(x_ref, g_ref, w_ref, cw_ref, wout_ref, gpost_ref, h_ref, z_ref, y_ref, m_ref, xo_ref, carry, buf, macc):
        @pl.when(pl.program_id(0) == 0)
        def _():
            carry[...] = jnp.zeros_like(carry)

        xv = x_ref[...]
        hv = (xv * _rms_scale(xv) * g_ref[...]).astype(BF16)
        h_ref[...] = hv

        def project(j):
            out = []
            for part in range(3):
                cs = slice(part * d + j * cw, part * d + (j + 1) * cw)
                zc = jnp.dot(hv, w_ref[:, cs], preferred_element_type=F32).astype(BF16)
                z_ref[:, cs] = zc
                out.append(zc.astype(F32))
            return out

        ahead = [project(0), project(1)]
        for j in range(nchunk):
            cb = slice(j * cw, (j + 1) * cw)
            bval, cval, uval = ahead.pop(0)
            if j + 2 < nchunk:
                ahead.append(project(j + 2))
            bf = buf.at[j % FFN_NBUF]
            cu = cval * uval
            bf[0:h, :] = carry[:, cb]
            bf[h:h + tr, :] = cu
            carry[:, cb] = cu[tr - h:tr, :]
            yv = (bval * _conv3_taps(bf, cw_ref[:, cb], h - 2, tr)).astype(BF16)
            y_ref[:, cb] = yv
            part = jnp.dot(yv, wout_ref[cb, :], preferred_element_type=F32)
            if j == 0:
                macc[...] = part
            else:
                macc[...] += part
        mv = macc[...]
        m_ref[...] = mv
        xo_ref[...] = xv + mv * _rms_scale(mv) * gpost_ref[...]

    row = _cur(tr, d)
    vec = _full((1, d))
    return pl.pallas_call(
        body, name=name, grid=(s // tr,),
        in_specs=[row, vec, WHOLE_VMEM, _full((3, d)), WHOLE_VMEM, vec],
        out_specs=[row, _cur(tr, d3), row, row, row],
        out_shape=[jax.ShapeDtypeStruct((s, d), BF16), jax.ShapeDtypeStruct((s, d3), BF16),
                   jax.ShapeDtypeStruct((s, d), BF16), jax.ShapeDtypeStruct((s, d), F32),
                   jax.ShapeDtypeStruct((s, d), F32)],
        scratch_shapes=[pltpu.VMEM((h, d), F32), pltpu.VMEM((FFN_NBUF, h + tr, cw), F32), pltpu.VMEM((tr, d), F32)],
        compiler_params=_cparams(("arbitrary",)),
    )(x, g, w, conv_w, w_out, g_post)


def _od_bwd(dxo, m, g_post, w_out_t, z, conv_w, w_t, x, g, name):
    s, d3 = z.shape
    d = d3 // 3
    tr = min(512, s)
    nt = s // tr
    h = HALO16
    cw = FFN_CW
    ext = tr + h
    nchunk = d // cw

    def body(dxo_ref, m_ref, gpost_ref, wot_ref, z_ref, zp_ref, w_ref, wt_ref, x_ref, g_ref,
             dm_ref, o_ref, dw_ref, dx_ref, dgpost_ref, dg_ref, carry, buf, dbuf, dhacc):
        i = pl.program_id(0)
        row0 = i == nt - 1

        @pl.when(i == 0)
        def _():
            carry[...] = jnp.zeros_like(carry)
            dw_ref[...] = jnp.zeros_like(dw_ref)
            dg_ref[...] = jnp.zeros_like(dg_ref)
            dgpost_ref[...] = jnp.zeros_like(dgpost_ref)

        dyo = dxo_ref[...]
        dm, dgp = _rms_bwd(dyo, m_ref[...], gpost_ref[...])
        dgpost_ref[...] += dgp
        dmb = dm.astype(BF16)
        dm_ref[...] = dmb

        def dy_of(j):
            return jnp.dot(dmb, wot_ref[:, j * cw:(j + 1) * cw], preferred_element_type=F32)

        ahead = [dy_of(0)]
        started = False
        for j in range(nchunk):
            cb = slice(j * cw, (j + 1) * cw)
            cc = slice(d + j * cw, d + (j + 1) * cw)
            cu = slice(2 * d + j * cw, 2 * d + (j + 1) * cw)
            dyv = ahead.pop(0)
            if j + 1 < nchunk:
                ahead.append(dy_of(j + 1))
            bf = buf.at[j % FFN_NBUF]
            db = dbuf.at[j % FFN_NBUF]
            w = w_ref[:, cb]
            cval = z_ref[:, cc].astype(F32)
            uval = z_ref[:, cu].astype(F32)
            bf[0:h, :] = jnp.where(row0, 0.0, zp_ref[:, cc].astype(F32) * zp_ref[:, cu].astype(F32))
            bf[h:h + tr, :] = cval * uval
            k = _conv3_taps(bf, w, h - 2, tr)
            dk = dyv * z_ref[:, cb].astype(F32)
            db[0:tr, :] = dk
            db[tr:ext, :] = carry[:, cb]
            carry[:, cb] = dk[0:h, :]
            dcu = w[2:3] * db[pl.ds(0, tr), :] + w[1:2] * db[pl.ds(1, tr), :] + w[0:1] * db[pl.ds(2, tr), :]
            for t in range(3):
                dw_ref[t:t + 1, cb] += jnp.sum(dk * bf[pl.ds(h - 2 + t, tr), :], axis=0, keepdims=True)
            for cs, val in ((cb, dyv * k), (cc, dcu * uval), (cu, dcu * cval)):
                piece = val.astype(BF16)
                o_ref[:, cs] = piece
                part = jnp.dot(piece, wt_ref[cs, :], preferred_element_type=F32)
                if started:
                    dhacc[...] += part
                else:
                    dhacc[...] = part
                    started = True
        dx, dg = _rms_bwd(dhacc[...], x_ref[...], g_ref[...])
        dg_ref[...] += dg
        dx_ref[...] = dyo + dx

    def rev(w):
        return pl.BlockSpec((tr, w), lambda i: (nt - 1 - i, 0))

    prev = pl.BlockSpec((h, d3), lambda i: (jnp.maximum((nt - 1 - i) * (tr // h) - 1, 0), 0))
    vec = _full((1, d))
    return pl.pallas_call(
        body, name=name, grid=(nt,),
        in_specs=[rev(d), rev(d), vec, WHOLE_VMEM, rev(d3), prev, _full((3, d)), WHOLE_VMEM, rev(d), vec],
        out_specs=[rev(d), rev(d3), _full((3, d)), rev(d), vec, vec],
        out_shape=[jax.ShapeDtypeStruct((s, d), BF16), jax.ShapeDtypeStruct((s, d3), BF16),
                   jax.ShapeDtypeStruct((3, d), F32), jax.ShapeDtypeStruct((s, d), F32),
                   jax.ShapeDtypeStruct((1, d), F32), jax.ShapeDtypeStruct((1, d), F32)],
        scratch_shapes=[pltpu.VMEM((h, d), F32), pltpu.VMEM((FFN_NBUF, h + tr, cw), F32),
                        pltpu.VMEM((FFN_NBUF, ext, cw), F32), pltpu.VMEM((tr, d), F32)],
        compiler_params=_cparams(("arbitrary",)),
    )(dxo, m, g_post, w_out_t, z, z, conv_w, w_t, x, g)


Q0 = 2 * A_CH
K0 = Q0 + N_Q_HEADS * HEAD_DIM
V0 = K0 + 2 * HEAD_DIM
EVEN_IN = V0 + 2 * HEAD_DIM


def _rope_tables(positions):
    half = ROPE_DIM // 2
    inv_freq = ROPE_THETA ** (-(jnp.arange(half, dtype=F32) * 2.0 / ROPE_DIM))
    ang = positions.astype(F32)[:, None] * inv_freq
    cs = jnp.concatenate([jnp.cos(ang), jnp.sin(ang)], axis=1)
    spread = np.zeros((2 * half, 3 * LANES), np.float32)
    const = np.zeros((1, 3 * LANES), np.float32)
    for lane in range(3 * LANES):
        dim, part = lane % HEAD_DIM, lane // LANES
        if part == 0:
            if dim < ROPE_DIM:
                spread[dim % half, lane] = 1.0
            else:
                const[0, lane] = 1.0
        elif part == 1 and half <= dim < ROPE_DIM:
            spread[half + dim - half, lane] = 1.0
        elif part == 2 and dim < half:
            spread[half + dim, lane] = -1.0
    return jnp.dot(cs, jnp.asarray(spread), precision=lax.Precision.HIGHEST) + jnp.asarray(const)


def _rope_fwd(x, tab):
    c, sa, sb = tab[:, 0:LANES], tab[:, LANES:2 * LANES], tab[:, 2 * LANES:3 * LANES]
    return x * c + pltpu.roll(x, 8, 1) * sa + pltpu.roll(x, LANES - 8, 1) * sb


def _rope_bwd(dy, tab):
    c, sa, sb = tab[:, 0:LANES], tab[:, LANES:2 * LANES], tab[:, 2 * LANES:3 * LANES]
    return dy * c + pltpu.roll(dy * sa, LANES - 8, 1) + pltpu.roll(dy * sb, 8, 1)


def _ln_fwd(c, g, b):
    mu = jnp.mean(c, axis=-1, keepdims=True)
    xc = c - mu
    r = lax.rsqrt(jnp.mean(xc * xc, axis=-1, keepdims=True) + LN_EPS)
    nrm = xc * r
    return nrm, r, nrm * g + b


def _phase_fill(buf, ph, rows):
    for k in range(1, 8):
        ph[k - 1, 0:rows - 8, :] = buf[pl.ds(k, rows - 8), :]


def _phase_rows(buf, ph, off, n, cs):
    k = off % 8
    src = buf if k == 0 else ph.at[k - 1]
    return src[pl.ds(off - k, n), cs]


def _ev_in_fwd(x, g_pre, w_in, tab, conv_w, conv_b, ln_g, ln_b, name):
    s, d = x.shape
    tr = min(512, s)
    h = HALO32
    cw = LANES
    pw = 2 * LANES

    def body(x_ref, gpre_ref, win_ref, tab_ref, w_ref, b_ref, g_ref, lb_ref, h_ref, z_ref, c_ref, a_ref, qkv_ref,
             gbuf, cbuf, gph, carry):
        @pl.when(pl.program_id(0) == 0)
        def _():
            carry[...] = jnp.zeros_like(carry)

        xv = x_ref[...]
        hv = (xv * _rms_scale(xv) * gpre_ref[...]).astype(BF16)
        h_ref[...] = hv

        def project(lo_col, hi_col):
            for c0 in range(lo_col, hi_col, pw):
                cs = slice(c0, c0 + pw)
                z_ref[:, cs] = jnp.dot(hv, win_ref[:, cs], preferred_element_type=F32).astype(BF16)

        project(0, 2 * A_CH)
        glu = z_ref[:, 0:A_CH].astype(F32) * jax.nn.sigmoid(z_ref[:, A_CH:2 * A_CH].astype(F32))
        project(2 * A_CH, EVEN_IN)
        gbuf[0:h, :] = carry[...]
        gbuf[h:h + tr, :] = glu
        carry[...] = glu[tr - h:tr, :]
        _phase_fill(gbuf, gph, h + tr)
        for j in range(A_CH // cw):
            cs = slice(j * cw, (j + 1) * cw)
            acc = jnp.broadcast_to(b_ref[:, cs], (tr, cw))
            for t in range(A_CONV):
                acc = acc + w_ref[t:t + 1, cs] * _phase_rows(gbuf, gph, h - (A_CONV - 1) + t, tr, cs)
            cbuf[:, cs] = acc
        c = cbuf[...]
        c_ref[...] = c.astype(BF16)
        _, _, l = _ln_fwd(c, g_ref[...], lb_ref[...])
        a_ref[...] = (l * jax.nn.sigmoid(l)).astype(BF16)
        tab_v = tab_ref[...]
        for p in range(4):
            xq = z_ref[:, Q0 + p * LANES:Q0 + (p + 1) * LANES].astype(F32)
            qkv_ref[:, p * LANES:(p + 1) * LANES] = _rope_fwd(xq, tab_v).astype(BF16)
        lane = lax.broadcasted_iota(jnp.int32, (tr, LANES), 1)
        lo = lane < HEAD_DIM
        kr = _rope_fwd(z_ref[:, K0:K0 + LANES].astype(F32), tab_v)
        vr = z_ref[:, V0:V0 + LANES].astype(F32)
        for base, val in ((4 * LANES, kr), (6 * LANES, vr)):
            sw = pltpu.roll(val, HEAD_DIM, 1)
            qkv_ref[:, base:base + LANES] = jnp.where(lo, val, sw).astype(BF16)
            qkv_ref[:, base + LANES:base + 2 * LANES] = jnp.where(lo, sw, val).astype(BF16)

    return pl.pallas_call(
        body, name=name, grid=(s // tr,),
        in_specs=[_cur(tr, d), _full((1, d)), WHOLE_VMEM, _cur(tr, 3 * LANES), _full((A_CONV, A_CH)),
                  _full((1, A_CH)), _full((1, A_CH)), _full((1, A_CH))],
        out_specs=[_cur(tr, d), _cur(tr, EVEN_IN), _cur(tr, A_CH), _cur(tr, A_CH), _cur(tr, 2 * A_CH)],
        out_shape=[jax.ShapeDtypeStruct((s, d), BF16), jax.ShapeDtypeStruct((s, EVEN_IN), BF16),
                   jax.ShapeDtypeStruct((s, A_CH), BF16), jax.ShapeDtypeStruct((s, A_CH), BF16),
                   jax.ShapeDtypeStruct((s, 2 * A_CH), BF16)],
        scratch_shapes=[pltpu.VMEM((h + tr, A_CH), F32), pltpu.VMEM((tr, A_CH), F32),
                        pltpu.VMEM((7, h + tr, A_CH), F32), pltpu.VMEM((h, A_CH), F32)],
        compiler_params=_cparams(("arbitrary",)),
    )(x, g_pre, w_in, tab, conv_w, conv_b, ln_g, ln_b)


def _ev_mid_bwd(dcat, c, z, dq, dkv, tab, conv_w, ln_g, ln_b, w_t, x, g_pre, res, name):
    s = z.shape[0]
    tr = min(512, s)
    h = HALO32
    cw = LANES
    ext = tr + h

    def body(da_ref, dan_ref, c_ref, cn_ref, z_ref, dq_ref, dkv_ref, tab_ref, w_ref, g_ref, lb_ref,
             wt_ref, x_ref, gpre_ref, res_ref, dz_ref, dw_ref, dvec_ref, dx_ref, dg_ref, dcbuf, dcph, dhacc):
        i = pl.program_id(0)
        first = i == 0
        last = i == pl.num_programs(0) - 1

        @pl.when(first)
        def _():
            dw_ref[...] = jnp.zeros_like(dw_ref)
            dvec_ref[...] = jnp.zeros_like(dvec_ref)
            dg_ref[...] = jnp.zeros_like(dg_ref)

        started = []

        def dh_part(cs):
            part = jnp.dot(dz_ref[:, cs], wt_ref[cs, :], preferred_element_type=F32)
            if started:
                dhacc[...] += part
            else:
                dhacc[...] = part
                started.append(True)

        tab_v = tab_ref[...]
        for p in range(4):
            cs = slice(p * LANES, (p + 1) * LANES)
            dz_ref[:, Q0 + p * LANES:Q0 + (p + 1) * LANES] = _rope_bwd(dq_ref[:, cs], tab_v).astype(BF16)
        lane = lax.broadcasted_iota(jnp.int32, (tr, LANES), 1)
        lo = lane < HEAD_DIM

        def fold(base):
            p0 = dkv_ref[:, base:base + LANES]
            p1 = dkv_ref[:, base + LANES:base + 2 * LANES]
            s0 = p0 + pltpu.roll(p0, HEAD_DIM, 1)
            s1 = p1 + pltpu.roll(p1, HEAD_DIM, 1)
            return jnp.where(lo, s0, s1)

        dz_ref[:, K0:K0 + LANES] = _rope_bwd(fold(0), tab_v).astype(BF16)
        dz_ref[:, V0:V0 + LANES] = fold(2 * LANES).astype(BF16)
        dh_part(slice(Q0, EVEN_IN))

        gv = g_ref[...]

        def ln_silu_bwd(cv, dav):
            nrm, r, l = _ln_fwd(cv, gv, lb_ref[...])
            sig = jax.nn.sigmoid(l)
            dl = dav * (sig * (1.0 + l * (1.0 - sig)))
            dn = dl * gv
            dc = r * (dn - jnp.mean(dn, axis=-1, keepdims=True) - nrm * jnp.mean(dn * nrm, axis=-1, keepdims=True))
            return dc, dl, nrm

        dc, dl, nrm = ln_silu_bwd(c_ref[...].astype(F32), da_ref[...])
        dcn, _, _ = ln_silu_bwd(cn_ref[...].astype(F32), dan_ref[...])
        dcbuf[0:tr, :] = dc
        dcbuf[tr:ext, :] = jnp.where(last, 0.0, dcn)
        dvec_ref[0:1, :] += jnp.sum(dc, axis=0, keepdims=True)
        dvec_ref[1:2, :] += jnp.sum(dl * nrm, axis=0, keepdims=True)
        dvec_ref[2:3, :] += jnp.sum(dl, axis=0, keepdims=True)

        _phase_fill(dcbuf, dcph, ext)
        a_lin = z_ref[:, 0:A_CH].astype(F32)
        sig_g = jax.nn.sigmoid(z_ref[:, A_CH:2 * A_CH].astype(F32))
        glu = a_lin * sig_g
        for j in range(A_CH // cw):
            cs = slice(j * cw, (j + 1) * cw)
            gluj = glu[:, cs]
            acc = jnp.zeros((tr, cw), F32)
            for t in range(A_CONV):
                dsh = _phase_rows(dcbuf, dcph, A_CONV - 1 - t, tr, cs)
                acc = acc + w_ref[t:t + 1, cs] * dsh
                dw_ref[t:t + 1, cs] += jnp.sum(gluj * dsh, axis=0, keepdims=True)
            dz_ref[:, cs] = (acc * sig_g[:, cs]).astype(BF16)
            dz_ref[:, A_CH + j * cw:A_CH + (j + 1) * cw] = (
                acc * a_lin[:, cs] * sig_g[:, cs] * (1.0 - sig_g[:, cs])).astype(BF16)
            if j % 2 == 1:
                dh_part(slice((j - 1) * cw, (j + 1) * cw))
                dh_part(slice(A_CH + (j - 1) * cw, A_CH + (j + 1) * cw))

        dx, dg = _rms_bwd(dhacc[...], x_ref[...], gpre_ref[...])
        dg_ref[...] += dg
        dx_ref[...] = res_ref[...] + dx

    row = _cur(tr, D_MODEL)
    vec = _full((1, D_MODEL))
    return pl.pallas_call(
        body, name=name, grid=(s // tr,),
        in_specs=[_cur(tr, A_CH), _next(tr, h, A_CH, s), _cur(tr, A_CH), _next(tr, h, A_CH, s),
                  _cur(tr, EVEN_IN), _cur(tr, A_CH), _cur(tr, A_CH), _cur(tr, 3 * LANES),
                  _full((A_CONV, A_CH)), _full((1, A_CH)), _full((1, A_CH)), WHOLE_VMEM, row, vec, row],
        out_specs=[_cur(tr, EVEN_IN), _full((A_CONV, A_CH)), _full((8, A_CH)), row, vec],
        out_shape=[jax.ShapeDtypeStruct((s, EVEN_IN), BF16), jax.ShapeDtypeStruct((A_CONV, A_CH), F32),
                   jax.ShapeDtypeStruct((8, A_CH), F32), jax.ShapeDtypeStruct((s, D_MODEL), F32),
                   jax.ShapeDtypeStruct((1, D_MODEL), F32)],
        scratch_shapes=[pltpu.VMEM((ext, A_CH), F32), pltpu.VMEM((7, ext, A_CH), F32),
                        pltpu.VMEM((tr, D_MODEL), F32)],
        compiler_params=_cparams(("arbitrary",)),
    )(dcat, dcat, c, c, z, dq, dkv, tab, conv_w, ln_g, ln_b, w_t, x, g_pre, res)


NT = (((1,), (1,)), ((), ()))
TN = (((0,), (0,)), ((), ()))
QB = WINDOW
SCALE = HEAD_DIM ** -0.5
ATT_AHEAD = 2


def _att_scores(q2m, kwin):
    return lax.dot_general(q2m, kwin, NT, preferred_element_type=F32)


def _att_probs(raw, sink, mask):
    sc = jnp.where(mask, raw * SCALE, -jnp.inf)
    mx = jnp.maximum(jnp.max(sc, axis=-1, keepdims=True), sink)
    p = jnp.exp(sc - mx)
    ps = jnp.exp(sink - mx)
    inv = 1.0 / (jnp.sum(p, axis=-1, keepdims=True) + ps)
    return p * inv, ps * inv


def _att_mask(i):
    r = lax.broadcasted_iota(jnp.int32, (QB, 2 * QB), 0)
    kc = lax.broadcasted_iota(jnp.int32, (QB, 2 * QB), 1)
    diff = r + QB - kc
    return (diff >= 0) & (diff < WINDOW) & ((kc >= QB) | (i > 0))


def _half_masks(dtype):
    lane = lax.broadcasted_iota(jnp.int32, (1, LANES), 1)
    return (lane < HEAD_DIM).astype(dtype), (lane >= HEAD_DIM).astype(dtype)


def _att_fwd(qkv, a, sinks, name):
    s = qkv.shape[0]
    nq = 2 if s % (2 * QB) == 0 else 1
    rows = nq * QB
    nb = s // rows

    def body(sink_ref, qkv_ref, kvp_ref, a_ref, o_ref):
        i = pl.program_id(0)
        mlo, mhi = _half_masks(BF16)
        o_ref[:, 0:A_CH] = a_ref[...]

        def window(b, col):
            cur = qkv_ref[b * QB:(b + 1) * QB, A_CH + col * LANES:A_CH + (col + 1) * LANES]
            if b == 0:
                before = kvp_ref[:, col * LANES:(col + 1) * LANES]
            else:
                before = qkv_ref[(b - 1) * QB:b * QB, A_CH + col * LANES:A_CH + (col + 1) * LANES]
            return jnp.concatenate([before, cur], axis=0)

        def raw_scores(b, p):
            q2 = qkv_ref[b * QB:(b + 1) * QB, p * LANES:(p + 1) * LANES]
            kwin = window(b, p // 2)
            return _att_scores(q2 * mlo, kwin), _att_scores(q2 * mhi, kwin)

        units = [(b, p) for p in range(4) for b in range(nq)]
        ahead = [raw_scores(b, p) for b, p in units]
        masks = [_att_mask(i) if b == 0 else _att_mask(1) for b in range(nq)]
        for (b, p), (raw_e, raw_o) in zip(units, ahead):
            vwin = window(b, 2 + p // 2)
            pe, _ = _att_probs(raw_e, sink_ref[2 * p], masks[b])
            po, _ = _att_probs(raw_o, sink_ref[2 * p + 1], masks[b])
            o = (jnp.dot(pe.astype(BF16), vwin * mlo, preferred_element_type=F32)
                 + jnp.dot(po.astype(BF16), vwin * mhi, preferred_element_type=F32))
            o_ref[b * QB:(b + 1) * QB, A_CH + p * LANES:A_CH + (p + 1) * LANES] = o.astype(BF16)

    grid_spec = pltpu.PrefetchScalarGridSpec(
        num_scalar_prefetch=1, grid=(nb,),
        in_specs=[pl.BlockSpec((rows, 2 * A_CH), lambda i, sk: (i, 0)),
                  pl.BlockSpec((QB, A_CH), lambda i, sk: (jnp.maximum(nq * i - 1, 0), 1)),
                  pl.BlockSpec((rows, A_CH), lambda i, sk: (i, 0))],
        out_specs=pl.BlockSpec((rows, 2 * A_CH), lambda i, sk: (i, 0)),
    )
    return pl.pallas_call(
        body, name=name, grid_spec=grid_spec,
        out_shape=jax.ShapeDtypeStruct((s, 2 * A_CH), BF16),
        compiler_params=_cparams(("parallel",)),
    )(sinks, qkv, qkv, a)


def _att_bwd(qkv, dcat, sinks, name):
    s = qkv.shape[0]
    nb = s // QB

    def body(sink_ref, qkv_ref, kvp_ref, do_ref, dq_ref, dkv_ref, ds_ref, carry):
        i = pl.program_id(0)

        @pl.when(i == 0)
        def _():
            ds_ref[...] = jnp.zeros_like(ds_ref)
            carry[...] = jnp.zeros_like(carry)

        @pl.when(i < nb)
        def _():
            mask = _att_mask(i)
            mlo, mhi = _half_masks(BF16)
            dwin = [jnp.zeros((2 * QB, LANES), F32) for _ in range(4)]

            def window(col):
                return jnp.concatenate([kvp_ref[:, col * LANES:(col + 1) * LANES],
                                        qkv_ref[:, A_CH + col * LANES:A_CH + (col + 1) * LANES]], axis=0)

            def first_products(n):
                p, hm = n // 2, (mlo, mhi)[n % 2]
                qm = qkv_ref[:, p * LANES:(p + 1) * LANES] * hm
                dom = do_ref[:, p * LANES:(p + 1) * LANES].astype(BF16) * hm
                kwin, vwin = window(p // 2), window(2 + p // 2)
                return (qm, dom, kwin * hm, _att_scores(qm, kwin),
                        lax.dot_general(dom, vwin, NT, preferred_element_type=F32))

            ahead = [first_products(n) for n in range(ATT_AHEAD)]
            dq2 = None
            for n in range(N_Q_HEADS):
                g = n // 4
                qm, dom, kwm, raw, dp = ahead.pop(0)
                if n + ATT_AHEAD < N_Q_HEADS:
                    ahead.append(first_products(n + ATT_AHEAD))
                prob, psink = _att_probs(raw, sink_ref[n], mask)
                delta = jnp.sum(prob * dp, axis=-1, keepdims=True)
                dsc = (prob * (dp - delta) * SCALE).astype(BF16)
                ds_ref[n:n + 1, :] += jnp.broadcast_to(jnp.sum(-psink * delta, axis=0, keepdims=True), (1, LANES))
                part = jnp.dot(dsc, kwm, preferred_element_type=F32)
                dq2 = part if n % 2 == 0 else dq2 + part
                dwin[g] = dwin[g] + lax.dot_general(dsc, qm, TN, preferred_element_type=F32)
                dwin[2 + g] = dwin[2 + g] + lax.dot_general(prob.astype(BF16), dom, TN, preferred_element_type=F32)
                if n % 2 == 1:
                    dq_ref[:, (n // 2) * LANES:(n // 2 + 1) * LANES] = dq2
            for n in range(4):
                cs = slice(n * LANES, (n + 1) * LANES)
                dkv_ref[:, cs] = carry[:, cs] + dwin[n][0:QB, :]
                carry[:, cs] = dwin[n][QB:2 * QB, :]

        @pl.when(i == nb)
        def _():
            dkv_ref[...] = carry[...]

    grid_spec = pltpu.PrefetchScalarGridSpec(
        num_scalar_prefetch=1, grid=(nb + 1,),
        in_specs=[pl.BlockSpec((QB, 2 * A_CH), lambda i, sk: (jnp.minimum(i, nb - 1), 0)),
                  pl.BlockSpec((QB, A_CH), lambda i, sk: (jnp.maximum(jnp.minimum(i, nb - 1) - 1, 0), 1)),
                  pl.BlockSpec((QB, A_CH), lambda i, sk: (jnp.minimum(i, nb - 1), 1))],
        out_specs=[pl.BlockSpec((QB, A_CH), lambda i, sk: (jnp.minimum(i, nb - 1), 0)),
                   pl.BlockSpec((QB, A_CH), lambda i, sk: (jnp.maximum(i - 1, 0), 0)),
                   pl.BlockSpec((8, LANES), lambda i, sk: (0, 0))],
        scratch_shapes=[pltpu.VMEM((QB, A_CH), F32)],
    )
    return pl.pallas_call(
        body, name=name, grid_spec=grid_spec,
        out_shape=[jax.ShapeDtypeStruct((s, A_CH), F32), jax.ShapeDtypeStruct((s, A_CH), F32),
                   jax.ShapeDtypeStruct((8, LANES), F32)],
        compiler_params=_cparams(("arbitrary",)),
    )(sinks, qkv, qkv, dcat)


def _local_step(x, positions, target, w, fetch=None, emit=None):
    row = lambda a, i: a[i:i + 1]
    tab = _rope_tables(positions)
    g = {}

    def ffn_fwd(xin, i, tgt=None):
        outs = _ffn_fwd(xin, row(w["ffn_norm_pre"], i), w["ffn_w_up", i], w["ffn_conv_w"][i],
                        w["ffn_w_down", i], row(w["ffn_norm_post"], i), f"ffn{i}_fwd", tgt)
        f, h, up, u = outs[-4:]
        return outs[:-4], (xin, f, h, up, u)

    def point(name, after):
        return emit(name, after, g) if emit is not None else 0.0

    def ffn_bwd(dxout, saved, i, tok):
        xin, f, h, up, u = saved
        dxin, dup, act, df, d_cw, dg_post, dg_pre = _ffn_bwd(
            dxout, f, xin, up, u, row(w["ffn_norm_pre"], i), row(w["ffn_norm_post"], i) + tok, w["ffn_w_down_t", i],
            w["ffn_w_up_t", i], w["ffn_conv_w"][i], f"ffn{i}_bwd")
        tok = point(f"ffn{i}_bwd_done", dxin)
        g["ffn_w_down", i] = _mm_tn(act, df, f"ffn{i}_down_dw")
        g["ffn_w_up", i] = _mm_tn(dup, h, f"ffn{i}_up_dw")
        return dxin, tok, dict(ffn_norm_post=dg_post, ffn_norm_pre=dg_pre, ffn_conv_w=d_cw)

    h0, z0, c0, a0, qkv = _ev_in_fwd(x, row(w["mix_norm_pre"], 0), w["ev_w_in"], tab, w["ev_a_conv_w"],
                                     w["ev_a_conv_b"], w["ev_a_ln_g"], w["ev_a_ln_b"], "ev_in")
    cat = _att_fwd(qkv, a0, w["ev_sinks"], "ev_att")
    m0, x1 = _mm_post(cat, w["ev_w_out"], row(w["mix_norm_post"], 0), x, "ev_out")
    if fetch is not None:
        w = {**w, **fetch("ffn0", x1)}
    (x2,), ffn0 = ffn_fwd(x1, 0)
    if fetch is not None:
        w = {**w, **fetch("layer1", x2)}
    h2, z1, y1, m1, x3 = _od_fwd(x2, row(w["mix_norm_pre"], 1), w["od_w_in"], w["od_conv_w"], w["od_w_out"],
                                 row(w["mix_norm_post"], 1), "od_fwd")
    (dx4, sq), ffn1 = ffn_fwd(x3, 1, target)

    dx3, _, gf1 = ffn_bwd(dx4, ffn1, 1, 0.0)
    dm1, dz1, g["od_conv_w"], dx2, dg_mo1, dg_mp1 = _od_bwd(
        dx3, m1, row(w["mix_norm_post"], 1), w["od_w_out_t"], z1, w["od_conv_w"], w["od_w_in_t"], x2,
        row(w["mix_norm_pre"], 1), "od_bwd")
    g["od_w_out"] = _mm_tn(y1, dm1, "od_out_dw")
    g["od_w_in"] = _mm_tn(dz1, h2, "od_in_dw")
    tok = point("layer1_grads", dx2)

    dx1, tok, gf0 = ffn_bwd(dx2, ffn0, 0, tok)
    tok = tok + point("ffn0_grads", dx1)
    dm0, dcat, dg_mo0 = _mm_post_bwd(dx1, m0, row(w["mix_norm_post"], 0) + tok, w["ev_w_out_t"], "ev_out_bwd")
    tok = point("ev_out_bwd_done", dcat)
    g["ev_w_out"] = _mm_tn(cat, dm0, "ev_out_dw")
    dq, dkv, dsk = _att_bwd(qkv, dcat, w["ev_sinks"] + tok, "ev_att_bwd")
    tok = point("ev_att_bwd_done", dq)
    dz0, g["ev_a_conv_w"], dvec, dx0, dg_mp0 = _ev_mid_bwd(
        dcat, c0, z0, dq, dkv, tab, w["ev_a_conv_w"], w["ev_a_ln_g"] + tok, w["ev_a_ln_b"], w["ev_w_in_t"], x,
        row(w["mix_norm_pre"], 0), dx1, "ev_in_bwd")
    point("ev_mid_bwd_done", dz0)
    g["ev_w_in"] = _mm_tn(dz0, h0, "ev_in_dw")

    g["ev_a_conv_b"] = dvec[0:1]
    g["ev_a_ln_g"] = dvec[1:2]
    g["ev_a_ln_b"] = dvec[2:3]
    g["ev_sinks"] = dsk[:, 0]
    g["mix_norm_pre"] = jnp.concatenate([dg_mp0, dg_mp1], axis=0)
    g["mix_norm_post"] = jnp.concatenate([dg_mo0, dg_mo1], axis=0)
    g["ffn_norm_pre"] = jnp.concatenate([gf0["ffn_norm_pre"], gf1["ffn_norm_pre"]], axis=0)
    g["ffn_norm_post"] = jnp.concatenate([gf0["ffn_norm_post"], gf1["ffn_norm_post"]], axis=0)
    g["ffn_conv_w"] = jnp.stack([gf0["ffn_conv_w"], gf1["ffn_conv_w"]], axis=0)
    return sq, dx0, g


ANY = pl.BlockSpec(memory_space=pl.ANY)
PACK_COLS = 1024


def _me():
    return lax.axis_index("x"), lax.axis_index("y"), lax.axis_index("c")


def _other_chips(x, y):
    return [(1 - x, y), (x, 1 - y), (1 - x, 1 - y)]


def _remote(src, dst, send, recv, dev):
    return pltpu.make_async_remote_copy(src_ref=src, dst_ref=dst, send_sem=send, recv_sem=recv,
                                        device_id=dev, device_id_type=MESH)


def _gather_chips(wp, name):
    r, cols = wp.shape
    rh = r // 2

    def body(w_ref, o_ref, send, recv):
        x, y, c = _me()
        p = 2 * x + y
        sib = (x, y, 1 - c)
        chips = _other_chips(x, y)
        half = pl.ds(c * rh, rh)
        other = pl.ds((1 - c) * rh, rh)
        sent = [_remote(w_ref.at[half], o_ref.at[p, half], send.at[k], recv.at[k], (cx, cy, c))
                for k, (cx, cy) in enumerate(chips)]
        for cp in sent:
            cp.start()
        for k, (cx, cy) in enumerate(chips):
            q = 2 * cx + cy
            _remote(w_ref.at[half], o_ref.at[q, half], send.at[k], recv.at[k], (cx, cy, c)).wait_recv()
            fwd = _remote(o_ref.at[q, half], o_ref.at[q, half], send.at[3 + k], recv.at[3 + k], sib)
            fwd.start()
            sent.append(fwd)
        for k, (cx, cy) in enumerate(chips):
            q = 2 * cx + cy
            _remote(o_ref.at[q, other], o_ref.at[q, other], send.at[3 + k], recv.at[3 + k], sib).wait_recv()
        for cp in sent:
            cp.wait_send()

    return pl.pallas_call(
        body, name=name, in_specs=[ANY], out_specs=ANY,
        out_shape=jax.ShapeDtypeStruct((N_CHIPS, r, cols), wp.dtype),
        scratch_shapes=[pltpu.SemaphoreType.DMA((6,)), pltpu.SemaphoreType.DMA((6,))],
    )(wp)


HBM_SPEC = pl.BlockSpec(memory_space=pltpu.HBM)
SEM_SPEC = pl.BlockSpec(memory_space=pltpu.SEMAPHORE)
DATAFLOW = pltpu.SideEffectType.DATAFLOW_SIDE_EFFECTING


def _gather_plan(w_ref, land_ref):
    x, y, c = _me()
    return [(w_ref, land_ref.at[2 * x + y], (cx, cy, c)) for cx, cy in _other_chips(x, y)]


def _copies_start(src, land_shape, plan, n, name):
    def body(src_ref, land_ref, send, recv, src_thru, land_thru, token):
        for k, (s_view, d_view, dev) in enumerate(plan(src_ref, land_ref)):
            _remote(s_view, d_view, send.at[k], recv.at[k], dev).start()
        token[...] = jnp.zeros_like(token)

    return pl.pallas_call(
        body, name=name,
        out_shape=(pltpu.SemaphoreType.DMA((n,)), pltpu.SemaphoreType.DMA((n,)), pltpu.HBM(src.shape, src.dtype),
                   pltpu.HBM(land_shape, src.dtype), jax.ShapeDtypeStruct((8, LANES), F32)),
        in_specs=(HBM_SPEC, HBM_SPEC),
        out_specs=(SEM_SPEC, SEM_SPEC, HBM_SPEC, HBM_SPEC, pl.BlockSpec(memory_space=pltpu.VMEM)),
        input_output_aliases={0: 2, 1: 3},
        compiler_params=pltpu.CompilerParams(has_side_effects=DATAFLOW),
    )(pltpu.with_memory_space_constraint(src, pltpu.HBM),
      pltpu.with_memory_space_constraint(lax.empty(land_shape, src.dtype), pltpu.HBM))


def _copies_wait(started, after, plan, name):
    send, recv, src_thru, land_thru, _ = started

    def body(src_ref, land_ref, send, recv, after_ref, src_dead, land_out):
        for k, (s_view, d_view, dev) in enumerate(plan(src_ref, land_ref)):
            cp = _remote(s_view, d_view, send.at[k], recv.at[k], dev)
            cp.wait_send()
            cp.wait_recv()

    return pl.pallas_call(
        body, name=name,
        out_shape=(pltpu.HBM(src_thru.shape, src_thru.dtype), pltpu.HBM(land_thru.shape, land_thru.dtype)),
        in_specs=(HBM_SPEC, HBM_SPEC, SEM_SPEC, SEM_SPEC, ANY),
        out_specs=(HBM_SPEC, HBM_SPEC),
        input_output_aliases={0: 0, 1: 1},
        compiler_params=pltpu.CompilerParams(has_side_effects=DATAFLOW),
    )(src_thru, land_thru, send, recv, after)


def _swap_plan(g_ref, land_ref):
    x, y, c = _me()
    return [(g_ref.at[q, 1 - c], land_ref.at[q], (x, y, 1 - c)) for q in range(N_CHIPS)]


def _ici_plan(a_ref, land_ref):
    x, y, c = _me()
    return [(a_ref.at[2 * cx + cy], land_ref.at[2 * x + y], (cx, cy, c)) for cx, cy in _other_chips(x, y)]


def _share_plan(h_ref, land_ref):
    x, y, c = _me()
    return [(h_ref, land_ref, (x, y, 1 - c))]


def _exchange8(v, reduce, name):
    r, cols = v.shape
    rel = [(a, b, d) for a in (0, 1) for b in (0, 1) for d in (0, 1) if (a, b, d) != (0, 0, 0)]

    def body(v_ref, o_ref, *rest):
        if reduce:
            gbuf, send, recv = rest
        else:
            gbuf = o_ref
            send, recv = rest
        x, y, c = _me()
        me = 4 * x + 2 * y + c
        gbuf[me] = v_ref[...]
        sent = []
        for k, (a, b, d) in enumerate(rel):
            cp = _remote(v_ref, gbuf.at[me], send.at[k], recv.at[k], ((x + a) % 2, (y + b) % 2, (c + d) % 2))
            cp.start()
            sent.append(cp)
        for k, (a, b, d) in enumerate(rel):
            src = 4 * ((x + a) % 2) + 2 * ((y + b) % 2) + (c + d) % 2
            _remote(v_ref, gbuf.at[src], send.at[k], recv.at[k], (x, y, c)).wait_recv()
        for cp in sent:
            cp.wait_send()
        if reduce:
            acc = gbuf[0]
            for n in range(1, 8):
                acc = acc + gbuf[n]
            o_ref[...] = acc

    vmem = pl.BlockSpec(memory_space=pltpu.VMEM)
    sems = [pltpu.SemaphoreType.DMA((7,)), pltpu.SemaphoreType.DMA((7,))]
    if reduce:
        out_shape = jax.ShapeDtypeStruct((r, cols), F32)
        scratch = [pltpu.VMEM((8, r, cols), F32)] + sems
    else:
        out_shape = jax.ShapeDtypeStruct((8, r, cols), F32)
        scratch = sems
    return pl.pallas_call(body, name=name, in_specs=[vmem], out_specs=vmem, out_shape=out_shape,
                          scratch_shapes=scratch)(v)


def _row_tile(rows, pref, mult=8):
    if rows <= pref:
        return rows
    t = (pref // mult) * mult
    while t >= mult:
        if rows % t == 0:
            return t
        t -= mult
    return rows


def _rs_add(g, sib, c, name):
    _, _, rh, cols = g.shape
    tr = _row_tile(rh, 512, 16)

    def body(c_ref, g_ref, s_ref, o_ref):
        o_ref[...] = (g_ref[...] + s_ref[...]).astype(BF16)

    grid_spec = pltpu.PrefetchScalarGridSpec(
        num_scalar_prefetch=1, grid=(N_CHIPS, rh // tr),
        in_specs=[pl.BlockSpec((None, None, tr, cols), lambda q, i, cr: (q, cr[0], i, 0)),
                  pl.BlockSpec((None, tr, cols), lambda q, i, cr: (q, i, 0))],
        out_specs=pl.BlockSpec((None, tr, cols), lambda q, i, cr: (q, i, 0)),
    )
    return pl.pallas_call(
        body, name=name, grid_spec=grid_spec,
        out_shape=jax.ShapeDtypeStruct((N_CHIPS, rh, cols), BF16),
        compiler_params=_cparams(("parallel", "parallel")),
    )(c, g, sib)


def _rs_sum(rb, a, chip, name):
    _, rh, cols = rb.shape
    tr = _row_tile(rh, 512, 16)

    def body(p_ref, r0, r1, r2, r3, own, o_ref):
        p = p_ref[0]
        ownv = own[...].astype(F32)
        acc = None
        for q, r in enumerate((r0, r1, r2, r3)):
            v = jnp.where(p == q, ownv, r[...].astype(F32))
            acc = v if acc is None else acc + v
        o_ref[...] = acc

    def spec(q):
        return pl.BlockSpec((None, tr, cols), lambda i, pr: (jnp.where(pr[0] == q, (q + 1) % N_CHIPS, q), i, 0))

    grid_spec = pltpu.PrefetchScalarGridSpec(
        num_scalar_prefetch=1, grid=(rh // tr,),
        in_specs=[spec(0), spec(1), spec(2), spec(3), pl.BlockSpec((None, tr, cols), lambda i, pr: (pr[0], i, 0))],
        out_specs=pl.BlockSpec((tr, cols), lambda i, pr: (i, 0)),
    )
    return pl.pallas_call(
        body, name=name, grid_spec=grid_spec,
        out_shape=jax.ShapeDtypeStruct((rh, cols), F32),
        compiler_params=_cparams(("parallel",)),
    )(chip, rb, rb, rb, rb, a)


def _adamw(w, g, m, v, name):
    rows, cols = w.shape
    tr = _row_tile(rows, 512)

    def body(w_ref, g_ref, m_ref, v_ref, d_ref, nm_ref, nv_ref):
        gv = g_ref[...]
        nm = ADAM_B1 * m_ref[...] + (1.0 - ADAM_B1) * gv
        nv = ADAM_B2 * v_ref[...] + (1.0 - ADAM_B2) * (gv * gv)
        m_hat = nm / (1.0 - ADAM_B1 ** ADAM_STEP)
        v_hat = nv / (1.0 - ADAM_B2 ** ADAM_STEP)
        d_ref[...] = -ADAM_LR * (m_hat / (jnp.sqrt(v_hat) + ADAM_EPS) + ADAM_WD * w_ref[...])
        nm_ref[...] = nm
        nv_ref[...] = nv

    spec = pl.BlockSpec((tr, cols), lambda i: (i, 0))
    shp = jax.ShapeDtypeStruct((rows, cols), F32)
    return pl.pallas_call(
        body, name=name, grid=(rows // tr,), in_specs=[spec] * 4, out_specs=[spec] * 3, out_shape=[shp] * 3,
        compiler_params=_cparams(("parallel",)),
    )(w, g, m, v)


WEIGHTS = ("mix_norm_pre", "mix_norm_post", "ffn_norm_pre", "ffn_norm_post", "ev_w_in", "ev_a_conv_w", "ev_a_conv_b",
           "ev_a_ln_g", "ev_a_ln_b", "ev_sinks", "ev_w_out", "od_w_in", "od_conv_w", "od_w_out", "ffn_w_up",
           "ffn_conv_w", "ffn_w_down")
MATS = (("ev_w_in", 2), ("ev_w_out", 1), ("od_w_in", 2), ("od_w_out", 1), ("ffn_w_up", 2), ("ffn_w_down", 1))
UNITS = (("ev_w_in", 0, 2), ("ev_w_out", 0, 1), ("ffn_w_up", 0, 2), ("ffn_w_down", 0, 1),
         ("od_w_in", 0, 2), ("od_w_out", 0, 1), ("ffn_w_up", 1, 2), ("ffn_w_down", 1, 1))
GATHER_GROUPS = ((0, 1), (2, 3), (4, 5, 6, 7))
REDUCE_GROUPS = {"layer1": (4, 5, 6, 7), "ffn0": (2, 3), "ev": (0, 1)}
SMALL_SHARDED = ("ev_a_conv_w", "od_conv_w", "ffn_conv_w")
REPLICATED = ("mix_norm_pre", "mix_norm_post", "ffn_norm_pre", "ffn_norm_post", "ev_a_conv_b", "ev_a_ln_g",
              "ev_a_ln_b", "ev_sinks")


def _pack(parts, rows_multiple):
    flat = jnp.concatenate([p.reshape(-1) for p in parts])
    unit = rows_multiple * PACK_COLS
    pad = (-flat.shape[0]) % unit
    if pad:
        flat = jnp.concatenate([flat, jnp.zeros((pad,), flat.dtype)])
    return flat.reshape(-1, PACK_COLS)


def _unpack(buf, shapes):
    flat = buf.reshape(-1)
    out, off = [], 0
    for shp in shapes:
        n = 1
        for d in shp:
            n *= d
        out.append(flat[off:off + n].reshape(shp))
        off += n
    return out


def _shard_rows(shard, axis):
    if axis == 2:
        shard = jnp.swapaxes(shard, 1, 2)
    return shard.reshape(-1, PACK_COLS)


def kernel(x, positions, mix_norm_pre, mix_norm_post, ffn_norm_pre, ffn_norm_post, ev_w_in, ev_a_conv_w, ev_a_conv_b, ev_a_ln_g, ev_a_ln_b, ev_sinks, ev_w_out, od_w_in, od_conv_w, od_w_out, ffn_w_up, ffn_conv_w, ffn_w_down, loss_target, m_mix_norm_pre, m_mix_norm_post, m_ffn_norm_pre, m_ffn_norm_post, m_ev_w_in, m_ev_a_conv_w, m_ev_a_conv_b, m_ev_a_ln_g, m_ev_a_ln_b, m_ev_sinks, m_ev_w_out, m_od_w_in, m_od_conv_w, m_od_w_out, m_ffn_w_up, m_ffn_conv_w, m_ffn_w_down, v_mix_norm_pre, v_mix_norm_post, v_ffn_norm_pre, v_ffn_norm_post, v_ev_w_in, v_ev_a_conv_w, v_ev_a_conv_b, v_ev_a_ln_g, v_ev_a_ln_b, v_ev_sinks, v_ev_w_out, v_od_w_in, v_od_conv_w, v_od_w_out, v_ffn_w_up, v_ffn_conv_w, v_ffn_w_down):
    wts = dict(zip(WEIGHTS, (mix_norm_pre, mix_norm_post, ffn_norm_pre, ffn_norm_post, ev_w_in, ev_a_conv_w, ev_a_conv_b,
                             ev_a_ln_g, ev_a_ln_b, ev_sinks, ev_w_out, od_w_in, od_conv_w, od_w_out, ffn_w_up, ffn_conv_w,
                             ffn_w_down)))
    mom = dict(zip(WEIGHTS, (m_mix_norm_pre, m_mix_norm_post, m_ffn_norm_pre, m_ffn_norm_post, m_ev_w_in, m_ev_a_conv_w,
                             m_ev_a_conv_b, m_ev_a_ln_g, m_ev_a_ln_b, m_ev_sinks, m_ev_w_out, m_od_w_in, m_od_conv_w,
                             m_od_w_out, m_ffn_w_up, m_ffn_conv_w, m_ffn_w_down)))
    var = dict(zip(WEIGHTS, (v_mix_norm_pre, v_mix_norm_post, v_ffn_norm_pre, v_ffn_norm_post, v_ev_w_in, v_ev_a_conv_w,
                             v_ev_a_conv_b, v_ev_a_ln_g, v_ev_a_ln_b, v_ev_sinks, v_ev_w_out, v_od_w_in, v_od_conv_w,
                             v_od_w_out, v_ffn_w_up, v_ffn_conv_w, v_ffn_w_down)))
    xi, yi, ci = _me()
    chip = 2 * xi + yi

    unit_rows = [_shard_rows(wts[k][l:l + 1].astype(BF16), axis) for k, l, axis in UNITS]

    def group_block(group):
        return jnp.concatenate([unit_rows[u] for u in group], axis=0)

    def unpack_group(group, landed, own):
        full = lax.dynamic_update_slice(landed, own[None], (chip, 0, 0))
        out, off = {}, 0
        for u in group:
            k, l, axis = UNITS[u]
            n = unit_rows[u].shape[0]
            native = full[:, off:off + n].reshape(N_CHIPS * n, PACK_COLS)
            off += n
            key = (lambda name: (name, l)) if k.startswith("ffn") else (lambda name: name)
            out[key(k + "_t" if axis == 2 else k)] = native
            out[key(k if axis == 2 else k + "_t")] = native.T
        return out

    small_shapes = [wts[k].shape for k in SMALL_SHARDED]
    small_all = _exchange8(_pack([wts[k] for k in SMALL_SHARDED], 8), False, "gather_small")
    blocks = [group_block(grp) for grp in GATHER_GROUPS]
    first = _gather_chips(blocks[0], "gather_mats")
    later = {}
    for stage, grp, blk in zip(("ffn0", "layer1"), GATHER_GROUPS[1:], blocks[1:]):
        later[stage] = (grp, blk, _copies_start(blk, (N_CHIPS,) + blk.shape, _gather_plan, 3, "gather_" + stage + "_start"))

    def fetch(stage, after):
        grp, blk, started = later[stage]
        own, landed = _copies_wait(started, after, _gather_plan, "gather_" + stage + "_wait")
        return unpack_group(grp, landed, own)

    w = {k: wts[k] for k in REPLICATED}
    w.update(unpack_group(GATHER_GROUPS[0], first, blocks[0]))
    per_chip = [_unpack(small_all[2 * q], small_shapes) for q in range(N_CHIPS)]
    for n, k in enumerate(SMALL_SHARDED):
        w[k] = jnp.concatenate([per_chip[q][n] for q in range(N_CHIPS)], axis=-1)
    for k in ("ev_a_conv_w", "od_conv_w"):
        w[k] = w[k][0]
    w["ev_sinks"] = w["ev_sinks"][0]
    w["mix_norm_pre"] = w["mix_norm_pre"] + sum(later[s][2][4][0, 0] for s in later)

    core = jnp.reshape(ci, (1,)).astype(jnp.int32)
    chip_arr = jnp.reshape(chip, (1,)).astype(jnp.int32)
    per_layer = {}

    def group_grads(group, g):
        gp = jnp.concatenate([(g[k, l] if k.startswith("ffn") else g[k]).reshape(N_CHIPS, -1, PACK_COLS)
                              for k, l, _ in (UNITS[u] for u in group)], axis=1)
        return gp.reshape(N_CHIPS, 2, gp.shape[1] // 2, PACK_COLS)

    def finish(group, half, other):
        red = jnp.concatenate([jnp.where(ci == 0, half, other), jnp.where(ci == 0, other, half)], axis=0)
        off = 0
        for u in group:
            k, l, axis = UNITS[u]
            n = unit_rows[u].shape[0]
            part = red[off:off + n]
            off += n
            per_layer[k, l] = part.T if axis == 2 else part

    chains = {}

    def chain_step(tag, after, g):
        group = REDUCE_GROUPS[tag]
        st = chains.setdefault(tag, {"step": 0})
        step = st["step"]
        st["step"] = step + 1
        if step == 0:
            gp = group_grads(group, g)
            rh = gp.shape[2]
            st["swap"] = _copies_start(gp, (N_CHIPS, rh, PACK_COLS), _swap_plan, N_CHIPS, f"rs_{tag}_swap_start")
            return st["swap"][4][0, 0]
        if step == 1:
            gp, sib = _copies_wait(st["swap"], after, _swap_plan, f"rs_{tag}_swap_wait")
            pair = _rs_add(gp, sib, core, f"rs_{tag}_add")
            st["ici"] = _copies_start(pair, pair.shape, _ici_plan, 3, f"rs_{tag}_ici_start")
            return st["ici"][4][0, 0]
        if step == 2:
            pair, landed = _copies_wait(st["ici"], after, _ici_plan, f"rs_{tag}_ici_wait")
            half = _rs_sum(landed, pair, chip_arr, f"rs_{tag}_sum")
            st["share"] = _copies_start(half, half.shape, _share_plan, 1, f"rs_{tag}_share_start")
            return st["share"][4][0, 0]
        half, other = _copies_wait(st["share"], after, _share_plan, f"rs_{tag}_share_wait")
        finish(group, half, other)
        return 0.0

    schedule = {"layer1_grads": ("layer1",), "ffn0_bwd_done": ("layer1",), "ffn0_grads": ("ffn0",),
                "ev_out_bwd_done": ("layer1", "ffn0"), "ev_att_bwd_done": ("layer1", "ffn0"),
                "ev_mid_bwd_done": ("ffn0",)}

    def emit(place, after, g):
        return sum(chain_step(tag, after, g) for tag in schedule.get(place, ()))

    sq, dx, g = _local_step(x[0], positions[0], loss_target[0], w, fetch, emit)
    loss = lax.psum(0.5 * jnp.sum(sq) / D_MODEL, ("x", "y", "c"))

    grads, deltas, new_m, new_v = {}, {}, {}, {}

    def adamw(k):
        shp = wts[k].shape
        two_d = (-1, shp[-1])
        d, nm, nv = _adamw(wts[k].reshape(two_d), grads[k].reshape(two_d), mom[k].reshape(two_d), var[k].reshape(two_d),
                           "adamw_" + k)
        deltas[k], new_m[k], new_v[k] = d.reshape(shp), nm.reshape(shp), nv.reshape(shp)
        return nv

    def stacked(k):
        return jnp.stack([per_layer[k, l] for l in range(wts[k].shape[0])], axis=0)

    chain_step("ev", None, g)

    small_keys = REPLICATED + SMALL_SHARDED
    full_shapes = [wts[k].shape for k in REPLICATED] + [wts[k].shape[:-1] + (wts[k].shape[-1] * N_CHIPS,) for k in SMALL_SHARDED]
    sm = _exchange8(_pack([g[k] for k in small_keys], 8), True, "reduce_small")
    for k, full in zip(small_keys, _unpack(sm, full_shapes)):
        if k in SMALL_SHARDED:
            n = wts[k].shape[-1]
            full = lax.dynamic_slice_in_dim(full, chip * n, n, axis=full.ndim - 1)
        grads[k] = full
    chain_step("ev", sm, g)

    for k in ("od_w_in", "od_w_out", "ffn_w_up"):
        grads[k] = stacked(k)
        last = adamw(k)
    chain_step("ev", last, g)
    grads["ffn_w_down"] = stacked("ffn_w_down")
    last = adamw("ffn_w_down")
    for k in small_keys:
        last = adamw(k)
    chain_step("ev", last, g)
    for k in ("ev_w_in", "ev_w_out"):
        grads[k] = stacked(k)
        adamw(k)

    return (loss, dx[None], *[grads[k] for k in WEIGHTS], *[deltas[k] for k in WEIGHTS],
            *[new_m[k] for k in WEIGHTS], *[new_v[k] for k in WEIGHTS])
```
